```python
import math
import jax, jax.numpy as jnp
from jax import lax
import numpy as np

D_MODEL = 1024
BATCH = 2
SEQ = 8192
DEPTH = 1

GRID_W = 64
CTX_LEN = 256
RET_HEADS = 4
RET_DK = D_MODEL // RET_HEADS
RET_DV = 2 * D_MODEL // RET_HEADS
RET_CHUNK = 128
DIFF_DH = 64
DIFF_HEADS = D_MODEL // (2 * DIFF_DH)
DIFF_DV = 2 * DIFF_DH
Q_BLOCK = 128
N_EXPERTS = 32
TOP_K = 4
D_FF = D_MODEL
SWIGLU_LIMIT = 7.0
SWIGLU_ALPHA = 1.702
MOE_BLOCK = 128
ROPE_BASE = 10000.0
EPS = 1e-6

RET_QK_W = RET_HEADS * RET_DK
RET_V_W = RET_HEADS * RET_DV
DIFF_QK_W = DIFF_HEADS * 2 * DIFF_DH
DIFF_V_W = DIFF_HEADS * DIFF_DV
IN_SPLITS = (RET_QK_W, RET_QK_W, RET_V_W, RET_V_W, DIFF_QK_W, DIFF_QK_W, DIFF_V_W, D_MODEL, D_MODEL)
IN_COLS = sum(IN_SPLITS)

kernel_name = "hybrid_retention_diffattn_moe_dit_block"


def _rmsnorm(x, g):
    xf = x.astype(jnp.float32)
    y = xf * lax.rsqrt(jnp.mean(xf * xf, axis=-1, keepdims=True) + EPS)
    return (y * g.astype(jnp.float32)).astype(x.dtype)


def _modulate(h, shift, scale):
    return h * (1.0 + scale) + shift


def _rope_axis(x, pos):
    half = x.shape[-1] // 2
    inv = ROPE_BASE ** (-jnp.arange(half, dtype=jnp.float32) / half)
    ang = pos.astype(jnp.float32)[:, None] * inv[None, :]
    cos, sin = jnp.cos(ang), jnp.sin(ang)
    x1 = x[..., :half].astype(jnp.float32)
    x2 = x[..., half:].astype(jnp.float32)
    return jnp.concatenate([x1 * cos - x2 * sin, x1 * sin + x2 * cos], axis=-1).astype(x.dtype)


def _rope_2d(x, row, col):
    d2 = x.shape[-1] // 2
    return jnp.concatenate([_rope_axis(x[..., :d2], row), _rope_axis(x[..., d2:], col)], axis=-1)


def _project(h, w_in):
    p = h @ w_in
    return jnp.split(p, np.cumsum(IN_SPLITS)[:-1].tolist(), axis=-1)


def _ret_heads(t, d):
    b, l, _ = t.shape
    return t.reshape(b, l, RET_HEADS, d).transpose(0, 2, 1, 3).astype(jnp.float32)


def _retention_scan(q, k, v, log_gamma, s0):
    b, h, l, _ = q.shape
    dv = v.shape[-1]
    n = l // RET_CHUNK

    def chunks(t):
        return t.reshape(b, h, n, RET_CHUNK, t.shape[-1]).transpose(2, 0, 1, 3, 4)

    j = jnp.arange(RET_CHUNK, dtype=jnp.float32)
    diff = j[:, None] - j[None, :]
    lg = log_gamma.astype(jnp.float32)
    decay_in = jnp.where(diff >= 0, jnp.exp(lg[:, None, None] * jnp.maximum(diff, 0.0)), 0.0)
    q_decay = jnp.exp(lg[:, None] * (j + 1.0))[:, :, None]
    k_decay = jnp.exp(lg[:, None] * (RET_CHUNK - 1.0 - j))[:, :, None]
    chunk_decay = jnp.exp(lg * RET_CHUNK)[:, None, None]

    def step(s, qkv):
        qc, kc, vc = qkv
        inner = jnp.einsum('bhid,bhjd->bhij', qc, kc) * decay_in
        o = jnp.einsum('bhij,bhjv->bhiv', inner, vc) + jnp.einsum('bhid,bhdv->bhiv', qc * q_decay, s)
        s = chunk_decay * s + jnp.einsum('bhjd,bhjv->bhdv', kc * k_decay, vc)
        return s, o

    s_fin, o = lax.scan(step, s0, (chunks(q), chunks(k), chunks(v)))
    return o.transpose(1, 2, 0, 3, 4).reshape(b, h, l, dv), s_fin


def _retention_bidir(q, k, v, log_g_fwd, log_g_bwd, s0_fwd, s0_bwd):
    o_f, s_f = _retention_scan(q, k, v, log_g_fwd, s0_fwd)
    flip = lambda t: t[:, :, ::-1]
    o_b, s_b = _retention_scan(flip(q), flip(k), flip(v), log_g_bwd, s0_bwd)
    return o_f + flip(o_b), s_f, s_b


def _diff_qk(t, g, row, col):
    b, l, _ = t.shape
    t = _rmsnorm(t.reshape(b, l, DIFF_HEADS, 2, DIFF_DH), g).transpose(0, 2, 3, 1, 4)
    if row is not None:
        t = _rope_2d(t, row, col)
    return t[:, :, 0], t[:, :, 1]


def _diff_attend(q1, q2, k1, k2, v, lam):
    scale = DIFF_DH ** -0.5
    s1 = jnp.einsum('bhqd,bhkd->bhqk', q1, k1).astype(jnp.float32) * scale
    s2 = jnp.einsum('bhqd,bhkd->bhqk', q2, k2).astype(jnp.float32) * scale
    a = jax.nn.softmax(s1, axis=-1) - lam * jax.nn.softmax(s2, axis=-1)
    return jnp.einsum('bhqk,bhkv->bhqv', a.astype(v.dtype), v)


def _diff_attend_blocks(q1, q2, k1, k2, v, lam):
    b, h, l, dh = q1.shape
    nb = l // Q_BLOCK
    blk = lambda t: t.reshape(b, h, nb, Q_BLOCK, dh).transpose(2, 0, 1, 3, 4)
    o = lax.map(lambda qq: _diff_attend(qq[0], qq[1], k1, k2, v, lam), (blk(q1), blk(q2)))
    return o.transpose(1, 2, 0, 3, 4).reshape(b, h, l, v.shape[-1])


def _moe(h, router_w, router_b, w1, b1, w2, b2):
    b, l, d = h.shape
    xt = h.reshape(-1, d)
    t = xt.shape[0]
    logits = (xt @ router_w + router_b).astype(jnp.float32)
    top_val, top_idx = lax.top_k(logits, TOP_K)
    gate = jax.nn.softmax(top_val, axis=-1)
    p = t * TOP_K
    expert_flat = top_idx.reshape(-1)
    order = jnp.argsort(expert_flat)
    sorted_e = expert_flat[order]
    counts = jnp.bincount(expert_flat, length=N_EXPERTS)
    padded = (counts + MOE_BLOCK - 1) // MOE_BLOCK * MOE_BLOCK
    pad_end = jnp.cumsum(padded)
    pad_start = pad_end - padded
    cnt_start = jnp.cumsum(counts) - counts
    dest = pad_start[sorted_e] + jnp.arange(p, dtype=jnp.int32) - cnt_start[sorted_e]
    n_blocks = -(-p // MOE_BLOCK) + N_EXPERTS
    rows = n_blocks * MOE_BLOCK
    row_tok = jnp.full((rows,), t, dtype=jnp.int32).at[dest].set((order // TOP_K).astype(jnp.int32))
    row_gate = jnp.zeros((rows,), jnp.float32).at[dest].set(gate.reshape(-1)[order])
    block_e = jnp.minimum(jnp.searchsorted(pad_end, jnp.arange(n_blocks) * MOE_BLOCK, side='right'),
                          N_EXPERTS - 1)
    x_pad = jnp.concatenate([xt, jnp.zeros((1, d), xt.dtype)], axis=0)
    xb = x_pad[row_tok].reshape(n_blocks, MOE_BLOCK, d)

    def expert_block(args):
        xblk, e = args
        hh = xblk @ w1[e] + b1[e]
        glu = jnp.minimum(hh[:, :D_FF], SWIGLU_LIMIT)
        lin = jnp.clip(hh[:, D_FF:], -SWIGLU_LIMIT, SWIGLU_LIMIT)
        act = glu * jax.nn.sigmoid(SWIGLU_ALPHA * glu) * (lin + 1.0)
        return act @ w2[e] + b2[e]

    yb = lax.map(expert_block, (xb, block_e)).reshape(rows, d)
    y = jax.ops.segment_sum(yb * row_gate[:, None].astype(yb.dtype), row_tok, num_segments=t + 1)[:t]
    return y.reshape(b, l, d)


def _layer(x, ctx, c, c_ctx, row, col, layer_idx, need_ctx_out,
           norm1_g, norm2_g, w_mod, b_mod, w_in, ret_decay_logit, ret_norm_g,
           diff_q_norm_g, diff_k_norm_g, diff_lambda, diff_norm_g,
           w_br_ret, w_br_diff, w_out, router_w, router_b, exp_w1, exp_b1, exp_w2, exp_b2):
    b, l, d = x.shape
    mod_l = (jax.nn.silu(c) @ w_mod + b_mod)[:, None, :]
    mod_c = (jax.nn.silu(c_ctx) @ w_mod + b_mod)[None, None, :]
    sh_a, sc_a, g_a, sh_f, sc_f, g_f = jnp.split(mod_l, 6, axis=-1)
    csh_a, csc_a, cg_a, csh_f, csc_f, cg_f = jnp.split(mod_c, 6, axis=-1)

    hl = _modulate(_rmsnorm(x, norm1_g), sh_a, sc_a)
    hc = _modulate(_rmsnorm(ctx, norm1_g), csh_a, csc_a)
    rq, rk, rv, rg, dq, dk, dv, ga, gb = _project(hl, w_in)
    crq, crk, crv, crg, cdq, cdk, cdv, cga, cgb = _project(hc, w_in)

    log_g = jax.nn.log_sigmoid(ret_decay_logit.astype(jnp.float32))
    qs = RET_DK ** -0.5
    lq = _rope_2d(_ret_heads(rq, RET_DK) * qs, row, col)
    lk = _rope_2d(_ret_heads(rk, RET_DK), row, col)
    lv = _ret_heads(rv, RET_DV)
    cq = _ret_heads(crq, RET_DK) * qs
    ck = _ret_heads(crk, RET_DK)
    cv = _ret_heads(crv, RET_DV)
    s0 = jnp.zeros((b, RET_HEADS, RET_DK, RET_DV), jnp.float32)
    ret_c, s_f, s_b = _retention_bidir(cq, ck, cv, log_g[0], log_g[1], s0, s0)
    ret_l, _, _ = _retention_bidir(lq, lk, lv, log_g[0], log_g[1], s_f, s_b)

    def ret_branch(o, gate):
        y = _rmsnorm(o, ret_norm_g[:, None, :])
        y = y.transpose(0, 2, 1, 3).reshape(o.shape[0], o.shape[2], RET_V_W).astype(gate.dtype)
        return (y * jax.nn.silu(gate)) @ w_br_ret

    lambda_init = 0.8 - 0.6 * math.exp(-0.3 * layer_idx)
    lp = diff_lambda.astype(jnp.float32)
    lam = jnp.exp(jnp.sum(lp[0] * lp[1])) - jnp.exp(jnp.sum(lp[2] * lp[3])) + lambda_init
    lq1, lq2 = _diff_qk(dq, diff_q_norm_g, row, col)
    lk1, lk2 = _diff_qk(dk, diff_k_norm_g, row, col)
    cq1, cq2 = _diff_qk(cdq, diff_q_norm_g, None, None)
    ck1, ck2 = _diff_qk(cdk, diff_k_norm_g, None, None)
    lvd = dv.reshape(b, l, DIFF_HEADS, DIFF_DV).transpose(0, 2, 1, 3)
    cvd = cdv.reshape(b, ctx.shape[1], DIFF_HEADS, DIFF_DV).transpose(0, 2, 1, 3)
    k1_all = jnp.concatenate([ck1, lk1], axis=2)
    k2_all = jnp.concatenate([ck2, lk2], axis=2)
    v_all = jnp.concatenate([cvd, lvd], axis=2)
    diff_l = _diff_attend_blocks(lq1, lq2, k1_all, k2_all, v_all, lam)

    def diff_branch(o):
        y = _rmsnorm(o, diff_norm_g) * (1.0 - lambda_init)
        y = y.transpose(0, 2, 1, 3).reshape(o.shape[0], o.shape[2], DIFF_V_W)
        return y @ w_br_diff

    def merge(yr, yd, gate_r, gate_d):
        return (jax.nn.sigmoid(gate_r) * yr + jax.nn.sigmoid(gate_d) * yd) @ w_out

    x_new = x + g_a * merge(ret_branch(ret_l, rg), diff_branch(diff_l), ga, gb)
    x_new = x_new + g_f * _moe(_modulate(_rmsnorm(x_new, norm2_g), sh_f, sc_f),
                               router_w, router_b, exp_w1, exp_b1, exp_w2, exp_b2)
    if need_ctx_out:
        diff_c = _diff_attend(cq1, cq2, ck1, ck2, cvd, lam)
        ctx = ctx + cg_a * merge(ret_branch(ret_c, crg), diff_branch(diff_c), cga, cgb)
        ctx = ctx + cg_f * _moe(_modulate(_rmsnorm(ctx, norm2_g), csh_f, csc_f),
                                router_w, router_b, exp_w1, exp_b1, exp_w2, exp_b2)
    return x_new, ctx


def setup_inputs(seed: int = 0) -> dict:
    key = jax.random.key(seed)
    ks = jax.random.split(key, 32)
    f32 = jnp.float32
    nrm = lambda k, shape, s: jax.random.normal(k, shape, f32) * s
    decay_base = jnp.asarray(np.log(2.0 ** (5 + np.arange(RET_HEADS)) - 1.0), f32)
    return {
        "x": nrm(ks[0], (BATCH, SEQ, D_MODEL), 1.0),
        "c": nrm(ks[1], (BATCH, D_MODEL), 1.0),
        "ctx": nrm(ks[2], (BATCH, CTX_LEN, D_MODEL), 1.0),
        "c_ctx": nrm(ks[3], (D_MODEL,), 1.0),
        "norm1_g": 1.0 + nrm(ks[4], (DEPTH, D_MODEL), 0.1),
        "norm2_g": 1.0 + nrm(ks[5], (DEPTH, D_MODEL), 0.1),
        "w_mod": nrm(ks[6], (DEPTH, D_MODEL, 6 * D_MODEL), 0.5 * D_MODEL ** -0.5),
        "b_mod": nrm(ks[7], (DEPTH, 6 * D_MODEL), 0.02),
        "w_in": nrm(ks[8], (DEPTH, D_MODEL, IN_COLS), D_MODEL ** -0.5),
        "ret_decay_logit": decay_base[None, None, :] + nrm(ks[9], (DEPTH, 2, RET_HEADS), 0.1),
        "ret_norm_g": 1.0 + nrm(ks[10], (DEPTH, RET_HEADS, RET_DV), 0.1),
        "diff_q_norm_g": 1.0 + nrm(ks[11], (DEPTH, DIFF_DH), 0.1),
        "diff_k_norm_g": 1.0 + nrm(ks[12], (DEPTH, DIFF_DH), 0.1),
        "diff_lambda": nrm(ks[13], (DEPTH, 4, DIFF_DH), 0.1),
        "diff_norm_g": 1.0 + nrm(ks[14], (DEPTH, DIFF_DV), 0.1),
        "w_br_ret": nrm(ks[15], (DEPTH, RET_V_W, D_MODEL), RET_V_W ** -0.5),
        "w_br_diff": nrm(ks[16], (DEPTH, DIFF_V_W, D_MODEL), DIFF_V_W ** -0.5),
        "w_out": nrm(ks[17], (DEPTH, D_MODEL, D_MODEL), D_MODEL ** -0.5),
        "router_w": nrm(ks[18], (DEPTH, D_MODEL, N_EXPERTS), D_MODEL ** -0.5),
        "router_b": nrm(ks[19], (DEPTH, N_EXPERTS), 0.01),
        "exp_w1": nrm(ks[20], (DEPTH, N_EXPERTS, D_MODEL, 2 * D_FF), D_MODEL ** -0.5),
        "exp_b1": nrm(ks[21], (DEPTH, N_EXPERTS, 2 * D_FF), 0.02),
        "exp_w2": nrm(ks[22], (DEPTH, N_EXPERTS, D_FF, D_MODEL), D_FF ** -0.5),
        "exp_b2": nrm(ks[23], (DEPTH, N_EXPERTS, D_MODEL), 0.02),
    }


def reference(x, c, ctx, c_ctx, norm1_g, norm2_g, w_mod, b_mod, w_in, ret_decay_logit, ret_norm_g,
              diff_q_norm_g, diff_k_norm_g, diff_lambda, diff_norm_g, w_br_ret, w_br_diff, w_out,
              router_w, router_b, exp_w1, exp_b1, exp_w2, exp_b2):
    n_tok = x.shape[1]
    ROWS = n_tok // GRID_W
    row = jnp.repeat(jnp.arange(ROWS, dtype=jnp.int32), GRID_W)
    col = jnp.tile(jnp.arange(GRID_W, dtype=jnp.int32), ROWS)
    for i in range(DEPTH):
        x, ctx = _layer(x, ctx, c, c_ctx, row, col, i, i < DEPTH - 1,
                        norm1_g[i], norm2_g[i], w_mod[i], b_mod[i], w_in[i], ret_decay_logit[i], ret_norm_g[i],
                        diff_q_norm_g[i], diff_k_norm_g[i], diff_lambda[i], diff_norm_g[i],
                        w_br_ret[i], w_br_diff[i], w_out[i], router_w[i], router_b[i],
                        exp_w1[i], exp_b1[i], exp_w2[i], exp_b2[i])
    return x
```

```python
import functools
import math

import jax
import jax.numpy as jnp
from jax import lax
from jax.experimental import pallas as pl
from jax.experimental.pallas import tpu as pltpu

F32 = jnp.float32
BF16 = jnp.bfloat16
U32 = jnp.uint32
I32 = jnp.int32

D_MODEL = 1024
GRID_W = 64
RET_HEADS = 4
RET_DK = 256
RET_DV = 512
DIFF_DH = 64
DIFF_HEADS = 8
DIFF_DV = 128
N_EXPERTS = 32
TOP_K = 4
D_FF = 1024
SWIGLU_LIMIT = 7.0
SWIGLU_ALPHA = 1.702
ROPE_BASE = 10000.0
EPS = 1e-6
LAMBDA_INIT = 0.8 - 0.6 * math.exp(-0.3 * 0)

IN_COLS = 11264
COL_TILE = 1024
LANES = 128
MOE_BLK = 256
NEG_BIG = -1e30
HIGHEST = lax.Precision.HIGHEST
MIB = 1024 * 1024


def _params(sem, vmem_mib):
    return pltpu.CompilerParams(dimension_semantics=sem, vmem_limit_bytes=vmem_mib * MIB)


def _mod_kernel(c_ref, w_ref, b_ref, o_ref):
    c = c_ref[...]
    s = c * jax.nn.sigmoid(c)
    o_ref[...] = jnp.dot(s, w_ref[...], preferred_element_type=F32, precision=HIGHEST) + b_ref[...]


def _mod(cc, w_mod, b_mod):
    n = w_mod.shape[1]
    tn = 1024
    return pl.pallas_call(
        _mod_kernel,
        out_shape=jax.ShapeDtypeStruct((8, n), F32),
        grid=(n // tn,),
        in_specs=[pl.BlockSpec((8, D_MODEL), lambda j: (0, 0)),
                  pl.BlockSpec((D_MODEL, tn), lambda j: (0, j)),
                  pl.BlockSpec((1, tn), lambda j: (0, j))],
        out_specs=pl.BlockSpec((8, tn), lambda j: (0, j)),
        compiler_params=_params(("arbitrary",), 32),
        name="mod",
    )(cc, w_mod, b_mod.reshape(1, n))


def _inproj_kernel(x_ref, g_ref, sh_ref, sc_ref, w_ref, qkg_ref, gmat_ref,
                   cr_ref, sr_ref, cc_ref, sc2_ref, cd_ref, sa_ref, sb_ref,
                   o_ref, h_scr, acc_scr):
    j = pl.program_id(1)

    @pl.when(j == 0)
    def _():
        xf = x_ref[...]
        ms = jnp.mean(xf * xf, axis=-1, keepdims=True)
        y = xf * lax.rsqrt(ms + EPS) * g_ref[...]
        h_scr[...] = (y * (1.0 + sc_ref[0]) + sh_ref[0]).astype(BF16)

    acc_scr[...] = jnp.dot(h_scr[...], w_ref[...], preferred_element_type=F32)
    is_ret = j <= 1
    is_dqk = (j == 6) | (j == 7)

    @pl.when(is_ret)
    def _():
        scale = jnp.where(j == 0, RET_DK ** -0.5, 1.0).astype(F32)
        for b in range(COL_TILE // LANES):
            xb = acc_scr[:, b * LANES:(b + 1) * LANES]
            cos = cr_ref[...] if b % 2 == 0 else cc_ref[...]
            sin = sr_ref[...] if b % 2 == 0 else sc2_ref[...]
            o = (xb * cos + pltpu.roll(xb, 64, 1) * sin) * scale
            o_ref[:, b * LANES:(b + 1) * LANES] = o.astype(BF16)

    @pl.when(is_dqk)
    def _():
        g = jnp.where(j == 6, qkg_ref[0:1, :], qkg_ref[1:2, :])
        for b in range(COL_TILE // LANES):
            xb = acc_scr[:, b * LANES:(b + 1) * LANES]
            ss = jnp.dot((xb * xb).astype(BF16), gmat_ref[...], preferred_element_type=F32)
            yn = xb * lax.rsqrt(ss * (1.0 / DIFF_DH) + EPS) * g
            o = yn * cd_ref[...] + pltpu.roll(yn, 16, 1) * sa_ref[...] + pltpu.roll(yn, 112, 1) * sb_ref[...]
            o_ref[:, b * LANES:(b + 1) * LANES] = o.astype(BF16)

    @pl.when(jnp.logical_not(is_ret | is_dqk))
    def _():
        o_ref[...] = acc_scr[...].astype(BF16)


def _inproj(x2, g1, sh, sc, w_bf, qkg, gmat, tables, tm, tiles_per_batch):
    n = x2.shape[0]
    nb = sh.shape[0]
    tab_spec = pl.BlockSpec((tm, LANES), lambda i, j: (i % tiles_per_batch, 0))
    mod_spec = pl.BlockSpec((1, 1, D_MODEL), lambda i, j: (jnp.minimum(i // tiles_per_batch, nb - 1), 0, 0))
    return pl.pallas_call(
        _inproj_kernel,
        out_shape=jax.ShapeDtypeStruct((n, IN_COLS), BF16),
        grid=(n // tm, IN_COLS // COL_TILE),
        in_specs=[pl.BlockSpec((tm, D_MODEL), lambda i, j: (i, 0)),
                  pl.BlockSpec((1, D_MODEL), lambda i, j: (0, 0)),
                  mod_spec, mod_spec,
                  pl.BlockSpec((D_MODEL, COL_TILE), lambda i, j: (0, j)),
                  pl.BlockSpec((8, LANES), lambda i, j: (0, 0)),
                  pl.BlockSpec((LANES, LANES), lambda i, j: (0, 0))] + [tab_spec] * 7,
        out_specs=pl.BlockSpec((tm, COL_TILE), lambda i, j: (i, j)),
        scratch_shapes=[pltpu.VMEM((tm, D_MODEL), BF16), pltpu.VMEM((tm, COL_TILE), F32)],
        compiler_params=_params(("arbitrary", "arbitrary"), 48),
        name="inproj",
    )(x2, g1, sh, sc, w_bf, qkg, gmat, *tables)


def _rope_tables(seq):
    t = jnp.arange(seq, dtype=jnp.int32)
    row = (t // GRID_W).astype(F32)[:, None]
    col = (t % GRID_W).astype(F32)[:, None]
    inv64 = ROPE_BASE ** (-jnp.arange(64, dtype=F32) / 64)
    inv16 = ROPE_BASE ** (-jnp.arange(16, dtype=F32) / 16)
    ar, ac = row * inv64[None, :], col * inv64[None, :]
    cr = jnp.concatenate([jnp.cos(ar), jnp.cos(ar)], axis=1)
    sr = jnp.concatenate([-jnp.sin(ar), jnp.sin(ar)], axis=1)
    cc = jnp.concatenate([jnp.cos(ac), jnp.cos(ac)], axis=1)
    sc = jnp.concatenate([-jnp.sin(ac), jnp.sin(ac)], axis=1)
    br, bc = row * inv16[None, :], col * inv16[None, :]
    z = jnp.zeros_like(br)
    cd = jnp.tile(jnp.concatenate([jnp.cos(br), jnp.cos(br), jnp.cos(bc), jnp.cos(bc)], axis=1), (1, 2))
    sa = jnp.tile(jnp.concatenate([z, jnp.sin(br), z, jnp.sin(bc)], axis=1), (1, 2))
    sb = jnp.tile(jnp.concatenate([-jnp.sin(br), z, -jnp.sin(bc), z], axis=1), (1, 2))
    return [cr, sr, cc, sc, cd, sa, sb]


def _identity_tables(seq):
    one = jnp.ones((seq, LANES), F32)
    zero = jnp.zeros((seq, LANES), F32)
    return [one, zero, one, zero, one, zero, zero]


def _tn_dot(a, b):
    return lax.dot_general(a, b, (((0,), (0,)), ((), ())), preferred_element_type=F32)


def _nt_dot(a, b):
    return lax.dot_general(a, b, (((1,), (1,)), ((), ())), preferred_element_type=F32)


def _ret_kernel(lg_ref, q_ref, k_ref, v_ref, g_ref, ck_ref, cv_ref, gn_ref, o_ref,
                sf_scr, sb_scr, ob_scr, *, chunk, n_chunks, ctx_len):
    h = pl.program_id(1)
    p = pl.program_id(2)
    c = pl.program_id(3)
    lgf = lg_ref[0, h]
    lgb = lg_ref[1, h]

    def col_iota(n):
        return lax.broadcasted_iota(I32, (n, 1), 0).astype(F32)

    def vexp(s):
        return jnp.exp(jnp.zeros((1, 1), F32) + s)

    @pl.when((p == 0) & (c == 0))
    def _():
        jc = col_iota(ctx_len)
        kc = ck_ref[...].astype(F32)
        vc = cv_ref[...]
        sf_scr[...] = _tn_dot((kc * jnp.exp(lgf * (ctx_len - 1.0 - jc))).astype(BF16), vc)
        sb_scr[...] = _tn_dot((kc * jnp.exp(lgb * jc)).astype(BF16), vc)

    ic = col_iota(chunk)

    @pl.when(p == 0)
    def _():
        cidx = n_chunks - 1 - c
        q = q_ref[...].astype(F32)
        qb = (q * jnp.exp(lgb * (chunk - ic))).astype(BF16)
        ob = jnp.dot(qb, sb_scr[...].astype(BF16), preferred_element_type=F32)
        ob_scr[pl.ds(pl.multiple_of(cidx * chunk, chunk), chunk), :] = ob
        kb = (k_ref[...].astype(F32) * jnp.exp(lgb * ic)).astype(BF16)
        sb_scr[...] = vexp(lgb * chunk) * sb_scr[...] + _tn_dot(kb, v_ref[...])

    @pl.when(p == 1)
    def _():
        q = q_ref[...].astype(F32)
        s = _nt_dot(q_ref[...], k_ref[...])
        ri = lax.broadcasted_iota(I32, (chunk, chunk), 0)
        ci = lax.broadcasted_iota(I32, (chunk, chunk), 1)
        d = (ri - ci).astype(F32)
        mask = jnp.where(d > 0, jnp.exp(lgf * jnp.maximum(d, 0.0)),
                         jnp.where(d < 0, jnp.exp(lgb * jnp.maximum(-d, 0.0)), 2.0))
        a = (s * mask).astype(BF16)
        qf = (q * jnp.exp(lgf * (ic + 1.0))).astype(BF16)
        o = (jnp.dot(a, v_ref[...], preferred_element_type=F32)
             + jnp.dot(qf, sf_scr[...].astype(BF16), preferred_element_type=F32)
             + ob_scr[pl.ds(pl.multiple_of(c * chunk, chunk), chunk), :])
        kf = (k_ref[...].astype(F32) * jnp.exp(lgf * (chunk - 1.0 - ic))).astype(BF16)
        sf_scr[...] = vexp(lgf * chunk) * sf_scr[...] + _tn_dot(kf, v_ref[...])
        ms = jnp.mean(o * o, axis=-1, keepdims=True)
        y = o * lax.rsqrt(ms + EPS) * gn_ref[0]
        gt = g_ref[...].astype(F32)
        o_ref[...] = (y * (gt * jax.nn.sigmoid(gt))).astype(BF16)


def _retention(lg, p_lat, p_ctx, ret_norm_g, batch, seq, ctx_len):
    chunk = min(256, seq)
    nc = seq // chunk
    kern = functools.partial(_ret_kernel, chunk=chunk, n_chunks=nc, ctx_len=ctx_len)

    def rows(b, p, c):
        return b * nc + jnp.where(p == 0, nc - 1 - c, c)

    def rows_fwd(b, p, c):
        return b * nc + jnp.where(p == 0, 0, c)

    return pl.pallas_call(
        kern,
        out_shape=jax.ShapeDtypeStruct((batch * seq, RET_HEADS * RET_DV), BF16),
        grid=(batch, RET_HEADS, 2, nc),
        in_specs=[pl.BlockSpec(memory_space=pltpu.SMEM),
                  pl.BlockSpec((chunk, RET_DK), lambda b, h, p, c: (rows(b, p, c), h)),
                  pl.BlockSpec((chunk, RET_DK), lambda b, h, p, c: (rows(b, p, c), 4 + h)),
                  pl.BlockSpec((chunk, RET_DV), lambda b, h, p, c: (rows(b, p, c), 4 + h)),
                  pl.BlockSpec((chunk, RET_DV), lambda b, h, p, c: (rows_fwd(b, p, c), 8 + h)),
                  pl.BlockSpec((ctx_len, RET_DK), lambda b, h, p, c: (b, 4 + h)),
                  pl.BlockSpec((ctx_len, RET_DV), lambda b, h, p, c: (b, 4 + h)),
                  pl.BlockSpec((1, 1, RET_DV), lambda b, h, p, c: (h, 0, 0))],
        out_specs=pl.BlockSpec((chunk, RET_DV), lambda b, h, p, c: (rows_fwd(b, p, c), h)),
        scratch_shapes=[pltpu.VMEM((RET_DK, RET_DV), F32), pltpu.VMEM((RET_DK, RET_DV), F32),
                        pltpu.VMEM((seq, RET_DV), F32)],
        compiler_params=_params(("arbitrary",) * 4, 48),
        name="ret",
    )(lg, p_lat, p_lat, p_lat, p_lat, p_ctx, p_ctx, ret_norm_g.reshape(RET_HEADS, 1, RET_DV))


def _dattn_kernel(lam_ref, q_ref, ck_ref, cv_ref, k_ref, v_ref, gn_ref, o_ref,
                  qq_scr, m_scr, l_scr, acc_scr, *, tq, n_kv):
    ki = pl.program_id(3)

    def update(k, v):
        s = _nt_dot(qq_scr[...], k)
        m_prev = m_scr[...]
        m_new = jnp.maximum(m_prev, jnp.max(s, axis=1, keepdims=True))
        alpha = jnp.exp(m_prev - m_new)
        p = jnp.exp(s - m_new)
        l_scr[...] = alpha * l_scr[...] + jnp.sum(p, axis=1, keepdims=True)
        acc_scr[...] = alpha * acc_scr[...] + jnp.dot(p.astype(BF16), v, preferred_element_type=F32)
        m_scr[...] = m_new

    @pl.when(ki == 0)
    def _():
        q = q_ref[...]
        lane = lax.broadcasted_iota(I32, q.shape, 1)
        z = jnp.zeros_like(q)
        qq_scr[0:tq, :] = jnp.where(lane < DIFF_DH, q, z)
        qq_scr[tq:2 * tq, :] = jnp.where(lane >= DIFF_DH, q, z)
        m_scr[...] = jnp.full(m_scr.shape, NEG_BIG, F32)
        l_scr[...] = jnp.zeros(l_scr.shape, F32)
        acc_scr[...] = jnp.zeros(acc_scr.shape, F32)
        update(ck_ref[...], cv_ref[...])

    update(k_ref[...], v_ref[...])

    @pl.when(ki == n_kv - 1)
    def _():
        o = acc_scr[...] / l_scr[...]
        d = o[0:tq, :] - lam_ref[0] * o[tq:2 * tq, :]
        ms = jnp.mean(d * d, axis=-1, keepdims=True)
        y = d * lax.rsqrt(ms + EPS) * gn_ref[...] * (1.0 - LAMBDA_INIT)
        o_ref[...] = y.astype(BF16)


def _diff_attention(lam, p_lat, p_ctx, diff_norm_g, batch, seq, ctx_len):
    tq = min(512, seq)
    tk = min(512, seq)
    nq, nk = seq // tq, seq // tk
    kern = functools.partial(_dattn_kernel, tq=tq, n_kv=nk)
    return pl.pallas_call(
        kern,
        out_shape=jax.ShapeDtypeStruct((batch * seq, DIFF_HEADS * DIFF_DV), BF16),
        grid=(batch, DIFF_HEADS, nq, nk),
        in_specs=[pl.BlockSpec(memory_space=pltpu.SMEM),
                  pl.BlockSpec((tq, LANES), lambda b, h, qi, ki: (b * nq + qi, 48 + h)),
                  pl.BlockSpec((ctx_len, LANES), lambda b, h, qi, ki: (b, 56 + h)),
                  pl.BlockSpec((ctx_len, LANES), lambda b, h, qi, ki: (b, 64 + h)),
                  pl.BlockSpec((tk, LANES), lambda b, h, qi, ki: (b * nk + ki, 56 + h)),
                  pl.BlockSpec((tk, LANES), lambda b, h, qi, ki: (b * nk + ki, 64 + h)),
                  pl.BlockSpec((1, LANES), lambda b, h, qi, ki: (0, 0))],
        out_specs=pl.BlockSpec((tq, LANES), lambda b, h, qi, ki: (b * nq + qi, h)),
        scratch_shapes=[pltpu.VMEM((2 * tq, LANES), BF16), pltpu.VMEM((2 * tq, 1), F32),
                        pltpu.VMEM((2 * tq, 1), F32), pltpu.VMEM((2 * tq, LANES), F32)],
        compiler_params=_params(("arbitrary",) * 4, 48),
        name="dattn",
    )(lam, p_lat, p_ctx, p_ctx, p_lat, p_lat, diff_norm_g.reshape(1, DIFF_DV))


def _merge_kernel(yr_ref, yd_ref, ga_ref, gb_ref, x_ref, gatea_ref, shf_ref, scf_ref, n2_ref,
                  wr_ref, wd_ref, wo_ref, rw_ref, rb_ref,
                  xn_ref, hp_ref, idx_ref, rank_ref, gate_ref, cnt_ref, run_scr, *, tm):
    i = pl.program_id(0)

    @pl.when(i == 0)
    def _():
        run_scr[...] = jnp.zeros(run_scr.shape, F32)

    yr = jnp.dot(yr_ref[...], wr_ref[...], preferred_element_type=F32)
    yd = jnp.dot(yd_ref[...], wd_ref[...], preferred_element_type=F32)
    m = (jax.nn.sigmoid(ga_ref[...].astype(F32)) * yr + jax.nn.sigmoid(gb_ref[...].astype(F32)) * yd)
    z = jnp.dot(m.astype(BF16), wo_ref[...], preferred_element_type=F32)
    xn = x_ref[...] + gatea_ref[0] * z
    xn_ref[...] = xn

    ms = jnp.mean(xn * xn, axis=-1, keepdims=True)
    h2 = xn * lax.rsqrt(ms + EPS) * n2_ref[...]
    h2 = h2 * (1.0 + scf_ref[0]) + shf_ref[0]
    bits = pltpu.bitcast(h2.astype(BF16).astype(F32), U32)
    half = D_MODEL // 2
    hp_ref[...] = (bits[:, :half] >> 16) | (bits[:, half:] & jnp.uint32(0xFFFF0000))

    logits = jnp.dot(h2, rw_ref[...], preferred_element_type=F32, precision=HIGHEST) + rb_ref[...]
    lane = lax.broadcasted_iota(I32, (tm, LANES), 1)
    lanef = lane.astype(F32)
    work = logits
    vals, idxs = [], []
    for _ in range(TOP_K):
        mk = jnp.max(work, axis=1, keepdims=True)
        ik = jnp.min(jnp.where(work == mk, lanef, float(LANES)), axis=1, keepdims=True)
        vals.append(mk)
        idxs.append(ik)
        work = jnp.where(lanef == ik, -jnp.inf, work)
    ex = [jnp.exp(v - vals[0]) for v in vals]
    den = ex[0] + ex[1] + ex[2] + ex[3]

    onehot = jnp.zeros((tm, LANES), F32)
    for ik in idxs:
        onehot = onehot + jnp.where(lanef == ik, 1.0, 0.0)
    ri = lax.broadcasted_iota(I32, (tm, tm), 0)
    ci = lax.broadcasted_iota(I32, (tm, tm), 1)
    tri = jnp.where(ri > ci, 1.0, 0.0).astype(BF16)
    base = run_scr[0:1, :] + jnp.dot(tri, onehot.astype(BF16), preferred_element_type=F32)
    run_scr[...] = run_scr[...] + jnp.sum(onehot, axis=0, keepdims=True)

    idx_out = jnp.zeros((tm, LANES), F32)
    rank_out = jnp.zeros((tm, LANES), F32)
    gate_out = jnp.zeros((tm, LANES), F32)
    for k in range(TOP_K):
        rk = jnp.sum(jnp.where(lanef == idxs[k], base, 0.0), axis=1, keepdims=True)
        sel = lane == k
        idx_out = jnp.where(sel, idxs[k], idx_out)
        rank_out = jnp.where(sel, rk, rank_out)
        gate_out = jnp.where(sel, ex[k] / den, gate_out)
    idx_ref[...] = idx_out.astype(I32)
    rank_ref[...] = rank_out.astype(I32)
    gate_ref[...] = gate_out
    cnt_ref[...] = run_scr[...]


def _merge(y_ret, y_diff, p_lat, x2, g_a, sh_f, sc_f, norm2_g, w_r, w_d, w_o, rw, rb, seq):
    n = x2.shape[0]
    tm = min(512, seq)
    tpb = seq // tm
    kern = functools.partial(_merge_kernel, tm=tm)
    mod_spec = pl.BlockSpec((1, 1, D_MODEL), lambda i: (i // tpb, 0, 0))
    const = lambda shape: pl.BlockSpec(shape, lambda i: (0,) * len(shape))
    tok = lambda w: pl.BlockSpec((tm, w), lambda i: (i, 0))
    return pl.pallas_call(
        kern,
        out_shape=(jax.ShapeDtypeStruct((n, D_MODEL), F32),
                   jax.ShapeDtypeStruct((n, D_MODEL // 2), U32),
                   jax.ShapeDtypeStruct((n, LANES), I32),
                   jax.ShapeDtypeStruct((n, LANES), I32),
                   jax.ShapeDtypeStruct((n, LANES), F32),
                   jax.ShapeDtypeStruct((8, LANES), F32)),
        grid=(n // tm,),
        in_specs=[tok(RET_HEADS * RET_DV), tok(D_MODEL),
                  pl.BlockSpec((tm, COL_TILE), lambda i: (i, 9)),
                  pl.BlockSpec((tm, COL_TILE), lambda i: (i, 10)),
                  tok(D_MODEL), mod_spec, mod_spec, mod_spec, const((1, D_MODEL)),
                  const((RET_HEADS * RET_DV, D_MODEL)), const((D_MODEL, D_MODEL)), const((D_MODEL, D_MODEL)),
                  const((D_MODEL, LANES)), const((1, LANES))],
        out_specs=(tok(D_MODEL), tok(D_MODEL // 2), tok(LANES), tok(LANES), tok(LANES), const((8, LANES))),
        scratch_shapes=[pltpu.VMEM((8, LANES), F32)],
        compiler_params=_params(("arbitrary",), 56),
        name="merge",
    )(y_ret, y_diff, p_lat, p_lat, x2, g_a, sh_f, sc_f, norm2_g, w_r, w_d, w_o, rw, rb)


def _dispatch_kernel(dest_ref, hp_ref, xb_in_ref, xb_ref, sem, *, tm):
    del xb_in_ref
    i = pl.program_id(0)

    def body(r, carry):
        base = (i * tm + r) * TOP_K
        for k in range(TOP_K):
            d = dest_ref[base + k]
            pltpu.make_async_copy(hp_ref.at[pl.ds(r, 1), :], xb_ref.at[pl.ds(d, 1), :], sem).start()
        return carry

    lax.fori_loop(0, tm, body, 0)
    for _ in range(TOP_K):
        pltpu.make_async_copy(hp_ref, xb_ref.at[pl.ds(0, tm), :], sem).wait()


def _dispatch(dest, hp, xb_zero):
    n = hp.shape[0]
    tm = min(512, n)
    kern = functools.partial(_dispatch_kernel, tm=tm)
    return pl.pallas_call(
        kern,
        out_shape=jax.ShapeDtypeStruct(xb_zero.shape, U32),
        grid_spec=pltpu.PrefetchScalarGridSpec(
            num_scalar_prefetch=1,
            grid=(n // tm,),
            in_specs=[pl.BlockSpec((tm, D_MODEL // 2), lambda i, d: (i, 0)),
                      pl.BlockSpec(memory_space=pl.ANY)],
            out_specs=pl.BlockSpec(memory_space=pl.ANY),
            scratch_shapes=[pltpu.SemaphoreType.DMA(())]),
        input_output_aliases={2: 0},
        compiler_params=_params(("arbitrary",), 32),
        name="dispatch",
    )(dest, hp, xb_zero)


def _expert_kernel(be_ref, nb_ref, xb_ref, w1_ref, b1_ref, w2_ref, b2_ref, y_ref, w1b_scr, w2b_scr):
    i = pl.program_id(0)
    e = be_ref[i]
    prev = be_ref[jnp.maximum(i - 1, 0)]

    @pl.when((i == 0) | (e != prev))
    def _():
        w1b_scr[...] = w1_ref[0].astype(BF16)
        w2b_scr[...] = w2_ref[0].astype(BF16)

    @pl.when(i < nb_ref[0])
    def _():
        xu = xb_ref[...]
        x_lo = pltpu.bitcast(xu << 16, F32).astype(BF16)
        x_hi = pltpu.bitcast(xu & jnp.uint32(0xFFFF0000), F32).astype(BF16)
        half = D_MODEL // 2
        hh = (jnp.dot(x_lo, w1b_scr[0:half, :], preferred_element_type=F32)
              + jnp.dot(x_hi, w1b_scr[half:, :], preferred_element_type=F32) + b1_ref[0])
        glu = jnp.minimum(hh[:, :D_FF], SWIGLU_LIMIT)
        lin = jnp.clip(hh[:, D_FF:], -SWIGLU_LIMIT, SWIGLU_LIMIT)
        act = glu * jax.nn.sigmoid(SWIGLU_ALPHA * glu) * (lin + 1.0)
        y_ref[...] = jnp.dot(act.astype(BF16), w2b_scr[...], preferred_element_type=F32) + b2_ref[0]

    @pl.when(i >= nb_ref[0])
    def _():
        y_ref[...] = jnp.zeros(y_ref.shape, F32)


def _experts(block_e, n_used, xb, w1, b1, w2, b2):
    rows = xb.shape[0]
    n_blocks = rows // MOE_BLK
    return pl.pallas_call(
        _expert_kernel,
        out_shape=jax.ShapeDtypeStruct((rows, D_MODEL), F32),
        grid_spec=pltpu.PrefetchScalarGridSpec(
            num_scalar_prefetch=2,
            grid=(n_blocks,),
            in_specs=[pl.BlockSpec((MOE_BLK, D_MODEL // 2), lambda i, be, nb: (i, 0)),
                      pl.BlockSpec((1, D_MODEL, 2 * D_FF), lambda i, be, nb: (be[i], 0, 0)),
                      pl.BlockSpec((1, 1, 2 * D_FF), lambda i, be, nb: (be[i], 0, 0)),
                      pl.BlockSpec((1, D_FF, D_MODEL), lambda i, be, nb: (be[i], 0, 0)),
                      pl.BlockSpec((1, 1, D_MODEL), lambda i, be, nb: (be[i], 0, 0))],
            out_specs=pl.BlockSpec((MOE_BLK, D_MODEL), lambda i, be, nb: (i, 0)),
            scratch_shapes=[pltpu.VMEM((D_MODEL, 2 * D_FF), BF16), pltpu.VMEM((D_FF, D_MODEL), BF16)]),
        compiler_params=_params(("arbitrary",), 56),
        name="expert",
    )(block_e, n_used, xb, w1, b1.reshape(N_EXPERTS, 1, 2 * D_FF), w2, b2.reshape(N_EXPERTS, 1, D_MODEL))


def _combine_kernel(dest_ref, yb_ref, gate_ref, xn_ref, gf_ref, o_ref, buf, sem, *, tm):
    i = pl.program_id(0)

    def body(r, carry):
        base = (i * tm + r) * TOP_K
        for k in range(TOP_K):
            d = dest_ref[base + k]
            pltpu.make_async_copy(yb_ref.at[pl.ds(d, 1), :], buf.at[k, pl.ds(r, 1), :], sem).start()
        return carry

    lax.fori_loop(0, tm, body, 0)
    for k in range(TOP_K):
        pltpu.make_async_copy(yb_ref.at[pl.ds(0, tm), :], buf.at[k], sem).wait()
    g = gate_ref[...]
    y = g[:, 0:1] * buf[0]
    for k in range(1, TOP_K):
        y = y + g[:, k:k + 1] * buf[k]
    o_ref[...] = xn_ref[...] + gf_ref[0] * y


def _combine(dest, yb, gate, xn, g_f, seq):
    n = xn.shape[0]
    tm = min(256, seq)
    tpb = seq // tm
    kern = functools.partial(_combine_kernel, tm=tm)
    return pl.pallas_call(
        kern,
        out_shape=jax.ShapeDtypeStruct((n, D_MODEL), F32),
        grid_spec=pltpu.PrefetchScalarGridSpec(
            num_scalar_prefetch=1,
            grid=(n // tm,),
            in_specs=[pl.BlockSpec(memory_space=pl.ANY),
                      pl.BlockSpec((tm, LANES), lambda i, d: (i, 0)),
                      pl.BlockSpec((tm, D_MODEL), lambda i, d: (i, 0)),
                      pl.BlockSpec((1, 1, D_MODEL), lambda i, d: (i // tpb, 0, 0))],
            out_specs=pl.BlockSpec((tm, D_MODEL), lambda i, d: (i, 0)),
            scratch_shapes=[pltpu.VMEM((TOP_K, tm, D_MODEL), F32), pltpu.SemaphoreType.DMA(())]),
        compiler_params=_params(("arbitrary",), 32),
        name="combine",
    )(dest, yb, gate, xn, g_f)


def _layer(x, ctx, c, c_ctx, norm1_g, norm2_g, w_mod, b_mod, w_in, ret_decay_logit, ret_norm_g,
           diff_q_norm_g, diff_k_norm_g, diff_lambda, diff_norm_g, w_br_ret, w_br_diff, w_out,
           router_w, router_b, exp_w1, exp_b1, exp_w2, exp_b2):
    batch, seq, d = x.shape
    ctx_len = ctx.shape[1]
    assert d == D_MODEL and seq % GRID_W == 0 and batch + 1 <= 8
    n_tok = batch * seq

    cc = jnp.zeros((8, D_MODEL), F32).at[:batch].set(c).at[batch].set(c_ctx)
    mod = _mod(cc, w_mod, b_mod)
    sh_a, sc_a, g_a, sh_f, sc_f, g_f = [mod[:batch, i * D_MODEL:(i + 1) * D_MODEL].reshape(batch, 1, D_MODEL)
                                         for i in range(6)]
    csh_a = mod[batch:batch + 1, 0:D_MODEL].reshape(1, 1, D_MODEL)
    csc_a = mod[batch:batch + 1, D_MODEL:2 * D_MODEL].reshape(1, 1, D_MODEL)

    w_in_bf = w_in.astype(BF16)
    g1 = norm1_g.reshape(1, D_MODEL)
    tile = lambda g: jnp.tile(g.astype(F32), 2)
    qkg = jnp.zeros((8, LANES), F32).at[0].set(tile(diff_q_norm_g) * DIFF_DH ** -0.5).at[1].set(tile(diff_k_norm_g))
    lane = jnp.arange(LANES)
    gmat = (lane[:, None] // DIFF_DH == lane[None, :] // DIFF_DH).astype(BF16)
    x2 = x.reshape(n_tok, D_MODEL)
    tm = min(1024, seq)
    p_lat = _inproj(x2, g1, sh_a, sc_a, w_in_bf, qkg, gmat, _rope_tables(seq), tm, seq // tm)
    p_ctx = _inproj(ctx.reshape(batch * ctx_len, D_MODEL), g1, csh_a, csc_a, w_in_bf, qkg, gmat,
                    _identity_tables(ctx_len), ctx_len, 1)

    lg = jax.nn.log_sigmoid(ret_decay_logit.astype(F32))
    y_ret = _retention(lg, p_lat, p_ctx, ret_norm_g, batch, seq, ctx_len)

    lp = diff_lambda.astype(F32)
    lam = (jnp.exp(jnp.sum(lp[0] * lp[1])) - jnp.exp(jnp.sum(lp[2] * lp[3])) + LAMBDA_INIT).reshape(1)
    y_diff = _diff_attention(lam, p_lat, p_ctx, diff_norm_g, batch, seq, ctx_len)

    rw = jnp.zeros((D_MODEL, LANES), F32).at[:, :N_EXPERTS].set(router_w)
    rb = jnp.full((1, LANES), NEG_BIG, F32).at[0, :N_EXPERTS].set(router_b)
    xn, hp, idx4, rank4, gate4, cnt = _merge(
        y_ret, y_diff, p_lat, x2, g_a, sh_f, sc_f, norm2_g.reshape(1, D_MODEL),
        w_br_ret.astype(BF16), w_br_diff.astype(BF16), w_out.astype(BF16), rw, rb, seq)

    counts = cnt[0, :N_EXPERTS].astype(I32)
    padded = (counts + MOE_BLK - 1) // MOE_BLK * MOE_BLK
    pad_end = jnp.cumsum(padded)
    pad_start = pad_end - padded
    n_pairs = n_tok * TOP_K
    n_blocks = n_pairs // MOE_BLK + N_EXPERTS
    dest = (pad_start[idx4[:, :TOP_K]] + rank4[:, :TOP_K]).reshape(n_pairs).astype(I32)
    block_e = jnp.minimum(jnp.searchsorted(pad_end, jnp.arange(n_blocks, dtype=I32) * MOE_BLK, side='right'),
                          N_EXPERTS - 1).astype(I32)
    n_used = (pad_end[-1] // MOE_BLK).reshape(1).astype(I32)

    xb = _dispatch(dest, hp, jnp.zeros((n_blocks * MOE_BLK, D_MODEL // 2), U32))
    yb = _experts(block_e, n_used, xb, exp_w1, exp_b1, exp_w2, exp_b2)
    out = _combine(dest, yb, gate4, xn, g_f, seq)
    return out.reshape(batch, seq, D_MODEL)


def kernel(x, c, ctx, c_ctx, norm1_g, norm2_g, w_mod, b_mod, w_in, ret_decay_logit, ret_norm_g, diff_q_norm_g, diff_k_norm_g, diff_lambda, diff_norm_g, w_br_ret, w_br_diff, w_out, router_w, router_b, exp_w1, exp_b1, exp_w2, exp_b2):
    assert norm1_g.shape[0] == 1, "single-layer block"
    return _layer(x, ctx, c, c_ctx, norm1_g[0], norm2_g[0], w_mod[0], b_mod[0], w_in[0], ret_decay_logit[0],
                  ret_norm_g[0], diff_q_norm_g[0], diff_k_norm_g[0], diff_lambda[0], diff_norm_g[0],
                  w_br_ret[0], w_br_diff[0], w_out[0], router_w[0], router_b[0],
                  exp_w1[0], exp_b1[0], exp_w2[0], exp_b2[0])
```

```python
import functools
import math

import jax
import jax.numpy as jnp
from jax import lax
from jax.experimental import pallas as pl
from jax.experimental.pallas import tpu as pltpu

F32 = jnp.float32
BF16 = jnp.bfloat16
U32 = jnp.uint32
I32 = jnp.int32

D_MODEL = 1024
GRID_W = 64
RET_HEADS = 4
RET_DK = 256
RET_DV = 512
DIFF_DH = 64
DIFF_HEADS = 8
DIFF_DV = 128
N_EXPERTS = 32
TOP_K = 4
D_FF = 1024
SWIGLU_LIMIT = 7.0
SWIGLU_ALPHA = 1.702
ROPE_BASE = 10000.0
EPS = 1e-6
LAMBDA_INIT = 0.8 - 0.6 * math.exp(-0.3 * 0)

IN_COLS = 11264
COL_TILE = 1024
LANES = 128
MOE_BLK = 256
NEG_BIG = -1e30
LOG2E = 1.4426950408889634
SHIFT_SLACK = 1.0 + 2.0 ** -6
MAX_SAFE_SHIFT = 60.0
KV_UNROLL = 2
HIGHEST = lax.Precision.HIGHEST
MIB = 1024 * 1024


def _params(sem, vmem_mib):
    return pltpu.CompilerParams(dimension_semantics=sem, vmem_limit_bytes=vmem_mib * MIB)


def _mod_kernel(c_ref, w_ref, b_ref, o_ref):
    c = c_ref[...]
    s = c * jax.nn.sigmoid(c)
    o_ref[...] = jnp.dot(s, w_ref[...], preferred_element_type=F32, precision=HIGHEST) + b_ref[...]


def _mod(cc, w_mod, b_mod):
    n = w_mod.shape[1]
    tn = 1024
    return pl.pallas_call(
        _mod_kernel,
        out_shape=jax.ShapeDtypeStruct((8, n), F32),
        grid=(n // tn,),
        in_specs=[pl.BlockSpec((8, D_MODEL), lambda j: (0, 0)),
                  pl.BlockSpec((D_MODEL, tn), lambda j: (0, j)),
                  pl.BlockSpec((1, tn), lambda j: (0, j))],
        out_specs=pl.BlockSpec((8, tn), lambda j: (0, j)),
        compiler_params=_params(("arbitrary",), 32),
        name="mod",
    )(cc, w_mod, b_mod.reshape(1, n))


def _inproj_kernel(x_ref, g_ref, sh_ref, sc_ref, w_ref, qkg_ref, gmat_ref,
                   cr_ref, sr_ref, cc_ref, sc2_ref, cd_ref, sa_ref, sb_ref,
                   o_ref, h_scr, acc_scr):
    j = pl.program_id(1)

    @pl.when(j == 0)
    def _():
        xf = x_ref[...]
        ms = jnp.mean(xf * xf, axis=-1, keepdims=True)
        y = xf * lax.rsqrt(ms + EPS) * g_ref[...]
        h_scr[...] = (y * (1.0 + sc_ref[0]) + sh_ref[0]).astype(BF16)

    acc_scr[...] = jnp.dot(h_scr[...], w_ref[...], preferred_element_type=F32)
    is_ret = j <= 1
    is_dqk = (j == 6) | (j == 7)

    @pl.when(is_ret)
    def _():
        scale = jnp.where(j == 0, RET_DK ** -0.5, 1.0).astype(F32)
        for b in range(COL_TILE // LANES):
            xb = acc_scr[:, b * LANES:(b + 1) * LANES]
            cos = cr_ref[...] if b % 2 == 0 else cc_ref[...]
            sin = sr_ref[...] if b % 2 == 0 else sc2_ref[...]
            o = (xb * cos + pltpu.roll(xb, 64, 1) * sin) * scale
            o_ref[:, b * LANES:(b + 1) * LANES] = o.astype(BF16)

    @pl.when(is_dqk)
    def _():
        g = jnp.where(j == 6, qkg_ref[0:1, :], qkg_ref[1:2, :])
        for b in range(COL_TILE // LANES):
            xb = acc_scr[:, b * LANES:(b + 1) * LANES]
            ss = jnp.dot((xb * xb).astype(BF16), gmat_ref[...], preferred_element_type=F32)
            yn = xb * lax.rsqrt(ss * (1.0 / DIFF_DH) + EPS) * g
            o = yn * cd_ref[...] + pltpu.roll(yn, 16, 1) * sa_ref[...] + pltpu.roll(yn, 112, 1) * sb_ref[...]
            o_ref[:, b * LANES:(b + 1) * LANES] = o.astype(BF16)

    @pl.when(jnp.logical_not(is_ret | is_dqk))
    def _():
        o_ref[...] = acc_scr[...].astype(BF16)


def _inproj(x2, g1, sh, sc, w_bf, qkg, gmat, tables, tm, tiles_per_batch):
    n = x2.shape[0]
    nb = sh.shape[0]
    tab_spec = pl.BlockSpec((tm, LANES), lambda i, j: (i % tiles_per_batch, 0))
    mod_spec = pl.BlockSpec((1, 1, D_MODEL), lambda i, j: (jnp.minimum(i // tiles_per_batch, nb - 1), 0, 0))
    return pl.pallas_call(
        _inproj_kernel,
        out_shape=jax.ShapeDtypeStruct((n, IN_COLS), BF16),
        grid=(n // tm, IN_COLS // COL_TILE),
        in_specs=[pl.BlockSpec((tm, D_MODEL), lambda i, j: (i, 0)),
                  pl.BlockSpec((1, D_MODEL), lambda i, j: (0, 0)),
                  mod_spec, mod_spec,
                  pl.BlockSpec((D_MODEL, COL_TILE), lambda i, j: (0, j)),
                  pl.BlockSpec((8, LANES), lambda i, j: (0, 0)),
                  pl.BlockSpec((LANES, LANES), lambda i, j: (0, 0))] + [tab_spec] * 7,
        out_specs=pl.BlockSpec((tm, COL_TILE), lambda i, j: (i, j)),
        scratch_shapes=[pltpu.VMEM((tm, D_MODEL), BF16), pltpu.VMEM((tm, COL_TILE), F32)],
        compiler_params=_params(("arbitrary", "arbitrary"), 48),
        name="inproj",
    )(x2, g1, sh, sc, w_bf, qkg, gmat, *tables)


def _rope_tables(seq):
    t = jnp.arange(seq, dtype=jnp.int32)
    row = (t // GRID_W).astype(F32)[:, None]
    col = (t % GRID_W).astype(F32)[:, None]
    inv64 = ROPE_BASE ** (-jnp.arange(64, dtype=F32) / 64)
    inv16 = ROPE_BASE ** (-jnp.arange(16, dtype=F32) / 16)
    ar, ac = row * inv64[None, :], col * inv64[None, :]
    cr = jnp.concatenate([jnp.cos(ar), jnp.cos(ar)], axis=1)
    sr = jnp.concatenate([-jnp.sin(ar), jnp.sin(ar)], axis=1)
    cc = jnp.concatenate([jnp.cos(ac), jnp.cos(ac)], axis=1)
    sc = jnp.concatenate([-jnp.sin(ac), jnp.sin(ac)], axis=1)
    br, bc = row * inv16[None, :], col * inv16[None, :]
    z = jnp.zeros_like(br)
    cd = jnp.tile(jnp.concatenate([jnp.cos(br), jnp.cos(br), jnp.cos(bc), jnp.cos(bc)], axis=1), (1, 2))
    sa = jnp.tile(jnp.concatenate([z, jnp.sin(br), z, jnp.sin(bc)], axis=1), (1, 2))
    sb = jnp.tile(jnp.concatenate([-jnp.sin(br), z, -jnp.sin(bc), z], axis=1), (1, 2))
    return [cr, sr, cc, sc, cd, sa, sb]


def _identity_tables(seq):
    one = jnp.ones((seq, LANES), F32)
    zero = jnp.zeros((seq, LANES), F32)
    return [one, zero, one, zero, one, zero, zero]


def _tn_dot(a, b):
    return lax.dot_general(a, b, (((0,), (0,)), ((), ())), preferred_element_type=F32)


def _nt_dot(a, b):
    return lax.dot_general(a, b, (((1,), (1,)), ((), ())), preferred_element_type=F32)


def _ret_kernel(lg_ref, q_ref, k_ref, v_ref, g_ref, ck_ref, cv_ref, gn_ref, o_ref,
                sf_scr, sb_scr, ob_scr, *, chunk, n_chunks, ctx_len):
    h = pl.program_id(1)
    p = pl.program_id(2)
    c = pl.program_id(3)
    lgf = lg_ref[0, h]
    lgb = lg_ref[1, h]

    def col_iota(n):
        return lax.broadcasted_iota(I32, (n, 1), 0).astype(F32)

    def vexp(s):
        return jnp.exp(jnp.zeros((1, 1), F32) + s)

    @pl.when((p == 0) & (c == 0))
    def _():
        jc = col_iota(ctx_len)
        kc = ck_ref[...].astype(F32)
        vc = cv_ref[...]
        sf_scr[...] = _tn_dot((kc * jnp.exp(lgf * (ctx_len - 1.0 - jc))).astype(BF16), vc)
        sb_scr[...] = _tn_dot((kc * jnp.exp(lgb * jc)).astype(BF16), vc)

    ic = col_iota(chunk)

    @pl.when(p == 0)
    def _():
        cidx = n_chunks - 1 - c
        q = q_ref[...].astype(F32)
        qb = (q * jnp.exp(lgb * (chunk - ic))).astype(BF16)
        ob = jnp.dot(qb, sb_scr[...].astype(BF16), preferred_element_type=F32)
        ob_scr[pl.ds(pl.multiple_of(cidx * chunk, chunk), chunk), :] = ob
        kb = (k_ref[...].astype(F32) * jnp.exp(lgb * ic)).astype(BF16)
        sb_scr[...] = vexp(lgb * chunk) * sb_scr[...] + _tn_dot(kb, v_ref[...])

    @pl.when(p == 1)
    def _():
        q = q_ref[...].astype(F32)
        s = _nt_dot(q_ref[...], k_ref[...])
        ri = lax.broadcasted_iota(I32, (chunk, chunk), 0)
        ci = lax.broadcasted_iota(I32, (chunk, chunk), 1)
        d = (ri - ci).astype(F32)
        mask = jnp.where(d > 0, jnp.exp(lgf * jnp.maximum(d, 0.0)),
                         jnp.where(d < 0, jnp.exp(lgb * jnp.maximum(-d, 0.0)), 2.0))
        a = (s * mask).astype(BF16)
        qf = (q * jnp.exp(lgf * (ic + 1.0))).astype(BF16)
        o = (jnp.dot(a, v_ref[...], preferred_element_type=F32)
             + jnp.dot(qf, sf_scr[...].astype(BF16), preferred_element_type=F32)
             + ob_scr[pl.ds(pl.multiple_of(c * chunk, chunk), chunk), :])
        kf = (k_ref[...].astype(F32) * jnp.exp(lgf * (chunk - 1.0 - ic))).astype(BF16)
        sf_scr[...] = vexp(lgf * chunk) * sf_scr[...] + _tn_dot(kf, v_ref[...])
        ms = jnp.mean(o * o, axis=-1, keepdims=True)
        y = o * lax.rsqrt(ms + EPS) * gn_ref[0]
        gt = g_ref[...].astype(F32)
        o_ref[...] = (y * (gt * jax.nn.sigmoid(gt))).astype(BF16)


def _retention(lg, p_lat, p_ctx, ret_norm_g, batch, seq, ctx_len):
    chunk = min(256, seq)
    nc = seq // chunk
    kern = functools.partial(_ret_kernel, chunk=chunk, n_chunks=nc, ctx_len=ctx_len)

    def rows(b, p, c):
        return b * nc + jnp.where(p == 0, nc - 1 - c, c)

    def rows_fwd(b, p, c):
        return b * nc + jnp.where(p == 0, 0, c)

    return pl.pallas_call(
        kern,
        out_shape=jax.ShapeDtypeStruct((batch * seq, RET_HEADS * RET_DV), BF16),
        grid=(batch, RET_HEADS, 2, nc),
        in_specs=[pl.BlockSpec(memory_space=pltpu.SMEM),
                  pl.BlockSpec((chunk, RET_DK), lambda b, h, p, c: (rows(b, p, c), h)),
                  pl.BlockSpec((chunk, RET_DK), lambda b, h, p, c: (rows(b, p, c), 4 + h)),
                  pl.BlockSpec((chunk, RET_DV), lambda b, h, p, c: (rows(b, p, c), 4 + h)),
                  pl.BlockSpec((chunk, RET_DV), lambda b, h, p, c: (rows_fwd(b, p, c), 8 + h)),
                  pl.BlockSpec((ctx_len, RET_DK), lambda b, h, p, c: (b, 4 + h)),
                  pl.BlockSpec((ctx_len, RET_DV), lambda b, h, p, c: (b, 4 + h)),
                  pl.BlockSpec((1, 1, RET_DV), lambda b, h, p, c: (h, 0, 0))],
        out_specs=pl.BlockSpec((chunk, RET_DV), lambda b, h, p, c: (rows_fwd(b, p, c), h)),
        scratch_shapes=[pltpu.VMEM((RET_DK, RET_DV), F32), pltpu.VMEM((RET_DK, RET_DV), F32),
                        pltpu.VMEM((seq, RET_DV), F32)],
        compiler_params=_params(("arbitrary",) * 4, 48),
        name="ret",
    )(lg, p_lat, p_lat, p_lat, p_lat, p_ctx, p_ctx, ret_norm_g.reshape(RET_HEADS, 1, RET_DV))


def _dattn_kernel(lam_ref, q_ref, ck_ref, cv_ref, k_ref, v_ref, gn_ref, gmat_ref, o_ref,
                  qq_scr, kmax_scr, mp_scr, acc_scr, *, tq, tk, n_kv):
    rows = 2 * tq
    qi = pl.program_id(2)

    def sq_norms(t):
        tf = t.astype(F32)
        return jnp.dot((tf * tf).astype(BF16), gmat_ref[...], preferred_element_type=F32)

    def kv_slice(ref, c):
        return ref[pl.ds(pl.multiple_of(c * tk, tk), tk), :]

    @pl.when(qi == 0)
    def _():
        kmax_scr[...] = jnp.max(sq_norms(ck_ref[...]), axis=0, keepdims=True)

        def body(c, carry):
            kmax_scr[...] = jnp.maximum(kmax_scr[...], jnp.max(sq_norms(kv_slice(k_ref, c)), axis=0, keepdims=True))
            return carry

        lax.fori_loop(0, n_kv, body, 0)

    q = q_ref[...]
    lane = lax.broadcasted_iota(I32, (tq, LANES), 1)
    z = jnp.zeros_like(q)
    qq_scr[0:tq, 0:LANES] = jnp.where(lane < DIFF_DH, q, z)
    qq_scr[tq:rows, 0:LANES] = jnp.where(lane >= DIFF_DH, q, z)

    def set_shift(shift):
        neg = -shift
        hi = neg.astype(BF16)
        r1 = neg - hi.astype(F32)
        mid = r1.astype(BF16)
        lo = (r1 - mid.astype(F32)).astype(BF16)
        lane_r = lax.broadcasted_iota(I32, (rows, LANES), 1)
        pieces = jnp.where(lane_r == 0, hi.astype(F32), jnp.where(lane_r == 1, mid.astype(F32),
                           jnp.where(lane_r == 2, lo.astype(F32), 0.0)))
        qq_scr[:, LANES:2 * LANES] = pieces.astype(BF16)

    bound = jnp.sqrt(sq_norms(q) * kmax_scr[...]) * SHIFT_SLACK
    b1 = jnp.max(jnp.where(lane < DIFF_DH, bound, 0.0), axis=1, keepdims=True)
    b2 = jnp.max(jnp.where(lane >= DIFF_DH, bound, 0.0), axis=1, keepdims=True)
    set_shift(jnp.concatenate([b1, b2], axis=0))

    @pl.when(jnp.max(bound) > MAX_SAFE_SHIFT)
    def _():
        def tile_max(k):
            s = _nt_dot(qq_scr[:, 0:LANES], k)
            mp = mp_scr[...]
            for cb in range(k.shape[0] // LANES):
                mp = jnp.maximum(mp, s[:, cb * LANES:(cb + 1) * LANES])
            mp_scr[...] = mp

        mp_scr[...] = jnp.full(mp_scr.shape, NEG_BIG, F32)
        tile_max(ck_ref[...])

        def body(c, carry):
            tile_max(kv_slice(k_ref, c))
            return carry

        lax.fori_loop(0, n_kv, body, 0)
        set_shift(jnp.max(mp_scr[...], axis=1, keepdims=True))

    def lane_const(n, hot):
        return jnp.where(lax.broadcasted_iota(I32, (n, LANES), 1) < hot, 1.0, 0.0).astype(BF16)

    acc_scr[...] = jnp.zeros(acc_scr.shape, F32)

    def tile_acc(k, v):
        n = k.shape[0]
        s = _nt_dot(qq_scr[...], jnp.concatenate([k, lane_const(n, 3)], axis=1))
        p = jnp.exp2(s).astype(BF16)
        acc_scr[...] += jnp.dot(p, jnp.concatenate([v, lane_const(n, 1)], axis=1), preferred_element_type=F32)

    tile_acc(ck_ref[...], cv_ref[...])

    def body_acc(c, carry):
        for u in range(KV_UNROLL):
            tile_acc(kv_slice(k_ref, c * KV_UNROLL + u), kv_slice(v_ref, c * KV_UNROLL + u))
        return carry

    lax.fori_loop(0, n_kv // KV_UNROLL, body_acc, 0)

    o = acc_scr[:, 0:LANES] / acc_scr[:, LANES:LANES + 1]
    d = o[0:tq, :] - lam_ref[0] * o[tq:rows, :]
    ms = jnp.mean(d * d, axis=-1, keepdims=True)
    y = d * lax.rsqrt(ms + EPS) * gn_ref[...] * (1.0 - LAMBDA_INIT)
    o_ref[...] = y.astype(BF16)


def _diff_attention(lam, p_lat, p_ctx, diff_norm_g, gmat, batch, seq, ctx_len):
    tq = min(256, seq)
    tk = min(512, seq // KV_UNROLL)
    nq, nk = seq // tq, seq // tk
    assert ctx_len <= tk and ctx_len % LANES == 0 and nk % KV_UNROLL == 0
    kern = functools.partial(_dattn_kernel, tq=tq, tk=tk, n_kv=nk)
    return pl.pallas_call(
        kern,
        out_shape=jax.ShapeDtypeStruct((batch * seq, DIFF_HEADS * DIFF_DV), BF16),
        grid=(batch, DIFF_HEADS, nq),
        in_specs=[pl.BlockSpec(memory_space=pltpu.SMEM),
                  pl.BlockSpec((tq, LANES), lambda b, h, qi: (b * nq + qi, 48 + h)),
                  pl.BlockSpec((ctx_len, LANES), lambda b, h, qi: (b, 56 + h)),
                  pl.BlockSpec((ctx_len, LANES), lambda b, h, qi: (b, 64 + h)),
                  pl.BlockSpec((seq, LANES), lambda b, h, qi: (b, 56 + h)),
                  pl.BlockSpec((seq, LANES), lambda b, h, qi: (b, 64 + h)),
                  pl.BlockSpec((1, LANES), lambda b, h, qi: (0, 0)),
                  pl.BlockSpec((LANES, LANES), lambda b, h, qi: (0, 0))],
        out_specs=pl.BlockSpec((tq, LANES), lambda b, h, qi: (b * nq + qi, h)),
        scratch_shapes=[pltpu.VMEM((2 * tq, 2 * LANES), BF16), pltpu.VMEM((1, LANES), F32),
                        pltpu.VMEM((2 * tq, LANES), F32), pltpu.VMEM((2 * tq, 2 * LANES), F32)],
        compiler_params=_params(("arbitrary",) * 3, 48),
        name="dattn",
    )(lam, p_lat, p_ctx, p_ctx, p_lat, p_lat, diff_norm_g.reshape(1, DIFF_DV), gmat)


def _merge_kernel(yr_ref, yd_ref, ga_ref, gb_ref, x_ref, gatea_ref, shf_ref, scf_ref, n2_ref,
                  wr_ref, wd_ref, wo_ref, rw_ref, rb_ref,
                  xn_ref, hp_ref, idx_ref, rank_ref, gate_ref, cnt_ref, run_scr, *, tm):
    i = pl.program_id(0)

    @pl.when(i == 0)
    def _():
        run_scr[...] = jnp.zeros(run_scr.shape, F32)

    yr = jnp.dot(yr_ref[...], wr_ref[...], preferred_element_type=F32)
    yd = jnp.dot(yd_ref[...], wd_ref[...], preferred_element_type=F32)
    m = (jax.nn.sigmoid(ga_ref[...].astype(F32)) * yr + jax.nn.sigmoid(gb_ref[...].astype(F32)) * yd)
    z = jnp.dot(m.astype(BF16), wo_ref[...], preferred_element_type=F32)
    xn = x_ref[...] + gatea_ref[0] * z
    xn_ref[...] = xn

    ms = jnp.mean(xn * xn, axis=-1, keepdims=True)
    h2 = xn * lax.rsqrt(ms + EPS) * n2_ref[...]
    h2 = h2 * (1.0 + scf_ref[0]) + shf_ref[0]
    bits = pltpu.bitcast(h2.astype(BF16).astype(F32), U32)
    half = D_MODEL // 2
    hp_ref[...] = (bits[:, :half] >> 16) | (bits[:, half:] & jnp.uint32(0xFFFF0000))

    logits = jnp.dot(h2, rw_ref[...], preferred_element_type=F32, precision=HIGHEST) + rb_ref[...]
    lane = lax.broadcasted_iota(I32, (tm, LANES), 1)
    lanef = lane.astype(F32)
    work = logits
    vals, idxs = [], []
    for _ in range(TOP_K):
        mk = jnp.max(work, axis=1, keepdims=True)
        ik = jnp.min(jnp.where(work == mk, lanef, float(LANES)), axis=1, keepdims=True)
        vals.append(mk)
        idxs.append(ik)
        work = jnp.where(lanef == ik, -jnp.inf, work)
    ex = [jnp.exp(v - vals[0]) for v in vals]
    den = ex[0] + ex[1] + ex[2] + ex[3]

    onehot = jnp.zeros((tm, LANES), F32)
    for ik in idxs:
        onehot = onehot + jnp.where(lanef == ik, 1.0, 0.0)
    ri = lax.broadcasted_iota(I32, (tm, tm), 0)
    ci = lax.broadcasted_iota(I32, (tm, tm), 1)
    tri = jnp.where(ri > ci, 1.0, 0.0).astype(BF16)
    base = run_scr[0:1, :] + jnp.dot(tri, onehot.astype(BF16), preferred_element_type=F32)
    run_scr[...] = run_scr[...] + jnp.sum(onehot, axis=0, keepdims=True)

    idx_out = jnp.zeros((tm, LANES), F32)
    rank_out = jnp.zeros((tm, LANES), F32)
    gate_out = jnp.zeros((tm, LANES), F32)
    for k in range(TOP_K):
        rk = jnp.sum(jnp.where(lanef == idxs[k], base, 0.0), axis=1, keepdims=True)
        sel = lane == k
        idx_out = jnp.where(sel, idxs[k], idx_out)
        rank_out = jnp.where(sel, rk, rank_out)
        gate_out = jnp.where(sel, ex[k] / den, gate_out)
    idx_ref[...] = idx_out.astype(I32)
    rank_ref[...] = rank_out.astype(I32)
    gate_ref[...] = gate_out
    cnt_ref[...] = run_scr[...]


def _merge(y_ret, y_diff, p_lat, x2, g_a, sh_f, sc_f, norm2_g, w_r, w_d, w_o, rw, rb, seq):
    n = x2.shape[0]
    tm = min(512, seq)
    tpb = seq // tm
    kern = functools.partial(_merge_kernel, tm=tm)
    mod_spec = pl.BlockSpec((1, 1, D_MODEL), lambda i: (i // tpb, 0, 0))
    const = lambda shape: pl.BlockSpec(shape, lambda i: (0,) * len(shape))
    tok = lambda w: pl.BlockSpec((tm, w), lambda i: (i, 0))
    return pl.pallas_call(
        kern,
        out_shape=(jax.ShapeDtypeStruct((n, D_MODEL), F32),
                   jax.ShapeDtypeStruct((n, D_MODEL // 2), U32),
                   jax.ShapeDtypeStruct((n, LANES), I32),
                   jax.ShapeDtypeStruct((n, LANES), I32),
                   jax.ShapeDtypeStruct((n, LANES), F32),
                   jax.ShapeDtypeStruct((8, LANES), F32)),
        grid=(n // tm,),
        in_specs=[tok(RET_HEADS * RET_DV), tok(D_MODEL),
                  pl.BlockSpec((tm, COL_TILE), lambda i: (i, 9)),
                  pl.BlockSpec((tm, COL_TILE), lambda i: (i, 10)),
                  tok(D_MODEL), mod_spec, mod_spec, mod_spec, const((1, D_MODEL)),
                  const((RET_HEADS * RET_DV, D_MODEL)), const((D_MODEL, D_MODEL)), const((D_MODEL, D_MODEL)),
                  const((D_MODEL, LANES)), const((1, LANES))],
        out_specs=(tok(D_MODEL), tok(D_MODEL // 2), tok(LANES), tok(LANES), tok(LANES), const((8, LANES))),
        scratch_shapes=[pltpu.VMEM((8, LANES), F32)],
        compiler_params=_params(("arbitrary",), 56),
        name="merge",
    )(y_ret, y_diff, p_lat, p_lat, x2, g_a, sh_f, sc_f, norm2_g, w_r, w_d, w_o, rw, rb)


def _dispatch_kernel(dest_ref, hp_ref, xb_in_ref, xb_ref, sem, *, tm):
    del xb_in_ref
    i = pl.program_id(0)

    def body(r, carry):
        base = (i * tm + r) * TOP_K
        for k in range(TOP_K):
            d = dest_ref[base + k]
            pltpu.make_async_copy(hp_ref.at[pl.ds(r, 1), :], xb_ref.at[pl.ds(d, 1), :], sem).start()
        return carry

    lax.fori_loop(0, tm, body, 0)
    for _ in range(TOP_K):
        pltpu.make_async_copy(hp_ref, xb_ref.at[pl.ds(0, tm), :], sem).wait()


def _dispatch(dest, hp, xb_zero):
    n = hp.shape[0]
    tm = min(512, n)
    kern = functools.partial(_dispatch_kernel, tm=tm)
    return pl.pallas_call(
        kern,
        out_shape=jax.ShapeDtypeStruct(xb_zero.shape, U32),
        grid_spec=pltpu.PrefetchScalarGridSpec(
            num_scalar_prefetch=1,
            grid=(n // tm,),
            in_specs=[pl.BlockSpec((tm, D_MODEL // 2), lambda i, d: (i, 0)),
                      pl.BlockSpec(memory_space=pl.ANY)],
            out_specs=pl.BlockSpec(memory_space=pl.ANY),
            scratch_shapes=[pltpu.SemaphoreType.DMA(())]),
        input_output_aliases={2: 0},
        compiler_params=_params(("arbitrary",), 32),
        name="dispatch",
    )(dest, hp, xb_zero)


def _expert_kernel(be_ref, nb_ref, xb_ref, w1_ref, b1_ref, w2_ref, b2_ref, y_ref, w1b_scr, w2b_scr):
    i = pl.program_id(0)
    e = be_ref[i]
    prev = be_ref[jnp.maximum(i - 1, 0)]

    @pl.when((i == 0) | (e != prev))
    def _():
        w1b_scr[...] = w1_ref[0].astype(BF16)
        w2b_scr[...] = w2_ref[0].astype(BF16)

    @pl.when(i < nb_ref[0])
    def _():
        xu = xb_ref[...]
        x_lo = pltpu.bitcast(xu << 16, F32).astype(BF16)
        x_hi = pltpu.bitcast(xu & jnp.uint32(0xFFFF0000), F32).astype(BF16)
        half = D_MODEL // 2
        hh = (jnp.dot(x_lo, w1b_scr[0:half, :], preferred_element_type=F32)
              + jnp.dot(x_hi, w1b_scr[half:, :], preferred_element_type=F32) + b1_ref[0])
        glu = jnp.minimum(hh[:, :D_FF], SWIGLU_LIMIT)
        lin = jnp.clip(hh[:, D_FF:], -SWIGLU_LIMIT, SWIGLU_LIMIT)
        act = glu * jax.nn.sigmoid(SWIGLU_ALPHA * glu) * (lin + 1.0)
        y_ref[...] = jnp.dot(act.astype(BF16), w2b_scr[...], preferred_element_type=F32) + b2_ref[0]

    @pl.when(i >= nb_ref[0])
    def _():
        y_ref[...] = jnp.zeros(y_ref.shape, F32)


def _experts(block_e, n_used, xb, w1, b1, w2, b2):
    rows = xb.shape[0]
    n_blocks = rows // MOE_BLK
    return pl.pallas_call(
        _expert_kernel,
        out_shape=jax.ShapeDtypeStruct((rows, D_MODEL), F32),
        grid_spec=pltpu.PrefetchScalarGridSpec(
            num_scalar_prefetch=2,
            grid=(n_blocks,),
            in_specs=[pl.BlockSpec((MOE_BLK, D_MODEL // 2), lambda i, be, nb: (i, 0)),
                      pl.BlockSpec((1, D_MODEL, 2 * D_FF), lambda i, be, nb: (be[i], 0, 0)),
                      pl.BlockSpec((1, 1, 2 * D_FF), lambda i, be, nb: (be[i], 0, 0)),
                      pl.BlockSpec((1, D_FF, D_MODEL), lambda i, be, nb: (be[i], 0, 0)),
                      pl.BlockSpec((1, 1, D_MODEL), lambda i, be, nb: (be[i], 0, 0))],
            out_specs=pl.BlockSpec((MOE_BLK, D_MODEL), lambda i, be, nb: (i, 0)),
            scratch_shapes=[pltpu.VMEM((D_MODEL, 2 * D_FF), BF16), pltpu.VMEM((D_FF, D_MODEL), BF16)]),
        compiler_params=_params(("arbitrary",), 56),
        name="expert",
    )(block_e, n_used, xb, w1, b1.reshape(N_EXPERTS, 1, 2 * D_FF), w2, b2.reshape(N_EXPERTS, 1, D_MODEL))


def _combine_kernel(dest_ref, yb_ref, gate_ref, xn_ref, gf_ref, o_ref, buf, sem, *, tm):
    i = pl.program_id(0)

    def body(r, carry):
        base = (i * tm + r) * TOP_K
        for k in range(TOP_K):
            d = dest_ref[base + k]
            pltpu.make_async_copy(yb_ref.at[pl.ds(d, 1), :], buf.at[k, pl.ds(r, 1), :], sem).start()
        return carry

    lax.fori_loop(0, tm, body, 0)
    for k in range(TOP_K):
        pltpu.make_async_copy(yb_ref.at[pl.ds(0, tm), :], buf.at[k], sem).wait()
    g = gate_ref[...]
    y = g[:, 0:1] * buf[0]
    for k in range(1, TOP_K):
        y = y + g[:, k:k + 1] * buf[k]
    o_ref[...] = xn_ref[...] + gf_ref[0] * y


def _combine(dest, yb, gate, xn, g_f, seq):
    n = xn.shape[0]
    tm = min(256, seq)
    tpb = seq // tm
    kern = functools.partial(_combine_kernel, tm=tm)
    return pl.pallas_call(
        kern,
        out_shape=jax.ShapeDtypeStruct((n, D_MODEL), F32),
        grid_spec=pltpu.PrefetchScalarGridSpec(
            num_scalar_prefetch=1,
            grid=(n // tm,),
            in_specs=[pl.BlockSpec(memory_space=pl.ANY),
                      pl.BlockSpec((tm, LANES), lambda i, d: (i, 0)),
                      pl.BlockSpec((tm, D_MODEL), lambda i, d: (i, 0)),
                      pl.BlockSpec((1, 1, D_MODEL), lambda i, d: (i // tpb, 0, 0))],
            out_specs=pl.BlockSpec((tm, D_MODEL), lambda i, d: (i, 0)),
            scratch_shapes=[pltpu.VMEM((TOP_K, tm, D_MODEL), F32), pltpu.SemaphoreType.DMA(())]),
        compiler_params=_params(("arbitrary",), 32),
        name="combine",
    )(dest, yb, gate, xn, g_f)


def _layer(x, ctx, c, c_ctx, norm1_g, norm2_g, w_mod, b_mod, w_in, ret_decay_logit, ret_norm_g,
           diff_q_norm_g, diff_k_norm_g, diff_lambda, diff_norm_g, w_br_ret, w_br_diff, w_out,
           router_w, router_b, exp_w1, exp_b1, exp_w2, exp_b2):
    batch, seq, d = x.shape
    ctx_len = ctx.shape[1]
    assert d == D_MODEL and seq % GRID_W == 0 and batch + 1 <= 8
    n_tok = batch * seq

    cc = jnp.zeros((8, D_MODEL), F32).at[:batch].set(c).at[batch].set(c_ctx)
    mod = _mod(cc, w_mod, b_mod)
    sh_a, sc_a, g_a, sh_f, sc_f, g_f = [mod[:batch, i * D_MODEL:(i + 1) * D_MODEL].reshape(batch, 1, D_MODEL)
                                         for i in range(6)]
    csh_a = mod[batch:batch + 1, 0:D_MODEL].reshape(1, 1, D_MODEL)
    csc_a = mod[batch:batch + 1, D_MODEL:2 * D_MODEL].reshape(1, 1, D_MODEL)

    w_in_bf = w_in.astype(BF16)
    g1 = norm1_g.reshape(1, D_MODEL)
    tile = lambda g: jnp.tile(g.astype(F32), 2)
    qkg = jnp.zeros((8, LANES), F32).at[0].set(tile(diff_q_norm_g) * (DIFF_DH ** -0.5 * LOG2E)).at[1].set(tile(diff_k_norm_g))
    lane = jnp.arange(LANES)
    gmat = (lane[:, None] // DIFF_DH == lane[None, :] // DIFF_DH).astype(BF16)
    x2 = x.reshape(n_tok, D_MODEL)
    tm = min(1024, seq)
    p_lat = _inproj(x2, g1, sh_a, sc_a, w_in_bf, qkg, gmat, _rope_tables(seq), tm, seq // tm)
    p_ctx = _inproj(ctx.reshape(batch * ctx_len, D_MODEL), g1, csh_a, csc_a, w_in_bf, qkg, gmat,
                    _identity_tables(ctx_len), ctx_len, 1)

    lg = jax.nn.log_sigmoid(ret_decay_logit.astype(F32))
    y_ret = _retention(lg, p_lat, p_ctx, ret_norm_g, batch, seq, ctx_len)

    lp = diff_lambda.astype(F32)
    lam = (jnp.exp(jnp.sum(lp[0] * lp[1])) - jnp.exp(jnp.sum(lp[2] * lp[3])) + LAMBDA_INIT).reshape(1)
    y_diff = _diff_attention(lam, p_lat, p_ctx, diff_norm_g, gmat, batch, seq, ctx_len)

    rw = jnp.zeros((D_MODEL, LANES), F32).at[:, :N_EXPERTS].set(router_w)
    rb = jnp.full((1, LANES), NEG_BIG, F32).at[0, :N_EXPERTS].set(router_b)
    xn, hp, idx4, rank4, gate4, cnt = _merge(
        y_ret, y_diff, p_lat, x2, g_a, sh_f, sc_f, norm2_g.reshape(1, D_MODEL),
        w_br_ret.astype(BF16), w_br_diff.astype(BF16), w_out.astype(BF16), rw, rb, seq)

    counts = cnt[0, :N_EXPERTS].astype(I32)
    padded = (counts + MOE_BLK - 1) // MOE_BLK * MOE_BLK
    pad_end = jnp.cumsum(padded)
    pad_start = pad_end - padded
    n_pairs = n_tok * TOP_K
    n_blocks = n_pairs // MOE_BLK + N_EXPERTS
    dest = (pad_start[idx4[:, :TOP_K]] + rank4[:, :TOP_K]).reshape(n_pairs).astype(I32)
    block_e = jnp.minimum(jnp.searchsorted(pad_end, jnp.arange(n_blocks, dtype=I32) * MOE_BLK, side='right'),
                          N_EXPERTS - 1).astype(I32)
    n_used = (pad_end[-1] // MOE_BLK).reshape(1).astype(I32)

    xb = _dispatch(dest, hp, jnp.zeros((n_blocks * MOE_BLK, D_MODEL // 2), U32))
    yb = _experts(block_e, n_used, xb, exp_w1, exp_b1, exp_w2, exp_b2)
    out = _combine(dest, yb, gate4, xn, g_f, seq)
    return out.reshape(batch, seq, D_MODEL)


def kernel(x, c, ctx, c_ctx, norm1_g, norm2_g, w_mod, b_mod, w_in, ret_decay_logit, ret_norm_g, diff_q_norm_g, diff_k_norm_g, diff_lambda, diff_norm_g, w_br_ret, w_br_diff, w_out, router_w, router_b, exp_w1, exp_b1, exp_w2, exp_b2):
    assert norm1_g.shape[0] == 1, "single-layer block"
    return _layer(x, ctx, c, c_ctx, norm1_g[0], norm2_g[0], w_mod[0], b_mod[0], w_in[0], ret_decay_logit[0],
                  ret_norm_g[0], diff_q_norm_g[0], diff_k_norm_g[0], diff_lambda[0], diff_norm_g[0],
                  w_br_ret[0], w_br_diff[0], w_out[0], router_w[0], router_b[0],
                  exp_w1[0], exp_b1[0], exp_w2[0], exp_b2[0])
```

```python
import functools
import math

import numpy as np

import jax
import jax.numpy as jnp
from jax import lax
from jax.experimental import pallas as pl
from jax.experimental.pallas import tpu as pltpu

F32 = jnp.float32
BF16 = jnp.bfloat16
U32 = jnp.uint32
I32 = jnp.int32

D_MODEL = 1024
GRID_W = 64
RET_HEADS = 4
RET_DK = 256
RET_DV = 512
DIFF_DH = 64
DIFF_HEADS = 8
DIFF_DV = 128
N_EXPERTS = 32
TOP_K = 4
D_FF = 1024
SWIGLU_LIMIT = 7.0
SWIGLU_ALPHA = 1.702
ROPE_BASE = 10000.0
EPS = 1e-6
LAMBDA_INIT = 0.8 - 0.6 * math.exp(-0.3 * 0)

IN_COLS = 11264
COL_TILE = 1024
LANES = 128
MOE_BLK = 256
NEG_BIG = -1e30
LOG2E = 1.4426950408889634
SHIFT_SLACK = 1.0 + 2.0 ** -6
MAX_SAFE_SHIFT = 60.0
KV_UNROLL = 4
HIGHEST = lax.Precision.HIGHEST
MIB = 1024 * 1024


def _params(sem, vmem_mib):
    return pltpu.CompilerParams(dimension_semantics=sem, vmem_limit_bytes=vmem_mib * MIB)


def _mod_kernel(c_ref, w_ref, b_ref, o_ref):
    c = c_ref[...]
    s = c * jax.nn.sigmoid(c)
    o_ref[...] = jnp.dot(s, w_ref[...], preferred_element_type=F32, precision=HIGHEST) + b_ref[...]


def _mod(cc, w_mod, b_mod):
    n = w_mod.shape[1]
    tn = 1024
    return pl.pallas_call(
        _mod_kernel,
        out_shape=jax.ShapeDtypeStruct((8, n), F32),
        grid=(n // tn,),
        in_specs=[pl.BlockSpec((8, D_MODEL), lambda j: (0, 0)),
                  pl.BlockSpec((D_MODEL, tn), lambda j: (0, j)),
                  pl.BlockSpec((1, tn), lambda j: (0, j))],
        out_specs=pl.BlockSpec((8, tn), lambda j: (0, j)),
        compiler_params=_params(("arbitrary",), 32),
        name="mod",
    )(cc, w_mod, b_mod.reshape(1, n))


def _inproj_kernel(x_ref, g_ref, sh_ref, sc_ref, w_ref, qkg_ref, gmat_ref,
                   cr_ref, sr_ref, cc_ref, sc2_ref, cd_ref, sa_ref, sb_ref,
                   o_ref, h_scr, acc_scr):
    j = pl.program_id(1)

    @pl.when(j == 0)
    def _():
        xf = x_ref[...]
        ms = jnp.mean(xf * xf, axis=-1, keepdims=True)
        y = xf * lax.rsqrt(ms + EPS) * g_ref[...]
        h_scr[...] = (y * (1.0 + sc_ref[0]) + sh_ref[0]).astype(BF16)

    acc_scr[...] = jnp.dot(h_scr[...], w_ref[...], preferred_element_type=F32)
    is_ret = j <= 1
    is_dqk = (j == 6) | (j == 7)

    @pl.when(is_ret)
    def _():
        scale = jnp.where(j == 0, RET_DK ** -0.5, 1.0).astype(F32)
        for b in range(COL_TILE // LANES):
            xb = acc_scr[:, b * LANES:(b + 1) * LANES]
            cos = cr_ref[...] if b % 2 == 0 else cc_ref[...]
            sin = sr_ref[...] if b % 2 == 0 else sc2_ref[...]
            o = (xb * cos + pltpu.roll(xb, 64, 1) * sin) * scale
            o_ref[:, b * LANES:(b + 1) * LANES] = o.astype(BF16)

    @pl.when(is_dqk)
    def _():
        g = jnp.where(j == 6, qkg_ref[0:1, :], qkg_ref[1:2, :])
        for b in range(COL_TILE // LANES):
            xb = acc_scr[:, b * LANES:(b + 1) * LANES]
            ss = jnp.dot((xb * xb).astype(BF16), gmat_ref[...], preferred_element_type=F32)
            yn = xb * lax.rsqrt(ss * (1.0 / DIFF_DH) + EPS) * g
            o = yn * cd_ref[...] + pltpu.roll(yn, 16, 1) * sa_ref[...] + pltpu.roll(yn, 112, 1) * sb_ref[...]
            o_ref[:, b * LANES:(b + 1) * LANES] = o.astype(BF16)

    @pl.when(jnp.logical_not(is_ret | is_dqk))
    def _():
        o_ref[...] = acc_scr[...].astype(BF16)


def _inproj(x2, g1, sh, sc, w_bf, qkg, gmat, tables, tm, tiles_per_batch):
    n = x2.shape[0]
    nb = sh.shape[0]
    tab_spec = pl.BlockSpec((tm, LANES), lambda i, j: (i % tiles_per_batch, 0))
    mod_spec = pl.BlockSpec((1, 1, D_MODEL), lambda i, j: (jnp.minimum(i // tiles_per_batch, nb - 1), 0, 0))
    return pl.pallas_call(
        _inproj_kernel,
        out_shape=jax.ShapeDtypeStruct((n, IN_COLS), BF16),
        grid=(n // tm, IN_COLS // COL_TILE),
        in_specs=[pl.BlockSpec((tm, D_MODEL), lambda i, j: (i, 0)),
                  pl.BlockSpec((1, D_MODEL), lambda i, j: (0, 0)),
                  mod_spec, mod_spec,
                  pl.BlockSpec((D_MODEL, COL_TILE), lambda i, j: (0, j)),
                  pl.BlockSpec((8, LANES), lambda i, j: (0, 0)),
                  pl.BlockSpec((LANES, LANES), lambda i, j: (0, 0))] + [tab_spec] * 7,
        out_specs=pl.BlockSpec((tm, COL_TILE), lambda i, j: (i, j)),
        scratch_shapes=[pltpu.VMEM((tm, D_MODEL), BF16), pltpu.VMEM((tm, COL_TILE), F32)],
        compiler_params=_params(("arbitrary", "arbitrary"), 48),
        name="inproj",
    )(x2, g1, sh, sc, w_bf, qkg, gmat, *tables)


def _rope_tables(seq):
    n_rows = seq // GRID_W
    f32 = np.float32

    def angles(pos, half):
        inv = f32(ROPE_BASE) ** (-np.arange(half, dtype=f32) / f32(half))
        return (pos.astype(f32)[:, None] * inv[None, :]).astype(np.float64)

    ar, ac = angles(np.arange(n_rows), 64), angles(np.arange(GRID_W), 64)
    br, bc = angles(np.arange(n_rows), 16), angles(np.arange(GRID_W), 16)
    zr, zc = np.zeros_like(br), np.zeros_like(bc)
    cat = lambda parts, reps=1: np.tile(np.concatenate(parts, axis=1), (1, reps)).astype(f32)
    by_row = lambda t: jnp.repeat(jnp.asarray(t), GRID_W, axis=0)
    by_col = lambda t: jnp.tile(jnp.asarray(t), (n_rows, 1))
    cr = by_row(cat([np.cos(ar), np.cos(ar)]))
    sr = by_row(cat([-np.sin(ar), np.sin(ar)]))
    cc = by_col(cat([np.cos(ac), np.cos(ac)]))
    sc = by_col(cat([-np.sin(ac), np.sin(ac)]))
    cd = by_row(cat([np.cos(br), np.cos(br), zr, zr], 2)) + by_col(cat([zc, zc, np.cos(bc), np.cos(bc)], 2))
    sa = by_row(cat([zr, np.sin(br), zr, zr], 2)) + by_col(cat([zc, zc, zc, np.sin(bc)], 2))
    sb = by_row(cat([-np.sin(br), zr, zr, zr], 2)) + by_col(cat([zc, zc, -np.sin(bc), zc], 2))
    return [cr, sr, cc, sc, cd, sa, sb]


def _identity_tables(seq):
    one = jnp.ones((seq, LANES), F32)
    zero = jnp.zeros((seq, LANES), F32)
    return [one, zero, one, zero, one, zero, zero]


def _tn_dot(a, b):
    return lax.dot_general(a, b, (((0,), (0,)), ((), ())), preferred_element_type=F32)


def _nt_dot(a, b):
    return lax.dot_general(a, b, (((1,), (1,)), ((), ())), preferred_element_type=F32)


def _ret_kernel(lg_ref, q_ref, k_ref, v_ref, g_ref, ck_ref, cv_ref, gn_ref, o_ref,
                sf_scr, sb_scr, ob_scr, *, chunk, n_chunks, ctx_len):
    h = pl.program_id(1)
    p = pl.program_id(2)
    c = pl.program_id(3)
    lgf = lg_ref[0, h]
    lgb = lg_ref[1, h]

    def col_iota(n):
        return lax.broadcasted_iota(I32, (n, 1), 0).astype(F32)

    def vexp(s):
        return jnp.exp(jnp.zeros((1, 1), F32) + s)

    @pl.when((p == 0) & (c == 0))
    def _():
        jc = col_iota(ctx_len)
        kc = ck_ref[...].astype(F32)
        vc = cv_ref[...]
        sf_scr[...] = _tn_dot((kc * jnp.exp(lgf * (ctx_len - 1.0 - jc))).astype(BF16), vc)
        sb_scr[...] = _tn_dot((kc * jnp.exp(lgb * jc)).astype(BF16), vc)

    ic = col_iota(chunk)

    @pl.when(p == 0)
    def _():
        cidx = n_chunks - 1 - c
        q = q_ref[...].astype(F32)
        qb = (q * jnp.exp(lgb * (chunk - ic))).astype(BF16)
        ob = jnp.dot(qb, sb_scr[...].astype(BF16), preferred_element_type=F32)
        ob_scr[pl.ds(pl.multiple_of(cidx * chunk, chunk), chunk), :] = ob
        kb = (k_ref[...].astype(F32) * jnp.exp(lgb * ic)).astype(BF16)
        sb_scr[...] = vexp(lgb * chunk) * sb_scr[...] + _tn_dot(kb, v_ref[...])

    @pl.when(p == 1)
    def _():
        q = q_ref[...].astype(F32)
        s = _nt_dot(q_ref[...], k_ref[...])
        ri = lax.broadcasted_iota(I32, (chunk, chunk), 0)
        ci = lax.broadcasted_iota(I32, (chunk, chunk), 1)
        d = (ri - ci).astype(F32)
        mask = jnp.where(d > 0, jnp.exp(lgf * jnp.maximum(d, 0.0)),
                         jnp.where(d < 0, jnp.exp(lgb * jnp.maximum(-d, 0.0)), 2.0))
        a = (s * mask).astype(BF16)
        qf = (q * jnp.exp(lgf * (ic + 1.0))).astype(BF16)
        o = (jnp.dot(a, v_ref[...], preferred_element_type=F32)
             + jnp.dot(qf, sf_scr[...].astype(BF16), preferred_element_type=F32)
             + ob_scr[pl.ds(pl.multiple_of(c * chunk, chunk), chunk), :])
        kf = (k_ref[...].astype(F32) * jnp.exp(lgf * (chunk - 1.0 - ic))).astype(BF16)
        sf_scr[...] = vexp(lgf * chunk) * sf_scr[...] + _tn_dot(kf, v_ref[...])
        ms = jnp.mean(o * o, axis=-1, keepdims=True)
        y = o * lax.rsqrt(ms + EPS) * gn_ref[0]
        gt = g_ref[...].astype(F32)
        o_ref[...] = (y * (gt * jax.nn.sigmoid(gt))).astype(BF16)


def _retention(lg, p_lat, p_ctx, ret_norm_g, batch, seq, ctx_len):
    chunk = min(256, seq)
    nc = seq // chunk
    kern = functools.partial(_ret_kernel, chunk=chunk, n_chunks=nc, ctx_len=ctx_len)

    def rows(b, p, c):
        return b * nc + jnp.where(p == 0, nc - 1 - c, c)

    def rows_fwd(b, p, c):
        return b * nc + jnp.where(p == 0, 0, c)

    return pl.pallas_call(
        kern,
        out_shape=jax.ShapeDtypeStruct((batch * seq, RET_HEADS * RET_DV), BF16),
        grid=(batch, RET_HEADS, 2, nc),
        in_specs=[pl.BlockSpec(memory_space=pltpu.SMEM),
                  pl.BlockSpec((chunk, RET_DK), lambda b, h, p, c: (rows(b, p, c), h)),
                  pl.BlockSpec((chunk, RET_DK), lambda b, h, p, c: (rows(b, p, c), 4 + h)),
                  pl.BlockSpec((chunk, RET_DV), lambda b, h, p, c: (rows(b, p, c), 4 + h)),
                  pl.BlockSpec((chunk, RET_DV), lambda b, h, p, c: (rows_fwd(b, p, c), 8 + h)),
                  pl.BlockSpec((ctx_len, RET_DK), lambda b, h, p, c: (b, 4 + h)),
                  pl.BlockSpec((ctx_len, RET_DV), lambda b, h, p, c: (b, 4 + h)),
                  pl.BlockSpec((1, 1, RET_DV), lambda b, h, p, c: (h, 0, 0))],
        out_specs=pl.BlockSpec((chunk, RET_DV), lambda b, h, p, c: (rows_fwd(b, p, c), h)),
        scratch_shapes=[pltpu.VMEM((RET_DK, RET_DV), F32), pltpu.VMEM((RET_DK, RET_DV), F32),
                        pltpu.VMEM((seq, RET_DV), F32)],
        compiler_params=_params(("arbitrary",) * 4, 48),
        name="ret",
    )(lg, p_lat, p_lat, p_lat, p_lat, p_ctx, p_ctx, ret_norm_g.reshape(RET_HEADS, 1, RET_DV))


def _dattn_kernel(lam_ref, q_ref, ck_ref, cv_ref, k_ref, v_ref, gn_ref, gmat_ref, o_ref,
                  qq_scr, kmax_scr, mp_scr, acc_scr, *, tq, tk, n_kv):
    rows = 2 * tq
    qi = pl.program_id(2)

    def sq_norms(t):
        tf = t.astype(F32)
        return jnp.dot((tf * tf).astype(BF16), gmat_ref[...], preferred_element_type=F32)

    def kv_slice(ref, c):
        return ref[pl.ds(pl.multiple_of(c * tk, tk), tk), :]

    @pl.when(qi == 0)
    def _():
        kmax_scr[...] = jnp.max(sq_norms(ck_ref[...]), axis=0, keepdims=True)

        def body(c, carry):
            kmax_scr[...] = jnp.maximum(kmax_scr[...], jnp.max(sq_norms(kv_slice(k_ref, c)), axis=0, keepdims=True))
            return carry

        lax.fori_loop(0, n_kv, body, 0)

    q = q_ref[...]
    lane = lax.broadcasted_iota(I32, (tq, LANES), 1)
    z = jnp.zeros_like(q)
    qq_scr[0:tq, 0:LANES] = jnp.where(lane < DIFF_DH, q, z)
    qq_scr[tq:rows, 0:LANES] = jnp.where(lane >= DIFF_DH, q, z)

    def set_shift(shift):
        neg = -shift
        hi = neg.astype(BF16)
        r1 = neg - hi.astype(F32)
        mid = r1.astype(BF16)
        lo = (r1 - mid.astype(F32)).astype(BF16)
        lane_r = lax.broadcasted_iota(I32, (rows, LANES), 1)
        pieces = jnp.where(lane_r == 0, hi.astype(F32), jnp.where(lane_r == 1, mid.astype(F32),
                           jnp.where(lane_r == 2, lo.astype(F32), 0.0)))
        qq_scr[:, LANES:2 * LANES] = pieces.astype(BF16)

    bound = jnp.sqrt(sq_norms(q) * kmax_scr[...]) * SHIFT_SLACK
    b1 = jnp.max(jnp.where(lane < DIFF_DH, bound, 0.0), axis=1, keepdims=True)
    b2 = jnp.max(jnp.where(lane >= DIFF_DH, bound, 0.0), axis=1, keepdims=True)
    set_shift(jnp.concatenate([b1, b2], axis=0))

    @pl.when(jnp.max(bound) > MAX_SAFE_SHIFT)
    def _():
        def tile_max(k):
            s = _nt_dot(qq_scr[:, 0:LANES], k)
            mp = mp_scr[...]
            for cb in range(k.shape[0] // LANES):
                mp = jnp.maximum(mp, s[:, cb * LANES:(cb + 1) * LANES])
            mp_scr[...] = mp

        mp_scr[...] = jnp.full(mp_scr.shape, NEG_BIG, F32)
        tile_max(ck_ref[...])

        def body(c, carry):
            tile_max(kv_slice(k_ref, c))
            return carry

        lax.fori_loop(0, n_kv, body, 0)
        set_shift(jnp.max(mp_scr[...], axis=1, keepdims=True))

    def lane_const(n, hot):
        return jnp.where(lax.broadcasted_iota(I32, (n, LANES), 1) < hot, 1.0, 0.0).astype(BF16)

    acc_scr[...] = jnp.zeros(acc_scr.shape, F32)

    def tile_acc(k, v):
        n = k.shape[0]
        s = _nt_dot(qq_scr[...], jnp.concatenate([k, lane_const(n, 3)], axis=1))
        p = jnp.exp2(s).astype(BF16)
        acc_scr[...] += jnp.dot(p, jnp.concatenate([v, lane_const(n, 1)], axis=1), preferred_element_type=F32)

    tile_acc(ck_ref[...], cv_ref[...])

    def body_acc(c, carry):
        for u in range(KV_UNROLL):
            tile_acc(kv_slice(k_ref, c * KV_UNROLL + u), kv_slice(v_ref, c * KV_UNROLL + u))
        return carry

    lax.fori_loop(0, n_kv // KV_UNROLL, body_acc, 0)

    o = acc_scr[:, 0:LANES] / acc_scr[:, LANES:LANES + 1]
    d = o[0:tq, :] - lam_ref[0] * o[tq:rows, :]
    ms = jnp.mean(d * d, axis=-1, keepdims=True)
    y = d * lax.rsqrt(ms + EPS) * gn_ref[...] * (1.0 - LAMBDA_INIT)
    o_ref[...] = y.astype(BF16)


def _diff_attention(lam, p_lat, p_ctx, diff_norm_g, gmat, batch, seq, ctx_len):
    tq = min(256, seq)
    tk = min(512, seq // KV_UNROLL)
    nq, nk = seq // tq, seq // tk
    assert ctx_len <= tk and ctx_len % LANES == 0 and nk % KV_UNROLL == 0
    kern = functools.partial(_dattn_kernel, tq=tq, tk=tk, n_kv=nk)
    return pl.pallas_call(
        kern,
        out_shape=jax.ShapeDtypeStruct((batch * seq, DIFF_HEADS * DIFF_DV), BF16),
        grid=(batch, DIFF_HEADS, nq),
        in_specs=[pl.BlockSpec(memory_space=pltpu.SMEM),
                  pl.BlockSpec((tq, LANES), lambda b, h, qi: (b * nq + qi, 48 + h)),
                  pl.BlockSpec((ctx_len, LANES), lambda b, h, qi: (b, 56 + h)),
                  pl.BlockSpec((ctx_len, LANES), lambda b, h, qi: (b, 64 + h)),
                  pl.BlockSpec((seq, LANES), lambda b, h, qi: (b, 56 + h)),
                  pl.BlockSpec((seq, LANES), lambda b, h, qi: (b, 64 + h)),
                  pl.BlockSpec((1, LANES), lambda b, h, qi: (0, 0)),
                  pl.BlockSpec((LANES, LANES), lambda b, h, qi: (0, 0))],
        out_specs=pl.BlockSpec((tq, LANES), lambda b, h, qi: (b * nq + qi, h)),
        scratch_shapes=[pltpu.VMEM((2 * tq, 2 * LANES), BF16), pltpu.VMEM((1, LANES), F32),
                        pltpu.VMEM((2 * tq, LANES), F32), pltpu.VMEM((2 * tq, 2 * LANES), F32)],
        compiler_params=_params(("arbitrary",) * 3, 48),
        name="dattn",
    )(lam, p_lat, p_ctx, p_ctx, p_lat, p_lat, diff_norm_g.reshape(1, DIFF_DV), gmat)


def _merge_kernel(yr_ref, yd_ref, ga_ref, gb_ref, x_ref, gatea_ref, shf_ref, scf_ref, n2_ref,
                  wr_ref, wd_ref, wo_ref, rw_ref, rb_ref,
                  xn_ref, hp_ref, meta_ref, gate_ref, cnt_ref, run_scr, *, tm):
    i = pl.program_id(0)

    @pl.when(i == 0)
    def _():
        run_scr[...] = jnp.zeros(run_scr.shape, F32)

    yr = jnp.dot(yr_ref[...], wr_ref[...], preferred_element_type=F32)
    yd = jnp.dot(yd_ref[...], wd_ref[...], preferred_element_type=F32)
    m = (jax.nn.sigmoid(ga_ref[...].astype(F32)) * yr + jax.nn.sigmoid(gb_ref[...].astype(F32)) * yd)
    z = jnp.dot(m.astype(BF16), wo_ref[...], preferred_element_type=F32)
    xn = x_ref[...] + gatea_ref[0] * z
    xn_ref[...] = xn

    ms = jnp.mean(xn * xn, axis=-1, keepdims=True)
    h2 = xn * lax.rsqrt(ms + EPS) * n2_ref[...]
    h2 = h2 * (1.0 + scf_ref[0]) + shf_ref[0]
    bits = pltpu.bitcast(h2.astype(BF16).astype(F32), U32)
    half = D_MODEL // 2
    hp_ref[...] = (bits[:, :half] >> 16) | (bits[:, half:] & jnp.uint32(0xFFFF0000))

    logits = jnp.dot(h2, rw_ref[...], preferred_element_type=F32, precision=HIGHEST) + rb_ref[...]
    lane = lax.broadcasted_iota(I32, (tm, LANES), 1)
    lanef = lane.astype(F32)
    work = logits
    vals, idxs = [], []
    for _ in range(TOP_K):
        mk = jnp.max(work, axis=1, keepdims=True)
        ik = jnp.min(jnp.where(work == mk, lanef, float(LANES)), axis=1, keepdims=True)
        vals.append(mk)
        idxs.append(ik)
        work = jnp.where(lanef == ik, -jnp.inf, work)
    ex = [jnp.exp(v - vals[0]) for v in vals]
    den = ex[0] + ex[1] + ex[2] + ex[3]

    onehot = jnp.zeros((tm, LANES), F32)
    for ik in idxs:
        onehot = onehot + jnp.where(lanef == ik, 1.0, 0.0)
    ri = lax.broadcasted_iota(I32, (tm, tm), 0)
    ci = lax.broadcasted_iota(I32, (tm, tm), 1)
    tri = jnp.where(ri > ci, 1.0, 0.0).astype(BF16)
    base = run_scr[0:1, :] + jnp.dot(tri, onehot.astype(BF16), preferred_element_type=F32)
    run_scr[...] = run_scr[...] + jnp.sum(onehot, axis=0, keepdims=True)

    meta = jnp.zeros((tm, LANES), F32)
    gate_out = jnp.zeros((tm, LANES), F32)
    for k in range(TOP_K):
        rk = jnp.sum(jnp.where(lanef == idxs[k], base, 0.0), axis=1, keepdims=True)
        meta = jnp.where(lane == k, idxs[k], jnp.where(lane == TOP_K + k, rk, meta))
        gate_out = jnp.where(lane == k, ex[k] / den, gate_out)
    meta_ref[...] = meta.T[0:2 * TOP_K, :]
    gate_ref[...] = gate_out
    cnt_ref[...] = run_scr[...]


def _merge(y_ret, y_diff, p_lat, x2, g_a, sh_f, sc_f, norm2_g, w_r, w_d, w_o, rw, rb, seq):
    n = x2.shape[0]
    tm = min(512, seq)
    tpb = seq // tm
    kern = functools.partial(_merge_kernel, tm=tm)
    mod_spec = pl.BlockSpec((1, 1, D_MODEL), lambda i: (i // tpb, 0, 0))
    const = lambda shape: pl.BlockSpec(shape, lambda i: (0,) * len(shape))
    tok = lambda w: pl.BlockSpec((tm, w), lambda i: (i, 0))
    return pl.pallas_call(
        kern,
        out_shape=(jax.ShapeDtypeStruct((n, D_MODEL), F32),
                   jax.ShapeDtypeStruct((n, D_MODEL // 2), U32),
                   jax.ShapeDtypeStruct((2 * TOP_K, n), F32),
                   jax.ShapeDtypeStruct((n, LANES), F32),
                   jax.ShapeDtypeStruct((8, LANES), F32)),
        grid=(n // tm,),
        in_specs=[tok(RET_HEADS * RET_DV), tok(D_MODEL),
                  pl.BlockSpec((tm, COL_TILE), lambda i: (i, 9)),
                  pl.BlockSpec((tm, COL_TILE), lambda i: (i, 10)),
                  tok(D_MODEL), mod_spec, mod_spec, mod_spec, const((1, D_MODEL)),
                  const((RET_HEADS * RET_DV, D_MODEL)), const((D_MODEL, D_MODEL)), const((D_MODEL, D_MODEL)),
                  const((D_MODEL, LANES)), const((1, LANES))],
        out_specs=(tok(D_MODEL), tok(D_MODEL // 2), pl.BlockSpec((2 * TOP_K, tm), lambda i: (0, i)),
                   tok(LANES), const((8, LANES))),
        scratch_shapes=[pltpu.VMEM((8, LANES), F32)],
        compiler_params=_params(("arbitrary",), 56),
        name="merge",
    )(y_ret, y_diff, p_lat, p_lat, x2, g_a, sh_f, sc_f, norm2_g, w_r, w_d, w_o, rw, rb)


def _dispatch_kernel(dest_ref, hp_ref, xb_in_ref, xb_ref, sem, *, tm, n_tok):
    del xb_in_ref
    i = pl.program_id(0)

    def body(r, carry):
        tok = i * tm + r
        for k in range(TOP_K):
            d = dest_ref[k * n_tok + tok]
            pltpu.make_async_copy(hp_ref.at[pl.ds(r, 1), :], xb_ref.at[pl.ds(d, 1), :], sem).start()
        return carry

    lax.fori_loop(0, tm, body, 0)
    for _ in range(TOP_K):
        pltpu.make_async_copy(hp_ref, xb_ref.at[pl.ds(0, tm), :], sem).wait()


def _dispatch(dest, hp, xb_zero):
    n = hp.shape[0]
    tm = min(512, n)
    kern = functools.partial(_dispatch_kernel, tm=tm, n_tok=n)
    return pl.pallas_call(
        kern,
        out_shape=jax.ShapeDtypeStruct(xb_zero.shape, U32),
        grid_spec=pltpu.PrefetchScalarGridSpec(
            num_scalar_prefetch=1,
            grid=(n // tm,),
            in_specs=[pl.BlockSpec((tm, D_MODEL // 2), lambda i, d: (i, 0)),
                      pl.BlockSpec(memory_space=pl.ANY)],
            out_specs=pl.BlockSpec(memory_space=pl.ANY),
            scratch_shapes=[pltpu.SemaphoreType.DMA(())]),
        input_output_aliases={2: 0},
        compiler_params=_params(("arbitrary",), 32),
        name="dispatch",
    )(dest, hp, xb_zero)


def _expert_kernel(be_ref, nb_ref, xb_ref, w1_ref, b1_ref, w2_ref, b2_ref, y_ref, w1b_scr, w2b_scr):
    i = pl.program_id(0)
    e = be_ref[i]
    prev = be_ref[jnp.maximum(i - 1, 0)]

    @pl.when((i == 0) | (e != prev))
    def _():
        w1b_scr[...] = w1_ref[0].astype(BF16)
        w2b_scr[...] = w2_ref[0].astype(BF16)

    @pl.when(i < nb_ref[0])
    def _():
        xu = xb_ref[...]
        x_lo = pltpu.bitcast(xu << 16, F32).astype(BF16)
        x_hi = pltpu.bitcast(xu & jnp.uint32(0xFFFF0000), F32).astype(BF16)
        half = D_MODEL // 2
        hh = (jnp.dot(x_lo, w1b_scr[0:half, :], preferred_element_type=F32)
              + jnp.dot(x_hi, w1b_scr[half:, :], preferred_element_type=F32) + b1_ref[0])
        glu = jnp.minimum(hh[:, :D_FF], SWIGLU_LIMIT)
        lin = jnp.clip(hh[:, D_FF:], -SWIGLU_LIMIT, SWIGLU_LIMIT)
        act = glu * jax.nn.sigmoid(SWIGLU_ALPHA * glu) * (lin + 1.0)
        y_ref[...] = jnp.dot(act.astype(BF16), w2b_scr[...], preferred_element_type=F32) + b2_ref[0]

    @pl.when(i >= nb_ref[0])
    def _():
        y_ref[...] = jnp.zeros(y_ref.shape, F32)


def _experts(block_e, n_used, xb, w1, b1, w2, b2):
    rows = xb.shape[0]
    n_blocks = rows // MOE_BLK
    return pl.pallas_call(
        _expert_kernel,
        out_shape=jax.ShapeDtypeStruct((rows, D_MODEL), F32),
        grid_spec=pltpu.PrefetchScalarGridSpec(
            num_scalar_prefetch=2,
            grid=(n_blocks,),
            in_specs=[pl.BlockSpec((MOE_BLK, D_MODEL // 2), lambda i, be, nb: (i, 0)),
                      pl.BlockSpec((1, D_MODEL, 2 * D_FF), lambda i, be, nb: (be[i], 0, 0)),
                      pl.BlockSpec((1, 1, 2 * D_FF), lambda i, be, nb: (be[i], 0, 0)),
                      pl.BlockSpec((1, D_FF, D_MODEL), lambda i, be, nb: (be[i], 0, 0)),
                      pl.BlockSpec((1, 1, D_MODEL), lambda i, be, nb: (be[i], 0, 0))],
            out_specs=pl.BlockSpec((MOE_BLK, D_MODEL), lambda i, be, nb: (i, 0)),
            scratch_shapes=[pltpu.VMEM((D_MODEL, 2 * D_FF), BF16), pltpu.VMEM((D_FF, D_MODEL), BF16)]),
        compiler_params=_params(("arbitrary",), 56),
        name="expert",
    )(block_e, n_used, xb, w1, b1.reshape(N_EXPERTS, 1, 2 * D_FF), w2, b2.reshape(N_EXPERTS, 1, D_MODEL))


def _combine_kernel(dest_ref, yb_ref, gate_ref, xn_ref, gf_ref, o_ref, buf, sem, *, tm, n_tok):
    i = pl.program_id(0)

    def body(r, carry):
        tok = i * tm + r
        for k in range(TOP_K):
            d = dest_ref[k * n_tok + tok]
            pltpu.make_async_copy(yb_ref.at[pl.ds(d, 1), :], buf.at[k, pl.ds(r, 1), :], sem).start()
        return carry

    lax.fori_loop(0, tm, body, 0)
    for k in range(TOP_K):
        pltpu.make_async_copy(yb_ref.at[pl.ds(0, tm), :], buf.at[k], sem).wait()
    g = gate_ref[...]
    y = g[:, 0:1] * buf[0]
    for k in range(1, TOP_K):
        y = y + g[:, k:k + 1] * buf[k]
    o_ref[...] = xn_ref[...] + gf_ref[0] * y


def _combine(dest, yb, gate, xn, g_f, seq):
    n = xn.shape[0]
    tm = min(256, seq)
    tpb = seq // tm
    kern = functools.partial(_combine_kernel, tm=tm, n_tok=n)
    return pl.pallas_call(
        kern,
        out_shape=jax.ShapeDtypeStruct((n, D_MODEL), F32),
        grid_spec=pltpu.PrefetchScalarGridSpec(
            num_scalar_prefetch=1,
            grid=(n // tm,),
            in_specs=[pl.BlockSpec(memory_space=pl.ANY),
                      pl.BlockSpec((tm, LANES), lambda i, d: (i, 0)),
                      pl.BlockSpec((tm, D_MODEL), lambda i, d: (i, 0)),
                      pl.BlockSpec((1, 1, D_MODEL), lambda i, d: (i // tpb, 0, 0))],
            out_specs=pl.BlockSpec((tm, D_MODEL), lambda i, d: (i, 0)),
            scratch_shapes=[pltpu.VMEM((TOP_K, tm, D_MODEL), F32), pltpu.SemaphoreType.DMA(())]),
        compiler_params=_params(("arbitrary",), 32),
        name="combine",
    )(dest, yb, gate, xn, g_f)


def _layer(x, ctx, c, c_ctx, norm1_g, norm2_g, w_mod, b_mod, w_in, ret_decay_logit, ret_norm_g,
           diff_q_norm_g, diff_k_norm_g, diff_lambda, diff_norm_g, w_br_ret, w_br_diff, w_out,
           router_w, router_b, exp_w1, exp_b1, exp_w2, exp_b2):
    batch, seq, d = x.shape
    ctx_len = ctx.shape[1]
    assert d == D_MODEL and seq % GRID_W == 0 and batch + 1 <= 8
    n_tok = batch * seq

    cc = jnp.zeros((8, D_MODEL), F32).at[:batch].set(c).at[batch].set(c_ctx)
    mod = _mod(cc, w_mod, b_mod)
    sh_a, sc_a, g_a, sh_f, sc_f, g_f = [mod[:batch, i * D_MODEL:(i + 1) * D_MODEL].reshape(batch, 1, D_MODEL)
                                         for i in range(6)]
    csh_a = mod[batch:batch + 1, 0:D_MODEL].reshape(1, 1, D_MODEL)
    csc_a = mod[batch:batch + 1, D_MODEL:2 * D_MODEL].reshape(1, 1, D_MODEL)

    w_in_bf = w_in.astype(BF16)
    g1 = norm1_g.reshape(1, D_MODEL)
    tile = lambda g: jnp.tile(g.astype(F32), 2)
    qkg = jnp.zeros((8, LANES), F32).at[0].set(tile(diff_q_norm_g) * (DIFF_DH ** -0.5 * LOG2E)).at[1].set(tile(diff_k_norm_g))
    lane = jnp.arange(LANES)
    gmat = (lane[:, None] // DIFF_DH == lane[None, :] // DIFF_DH).astype(BF16)
    x2 = x.reshape(n_tok, D_MODEL)
    tm = min(1024, seq)
    p_lat = _inproj(x2, g1, sh_a, sc_a, w_in_bf, qkg, gmat, _rope_tables(seq), tm, seq // tm)
    p_ctx = _inproj(ctx.reshape(batch * ctx_len, D_MODEL), g1, csh_a, csc_a, w_in_bf, qkg, gmat,
                    _identity_tables(ctx_len), ctx_len, 1)

    lg = jax.nn.log_sigmoid(ret_decay_logit.astype(F32))
    y_ret = _retention(lg, p_lat, p_ctx, ret_norm_g, batch, seq, ctx_len)

    lp = diff_lambda.astype(F32)
    lam = (jnp.exp(jnp.sum(lp[0] * lp[1])) - jnp.exp(jnp.sum(lp[2] * lp[3])) + LAMBDA_INIT).reshape(1)
    y_diff = _diff_attention(lam, p_lat, p_ctx, diff_norm_g, gmat, batch, seq, ctx_len)

    rw = jnp.zeros((D_MODEL, LANES), F32).at[:, :N_EXPERTS].set(router_w)
    rb = jnp.full((1, LANES), NEG_BIG, F32).at[0, :N_EXPERTS].set(router_b)
    xn, hp, meta, gate4, cnt = _merge(
        y_ret, y_diff, p_lat, x2, g_a, sh_f, sc_f, norm2_g.reshape(1, D_MODEL),
        w_br_ret.astype(BF16), w_br_diff.astype(BF16), w_out.astype(BF16), rw, rb, seq)

    counts = cnt[0, :N_EXPERTS].astype(I32)
    padded = (counts + MOE_BLK - 1) // MOE_BLK * MOE_BLK
    pad_end = jnp.cumsum(padded)
    pad_start = pad_end - padded
    n_pairs = n_tok * TOP_K
    n_blocks = n_pairs // MOE_BLK + N_EXPERTS
    meta_i = meta.astype(I32)
    dest = (pad_start[meta_i[:TOP_K]] + meta_i[TOP_K:]).reshape(n_pairs)
    block_start = jnp.arange(n_blocks, dtype=I32) * MOE_BLK
    block_e = jnp.minimum(jnp.sum((pad_end[None, :] <= block_start[:, None]).astype(I32), axis=1), N_EXPERTS - 1)
    n_used = (pad_end[-1] // MOE_BLK).reshape(1).astype(I32)

    xb = _dispatch(dest, hp, jnp.zeros((n_blocks * MOE_BLK, D_MODEL // 2), U32))
    yb = _experts(block_e, n_used, xb, exp_w1, exp_b1, exp_w2, exp_b2)
    out = _combine(dest, yb, gate4, xn, g_f, seq)
    return out.reshape(batch, seq, D_MODEL)


def kernel(x, c, ctx, c_ctx, norm1_g, norm2_g, w_mod, b_mod, w_in, ret_decay_logit, ret_norm_g, diff_q_norm_g, diff_k_norm_g, diff_lambda, diff_norm_g, w_br_ret, w_br_diff, w_out, router_w, router_b, exp_w1, exp_b1, exp_w2, exp_b2):
    assert norm1_g.shape[0] == 1, "single-layer block"
    return _layer(x, ctx, c, c_ctx, norm1_g[0], norm2_g[0], w_mod[0], b_mod[0], w_in[0], ret_decay_logit[0],
                  ret_norm_g[0], diff_q_norm_g[0], diff_k_norm_g[0], diff_lambda[0], diff_norm_g[0],
                  w_br_ret[0], w_br_diff[0], w_out[0], router_w[0], router_b[0],
                  exp_w1[0], exp_b1[0], exp_w2[0], exp_b2[0])
```

```python
import functools
import math

import numpy as np

import jax
import jax.numpy as jnp
from jax import lax
from jax.experimental import pallas as pl
from jax.experimental.pallas import tpu as pltpu

F32 = jnp.float32
BF16 = jnp.bfloat16
U32 = jnp.uint32
I32 = jnp.int32

D_MODEL = 1024
GRID_W = 64
RET_HEADS = 4
RET_DK = 256
RET_DV = 512
DIFF_DH = 64
DIFF_HEADS = 8
DIFF_DV = 128
N_EXPERTS = 32
TOP_K = 4
D_FF = 1024
SWIGLU_LIMIT = 7.0
SWIGLU_ALPHA = 1.702
ROPE_BASE = 10000.0
EPS = 1e-6
LAMBDA_INIT = 0.8 - 0.6 * math.exp(-0.3 * 0)

IN_COLS = 11264
COL_TILE = 1024
LANES = 128
MOE_BLK = 256
NEG_BIG = -1e30
LOG2E = 1.4426950408889634
SHIFT_SLACK = 1.0 + 2.0 ** -6
MAX_SAFE_SHIFT = 60.0
RET_SUB = 4
KV_UNROLL = 8
HIGHEST = lax.Precision.HIGHEST
MIB = 1024 * 1024


def _params(sem, vmem_mib):
    return pltpu.CompilerParams(dimension_semantics=sem, vmem_limit_bytes=vmem_mib * MIB)


def _mod_kernel(c_ref, w_ref, b_ref, o_ref):
    c = c_ref[...]
    s = c * jax.nn.sigmoid(c)
    o_ref[...] = jnp.dot(s, w_ref[...], preferred_element_type=F32, precision=HIGHEST) + b_ref[...]


def _mod(cc, w_mod, b_mod):
    n = w_mod.shape[1]
    tn = 1024
    return pl.pallas_call(
        _mod_kernel,
        out_shape=jax.ShapeDtypeStruct((8, n), F32),
        grid=(n // tn,),
        in_specs=[pl.BlockSpec((8, D_MODEL), lambda j: (0, 0)),
                  pl.BlockSpec((D_MODEL, tn), lambda j: (0, j)),
                  pl.BlockSpec((1, tn), lambda j: (0, j))],
        out_specs=pl.BlockSpec((8, tn), lambda j: (0, j)),
        compiler_params=_params(("arbitrary",), 32),
        name="mod",
    )(cc, w_mod, b_mod.reshape(1, n))


def _inproj_kernel(x_ref, g_ref, sh_ref, sc_ref, w_ref, qkg_ref, gmat_ref,
                   cr_ref, sr_ref, cc_ref, sc2_ref, cd_ref, sa_ref, sb_ref,
                   o_ref, h_scr, acc_scr):
    j = pl.program_id(1)

    @pl.when(j == 0)
    def _():
        xf = x_ref[...]
        ms = jnp.mean(xf * xf, axis=-1, keepdims=True)
        y = xf * lax.rsqrt(ms + EPS) * g_ref[...]
        h_scr[...] = (y * (1.0 + sc_ref[0]) + sh_ref[0]).astype(BF16)

    acc_scr[...] = jnp.dot(h_scr[...], w_ref[...], preferred_element_type=F32)
    is_ret = j <= 1
    is_dqk = (j == 6) | (j == 7)

    @pl.when(is_ret)
    def _():
        scale = jnp.where(j == 0, RET_DK ** -0.5, 1.0).astype(F32)
        for b in range(COL_TILE // LANES):
            xb = acc_scr[:, b * LANES:(b + 1) * LANES]
            cos = cr_ref[...] if b % 2 == 0 else cc_ref[...]
            sin = sr_ref[...] if b % 2 == 0 else sc2_ref[...]
            o = (xb * cos + pltpu.roll(xb, 64, 1) * sin) * scale
            o_ref[:, b * LANES:(b + 1) * LANES] = o.astype(BF16)

    @pl.when(is_dqk)
    def _():
        g = jnp.where(j == 6, qkg_ref[0:1, :], qkg_ref[1:2, :])
        for b in range(COL_TILE // LANES):
            xb = acc_scr[:, b * LANES:(b + 1) * LANES]
            ss = jnp.dot((xb * xb).astype(BF16), gmat_ref[...], preferred_element_type=F32)
            yn = xb * lax.rsqrt(ss * (1.0 / DIFF_DH) + EPS) * g
            o = yn * cd_ref[...] + pltpu.roll(yn, 16, 1) * sa_ref[...] + pltpu.roll(yn, 112, 1) * sb_ref[...]
            o_ref[:, b * LANES:(b + 1) * LANES] = o.astype(BF16)

    @pl.when(jnp.logical_not(is_ret | is_dqk))
    def _():
        o_ref[...] = acc_scr[...].astype(BF16)


def _inproj(x2, g1, sh, sc, w_bf, qkg, gmat, tables, tm, tiles_per_batch):
    n = x2.shape[0]
    nb = sh.shape[0]
    tab_spec = pl.BlockSpec((tm, LANES), lambda i, j: (i % tiles_per_batch, 0))
    mod_spec = pl.BlockSpec((1, 1, D_MODEL), lambda i, j: (jnp.minimum(i // tiles_per_batch, nb - 1), 0, 0))
    return pl.pallas_call(
        _inproj_kernel,
        out_shape=jax.ShapeDtypeStruct((n, IN_COLS), BF16),
        grid=(n // tm, IN_COLS // COL_TILE),
        in_specs=[pl.BlockSpec((tm, D_MODEL), lambda i, j: (i, 0)),
                  pl.BlockSpec((1, D_MODEL), lambda i, j: (0, 0)),
                  mod_spec, mod_spec,
                  pl.BlockSpec((D_MODEL, COL_TILE), lambda i, j: (0, j)),
                  pl.BlockSpec((8, LANES), lambda i, j: (0, 0)),
                  pl.BlockSpec((LANES, LANES), lambda i, j: (0, 0))] + [tab_spec] * 7,
        out_specs=pl.BlockSpec((tm, COL_TILE), lambda i, j: (i, j)),
        scratch_shapes=[pltpu.VMEM((tm, D_MODEL), BF16), pltpu.VMEM((tm, COL_TILE), F32)],
        compiler_params=_params(("arbitrary", "arbitrary"), 48),
        name="inproj",
    )(x2, g1, sh, sc, w_bf, qkg, gmat, *tables)


def _rope_tables(seq):
    n_rows = seq // GRID_W
    f32 = np.float32

    def angles(pos, half):
        inv = f32(ROPE_BASE) ** (-np.arange(half, dtype=f32) / f32(half))
        return (pos.astype(f32)[:, None] * inv[None, :]).astype(np.float64)

    ar, ac = angles(np.arange(n_rows), 64), angles(np.arange(GRID_W), 64)
    br, bc = angles(np.arange(n_rows), 16), angles(np.arange(GRID_W), 16)
    zr, zc = np.zeros_like(br), np.zeros_like(bc)
    cat = lambda parts, reps=1: np.tile(np.concatenate(parts, axis=1), (1, reps)).astype(f32)
    by_row = lambda t: jnp.repeat(jnp.asarray(t), GRID_W, axis=0)
    by_col = lambda t: jnp.tile(jnp.asarray(t), (n_rows, 1))
    cr = by_row(cat([np.cos(ar), np.cos(ar)]))
    sr = by_row(cat([-np.sin(ar), np.sin(ar)]))
    cc = by_col(cat([np.cos(ac), np.cos(ac)]))
    sc = by_col(cat([-np.sin(ac), np.sin(ac)]))
    cd = by_row(cat([np.cos(br), np.cos(br), zr, zr], 2)) + by_col(cat([zc, zc, np.cos(bc), np.cos(bc)], 2))
    sa = by_row(cat([zr, np.sin(br), zr, zr], 2)) + by_col(cat([zc, zc, zc, np.sin(bc)], 2))
    sb = by_row(cat([-np.sin(br), zr, zr, zr], 2)) + by_col(cat([zc, zc, -np.sin(bc), zc], 2))
    return [cr, sr, cc, sc, cd, sa, sb]


def _identity_tables(seq):
    one = jnp.ones((seq, LANES), F32)
    zero = jnp.zeros((seq, LANES), F32)
    return [one, zero, one, zero, one, zero, zero]


def _tn_dot(a, b):
    return lax.dot_general(a, b, (((0,), (0,)), ((), ())), preferred_element_type=F32)


def _nt_dot(a, b):
    return lax.dot_general(a, b, (((1,), (1,)), ((), ())), preferred_element_type=F32)


def _ret_kernel(lg_ref, q_ref, k_ref, v_ref, g_ref, ck_ref, cv_ref, gn_ref, o_ref,
                sf_scr, sb_scr, ob_scr, *, chunk, sub, n_blocks, ctx_len):
    h = pl.program_id(1)
    p = pl.program_id(2)
    c = pl.program_id(3)
    lgf = lg_ref[0, h]
    lgb = lg_ref[1, h]

    def col_iota(n):
        return lax.broadcasted_iota(I32, (n, 1), 0).astype(F32)

    def vexp(s):
        return jnp.exp(jnp.zeros((1, 1), F32) + s)

    @pl.when((p == 0) & (c == 0))
    def _():
        jc = col_iota(ctx_len)
        kc = ck_ref[...].astype(F32)
        vc = cv_ref[...]
        sf_scr[...] = _tn_dot((kc * jnp.exp(lgf * (ctx_len - 1.0 - jc))).astype(BF16), vc)
        sb_scr[...] = _tn_dot((kc * jnp.exp(lgb * jc)).astype(BF16), vc)

    ic = col_iota(chunk)

    @pl.when(p == 0)
    def _():
        q_decay = jnp.exp(lgb * (chunk - ic))
        k_decay = jnp.exp(lgb * ic)
        s_decay = vexp(lgb * chunk)
        for j in reversed(range(sub)):
            loc = pl.ds(j * chunk, chunk)
            glob = pl.ds(pl.multiple_of(((n_blocks - 1 - c) * sub + j) * chunk, chunk), chunk)
            qb = (q_ref[loc, :].astype(F32) * q_decay).astype(BF16)
            ob_scr[glob, :] = jnp.dot(qb, sb_scr[...].astype(BF16), preferred_element_type=F32)
            kb = (k_ref[loc, :].astype(F32) * k_decay).astype(BF16)
            sb_scr[...] = s_decay * sb_scr[...] + _tn_dot(kb, v_ref[loc, :])

    @pl.when(p == 1)
    def _():
        ri = lax.broadcasted_iota(I32, (chunk, chunk), 0)
        ci = lax.broadcasted_iota(I32, (chunk, chunk), 1)
        d = (ri - ci).astype(F32)
        mask = jnp.where(d > 0, jnp.exp(lgf * jnp.maximum(d, 0.0)),
                         jnp.where(d < 0, jnp.exp(lgb * jnp.maximum(-d, 0.0)), 2.0))
        q_decay = jnp.exp(lgf * (ic + 1.0))
        k_decay = jnp.exp(lgf * (chunk - 1.0 - ic))
        s_decay = vexp(lgf * chunk)
        for j in range(sub):
            loc = pl.ds(j * chunk, chunk)
            glob = pl.ds(pl.multiple_of((c * sub + j) * chunk, chunk), chunk)
            a = (_nt_dot(q_ref[loc, :], k_ref[loc, :]) * mask).astype(BF16)
            qf = (q_ref[loc, :].astype(F32) * q_decay).astype(BF16)
            o = (jnp.dot(a, v_ref[loc, :], preferred_element_type=F32)
                 + jnp.dot(qf, sf_scr[...].astype(BF16), preferred_element_type=F32)
                 + ob_scr[glob, :])
            kf = (k_ref[loc, :].astype(F32) * k_decay).astype(BF16)
            sf_scr[...] = s_decay * sf_scr[...] + _tn_dot(kf, v_ref[loc, :])
            ms = jnp.mean(o * o, axis=-1, keepdims=True)
            y = o * lax.rsqrt(ms + EPS) * gn_ref[0]
            gt = g_ref[loc, :].astype(F32)
            o_ref[loc, :] = (y * (gt * jax.nn.sigmoid(gt))).astype(BF16)


def _retention(lg, p_lat, p_ctx, ret_norm_g, batch, seq, ctx_len):
    chunk = min(256, seq)
    sub = math.gcd(seq // chunk, RET_SUB)
    blk = chunk * sub
    nb = seq // blk
    kern = functools.partial(_ret_kernel, chunk=chunk, sub=sub, n_blocks=nb, ctx_len=ctx_len)

    def rows(b, p, c):
        return b * nb + jnp.where(p == 0, nb - 1 - c, c)

    def rows_fwd(b, p, c):
        return b * nb + jnp.where(p == 0, 0, c)

    return pl.pallas_call(
        kern,
        out_shape=jax.ShapeDtypeStruct((batch * seq, RET_HEADS * RET_DV), BF16),
        grid=(batch, RET_HEADS, 2, nb),
        in_specs=[pl.BlockSpec(memory_space=pltpu.SMEM),
                  pl.BlockSpec((blk, RET_DK), lambda b, h, p, c: (rows(b, p, c), h)),
                  pl.BlockSpec((blk, RET_DK), lambda b, h, p, c: (rows(b, p, c), 4 + h)),
                  pl.BlockSpec((blk, RET_DV), lambda b, h, p, c: (rows(b, p, c), 4 + h)),
                  pl.BlockSpec((blk, RET_DV), lambda b, h, p, c: (rows_fwd(b, p, c), 8 + h)),
                  pl.BlockSpec((ctx_len, RET_DK), lambda b, h, p, c: (b, 4 + h)),
                  pl.BlockSpec((ctx_len, RET_DV), lambda b, h, p, c: (b, 4 + h)),
                  pl.BlockSpec((1, 1, RET_DV), lambda b, h, p, c: (h, 0, 0))],
        out_specs=pl.BlockSpec((blk, RET_DV), lambda b, h, p, c: (rows_fwd(b, p, c), h)),
        scratch_shapes=[pltpu.VMEM((RET_DK, RET_DV), F32), pltpu.VMEM((RET_DK, RET_DV), F32),
                        pltpu.VMEM((seq, RET_DV), F32)],
        compiler_params=_params(("arbitrary",) * 4, 48),
        name="ret",
    )(lg, p_lat, p_lat, p_lat, p_lat, p_ctx, p_ctx, ret_norm_g.reshape(RET_HEADS, 1, RET_DV))


def _dattn_kernel(lam_ref, q_ref, ck_ref, cv_ref, k_ref, v_ref, gn_ref, gmat_ref, o_ref,
                  qq_scr, kmax_scr, mp_scr, acc_scr, *, tq, tk, n_kv, unroll):
    rows = 2 * tq
    qi = pl.program_id(2)

    def sq_norms(t):
        tf = t.astype(F32)
        return jnp.dot((tf * tf).astype(BF16), gmat_ref[...], preferred_element_type=F32)

    def kv_slice(ref, c):
        return ref[pl.ds(pl.multiple_of(c * tk, tk), tk), :]

    @pl.when(qi == 0)
    def _():
        kmax_scr[...] = jnp.max(sq_norms(ck_ref[...]), axis=0, keepdims=True)

        def body(c, carry):
            kmax_scr[...] = jnp.maximum(kmax_scr[...], jnp.max(sq_norms(kv_slice(k_ref, c)), axis=0, keepdims=True))
            return carry

        lax.fori_loop(0, n_kv, body, 0)

    q = q_ref[...]
    lane = lax.broadcasted_iota(I32, (tq, LANES), 1)
    z = jnp.zeros_like(q)
    qq_scr[0:tq, 0:LANES] = jnp.where(lane < DIFF_DH, q, z)
    qq_scr[tq:rows, 0:LANES] = jnp.where(lane >= DIFF_DH, q, z)

    def set_shift(shift):
        neg = -shift
        hi = neg.astype(BF16)
        r1 = neg - hi.astype(F32)
        mid = r1.astype(BF16)
        lo = (r1 - mid.astype(F32)).astype(BF16)
        lane_r = lax.broadcasted_iota(I32, (rows, LANES), 1)
        pieces = jnp.where(lane_r == 0, hi.astype(F32), jnp.where(lane_r == 1, mid.astype(F32),
                           jnp.where(lane_r == 2, lo.astype(F32), 0.0)))
        qq_scr[:, LANES:2 * LANES] = pieces.astype(BF16)

    bound = jnp.sqrt(sq_norms(q) * kmax_scr[...]) * SHIFT_SLACK
    b1 = jnp.max(jnp.where(lane < DIFF_DH, bound, 0.0), axis=1, keepdims=True)
    b2 = jnp.max(jnp.where(lane >= DIFF_DH, bound, 0.0), axis=1, keepdims=True)
    set_shift(jnp.concatenate([b1, b2], axis=0))

    @pl.when(jnp.max(bound) > MAX_SAFE_SHIFT)
    def _():
        def tile_max(k):
            s = _nt_dot(qq_scr[:, 0:LANES], k)
            mp = mp_scr[...]
            for cb in range(k.shape[0] // LANES):
                mp = jnp.maximum(mp, s[:, cb * LANES:(cb + 1) * LANES])
            mp_scr[...] = mp

        mp_scr[...] = jnp.full(mp_scr.shape, NEG_BIG, F32)
        tile_max(ck_ref[...])

        def body(c, carry):
            tile_max(kv_slice(k_ref, c))
            return carry

        lax.fori_loop(0, n_kv, body, 0)
        set_shift(jnp.max(mp_scr[...], axis=1, keepdims=True))

    def lane_const(n, hot):
        return jnp.where(lax.broadcasted_iota(I32, (n, LANES), 1) < hot, 1.0, 0.0).astype(BF16)

    acc_scr[...] = jnp.zeros(acc_scr.shape, F32)

    def tile_acc(k, v):
        n = k.shape[0]
        s = _nt_dot(qq_scr[...], jnp.concatenate([k, lane_const(n, 3)], axis=1))
        p = jnp.exp2(s).astype(BF16)
        acc_scr[...] += jnp.dot(p, jnp.concatenate([v, lane_const(n, 1)], axis=1), preferred_element_type=F32)

    tile_acc(ck_ref[...], cv_ref[...])

    def body_acc(c, carry):
        for u in range(unroll):
            tile_acc(kv_slice(k_ref, c * unroll + u), kv_slice(v_ref, c * unroll + u))
        return carry

    lax.fori_loop(0, n_kv // unroll, body_acc, 0)

    o = acc_scr[:, 0:LANES] / acc_scr[:, LANES:LANES + 1]
    d = o[0:tq, :] - lam_ref[0] * o[tq:rows, :]
    ms = jnp.mean(d * d, axis=-1, keepdims=True)
    y = d * lax.rsqrt(ms + EPS) * gn_ref[...] * (1.0 - LAMBDA_INIT)
    o_ref[...] = y.astype(BF16)


def _diff_attention(lam, p_lat, p_ctx, diff_norm_g, gmat, batch, seq, ctx_len):
    tq = min(256, seq)
    tk = min(512, seq)
    nq, nk = seq // tq, seq // tk
    unroll = math.gcd(nk, KV_UNROLL)
    assert ctx_len <= tk and ctx_len % LANES == 0
    kern = functools.partial(_dattn_kernel, tq=tq, tk=tk, n_kv=nk, unroll=unroll)
    return pl.pallas_call(
        kern,
        out_shape=jax.ShapeDtypeStruct((batch * seq, DIFF_HEADS * DIFF_DV), BF16),
        grid=(batch, DIFF_HEADS, nq),
        in_specs=[pl.BlockSpec(memory_space=pltpu.SMEM),
                  pl.BlockSpec((tq, LANES), lambda b, h, qi: (b * nq + qi, 48 + h)),
                  pl.BlockSpec((ctx_len, LANES), lambda b, h, qi: (b, 56 + h)),
                  pl.BlockSpec((ctx_len, LANES), lambda b, h, qi: (b, 64 + h)),
                  pl.BlockSpec((seq, LANES), lambda b, h, qi: (b, 56 + h)),
                  pl.BlockSpec((seq, LANES), lambda b, h, qi: (b, 64 + h)),
                  pl.BlockSpec((1, LANES), lambda b, h, qi: (0, 0)),
                  pl.BlockSpec((LANES, LANES), lambda b, h, qi: (0, 0))],
        out_specs=pl.BlockSpec((tq, LANES), lambda b, h, qi: (b * nq + qi, h)),
        scratch_shapes=[pltpu.VMEM((2 * tq, 2 * LANES), BF16), pltpu.VMEM((1, LANES), F32),
                        pltpu.VMEM((2 * tq, LANES), F32), pltpu.VMEM((2 * tq, 2 * LANES), F32)],
        compiler_params=_params(("arbitrary",) * 3, 48),
        name="dattn",
    )(lam, p_lat, p_ctx, p_ctx, p_lat, p_lat, diff_norm_g.reshape(1, DIFF_DV), gmat)


def _merge_kernel(yr_ref, yd_ref, ga_ref, gb_ref, x_ref, gatea_ref, shf_ref, scf_ref, n2_ref,
                  wr_ref, wd_ref, wo_ref, rwh_ref, rwl_ref, rb_ref,
                  xn_ref, hp_ref, meta_ref, gate_ref, cnt_ref, run_scr, *, tm):
    i = pl.program_id(0)

    @pl.when(i == 0)
    def _():
        run_scr[...] = jnp.zeros(run_scr.shape, F32)

    yr = jnp.dot(yr_ref[...], wr_ref[...], preferred_element_type=F32)
    yd = jnp.dot(yd_ref[...], wd_ref[...], preferred_element_type=F32)
    m = (jax.nn.sigmoid(ga_ref[...].astype(F32)) * yr + jax.nn.sigmoid(gb_ref[...].astype(F32)) * yd)
    z = jnp.dot(m.astype(BF16), wo_ref[...], preferred_element_type=F32)
    xn = x_ref[...] + gatea_ref[0] * z
    xn_ref[...] = xn

    ms = jnp.mean(xn * xn, axis=-1, keepdims=True)
    h2 = xn * lax.rsqrt(ms + EPS) * n2_ref[...]
    h2 = h2 * (1.0 + scf_ref[0]) + shf_ref[0]
    h_hi = h2.astype(BF16)
    bits = pltpu.bitcast(h_hi.astype(F32), U32)
    half = D_MODEL // 2
    hp_ref[...] = (bits[:, :half] >> 16) | (bits[:, half:] & jnp.uint32(0xFFFF0000))

    h_lo = (h2 - h_hi.astype(F32)).astype(BF16)
    logits = (jnp.dot(h_hi, rwh_ref[...], preferred_element_type=F32)
              + jnp.dot(h_lo, rwh_ref[...], preferred_element_type=F32)
              + jnp.dot(h_hi, rwl_ref[...], preferred_element_type=F32) + rb_ref[...])
    lane = lax.broadcasted_iota(I32, (tm, LANES), 1)
    lanef = lane.astype(F32)
    work = logits
    vals, idxs = [], []
    for _ in range(TOP_K):
        mk = jnp.max(work, axis=1, keepdims=True)
        ik = jnp.min(jnp.where(work == mk, lanef, float(LANES)), axis=1, keepdims=True)
        vals.append(mk)
        idxs.append(ik)
        work = jnp.where(lanef == ik, -jnp.inf, work)
    ex = [jnp.exp(v - vals[0]) for v in vals]
    den = ex[0] + ex[1] + ex[2] + ex[3]

    onehot = jnp.zeros((tm, LANES), F32)
    for ik in idxs:
        onehot = onehot + jnp.where(lanef == ik, 1.0, 0.0)
    ri = lax.broadcasted_iota(I32, (tm, tm), 0)
    ci = lax.broadcasted_iota(I32, (tm, tm), 1)
    tri = jnp.where(ri > ci, 1.0, 0.0).astype(BF16)
    base = run_scr[0:1, :] + jnp.dot(tri, onehot.astype(BF16), preferred_element_type=F32)
    run_scr[...] = run_scr[...] + jnp.sum(onehot, axis=0, keepdims=True)

    meta = jnp.zeros((tm, LANES), F32)
    gate_out = jnp.zeros((tm, LANES), F32)
    for k in range(TOP_K):
        rk = jnp.sum(jnp.where(lanef == idxs[k], base, 0.0), axis=1, keepdims=True)
        meta = jnp.where(lane == k, idxs[k], jnp.where(lane == TOP_K + k, rk, meta))
        gate_out = jnp.where(lane == k, ex[k] / den, gate_out)
    meta_ref[...] = meta.T[0:2 * TOP_K, :]
    gate_ref[...] = gate_out
    cnt_ref[...] = run_scr[...]


def _merge(y_ret, y_diff, p_lat, x2, g_a, sh_f, sc_f, norm2_g, w_r, w_d, w_o, rw_hi, rw_lo, rb, seq):
    n = x2.shape[0]
    tm = min(512, seq)
    tpb = seq // tm
    kern = functools.partial(_merge_kernel, tm=tm)
    mod_spec = pl.BlockSpec((1, 1, D_MODEL), lambda i: (i // tpb, 0, 0))
    const = lambda shape: pl.BlockSpec(shape, lambda i: (0,) * len(shape))
    tok = lambda w: pl.BlockSpec((tm, w), lambda i: (i, 0))
    return pl.pallas_call(
        kern,
        out_shape=(jax.ShapeDtypeStruct((n, D_MODEL), F32),
                   jax.ShapeDtypeStruct((n, D_MODEL // 2), U32),
                   jax.ShapeDtypeStruct((2 * TOP_K, n), F32),
                   jax.ShapeDtypeStruct((n, LANES), F32),
                   jax.ShapeDtypeStruct((8, LANES), F32)),
        grid=(n // tm,),
        in_specs=[tok(RET_HEADS * RET_DV), tok(D_MODEL),
                  pl.BlockSpec((tm, COL_TILE), lambda i: (i, 9)),
                  pl.BlockSpec((tm, COL_TILE), lambda i: (i, 10)),
                  tok(D_MODEL), mod_spec, mod_spec, mod_spec, const((1, D_MODEL)),
                  const((RET_HEADS * RET_DV, D_MODEL)), const((D_MODEL, D_MODEL)), const((D_MODEL, D_MODEL)),
                  const((D_MODEL, LANES)), const((D_MODEL, LANES)), const((1, LANES))],
        out_specs=(tok(D_MODEL), tok(D_MODEL // 2), pl.BlockSpec((2 * TOP_K, tm), lambda i: (0, i)),
                   tok(LANES), const((8, LANES))),
        scratch_shapes=[pltpu.VMEM((8, LANES), F32)],
        compiler_params=_params(("arbitrary",), 56),
        name="merge",
    )(y_ret, y_diff, p_lat, p_lat, x2, g_a, sh_f, sc_f, norm2_g, w_r, w_d, w_o, rw_hi, rw_lo, rb)


def _dispatch_kernel(dest_ref, hp_ref, xb_in_ref, xb_ref, sem, *, tm, n_tok):
    del xb_in_ref
    i = pl.program_id(0)

    def body(r, carry):
        tok = i * tm + r
        for k in range(TOP_K):
            d = dest_ref[k * n_tok + tok]
            pltpu.make_async_copy(hp_ref.at[pl.ds(r, 1), :], xb_ref.at[pl.ds(d, 1), :], sem).start()
        return carry

    lax.fori_loop(0, tm, body, 0)
    for _ in range(TOP_K):
        pltpu.make_async_copy(hp_ref, xb_ref.at[pl.ds(0, tm), :], sem).wait()


def _dispatch(dest, hp, xb_zero):
    n = hp.shape[0]
    tm = min(512, n)
    kern = functools.partial(_dispatch_kernel, tm=tm, n_tok=n)
    return pl.pallas_call(
        kern,
        out_shape=jax.ShapeDtypeStruct(xb_zero.shape, U32),
        grid_spec=pltpu.PrefetchScalarGridSpec(
            num_scalar_prefetch=1,
            grid=(n // tm,),
            in_specs=[pl.BlockSpec((tm, D_MODEL // 2), lambda i, d: (i, 0)),
                      pl.BlockSpec(memory_space=pl.ANY)],
            out_specs=pl.BlockSpec(memory_space=pl.ANY),
            scratch_shapes=[pltpu.SemaphoreType.DMA(())]),
        input_output_aliases={2: 0},
        compiler_params=_params(("arbitrary",), 32),
        name="dispatch",
    )(dest, hp, xb_zero)


def _expert_kernel(be_ref, nb_ref, xb_ref, w1_ref, b1_ref, w2_ref, b2_ref, y_ref, w1b_scr, w2b_scr):
    i = pl.program_id(0)
    e = be_ref[i]
    prev = be_ref[jnp.maximum(i - 1, 0)]

    @pl.when((i == 0) | (e != prev))
    def _():
        w1b_scr[...] = w1_ref[0].astype(BF16)
        w2b_scr[...] = w2_ref[0].astype(BF16)

    @pl.when(i < nb_ref[0])
    def _():
        xu = xb_ref[...]
        x_lo = pltpu.bitcast(xu << 16, F32).astype(BF16)
        x_hi = pltpu.bitcast(xu & jnp.uint32(0xFFFF0000), F32).astype(BF16)
        half = D_MODEL // 2
        hh = (jnp.dot(x_lo, w1b_scr[0:half, :], preferred_element_type=F32)
              + jnp.dot(x_hi, w1b_scr[half:, :], preferred_element_type=F32) + b1_ref[0])
        glu = jnp.minimum(hh[:, :D_FF], SWIGLU_LIMIT)
        lin = jnp.clip(hh[:, D_FF:], -SWIGLU_LIMIT, SWIGLU_LIMIT)
        act = glu * jax.nn.sigmoid(SWIGLU_ALPHA * glu) * (lin + 1.0)
        y_ref[...] = jnp.dot(act.astype(BF16), w2b_scr[...], preferred_element_type=F32) + b2_ref[0]

    @pl.when(i >= nb_ref[0])
    def _():
        y_ref[...] = jnp.zeros(y_ref.shape, F32)


def _experts(block_e, n_used, xb, w1, b1, w2, b2):
    rows = xb.shape[0]
    n_blocks = rows // MOE_BLK
    return pl.pallas_call(
        _expert_kernel,
        out_shape=jax.ShapeDtypeStruct((rows, D_MODEL), F32),
        grid_spec=pltpu.PrefetchScalarGridSpec(
            num_scalar_prefetch=2,
            grid=(n_blocks,),
            in_specs=[pl.BlockSpec((MOE_BLK, D_MODEL // 2), lambda i, be, nb: (i, 0)),
                      pl.BlockSpec((1, D_MODEL, 2 * D_FF), lambda i, be, nb: (be[i], 0, 0)),
                      pl.BlockSpec((1, 1, 2 * D_FF), lambda i, be, nb: (be[i], 0, 0)),
                      pl.BlockSpec((1, D_FF, D_MODEL), lambda i, be, nb: (be[i], 0, 0)),
                      pl.BlockSpec((1, 1, D_MODEL), lambda i, be, nb: (be[i], 0, 0))],
            out_specs=pl.BlockSpec((MOE_BLK, D_MODEL), lambda i, be, nb: (i, 0)),
            scratch_shapes=[pltpu.VMEM((D_MODEL, 2 * D_FF), BF16), pltpu.VMEM((D_FF, D_MODEL), BF16)]),
        compiler_params=_params(("arbitrary",), 56),
        name="expert",
    )(block_e, n_used, xb, w1, b1.reshape(N_EXPERTS, 1, 2 * D_FF), w2, b2.reshape(N_EXPERTS, 1, D_MODEL))


def _combine_kernel(dest_ref, yb_ref, gate_ref, xn_ref, gf_ref, o_ref, buf, sem, *, tm, n_tok):
    i = pl.program_id(0)

    def body(r, carry):
        tok = i * tm + r
        for k in range(TOP_K):
            d = dest_ref[k * n_tok + tok]
            pltpu.make_async_copy(yb_ref.at[pl.ds(d, 1), :], buf.at[k, pl.ds(r, 1), :], sem).start()
        return carry

    lax.fori_loop(0, tm, body, 0)
    for k in range(TOP_K):
        pltpu.make_async_copy(yb_ref.at[pl.ds(0, tm), :], buf.at[k], sem).wait()
    g = gate_ref[...]
    y = g[:, 0:1] * buf[0]
    for k in range(1, TOP_K):
        y = y + g[:, k:k + 1] * buf[k]
    o_ref[...] = xn_ref[...] + gf_ref[0] * y


def _combine(dest, yb, gate, xn, g_f, seq):
    n = xn.shape[0]
    tm = min(256, seq)
    tpb = seq // tm
    kern = functools.partial(_combine_kernel, tm=tm, n_tok=n)
    return pl.pallas_call(
        kern,
        out_shape=jax.ShapeDtypeStruct((n, D_MODEL), F32),
        grid_spec=pltpu.PrefetchScalarGridSpec(
            num_scalar_prefetch=1,
            grid=(n // tm,),
            in_specs=[pl.BlockSpec(memory_space=pl.ANY),
                      pl.BlockSpec((tm, LANES), lambda i, d: (i, 0)),
                      pl.BlockSpec((tm, D_MODEL), lambda i, d: (i, 0)),
                      pl.BlockSpec((1, 1, D_MODEL), lambda i, d: (i // tpb, 0, 0))],
            out_specs=pl.BlockSpec((tm, D_MODEL), lambda i, d: (i, 0)),
            scratch_shapes=[pltpu.VMEM((TOP_K, tm, D_MODEL), F32), pltpu.SemaphoreType.DMA(())]),
        compiler_params=_params(("arbitrary",), 32),
        name="combine",
    )(dest, yb, gate, xn, g_f)


def _layer(x, ctx, c, c_ctx, norm1_g, norm2_g, w_mod, b_mod, w_in, ret_decay_logit, ret_norm_g,
           diff_q_norm_g, diff_k_norm_g, diff_lambda, diff_norm_g, w_br_ret, w_br_diff, w_out,
           router_w, router_b, exp_w1, exp_b1, exp_w2, exp_b2):
    batch, seq, d = x.shape
    ctx_len = ctx.shape[1]
    assert d == D_MODEL and seq % GRID_W == 0 and batch + 1 <= 8
    n_tok = batch * seq

    cc = jnp.zeros((8, D_MODEL), F32).at[:batch].set(c).at[batch].set(c_ctx)
    mod = _mod(cc, w_mod, b_mod)
    sh_a, sc_a, g_a, sh_f, sc_f, g_f = [mod[:batch, i * D_MODEL:(i + 1) * D_MODEL].reshape(batch, 1, D_MODEL)
                                         for i in range(6)]
    csh_a = mod[batch:batch + 1, 0:D_MODEL].reshape(1, 1, D_MODEL)
    csc_a = mod[batch:batch + 1, D_MODEL:2 * D_MODEL].reshape(1, 1, D_MODEL)

    w_in_bf = w_in.astype(BF16)
    g1 = norm1_g.reshape(1, D_MODEL)
    tile = lambda g: jnp.tile(g.astype(F32), 2)
    qkg = jnp.zeros((8, LANES), F32).at[0].set(tile(diff_q_norm_g) * (DIFF_DH ** -0.5 * LOG2E)).at[1].set(tile(diff_k_norm_g))
    lane = jnp.arange(LANES)
    gmat = (lane[:, None] // DIFF_DH == lane[None, :] // DIFF_DH).astype(BF16)
    x2 = x.reshape(n_tok, D_MODEL)
    tm = min(1024, seq)
    p_lat = _inproj(x2, g1, sh_a, sc_a, w_in_bf, qkg, gmat, _rope_tables(seq), tm, seq // tm)
    p_ctx = _inproj(ctx.reshape(batch * ctx_len, D_MODEL), g1, csh_a, csc_a, w_in_bf, qkg, gmat,
                    _identity_tables(ctx_len), ctx_len, 1)

    lg = jax.nn.log_sigmoid(ret_decay_logit.astype(F32))
    y_ret = _retention(lg, p_lat, p_ctx, ret_norm_g, batch, seq, ctx_len)

    lp = diff_lambda.astype(F32)
    lam = (jnp.exp(jnp.sum(lp[0] * lp[1])) - jnp.exp(jnp.sum(lp[2] * lp[3])) + LAMBDA_INIT).reshape(1)
    y_diff = _diff_attention(lam, p_lat, p_ctx, diff_norm_g, gmat, batch, seq, ctx_len)

    rw = jnp.zeros((D_MODEL, LANES), F32).at[:, :N_EXPERTS].set(router_w)
    rw_hi = rw.astype(BF16)
    rw_lo = (rw - rw_hi.astype(F32)).astype(BF16)
    rb = jnp.full((1, LANES), NEG_BIG, F32).at[0, :N_EXPERTS].set(router_b)
    xn, hp, meta, gate4, cnt = _merge(
        y_ret, y_diff, p_lat, x2, g_a, sh_f, sc_f, norm2_g.reshape(1, D_MODEL),
        w_br_ret.astype(BF16), w_br_diff.astype(BF16), w_out.astype(BF16), rw_hi, rw_lo, rb, seq)

    counts = cnt[0, :N_EXPERTS].astype(I32)
    padded = (counts + MOE_BLK - 1) // MOE_BLK * MOE_BLK
    pad_end = jnp.cumsum(padded)
    pad_start = pad_end - padded
    n_pairs = n_tok * TOP_K
    n_blocks = n_pairs // MOE_BLK + N_EXPERTS
    meta_i = meta.astype(I32)
    is_e = meta_i[None, :TOP_K] == jnp.arange(N_EXPERTS, dtype=I32)[:, None, None]
    dest = (jnp.sum(jnp.where(is_e, pad_start[:, None, None], 0), axis=0) + meta_i[TOP_K:]).reshape(n_pairs)
    block_start = jnp.arange(n_blocks, dtype=I32) * MOE_BLK
    block_e = jnp.minimum(jnp.sum((pad_end[None, :] <= block_start[:, None]).astype(I32), axis=1), N_EXPERTS - 1)
    n_used = (pad_end[-1] // MOE_BLK).reshape(1).astype(I32)

    xb = _dispatch(dest, hp, jnp.zeros((n_blocks * MOE_BLK, D_MODEL // 2), U32))
    yb = _experts(block_e, n_used, xb, exp_w1, exp_b1, exp_w2, exp_b2)
    out = _combine(dest, yb, gate4, xn, g_f, seq)
    return out.reshape(batch, seq, D_MODEL)


def kernel(x, c, ctx, c_ctx, norm1_g, norm2_g, w_mod, b_mod, w_in, ret_decay_logit, ret_norm_g, diff_q_norm_g, diff_k_norm_g, diff_lambda, diff_norm_g, w_br_ret, w_br_diff, w_out, router_w, router_b, exp_w1, exp_b1, exp_w2, exp_b2):
    assert norm1_g.shape[0] == 1, "single-layer block"
    return _layer(x, ctx, c, c_ctx, norm1_g[0], norm2_g[0], w_mod[0], b_mod[0], w_in[0], ret_decay_logit[0],
                  ret_norm_g[0], diff_q_norm_g[0], diff_k_norm_g[0], diff_lambda[0], diff_norm_g[0],
                  w_br_ret[0], w_br_diff[0], w_out[0], router_w[0], router_b[0],
                  exp_w1[0], exp_b1[0], exp_w2[0], exp_b2[0])
```

```python
import functools
import math

import numpy as np

import jax
import jax.numpy as jnp
from jax import lax
from jax.experimental import pallas as pl
from jax.experimental.pallas import tpu as pltpu

F32 = jnp.float32
BF16 = jnp.bfloat16
U32 = jnp.uint32
I32 = jnp.int32

D_MODEL = 1024
GRID_W = 64
RET_HEADS = 4
RET_DK = 256
RET_DV = 512
DIFF_DH = 64
DIFF_HEADS = 8
DIFF_DV = 128
N_EXPERTS = 32
TOP_K = 4
D_FF = 1024
SWIGLU_LIMIT = 7.0
SWIGLU_ALPHA = 1.702
ROPE_BASE = 10000.0
EPS = 1e-6
LAMBDA_INIT = 0.8 - 0.6 * math.exp(-0.3 * 0)

IN_COLS = 11264
COL_TILE = 1024
LANES = 128
MOE_BLK = 256
NEG_BIG = -1e30
LOG2E = 1.4426950408889634
SHIFT_SLACK = 1.0 + 2.0 ** -6
MAX_SAFE_SHIFT = 60.0
RET_SUB = 4
VT_PAD = 16
KV_UNROLL = 8
HIGHEST = lax.Precision.HIGHEST
MIB = 1024 * 1024


def _params(sem, vmem_mib):
    return pltpu.CompilerParams(dimension_semantics=sem, vmem_limit_bytes=vmem_mib * MIB)


def _mod_kernel(c_ref, w_ref, b_ref, o_ref):
    c = c_ref[...]
    s = c * jax.nn.sigmoid(c)
    o_ref[...] = jnp.dot(s, w_ref[...], preferred_element_type=F32, precision=HIGHEST) + b_ref[...]


def _mod(cc, w_mod, b_mod):
    n = w_mod.shape[1]
    tn = 1024
    return pl.pallas_call(
        _mod_kernel,
        out_shape=jax.ShapeDtypeStruct((8, n), F32),
        grid=(n // tn,),
        in_specs=[pl.BlockSpec((8, D_MODEL), lambda j: (0, 0)),
                  pl.BlockSpec((D_MODEL, tn), lambda j: (0, j)),
                  pl.BlockSpec((1, tn), lambda j: (0, j))],
        out_specs=pl.BlockSpec((8, tn), lambda j: (0, j)),
        compiler_params=_params(("arbitrary",), 32),
        name="mod",
    )(cc, w_mod, b_mod.reshape(1, n))


def _inproj_kernel(x_ref, g_ref, sh_ref, sc_ref, w_ref, qkg_ref, gmat_ref,
                   cr_ref, sr_ref, cc_ref, sc2_ref, cd_ref, sa_ref, sb_ref,
                   o_ref, h_scr, acc_scr):
    j = pl.program_id(1)

    @pl.when(j == 0)
    def _():
        xf = x_ref[...]
        ms = jnp.mean(xf * xf, axis=-1, keepdims=True)
        y = xf * lax.rsqrt(ms + EPS) * g_ref[...]
        h_scr[...] = (y * (1.0 + sc_ref[0]) + sh_ref[0]).astype(BF16)

    acc_scr[...] = jnp.dot(h_scr[...], w_ref[...], preferred_element_type=F32)
    is_ret = j <= 1
    is_dqk = (j == 6) | (j == 7)

    @pl.when(is_ret)
    def _():
        scale = jnp.where(j == 0, RET_DK ** -0.5, 1.0).astype(F32)
        for b in range(COL_TILE // LANES):
            xb = acc_scr[:, b * LANES:(b + 1) * LANES]
            cos = cr_ref[...] if b % 2 == 0 else cc_ref[...]
            sin = sr_ref[...] if b % 2 == 0 else sc2_ref[...]
            o = (xb * cos + pltpu.roll(xb, 64, 1) * sin) * scale
            o_ref[:, b * LANES:(b + 1) * LANES] = o.astype(BF16)

    @pl.when(is_dqk)
    def _():
        g = jnp.where(j == 6, qkg_ref[0:1, :], qkg_ref[1:2, :])
        for b in range(COL_TILE // LANES):
            xb = acc_scr[:, b * LANES:(b + 1) * LANES]
            ss = jnp.dot((xb * xb).astype(BF16), gmat_ref[...], preferred_element_type=F32)
            yn = xb * lax.rsqrt(ss * (1.0 / DIFF_DH) + EPS) * g
            o = yn * cd_ref[...] + pltpu.roll(yn, 16, 1) * sa_ref[...] + pltpu.roll(yn, 112, 1) * sb_ref[...]
            o_ref[:, b * LANES:(b + 1) * LANES] = o.astype(BF16)

    @pl.when(jnp.logical_not(is_ret | is_dqk))
    def _():
        o_ref[...] = acc_scr[...].astype(BF16)


def _inproj(x2, g1, sh, sc, w_bf, qkg, gmat, tables, tm, tiles_per_batch):
    n = x2.shape[0]
    nb = sh.shape[0]
    tab_spec = pl.BlockSpec((tm, LANES), lambda i, j: (i % tiles_per_batch, 0))
    mod_spec = pl.BlockSpec((1, 1, D_MODEL), lambda i, j: (jnp.minimum(i // tiles_per_batch, nb - 1), 0, 0))
    return pl.pallas_call(
        _inproj_kernel,
        out_shape=jax.ShapeDtypeStruct((n, IN_COLS), BF16),
        grid=(n // tm, IN_COLS // COL_TILE),
        in_specs=[pl.BlockSpec((tm, D_MODEL), lambda i, j: (i, 0)),
                  pl.BlockSpec((1, D_MODEL), lambda i, j: (0, 0)),
                  mod_spec, mod_spec,
                  pl.BlockSpec((D_MODEL, COL_TILE), lambda i, j: (0, j)),
                  pl.BlockSpec((8, LANES), lambda i, j: (0, 0)),
                  pl.BlockSpec((LANES, LANES), lambda i, j: (0, 0))] + [tab_spec] * 7,
        out_specs=pl.BlockSpec((tm, COL_TILE), lambda i, j: (i, j)),
        scratch_shapes=[pltpu.VMEM((tm, D_MODEL), BF16), pltpu.VMEM((tm, COL_TILE), F32)],
        compiler_params=_params(("arbitrary", "arbitrary"), 48),
        name="inproj",
    )(x2, g1, sh, sc, w_bf, qkg, gmat, *tables)


def _rope_tables(seq):
    n_rows = seq // GRID_W
    f32 = np.float32

    def angles(pos, half):
        inv = f32(ROPE_BASE) ** (-np.arange(half, dtype=f32) / f32(half))
        return (pos.astype(f32)[:, None] * inv[None, :]).astype(np.float64)

    ar, ac = angles(np.arange(n_rows), 64), angles(np.arange(GRID_W), 64)
    br, bc = angles(np.arange(n_rows), 16), angles(np.arange(GRID_W), 16)
    zr, zc = np.zeros_like(br), np.zeros_like(bc)
    cat = lambda parts, reps=1: np.tile(np.concatenate(parts, axis=1), (1, reps)).astype(f32)
    by_row = lambda t: jnp.repeat(jnp.asarray(t), GRID_W, axis=0)
    by_col = lambda t: jnp.tile(jnp.asarray(t), (n_rows, 1))
    cr = by_row(cat([np.cos(ar), np.cos(ar)]))
    sr = by_row(cat([-np.sin(ar), np.sin(ar)]))
    cc = by_col(cat([np.cos(ac), np.cos(ac)]))
    sc = by_col(cat([-np.sin(ac), np.sin(ac)]))
    cd = by_row(cat([np.cos(br), np.cos(br), zr, zr], 2)) + by_col(cat([zc, zc, np.cos(bc), np.cos(bc)], 2))
    sa = by_row(cat([zr, np.sin(br), zr, zr], 2)) + by_col(cat([zc, zc, zc, np.sin(bc)], 2))
    sb = by_row(cat([-np.sin(br), zr, zr, zr], 2)) + by_col(cat([zc, zc, -np.sin(bc), zc], 2))
    return [cr, sr, cc, sc, cd, sa, sb]


def _identity_tables(seq):
    one = jnp.ones((seq, LANES), F32)
    zero = jnp.zeros((seq, LANES), F32)
    return [one, zero, one, zero, one, zero, zero]


def _tn_dot(a, b):
    return lax.dot_general(a, b, (((0,), (0,)), ((), ())), preferred_element_type=F32)


def _nt_dot(a, b):
    return lax.dot_general(a, b, (((1,), (1,)), ((), ())), preferred_element_type=F32)


def _ret_kernel(lg_ref, q_ref, k_ref, v_ref, g_ref, ck_ref, cv_ref, gn_ref, o_ref,
                sf_scr, sb_scr, ob_scr, *, chunk, sub, n_blocks, ctx_len):
    h = pl.program_id(1)
    p = pl.program_id(2)
    c = pl.program_id(3)
    lgf = lg_ref[0, h]
    lgb = lg_ref[1, h]

    def col_iota(n):
        return lax.broadcasted_iota(I32, (n, 1), 0).astype(F32)

    def vexp(s):
        return jnp.exp(jnp.zeros((1, 1), F32) + s)

    @pl.when((p == 0) & (c == 0))
    def _():
        jc = col_iota(ctx_len)
        kc = ck_ref[...].astype(F32)
        vc = cv_ref[...]
        sf_scr[...] = _tn_dot((kc * jnp.exp(lgf * (ctx_len - 1.0 - jc))).astype(BF16), vc)
        sb_scr[...] = _tn_dot((kc * jnp.exp(lgb * jc)).astype(BF16), vc)

    ic = col_iota(chunk)

    @pl.when(p == 0)
    def _():
        q_decay = jnp.exp(lgb * (chunk - ic))
        k_decay = jnp.exp(lgb * ic)
        s_decay = vexp(lgb * chunk)
        for j in reversed(range(sub)):
            loc = pl.ds(j * chunk, chunk)
            glob = pl.ds(pl.multiple_of(((n_blocks - 1 - c) * sub + j) * chunk, chunk), chunk)
            qb = (q_ref[loc, :].astype(F32) * q_decay).astype(BF16)
            ob_scr[glob, :] = jnp.dot(qb, sb_scr[...].astype(BF16), preferred_element_type=F32)
            kb = (k_ref[loc, :].astype(F32) * k_decay).astype(BF16)
            sb_scr[...] = s_decay * sb_scr[...] + _tn_dot(kb, v_ref[loc, :])

    @pl.when(p == 1)
    def _():
        ri = lax.broadcasted_iota(I32, (chunk, chunk), 0)
        ci = lax.broadcasted_iota(I32, (chunk, chunk), 1)
        d = (ri - ci).astype(F32)
        mask = jnp.where(d > 0, jnp.exp(lgf * jnp.maximum(d, 0.0)),
                         jnp.where(d < 0, jnp.exp(lgb * jnp.maximum(-d, 0.0)), 2.0))
        q_decay = jnp.exp(lgf * (ic + 1.0))
        k_decay = jnp.exp(lgf * (chunk - 1.0 - ic))
        s_decay = vexp(lgf * chunk)
        for j in range(sub):
            loc = pl.ds(j * chunk, chunk)
            glob = pl.ds(pl.multiple_of((c * sub + j) * chunk, chunk), chunk)
            a = (_nt_dot(q_ref[loc, :], k_ref[loc, :]) * mask).astype(BF16)
            qf = (q_ref[loc, :].astype(F32) * q_decay).astype(BF16)
            o = (jnp.dot(a, v_ref[loc, :], preferred_element_type=F32)
                 + jnp.dot(qf, sf_scr[...].astype(BF16), preferred_element_type=F32)
                 + ob_scr[glob, :])
            kf = (k_ref[loc, :].astype(F32) * k_decay).astype(BF16)
            sf_scr[...] = s_decay * sf_scr[...] + _tn_dot(kf, v_ref[loc, :])
            ms = jnp.mean(o * o, axis=-1, keepdims=True)
            y = o * lax.rsqrt(ms + EPS) * gn_ref[0]
            gt = g_ref[loc, :].astype(F32)
            o_ref[loc, :] = (y * (gt * jax.nn.sigmoid(gt))).astype(BF16)


def _retention(lg, p_lat, p_ctx, ret_norm_g, batch, seq, ctx_len):
    chunk = min(256, seq)
    sub = math.gcd(seq // chunk, RET_SUB)
    blk = chunk * sub
    nb = seq // blk
    kern = functools.partial(_ret_kernel, chunk=chunk, sub=sub, n_blocks=nb, ctx_len=ctx_len)

    def rows(b, p, c):
        return b * nb + jnp.where(p == 0, nb - 1 - c, c)

    def rows_fwd(b, p, c):
        return b * nb + jnp.where(p == 0, 0, c)

    return pl.pallas_call(
        kern,
        out_shape=jax.ShapeDtypeStruct((batch * seq, RET_HEADS * RET_DV), BF16),
        grid=(batch, RET_HEADS, 2, nb),
        in_specs=[pl.BlockSpec(memory_space=pltpu.SMEM),
                  pl.BlockSpec((blk, RET_DK), lambda b, h, p, c: (rows(b, p, c), h)),
                  pl.BlockSpec((blk, RET_DK), lambda b, h, p, c: (rows(b, p, c), 4 + h)),
                  pl.BlockSpec((blk, RET_DV), lambda b, h, p, c: (rows(b, p, c), 4 + h)),
                  pl.BlockSpec((blk, RET_DV), lambda b, h, p, c: (rows_fwd(b, p, c), 8 + h)),
                  pl.BlockSpec((ctx_len, RET_DK), lambda b, h, p, c: (b, 4 + h)),
                  pl.BlockSpec((ctx_len, RET_DV), lambda b, h, p, c: (b, 4 + h)),
                  pl.BlockSpec((1, 1, RET_DV), lambda b, h, p, c: (h, 0, 0))],
        out_specs=pl.BlockSpec((blk, RET_DV), lambda b, h, p, c: (rows_fwd(b, p, c), h)),
        scratch_shapes=[pltpu.VMEM((RET_DK, RET_DV), F32), pltpu.VMEM((RET_DK, RET_DV), F32),
                        pltpu.VMEM((seq, RET_DV), F32)],
        compiler_params=_params(("arbitrary",) * 4, 48),
        name="ret",
    )(lg, p_lat, p_lat, p_lat, p_lat, p_ctx, p_ctx, ret_norm_g.reshape(RET_HEADS, 1, RET_DV))


def _dattn_kernel(lam_ref, q_ref, ck_ref, cv_ref, k_ref, v_ref, gn_ref, gmat_ref, o_ref,
                  qq_scr, kmax_scr, mp_scr, kp_scr, vt_scr, pt_scr, *, tq, tk, n_kv, ta, unroll, ctx_len):
    rows = 2 * tq
    n_all = kp_scr.shape[0]
    qi = pl.program_id(2)

    def sq_norms(t):
        tf = t.astype(F32)
        return jnp.dot((tf * tf).astype(BF16), gmat_ref[...], preferred_element_type=F32)

    def kv_slice(ref, c):
        return ref[pl.ds(pl.multiple_of(c * tk, tk), tk), :]

    def lane_const(n, hot):
        return jnp.where(lax.broadcasted_iota(I32, (n, LANES), 1) < hot, 1.0, 0.0).astype(BF16)

    @pl.when(qi == 0)
    def _():
        kmax_scr[...] = jnp.max(sq_norms(ck_ref[...]), axis=0, keepdims=True)
        kp_scr[:, LANES:2 * LANES] = lane_const(n_all, 3)
        kp_scr[0:ctx_len, 0:LANES] = ck_ref[...]
        tail = lax.broadcasted_iota(I32, (VT_PAD, n_all), 0)
        vt_scr[DIFF_DV:, :] = jnp.where(tail == 0, 1.0, 0.0).astype(BF16)
        vt_scr[0:DIFF_DV, 0:ctx_len] = cv_ref[...].astype(F32).T.astype(BF16)

        def body(c, carry):
            start = pl.multiple_of(c * tk, tk)
            k = k_ref[pl.ds(start, tk), :]
            kmax_scr[...] = jnp.maximum(kmax_scr[...], jnp.max(sq_norms(k), axis=0, keepdims=True))
            kp_scr[pl.ds(pl.multiple_of(ctx_len + start, LANES), tk), 0:LANES] = k
            vt_scr[0:DIFF_DV, pl.ds(pl.multiple_of(ctx_len + start, LANES), tk)] = (
                v_ref[pl.ds(start, tk), :].astype(F32).T.astype(BF16))
            return carry

        lax.fori_loop(0, n_kv, body, 0)

    q = q_ref[...]
    lane = lax.broadcasted_iota(I32, (tq, LANES), 1)
    z = jnp.zeros_like(q)
    qq_scr[0:tq, 0:LANES] = jnp.where(lane < DIFF_DH, q, z)
    qq_scr[tq:rows, 0:LANES] = jnp.where(lane >= DIFF_DH, q, z)

    def set_shift(shift):
        neg = -shift
        hi = neg.astype(BF16)
        r1 = neg - hi.astype(F32)
        mid = r1.astype(BF16)
        lo = (r1 - mid.astype(F32)).astype(BF16)
        lane_r = lax.broadcasted_iota(I32, (rows, LANES), 1)
        pieces = jnp.where(lane_r == 0, hi.astype(F32), jnp.where(lane_r == 1, mid.astype(F32),
                           jnp.where(lane_r == 2, lo.astype(F32), 0.0)))
        qq_scr[:, LANES:2 * LANES] = pieces.astype(BF16)

    bound = jnp.sqrt(sq_norms(q) * kmax_scr[...]) * SHIFT_SLACK
    b1 = jnp.max(jnp.where(lane < DIFF_DH, bound, 0.0), axis=1, keepdims=True)
    b2 = jnp.max(jnp.where(lane >= DIFF_DH, bound, 0.0), axis=1, keepdims=True)
    set_shift(jnp.concatenate([b1, b2], axis=0))

    @pl.when(jnp.max(bound) > MAX_SAFE_SHIFT)
    def _():
        def tile_max(k):
            s = _nt_dot(qq_scr[:, 0:LANES], k)
            mp = mp_scr[...]
            for cb in range(k.shape[0] // LANES):
                mp = jnp.maximum(mp, s[:, cb * LANES:(cb + 1) * LANES])
            mp_scr[...] = mp

        mp_scr[...] = jnp.full(mp_scr.shape, NEG_BIG, F32)
        tile_max(ck_ref[...])

        def body(c, carry):
            tile_max(kv_slice(k_ref, c))
            return carry

        lax.fori_loop(0, n_kv, body, 0)
        set_shift(jnp.max(mp_scr[...], axis=1, keepdims=True))

    def body_a(t, carry):
        for u in range(unroll):
            r = pl.ds(pl.multiple_of((t * unroll + u) * ta, ta), ta)
            pt_scr[r, :] = jnp.exp2(_nt_dot(kp_scr[r, :], qq_scr[...])).astype(BF16)
        return carry

    lax.fori_loop(0, n_all // (ta * unroll), body_a, 0)

    acc = jnp.dot(vt_scr[...], pt_scr[...], preferred_element_type=F32)
    ot = acc[0:DIFF_DV, :] / acc[DIFF_DV:DIFF_DV + 1, :]
    d = (ot[:, 0:tq] - lam_ref[0] * ot[:, tq:rows]).T
    ms = jnp.mean(d * d, axis=-1, keepdims=True)
    y = d * lax.rsqrt(ms + EPS) * gn_ref[...] * (1.0 - LAMBDA_INIT)
    o_ref[...] = y.astype(BF16)


def _diff_attention(lam, p_lat, p_ctx, diff_norm_g, gmat, batch, seq, ctx_len):
    tq = min(256, seq)
    tk = min(512, seq)
    nq, nk = seq // tq, seq // tk
    n_all = ctx_len + seq
    ta = next(t for t in (528, 512, 384, 320, 256, 128) if n_all % t == 0)
    unroll = math.gcd(n_all // ta, KV_UNROLL)
    assert ctx_len % LANES == 0
    kern = functools.partial(_dattn_kernel, tq=tq, tk=tk, n_kv=nk, ta=ta, unroll=unroll, ctx_len=ctx_len)
    return pl.pallas_call(
        kern,
        out_shape=jax.ShapeDtypeStruct((batch * seq, DIFF_HEADS * DIFF_DV), BF16),
        grid=(batch, DIFF_HEADS, nq),
        in_specs=[pl.BlockSpec(memory_space=pltpu.SMEM),
                  pl.BlockSpec((tq, LANES), lambda b, h, qi: (b * nq + qi, 48 + h)),
                  pl.BlockSpec((ctx_len, LANES), lambda b, h, qi: (b, 56 + h)),
                  pl.BlockSpec((ctx_len, LANES), lambda b, h, qi: (b, 64 + h)),
                  pl.BlockSpec((seq, LANES), lambda b, h, qi: (b, 56 + h)),
                  pl.BlockSpec((seq, LANES), lambda b, h, qi: (b, 64 + h)),
                  pl.BlockSpec((1, LANES), lambda b, h, qi: (0, 0)),
                  pl.BlockSpec((LANES, LANES), lambda b, h, qi: (0, 0))],
        out_specs=pl.BlockSpec((tq, LANES), lambda b, h, qi: (b * nq + qi, h)),
        scratch_shapes=[pltpu.VMEM((2 * tq, 2 * LANES), BF16), pltpu.VMEM((1, LANES), F32),
                        pltpu.VMEM((2 * tq, LANES), F32), pltpu.VMEM((n_all, 2 * LANES), BF16),
                        pltpu.VMEM((DIFF_DV + VT_PAD, n_all), BF16), pltpu.VMEM((n_all, 2 * tq), BF16)],
        compiler_params=_params(("arbitrary",) * 3, 48),
        name="dattn",
    )(lam, p_lat, p_ctx, p_ctx, p_lat, p_lat, diff_norm_g.reshape(1, DIFF_DV), gmat)


def _merge_kernel(yr_ref, yd_ref, ga_ref, gb_ref, x_ref, gatea_ref, shf_ref, scf_ref, n2_ref,
                  wr_ref, wd_ref, wo_ref, rwh_ref, rwl_ref, rb_ref,
                  xn_ref, hp_ref, meta_ref, gate_ref, cnt_ref, run_scr, *, tm):
    i = pl.program_id(0)

    @pl.when(i == 0)
    def _():
        run_scr[...] = jnp.zeros(run_scr.shape, F32)

    yr = jnp.dot(yr_ref[...], wr_ref[...], preferred_element_type=F32)
    yd = jnp.dot(yd_ref[...], wd_ref[...], preferred_element_type=F32)
    m = (jax.nn.sigmoid(ga_ref[...].astype(F32)) * yr + jax.nn.sigmoid(gb_ref[...].astype(F32)) * yd)
    z = jnp.dot(m.astype(BF16), wo_ref[...], preferred_element_type=F32)
    xn = x_ref[...] + gatea_ref[0] * z
    xn_ref[...] = xn

    ms = jnp.mean(xn * xn, axis=-1, keepdims=True)
    h2 = xn * lax.rsqrt(ms + EPS) * n2_ref[...]
    h2 = h2 * (1.0 + scf_ref[0]) + shf_ref[0]
    h_hi = h2.astype(BF16)
    bits = pltpu.bitcast(h_hi.astype(F32), U32)
    half = D_MODEL // 2
    hp_ref[...] = (bits[:, :half] >> 16) | (bits[:, half:] & jnp.uint32(0xFFFF0000))

    h_lo = (h2 - h_hi.astype(F32)).astype(BF16)
    logits = (jnp.dot(h_hi, rwh_ref[...], preferred_element_type=F32)
              + jnp.dot(h_lo, rwh_ref[...], preferred_element_type=F32)
              + jnp.dot(h_hi, rwl_ref[...], preferred_element_type=F32) + rb_ref[...])
    lane = lax.broadcasted_iota(I32, (tm, LANES), 1)
    lanef = lane.astype(F32)
    work = logits
    vals, idxs = [], []
    for _ in range(TOP_K):
        mk = jnp.max(work, axis=1, keepdims=True)
        ik = jnp.min(jnp.where(work == mk, lanef, float(LANES)), axis=1, keepdims=True)
        vals.append(mk)
        idxs.append(ik)
        work = jnp.where(lanef == ik, -jnp.inf, work)
    ex = [jnp.exp(v - vals[0]) for v in vals]
    den = ex[0] + ex[1] + ex[2] + ex[3]

    onehot = jnp.zeros((tm, LANES), F32)
    for ik in idxs:
        onehot = onehot + jnp.where(lanef == ik, 1.0, 0.0)
    ri = lax.broadcasted_iota(I32, (tm, tm), 0)
    ci = lax.broadcasted_iota(I32, (tm, tm), 1)
    tri = jnp.where(ri > ci, 1.0, 0.0).astype(BF16)
    base = run_scr[0:1, :] + jnp.dot(tri, onehot.astype(BF16), preferred_element_type=F32)
    run_scr[...] = run_scr[...] + jnp.sum(onehot, axis=0, keepdims=True)

    meta = jnp.zeros((tm, LANES), F32)
    gate_out = jnp.zeros((tm, LANES), F32)
    for k in range(TOP_K):
        rk = jnp.sum(jnp.where(lanef == idxs[k], base, 0.0), axis=1, keepdims=True)
        meta = jnp.where(lane == k, idxs[k], jnp.where(lane == TOP_K + k, rk, meta))
        gate_out = jnp.where(lane == k, ex[k] / den, gate_out)
    meta_ref[...] = meta.T[0:2 * TOP_K, :]
    gate_ref[...] = gate_out
    cnt_ref[...] = run_scr[...]


def _merge(y_ret, y_diff, p_lat, x2, g_a, sh_f, sc_f, norm2_g, w_r, w_d, w_o, rw_hi, rw_lo, rb, seq):
    n = x2.shape[0]
    tm = min(512, seq)
    tpb = seq // tm
    kern = functools.partial(_merge_kernel, tm=tm)
    mod_spec = pl.BlockSpec((1, 1, D_MODEL), lambda i: (i // tpb, 0, 0))
    const = lambda shape: pl.BlockSpec(shape, lambda i: (0,) * len(shape))
    tok = lambda w: pl.BlockSpec((tm, w), lambda i: (i, 0))
    return pl.pallas_call(
        kern,
        out_shape=(jax.ShapeDtypeStruct((n, D_MODEL), F32),
                   jax.ShapeDtypeStruct((n, D_MODEL // 2), U32),
                   jax.ShapeDtypeStruct((2 * TOP_K, n), F32),
                   jax.ShapeDtypeStruct((n, LANES), F32),
                   jax.ShapeDtypeStruct((8, LANES), F32)),
        grid=(n // tm,),
        in_specs=[tok(RET_HEADS * RET_DV), tok(D_MODEL),
                  pl.BlockSpec((tm, COL_TILE), lambda i: (i, 9)),
                  pl.BlockSpec((tm, COL_TILE), lambda i: (i, 10)),
                  tok(D_MODEL), mod_spec, mod_spec, mod_spec, const((1, D_MODEL)),
                  const((RET_HEADS * RET_DV, D_MODEL)), const((D_MODEL, D_MODEL)), const((D_MODEL, D_MODEL)),
                  const((D_MODEL, LANES)), const((D_MODEL, LANES)), const((1, LANES))],
        out_specs=(tok(D_MODEL), tok(D_MODEL // 2), pl.BlockSpec((2 * TOP_K, tm), lambda i: (0, i)),
                   tok(LANES), const((8, LANES))),
        scratch_shapes=[pltpu.VMEM((8, LANES), F32)],
        compiler_params=_params(("arbitrary",), 56),
        name="merge",
    )(y_ret, y_diff, p_lat, p_lat, x2, g_a, sh_f, sc_f, norm2_g, w_r, w_d, w_o, rw_hi, rw_lo, rb)


def _dispatch_kernel(dest_ref, hp_ref, xb_in_ref, xb_ref, sem, *, tm, n_tok):
    del xb_in_ref
    i = pl.program_id(0)

    def body(r, carry):
        tok = i * tm + r
        for k in range(TOP_K):
            d = dest_ref[k * n_tok + tok]
            pltpu.make_async_copy(hp_ref.at[pl.ds(r, 1), :], xb_ref.at[pl.ds(d, 1), :], sem).start()
        return carry

    lax.fori_loop(0, tm, body, 0)
    for _ in range(TOP_K):
        pltpu.make_async_copy(hp_ref, xb_ref.at[pl.ds(0, tm), :], sem).wait()


def _dispatch(dest, hp, xb_zero):
    n = hp.shape[0]
    tm = min(512, n)
    kern = functools.partial(_dispatch_kernel, tm=tm, n_tok=n)
    return pl.pallas_call(
        kern,
        out_shape=jax.ShapeDtypeStruct(xb_zero.shape, U32),
        grid_spec=pltpu.PrefetchScalarGridSpec(
            num_scalar_prefetch=1,
            grid=(n // tm,),
            in_specs=[pl.BlockSpec((tm, D_MODEL // 2), lambda i, d: (i, 0)),
                      pl.BlockSpec(memory_space=pl.ANY)],
            out_specs=pl.BlockSpec(memory_space=pl.ANY),
            scratch_shapes=[pltpu.SemaphoreType.DMA(())]),
        input_output_aliases={2: 0},
        compiler_params=_params(("arbitrary",), 32),
        name="dispatch",
    )(dest, hp, xb_zero)


def _expert_kernel(be_ref, nb_ref, xb_ref, w1_ref, b1_ref, w2_ref, b2_ref, y_ref, w1b_scr, w2b_scr):
    i = pl.program_id(0)
    e = be_ref[i]
    prev = be_ref[jnp.maximum(i - 1, 0)]

    @pl.when((i == 0) | (e != prev))
    def _():
        w1b_scr[...] = w1_ref[0].astype(BF16)
        w2b_scr[...] = w2_ref[0].astype(BF16)

    @pl.when(i < nb_ref[0])
    def _():
        xu = xb_ref[...]
        x_lo = pltpu.bitcast(xu << 16, F32).astype(BF16)
        x_hi = pltpu.bitcast(xu & jnp.uint32(0xFFFF0000), F32).astype(BF16)
        half = D_MODEL // 2
        hh = (jnp.dot(x_lo, w1b_scr[0:half, :], preferred_element_type=F32)
              + jnp.dot(x_hi, w1b_scr[half:, :], preferred_element_type=F32) + b1_ref[0])
        glu = jnp.minimum(hh[:, :D_FF], SWIGLU_LIMIT)
        lin = jnp.clip(hh[:, D_FF:], -SWIGLU_LIMIT, SWIGLU_LIMIT)
        act = glu * jax.nn.sigmoid(SWIGLU_ALPHA * glu) * (lin + 1.0)
        y_ref[...] = jnp.dot(act.astype(BF16), w2b_scr[...], preferred_element_type=F32) + b2_ref[0]

    @pl.when(i >= nb_ref[0])
    def _():
        y_ref[...] = jnp.zeros(y_ref.shape, F32)


def _experts(block_e, n_used, xb, w1, b1, w2, b2):
    rows = xb.shape[0]
    n_blocks = rows // MOE_BLK
    return pl.pallas_call(
        _expert_kernel,
        out_shape=jax.ShapeDtypeStruct((rows, D_MODEL), F32),
        grid_spec=pltpu.PrefetchScalarGridSpec(
            num_scalar_prefetch=2,
            grid=(n_blocks,),
            in_specs=[pl.BlockSpec((MOE_BLK, D_MODEL // 2), lambda i, be, nb: (i, 0)),
                      pl.BlockSpec((1, D_MODEL, 2 * D_FF), lambda i, be, nb: (be[i], 0, 0)),
                      pl.BlockSpec((1, 1, 2 * D_FF), lambda i, be, nb: (be[i], 0, 0)),
                      pl.BlockSpec((1, D_FF, D_MODEL), lambda i, be, nb: (be[i], 0, 0)),
                      pl.BlockSpec((1, 1, D_MODEL), lambda i, be, nb: (be[i], 0, 0))],
            out_specs=pl.BlockSpec((MOE_BLK, D_MODEL), lambda i, be, nb: (i, 0)),
            scratch_shapes=[pltpu.VMEM((D_MODEL, 2 * D_FF), BF16), pltpu.VMEM((D_FF, D_MODEL), BF16)]),
        compiler_params=_params(("arbitrary",), 56),
        name="expert",
    )(block_e, n_used, xb, w1, b1.reshape(N_EXPERTS, 1, 2 * D_FF), w2, b2.reshape(N_EXPERTS, 1, D_MODEL))


def _combine_kernel(dest_ref, yb_ref, gate_ref, xn_ref, gf_ref, o_ref, buf, sem, *, tm, n_tok):
    i = pl.program_id(0)

    def body(r, carry):
        tok = i * tm + r
        for k in range(TOP_K):
            d = dest_ref[k * n_tok + tok]
            pltpu.make_async_copy(yb_ref.at[pl.ds(d, 1), :], buf.at[k, pl.ds(r, 1), :], sem).start()
        return carry

    lax.fori_loop(0, tm, body, 0)
    for k in range(TOP_K):
        pltpu.make_async_copy(yb_ref.at[pl.ds(0, tm), :], buf.at[k], sem).wait()
    g = gate_ref[...]
    y = g[:, 0:1] * buf[0]
    for k in range(1, TOP_K):
        y = y + g[:, k:k + 1] * buf[k]
    o_ref[...] = xn_ref[...] + gf_ref[0] * y


def _combine(dest, yb, gate, xn, g_f, seq):
    n = xn.shape[0]
    tm = min(256, seq)
    tpb = seq // tm
    kern = functools.partial(_combine_kernel, tm=tm, n_tok=n)
    return pl.pallas_call(
        kern,
        out_shape=jax.ShapeDtypeStruct((n, D_MODEL), F32),
        grid_spec=pltpu.PrefetchScalarGridSpec(
            num_scalar_prefetch=1,
            grid=(n // tm,),
            in_specs=[pl.BlockSpec(memory_space=pl.ANY),
                      pl.BlockSpec((tm, LANES), lambda i, d: (i, 0)),
                      pl.BlockSpec((tm, D_MODEL), lambda i, d: (i, 0)),
                      pl.BlockSpec((1, 1, D_MODEL), lambda i, d: (i // tpb, 0, 0))],
            out_specs=pl.BlockSpec((tm, D_MODEL), lambda i, d: (i, 0)),
            scratch_shapes=[pltpu.VMEM((TOP_K, tm, D_MODEL), F32), pltpu.SemaphoreType.DMA(())]),
        compiler_params=_params(("arbitrary",), 32),
        name="combine",
    )(dest, yb, gate, xn, g_f)


def _layer(x, ctx, c, c_ctx, norm1_g, norm2_g, w_mod, b_mod, w_in, ret_decay_logit, ret_norm_g,
           diff_q_norm_g, diff_k_norm_g, diff_lambda, diff_norm_g, w_br_ret, w_br_diff, w_out,
           router_w, router_b, exp_w1, exp_b1, exp_w2, exp_b2):
    batch, seq, d = x.shape
    ctx_len = ctx.shape[1]
    assert d == D_MODEL and seq % GRID_W == 0 and batch + 1 <= 8
    n_tok = batch * seq

    cc = jnp.zeros((8, D_MODEL), F32).at[:batch].set(c).at[batch].set(c_ctx)
    mod = _mod(cc, w_mod, b_mod)
    sh_a, sc_a, g_a, sh_f, sc_f, g_f = [mod[:batch, i * D_MODEL:(i + 1) * D_MODEL].reshape(batch, 1, D_MODEL)
                                         for i in range(6)]
    csh_a = mod[batch:batch + 1, 0:D_MODEL].reshape(1, 1, D_MODEL)
    csc_a = mod[batch:batch + 1, D_MODEL:2 * D_MODEL].reshape(1, 1, D_MODEL)

    w_in_bf = w_in.astype(BF16)
    g1 = norm1_g.reshape(1, D_MODEL)
    tile = lambda g: jnp.tile(g.astype(F32), 2)
    qkg = jnp.zeros((8, LANES), F32).at[0].set(tile(diff_q_norm_g) * (DIFF_DH ** -0.5 * LOG2E)).at[1].set(tile(diff_k_norm_g))
    lane = jnp.arange(LANES)
    gmat = (lane[:, None] // DIFF_DH == lane[None, :] // DIFF_DH).astype(BF16)
    x2 = x.reshape(n_tok, D_MODEL)
    tm = min(1024, seq)
    p_lat = _inproj(x2, g1, sh_a, sc_a, w_in_bf, qkg, gmat, _rope_tables(seq), tm, seq // tm)
    p_ctx = _inproj(ctx.reshape(batch * ctx_len, D_MODEL), g1, csh_a, csc_a, w_in_bf, qkg, gmat,
                    _identity_tables(ctx_len), ctx_len, 1)

    lg = jax.nn.log_sigmoid(ret_decay_logit.astype(F32))
    y_ret = _retention(lg, p_lat, p_ctx, ret_norm_g, batch, seq, ctx_len)

    lp = diff_lambda.astype(F32)
    lam = (jnp.exp(jnp.sum(lp[0] * lp[1])) - jnp.exp(jnp.sum(lp[2] * lp[3])) + LAMBDA_INIT).reshape(1)
    y_diff = _diff_attention(lam, p_lat, p_ctx, diff_norm_g, gmat, batch, seq, ctx_len)

    rw = jnp.zeros((D_MODEL, LANES), F32).at[:, :N_EXPERTS].set(router_w)
    rw_hi = rw.astype(BF16)
    rw_lo = (rw - rw_hi.astype(F32)).astype(BF16)
    rb = jnp.full((1, LANES), NEG_BIG, F32).at[0, :N_EXPERTS].set(router_b)
    xn, hp, meta, gate4, cnt = _merge(
        y_ret, y_diff, p_lat, x2, g_a, sh_f, sc_f, norm2_g.reshape(1, D_MODEL),
        w_br_ret.astype(BF16), w_br_diff.astype(BF16), w_out.astype(BF16), rw_hi, rw_lo, rb, seq)

    counts = cnt[0, :N_EXPERTS].astype(I32)
    padded = (counts + MOE_BLK - 1) // MOE_BLK * MOE_BLK
    pad_end = jnp.cumsum(padded)
    pad_start = pad_end - padded
    n_pairs = n_tok * TOP_K
    n_blocks = n_pairs // MOE_BLK + N_EXPERTS
    meta_i = meta.astype(I32)
    is_e = meta_i[None, :TOP_K] == jnp.arange(N_EXPERTS, dtype=I32)[:, None, None]
    dest = (jnp.sum(jnp.where(is_e, pad_start[:, None, None], 0), axis=0) + meta_i[TOP_K:]).reshape(n_pairs)
    block_start = jnp.arange(n_blocks, dtype=I32) * MOE_BLK
    block_e = jnp.minimum(jnp.sum((pad_end[None, :] <= block_start[:, None]).astype(I32), axis=1), N_EXPERTS - 1)
    n_used = (pad_end[-1] // MOE_BLK).reshape(1).astype(I32)

    xb = _dispatch(dest, hp, jnp.zeros((n_blocks * MOE_BLK, D_MODEL // 2), U32))
    yb = _experts(block_e, n_used, xb, exp_w1, exp_b1, exp_w2, exp_b2)
    out = _combine(dest, yb, gate4, xn, g_f, seq)
    return out.reshape(batch, seq, D_MODEL)


def kernel(x, c, ctx, c_ctx, norm1_g, norm2_g, w_mod, b_mod, w_in, ret_decay_logit, ret_norm_g, diff_q_norm_g, diff_k_norm_g, diff_lambda, diff_norm_g, w_br_ret, w_br_diff, w_out, router_w, router_b, exp_w1, exp_b1, exp_w2, exp_b2):
    assert norm1_g.shape[0] == 1, "single-layer block"
    return _layer(x, ctx, c, c_ctx, norm1_g[0], norm2_g[0], w_mod[0], b_mod[0], w_in[0], ret_decay_logit[0],
                  ret_norm_g[0], diff_q_norm_g[0], diff_k_norm_g[0], diff_lambda[0], diff_norm_g[0],
                  w_br_ret[0], w_br_diff[0], w_out[0], router_w[0], router_b[0],
                  exp_w1[0], exp_b1[0], exp_w2[0], exp_b2[0])
```

```python
import functools
import math

import numpy as np

import jax
import jax.numpy as jnp
from jax import lax
from jax.experimental import pallas as pl
from jax.experimental.pallas import tpu as pltpu

F32 = jnp.float32
BF16 = jnp.bfloat16
U32 = jnp.uint32
I32 = jnp.int32

D_MODEL = 1024
GRID_W = 64
RET_HEADS = 4
RET_DK = 256
RET_DV = 512
DIFF_DH = 64
DIFF_HEADS = 8
DIFF_DV = 128
N_EXPERTS = 32
TOP_K = 4
D_FF = 1024
SWIGLU_LIMIT = 7.0
SWIGLU_ALPHA = 1.702
ROPE_BASE = 10000.0
EPS = 1e-6
LAMBDA_INIT = 0.8 - 0.6 * math.exp(-0.3 * 0)

IN_COLS = 11264
COL_TILE = 1024
LANES = 128
SUBLANES = 8
MOE_BLK = 256
NEG_BIG = -1e30
LOG2E = 1.4426950408889634
SHIFT_SLACK = 1.0 + 2.0 ** -6
MAX_SAFE_SHIFT = 60.0
RET_SUB = 4
VT_PAD = 16
KV_UNROLL = 8
HIGHEST = lax.Precision.HIGHEST
MIB = 1024 * 1024


def _params(sem, vmem_mib):
    return pltpu.CompilerParams(dimension_semantics=sem, vmem_limit_bytes=vmem_mib * MIB)


def _mod_kernel(c_ref, w_ref, b_ref, o_ref):
    c = c_ref[...]
    s = c * jax.nn.sigmoid(c)
    o_ref[...] = jnp.dot(s, w_ref[...], preferred_element_type=F32, precision=HIGHEST) + b_ref[...]


def _mod(cc, w_mod, b_mod):
    n = w_mod.shape[1]
    tn = 1024
    return pl.pallas_call(
        _mod_kernel,
        out_shape=jax.ShapeDtypeStruct((8, n), F32),
        grid=(n // tn,),
        in_specs=[pl.BlockSpec((8, D_MODEL), lambda j: (0, 0)),
                  pl.BlockSpec((D_MODEL, tn), lambda j: (0, j)),
                  pl.BlockSpec((1, tn), lambda j: (0, j))],
        out_specs=pl.BlockSpec((8, tn), lambda j: (0, j)),
        compiler_params=_params(("arbitrary",), 32),
        name="mod",
    )(cc, w_mod, b_mod.reshape(1, n))


def _inproj_kernel(x_ref, g_ref, sh_ref, sc_ref, w_ref, qkg_ref, gmat_ref,
                   cr_ref, sr_ref, cc_ref, sc2_ref, cd_ref, sa_ref, sb_ref,
                   o_ref, h_scr, acc_scr):
    j = pl.program_id(1)

    @pl.when(j == 0)
    def _():
        xf = x_ref[...]
        ms = jnp.mean(xf * xf, axis=-1, keepdims=True)
        y = xf * lax.rsqrt(ms + EPS) * g_ref[...]
        h_scr[...] = (y * (1.0 + sc_ref[0]) + sh_ref[0]).astype(BF16)

    acc_scr[...] = jnp.dot(h_scr[...], w_ref[...], preferred_element_type=F32)
    is_ret = j <= 1
    is_dqk = (j == 6) | (j == 7)

    @pl.when(is_ret)
    def _():
        scale = jnp.where(j == 0, RET_DK ** -0.5, 1.0).astype(F32)
        for b in range(COL_TILE // LANES):
            xb = acc_scr[:, b * LANES:(b + 1) * LANES]
            cos = cr_ref[...] if b % 2 == 0 else cc_ref[...]
            sin = sr_ref[...] if b % 2 == 0 else sc2_ref[...]
            o = (xb * cos + pltpu.roll(xb, 64, 1) * sin) * scale
            o_ref[:, b * LANES:(b + 1) * LANES] = o.astype(BF16)

    @pl.when(is_dqk)
    def _():
        g = jnp.where(j == 6, qkg_ref[0:1, :], qkg_ref[1:2, :])
        for b in range(COL_TILE // LANES):
            xb = acc_scr[:, b * LANES:(b + 1) * LANES]
            ss = jnp.dot((xb * xb).astype(BF16), gmat_ref[...], preferred_element_type=F32)
            yn = xb * lax.rsqrt(ss * (1.0 / DIFF_DH) + EPS) * g
            o = yn * cd_ref[...] + pltpu.roll(yn, 16, 1) * sa_ref[...] + pltpu.roll(yn, 112, 1) * sb_ref[...]
            o_ref[:, b * LANES:(b + 1) * LANES] = o.astype(BF16)

    @pl.when(jnp.logical_not(is_ret | is_dqk))
    def _():
        o_ref[...] = acc_scr[...].astype(BF16)


def _inproj(x2, g1, sh, sc, w_bf, qkg, gmat, tables, tm, tiles_per_batch):
    n = x2.shape[0]
    nb = sh.shape[0]
    tab_spec = pl.BlockSpec((tm, LANES), lambda i, j: (i % tiles_per_batch, 0))
    mod_spec = pl.BlockSpec((1, 1, D_MODEL), lambda i, j: (jnp.minimum(i // tiles_per_batch, nb - 1), 0, 0))
    return pl.pallas_call(
        _inproj_kernel,
        out_shape=jax.ShapeDtypeStruct((n, IN_COLS), BF16),
        grid=(n // tm, IN_COLS // COL_TILE),
        in_specs=[pl.BlockSpec((tm, D_MODEL), lambda i, j: (i, 0)),
                  pl.BlockSpec((1, D_MODEL), lambda i, j: (0, 0)),
                  mod_spec, mod_spec,
                  pl.BlockSpec((D_MODEL, COL_TILE), lambda i, j: (0, j)),
                  pl.BlockSpec((8, LANES), lambda i, j: (0, 0)),
                  pl.BlockSpec((LANES, LANES), lambda i, j: (0, 0))] + [tab_spec] * 7,
        out_specs=pl.BlockSpec((tm, COL_TILE), lambda i, j: (i, j)),
        scratch_shapes=[pltpu.VMEM((tm, D_MODEL), BF16), pltpu.VMEM((tm, COL_TILE), F32)],
        compiler_params=_params(("arbitrary", "arbitrary"), 48),
        name="inproj",
    )(x2, g1, sh, sc, w_bf, qkg, gmat, *tables)


def _rope_tables(seq):
    n_rows = seq // GRID_W
    f32 = np.float32

    def angles(pos, half):
        inv = f32(ROPE_BASE) ** (-np.arange(half, dtype=f32) / f32(half))
        return (pos.astype(f32)[:, None] * inv[None, :]).astype(np.float64)

    ar, ac = angles(np.arange(n_rows), 64), angles(np.arange(GRID_W), 64)
    br, bc = angles(np.arange(n_rows), 16), angles(np.arange(GRID_W), 16)
    zr, zc = np.zeros_like(br), np.zeros_like(bc)
    cat = lambda parts, reps=1: np.tile(np.concatenate(parts, axis=1), (1, reps)).astype(f32)
    by_row = lambda t: jnp.repeat(jnp.asarray(t), GRID_W, axis=0)
    by_col = lambda t: jnp.tile(jnp.asarray(t), (n_rows, 1))
    cr = by_row(cat([np.cos(ar), np.cos(ar)]))
    sr = by_row(cat([-np.sin(ar), np.sin(ar)]))
    cc = by_col(cat([np.cos(ac), np.cos(ac)]))
    sc = by_col(cat([-np.sin(ac), np.sin(ac)]))
    cd = by_row(cat([np.cos(br), np.cos(br), zr, zr], 2)) + by_col(cat([zc, zc, np.cos(bc), np.cos(bc)], 2))
    sa = by_row(cat([zr, np.sin(br), zr, zr], 2)) + by_col(cat([zc, zc, zc, np.sin(bc)], 2))
    sb = by_row(cat([-np.sin(br), zr, zr, zr], 2)) + by_col(cat([zc, zc, -np.sin(bc), zc], 2))
    return [cr, sr, cc, sc, cd, sa, sb]


def _identity_tables(seq):
    one = jnp.ones((seq, LANES), F32)
    zero = jnp.zeros((seq, LANES), F32)
    return [one, zero, one, zero, one, zero, zero]


def _tn_dot(a, b):
    return lax.dot_general(a, b, (((0,), (0,)), ((), ())), preferred_element_type=F32)


def _nt_dot(a, b):
    return lax.dot_general(a, b, (((1,), (1,)), ((), ())), preferred_element_type=F32)


def _ret_kernel(lg_ref, q_ref, k_ref, v_ref, g_ref, ck_ref, cv_ref, gn_ref, o_ref,
                sf_scr, sb_scr, ob_scr, *, chunk, sub, n_blocks, ctx_len):
    h = pl.program_id(1)
    p = pl.program_id(2)
    c = pl.program_id(3)
    lgf = lg_ref[0, h]
    lgb = lg_ref[1, h]

    def col_iota(n):
        return lax.broadcasted_iota(I32, (n, 1), 0).astype(F32)

    def vexp(s):
        return jnp.exp(jnp.zeros((1, 1), F32) + s)

    @pl.when((p == 0) & (c == 0))
    def _():
        jc = col_iota(ctx_len)
        kc = ck_ref[...].astype(F32)
        vc = cv_ref[...]
        sf_scr[...] = _tn_dot((kc * jnp.exp(lgf * (ctx_len - 1.0 - jc))).astype(BF16), vc)
        sb_scr[...] = _tn_dot((kc * jnp.exp(lgb * jc)).astype(BF16), vc)

    ic = col_iota(chunk)

    @pl.when(p == 0)
    def _():
        q_decay = jnp.exp(lgb * (chunk - ic))
        k_decay = jnp.exp(lgb * ic)
        s_decay = vexp(lgb * chunk)
        for j in reversed(range(sub)):
            loc = pl.ds(j * chunk, chunk)
            glob = pl.ds(pl.multiple_of(((n_blocks - 1 - c) * sub + j) * chunk, chunk), chunk)
            qb = (q_ref[loc, :].astype(F32) * q_decay).astype(BF16)
            ob_scr[glob, :] = jnp.dot(qb, sb_scr[...].astype(BF16), preferred_element_type=F32)
            kb = (k_ref[loc, :].astype(F32) * k_decay).astype(BF16)
            sb_scr[...] = s_decay * sb_scr[...] + _tn_dot(kb, v_ref[loc, :])

    @pl.when(p == 1)
    def _():
        ri = lax.broadcasted_iota(I32, (chunk, chunk), 0)
        ci = lax.broadcasted_iota(I32, (chunk, chunk), 1)
        d = (ri - ci).astype(F32)
        mask = jnp.where(d > 0, jnp.exp(lgf * jnp.maximum(d, 0.0)),
                         jnp.where(d < 0, jnp.exp(lgb * jnp.maximum(-d, 0.0)), 2.0))
        q_decay = jnp.exp(lgf * (ic + 1.0))
        k_decay = jnp.exp(lgf * (chunk - 1.0 - ic))
        s_decay = vexp(lgf * chunk)
        for j in range(sub):
            loc = pl.ds(j * chunk, chunk)
            glob = pl.ds(pl.multiple_of((c * sub + j) * chunk, chunk), chunk)
            a = (_nt_dot(q_ref[loc, :], k_ref[loc, :]) * mask).astype(BF16)
            qf = (q_ref[loc, :].astype(F32) * q_decay).astype(BF16)
            o = (jnp.dot(a, v_ref[loc, :], preferred_element_type=F32)
                 + jnp.dot(qf, sf_scr[...].astype(BF16), preferred_element_type=F32)
                 + ob_scr[glob, :])
            kf = (k_ref[loc, :].astype(F32) * k_decay).astype(BF16)
            sf_scr[...] = s_decay * sf_scr[...] + _tn_dot(kf, v_ref[loc, :])
            ms = jnp.mean(o * o, axis=-1, keepdims=True)
            y = o * lax.rsqrt(ms + EPS) * gn_ref[0]
            gt = g_ref[loc, :].astype(F32)
            o_ref[loc, :] = (y * (gt * jax.nn.sigmoid(gt))).astype(BF16)


def _retention(lg, p_lat, p_ctx, ret_norm_g, batch, seq, ctx_len):
    chunk = min(256, seq)
    sub = math.gcd(seq // chunk, RET_SUB)
    blk = chunk * sub
    nb = seq // blk
    kern = functools.partial(_ret_kernel, chunk=chunk, sub=sub, n_blocks=nb, ctx_len=ctx_len)

    def rows(b, p, c):
        return b * nb + jnp.where(p == 0, nb - 1 - c, c)

    def rows_fwd(b, p, c):
        return b * nb + jnp.where(p == 0, 0, c)

    return pl.pallas_call(
        kern,
        out_shape=jax.ShapeDtypeStruct((batch * seq, RET_HEADS * RET_DV), BF16),
        grid=(batch, RET_HEADS, 2, nb),
        in_specs=[pl.BlockSpec(memory_space=pltpu.SMEM),
                  pl.BlockSpec((blk, RET_DK), lambda b, h, p, c: (rows(b, p, c), h)),
                  pl.BlockSpec((blk, RET_DK), lambda b, h, p, c: (rows(b, p, c), 4 + h)),
                  pl.BlockSpec((blk, RET_DV), lambda b, h, p, c: (rows(b, p, c), 4 + h)),
                  pl.BlockSpec((blk, RET_DV), lambda b, h, p, c: (rows_fwd(b, p, c), 8 + h)),
                  pl.BlockSpec((ctx_len, RET_DK), lambda b, h, p, c: (b, 4 + h)),
                  pl.BlockSpec((ctx_len, RET_DV), lambda b, h, p, c: (b, 4 + h)),
                  pl.BlockSpec((1, 1, RET_DV), lambda b, h, p, c: (h, 0, 0))],
        out_specs=pl.BlockSpec((blk, RET_DV), lambda b, h, p, c: (rows_fwd(b, p, c), h)),
        scratch_shapes=[pltpu.VMEM((RET_DK, RET_DV), F32), pltpu.VMEM((RET_DK, RET_DV), F32),
                        pltpu.VMEM((seq, RET_DV), F32)],
        compiler_params=_params(("arbitrary",) * 4, 48),
        name="ret",
    )(lg, p_lat, p_lat, p_lat, p_lat, p_ctx, p_ctx, ret_norm_g.reshape(RET_HEADS, 1, RET_DV))


def _dattn_kernel(lam_ref, q_ref, ck_ref, cv_ref, k_ref, v_ref, gn_ref, gmat_ref, o_ref,
                  qq_scr, kmax_scr, mp_scr, kp_scr, vt_scr, pt_scr, *, tq, tk, n_kv, ta, unroll, ctx_len):
    rows = 2 * tq
    n_all = kp_scr.shape[0]
    qi = pl.program_id(2)

    def sq_norms(t):
        tf = t.astype(F32)
        return jnp.dot((tf * tf).astype(BF16), gmat_ref[...], preferred_element_type=F32)

    def kv_slice(ref, c):
        return ref[pl.ds(pl.multiple_of(c * tk, tk), tk), :]

    def lane_const(n, hot):
        return jnp.where(lax.broadcasted_iota(I32, (n, LANES), 1) < hot, 1.0, 0.0).astype(BF16)

    @pl.when(qi == 0)
    def _():
        kmax_scr[...] = jnp.max(sq_norms(ck_ref[...]), axis=0, keepdims=True)
        kp_scr[:, LANES:2 * LANES] = lane_const(n_all, 3)
        kp_scr[0:ctx_len, 0:LANES] = ck_ref[...]
        tail = lax.broadcasted_iota(I32, (VT_PAD, n_all), 0)
        vt_scr[DIFF_DV:, :] = jnp.where(tail == 0, 1.0, 0.0).astype(BF16)
        vt_scr[0:DIFF_DV, 0:ctx_len] = cv_ref[...].astype(F32).T.astype(BF16)

        def body(c, carry):
            start = pl.multiple_of(c * tk, tk)
            k = k_ref[pl.ds(start, tk), :]
            kmax_scr[...] = jnp.maximum(kmax_scr[...], jnp.max(sq_norms(k), axis=0, keepdims=True))
            kp_scr[pl.ds(pl.multiple_of(ctx_len + start, LANES), tk), 0:LANES] = k
            vt_scr[0:DIFF_DV, pl.ds(pl.multiple_of(ctx_len + start, LANES), tk)] = (
                v_ref[pl.ds(start, tk), :].astype(F32).T.astype(BF16))
            return carry

        lax.fori_loop(0, n_kv, body, 0)

    q = q_ref[...]
    lane = lax.broadcasted_iota(I32, (tq, LANES), 1)
    z = jnp.zeros_like(q)
    qq_scr[0:tq, 0:LANES] = jnp.where(lane < DIFF_DH, q, z)
    qq_scr[tq:rows, 0:LANES] = jnp.where(lane >= DIFF_DH, q, z)

    def set_shift(shift):
        neg = -shift
        hi = neg.astype(BF16)
        r1 = neg - hi.astype(F32)
        mid = r1.astype(BF16)
        lo = (r1 - mid.astype(F32)).astype(BF16)
        lane_r = lax.broadcasted_iota(I32, (rows, LANES), 1)
        pieces = jnp.where(lane_r == 0, hi.astype(F32), jnp.where(lane_r == 1, mid.astype(F32),
                           jnp.where(lane_r == 2, lo.astype(F32), 0.0)))
        qq_scr[:, LANES:2 * LANES] = pieces.astype(BF16)

    bound = jnp.sqrt(sq_norms(q) * kmax_scr[...]) * SHIFT_SLACK
    b1 = jnp.max(jnp.where(lane < DIFF_DH, bound, 0.0), axis=1, keepdims=True)
    b2 = jnp.max(jnp.where(lane >= DIFF_DH, bound, 0.0), axis=1, keepdims=True)
    set_shift(jnp.concatenate([b1, b2], axis=0))

    @pl.when(jnp.max(bound) > MAX_SAFE_SHIFT)
    def _():
        def tile_max(k):
            s = _nt_dot(qq_scr[:, 0:LANES], k)
            mp = mp_scr[...]
            for cb in range(k.shape[0] // LANES):
                mp = jnp.maximum(mp, s[:, cb * LANES:(cb + 1) * LANES])
            mp_scr[...] = mp

        mp_scr[...] = jnp.full(mp_scr.shape, NEG_BIG, F32)
        tile_max(ck_ref[...])

        def body(c, carry):
            tile_max(kv_slice(k_ref, c))
            return carry

        lax.fori_loop(0, n_kv, body, 0)
        set_shift(jnp.max(mp_scr[...], axis=1, keepdims=True))

    def body_a(t, carry):
        for u in range(unroll):
            r = pl.ds(pl.multiple_of((t * unroll + u) * ta, ta), ta)
            pt_scr[r, :] = jnp.exp2(_nt_dot(kp_scr[r, :], qq_scr[...])).astype(BF16)
        return carry

    lax.fori_loop(0, n_all // (ta * unroll), body_a, 0)

    acc = jnp.dot(vt_scr[...], pt_scr[...], preferred_element_type=F32)
    ot = acc[0:DIFF_DV, :] / acc[DIFF_DV:DIFF_DV + 1, :]
    d = (ot[:, 0:tq] - lam_ref[0] * ot[:, tq:rows]).T
    ms = jnp.mean(d * d, axis=-1, keepdims=True)
    y = d * lax.rsqrt(ms + EPS) * gn_ref[...] * (1.0 - LAMBDA_INIT)
    o_ref[...] = y.astype(BF16)


def _diff_attention(lam, p_lat, p_ctx, diff_norm_g, gmat, batch, seq, ctx_len):
    tq = min(256, seq)
    tk = min(512, seq)
    nq, nk = seq // tq, seq // tk
    n_all = ctx_len + seq
    ta = next(t for t in (528, 512, 384, 320, 256, 128) if n_all % t == 0)
    unroll = math.gcd(n_all // ta, KV_UNROLL)
    assert ctx_len % LANES == 0
    kern = functools.partial(_dattn_kernel, tq=tq, tk=tk, n_kv=nk, ta=ta, unroll=unroll, ctx_len=ctx_len)
    return pl.pallas_call(
        kern,
        out_shape=jax.ShapeDtypeStruct((batch * seq, DIFF_HEADS * DIFF_DV), BF16),
        grid=(batch, DIFF_HEADS, nq),
        in_specs=[pl.BlockSpec(memory_space=pltpu.SMEM),
                  pl.BlockSpec((tq, LANES), lambda b, h, qi: (b * nq + qi, 48 + h)),
                  pl.BlockSpec((ctx_len, LANES), lambda b, h, qi: (b, 56 + h)),
                  pl.BlockSpec((ctx_len, LANES), lambda b, h, qi: (b, 64 + h)),
                  pl.BlockSpec((seq, LANES), lambda b, h, qi: (b, 56 + h)),
                  pl.BlockSpec((seq, LANES), lambda b, h, qi: (b, 64 + h)),
                  pl.BlockSpec((1, LANES), lambda b, h, qi: (0, 0)),
                  pl.BlockSpec((LANES, LANES), lambda b, h, qi: (0, 0))],
        out_specs=pl.BlockSpec((tq, LANES), lambda b, h, qi: (b * nq + qi, h)),
        scratch_shapes=[pltpu.VMEM((2 * tq, 2 * LANES), BF16), pltpu.VMEM((1, LANES), F32),
                        pltpu.VMEM((2 * tq, LANES), F32), pltpu.VMEM((n_all, 2 * LANES), BF16),
                        pltpu.VMEM((DIFF_DV + VT_PAD, n_all), BF16), pltpu.VMEM((n_all, 2 * tq), BF16)],
        compiler_params=_params(("arbitrary",) * 3, 48),
        name="dattn",
    )(lam, p_lat, p_ctx, p_ctx, p_lat, p_lat, diff_norm_g.reshape(1, DIFF_DV), gmat)


def _merge_kernel(yr_ref, yd_ref, ga_ref, gb_ref, x_ref, gatea_ref, shf_ref, scf_ref, n2_ref,
                  wr_ref, wd_ref, wo_ref, rwh_ref, rwl_ref, rb_ref,
                  xn_ref, hp_ref, meta_ref, gate_ref, cnt_ref, run_scr, *, tm):
    i = pl.program_id(0)

    @pl.when(i == 0)
    def _():
        run_scr[...] = jnp.zeros(run_scr.shape, F32)

    yr = jnp.dot(yr_ref[...], wr_ref[...], preferred_element_type=F32)
    yd = jnp.dot(yd_ref[...], wd_ref[...], preferred_element_type=F32)
    m = (jax.nn.sigmoid(ga_ref[...].astype(F32)) * yr + jax.nn.sigmoid(gb_ref[...].astype(F32)) * yd)
    z = jnp.dot(m.astype(BF16), wo_ref[...], preferred_element_type=F32)
    xn = x_ref[...] + gatea_ref[0] * z
    xn_ref[...] = xn

    ms = jnp.mean(xn * xn, axis=-1, keepdims=True)
    h2 = xn * lax.rsqrt(ms + EPS) * n2_ref[...]
    h2 = h2 * (1.0 + scf_ref[0]) + shf_ref[0]
    h_hi = h2.astype(BF16)
    bits = pltpu.bitcast(h_hi.astype(F32), U32)
    half = D_MODEL // 2
    hp_ref[...] = (bits[:, :half] >> 16) | (bits[:, half:] & jnp.uint32(0xFFFF0000))

    h_lo = (h2 - h_hi.astype(F32)).astype(BF16)
    logits = (jnp.dot(h_hi, rwh_ref[...], preferred_element_type=F32)
              + jnp.dot(h_lo, rwh_ref[...], preferred_element_type=F32)
              + jnp.dot(h_hi, rwl_ref[...], preferred_element_type=F32) + rb_ref[...])
    lane = lax.broadcasted_iota(I32, (tm, LANES), 1)
    lanef = lane.astype(F32)
    work = logits
    vals, idxs = [], []
    for _ in range(TOP_K):
        mk = jnp.max(work, axis=1, keepdims=True)
        ik = jnp.min(jnp.where(work == mk, lanef, float(LANES)), axis=1, keepdims=True)
        vals.append(mk)
        idxs.append(ik)
        work = jnp.where(lanef == ik, -jnp.inf, work)
    ex = [jnp.exp(v - vals[0]) for v in vals]
    den = ex[0] + ex[1] + ex[2] + ex[3]

    onehot = jnp.zeros((tm, LANES), F32)
    for ik in idxs:
        onehot = onehot + jnp.where(lanef == ik, 1.0, 0.0)
    ri = lax.broadcasted_iota(I32, (tm, tm), 0)
    ci = lax.broadcasted_iota(I32, (tm, tm), 1)
    tri = jnp.where(ri > ci, 1.0, 0.0).astype(BF16)
    base = run_scr[0:1, :] + jnp.dot(tri, onehot.astype(BF16), preferred_element_type=F32)
    run_scr[...] = run_scr[...] + jnp.sum(onehot, axis=0, keepdims=True)

    meta = jnp.zeros((tm, LANES), F32)
    gate_out = jnp.zeros((tm, LANES), F32)
    for k in range(TOP_K):
        rk = jnp.sum(jnp.where(lanef == idxs[k], base, 0.0), axis=1, keepdims=True)
        meta = jnp.where(lane == k, idxs[k], jnp.where(lane == TOP_K + k, rk, meta))
        gate_out = jnp.where(lane == k, ex[k] / den, gate_out)
    meta_ref[...] = meta.T[0:2 * TOP_K, :]
    gate_ref[...] = gate_out
    cnt_ref[...] = run_scr[...]


def _merge(y_ret, y_diff, p_lat, x2, g_a, sh_f, sc_f, norm2_g, w_r, w_d, w_o, rw_hi, rw_lo, rb, seq):
    n = x2.shape[0]
    tm = min(512, seq)
    tpb = seq // tm
    kern = functools.partial(_merge_kernel, tm=tm)
    mod_spec = pl.BlockSpec((1, 1, D_MODEL), lambda i: (i // tpb, 0, 0))
    const = lambda shape: pl.BlockSpec(shape, lambda i: (0,) * len(shape))
    tok = lambda w: pl.BlockSpec((tm, w), lambda i: (i, 0))
    return pl.pallas_call(
        kern,
        out_shape=(jax.ShapeDtypeStruct((n, D_MODEL), F32),
                   jax.ShapeDtypeStruct((n, D_MODEL // 2), U32),
                   jax.ShapeDtypeStruct((2 * TOP_K, n), F32),
                   jax.ShapeDtypeStruct((n, LANES), F32),
                   jax.ShapeDtypeStruct((8, LANES), F32)),
        grid=(n // tm,),
        in_specs=[tok(RET_HEADS * RET_DV), tok(D_MODEL),
                  pl.BlockSpec((tm, COL_TILE), lambda i: (i, 9)),
                  pl.BlockSpec((tm, COL_TILE), lambda i: (i, 10)),
                  tok(D_MODEL), mod_spec, mod_spec, mod_spec, const((1, D_MODEL)),
                  const((RET_HEADS * RET_DV, D_MODEL)), const((D_MODEL, D_MODEL)), const((D_MODEL, D_MODEL)),
                  const((D_MODEL, LANES)), const((D_MODEL, LANES)), const((1, LANES))],
        out_specs=(tok(D_MODEL), tok(D_MODEL // 2), pl.BlockSpec((2 * TOP_K, tm), lambda i: (0, i)),
                   tok(LANES), const((8, LANES))),
        scratch_shapes=[pltpu.VMEM((8, LANES), F32)],
        compiler_params=_params(("arbitrary",), 56),
        name="merge",
    )(y_ret, y_diff, p_lat, p_lat, x2, g_a, sh_f, sc_f, norm2_g, w_r, w_d, w_o, rw_hi, rw_lo, rb)


def _dispatch_kernel(dest_ref, hp_ref, xb_in_ref, xb_ref, sem, *, tm, n_tok):
    del xb_in_ref
    i = pl.program_id(0)

    def body(g, carry):
        r0 = pl.multiple_of(g * SUBLANES, SUBLANES)
        for j in range(SUBLANES):
            for k in range(TOP_K):
                d = dest_ref[k * n_tok + i * tm + r0 + j]
                pltpu.make_async_copy(hp_ref.at[pl.ds(r0 + j, 1), :], xb_ref.at[pl.ds(d, 1), :], sem).start()
        return carry

    lax.fori_loop(0, tm // SUBLANES, body, 0)
    for _ in range(TOP_K):
        pltpu.make_async_copy(hp_ref, xb_ref.at[pl.ds(0, tm), :], sem).wait()


def _dispatch(dest, hp, xb_zero):
    n = hp.shape[0]
    tm = min(512, n)
    kern = functools.partial(_dispatch_kernel, tm=tm, n_tok=n)
    return pl.pallas_call(
        kern,
        out_shape=jax.ShapeDtypeStruct(xb_zero.shape, U32),
        grid_spec=pltpu.PrefetchScalarGridSpec(
            num_scalar_prefetch=1,
            grid=(n // tm,),
            in_specs=[pl.BlockSpec((tm, D_MODEL // 2), lambda i, d: (i, 0)),
                      pl.BlockSpec(memory_space=pl.ANY)],
            out_specs=pl.BlockSpec(memory_space=pl.ANY),
            scratch_shapes=[pltpu.SemaphoreType.DMA(())]),
        input_output_aliases={2: 0},
        compiler_params=_params(("arbitrary",), 32),
        name="dispatch",
    )(dest, hp, xb_zero)


def _expert_kernel(be_ref, nb_ref, nxt_ref, xb_ref, w1_hbm, b1_ref, w2_hbm, b2_ref, y_ref,
                   w1f_scr, w2f_scr, w1b_scr, w2b_scr, grp_scr, sems):
    i = pl.program_id(0)
    e = be_ref[i]
    prev = be_ref[jnp.maximum(i - 1, 0)]
    active = i < nb_ref[0]

    def weight_copies(expert, slot):
        return (pltpu.make_async_copy(w1_hbm.at[expert], w1f_scr.at[slot], sems.at[0, slot]),
                pltpu.make_async_copy(w2_hbm.at[expert], w2f_scr.at[slot], sems.at[1, slot]))

    @pl.when(i == 0)
    def _():
        grp_scr[0] = 0
        for cp in weight_copies(e, 0):
            cp.start()

    @pl.when(active & (i > 0) & (e != prev))
    def _():
        grp_scr[0] = grp_scr[0] + 1

    @pl.when(active & ((i == 0) | (e != prev)))
    def _():
        slot = grp_scr[0] % 2
        for cp in weight_copies(e, slot):
            cp.wait()
        w1b_scr[...] = w1f_scr[slot].astype(BF16)
        w2b_scr[...] = w2f_scr[slot].astype(BF16)
        nxt = nxt_ref[e]

        @pl.when(nxt >= 0)
        def _():
            for cp in weight_copies(nxt, 1 - slot):
                cp.start()

    @pl.when(active)
    def _():
        xu = xb_ref[...]
        x_lo = pltpu.bitcast(xu << 16, F32).astype(BF16)
        x_hi = pltpu.bitcast(xu & jnp.uint32(0xFFFF0000), F32).astype(BF16)
        half = D_MODEL // 2
        hh = (jnp.dot(x_lo, w1b_scr[0:half, :], preferred_element_type=F32)
              + jnp.dot(x_hi, w1b_scr[half:, :], preferred_element_type=F32) + b1_ref[0])
        glu = jnp.minimum(hh[:, :D_FF], SWIGLU_LIMIT)
        lin = jnp.clip(hh[:, D_FF:], -SWIGLU_LIMIT, SWIGLU_LIMIT)
        act = glu * jax.nn.sigmoid(SWIGLU_ALPHA * glu) * (lin + 1.0)
        y_ref[...] = jnp.dot(act.astype(BF16), w2b_scr[...], preferred_element_type=F32) + b2_ref[0]

    @pl.when(i >= nb_ref[0])
    def _():
        y_ref[...] = jnp.zeros(y_ref.shape, F32)


def _experts(block_e, n_used, next_e, xb, w1, b1, w2, b2):
    rows = xb.shape[0]
    n_blocks = rows // MOE_BLK
    return pl.pallas_call(
        _expert_kernel,
        out_shape=jax.ShapeDtypeStruct((rows, D_MODEL), F32),
        grid_spec=pltpu.PrefetchScalarGridSpec(
            num_scalar_prefetch=3,
            grid=(n_blocks,),
            in_specs=[pl.BlockSpec((MOE_BLK, D_MODEL // 2), lambda i, be, nb, nx: (i, 0)),
                      pl.BlockSpec(memory_space=pl.ANY),
                      pl.BlockSpec((1, 1, 2 * D_FF), lambda i, be, nb, nx: (be[i], 0, 0)),
                      pl.BlockSpec(memory_space=pl.ANY),
                      pl.BlockSpec((1, 1, D_MODEL), lambda i, be, nb, nx: (be[i], 0, 0))],
            out_specs=pl.BlockSpec((MOE_BLK, D_MODEL), lambda i, be, nb, nx: (i, 0)),
            scratch_shapes=[pltpu.VMEM((2, D_MODEL, 2 * D_FF), F32), pltpu.VMEM((2, D_FF, D_MODEL), F32),
                            pltpu.VMEM((D_MODEL, 2 * D_FF), BF16), pltpu.VMEM((D_FF, D_MODEL), BF16),
                            pltpu.SMEM((1,), I32), pltpu.SemaphoreType.DMA((2, 2))]),
        compiler_params=_params(("arbitrary",), 56),
        name="expert",
    )(block_e, n_used, next_e, xb, w1, b1.reshape(N_EXPERTS, 1, 2 * D_FF), w2, b2.reshape(N_EXPERTS, 1, D_MODEL))


def _combine_kernel(dest_ref, yb_ref, gate_ref, xn_ref, gf_ref, o_ref, buf, sem, *, tm, n_tok):
    i = pl.program_id(0)

    def body(g, carry):
        r0 = pl.multiple_of(g * SUBLANES, SUBLANES)
        for j in range(SUBLANES):
            for k in range(TOP_K):
                d = dest_ref[k * n_tok + i * tm + r0 + j]
                pltpu.make_async_copy(yb_ref.at[pl.ds(d, 1), :], buf.at[k, pl.ds(r0 + j, 1), :], sem).start()
        return carry

    lax.fori_loop(0, tm // SUBLANES, body, 0)
    for k in range(TOP_K):
        pltpu.make_async_copy(yb_ref.at[pl.ds(0, tm), :], buf.at[k], sem).wait()
    g = gate_ref[...]
    y = g[:, 0:1] * buf[0]
    for k in range(1, TOP_K):
        y = y + g[:, k:k + 1] * buf[k]
    o_ref[...] = xn_ref[...] + gf_ref[0] * y


def _combine(dest, yb, gate, xn, g_f, seq):
    n = xn.shape[0]
    tm = min(256, seq)
    tpb = seq // tm
    kern = functools.partial(_combine_kernel, tm=tm, n_tok=n)
    return pl.pallas_call(
        kern,
        out_shape=jax.ShapeDtypeStruct((n, D_MODEL), F32),
        grid_spec=pltpu.PrefetchScalarGridSpec(
            num_scalar_prefetch=1,
            grid=(n // tm,),
            in_specs=[pl.BlockSpec(memory_space=pl.ANY),
                      pl.BlockSpec((tm, LANES), lambda i, d: (i, 0)),
                      pl.BlockSpec((tm, D_MODEL), lambda i, d: (i, 0)),
                      pl.BlockSpec((1, 1, D_MODEL), lambda i, d: (i // tpb, 0, 0))],
            out_specs=pl.BlockSpec((tm, D_MODEL), lambda i, d: (i, 0)),
            scratch_shapes=[pltpu.VMEM((TOP_K, tm, D_MODEL), F32), pltpu.SemaphoreType.DMA(())]),
        compiler_params=_params(("arbitrary",), 32),
        name="combine",
    )(dest, yb, gate, xn, g_f)


def _layer(x, ctx, c, c_ctx, norm1_g, norm2_g, w_mod, b_mod, w_in, ret_decay_logit, ret_norm_g,
           diff_q_norm_g, diff_k_norm_g, diff_lambda, diff_norm_g, w_br_ret, w_br_diff, w_out,
           router_w, router_b, exp_w1, exp_b1, exp_w2, exp_b2):
    batch, seq, d = x.shape
    ctx_len = ctx.shape[1]
    assert d == D_MODEL and seq % GRID_W == 0 and batch + 1 <= 8
    n_tok = batch * seq

    cc = jnp.zeros((8, D_MODEL), F32).at[:batch].set(c).at[batch].set(c_ctx)
    mod = _mod(cc, w_mod, b_mod)
    sh_a, sc_a, g_a, sh_f, sc_f, g_f = [mod[:batch, i * D_MODEL:(i + 1) * D_MODEL].reshape(batch, 1, D_MODEL)
                                         for i in range(6)]
    csh_a = mod[batch:batch + 1, 0:D_MODEL].reshape(1, 1, D_MODEL)
    csc_a = mod[batch:batch + 1, D_MODEL:2 * D_MODEL].reshape(1, 1, D_MODEL)

    w_in_bf = w_in.astype(BF16)
    g1 = norm1_g.reshape(1, D_MODEL)
    tile = lambda g: jnp.tile(g.astype(F32), 2)
    qkg = jnp.zeros((8, LANES), F32).at[0].set(tile(diff_q_norm_g) * (DIFF_DH ** -0.5 * LOG2E)).at[1].set(tile(diff_k_norm_g))
    lane = jnp.arange(LANES)
    gmat = (lane[:, None] // DIFF_DH == lane[None, :] // DIFF_DH).astype(BF16)
    x2 = x.reshape(n_tok, D_MODEL)
    tm = min(1024, seq)
    p_lat = _inproj(x2, g1, sh_a, sc_a, w_in_bf, qkg, gmat, _rope_tables(seq), tm, seq // tm)
    p_ctx = _inproj(ctx.reshape(batch * ctx_len, D_MODEL), g1, csh_a, csc_a, w_in_bf, qkg, gmat,
                    _identity_tables(ctx_len), ctx_len, 1)

    lg = jax.nn.log_sigmoid(ret_decay_logit.astype(F32))
    y_ret = _retention(lg, p_lat, p_ctx, ret_norm_g, batch, seq, ctx_len)

    lp = diff_lambda.astype(F32)
    lam = (jnp.exp(jnp.sum(lp[0] * lp[1])) - jnp.exp(jnp.sum(lp[2] * lp[3])) + LAMBDA_INIT).reshape(1)
    y_diff = _diff_attention(lam, p_lat, p_ctx, diff_norm_g, gmat, batch, seq, ctx_len)

    rw = jnp.zeros((D_MODEL, LANES), F32).at[:, :N_EXPERTS].set(router_w)
    rw_hi = rw.astype(BF16)
    rw_lo = (rw - rw_hi.astype(F32)).astype(BF16)
    rb = jnp.full((1, LANES), NEG_BIG, F32).at[0, :N_EXPERTS].set(router_b)
    xn, hp, meta, gate4, cnt = _merge(
        y_ret, y_diff, p_lat, x2, g_a, sh_f, sc_f, norm2_g.reshape(1, D_MODEL),
        w_br_ret.astype(BF16), w_br_diff.astype(BF16), w_out.astype(BF16), rw_hi, rw_lo, rb, seq)

    counts = cnt[0, :N_EXPERTS].astype(I32)
    padded = (counts + MOE_BLK - 1) // MOE_BLK * MOE_BLK
    pad_end = jnp.cumsum(padded)
    pad_start = pad_end - padded
    n_pairs = n_tok * TOP_K
    n_blocks = n_pairs // MOE_BLK + N_EXPERTS
    meta_i = meta.astype(I32)
    is_e = meta_i[None, :TOP_K] == jnp.arange(N_EXPERTS, dtype=I32)[:, None, None]
    dest = (jnp.sum(jnp.where(is_e, pad_start[:, None, None], 0), axis=0) + meta_i[TOP_K:]).reshape(n_pairs)
    block_start = jnp.arange(n_blocks, dtype=I32) * MOE_BLK
    block_e = jnp.minimum(jnp.sum((pad_end[None, :] <= block_start[:, None]).astype(I32), axis=1), N_EXPERTS - 1)
    n_used = (pad_end[-1] // MOE_BLK).reshape(1).astype(I32)

    xb = _dispatch(dest, hp, jnp.zeros((n_blocks * MOE_BLK, D_MODEL // 2), U32))
    e_ids = jnp.arange(N_EXPERTS, dtype=I32)
    later = jnp.where((counts[None, :] > 0) & (e_ids[None, :] > e_ids[:, None]), e_ids[None, :], N_EXPERTS)
    next_e = jnp.min(later, axis=1)
    next_e = jnp.where(next_e == N_EXPERTS, -1, next_e).astype(I32)
    yb = _experts(block_e, n_used, next_e, xb, exp_w1, exp_b1, exp_w2, exp_b2)
    out = _combine(dest, yb, gate4, xn, g_f, seq)
    return out.reshape(batch, seq, D_MODEL)


def kernel(x, c, ctx, c_ctx, norm1_g, norm2_g, w_mod, b_mod, w_in, ret_decay_logit, ret_norm_g, diff_q_norm_g, diff_k_norm_g, diff_lambda, diff_norm_g, w_br_ret, w_br_diff, w_out, router_w, router_b, exp_w1, exp_b1, exp_w2, exp_b2):
    assert norm1_g.shape[0] == 1, "single-layer block"
    return _layer(x, ctx, c, c_ctx, norm1_g[0], norm2_g[0], w_mod[0], b_mod[0], w_in[0], ret_decay_logit[0],
                  ret_norm_g[0], diff_q_norm_g[0], diff_k_norm_g[0], diff_lambda[0], diff_norm_g[0],
                  w_br_ret[0], w_br_diff[0], w_out[0], router_w[0], router_b[0],
                  exp_w1[0], exp_b1[0], exp_w2[0], exp_b2[0])
```

```python
import functools
import math

import numpy as np

import jax
import jax.numpy as jnp
from jax import lax
from jax.experimental import pallas as pl
from jax.experimental.pallas import tpu as pltpu

F32 = jnp.float32
BF16 = jnp.bfloat16
U32 = jnp.uint32
I32 = jnp.int32

D_MODEL = 1024
GRID_W = 64
RET_HEADS = 4
RET_DK = 256
RET_DV = 512
DIFF_DH = 64
DIFF_HEADS = 8
DIFF_DV = 128
N_EXPERTS = 32
TOP_K = 4
D_FF = 1024
SWIGLU_LIMIT = 7.0
SWIGLU_ALPHA = 1.702
ROPE_BASE = 10000.0
EPS = 1e-6
LAMBDA_INIT = 0.8 - 0.6 * math.exp(-0.3 * 0)

IN_COLS = 11264
COL_TILE = 1024
LANES = 128
SUBLANES = 8
MOE_BLK = 256
NEG_BIG = -1e30
LOG2E = 1.4426950408889634
SHIFT_SLACK = 1.0 + 2.0 ** -6
MAX_SAFE_SHIFT = 60.0
RET_SUB = 4
VT_PAD = 16
KV_UNROLL = 8
HIGHEST = lax.Precision.HIGHEST
MIB = 1024 * 1024


def _params(sem, vmem_mib):
    return pltpu.CompilerParams(dimension_semantics=sem, vmem_limit_bytes=vmem_mib * MIB)


def _mod_kernel(c_ref, w_ref, b_ref, o_ref):
    c = c_ref[...]
    s = c * jax.nn.sigmoid(c)
    o_ref[...] = jnp.dot(s, w_ref[...], preferred_element_type=F32, precision=HIGHEST) + b_ref[...]


def _mod(cc, w_mod, b_mod):
    n = w_mod.shape[1]
    tn = 1024
    return pl.pallas_call(
        _mod_kernel,
        out_shape=jax.ShapeDtypeStruct((8, n), F32),
        grid=(n // tn,),
        in_specs=[pl.BlockSpec((8, D_MODEL), lambda j: (0, 0)),
                  pl.BlockSpec((D_MODEL, tn), lambda j: (0, j)),
                  pl.BlockSpec((1, tn), lambda j: (0, j))],
        out_specs=pl.BlockSpec((8, tn), lambda j: (0, j)),
        compiler_params=_params(("arbitrary",), 32),
        name="mod",
    )(cc, w_mod, b_mod.reshape(1, n))


def _inproj_kernel(x_ref, g_ref, sh_ref, sc_ref, w_ref, qkg_ref, gmat_ref,
                   cr_ref, sr_ref, cc_ref, sc2_ref, cd_ref, sa_ref, sb_ref,
                   o_ref, h_scr, acc_scr):
    j = pl.program_id(1)

    @pl.when(j == 0)
    def _():
        xf = x_ref[...]
        ms = jnp.mean(xf * xf, axis=-1, keepdims=True)
        y = xf * lax.rsqrt(ms + EPS) * g_ref[...]
        h_scr[...] = (y * (1.0 + sc_ref[0]) + sh_ref[0]).astype(BF16)

    acc_scr[...] = jnp.dot(h_scr[...], w_ref[...], preferred_element_type=F32)
    is_ret = j <= 1
    is_dqk = (j == 6) | (j == 7)

    @pl.when(is_ret)
    def _():
        scale = jnp.where(j == 0, RET_DK ** -0.5, 1.0).astype(F32)
        for b in range(COL_TILE // LANES):
            xb = acc_scr[:, b * LANES:(b + 1) * LANES]
            cos = cr_ref[...] if b % 2 == 0 else cc_ref[...]
            sin = sr_ref[...] if b % 2 == 0 else sc2_ref[...]
            o = (xb * cos + pltpu.roll(xb, 64, 1) * sin) * scale
            o_ref[:, b * LANES:(b + 1) * LANES] = o.astype(BF16)

    @pl.when(is_dqk)
    def _():
        g = jnp.where(j == 6, qkg_ref[0:1, :], qkg_ref[1:2, :])
        for b in range(COL_TILE // LANES):
            xb = acc_scr[:, b * LANES:(b + 1) * LANES]
            ss = jnp.dot((xb * xb).astype(BF16), gmat_ref[...], preferred_element_type=F32)
            yn = xb * lax.rsqrt(ss * (1.0 / DIFF_DH) + EPS) * g
            o = yn * cd_ref[...] + pltpu.roll(yn, 16, 1) * sa_ref[...] + pltpu.roll(yn, 112, 1) * sb_ref[...]
            o_ref[:, b * LANES:(b + 1) * LANES] = o.astype(BF16)

    @pl.when(jnp.logical_not(is_ret | is_dqk))
    def _():
        o_ref[...] = acc_scr[...].astype(BF16)


def _inproj(x2, g1, sh, sc, w_bf, qkg, gmat, tables, tm, tiles_per_batch):
    n = x2.shape[0]
    nb = sh.shape[0]
    tab_spec = pl.BlockSpec((tm, LANES), lambda i, j: (i % tiles_per_batch, 0))
    mod_spec = pl.BlockSpec((1, 1, D_MODEL), lambda i, j: (jnp.minimum(i // tiles_per_batch, nb - 1), 0, 0))
    return pl.pallas_call(
        _inproj_kernel,
        out_shape=jax.ShapeDtypeStruct((n, IN_COLS), BF16),
        grid=(n // tm, IN_COLS // COL_TILE),
        in_specs=[pl.BlockSpec((tm, D_MODEL), lambda i, j: (i, 0)),
                  pl.BlockSpec((1, D_MODEL), lambda i, j: (0, 0)),
                  mod_spec, mod_spec,
                  pl.BlockSpec((D_MODEL, COL_TILE), lambda i, j: (0, j)),
                  pl.BlockSpec((8, LANES), lambda i, j: (0, 0)),
                  pl.BlockSpec((LANES, LANES), lambda i, j: (0, 0))] + [tab_spec] * 7,
        out_specs=pl.BlockSpec((tm, COL_TILE), lambda i, j: (i, j)),
        scratch_shapes=[pltpu.VMEM((tm, D_MODEL), BF16), pltpu.VMEM((tm, COL_TILE), F32)],
        compiler_params=_params(("arbitrary", "arbitrary"), 48),
        name="inproj",
    )(x2, g1, sh, sc, w_bf, qkg, gmat, *tables)


def _rope_tables(seq):
    n_rows = seq // GRID_W
    f32 = np.float32

    def angles(pos, half):
        inv = f32(ROPE_BASE) ** (-np.arange(half, dtype=f32) / f32(half))
        return (pos.astype(f32)[:, None] * inv[None, :]).astype(np.float64)

    ar, ac = angles(np.arange(n_rows), 64), angles(np.arange(GRID_W), 64)
    br, bc = angles(np.arange(n_rows), 16), angles(np.arange(GRID_W), 16)
    zr, zc = np.zeros_like(br), np.zeros_like(bc)
    cat = lambda parts, reps=1: np.tile(np.concatenate(parts, axis=1), (1, reps)).astype(f32)
    by_row = lambda t: jnp.repeat(jnp.asarray(t), GRID_W, axis=0)
    by_col = lambda t: jnp.tile(jnp.asarray(t), (n_rows, 1))
    cr = by_row(cat([np.cos(ar), np.cos(ar)]))
    sr = by_row(cat([-np.sin(ar), np.sin(ar)]))
    cc = by_col(cat([np.cos(ac), np.cos(ac)]))
    sc = by_col(cat([-np.sin(ac), np.sin(ac)]))
    cd = by_row(cat([np.cos(br), np.cos(br), zr, zr], 2)) + by_col(cat([zc, zc, np.cos(bc), np.cos(bc)], 2))
    sa = by_row(cat([zr, np.sin(br), zr, zr], 2)) + by_col(cat([zc, zc, zc, np.sin(bc)], 2))
    sb = by_row(cat([-np.sin(br), zr, zr, zr], 2)) + by_col(cat([zc, zc, -np.sin(bc), zc], 2))
    return [cr, sr, cc, sc, cd, sa, sb]


def _identity_tables(seq):
    one = jnp.ones((seq, LANES), F32)
    zero = jnp.zeros((seq, LANES), F32)
    return [one, zero, one, zero, one, zero, zero]


def _tn_dot(a, b):
    return lax.dot_general(a, b, (((0,), (0,)), ((), ())), preferred_element_type=F32)


def _nt_dot(a, b):
    return lax.dot_general(a, b, (((1,), (1,)), ((), ())), preferred_element_type=F32)


def _ret_kernel(lg_ref, q_ref, k_ref, v_ref, g_ref, ck_ref, cv_ref, gn_ref, o_ref,
                sf_scr, sb_scr, ob_scr, *, chunk, sub, n_blocks, ctx_len):
    h = pl.program_id(1)
    p = pl.program_id(2)
    c = pl.program_id(3)
    lgf = lg_ref[0, h]
    lgb = lg_ref[1, h]

    def col_iota(n):
        return lax.broadcasted_iota(I32, (n, 1), 0).astype(F32)

    def vexp(s):
        return jnp.exp(jnp.zeros((1, 1), F32) + s)

    @pl.when((p == 0) & (c == 0))
    def _():
        jc = col_iota(ctx_len)
        kc = ck_ref[...].astype(F32)
        vc = cv_ref[...]
        sf_scr[...] = _tn_dot((kc * jnp.exp(lgf * (ctx_len - 1.0 - jc))).astype(BF16), vc)
        sb_scr[...] = _tn_dot((kc * jnp.exp(lgb * jc)).astype(BF16), vc)

    ic = col_iota(chunk)

    @pl.when(p == 0)
    def _():
        q_decay = jnp.exp(lgb * (chunk - ic))
        k_decay = jnp.exp(lgb * ic)
        s_decay = vexp(lgb * chunk)
        for j in reversed(range(sub)):
            loc = pl.ds(j * chunk, chunk)
            glob = pl.ds(pl.multiple_of(((n_blocks - 1 - c) * sub + j) * chunk, chunk), chunk)
            qb = (q_ref[loc, :].astype(F32) * q_decay).astype(BF16)
            ob_scr[glob, :] = jnp.dot(qb, sb_scr[...].astype(BF16), preferred_element_type=F32)
            kb = (k_ref[loc, :].astype(F32) * k_decay).astype(BF16)
            sb_scr[...] = s_decay * sb_scr[...] + _tn_dot(kb, v_ref[loc, :])

    @pl.when(p == 1)
    def _():
        ri = lax.broadcasted_iota(I32, (chunk, chunk), 0)
        ci = lax.broadcasted_iota(I32, (chunk, chunk), 1)
        d = (ri - ci).astype(F32)
        mask = jnp.where(d > 0, jnp.exp(lgf * jnp.maximum(d, 0.0)),
                         jnp.where(d < 0, jnp.exp(lgb * jnp.maximum(-d, 0.0)), 2.0))
        q_decay = jnp.exp(lgf * (ic + 1.0))
        k_decay = jnp.exp(lgf * (chunk - 1.0 - ic))
        s_decay = vexp(lgf * chunk)
        for j in range(sub):
            loc = pl.ds(j * chunk, chunk)
            glob = pl.ds(pl.multiple_of((c * sub + j) * chunk, chunk), chunk)
            a = (_nt_dot(q_ref[loc, :], k_ref[loc, :]) * mask).astype(BF16)
            qf = (q_ref[loc, :].astype(F32) * q_decay).astype(BF16)
            o = (jnp.dot(a, v_ref[loc, :], preferred_element_type=F32)
                 + jnp.dot(qf, sf_scr[...].astype(BF16), preferred_element_type=F32)
                 + ob_scr[glob, :])
            kf = (k_ref[loc, :].astype(F32) * k_decay).astype(BF16)
            sf_scr[...] = s_decay * sf_scr[...] + _tn_dot(kf, v_ref[loc, :])
            ms = jnp.mean(o * o, axis=-1, keepdims=True)
            y = o * lax.rsqrt(ms + EPS) * gn_ref[0]
            gt = g_ref[loc, :].astype(F32)
            o_ref[loc, :] = (y * (gt * jax.nn.sigmoid(gt))).astype(BF16)


def _retention(lg, p_lat, p_ctx, ret_norm_g, batch, seq, ctx_len):
    chunk = min(256, seq)
    sub = math.gcd(seq // chunk, RET_SUB)
    blk = chunk * sub
    nb = seq // blk
    kern = functools.partial(_ret_kernel, chunk=chunk, sub=sub, n_blocks=nb, ctx_len=ctx_len)

    def rows(b, p, c):
        return b * nb + jnp.where(p == 0, nb - 1 - c, c)

    def rows_fwd(b, p, c):
        return b * nb + jnp.where(p == 0, 0, c)

    return pl.pallas_call(
        kern,
        out_shape=jax.ShapeDtypeStruct((batch * seq, RET_HEADS * RET_DV), BF16),
        grid=(batch, RET_HEADS, 2, nb),
        in_specs=[pl.BlockSpec(memory_space=pltpu.SMEM),
                  pl.BlockSpec((blk, RET_DK), lambda b, h, p, c: (rows(b, p, c), h)),
                  pl.BlockSpec((blk, RET_DK), lambda b, h, p, c: (rows(b, p, c), 4 + h)),
                  pl.BlockSpec((blk, RET_DV), lambda b, h, p, c: (rows(b, p, c), 4 + h)),
                  pl.BlockSpec((blk, RET_DV), lambda b, h, p, c: (rows_fwd(b, p, c), 8 + h)),
                  pl.BlockSpec((ctx_len, RET_DK), lambda b, h, p, c: (b, 4 + h)),
                  pl.BlockSpec((ctx_len, RET_DV), lambda b, h, p, c: (b, 4 + h)),
                  pl.BlockSpec((1, 1, RET_DV), lambda b, h, p, c: (h, 0, 0))],
        out_specs=pl.BlockSpec((blk, RET_DV), lambda b, h, p, c: (rows_fwd(b, p, c), h)),
        scratch_shapes=[pltpu.VMEM((RET_DK, RET_DV), F32), pltpu.VMEM((RET_DK, RET_DV), F32),
                        pltpu.VMEM((seq, RET_DV), F32)],
        compiler_params=_params(("arbitrary",) * 4, 48),
        name="ret",
    )(lg, p_lat, p_lat, p_lat, p_lat, p_ctx, p_ctx, ret_norm_g.reshape(RET_HEADS, 1, RET_DV))


def _dattn_kernel(lam_ref, q_ref, ck_ref, cv_ref, k_ref, v_ref, gn_ref, gmat_ref, o_ref,
                  qq_scr, kmax_scr, mp_scr, kp_scr, vt_scr, pt_scr, *, tq, tk, n_kv, ta, unroll, ctx_len):
    rows = 2 * tq
    n_all = kp_scr.shape[0]
    qi = pl.program_id(2)

    def sq_norms(t):
        tf = t.astype(F32)
        return jnp.dot((tf * tf).astype(BF16), gmat_ref[...], preferred_element_type=F32)

    def lane_const(n, hot):
        return jnp.where(lax.broadcasted_iota(I32, (n, LANES), 1) < hot, 1.0, 0.0).astype(BF16)

    @pl.when(qi == 0)
    def _():
        kmax_scr[...] = jnp.max(sq_norms(ck_ref[...]), axis=0, keepdims=True)
        kp_scr[:, LANES:2 * LANES] = lane_const(n_all, 3)
        kp_scr[0:ctx_len, 0:LANES] = ck_ref[...]
        tail = lax.broadcasted_iota(I32, (VT_PAD, n_all), 0)
        vt_scr[DIFF_DV:, :] = jnp.where(tail == 0, 1.0, 0.0).astype(BF16)
        vt_scr[0:DIFF_DV, 0:ctx_len] = cv_ref[...].astype(F32).T.astype(BF16)

        def body(c, carry):
            start = pl.multiple_of(c * tk, tk)
            k = k_ref[pl.ds(start, tk), :]
            kmax_scr[...] = jnp.maximum(kmax_scr[...], jnp.max(sq_norms(k), axis=0, keepdims=True))
            kp_scr[pl.ds(pl.multiple_of(ctx_len + start, LANES), tk), 0:LANES] = k
            vt_scr[0:DIFF_DV, pl.ds(pl.multiple_of(ctx_len + start, LANES), tk)] = (
                v_ref[pl.ds(start, tk), :].astype(F32).T.astype(BF16))
            return carry

        lax.fori_loop(0, n_kv, body, 0)

    qt = q_ref[...].astype(F32).T
    row = lax.broadcasted_iota(I32, (LANES, tq), 0)
    q1t = jnp.where(row < DIFF_DH, qt, 0.0)
    q2t = jnp.where(row >= DIFF_DH, qt, 0.0)
    qq_scr[0:LANES, 0:tq] = q1t.astype(BF16)
    qq_scr[0:LANES, tq:rows] = q2t.astype(BF16)

    def set_shift(shift):
        neg = -shift
        hi = neg.astype(BF16).astype(F32)
        mid = (neg - hi).astype(BF16).astype(F32)
        lo = neg - hi - mid
        row_r = lax.broadcasted_iota(I32, (LANES, rows), 0)
        pieces = jnp.where(row_r == 0, hi, jnp.where(row_r == 1, mid, jnp.where(row_r == 2, lo, 0.0)))
        qq_scr[LANES:2 * LANES, :] = pieces.astype(BF16)

    kmax = kmax_scr[...]
    b1 = jnp.sqrt(jnp.sum(q1t * q1t, axis=0, keepdims=True) * kmax[:, 0:1]) * SHIFT_SLACK
    b2 = jnp.sqrt(jnp.sum(q2t * q2t, axis=0, keepdims=True) * kmax[:, DIFF_DH:DIFF_DH + 1]) * SHIFT_SLACK
    bound = jnp.concatenate([b1, b2], axis=1)
    set_shift(bound)

    def key_tile(t):
        return pl.ds(pl.multiple_of(t * ta, ta), ta)

    @pl.when(jnp.max(bound) > MAX_SAFE_SHIFT)
    def _():
        mp_scr[...] = jnp.full(mp_scr.shape, NEG_BIG, F32)

        def body(t, carry):
            st = jnp.dot(kp_scr[key_tile(t), 0:LANES], qq_scr[0:LANES, :], preferred_element_type=F32)
            mp_scr[...] = jnp.maximum(mp_scr[...], jnp.max(st, axis=0, keepdims=True))
            return carry

        lax.fori_loop(0, n_all // ta, body, 0)
        set_shift(mp_scr[...])

    def body_a(t, carry):
        for u in range(unroll):
            r = key_tile(t * unroll + u)
            pt_scr[r, :] = jnp.exp2(jnp.dot(kp_scr[r, :], qq_scr[...], preferred_element_type=F32)).astype(BF16)
        return carry

    lax.fori_loop(0, n_all // (ta * unroll), body_a, 0)

    acc = jnp.dot(vt_scr[...], pt_scr[...], preferred_element_type=F32)
    ot = acc[0:DIFF_DV, :] / acc[DIFF_DV:DIFF_DV + 1, :]
    d = (ot[:, 0:tq] - lam_ref[0] * ot[:, tq:rows]).T
    ms = jnp.mean(d * d, axis=-1, keepdims=True)
    y = d * lax.rsqrt(ms + EPS) * gn_ref[...] * (1.0 - LAMBDA_INIT)
    o_ref[...] = y.astype(BF16)


def _diff_attention(lam, p_lat, p_ctx, diff_norm_g, gmat, batch, seq, ctx_len):
    tq = min(256, seq)
    tk = min(512, seq)
    nq, nk = seq // tq, seq // tk
    n_all = ctx_len + seq
    ta = next(t for t in (528, 512, 384, 320, 256, 128) if n_all % t == 0)
    unroll = math.gcd(n_all // ta, KV_UNROLL)
    assert ctx_len % LANES == 0
    kern = functools.partial(_dattn_kernel, tq=tq, tk=tk, n_kv=nk, ta=ta, unroll=unroll, ctx_len=ctx_len)
    return pl.pallas_call(
        kern,
        out_shape=jax.ShapeDtypeStruct((batch * seq, DIFF_HEADS * DIFF_DV), BF16),
        grid=(batch, DIFF_HEADS, nq),
        in_specs=[pl.BlockSpec(memory_space=pltpu.SMEM),
                  pl.BlockSpec((tq, LANES), lambda b, h, qi: (b * nq + qi, 48 + h)),
                  pl.BlockSpec((ctx_len, LANES), lambda b, h, qi: (b, 56 + h)),
                  pl.BlockSpec((ctx_len, LANES), lambda b, h, qi: (b, 64 + h)),
                  pl.BlockSpec((seq, LANES), lambda b, h, qi: (b, 56 + h)),
                  pl.BlockSpec((seq, LANES), lambda b, h, qi: (b, 64 + h)),
                  pl.BlockSpec((1, LANES), lambda b, h, qi: (0, 0)),
                  pl.BlockSpec((LANES, LANES), lambda b, h, qi: (0, 0))],
        out_specs=pl.BlockSpec((tq, LANES), lambda b, h, qi: (b * nq + qi, h)),
        scratch_shapes=[pltpu.VMEM((2 * LANES, 2 * tq), BF16), pltpu.VMEM((1, LANES), F32),
                        pltpu.VMEM((1, 2 * tq), F32), pltpu.VMEM((n_all, 2 * LANES), BF16),
                        pltpu.VMEM((DIFF_DV + VT_PAD, n_all), BF16), pltpu.VMEM((n_all, 2 * tq), BF16)],
        compiler_params=_params(("arbitrary",) * 3, 48),
        name="dattn",
    )(lam, p_lat, p_ctx, p_ctx, p_lat, p_lat, diff_norm_g.reshape(1, DIFF_DV), gmat)


def _merge_kernel(yr_ref, yd_ref, ga_ref, gb_ref, x_ref, gatea_ref, shf_ref, scf_ref, n2_ref,
                  wr_ref, wd_ref, wo_ref, rwh_ref, rwl_ref, rb_ref,
                  xn_ref, hp_ref, meta_ref, gate_ref, cnt_ref, run_scr, *, tm):
    i = pl.program_id(0)

    @pl.when(i == 0)
    def _():
        run_scr[...] = jnp.zeros(run_scr.shape, F32)

    yr = jnp.dot(yr_ref[...], wr_ref[...], preferred_element_type=F32)
    yd = jnp.dot(yd_ref[...], wd_ref[...], preferred_element_type=F32)
    m = (jax.nn.sigmoid(ga_ref[...].astype(F32)) * yr + jax.nn.sigmoid(gb_ref[...].astype(F32)) * yd)
    z = jnp.dot(m.astype(BF16), wo_ref[...], preferred_element_type=F32)
    xn = x_ref[...] + gatea_ref[0] * z
    xn_ref[...] = xn

    ms = jnp.mean(xn * xn, axis=-1, keepdims=True)
    h2 = xn * lax.rsqrt(ms + EPS) * n2_ref[...]
    h2 = h2 * (1.0 + scf_ref[0]) + shf_ref[0]
    h_hi = h2.astype(BF16)
    bits = pltpu.bitcast(h_hi.astype(F32), U32)
    half = D_MODEL // 2
    hp_ref[...] = (bits[:, :half] >> 16) | (bits[:, half:] & jnp.uint32(0xFFFF0000))

    h_lo = (h2 - h_hi.astype(F32)).astype(BF16)
    logits = (jnp.dot(h_hi, rwh_ref[...], preferred_element_type=F32)
              + jnp.dot(h_lo, rwh_ref[...], preferred_element_type=F32)
              + jnp.dot(h_hi, rwl_ref[...], preferred_element_type=F32) + rb_ref[...])
    lane = lax.broadcasted_iota(I32, (tm, LANES), 1)
    lanef = lane.astype(F32)
    work = logits
    vals, idxs = [], []
    for _ in range(TOP_K):
        mk = jnp.max(work, axis=1, keepdims=True)
        ik = jnp.min(jnp.where(work == mk, lanef, float(LANES)), axis=1, keepdims=True)
        vals.append(mk)
        idxs.append(ik)
        work = jnp.where(lanef == ik, -jnp.inf, work)
    ex = [jnp.exp(v - vals[0]) for v in vals]
    den = ex[0] + ex[1] + ex[2] + ex[3]

    onehot = jnp.zeros((tm, LANES), F32)
    for ik in idxs:
        onehot = onehot + jnp.where(lanef == ik, 1.0, 0.0)
    ri = lax.broadcasted_iota(I32, (tm, tm), 0)
    ci = lax.broadcasted_iota(I32, (tm, tm), 1)
    tri = jnp.where(ri > ci, 1.0, 0.0).astype(BF16)
    base = run_scr[0:1, :] + jnp.dot(tri, onehot.astype(BF16), preferred_element_type=F32)
    run_scr[...] = run_scr[...] + jnp.sum(onehot, axis=0, keepdims=True)

    meta = jnp.zeros((tm, LANES), F32)
    gate_out = jnp.zeros((tm, LANES), F32)
    for k in range(TOP_K):
        rk = jnp.sum(jnp.where(lanef == idxs[k], base, 0.0), axis=1, keepdims=True)
        meta = jnp.where(lane == k, idxs[k], jnp.where(lane == TOP_K + k, rk, meta))
        gate_out = jnp.where(lane == k, ex[k] / den, gate_out)
    meta_ref[...] = meta.T[0:2 * TOP_K, :]
    gate_ref[...] = gate_out
    cnt_ref[...] = run_scr[...]


def _merge(y_ret, y_diff, p_lat, x2, g_a, sh_f, sc_f, norm2_g, w_r, w_d, w_o, rw_hi, rw_lo, rb, seq):
    n = x2.shape[0]
    tm = min(512, seq)
    tpb = seq // tm
    kern = functools.partial(_merge_kernel, tm=tm)
    mod_spec = pl.BlockSpec((1, 1, D_MODEL), lambda i: (i // tpb, 0, 0))
    const = lambda shape: pl.BlockSpec(shape, lambda i: (0,) * len(shape))
    tok = lambda w: pl.BlockSpec((tm, w), lambda i: (i, 0))
    return pl.pallas_call(
        kern,
        out_shape=(jax.ShapeDtypeStruct((n, D_MODEL), F32),
                   jax.ShapeDtypeStruct((n, D_MODEL // 2), U32),
                   jax.ShapeDtypeStruct((2 * TOP_K, n), F32),
                   jax.ShapeDtypeStruct((n, LANES), F32),
                   jax.ShapeDtypeStruct((8, LANES), F32)),
        grid=(n // tm,),
        in_specs=[tok(RET_HEADS * RET_DV), tok(D_MODEL),
                  pl.BlockSpec((tm, COL_TILE), lambda i: (i, 9)),
                  pl.BlockSpec((tm, COL_TILE), lambda i: (i, 10)),
                  tok(D_MODEL), mod_spec, mod_spec, mod_spec, const((1, D_MODEL)),
                  const((RET_HEADS * RET_DV, D_MODEL)), const((D_MODEL, D_MODEL)), const((D_MODEL, D_MODEL)),
                  const((D_MODEL, LANES)), const((D_MODEL, LANES)), const((1, LANES))],
        out_specs=(tok(D_MODEL), tok(D_MODEL // 2), pl.BlockSpec((2 * TOP_K, tm), lambda i: (0, i)),
                   tok(LANES), const((8, LANES))),
        scratch_shapes=[pltpu.VMEM((8, LANES), F32)],
        compiler_params=_params(("arbitrary",), 56),
        name="merge",
    )(y_ret, y_diff, p_lat, p_lat, x2, g_a, sh_f, sc_f, norm2_g, w_r, w_d, w_o, rw_hi, rw_lo, rb)


def _dispatch_kernel(dest_ref, hp_ref, xb_in_ref, xb_ref, sem, *, tm, n_tok):
    del xb_in_ref
    i = pl.program_id(0)

    def body(g, carry):
        r0 = pl.multiple_of(g * SUBLANES, SUBLANES)
        for j in range(SUBLANES):
            for k in range(TOP_K):
                d = dest_ref[k * n_tok + i * tm + r0 + j]
                pltpu.make_async_copy(hp_ref.at[pl.ds(r0 + j, 1), :], xb_ref.at[pl.ds(d, 1), :], sem).start()
        return carry

    lax.fori_loop(0, tm // SUBLANES, body, 0)
    for _ in range(TOP_K):
        pltpu.make_async_copy(hp_ref, xb_ref.at[pl.ds(0, tm), :], sem).wait()


def _dispatch(dest, hp, xb_zero):
    n = hp.shape[0]
    tm = min(512, n)
    kern = functools.partial(_dispatch_kernel, tm=tm, n_tok=n)
    return pl.pallas_call(
        kern,
        out_shape=jax.ShapeDtypeStruct(xb_zero.shape, U32),
        grid_spec=pltpu.PrefetchScalarGridSpec(
            num_scalar_prefetch=1,
            grid=(n // tm,),
            in_specs=[pl.BlockSpec((tm, D_MODEL // 2), lambda i, d: (i, 0)),
                      pl.BlockSpec(memory_space=pl.ANY)],
            out_specs=pl.BlockSpec(memory_space=pl.ANY),
            scratch_shapes=[pltpu.SemaphoreType.DMA(())]),
        input_output_aliases={2: 0},
        compiler_params=_params(("arbitrary",), 32),
        name="dispatch",
    )(dest, hp, xb_zero)


def _expert_kernel(be_ref, nb_ref, nxt_ref, xb_ref, w1_hbm, b1_ref, w2_hbm, b2_ref, y_ref,
                   w1f_scr, w2f_scr, w1b_scr, w2b_scr, grp_scr, sems):
    i = pl.program_id(0)
    e = be_ref[i]
    prev = be_ref[jnp.maximum(i - 1, 0)]
    active = i < nb_ref[0]

    def weight_copies(expert, slot):
        return (pltpu.make_async_copy(w1_hbm.at[expert], w1f_scr.at[slot], sems.at[0, slot]),
                pltpu.make_async_copy(w2_hbm.at[expert], w2f_scr.at[slot], sems.at[1, slot]))

    @pl.when(i == 0)
    def _():
        grp_scr[0] = 0
        for cp in weight_copies(e, 0):
            cp.start()

    @pl.when(active & (i > 0) & (e != prev))
    def _():
        grp_scr[0] = grp_scr[0] + 1

    @pl.when(active & ((i == 0) | (e != prev)))
    def _():
        slot = grp_scr[0] % 2
        for cp in weight_copies(e, slot):
            cp.wait()
        w1b_scr[...] = w1f_scr[slot].astype(BF16)
        w2b_scr[...] = w2f_scr[slot].astype(BF16)
        nxt = nxt_ref[e]

        @pl.when(nxt >= 0)
        def _():
            for cp in weight_copies(nxt, 1 - slot):
                cp.start()

    @pl.when(active)
    def _():
        xu = xb_ref[...]
        x_lo = pltpu.bitcast(xu << 16, F32).astype(BF16)
        x_hi = pltpu.bitcast(xu & jnp.uint32(0xFFFF0000), F32).astype(BF16)
        half = D_MODEL // 2
        hh = (jnp.dot(x_lo, w1b_scr[0:half, :], preferred_element_type=F32)
              + jnp.dot(x_hi, w1b_scr[half:, :], preferred_element_type=F32) + b1_ref[0])
        glu = jnp.minimum(hh[:, :D_FF], SWIGLU_LIMIT)
        lin = jnp.clip(hh[:, D_FF:], -SWIGLU_LIMIT, SWIGLU_LIMIT)
        act = glu * jax.nn.sigmoid(SWIGLU_ALPHA * glu) * (lin + 1.0)
        y_ref[...] = jnp.dot(act.astype(BF16), w2b_scr[...], preferred_element_type=F32) + b2_ref[0]

    @pl.when(i >= nb_ref[0])
    def _():
        y_ref[...] = jnp.zeros(y_ref.shape, F32)


def _experts(block_e, n_used, next_e, xb, w1, b1, w2, b2):
    rows = xb.shape[0]
    n_blocks = rows // MOE_BLK
    return pl.pallas_call(
        _expert_kernel,
        out_shape=jax.ShapeDtypeStruct((rows, D_MODEL), F32),
        grid_spec=pltpu.PrefetchScalarGridSpec(
            num_scalar_prefetch=3,
            grid=(n_blocks,),
            in_specs=[pl.BlockSpec((MOE_BLK, D_MODEL // 2), lambda i, be, nb, nx: (i, 0)),
                      pl.BlockSpec(memory_space=pl.ANY),
                      pl.BlockSpec((1, 1, 2 * D_FF), lambda i, be, nb, nx: (be[i], 0, 0)),
                      pl.BlockSpec(memory_space=pl.ANY),
                      pl.BlockSpec((1, 1, D_MODEL), lambda i, be, nb, nx: (be[i], 0, 0))],
            out_specs=pl.BlockSpec((MOE_BLK, D_MODEL), lambda i, be, nb, nx: (i, 0)),
            scratch_shapes=[pltpu.VMEM((2, D_MODEL, 2 * D_FF), F32), pltpu.VMEM((2, D_FF, D_MODEL), F32),
                            pltpu.VMEM((D_MODEL, 2 * D_FF), BF16), pltpu.VMEM((D_FF, D_MODEL), BF16),
                            pltpu.SMEM((1,), I32), pltpu.SemaphoreType.DMA((2, 2))]),
        compiler_params=_params(("arbitrary",), 56),
        name="expert",
    )(block_e, n_used, next_e, xb, w1, b1.reshape(N_EXPERTS, 1, 2 * D_FF), w2, b2.reshape(N_EXPERTS, 1, D_MODEL))


def _combine_kernel(dest_ref, yb_ref, gate_ref, xn_ref, gf_ref, o_ref, buf, sem, *, tm, n_tok):
    i = pl.program_id(0)

    def body(g, carry):
        r0 = pl.multiple_of(g * SUBLANES, SUBLANES)
        for j in range(SUBLANES):
            for k in range(TOP_K):
                d = dest_ref[k * n_tok + i * tm + r0 + j]
                pltpu.make_async_copy(yb_ref.at[pl.ds(d, 1), :], buf.at[k, pl.ds(r0 + j, 1), :], sem).start()
        return carry

    lax.fori_loop(0, tm // SUBLANES, body, 0)
    for k in range(TOP_K):
        pltpu.make_async_copy(yb_ref.at[pl.ds(0, tm), :], buf.at[k], sem).wait()
    g = gate_ref[...]
    y = g[:, 0:1] * buf[0]
    for k in range(1, TOP_K):
        y = y + g[:, k:k + 1] * buf[k]
    o_ref[...] = xn_ref[...] + gf_ref[0] * y


def _combine(dest, yb, gate, xn, g_f, seq):
    n = xn.shape[0]
    tm = min(256, seq)
    tpb = seq // tm
    kern = functools.partial(_combine_kernel, tm=tm, n_tok=n)
    return pl.pallas_call(
        kern,
        out_shape=jax.ShapeDtypeStruct((n, D_MODEL), F32),
        grid_spec=pltpu.PrefetchScalarGridSpec(
            num_scalar_prefetch=1,
            grid=(n // tm,),
            in_specs=[pl.BlockSpec(memory_space=pl.ANY),
                      pl.BlockSpec((tm, LANES), lambda i, d: (i, 0)),
                      pl.BlockSpec((tm, D_MODEL), lambda i, d: (i, 0)),
                      pl.BlockSpec((1, 1, D_MODEL), lambda i, d: (i // tpb, 0, 0))],
            out_specs=pl.BlockSpec((tm, D_MODEL), lambda i, d: (i, 0)),
            scratch_shapes=[pltpu.VMEM((TOP_K, tm, D_MODEL), F32), pltpu.SemaphoreType.DMA(())]),
        compiler_params=_params(("arbitrary",), 32),
        name="combine",
    )(dest, yb, gate, xn, g_f)


def _layer(x, ctx, c, c_ctx, norm1_g, norm2_g, w_mod, b_mod, w_in, ret_decay_logit, ret_norm_g,
           diff_q_norm_g, diff_k_norm_g, diff_lambda, diff_norm_g, w_br_ret, w_br_diff, w_out,
           router_w, router_b, exp_w1, exp_b1, exp_w2, exp_b2):
    batch, seq, d = x.shape
    ctx_len = ctx.shape[1]
    assert d == D_MODEL and seq % GRID_W == 0 and batch + 1 <= 8
    n_tok = batch * seq

    cc = jnp.zeros((8, D_MODEL), F32).at[:batch].set(c).at[batch].set(c_ctx)
    mod = _mod(cc, w_mod, b_mod)
    sh_a, sc_a, g_a, sh_f, sc_f, g_f = [mod[:batch, i * D_MODEL:(i + 1) * D_MODEL].reshape(batch, 1, D_MODEL)
                                         for i in range(6)]
    csh_a = mod[batch:batch + 1, 0:D_MODEL].reshape(1, 1, D_MODEL)
    csc_a = mod[batch:batch + 1, D_MODEL:2 * D_MODEL].reshape(1, 1, D_MODEL)

    w_in_bf = w_in.astype(BF16)
    g1 = norm1_g.reshape(1, D_MODEL)
    tile = lambda g: jnp.tile(g.astype(F32), 2)
    qkg = jnp.zeros((8, LANES), F32).at[0].set(tile(diff_q_norm_g) * (DIFF_DH ** -0.5 * LOG2E)).at[1].set(tile(diff_k_norm_g))
    lane = jnp.arange(LANES)
    gmat = (lane[:, None] // DIFF_DH == lane[None, :] // DIFF_DH).astype(BF16)
    x2 = x.reshape(n_tok, D_MODEL)
    tm = min(1024, seq)
    p_lat = _inproj(x2, g1, sh_a, sc_a, w_in_bf, qkg, gmat, _rope_tables(seq), tm, seq // tm)
    p_ctx = _inproj(ctx.reshape(batch * ctx_len, D_MODEL), g1, csh_a, csc_a, w_in_bf, qkg, gmat,
                    _identity_tables(ctx_len), ctx_len, 1)

    lg = jax.nn.log_sigmoid(ret_decay_logit.astype(F32))
    y_ret = _retention(lg, p_lat, p_ctx, ret_norm_g, batch, seq, ctx_len)

    lp = diff_lambda.astype(F32)
    lam = (jnp.exp(jnp.sum(lp[0] * lp[1])) - jnp.exp(jnp.sum(lp[2] * lp[3])) + LAMBDA_INIT).reshape(1)
    y_diff = _diff_attention(lam, p_lat, p_ctx, diff_norm_g, gmat, batch, seq, ctx_len)

    rw = jnp.zeros((D_MODEL, LANES), F32).at[:, :N_EXPERTS].set(router_w)
    rw_hi = rw.astype(BF16)
    rw_lo = (rw - rw_hi.astype(F32)).astype(BF16)
    rb = jnp.full((1, LANES), NEG_BIG, F32).at[0, :N_EXPERTS].set(router_b)
    xn, hp, meta, gate4, cnt = _merge(
        y_ret, y_diff, p_lat, x2, g_a, sh_f, sc_f, norm2_g.reshape(1, D_MODEL),
        w_br_ret.astype(BF16), w_br_diff.astype(BF16), w_out.astype(BF16), rw_hi, rw_lo, rb, seq)

    counts = cnt[0, :N_EXPERTS].astype(I32)
    padded = (counts + MOE_BLK - 1) // MOE_BLK * MOE_BLK
    pad_end = jnp.cumsum(padded)
    pad_start = pad_end - padded
    n_pairs = n_tok * TOP_K
    n_blocks = n_pairs // MOE_BLK + N_EXPERTS
    meta_i = meta.astype(I32)
    is_e = meta_i[None, :TOP_K] == jnp.arange(N_EXPERTS, dtype=I32)[:, None, None]
    dest = (jnp.sum(jnp.where(is_e, pad_start[:, None, None], 0), axis=0) + meta_i[TOP_K:]).reshape(n_pairs)
    block_start = jnp.arange(n_blocks, dtype=I32) * MOE_BLK
    block_e = jnp.minimum(jnp.sum((pad_end[None, :] <= block_start[:, None]).astype(I32), axis=1), N_EXPERTS - 1)
    n_used = (pad_end[-1] // MOE_BLK).reshape(1).astype(I32)

    xb = _dispatch(dest, hp, jnp.zeros((n_blocks * MOE_BLK, D_MODEL // 2), U32))
    e_ids = jnp.arange(N_EXPERTS, dtype=I32)
    later = jnp.where((counts[None, :] > 0) & (e_ids[None, :] > e_ids[:, None]), e_ids[None, :], N_EXPERTS)
    next_e = jnp.min(later, axis=1)
    next_e = jnp.where(next_e == N_EXPERTS, -1, next_e).astype(I32)
    yb = _experts(block_e, n_used, next_e, xb, exp_w1, exp_b1, exp_w2, exp_b2)
    out = _combine(dest, yb, gate4, xn, g_f, seq)
    return out.reshape(batch, seq, D_MODEL)


def kernel(x, c, ctx, c_ctx, norm1_g, norm2_g, w_mod, b_mod, w_in, ret_decay_logit, ret_norm_g, diff_q_norm_g, diff_k_norm_g, diff_lambda, diff_norm_g, w_br_ret, w_br_diff, w_out, router_w, router_b, exp_w1, exp_b1, exp_w2, exp_b2):
    assert norm1_g.shape[0] == 1, "single-layer block"
    return _layer(x, ctx, c, c_ctx, norm1_g[0], norm2_g[0], w_mod[0], b_mod[0], w_in[0], ret_decay_logit[0],
                  ret_norm_g[0], diff_q_norm_g[0], diff_k_norm_g[0], diff_lambda[0], diff_norm_g[0],
                  w_br_ret[0], w_br_diff[0], w_out[0], router_w[0], router_b[0],
                  exp_w1[0], exp_b1[0], exp_w2[0], exp_b2[0])
```

```python
import functools
import math

import numpy as np

import jax
import jax.numpy as jnp
from jax import lax
from jax.experimental import pallas as pl
from jax.experimental.pallas import tpu as pltpu

F32 = jnp.float32
BF16 = jnp.bfloat16
U32 = jnp.uint32
I32 = jnp.int32

D_MODEL = 1024
GRID_W = 64
RET_HEADS = 4
RET_DK = 256
RET_DV = 512
DIFF_DH = 64
DIFF_HEADS = 8
DIFF_DV = 128
N_EXPERTS = 32
TOP_K = 4
D_FF = 1024
SWIGLU_LIMIT = 7.0
SWIGLU_ALPHA = 1.702
ROPE_BASE = 10000.0
EPS = 1e-6
LAMBDA_INIT = 0.8 - 0.6 * math.exp(-0.3 * 0)

IN_COLS = 11264
COL_TILE = 1024
LANES = 128
SUBLANES = 8
MOE_BLK = 256
NEG_BIG = -1e30
LOG2E = 1.4426950408889634
SHIFT_SLACK = 1.0 + 2.0 ** -6
MAX_SAFE_SHIFT = 60.0
RET_SUB = 4
VT_PAD = 16
KV_UNROLL = 4
HIGHEST = lax.Precision.HIGHEST
MIB = 1024 * 1024


def _params(sem, vmem_mib):
    return pltpu.CompilerParams(dimension_semantics=sem, vmem_limit_bytes=vmem_mib * MIB)


def _mod_kernel(c_ref, w_ref, b_ref, o_ref):
    c = c_ref[...]
    s = c * jax.nn.sigmoid(c)
    o_ref[...] = jnp.dot(s, w_ref[...], preferred_element_type=F32, precision=HIGHEST) + b_ref[...]


def _mod(cc, w_mod, b_mod):
    n = w_mod.shape[1]
    tn = 1024
    return pl.pallas_call(
        _mod_kernel,
        out_shape=jax.ShapeDtypeStruct((8, n), F32),
        grid=(n // tn,),
        in_specs=[pl.BlockSpec((8, D_MODEL), lambda j: (0, 0)),
                  pl.BlockSpec((D_MODEL, tn), lambda j: (0, j)),
                  pl.BlockSpec((1, tn), lambda j: (0, j))],
        out_specs=pl.BlockSpec((8, tn), lambda j: (0, j)),
        compiler_params=_params(("arbitrary",), 32),
        name="mod",
    )(cc, w_mod, b_mod.reshape(1, n))


def _inproj_kernel(x_ref, g_ref, sh_ref, sc_ref, w_ref, qkg_ref, gmat_ref,
                   cr_ref, sr_ref, cc_ref, sc2_ref, cd_ref, sa_ref, sb_ref,
                   o_ref, h_scr, acc_scr):
    j = pl.program_id(1)

    @pl.when(j == 0)
    def _():
        xf = x_ref[...]
        ms = jnp.mean(xf * xf, axis=-1, keepdims=True)
        y = xf * lax.rsqrt(ms + EPS) * g_ref[...]
        h_scr[...] = (y * (1.0 + sc_ref[0]) + sh_ref[0]).astype(BF16)

    is_ret = j <= 1
    is_dqk = (j == 6) | (j == 7)

    def project():
        return jnp.dot(h_scr[...], w_ref[...], preferred_element_type=F32)

    @pl.when(is_ret)
    def _():
        acc_scr[...] = project()
        scale = jnp.where(j == 0, RET_DK ** -0.5, 1.0).astype(F32)
        for b in range(COL_TILE // LANES):
            xb = acc_scr[:, b * LANES:(b + 1) * LANES]
            cos = cr_ref[...] if b % 2 == 0 else cc_ref[...]
            sin = sr_ref[...] if b % 2 == 0 else sc2_ref[...]
            o = (xb * cos + pltpu.roll(xb, 64, 1) * sin) * scale
            o_ref[:, b * LANES:(b + 1) * LANES] = o.astype(BF16)

    @pl.when(is_dqk)
    def _():
        acc_scr[...] = project()
        g = jnp.where(j == 6, qkg_ref[0:1, :], qkg_ref[1:2, :])
        for b2 in range(COL_TILE // (2 * LANES)):
            x2 = acc_scr[:, b2 * 2 * LANES:(b2 + 1) * 2 * LANES]
            ms2 = jnp.dot((x2 * x2).astype(BF16), gmat_ref[...], preferred_element_type=F32)
            for half in range(2):
                b = 2 * b2 + half
                xb = x2[:, half * LANES:(half + 1) * LANES]
                yn = xb * lax.rsqrt(ms2[:, half * LANES:(half + 1) * LANES] + EPS) * g
                o = yn * cd_ref[...] + pltpu.roll(yn, 16, 1) * sa_ref[...] + pltpu.roll(yn, 112, 1) * sb_ref[...]
                o_ref[:, b * LANES:(b + 1) * LANES] = o.astype(BF16)

    @pl.when(jnp.logical_not(is_ret | is_dqk))
    def _():
        o_ref[...] = project().astype(BF16)


def _inproj(x2, g1, sh, sc, w_bf, qkg, gmat, tables, tm, tiles_per_batch):
    n = x2.shape[0]
    nb = sh.shape[0]
    tab_spec = pl.BlockSpec((tm, LANES), lambda i, j: (i % tiles_per_batch, 0))
    mod_spec = pl.BlockSpec((1, 1, D_MODEL), lambda i, j: (jnp.minimum(i // tiles_per_batch, nb - 1), 0, 0))
    return pl.pallas_call(
        _inproj_kernel,
        out_shape=jax.ShapeDtypeStruct((n, IN_COLS), BF16),
        grid=(n // tm, IN_COLS // COL_TILE),
        in_specs=[pl.BlockSpec((tm, D_MODEL), lambda i, j: (i, 0)),
                  pl.BlockSpec((1, D_MODEL), lambda i, j: (0, 0)),
                  mod_spec, mod_spec,
                  pl.BlockSpec((D_MODEL, COL_TILE), lambda i, j: (0, j)),
                  pl.BlockSpec((8, LANES), lambda i, j: (0, 0)),
                  pl.BlockSpec((2 * LANES, 2 * LANES), lambda i, j: (0, 0))] + [tab_spec] * 7,
        out_specs=pl.BlockSpec((tm, COL_TILE), lambda i, j: (i, j)),
        scratch_shapes=[pltpu.VMEM((tm, D_MODEL), BF16), pltpu.VMEM((tm, COL_TILE), F32)],
        compiler_params=_params(("arbitrary", "arbitrary"), 48),
        name="inproj",
    )(x2, g1, sh, sc, w_bf, qkg, gmat, *tables)


def _rope_tables(seq):
    n_rows = seq // GRID_W
    f32 = np.float32

    def angles(pos, half):
        inv = f32(ROPE_BASE) ** (-np.arange(half, dtype=f32) / f32(half))
        return (pos.astype(f32)[:, None] * inv[None, :]).astype(np.float64)

    ar, ac = angles(np.arange(n_rows), 64), angles(np.arange(GRID_W), 64)
    br, bc = angles(np.arange(n_rows), 16), angles(np.arange(GRID_W), 16)
    zr, zc = np.zeros_like(br), np.zeros_like(bc)
    cat = lambda parts, reps=1: np.tile(np.concatenate(parts, axis=1), (1, reps)).astype(f32)
    by_row = lambda t: jnp.repeat(jnp.asarray(t), GRID_W, axis=0)
    by_col = lambda t: jnp.tile(jnp.asarray(t), (n_rows, 1))
    cr = by_row(cat([np.cos(ar), np.cos(ar)]))
    sr = by_row(cat([-np.sin(ar), np.sin(ar)]))
    cc = by_col(cat([np.cos(ac), np.cos(ac)]))
    sc = by_col(cat([-np.sin(ac), np.sin(ac)]))
    cd = by_row(cat([np.cos(br), np.cos(br), zr, zr], 2)) + by_col(cat([zc, zc, np.cos(bc), np.cos(bc)], 2))
    sa = by_row(cat([zr, np.sin(br), zr, zr], 2)) + by_col(cat([zc, zc, zc, np.sin(bc)], 2))
    sb = by_row(cat([-np.sin(br), zr, zr, zr], 2)) + by_col(cat([zc, zc, -np.sin(bc), zc], 2))
    return [cr, sr, cc, sc, cd, sa, sb]


def _identity_tables(seq):
    one = jnp.ones((seq, LANES), F32)
    zero = jnp.zeros((seq, LANES), F32)
    return [one, zero, one, zero, one, zero, zero]


def _tn_dot(a, b):
    return lax.dot_general(a, b, (((0,), (0,)), ((), ())), preferred_element_type=F32)


def _nt_dot(a, b):
    return lax.dot_general(a, b, (((1,), (1,)), ((), ())), preferred_element_type=F32)


def _ret_kernel(lg_ref, q_ref, k_ref, v_ref, g_ref, ck_ref, cv_ref, gn_ref, o_ref,
                sf_scr, sb_scr, ob_scr, *, chunk, sub, n_blocks, ctx_len):
    h = pl.program_id(1)
    p = pl.program_id(2)
    c = pl.program_id(3)
    lgf = lg_ref[0, h]
    lgb = lg_ref[1, h]

    def col_iota(n):
        return lax.broadcasted_iota(I32, (n, 1), 0).astype(F32)

    def vexp(s):
        return jnp.exp(jnp.zeros((1, 1), F32) + s)

    @pl.when((p == 0) & (c == 0))
    def _():
        jc = col_iota(ctx_len)
        kc = ck_ref[...].astype(F32)
        vc = cv_ref[...]
        sf_scr[...] = _tn_dot((kc * jnp.exp(lgf * (ctx_len - 1.0 - jc))).astype(BF16), vc)
        sb_scr[...] = _tn_dot((kc * jnp.exp(lgb * jc)).astype(BF16), vc)

    ic = col_iota(chunk)

    @pl.when(p == 0)
    def _():
        q_decay = jnp.exp(lgb * (chunk - ic))
        k_decay = jnp.exp(lgb * ic)
        s_decay = vexp(lgb * chunk)
        for j in reversed(range(sub)):
            loc = pl.ds(j * chunk, chunk)
            glob = pl.ds(pl.multiple_of(((n_blocks - 1 - c) * sub + j) * chunk, chunk), chunk)
            qb = (q_ref[loc, :].astype(F32) * q_decay).astype(BF16)
            ob_scr[glob, :] = jnp.dot(qb, sb_scr[...].astype(BF16), preferred_element_type=F32)
            kb = (k_ref[loc, :].astype(F32) * k_decay).astype(BF16)
            sb_scr[...] = s_decay * sb_scr[...] + _tn_dot(kb, v_ref[loc, :])

    @pl.when(p == 1)
    def _():
        ri = lax.broadcasted_iota(I32, (chunk, chunk), 0)
        ci = lax.broadcasted_iota(I32, (chunk, chunk), 1)
        d = (ri - ci).astype(F32)
        mask = jnp.where(d > 0, jnp.exp(lgf * jnp.maximum(d, 0.0)),
                         jnp.where(d < 0, jnp.exp(lgb * jnp.maximum(-d, 0.0)), 2.0))
        q_decay = jnp.exp(lgf * (ic + 1.0))
        k_decay = jnp.exp(lgf * (chunk - 1.0 - ic))
        s_decay = vexp(lgf * chunk)
        for j in range(sub):
            loc = pl.ds(j * chunk, chunk)
            glob = pl.ds(pl.multiple_of((c * sub + j) * chunk, chunk), chunk)
            a = (_nt_dot(q_ref[loc, :], k_ref[loc, :]) * mask).astype(BF16)
            qf = (q_ref[loc, :].astype(F32) * q_decay).astype(BF16)
            o = (jnp.dot(a, v_ref[loc, :], preferred_element_type=F32)
                 + jnp.dot(qf, sf_scr[...].astype(BF16), preferred_element_type=F32)
                 + ob_scr[glob, :])
            kf = (k_ref[loc, :].astype(F32) * k_decay).astype(BF16)
            sf_scr[...] = s_decay * sf_scr[...] + _tn_dot(kf, v_ref[loc, :])
            ms = jnp.mean(o * o, axis=-1, keepdims=True)
            y = o * lax.rsqrt(ms + EPS) * gn_ref[0]
            gt = g_ref[loc, :].astype(F32)
            o_ref[loc, :] = (y * (gt * jax.nn.sigmoid(gt))).astype(BF16)


def _retention(lg, p_lat, p_ctx, ret_norm_g, batch, seq, ctx_len):
    chunk = min(256, seq)
    sub = math.gcd(seq // chunk, RET_SUB)
    blk = chunk * sub
    nb = seq // blk
    kern = functools.partial(_ret_kernel, chunk=chunk, sub=sub, n_blocks=nb, ctx_len=ctx_len)

    def rows(b, p, c):
        return b * nb + jnp.where(p == 0, nb - 1 - c, c)

    def rows_fwd(b, p, c):
        return b * nb + jnp.where(p == 0, 0, c)

    return pl.pallas_call(
        kern,
        out_shape=jax.ShapeDtypeStruct((batch * seq, RET_HEADS * RET_DV), BF16),
        grid=(batch, RET_HEADS, 2, nb),
        in_specs=[pl.BlockSpec(memory_space=pltpu.SMEM),
                  pl.BlockSpec((blk, RET_DK), lambda b, h, p, c: (rows(b, p, c), h)),
                  pl.BlockSpec((blk, RET_DK), lambda b, h, p, c: (rows(b, p, c), 4 + h)),
                  pl.BlockSpec((blk, RET_DV), lambda b, h, p, c: (rows(b, p, c), 4 + h)),
                  pl.BlockSpec((blk, RET_DV), lambda b, h, p, c: (rows_fwd(b, p, c), 8 + h)),
                  pl.BlockSpec((ctx_len, RET_DK), lambda b, h, p, c: (b, 4 + h)),
                  pl.BlockSpec((ctx_len, RET_DV), lambda b, h, p, c: (b, 4 + h)),
                  pl.BlockSpec((1, 1, RET_DV), lambda b, h, p, c: (h, 0, 0))],
        out_specs=pl.BlockSpec((blk, RET_DV), lambda b, h, p, c: (rows_fwd(b, p, c), h)),
        scratch_shapes=[pltpu.VMEM((RET_DK, RET_DV), F32), pltpu.VMEM((RET_DK, RET_DV), F32),
                        pltpu.VMEM((seq, RET_DV), F32)],
        compiler_params=_params(("arbitrary",) * 4, 48),
        name="ret",
    )(lg, p_lat, p_lat, p_lat, p_lat, p_ctx, p_ctx, ret_norm_g.reshape(RET_HEADS, 1, RET_DV))


def _dattn_kernel(lam_ref, q_ref, ck_ref, cv_ref, k_ref, v_ref, gn_ref, gmat_ref, o_ref,
                  qq_scr, kmax_scr, mp_scr, kp_scr, vt_scr, pt_scr, *, tq, tk, n_kv, ta, unroll, ctx_len):
    rows = 2 * tq
    n_all = kp_scr.shape[0]
    qi = pl.program_id(2)

    def sq_norms(t):
        tf = t.astype(F32)
        return jnp.dot((tf * tf).astype(BF16), gmat_ref[...], preferred_element_type=F32)

    def lane_const(n, hot):
        return jnp.where(lax.broadcasted_iota(I32, (n, LANES), 1) < hot, 1.0, 0.0).astype(BF16)

    @pl.when(qi == 0)
    def _():
        kmax_scr[...] = jnp.max(sq_norms(ck_ref[...]), axis=0, keepdims=True)
        kp_scr[:, LANES:2 * LANES] = lane_const(n_all, 3)
        kp_scr[0:ctx_len, 0:LANES] = ck_ref[...]
        tail = lax.broadcasted_iota(I32, (VT_PAD, n_all), 0)
        vt_scr[DIFF_DV:, :] = jnp.where(tail == 0, 1.0, 0.0).astype(BF16)
        vt_scr[0:DIFF_DV, 0:ctx_len] = cv_ref[...].astype(F32).T.astype(BF16)

        def body(c, carry):
            start = pl.multiple_of(c * tk, tk)
            k = k_ref[pl.ds(start, tk), :]
            kmax_scr[...] = jnp.maximum(kmax_scr[...], jnp.max(sq_norms(k), axis=0, keepdims=True))
            kp_scr[pl.ds(pl.multiple_of(ctx_len + start, LANES), tk), 0:LANES] = k
            vt_scr[0:DIFF_DV, pl.ds(pl.multiple_of(ctx_len + start, LANES), tk)] = (
                v_ref[pl.ds(start, tk), :].astype(F32).T.astype(BF16))
            return carry

        lax.fori_loop(0, n_kv, body, 0)

    qt = q_ref[...].astype(F32).T
    row = lax.broadcasted_iota(I32, (LANES, tq), 0)
    q1t = jnp.where(row < DIFF_DH, qt, 0.0)
    q2t = jnp.where(row >= DIFF_DH, qt, 0.0)
    qq_scr[0:LANES, 0:tq] = q1t.astype(BF16)
    qq_scr[0:LANES, tq:rows] = q2t.astype(BF16)

    def set_shift(shift):
        neg = -shift
        hi = neg.astype(BF16).astype(F32)
        mid = (neg - hi).astype(BF16).astype(F32)
        lo = neg - hi - mid
        row_r = lax.broadcasted_iota(I32, (LANES, rows), 0)
        pieces = jnp.where(row_r == 0, hi, jnp.where(row_r == 1, mid, jnp.where(row_r == 2, lo, 0.0)))
        qq_scr[LANES:2 * LANES, :] = pieces.astype(BF16)

    kmax = kmax_scr[...]
    b1 = jnp.sqrt(jnp.sum(q1t * q1t, axis=0, keepdims=True) * kmax[:, 0:1]) * SHIFT_SLACK
    b2 = jnp.sqrt(jnp.sum(q2t * q2t, axis=0, keepdims=True) * kmax[:, DIFF_DH:DIFF_DH + 1]) * SHIFT_SLACK
    bound = jnp.concatenate([b1, b2], axis=1)
    set_shift(bound)

    def key_tile(t):
        return pl.ds(pl.multiple_of(t * ta, ta), ta)

    @pl.when(jnp.max(bound) > MAX_SAFE_SHIFT)
    def _():
        mp_scr[...] = jnp.full(mp_scr.shape, NEG_BIG, F32)

        def body(t, carry):
            st = jnp.dot(kp_scr[key_tile(t), 0:LANES], qq_scr[0:LANES, :], preferred_element_type=F32)
            mp_scr[...] = jnp.maximum(mp_scr[...], jnp.max(st, axis=0, keepdims=True))
            return carry

        lax.fori_loop(0, n_all // ta, body, 0)
        set_shift(mp_scr[...])

    def body_a(t, carry):
        for u in range(unroll):
            r = key_tile(t * unroll + u)
            pt_scr[r, :] = jnp.exp2(jnp.dot(kp_scr[r, :], qq_scr[...], preferred_element_type=F32)).astype(BF16)
        return carry

    lax.fori_loop(0, n_all // (ta * unroll), body_a, 0)

    acc = jnp.dot(vt_scr[...], pt_scr[...], preferred_element_type=F32)
    ot = acc[0:DIFF_DV, :] / acc[DIFF_DV:DIFF_DV + 1, :]
    d = (ot[:, 0:tq] - lam_ref[0] * ot[:, tq:rows]).T
    ms = jnp.mean(d * d, axis=-1, keepdims=True)
    y = d * lax.rsqrt(ms + EPS) * gn_ref[...] * (1.0 - LAMBDA_INIT)
    o_ref[...] = y.astype(BF16)


def _diff_attention(lam, p_lat, p_ctx, diff_norm_g, gmat, batch, seq, ctx_len):
    tq = min(512, seq)
    tk = min(512, seq)
    nq, nk = seq // tq, seq // tk
    n_all = ctx_len + seq
    ta = next(t for t in (528, 512, 384, 320, 256, 128) if n_all % t == 0)
    unroll = math.gcd(n_all // ta, KV_UNROLL)
    assert ctx_len % LANES == 0
    kern = functools.partial(_dattn_kernel, tq=tq, tk=tk, n_kv=nk, ta=ta, unroll=unroll, ctx_len=ctx_len)
    return pl.pallas_call(
        kern,
        out_shape=jax.ShapeDtypeStruct((batch * seq, DIFF_HEADS * DIFF_DV), BF16),
        grid=(batch, DIFF_HEADS, nq),
        in_specs=[pl.BlockSpec(memory_space=pltpu.SMEM),
                  pl.BlockSpec((tq, LANES), lambda b, h, qi: (b * nq + qi, 48 + h)),
                  pl.BlockSpec((ctx_len, LANES), lambda b, h, qi: (b, 56 + h)),
                  pl.BlockSpec((ctx_len, LANES), lambda b, h, qi: (b, 64 + h)),
                  pl.BlockSpec((seq, LANES), lambda b, h, qi: (b, 56 + h)),
                  pl.BlockSpec((seq, LANES), lambda b, h, qi: (b, 64 + h)),
                  pl.BlockSpec((1, LANES), lambda b, h, qi: (0, 0)),
                  pl.BlockSpec((LANES, LANES), lambda b, h, qi: (0, 0))],
        out_specs=pl.BlockSpec((tq, LANES), lambda b, h, qi: (b * nq + qi, h)),
        scratch_shapes=[pltpu.VMEM((2 * LANES, 2 * tq), BF16), pltpu.VMEM((1, LANES), F32),
                        pltpu.VMEM((1, 2 * tq), F32), pltpu.VMEM((n_all, 2 * LANES), BF16),
                        pltpu.VMEM((DIFF_DV + VT_PAD, n_all), BF16), pltpu.VMEM((n_all, 2 * tq), BF16)],
        compiler_params=_params(("arbitrary",) * 3, 48),
        name="dattn",
    )(lam, p_lat, p_ctx, p_ctx, p_lat, p_lat, diff_norm_g.reshape(1, DIFF_DV), gmat)


def _merge_kernel(yr_ref, yd_ref, ga_ref, gb_ref, x_ref, gatea_ref, shf_ref, scf_ref, n2_ref,
                  wr_ref, wd_ref, wo_ref, rwh_ref, rwl_ref, rb_ref,
                  xn_ref, hp_ref, meta_ref, gate_ref, cnt_ref, run_scr, *, tm):
    i = pl.program_id(0)

    @pl.when(i == 0)
    def _():
        run_scr[...] = jnp.zeros(run_scr.shape, F32)

    yr = jnp.dot(yr_ref[...], wr_ref[...], preferred_element_type=F32)
    yd = jnp.dot(yd_ref[...], wd_ref[...], preferred_element_type=F32)
    m = (jax.nn.sigmoid(ga_ref[...].astype(F32)) * yr + jax.nn.sigmoid(gb_ref[...].astype(F32)) * yd)
    z = jnp.dot(m.astype(BF16), wo_ref[...], preferred_element_type=F32)
    xn = x_ref[...] + gatea_ref[0] * z
    xn_ref[...] = xn

    ms = jnp.mean(xn * xn, axis=-1, keepdims=True)
    h2 = xn * lax.rsqrt(ms + EPS) * n2_ref[...]
    h2 = h2 * (1.0 + scf_ref[0]) + shf_ref[0]
    h_hi = h2.astype(BF16)
    bits = pltpu.bitcast(h_hi.astype(F32), U32)
    half = D_MODEL // 2
    hp_ref[...] = (bits[:, :half] >> 16) | (bits[:, half:] & jnp.uint32(0xFFFF0000))

    h_lo = (h2 - h_hi.astype(F32)).astype(BF16)
    logits = (jnp.dot(h_hi, rwh_ref[...], preferred_element_type=F32)
              + jnp.dot(h_lo, rwh_ref[...], preferred_element_type=F32)
              + jnp.dot(h_hi, rwl_ref[...], preferred_element_type=F32) + rb_ref[...])
    lane = lax.broadcasted_iota(I32, (tm, LANES), 1)
    lanef = lane.astype(F32)
    work = logits
    vals, idxs = [], []
    for _ in range(TOP_K):
        mk = jnp.max(work, axis=1, keepdims=True)
        ik = jnp.min(jnp.where(work == mk, lanef, float(LANES)), axis=1, keepdims=True)
        vals.append(mk)
        idxs.append(ik)
        work = jnp.where(lanef == ik, -jnp.inf, work)
    ex = [jnp.exp(v - vals[0]) for v in vals]
    den = ex[0] + ex[1] + ex[2] + ex[3]

    onehot = jnp.zeros((tm, LANES), F32)
    for ik in idxs:
        onehot = onehot + jnp.where(lanef == ik, 1.0, 0.0)
    ri = lax.broadcasted_iota(I32, (tm, tm), 0)
    ci = lax.broadcasted_iota(I32, (tm, tm), 1)
    tri = jnp.where(ri > ci, 1.0, 0.0).astype(BF16)
    base = run_scr[0:1, :] + jnp.dot(tri, onehot.astype(BF16), preferred_element_type=F32)
    run_scr[...] = run_scr[...] + jnp.sum(onehot, axis=0, keepdims=True)

    meta = jnp.zeros((tm, LANES), F32)
    gate_out = jnp.zeros((tm, LANES), F32)
    for k in range(TOP_K):
        rk = jnp.sum(jnp.where(lanef == idxs[k], base, 0.0), axis=1, keepdims=True)
        meta = jnp.where(lane == k, idxs[k], jnp.where(lane == TOP_K + k, rk, meta))
        gate_out = jnp.where(lane == k, ex[k] / den, gate_out)
    meta_ref[...] = meta.T[0:2 * TOP_K, :]
    gate_ref[...] = gate_out
    cnt_ref[...] = run_scr[...]


def _merge(y_ret, y_diff, p_lat, x2, g_a, sh_f, sc_f, norm2_g, w_r, w_d, w_o, rw_hi, rw_lo, rb, seq):
    n = x2.shape[0]
    tm = min(512, seq)
    tpb = seq // tm
    kern = functools.partial(_merge_kernel, tm=tm)
    mod_spec = pl.BlockSpec((1, 1, D_MODEL), lambda i: (i // tpb, 0, 0))
    const = lambda shape: pl.BlockSpec(shape, lambda i: (0,) * len(shape))
    tok = lambda w: pl.BlockSpec((tm, w), lambda i: (i, 0))
    return pl.pallas_call(
        kern,
        out_shape=(jax.ShapeDtypeStruct((n, D_MODEL), F32),
                   jax.ShapeDtypeStruct((n, D_MODEL // 2), U32),
                   jax.ShapeDtypeStruct((2 * TOP_K, n), F32),
                   jax.ShapeDtypeStruct((n, LANES), F32),
                   jax.ShapeDtypeStruct((8, LANES), F32)),
        grid=(n // tm,),
        in_specs=[tok(RET_HEADS * RET_DV), tok(D_MODEL),
                  pl.BlockSpec((tm, COL_TILE), lambda i: (i, 9)),
                  pl.BlockSpec((tm, COL_TILE), lambda i: (i, 10)),
                  tok(D_MODEL), mod_spec, mod_spec, mod_spec, const((1, D_MODEL)),
                  const((RET_HEADS * RET_DV, D_MODEL)), const((D_MODEL, D_MODEL)), const((D_MODEL, D_MODEL)),
                  const((D_MODEL, LANES)), const((D_MODEL, LANES)), const((1, LANES))],
        out_specs=(tok(D_MODEL), tok(D_MODEL // 2), pl.BlockSpec((2 * TOP_K, tm), lambda i: (0, i)),
                   tok(LANES), const((8, LANES))),
        scratch_shapes=[pltpu.VMEM((8, LANES), F32)],
        compiler_params=_params(("arbitrary",), 56),
        name="merge",
    )(y_ret, y_diff, p_lat, p_lat, x2, g_a, sh_f, sc_f, norm2_g, w_r, w_d, w_o, rw_hi, rw_lo, rb)


def _dispatch_kernel(dest_ref, hp_ref, xb_in_ref, xb_ref, sem, *, tm, n_tok):
    del xb_in_ref
    i = pl.program_id(0)

    def body(g, carry):
        r0 = pl.multiple_of(g * SUBLANES, SUBLANES)
        for j in range(SUBLANES):
            for k in range(TOP_K):
                d = dest_ref[k * n_tok + i * tm + r0 + j]
                pltpu.make_async_copy(hp_ref.at[pl.ds(r0 + j, 1), :], xb_ref.at[pl.ds(d, 1), :], sem).start()
        return carry

    lax.fori_loop(0, tm // SUBLANES, body, 0)
    for _ in range(TOP_K):
        pltpu.make_async_copy(hp_ref, xb_ref.at[pl.ds(0, tm), :], sem).wait()


def _dispatch(dest, hp, xb_zero):
    n = hp.shape[0]
    tm = min(512, n)
    kern = functools.partial(_dispatch_kernel, tm=tm, n_tok=n)
    return pl.pallas_call(
        kern,
        out_shape=jax.ShapeDtypeStruct(xb_zero.shape, U32),
        grid_spec=pltpu.PrefetchScalarGridSpec(
            num_scalar_prefetch=1,
            grid=(n // tm,),
            in_specs=[pl.BlockSpec((tm, D_MODEL // 2), lambda i, d: (i, 0)),
                      pl.BlockSpec(memory_space=pl.ANY)],
            out_specs=pl.BlockSpec(memory_space=pl.ANY),
            scratch_shapes=[pltpu.SemaphoreType.DMA(())]),
        input_output_aliases={2: 0},
        compiler_params=_params(("arbitrary",), 32),
        name="dispatch",
    )(dest, hp, xb_zero)


def _expert_kernel(be_ref, nb_ref, nxt_ref, xb_ref, w1_hbm, b1_ref, w2_hbm, b2_ref, y_ref,
                   w1f_scr, w2f_scr, w1b_scr, w2b_scr, grp_scr, sems):
    i = pl.program_id(0)
    e = be_ref[i]
    prev = be_ref[jnp.maximum(i - 1, 0)]
    active = i < nb_ref[0]

    def weight_copies(expert, slot):
        return (pltpu.make_async_copy(w1_hbm.at[expert], w1f_scr.at[slot], sems.at[0, slot]),
                pltpu.make_async_copy(w2_hbm.at[expert], w2f_scr.at[slot], sems.at[1, slot]))

    @pl.when(i == 0)
    def _():
        grp_scr[0] = 0
        for cp in weight_copies(e, 0):
            cp.start()

    @pl.when(active & (i > 0) & (e != prev))
    def _():
        grp_scr[0] = grp_scr[0] + 1

    @pl.when(active & ((i == 0) | (e != prev)))
    def _():
        slot = grp_scr[0] % 2
        for cp in weight_copies(e, slot):
            cp.wait()
        w1b_scr[...] = w1f_scr[slot].astype(BF16)
        w2b_scr[...] = w2f_scr[slot].astype(BF16)
        nxt = nxt_ref[e]

        @pl.when(nxt >= 0)
        def _():
            for cp in weight_copies(nxt, 1 - slot):
                cp.start()

    @pl.when(active)
    def _():
        xu = xb_ref[...]
        x_lo = pltpu.bitcast(xu << 16, F32).astype(BF16)
        x_hi = pltpu.bitcast(xu & jnp.uint32(0xFFFF0000), F32).astype(BF16)
        half = D_MODEL // 2
        hh = (jnp.dot(x_lo, w1b_scr[0:half, :], preferred_element_type=F32)
              + jnp.dot(x_hi, w1b_scr[half:, :], preferred_element_type=F32) + b1_ref[0])
        glu = jnp.minimum(hh[:, :D_FF], SWIGLU_LIMIT)
        lin = jnp.clip(hh[:, D_FF:], -SWIGLU_LIMIT, SWIGLU_LIMIT)
        act = glu * jax.nn.sigmoid(SWIGLU_ALPHA * glu) * (lin + 1.0)
        y_ref[...] = jnp.dot(act.astype(BF16), w2b_scr[...], preferred_element_type=F32) + b2_ref[0]

    @pl.when(i >= nb_ref[0])
    def _():
        y_ref[...] = jnp.zeros(y_ref.shape, F32)


def _experts(block_e, n_used, next_e, xb, w1, b1, w2, b2):
    rows = xb.shape[0]
    n_blocks = rows // MOE_BLK
    return pl.pallas_call(
        _expert_kernel,
        out_shape=jax.ShapeDtypeStruct((rows, D_MODEL), F32),
        grid_spec=pltpu.PrefetchScalarGridSpec(
            num_scalar_prefetch=3,
            grid=(n_blocks,),
            in_specs=[pl.BlockSpec((MOE_BLK, D_MODEL // 2), lambda i, be, nb, nx: (i, 0)),
                      pl.BlockSpec(memory_space=pl.ANY),
                      pl.BlockSpec((1, 1, 2 * D_FF), lambda i, be, nb, nx: (be[i], 0, 0)),
                      pl.BlockSpec(memory_space=pl.ANY),
                      pl.BlockSpec((1, 1, D_MODEL), lambda i, be, nb, nx: (be[i], 0, 0))],
            out_specs=pl.BlockSpec((MOE_BLK, D_MODEL), lambda i, be, nb, nx: (i, 0)),
            scratch_shapes=[pltpu.VMEM((2, D_MODEL, 2 * D_FF), F32), pltpu.VMEM((2, D_FF, D_MODEL), F32),
                            pltpu.VMEM((D_MODEL, 2 * D_FF), BF16), pltpu.VMEM((D_FF, D_MODEL), BF16),
                            pltpu.SMEM((1,), I32), pltpu.SemaphoreType.DMA((2, 2))]),
        compiler_params=_params(("arbitrary",), 56),
        name="expert",
    )(block_e, n_used, next_e, xb, w1, b1.reshape(N_EXPERTS, 1, 2 * D_FF), w2, b2.reshape(N_EXPERTS, 1, D_MODEL))


def _combine_kernel(dest_ref, yb_ref, gate_ref, xn_ref, gf_ref, o_ref, buf, sem, *, tm, n_tok):
    i = pl.program_id(0)

    def body(g, carry):
        r0 = pl.multiple_of(g * SUBLANES, SUBLANES)
        for j in range(SUBLANES):
            for k in range(TOP_K):
                d = dest_ref[k * n_tok + i * tm + r0 + j]
                pltpu.make_async_copy(yb_ref.at[pl.ds(d, 1), :], buf.at[k, pl.ds(r0 + j, 1), :], sem).start()
        return carry

    lax.fori_loop(0, tm // SUBLANES, body, 0)
    for k in range(TOP_K):
        pltpu.make_async_copy(yb_ref.at[pl.ds(0, tm), :], buf.at[k], sem).wait()
    g = gate_ref[...]
    y = g[:, 0:1] * buf[0]
    for k in range(1, TOP_K):
        y = y + g[:, k:k + 1] * buf[k]
    o_ref[...] = xn_ref[...] + gf_ref[0] * y


def _combine(dest, yb, gate, xn, g_f, seq):
    n = xn.shape[0]
    tm = min(256, seq)
    tpb = seq // tm
    kern = functools.partial(_combine_kernel, tm=tm, n_tok=n)
    return pl.pallas_call(
        kern,
        out_shape=jax.ShapeDtypeStruct((n, D_MODEL), F32),
        grid_spec=pltpu.PrefetchScalarGridSpec(
            num_scalar_prefetch=1,
            grid=(n // tm,),
            in_specs=[pl.BlockSpec(memory_space=pl.ANY),
                      pl.BlockSpec((tm, LANES), lambda i, d: (i, 0)),
                      pl.BlockSpec((tm, D_MODEL), lambda i, d: (i, 0)),
                      pl.BlockSpec((1, 1, D_MODEL), lambda i, d: (i // tpb, 0, 0))],
            out_specs=pl.BlockSpec((tm, D_MODEL), lambda i, d: (i, 0)),
            scratch_shapes=[pltpu.VMEM((TOP_K, tm, D_MODEL), F32), pltpu.SemaphoreType.DMA(())]),
        compiler_params=_params(("arbitrary",), 32),
        name="combine",
    )(dest, yb, gate, xn, g_f)


def _layer(x, ctx, c, c_ctx, norm1_g, norm2_g, w_mod, b_mod, w_in, ret_decay_logit, ret_norm_g,
           diff_q_norm_g, diff_k_norm_g, diff_lambda, diff_norm_g, w_br_ret, w_br_diff, w_out,
           router_w, router_b, exp_w1, exp_b1, exp_w2, exp_b2):
    batch, seq, d = x.shape
    ctx_len = ctx.shape[1]
    assert d == D_MODEL and seq % GRID_W == 0 and batch + 1 <= 8
    n_tok = batch * seq

    cc = jnp.zeros((8, D_MODEL), F32).at[:batch].set(c).at[batch].set(c_ctx)
    mod = _mod(cc, w_mod, b_mod)
    sh_a, sc_a, g_a, sh_f, sc_f, g_f = [mod[:batch, i * D_MODEL:(i + 1) * D_MODEL].reshape(batch, 1, D_MODEL)
                                         for i in range(6)]
    csh_a = mod[batch:batch + 1, 0:D_MODEL].reshape(1, 1, D_MODEL)
    csc_a = mod[batch:batch + 1, D_MODEL:2 * D_MODEL].reshape(1, 1, D_MODEL)

    w_in_bf = w_in.astype(BF16)
    g1 = norm1_g.reshape(1, D_MODEL)
    tile = lambda g: jnp.tile(g.astype(F32), 2)
    qkg = jnp.zeros((8, LANES), F32).at[0].set(tile(diff_q_norm_g) * (DIFF_DH ** -0.5 * LOG2E)).at[1].set(tile(diff_k_norm_g))
    lane = jnp.arange(2 * LANES)
    same_group = lane[:, None] // DIFF_DH == lane[None, :] // DIFF_DH
    gmean = jnp.where(same_group, 1.0 / DIFF_DH, 0.0).astype(BF16)
    gmat = same_group[:LANES, :LANES].astype(BF16)
    x2 = x.reshape(n_tok, D_MODEL)
    tm = min(1024, seq)
    p_lat = _inproj(x2, g1, sh_a, sc_a, w_in_bf, qkg, gmean, _rope_tables(seq), tm, seq // tm)
    p_ctx = _inproj(ctx.reshape(batch * ctx_len, D_MODEL), g1, csh_a, csc_a, w_in_bf, qkg, gmean,
                    _identity_tables(ctx_len), ctx_len, 1)

    lg = jax.nn.log_sigmoid(ret_decay_logit.astype(F32))
    y_ret = _retention(lg, p_lat, p_ctx, ret_norm_g, batch, seq, ctx_len)

    lp = diff_lambda.astype(F32)
    lam = (jnp.exp(jnp.sum(lp[0] * lp[1])) - jnp.exp(jnp.sum(lp[2] * lp[3])) + LAMBDA_INIT).reshape(1)
    y_diff = _diff_attention(lam, p_lat, p_ctx, diff_norm_g, gmat, batch, seq, ctx_len)

    rw = jnp.zeros((D_MODEL, LANES), F32).at[:, :N_EXPERTS].set(router_w)
    rw_hi = rw.astype(BF16)
    rw_lo = (rw - rw_hi.astype(F32)).astype(BF16)
    rb = jnp.full((1, LANES), NEG_BIG, F32).at[0, :N_EXPERTS].set(router_b)
    xn, hp, meta, gate4, cnt = _merge(
        y_ret, y_diff, p_lat, x2, g_a, sh_f, sc_f, norm2_g.reshape(1, D_MODEL),
        w_br_ret.astype(BF16), w_br_diff.astype(BF16), w_out.astype(BF16), rw_hi, rw_lo, rb, seq)

    counts = cnt[0, :N_EXPERTS].astype(I32)
    padded = (counts + MOE_BLK - 1) // MOE_BLK * MOE_BLK
    pad_end = jnp.cumsum(padded)
    pad_start = pad_end - padded
    n_pairs = n_tok * TOP_K
    n_blocks = n_pairs // MOE_BLK + N_EXPERTS
    meta_i = meta.astype(I32)
    is_e = meta_i[None, :TOP_K] == jnp.arange(N_EXPERTS, dtype=I32)[:, None, None]
    dest = (jnp.sum(jnp.where(is_e, pad_start[:, None, None], 0), axis=0) + meta_i[TOP_K:]).reshape(n_pairs)
    block_start = jnp.arange(n_blocks, dtype=I32) * MOE_BLK
    block_e = jnp.minimum(jnp.sum((pad_end[None, :] <= block_start[:, None]).astype(I32), axis=1), N_EXPERTS - 1)
    n_used = (pad_end[-1] // MOE_BLK).reshape(1).astype(I32)

    xb = _dispatch(dest, hp, jnp.zeros((n_blocks * MOE_BLK, D_MODEL // 2), U32))
    e_ids = jnp.arange(N_EXPERTS, dtype=I32)
    later = jnp.where((counts[None, :] > 0) & (e_ids[None, :] > e_ids[:, None]), e_ids[None, :], N_EXPERTS)
    next_e = jnp.min(later, axis=1)
    next_e = jnp.where(next_e == N_EXPERTS, -1, next_e).astype(I32)
    yb = _experts(block_e, n_used, next_e, xb, exp_w1, exp_b1, exp_w2, exp_b2)
    out = _combine(dest, yb, gate4, xn, g_f, seq)
    return out.reshape(batch, seq, D_MODEL)


def kernel(x, c, ctx, c_ctx, norm1_g, norm2_g, w_mod, b_mod, w_in, ret_decay_logit, ret_norm_g, diff_q_norm_g, diff_k_norm_g, diff_lambda, diff_norm_g, w_br_ret, w_br_diff, w_out, router_w, router_b, exp_w1, exp_b1, exp_w2, exp_b2):
    assert norm1_g.shape[0] == 1, "single-layer block"
    return _layer(x, ctx, c, c_ctx, norm1_g[0], norm2_g[0], w_mod[0], b_mod[0], w_in[0], ret_decay_logit[0],
                  ret_norm_g[0], diff_q_norm_g[0], diff_k_norm_g[0], diff_lambda[0], diff_norm_g[0],
                  w_br_ret[0], w_br_diff[0], w_out[0], router_w[0], router_b[0],
                  exp_w1[0], exp_b1[0], exp_w2[0], exp_b2[0])
```

```python
import functools
import math

import numpy as np

import jax
import jax.numpy as jnp
from jax import lax
from jax.experimental import pallas as pl
from jax.experimental.pallas import tpu as pltpu

F32 = jnp.float32
BF16 = jnp.bfloat16
U32 = jnp.uint32
I32 = jnp.int32

D_MODEL = 1024
GRID_W = 64
RET_HEADS = 4
RET_DK = 256
RET_DV = 512
DIFF_DH = 64
DIFF_HEADS = 8
DIFF_DV = 128
N_EXPERTS = 32
TOP_K = 4
D_FF = 1024
SWIGLU_LIMIT = 7.0
SWIGLU_ALPHA = 1.702
ROPE_BASE = 10000.0
EPS = 1e-6
LAMBDA_INIT = 0.8 - 0.6 * math.exp(-0.3 * 0)

IN_COLS = 11264
COL_TILE = 1024
LANES = 128
SUBLANES = 8
MOE_BLK = 256
NEG_BIG = -1e30
LOG2E = 1.4426950408889634
SHIFT_SLACK = 1.0 + 2.0 ** -6
MAX_SAFE_SHIFT = 60.0
RET_SUB = 4
VT_PAD = 16
KV_UNROLL = 4
HIGHEST = lax.Precision.HIGHEST
MIB = 1024 * 1024


def _params(sem, vmem_mib):
    return pltpu.CompilerParams(dimension_semantics=sem, vmem_limit_bytes=vmem_mib * MIB)


def _mod_kernel(c_ref, w_ref, b_ref, o_ref):
    c = c_ref[...]
    s = c * jax.nn.sigmoid(c)
    o_ref[...] = jnp.dot(s, w_ref[...], preferred_element_type=F32, precision=HIGHEST) + b_ref[...]


def _mod(cc, w_mod, b_mod):
    n = w_mod.shape[1]
    tn = 1024
    return pl.pallas_call(
        _mod_kernel,
        out_shape=jax.ShapeDtypeStruct((8, n), F32),
        grid=(n // tn,),
        in_specs=[pl.BlockSpec((8, D_MODEL), lambda j: (0, 0)),
                  pl.BlockSpec((D_MODEL, tn), lambda j: (0, j)),
                  pl.BlockSpec((1, tn), lambda j: (0, j))],
        out_specs=pl.BlockSpec((8, tn), lambda j: (0, j)),
        compiler_params=_params(("arbitrary",), 32),
        name="mod",
    )(cc, w_mod, b_mod.reshape(1, n))


def _inproj_kernel(x_ref, g_ref, sh_ref, sc_ref, w_ref, qkg_ref, gmat_ref,
                   cr_ref, sr_ref, cc_ref, sc2_ref, cd_ref, sa_ref, sb_ref,
                   o_ref, h_scr, acc_scr):
    j = pl.program_id(1)

    @pl.when(j == 0)
    def _():
        xf = x_ref[...]
        ms = jnp.mean(xf * xf, axis=-1, keepdims=True)
        y = xf * lax.rsqrt(ms + EPS) * g_ref[...]
        h_scr[...] = (y * (1.0 + sc_ref[0]) + sh_ref[0]).astype(BF16)

    is_ret = j <= 1
    is_dqk = (j == 6) | (j == 7)

    def project():
        return jnp.dot(h_scr[...], w_ref[...], preferred_element_type=F32)

    @pl.when(is_ret)
    def _():
        acc_scr[...] = project()
        scale = jnp.where(j == 0, RET_DK ** -0.5, 1.0).astype(F32)
        for b in range(COL_TILE // LANES):
            xb = acc_scr[:, b * LANES:(b + 1) * LANES]
            cos = cr_ref[...] if b % 2 == 0 else cc_ref[...]
            sin = sr_ref[...] if b % 2 == 0 else sc2_ref[...]
            o = (xb * cos + pltpu.roll(xb, 64, 1) * sin) * scale
            o_ref[:, b * LANES:(b + 1) * LANES] = o.astype(BF16)

    @pl.when(is_dqk)
    def _():
        acc_scr[...] = project()
        g = jnp.where(j == 6, qkg_ref[0:1, :], qkg_ref[1:2, :])
        for b2 in range(COL_TILE // (2 * LANES)):
            x2 = acc_scr[:, b2 * 2 * LANES:(b2 + 1) * 2 * LANES]
            ms2 = jnp.dot((x2 * x2).astype(BF16), gmat_ref[...], preferred_element_type=F32)
            for half in range(2):
                b = 2 * b2 + half
                xb = x2[:, half * LANES:(half + 1) * LANES]
                yn = xb * lax.rsqrt(ms2[:, half * LANES:(half + 1) * LANES] + EPS) * g
                o = yn * cd_ref[...] + pltpu.roll(yn, 16, 1) * sa_ref[...] + pltpu.roll(yn, 112, 1) * sb_ref[...]
                o_ref[:, b * LANES:(b + 1) * LANES] = o.astype(BF16)

    @pl.when(jnp.logical_not(is_ret | is_dqk))
    def _():
        o_ref[...] = project().astype(BF16)


def _inproj(x2, g1, sh, sc, w_bf, qkg, gmat, tables, tm, tiles_per_batch):
    n = x2.shape[0]
    nb = sh.shape[0]
    tab_spec = pl.BlockSpec((tm, LANES), lambda i, j: (i % tiles_per_batch, 0))
    mod_spec = pl.BlockSpec((1, 1, D_MODEL), lambda i, j: (jnp.minimum(i // tiles_per_batch, nb - 1), 0, 0))
    return pl.pallas_call(
        _inproj_kernel,
        out_shape=jax.ShapeDtypeStruct((n, IN_COLS), BF16),
        grid=(n // tm, IN_COLS // COL_TILE),
        in_specs=[pl.BlockSpec((tm, D_MODEL), lambda i, j: (i, 0)),
                  pl.BlockSpec((1, D_MODEL), lambda i, j: (0, 0)),
                  mod_spec, mod_spec,
                  pl.BlockSpec((D_MODEL, COL_TILE), lambda i, j: (0, j)),
                  pl.BlockSpec((8, LANES), lambda i, j: (0, 0)),
                  pl.BlockSpec((2 * LANES, 2 * LANES), lambda i, j: (0, 0))] + [tab_spec] * 7,
        out_specs=pl.BlockSpec((tm, COL_TILE), lambda i, j: (i, j)),
        scratch_shapes=[pltpu.VMEM((tm, D_MODEL), BF16), pltpu.VMEM((tm, COL_TILE), F32)],
        compiler_params=_params(("arbitrary", "arbitrary"), 48),
        name="inproj",
    )(x2, g1, sh, sc, w_bf, qkg, gmat, *tables)


def _rope_tables(seq):
    n_rows = seq // GRID_W
    f32 = np.float32

    def angles(pos, half):
        inv = f32(ROPE_BASE) ** (-np.arange(half, dtype=f32) / f32(half))
        return (pos.astype(f32)[:, None] * inv[None, :]).astype(np.float64)

    ar, ac = angles(np.arange(n_rows), 64), angles(np.arange(GRID_W), 64)
    br, bc = angles(np.arange(n_rows), 16), angles(np.arange(GRID_W), 16)
    zr, zc = np.zeros_like(br), np.zeros_like(bc)
    cat = lambda parts, reps=1: np.tile(np.concatenate(parts, axis=1), (1, reps)).astype(f32)
    by_row = lambda t: jnp.repeat(jnp.asarray(t), GRID_W, axis=0)
    by_col = lambda t: jnp.tile(jnp.asarray(t), (n_rows, 1))
    cr = by_row(cat([np.cos(ar), np.cos(ar)]))
    sr = by_row(cat([-np.sin(ar), np.sin(ar)]))
    cc = by_col(cat([np.cos(ac), np.cos(ac)]))
    sc = by_col(cat([-np.sin(ac), np.sin(ac)]))
    cd = by_row(cat([np.cos(br), np.cos(br), zr, zr], 2)) + by_col(cat([zc, zc, np.cos(bc), np.cos(bc)], 2))
    sa = by_row(cat([zr, np.sin(br), zr, zr], 2)) + by_col(cat([zc, zc, zc, np.sin(bc)], 2))
    sb = by_row(cat([-np.sin(br), zr, zr, zr], 2)) + by_col(cat([zc, zc, -np.sin(bc), zc], 2))
    return [cr, sr, cc, sc, cd, sa, sb]


def _identity_tables(seq):
    one = jnp.ones((seq, LANES), F32)
    zero = jnp.zeros((seq, LANES), F32)
    return [one, zero, one, zero, one, zero, zero]


def _tn_dot(a, b):
    return lax.dot_general(a, b, (((0,), (0,)), ((), ())), preferred_element_type=F32)


def _nt_dot(a, b):
    return lax.dot_general(a, b, (((1,), (1,)), ((), ())), preferred_element_type=F32)


def _ret_kernel(lg_ref, q_ref, k_ref, v_ref, g_ref, ck_ref, cv_ref, gn_ref, o_ref,
                sf_scr, sb_scr, ob_scr, *, chunk, sub, n_blocks, ctx_len):
    h = pl.program_id(1)
    p = pl.program_id(2)
    c = pl.program_id(3)
    lgf = lg_ref[0, h]
    lgb = lg_ref[1, h]

    def col_iota(n):
        return lax.broadcasted_iota(I32, (n, 1), 0).astype(F32)

    def vexp(s):
        return jnp.exp(jnp.zeros((1, 1), F32) + s)

    @pl.when((p == 0) & (c == 0))
    def _():
        jc = col_iota(ctx_len)
        kc = ck_ref[...].astype(F32)
        vc = cv_ref[...]
        sf_scr[...] = _tn_dot((kc * jnp.exp(lgf * (ctx_len - 1.0 - jc))).astype(BF16), vc)
        sb_scr[...] = _tn_dot((kc * jnp.exp(lgb * jc)).astype(BF16), vc)

    ic = col_iota(chunk)

    @pl.when(p == 0)
    def _():
        q_decay = jnp.exp(lgb * (chunk - ic))
        k_decay = jnp.exp(lgb * ic)
        s_decay = vexp(lgb * chunk)
        for j in reversed(range(sub)):
            loc = pl.ds(j * chunk, chunk)
            glob = pl.ds(pl.multiple_of(((n_blocks - 1 - c) * sub + j) * chunk, chunk), chunk)
            qb = (q_ref[loc, :].astype(F32) * q_decay).astype(BF16)
            ob_scr[glob, :] = jnp.dot(qb, sb_scr[...].astype(BF16), preferred_element_type=F32)
            kb = (k_ref[loc, :].astype(F32) * k_decay).astype(BF16)
            sb_scr[...] = s_decay * sb_scr[...] + _tn_dot(kb, v_ref[loc, :])

    @pl.when(p == 1)
    def _():
        ri = lax.broadcasted_iota(I32, (chunk, chunk), 0)
        ci = lax.broadcasted_iota(I32, (chunk, chunk), 1)
        d = (ri - ci).astype(F32)
        mask = jnp.where(d > 0, jnp.exp(lgf * jnp.maximum(d, 0.0)),
                         jnp.where(d < 0, jnp.exp(lgb * jnp.maximum(-d, 0.0)), 2.0))
        q_decay = jnp.exp(lgf * (ic + 1.0))
        k_decay = jnp.exp(lgf * (chunk - 1.0 - ic))
        s_decay = vexp(lgf * chunk)
        for j in range(sub):
            loc = pl.ds(j * chunk, chunk)
            glob = pl.ds(pl.multiple_of((c * sub + j) * chunk, chunk), chunk)
            a = (_nt_dot(q_ref[loc, :], k_ref[loc, :]) * mask).astype(BF16)
            qf = (q_ref[loc, :].astype(F32) * q_decay).astype(BF16)
            o = (jnp.dot(a, v_ref[loc, :], preferred_element_type=F32)
                 + jnp.dot(qf, sf_scr[...].astype(BF16), preferred_element_type=F32)
                 + ob_scr[glob, :])
            kf = (k_ref[loc, :].astype(F32) * k_decay).astype(BF16)
            sf_scr[...] = s_decay * sf_scr[...] + _tn_dot(kf, v_ref[loc, :])
            ms = jnp.mean(o * o, axis=-1, keepdims=True)
            y = o * lax.rsqrt(ms + EPS) * gn_ref[0]
            gt = g_ref[loc, :].astype(F32)
            o_ref[loc, :] = (y * (gt * jax.nn.sigmoid(gt))).astype(BF16)


def _retention(lg, p_lat, p_ctx, ret_norm_g, batch, seq, ctx_len):
    chunk = min(256, seq)
    sub = math.gcd(seq // chunk, RET_SUB)
    blk = chunk * sub
    nb = seq // blk
    kern = functools.partial(_ret_kernel, chunk=chunk, sub=sub, n_blocks=nb, ctx_len=ctx_len)

    def rows(b, p, c):
        return b * nb + jnp.where(p == 0, nb - 1 - c, c)

    def rows_fwd(b, p, c):
        return b * nb + jnp.where(p == 0, 0, c)

    return pl.pallas_call(
        kern,
        out_shape=jax.ShapeDtypeStruct((batch * seq, RET_HEADS * RET_DV), BF16),
        grid=(batch, RET_HEADS, 2, nb),
        in_specs=[pl.BlockSpec(memory_space=pltpu.SMEM),
                  pl.BlockSpec((blk, RET_DK), lambda b, h, p, c: (rows(b, p, c), h)),
                  pl.BlockSpec((blk, RET_DK), lambda b, h, p, c: (rows(b, p, c), 4 + h)),
                  pl.BlockSpec((blk, RET_DV), lambda b, h, p, c: (rows(b, p, c), 4 + h)),
                  pl.BlockSpec((blk, RET_DV), lambda b, h, p, c: (rows_fwd(b, p, c), 8 + h)),
                  pl.BlockSpec((ctx_len, RET_DK), lambda b, h, p, c: (b, 4 + h)),
                  pl.BlockSpec((ctx_len, RET_DV), lambda b, h, p, c: (b, 4 + h)),
                  pl.BlockSpec((1, 1, RET_DV), lambda b, h, p, c: (h, 0, 0))],
        out_specs=pl.BlockSpec((blk, RET_DV), lambda b, h, p, c: (rows_fwd(b, p, c), h)),
        scratch_shapes=[pltpu.VMEM((RET_DK, RET_DV), F32), pltpu.VMEM((RET_DK, RET_DV), F32),
                        pltpu.VMEM((seq, RET_DV), F32)],
        compiler_params=_params(("arbitrary",) * 4, 48),
        name="ret",
    )(lg, p_lat, p_lat, p_lat, p_lat, p_ctx, p_ctx, ret_norm_g.reshape(RET_HEADS, 1, RET_DV))


def _dattn_kernel(lam_ref, q_ref, ck_ref, cv_ref, k_ref, v_ref, gn_ref, gmat_ref, o_ref,
                  qq_scr, kmax_scr, mp_scr, kp_scr, vt_scr, pt_scr, *, tq, tk, n_kv, ta, unroll, ctx_len):
    rows = 2 * tq
    n_all = kp_scr.shape[0]
    qi = pl.program_id(2)

    def sq_norms(t):
        tf = t.astype(F32)
        return jnp.dot((tf * tf).astype(BF16), gmat_ref[...], preferred_element_type=F32)

    def lane_const(n, hot):
        return jnp.where(lax.broadcasted_iota(I32, (n, LANES), 1) < hot, 1.0, 0.0).astype(BF16)

    @pl.when(qi == 0)
    def _():
        kmax_scr[...] = jnp.max(sq_norms(ck_ref[...]), axis=0, keepdims=True)
        kp_scr[:, LANES:2 * LANES] = lane_const(n_all, 3)
        kp_scr[0:ctx_len, 0:LANES] = ck_ref[...]
        tail = lax.broadcasted_iota(I32, (VT_PAD, n_all), 0)
        vt_scr[DIFF_DV:, :] = jnp.where(tail == 0, 1.0, 0.0).astype(BF16)
        vt_scr[0:DIFF_DV, 0:ctx_len] = cv_ref[...].astype(F32).T.astype(BF16)

        def body(c, carry):
            start = pl.multiple_of(c * tk, tk)
            k = k_ref[pl.ds(start, tk), :]
            kmax_scr[...] = jnp.maximum(kmax_scr[...], jnp.max(sq_norms(k), axis=0, keepdims=True))
            kp_scr[pl.ds(pl.multiple_of(ctx_len + start, LANES), tk), 0:LANES] = k
            vt_scr[0:DIFF_DV, pl.ds(pl.multiple_of(ctx_len + start, LANES), tk)] = (
                v_ref[pl.ds(start, tk), :].astype(F32).T.astype(BF16))
            return carry

        lax.fori_loop(0, n_kv, body, 0)

    qt = q_ref[...].astype(F32).T
    row = lax.broadcasted_iota(I32, (LANES, tq), 0)
    q1t = jnp.where(row < DIFF_DH, qt, 0.0)
    q2t = jnp.where(row >= DIFF_DH, qt, 0.0)
    qq_scr[0:LANES, 0:tq] = q1t.astype(BF16)
    qq_scr[0:LANES, tq:rows] = q2t.astype(BF16)

    def set_shift(shift):
        neg = -shift
        hi = neg.astype(BF16).astype(F32)
        mid = (neg - hi).astype(BF16).astype(F32)
        lo = neg - hi - mid
        row_r = lax.broadcasted_iota(I32, (LANES, rows), 0)
        pieces = jnp.where(row_r == 0, hi, jnp.where(row_r == 1, mid, jnp.where(row_r == 2, lo, 0.0)))
        qq_scr[LANES:2 * LANES, :] = pieces.astype(BF16)

    kmax = kmax_scr[...]
    b1 = jnp.sqrt(jnp.sum(q1t * q1t, axis=0, keepdims=True) * kmax[:, 0:1]) * SHIFT_SLACK
    b2 = jnp.sqrt(jnp.sum(q2t * q2t, axis=0, keepdims=True) * kmax[:, DIFF_DH:DIFF_DH + 1]) * SHIFT_SLACK
    bound = jnp.concatenate([b1, b2], axis=1)
    set_shift(bound)

    def key_tile(t):
        return pl.ds(pl.multiple_of(t * ta, ta), ta)

    @pl.when(jnp.max(bound) > MAX_SAFE_SHIFT)
    def _():
        mp_scr[...] = jnp.full(mp_scr.shape, NEG_BIG, F32)

        def body(t, carry):
            st = jnp.dot(kp_scr[key_tile(t), 0:LANES], qq_scr[0:LANES, :], preferred_element_type=F32)
            mp_scr[...] = jnp.maximum(mp_scr[...], jnp.max(st, axis=0, keepdims=True))
            return carry

        lax.fori_loop(0, n_all // ta, body, 0)
        set_shift(mp_scr[...])

    def body_a(t, carry):
        for u in range(unroll):
            r = key_tile(t * unroll + u)
            pt_scr[r, :] = jnp.exp2(jnp.dot(kp_scr[r, :], qq_scr[...], preferred_element_type=F32)).astype(BF16)
        return carry

    lax.fori_loop(0, n_all // (ta * unroll), body_a, 0)

    acc = jnp.dot(vt_scr[...], pt_scr[...], preferred_element_type=F32)
    ot = acc[0:DIFF_DV, :] / acc[DIFF_DV:DIFF_DV + 1, :]
    d = (ot[:, 0:tq] - lam_ref[0] * ot[:, tq:rows]).T
    ms = jnp.mean(d * d, axis=-1, keepdims=True)
    y = d * lax.rsqrt(ms + EPS) * gn_ref[...] * (1.0 - LAMBDA_INIT)
    o_ref[...] = y.astype(BF16)


def _diff_attention(lam, p_lat, p_ctx, diff_norm_g, gmat, batch, seq, ctx_len):
    tq = min(512, seq)
    tk = min(512, seq)
    nq, nk = seq // tq, seq // tk
    n_all = ctx_len + seq
    ta = next(t for t in (528, 512, 384, 320, 256, 128) if n_all % t == 0)
    unroll = math.gcd(n_all // ta, KV_UNROLL)
    assert ctx_len % LANES == 0
    kern = functools.partial(_dattn_kernel, tq=tq, tk=tk, n_kv=nk, ta=ta, unroll=unroll, ctx_len=ctx_len)
    return pl.pallas_call(
        kern,
        out_shape=jax.ShapeDtypeStruct((batch * seq, DIFF_HEADS * DIFF_DV), BF16),
        grid=(batch, DIFF_HEADS, nq),
        in_specs=[pl.BlockSpec(memory_space=pltpu.SMEM),
                  pl.BlockSpec((tq, LANES), lambda b, h, qi: (b * nq + qi, 48 + h)),
                  pl.BlockSpec((ctx_len, LANES), lambda b, h, qi: (b, 56 + h)),
                  pl.BlockSpec((ctx_len, LANES), lambda b, h, qi: (b, 64 + h)),
                  pl.BlockSpec((seq, LANES), lambda b, h, qi: (b, 56 + h)),
                  pl.BlockSpec((seq, LANES), lambda b, h, qi: (b, 64 + h)),
                  pl.BlockSpec((1, LANES), lambda b, h, qi: (0, 0)),
                  pl.BlockSpec((LANES, LANES), lambda b, h, qi: (0, 0))],
        out_specs=pl.BlockSpec((tq, LANES), lambda b, h, qi: (b * nq + qi, h)),
        scratch_shapes=[pltpu.VMEM((2 * LANES, 2 * tq), BF16), pltpu.VMEM((1, LANES), F32),
                        pltpu.VMEM((1, 2 * tq), F32), pltpu.VMEM((n_all, 2 * LANES), BF16),
                        pltpu.VMEM((DIFF_DV + VT_PAD, n_all), BF16), pltpu.VMEM((n_all, 2 * tq), BF16)],
        compiler_params=_params(("arbitrary",) * 3, 48),
        name="dattn",
    )(lam, p_lat, p_ctx, p_ctx, p_lat, p_lat, diff_norm_g.reshape(1, DIFF_DV), gmat)


def _merge_kernel(yr_ref, yd_ref, ga_ref, gb_ref, x_ref, gatea_ref, shf_ref, scf_ref, n2_ref,
                  wr_ref, wd_ref, wo_ref, rwh_ref, rwl_ref, rb_ref,
                  xn_ref, hp_ref, meta_ref, gate_ref, cnt_ref, run_scr, *, tm):
    i = pl.program_id(0)

    @pl.when(i == 0)
    def _():
        run_scr[...] = jnp.zeros(run_scr.shape, F32)

    yr = jnp.dot(yr_ref[...], wr_ref[...], preferred_element_type=F32)
    yd = jnp.dot(yd_ref[...], wd_ref[...], preferred_element_type=F32)
    m = (jax.nn.sigmoid(ga_ref[...].astype(F32)) * yr + jax.nn.sigmoid(gb_ref[...].astype(F32)) * yd)
    z = jnp.dot(m.astype(BF16), wo_ref[...], preferred_element_type=F32)
    xn = x_ref[...] + gatea_ref[0] * z
    xn_ref[...] = xn

    ms = jnp.mean(xn * xn, axis=-1, keepdims=True)
    h2 = xn * lax.rsqrt(ms + EPS) * n2_ref[...]
    h2 = h2 * (1.0 + scf_ref[0]) + shf_ref[0]
    h_hi = h2.astype(BF16)
    bits = pltpu.bitcast(h_hi.astype(F32), U32)
    half = D_MODEL // 2
    hp_ref[...] = (bits[:, :half] >> 16) | (bits[:, half:] & jnp.uint32(0xFFFF0000))

    h_lo = (h2 - h_hi.astype(F32)).astype(BF16)
    logits = (jnp.dot(h_hi, rwh_ref[...], preferred_element_type=F32)
              + jnp.dot(h_lo, rwh_ref[...], preferred_element_type=F32)
              + jnp.dot(h_hi, rwl_ref[...], preferred_element_type=F32) + rb_ref[...])
    lane = lax.broadcasted_iota(I32, (tm, LANES), 1)
    lanef = lane.astype(F32)
    work = logits
    vals, idxs = [], []
    for _ in range(TOP_K):
        mk = jnp.max(work, axis=1, keepdims=True)
        ik = jnp.min(jnp.where(work == mk, lanef, float(LANES)), axis=1, keepdims=True)
        vals.append(mk)
        idxs.append(ik)
        work = jnp.where(lanef == ik, -jnp.inf, work)
    ex = [jnp.exp(v - vals[0]) for v in vals]
    den = ex[0] + ex[1] + ex[2] + ex[3]

    onehot = jnp.zeros((tm, LANES), F32)
    for ik in idxs:
        onehot = onehot + jnp.where(lanef == ik, 1.0, 0.0)
    ri = lax.broadcasted_iota(I32, (tm, tm), 0)
    ci = lax.broadcasted_iota(I32, (tm, tm), 1)
    tri = jnp.where(ri > ci, 1.0, 0.0).astype(BF16)
    base = run_scr[0:1, :] + jnp.dot(tri, onehot.astype(BF16), preferred_element_type=F32)
    run_scr[...] = run_scr[...] + jnp.sum(onehot, axis=0, keepdims=True)

    meta = jnp.zeros((tm, LANES), F32)
    gate_out = jnp.zeros((tm, LANES), F32)
    for k in range(TOP_K):
        rk = jnp.sum(jnp.where(lanef == idxs[k], base, 0.0), axis=1, keepdims=True)
        meta = jnp.where(lane == k, idxs[k], jnp.where(lane == TOP_K + k, rk, meta))
        gate_out = jnp.where(lane == k, ex[k] / den, gate_out)
    meta_ref[...] = meta.T[0:2 * TOP_K, :]
    gate_ref[...] = gate_out
    cnt_ref[...] = run_scr[...]


def _merge(y_ret, y_diff, p_lat, x2, g_a, sh_f, sc_f, norm2_g, w_r, w_d, w_o, rw_hi, rw_lo, rb, seq):
    n = x2.shape[0]
    tm = min(512, seq)
    tpb = seq // tm
    kern = functools.partial(_merge_kernel, tm=tm)
    mod_spec = pl.BlockSpec((1, 1, D_MODEL), lambda i: (i // tpb, 0, 0))
    const = lambda shape: pl.BlockSpec(shape, lambda i: (0,) * len(shape))
    tok = lambda w: pl.BlockSpec((tm, w), lambda i: (i, 0))
    return pl.pallas_call(
        kern,
        out_shape=(jax.ShapeDtypeStruct((n, D_MODEL), F32),
                   jax.ShapeDtypeStruct((n, D_MODEL // 2), U32),
                   jax.ShapeDtypeStruct((2 * TOP_K, n), F32),
                   jax.ShapeDtypeStruct((n, LANES), F32),
                   jax.ShapeDtypeStruct((8, LANES), F32)),
        grid=(n // tm,),
        in_specs=[tok(RET_HEADS * RET_DV), tok(D_MODEL),
                  pl.BlockSpec((tm, COL_TILE), lambda i: (i, 9)),
                  pl.BlockSpec((tm, COL_TILE), lambda i: (i, 10)),
                  tok(D_MODEL), mod_spec, mod_spec, mod_spec, const((1, D_MODEL)),
                  const((RET_HEADS * RET_DV, D_MODEL)), const((D_MODEL, D_MODEL)), const((D_MODEL, D_MODEL)),
                  const((D_MODEL, LANES)), const((D_MODEL, LANES)), const((1, LANES))],
        out_specs=(tok(D_MODEL), tok(D_MODEL // 2), pl.BlockSpec((2 * TOP_K, tm), lambda i: (0, i)),
                   tok(LANES), const((8, LANES))),
        scratch_shapes=[pltpu.VMEM((8, LANES), F32)],
        compiler_params=_params(("arbitrary",), 56),
        name="merge",
    )(y_ret, y_diff, p_lat, p_lat, x2, g_a, sh_f, sc_f, norm2_g, w_r, w_d, w_o, rw_hi, rw_lo, rb)


def _expert_kernel(be_ref, nb_ref, nxt_ref, rt_ref, hp_hbm, w1_hbm, b1_ref, w2_hbm, b2_ref, y_ref,
                   w1f_scr, w2f_scr, w1b_scr, w2b_scr, xg_scr, x_scr, grp_scr, sems, gsems):
    i = pl.program_id(0)
    e = be_ref[i]
    prev = be_ref[jnp.maximum(i - 1, 0)]
    active = i < nb_ref[0]

    def weight_copies(expert, slot):
        return (pltpu.make_async_copy(w1_hbm.at[expert], w1f_scr.at[slot], sems.at[0, slot]),
                pltpu.make_async_copy(w2_hbm.at[expert], w2f_scr.at[slot], sems.at[1, slot]))

    def row_copy(block, r, slot):
        tok = rt_ref[block * MOE_BLK + r]
        return pltpu.make_async_copy(hp_hbm.at[pl.ds(tok, 1), :], xg_scr.at[slot, pl.ds(r, 1), :], gsems.at[slot])

    def rows_wait(slot):
        pltpu.make_async_copy(hp_hbm.at[pl.ds(0, MOE_BLK), :], xg_scr.at[slot], gsems.at[slot]).wait()

    @pl.when(i == 0)
    def _():
        grp_scr[0] = 0
        for cp in weight_copies(e, 0):
            cp.start()

        def body(r, carry):
            row_copy(0, r, 0).start()
            return carry

        lax.fori_loop(0, MOE_BLK, body, 0)

    @pl.when(active & (i > 0) & (e != prev))
    def _():
        grp_scr[0] = grp_scr[0] + 1

    @pl.when(active & ((i == 0) | (e != prev)))
    def _():
        slot = grp_scr[0] % 2
        for cp in weight_copies(e, slot):
            cp.wait()
        w1b_scr[...] = w1f_scr[slot].astype(BF16)
        w2b_scr[...] = w2f_scr[slot].astype(BF16)
        nxt = nxt_ref[e]

        @pl.when(nxt >= 0)
        def _():
            for cp in weight_copies(nxt, 1 - slot):
                cp.start()

    @pl.when(active)
    def _():
        slot = lax.rem(i, 2)
        rows_wait(slot)
        nxt_blk = jnp.minimum(i + 1, nb_ref[0] - 1)
        xu = xg_scr[slot]
        half = D_MODEL // 2
        x_scr[:, 0:half] = pltpu.bitcast(xu << 16, F32).astype(BF16)
        x_scr[:, half:] = pltpu.bitcast(xu & jnp.uint32(0xFFFF0000), F32).astype(BF16)
        for r in range(MOE_BLK):
            row_copy(nxt_blk, r, 1 - slot).start()
        hh = jnp.dot(x_scr[...], w1b_scr[...], preferred_element_type=F32) + b1_ref[0]
        glu = jnp.minimum(hh[:, :D_FF], SWIGLU_LIMIT)
        lin = jnp.clip(hh[:, D_FF:], -SWIGLU_LIMIT, SWIGLU_LIMIT)
        act = glu * jax.nn.sigmoid(SWIGLU_ALPHA * glu) * (lin + 1.0)
        y_ref[...] = jnp.dot(act.astype(BF16), w2b_scr[...], preferred_element_type=F32) + b2_ref[0]

        @pl.when(i == nb_ref[0] - 1)
        def _():
            rows_wait(1 - slot)

    @pl.when(i >= nb_ref[0])
    def _():
        y_ref[...] = jnp.zeros(y_ref.shape, F32)


def _experts(block_e, n_used, next_e, row_tok, hp, w1, b1, w2, b2):
    rows = row_tok.shape[0]
    n_blocks = rows // MOE_BLK
    return pl.pallas_call(
        _expert_kernel,
        out_shape=jax.ShapeDtypeStruct((rows, D_MODEL), F32),
        grid_spec=pltpu.PrefetchScalarGridSpec(
            num_scalar_prefetch=4,
            grid=(n_blocks,),
            in_specs=[pl.BlockSpec(memory_space=pl.ANY),
                      pl.BlockSpec(memory_space=pl.ANY),
                      pl.BlockSpec((1, 1, 2 * D_FF), lambda i, be, nb, nx, rt: (be[i], 0, 0)),
                      pl.BlockSpec(memory_space=pl.ANY),
                      pl.BlockSpec((1, 1, D_MODEL), lambda i, be, nb, nx, rt: (be[i], 0, 0))],
            out_specs=pl.BlockSpec((MOE_BLK, D_MODEL), lambda i, be, nb, nx, rt: (i, 0)),
            scratch_shapes=[pltpu.VMEM((2, D_MODEL, 2 * D_FF), F32), pltpu.VMEM((2, D_FF, D_MODEL), F32),
                            pltpu.VMEM((D_MODEL, 2 * D_FF), BF16), pltpu.VMEM((D_FF, D_MODEL), BF16),
                            pltpu.VMEM((2, MOE_BLK, D_MODEL // 2), U32), pltpu.VMEM((MOE_BLK, D_MODEL), BF16),
                            pltpu.SMEM((1,), I32), pltpu.SemaphoreType.DMA((2, 2)), pltpu.SemaphoreType.DMA((2,))]),
        compiler_params=_params(("arbitrary",), 56),
        name="expert",
    )(block_e, n_used, next_e, row_tok, hp, w1, b1.reshape(N_EXPERTS, 1, 2 * D_FF), w2,
      b2.reshape(N_EXPERTS, 1, D_MODEL))


def _combine_kernel(dest_ref, yb_ref, gate_ref, xn_ref, gf_ref, o_ref, buf, sem, *, tm, n_tok):
    i = pl.program_id(0)

    def body(g, carry):
        r0 = pl.multiple_of(g * SUBLANES, SUBLANES)
        for j in range(SUBLANES):
            for k in range(TOP_K):
                d = dest_ref[k * n_tok + i * tm + r0 + j]
                pltpu.make_async_copy(yb_ref.at[pl.ds(d, 1), :], buf.at[k, pl.ds(r0 + j, 1), :], sem).start()
        return carry

    lax.fori_loop(0, tm // SUBLANES, body, 0)
    for k in range(TOP_K):
        pltpu.make_async_copy(yb_ref.at[pl.ds(0, tm), :], buf.at[k], sem).wait()
    g = gate_ref[...]
    y = g[:, 0:1] * buf[0]
    for k in range(1, TOP_K):
        y = y + g[:, k:k + 1] * buf[k]
    o_ref[...] = xn_ref[...] + gf_ref[0] * y


def _combine(dest, yb, gate, xn, g_f, seq):
    n = xn.shape[0]
    tm = min(256, seq)
    tpb = seq // tm
    kern = functools.partial(_combine_kernel, tm=tm, n_tok=n)
    return pl.pallas_call(
        kern,
        out_shape=jax.ShapeDtypeStruct((n, D_MODEL), F32),
        grid_spec=pltpu.PrefetchScalarGridSpec(
            num_scalar_prefetch=1,
            grid=(n // tm,),
            in_specs=[pl.BlockSpec(memory_space=pl.ANY),
                      pl.BlockSpec((tm, LANES), lambda i, d: (i, 0)),
                      pl.BlockSpec((tm, D_MODEL), lambda i, d: (i, 0)),
                      pl.BlockSpec((1, 1, D_MODEL), lambda i, d: (i // tpb, 0, 0))],
            out_specs=pl.BlockSpec((tm, D_MODEL), lambda i, d: (i, 0)),
            scratch_shapes=[pltpu.VMEM((TOP_K, tm, D_MODEL), F32), pltpu.SemaphoreType.DMA(())]),
        compiler_params=_params(("arbitrary",), 32),
        name="combine",
    )(dest, yb, gate, xn, g_f)


def _layer(x, ctx, c, c_ctx, norm1_g, norm2_g, w_mod, b_mod, w_in, ret_decay_logit, ret_norm_g,
           diff_q_norm_g, diff_k_norm_g, diff_lambda, diff_norm_g, w_br_ret, w_br_diff, w_out,
           router_w, router_b, exp_w1, exp_b1, exp_w2, exp_b2):
    batch, seq, d = x.shape
    ctx_len = ctx.shape[1]
    assert d == D_MODEL and seq % GRID_W == 0 and batch + 1 <= 8
    n_tok = batch * seq

    cc = jnp.zeros((8, D_MODEL), F32).at[:batch].set(c).at[batch].set(c_ctx)
    mod = _mod(cc, w_mod, b_mod)
    sh_a, sc_a, g_a, sh_f, sc_f, g_f = [mod[:batch, i * D_MODEL:(i + 1) * D_MODEL].reshape(batch, 1, D_MODEL)
                                         for i in range(6)]
    csh_a = mod[batch:batch + 1, 0:D_MODEL].reshape(1, 1, D_MODEL)
    csc_a = mod[batch:batch + 1, D_MODEL:2 * D_MODEL].reshape(1, 1, D_MODEL)

    w_in_bf = w_in.astype(BF16)
    g1 = norm1_g.reshape(1, D_MODEL)
    tile = lambda g: jnp.tile(g.astype(F32), 2)
    qkg = jnp.zeros((8, LANES), F32).at[0].set(tile(diff_q_norm_g) * (DIFF_DH ** -0.5 * LOG2E)).at[1].set(tile(diff_k_norm_g))
    lane = jnp.arange(2 * LANES)
    same_group = lane[:, None] // DIFF_DH == lane[None, :] // DIFF_DH
    gmean = jnp.where(same_group, 1.0 / DIFF_DH, 0.0).astype(BF16)
    gmat = same_group[:LANES, :LANES].astype(BF16)
    x2 = x.reshape(n_tok, D_MODEL)
    tm = min(1024, seq)
    p_lat = _inproj(x2, g1, sh_a, sc_a, w_in_bf, qkg, gmean, _rope_tables(seq), tm, seq // tm)
    p_ctx = _inproj(ctx.reshape(batch * ctx_len, D_MODEL), g1, csh_a, csc_a, w_in_bf, qkg, gmean,
                    _identity_tables(ctx_len), ctx_len, 1)

    lg = jax.nn.log_sigmoid(ret_decay_logit.astype(F32))
    y_ret = _retention(lg, p_lat, p_ctx, ret_norm_g, batch, seq, ctx_len)

    lp = diff_lambda.astype(F32)
    lam = (jnp.exp(jnp.sum(lp[0] * lp[1])) - jnp.exp(jnp.sum(lp[2] * lp[3])) + LAMBDA_INIT).reshape(1)
    y_diff = _diff_attention(lam, p_lat, p_ctx, diff_norm_g, gmat, batch, seq, ctx_len)

    rw = jnp.zeros((D_MODEL, LANES), F32).at[:, :N_EXPERTS].set(router_w)
    rw_hi = rw.astype(BF16)
    rw_lo = (rw - rw_hi.astype(F32)).astype(BF16)
    rb = jnp.full((1, LANES), NEG_BIG, F32).at[0, :N_EXPERTS].set(router_b)
    xn, hp, meta, gate4, cnt = _merge(
        y_ret, y_diff, p_lat, x2, g_a, sh_f, sc_f, norm2_g.reshape(1, D_MODEL),
        w_br_ret.astype(BF16), w_br_diff.astype(BF16), w_out.astype(BF16), rw_hi, rw_lo, rb, seq)

    counts = cnt[0, :N_EXPERTS].astype(I32)
    padded = (counts + MOE_BLK - 1) // MOE_BLK * MOE_BLK
    pad_end = jnp.cumsum(padded)
    pad_start = pad_end - padded
    n_pairs = n_tok * TOP_K
    n_blocks = n_pairs // MOE_BLK + N_EXPERTS
    meta_i = meta.astype(I32)
    is_e = meta_i[None, :TOP_K] == jnp.arange(N_EXPERTS, dtype=I32)[:, None, None]
    dest = (jnp.sum(jnp.where(is_e, pad_start[:, None, None], 0), axis=0) + meta_i[TOP_K:]).reshape(n_pairs)
    block_start = jnp.arange(n_blocks, dtype=I32) * MOE_BLK
    block_e = jnp.minimum(jnp.sum((pad_end[None, :] <= block_start[:, None]).astype(I32), axis=1), N_EXPERTS - 1)
    n_used = (pad_end[-1] // MOE_BLK).reshape(1).astype(I32)

    row_tok = jnp.zeros((n_blocks * MOE_BLK,), I32).at[dest].set(
        jnp.arange(n_pairs, dtype=I32) % n_tok, unique_indices=True)
    e_ids = jnp.arange(N_EXPERTS, dtype=I32)
    later = jnp.where((counts[None, :] > 0) & (e_ids[None, :] > e_ids[:, None]), e_ids[None, :], N_EXPERTS)
    next_e = jnp.min(later, axis=1)
    next_e = jnp.where(next_e == N_EXPERTS, -1, next_e).astype(I32)
    yb = _experts(block_e, n_used, next_e, row_tok, hp, exp_w1, exp_b1, exp_w2, exp_b2)
    out = _combine(dest, yb, gate4, xn, g_f, seq)
    return out.reshape(batch, seq, D_MODEL)


def kernel(x, c, ctx, c_ctx, norm1_g, norm2_g, w_mod, b_mod, w_in, ret_decay_logit, ret_norm_g, diff_q_norm_g, diff_k_norm_g, diff_lambda, diff_norm_g, w_br_ret, w_br_diff, w_out, router_w, router_b, exp_w1, exp_b1, exp_w2, exp_b2):
    assert norm1_g.shape[0] == 1, "single-layer block"
    return _layer(x, ctx, c, c_ctx, norm1_g[0], norm2_g[0], w_mod[0], b_mod[0], w_in[0], ret_decay_logit[0],
                  ret_norm_g[0], diff_q_norm_g[0], diff_k_norm_g[0], diff_lambda[0], diff_norm_g[0],
                  w_br_ret[0], w_br_diff[0], w_out[0], router_w[0], router_b[0],
                  exp_w1[0], exp_b1[0], exp_w2[0], exp_b2[0])
```

```python
import functools
import math

import numpy as np

import jax
import jax.numpy as jnp
from jax import lax
from jax.experimental import pallas as pl
from jax.experimental.pallas import tpu as pltpu

F32 = jnp.float32
BF16 = jnp.bfloat16
U32 = jnp.uint32
I32 = jnp.int32

D_MODEL = 1024
GRID_W = 64
RET_HEADS = 4
RET_DK = 256
RET_DV = 512
DIFF_DH = 64
DIFF_HEADS = 8
DIFF_DV = 128
N_EXPERTS = 32
TOP_K = 4
D_FF = 1024
SWIGLU_LIMIT = 7.0
SWIGLU_ALPHA = 1.702
ROPE_BASE = 10000.0
EPS = 1e-6
LAMBDA_INIT = 0.8 - 0.6 * math.exp(-0.3 * 0)

IN_COLS = 11264
COL_TILE = 1024
LANES = 128
SUBLANES = 8
MOE_BLK = 256
NEG_BIG = -1e30
LOG2E = 1.4426950408889634
SHIFT_SLACK = 1.0 + 2.0 ** -6
MAX_SAFE_SHIFT = 60.0
RET_SUB = 8
KV_UNROLL = 4
HIGHEST = lax.Precision.HIGHEST
MIB = 1024 * 1024


def _params(sem, vmem_mib):
    return pltpu.CompilerParams(dimension_semantics=sem, vmem_limit_bytes=vmem_mib * MIB)


def _mod_kernel(c_ref, w_ref, b_ref, o_ref):
    c = c_ref[...]
    s = c * jax.nn.sigmoid(c)
    o_ref[...] = jnp.dot(s, w_ref[...], preferred_element_type=F32, precision=HIGHEST) + b_ref[...]


def _mod(cc, w_mod, b_mod):
    n = w_mod.shape[1]
    tn = 1024
    return pl.pallas_call(
        _mod_kernel,
        out_shape=jax.ShapeDtypeStruct((8, n), F32),
        grid=(n // tn,),
        in_specs=[pl.BlockSpec((8, D_MODEL), lambda j: (0, 0)),
                  pl.BlockSpec((D_MODEL, tn), lambda j: (0, j)),
                  pl.BlockSpec((1, tn), lambda j: (0, j))],
        out_specs=pl.BlockSpec((8, tn), lambda j: (0, j)),
        compiler_params=_params(("arbitrary",), 32),
        name="mod",
    )(cc, w_mod, b_mod.reshape(1, n))


def _inproj_kernel(x_ref, g_ref, sh_ref, sc_ref, w_ref, qkg_ref, gmat_ref,
                   cr_ref, sr_ref, cc_ref, sc2_ref, cd_ref, sa_ref, sb_ref,
                   o_ref, h_scr, acc_scr):
    j = pl.program_id(1)

    @pl.when(j == 0)
    def _():
        xf = x_ref[...]
        ms = jnp.mean(xf * xf, axis=-1, keepdims=True)
        y = xf * lax.rsqrt(ms + EPS) * g_ref[...]
        h_scr[...] = (y * (1.0 + sc_ref[0]) + sh_ref[0]).astype(BF16)

    is_ret = j <= 1
    is_dqk = (j == 6) | (j == 7)

    def project():
        return jnp.dot(h_scr[...], w_ref[...], preferred_element_type=F32)

    @pl.when(is_ret)
    def _():
        acc_scr[...] = project()
        scale = jnp.where(j == 0, RET_DK ** -0.5, 1.0).astype(F32)
        for b in range(COL_TILE // LANES):
            xb = acc_scr[:, b * LANES:(b + 1) * LANES]
            cos = cr_ref[...] if b % 2 == 0 else cc_ref[...]
            sin = sr_ref[...] if b % 2 == 0 else sc2_ref[...]
            o = (xb * cos + pltpu.roll(xb, 64, 1) * sin) * scale
            o_ref[:, b * LANES:(b + 1) * LANES] = o.astype(BF16)

    @pl.when(is_dqk)
    def _():
        acc_scr[...] = project()
        g = jnp.where(j == 6, qkg_ref[0:1, :], qkg_ref[1:2, :])
        for b2 in range(COL_TILE // (2 * LANES)):
            x2 = acc_scr[:, b2 * 2 * LANES:(b2 + 1) * 2 * LANES]
            ms2 = jnp.dot((x2 * x2).astype(BF16), gmat_ref[...], preferred_element_type=F32)
            for half in range(2):
                b = 2 * b2 + half
                xb = x2[:, half * LANES:(half + 1) * LANES]
                yn = xb * lax.rsqrt(ms2[:, half * LANES:(half + 1) * LANES] + EPS) * g
                o = yn * cd_ref[...] + pltpu.roll(yn, 16, 1) * sa_ref[...] + pltpu.roll(yn, 112, 1) * sb_ref[...]
                o_ref[:, b * LANES:(b + 1) * LANES] = o.astype(BF16)

    @pl.when(jnp.logical_not(is_ret | is_dqk))
    def _():
        o_ref[...] = project().astype(BF16)


def _inproj(x2, g1, sh, sc, w_bf, qkg, gmat, tables, tm, tiles_per_batch):
    n = x2.shape[0]
    nb = sh.shape[0]
    tab_spec = pl.BlockSpec((tm, LANES), lambda i, j: (i % tiles_per_batch, 0))
    mod_spec = pl.BlockSpec((1, 1, D_MODEL), lambda i, j: (jnp.minimum(i // tiles_per_batch, nb - 1), 0, 0))
    return pl.pallas_call(
        _inproj_kernel,
        out_shape=jax.ShapeDtypeStruct((n, IN_COLS), BF16),
        grid=(n // tm, IN_COLS // COL_TILE),
        in_specs=[pl.BlockSpec((tm, D_MODEL), lambda i, j: (i, 0)),
                  pl.BlockSpec((1, D_MODEL), lambda i, j: (0, 0)),
                  mod_spec, mod_spec,
                  pl.BlockSpec((D_MODEL, COL_TILE), lambda i, j: (0, j)),
                  pl.BlockSpec((8, LANES), lambda i, j: (0, 0)),
                  pl.BlockSpec((2 * LANES, 2 * LANES), lambda i, j: (0, 0))] + [tab_spec] * 7,
        out_specs=pl.BlockSpec((tm, COL_TILE), lambda i, j: (i, j)),
        scratch_shapes=[pltpu.VMEM((tm, D_MODEL), BF16), pltpu.VMEM((tm, COL_TILE), F32)],
        compiler_params=_params(("arbitrary", "arbitrary"), 48),
        name="inproj",
    )(x2, g1, sh, sc, w_bf, qkg, gmat, *tables)


def _rope_tables(seq):
    n_rows = seq // GRID_W
    f32 = np.float32

    def angles(pos, half):
        inv = f32(ROPE_BASE) ** (-np.arange(half, dtype=f32) / f32(half))
        return (pos.astype(f32)[:, None] * inv[None, :]).astype(np.float64)

    ar, ac = angles(np.arange(n_rows), 64), angles(np.arange(GRID_W), 64)
    br, bc = angles(np.arange(n_rows), 16), angles(np.arange(GRID_W), 16)
    zr, zc = np.zeros_like(br), np.zeros_like(bc)
    cat = lambda parts, reps=1: np.tile(np.concatenate(parts, axis=1), (1, reps)).astype(f32)
    by_row = lambda t: jnp.repeat(jnp.asarray(t), GRID_W, axis=0)
    by_col = lambda t: jnp.tile(jnp.asarray(t), (n_rows, 1))
    cr = by_row(cat([np.cos(ar), np.cos(ar)]))
    sr = by_row(cat([-np.sin(ar), np.sin(ar)]))
    cc = by_col(cat([np.cos(ac), np.cos(ac)]))
    sc = by_col(cat([-np.sin(ac), np.sin(ac)]))
    cd = by_row(cat([np.cos(br), np.cos(br), zr, zr], 2)) + by_col(cat([zc, zc, np.cos(bc), np.cos(bc)], 2))
    sa = by_row(cat([zr, np.sin(br), zr, zr], 2)) + by_col(cat([zc, zc, zc, np.sin(bc)], 2))
    sb = by_row(cat([-np.sin(br), zr, zr, zr], 2)) + by_col(cat([zc, zc, -np.sin(bc), zc], 2))
    return [cr, sr, cc, sc, cd, sa, sb]


def _identity_tables(seq):
    one = jnp.ones((seq, LANES), F32)
    zero = jnp.zeros((seq, LANES), F32)
    return [one, zero, one, zero, one, zero, zero]


def _tn_dot(a, b):
    return lax.dot_general(a, b, (((0,), (0,)), ((), ())), preferred_element_type=F32)


def _nt_dot(a, b):
    return lax.dot_general(a, b, (((1,), (1,)), ((), ())), preferred_element_type=F32)


def _ret_kernel(lg_ref, q_ref, k_ref, v_ref, g_ref, ck_ref, cv_ref, gn_ref, o_ref,
                sf_scr, sb_scr, ob_scr, *, chunk, sub, n_blocks, ctx_len):
    h = pl.program_id(1)
    p = pl.program_id(2)
    c = pl.program_id(3)
    lgf = lg_ref[0, h]
    lgb = lg_ref[1, h]

    def col_iota(n):
        return lax.broadcasted_iota(I32, (n, 1), 0).astype(F32)

    def vexp(s):
        return jnp.exp(jnp.zeros((1, 1), F32) + s)

    @pl.when((p == 0) & (c == 0))
    def _():
        jc = col_iota(ctx_len)
        kc = ck_ref[...].astype(F32)
        vc = cv_ref[...]
        sf_scr[...] = _tn_dot((kc * jnp.exp(lgf * (ctx_len - 1.0 - jc))).astype(BF16), vc)
        sb_scr[...] = _tn_dot((kc * jnp.exp(lgb * jc)).astype(BF16), vc)

    ic = col_iota(chunk)

    @pl.when(p == 0)
    def _():
        q_decay = jnp.exp(lgb * (chunk - ic))
        k_decay = jnp.exp(lgb * ic)
        s_decay = vexp(lgb * chunk)
        for j in reversed(range(sub)):
            loc = pl.ds(j * chunk, chunk)
            glob = pl.ds(pl.multiple_of(((n_blocks - 1 - c) * sub + j) * chunk, chunk), chunk)
            qb = (q_ref[loc, :].astype(F32) * q_decay).astype(BF16)
            ob_scr[glob, :] = jnp.dot(qb, sb_scr[...].astype(BF16), preferred_element_type=F32)
            kb = (k_ref[loc, :].astype(F32) * k_decay).astype(BF16)
            sb_scr[...] = s_decay * sb_scr[...] + _tn_dot(kb, v_ref[loc, :])

    @pl.when(p == 1)
    def _():
        ri = lax.broadcasted_iota(I32, (chunk, chunk), 0)
        ci = lax.broadcasted_iota(I32, (chunk, chunk), 1)
        d = (ri - ci).astype(F32)
        mask = jnp.where(d > 0, jnp.exp(lgf * jnp.maximum(d, 0.0)),
                         jnp.where(d < 0, jnp.exp(lgb * jnp.maximum(-d, 0.0)), 2.0))
        q_decay = jnp.exp(lgf * (ic + 1.0))
        k_decay = jnp.exp(lgf * (chunk - 1.0 - ic))
        s_decay = vexp(lgf * chunk)
        for j in range(sub):
            loc = pl.ds(j * chunk, chunk)
            glob = pl.ds(pl.multiple_of((c * sub + j) * chunk, chunk), chunk)
            a = (_nt_dot(q_ref[loc, :], k_ref[loc, :]) * mask).astype(BF16)
            qf = (q_ref[loc, :].astype(F32) * q_decay).astype(BF16)
            o = (jnp.dot(a, v_ref[loc, :], preferred_element_type=F32)
                 + jnp.dot(qf, sf_scr[...].astype(BF16), preferred_element_type=F32)
                 + ob_scr[glob, :])
            kf = (k_ref[loc, :].astype(F32) * k_decay).astype(BF16)
            sf_scr[...] = s_decay * sf_scr[...] + _tn_dot(kf, v_ref[loc, :])
            ms = jnp.mean(o * o, axis=-1, keepdims=True)
            y = o * lax.rsqrt(ms + EPS) * gn_ref[0]
            gt = g_ref[loc, :].astype(F32)
            o_ref[loc, :] = (y * (gt * jax.nn.sigmoid(gt))).astype(BF16)


def _retention(lg, p_lat, p_ctx, ret_norm_g, batch, seq, ctx_len):
    chunk = min(256, seq)
    sub = math.gcd(seq // chunk, RET_SUB)
    blk = chunk * sub
    nb = seq // blk
    kern = functools.partial(_ret_kernel, chunk=chunk, sub=sub, n_blocks=nb, ctx_len=ctx_len)

    def rows(b, p, c):
        return b * nb + jnp.where(p == 0, nb - 1 - c, c)

    def rows_fwd(b, p, c):
        return b * nb + jnp.where(p == 0, 0, c)

    return pl.pallas_call(
        kern,
        out_shape=jax.ShapeDtypeStruct((batch * seq, RET_HEADS * RET_DV), BF16),
        grid=(batch, RET_HEADS, 2, nb),
        in_specs=[pl.BlockSpec(memory_space=pltpu.SMEM),
                  pl.BlockSpec((blk, RET_DK), lambda b, h, p, c: (rows(b, p, c), h)),
                  pl.BlockSpec((blk, RET_DK), lambda b, h, p, c: (rows(b, p, c), 4 + h)),
                  pl.BlockSpec((blk, RET_DV), lambda b, h, p, c: (rows(b, p, c), 4 + h)),
                  pl.BlockSpec((blk, RET_DV), lambda b, h, p, c: (rows_fwd(b, p, c), 8 + h)),
                  pl.BlockSpec((ctx_len, RET_DK), lambda b, h, p, c: (b, 4 + h)),
                  pl.BlockSpec((ctx_len, RET_DV), lambda b, h, p, c: (b, 4 + h)),
                  pl.BlockSpec((1, 1, RET_DV), lambda b, h, p, c: (h, 0, 0))],
        out_specs=pl.BlockSpec((blk, RET_DV), lambda b, h, p, c: (rows_fwd(b, p, c), h)),
        scratch_shapes=[pltpu.VMEM((RET_DK, RET_DV), F32), pltpu.VMEM((RET_DK, RET_DV), F32),
                        pltpu.VMEM((seq, RET_DV), F32)],
        compiler_params=_params(("arbitrary",) * 4, 48),
        name="ret",
    )(lg, p_lat, p_lat, p_lat, p_lat, p_ctx, p_ctx, ret_norm_g.reshape(RET_HEADS, 1, RET_DV))


def _dattn_kernel(lam_ref, q_ref, ck_ref, cv_ref, k_ref, v_ref, gn_ref, gmat_ref, o_ref,
                  qq_scr, kmax_scr, mp_scr, kp_scr, vt_scr, pt_scr, *, tq, tk, n_kv, ta, unroll, ctx_len):
    rows = 2 * tq
    n_all = kp_scr.shape[0]
    qi = pl.program_id(2)

    def sq_norms(t):
        tf = t.astype(F32)
        return jnp.dot((tf * tf).astype(BF16), gmat_ref[...], preferred_element_type=F32)

    def lane_const(n, hot):
        return jnp.where(lax.broadcasted_iota(I32, (n, LANES), 1) < hot, 1.0, 0.0).astype(BF16)

    @pl.when(qi == 0)
    def _():
        kmax_scr[...] = jnp.max(sq_norms(ck_ref[...]), axis=0, keepdims=True)
        kp_scr[:, LANES:2 * LANES] = lane_const(n_all, 3)
        kp_scr[0:ctx_len, 0:LANES] = ck_ref[...]
        vt_scr[:, 0:ctx_len] = cv_ref[...].astype(F32).T.astype(BF16)

        def body(c, carry):
            start = pl.multiple_of(c * tk, tk)
            k = k_ref[pl.ds(start, tk), :]
            kmax_scr[...] = jnp.maximum(kmax_scr[...], jnp.max(sq_norms(k), axis=0, keepdims=True))
            kp_scr[pl.ds(pl.multiple_of(ctx_len + start, LANES), tk), 0:LANES] = k
            vt_scr[:, pl.ds(pl.multiple_of(ctx_len + start, LANES), tk)] = (
                v_ref[pl.ds(start, tk), :].astype(F32).T.astype(BF16))
            return carry

        lax.fori_loop(0, n_kv, body, 0)

    qt = q_ref[...].astype(F32).T
    row = lax.broadcasted_iota(I32, (LANES, tq), 0)
    q1t = jnp.where(row < DIFF_DH, qt, 0.0)
    q2t = jnp.where(row >= DIFF_DH, qt, 0.0)
    qq_scr[0:LANES, 0:tq] = q1t.astype(BF16)
    qq_scr[0:LANES, tq:rows] = q2t.astype(BF16)

    def set_shift(shift):
        neg = -shift
        hi = neg.astype(BF16).astype(F32)
        mid = (neg - hi).astype(BF16).astype(F32)
        lo = neg - hi - mid
        row_r = lax.broadcasted_iota(I32, (LANES, rows), 0)
        pieces = jnp.where(row_r == 0, hi, jnp.where(row_r == 1, mid, jnp.where(row_r == 2, lo, 0.0)))
        qq_scr[LANES:2 * LANES, :] = pieces.astype(BF16)

    kmax = kmax_scr[...]
    b1 = jnp.sqrt(jnp.sum(q1t * q1t, axis=0, keepdims=True) * kmax[:, 0:1]) * SHIFT_SLACK
    b2 = jnp.sqrt(jnp.sum(q2t * q2t, axis=0, keepdims=True) * kmax[:, DIFF_DH:DIFF_DH + 1]) * SHIFT_SLACK
    bound = jnp.concatenate([b1, b2], axis=1)
    set_shift(bound)

    def key_tile(t):
        return pl.ds(pl.multiple_of(t * ta, ta), ta)

    @pl.when(jnp.max(bound) > MAX_SAFE_SHIFT)
    def _():
        mp_scr[...] = jnp.full(mp_scr.shape, NEG_BIG, F32)

        def body(t, carry):
            st = jnp.dot(kp_scr[key_tile(t), 0:LANES], qq_scr[0:LANES, :], preferred_element_type=F32)
            mp_scr[...] = jnp.maximum(mp_scr[...], jnp.max(st, axis=0, keepdims=True))
            return carry

        lax.fori_loop(0, n_all // ta, body, 0)
        set_shift(mp_scr[0:1, :])

    mp_scr[...] = jnp.zeros(mp_scr.shape, F32)

    def body_a(t, carry):
        for u in range(unroll):
            r = key_tile(t * unroll + u)
            p = jnp.exp2(jnp.dot(kp_scr[r, :], qq_scr[...], preferred_element_type=F32))
            pt_scr[r, :] = p.astype(BF16)
            mp_scr[...] += jnp.sum(p.reshape(ta // SUBLANES, SUBLANES, rows), axis=0)
        return carry

    lax.fori_loop(0, n_all // (ta * unroll), body_a, 0)

    acc = jnp.dot(vt_scr[...], pt_scr[...], preferred_element_type=F32)
    ot = acc / jnp.sum(mp_scr[...], axis=0, keepdims=True)
    d = (ot[:, 0:tq] - lam_ref[0] * ot[:, tq:rows]).T
    ms = jnp.mean(d * d, axis=-1, keepdims=True)
    y = d * lax.rsqrt(ms + EPS) * gn_ref[...] * (1.0 - LAMBDA_INIT)
    o_ref[...] = y.astype(BF16)


def _diff_attention(lam, p_lat, p_ctx, diff_norm_g, gmat, batch, seq, ctx_len):
    tq = min(512, seq)
    tk = min(512, seq)
    nq, nk = seq // tq, seq // tk
    n_all = ctx_len + seq
    ta = next(t for t in (528, 512, 384, 320, 256, 128) if n_all % t == 0)
    unroll = math.gcd(n_all // ta, KV_UNROLL)
    assert ctx_len % LANES == 0
    kern = functools.partial(_dattn_kernel, tq=tq, tk=tk, n_kv=nk, ta=ta, unroll=unroll, ctx_len=ctx_len)
    return pl.pallas_call(
        kern,
        out_shape=jax.ShapeDtypeStruct((batch * seq, DIFF_HEADS * DIFF_DV), BF16),
        grid=(batch, DIFF_HEADS, nq),
        in_specs=[pl.BlockSpec(memory_space=pltpu.SMEM),
                  pl.BlockSpec((tq, LANES), lambda b, h, qi: (b * nq + qi, 48 + h)),
                  pl.BlockSpec((ctx_len, LANES), lambda b, h, qi: (b, 56 + h)),
                  pl.BlockSpec((ctx_len, LANES), lambda b, h, qi: (b, 64 + h)),
                  pl.BlockSpec((seq, LANES), lambda b, h, qi: (b, 56 + h)),
                  pl.BlockSpec((seq, LANES), lambda b, h, qi: (b, 64 + h)),
                  pl.BlockSpec((1, LANES), lambda b, h, qi: (0, 0)),
                  pl.BlockSpec((LANES, LANES), lambda b, h, qi: (0, 0))],
        out_specs=pl.BlockSpec((tq, LANES), lambda b, h, qi: (b * nq + qi, h)),
        scratch_shapes=[pltpu.VMEM((2 * LANES, 2 * tq), BF16), pltpu.VMEM((1, LANES), F32),
                        pltpu.VMEM((SUBLANES, 2 * tq), F32), pltpu.VMEM((n_all, 2 * LANES), BF16),
                        pltpu.VMEM((DIFF_DV, n_all), BF16), pltpu.VMEM((n_all, 2 * tq), BF16)],
        compiler_params=_params(("arbitrary",) * 3, 48),
        name="dattn",
    )(lam, p_lat, p_ctx, p_ctx, p_lat, p_lat, diff_norm_g.reshape(1, DIFF_DV), gmat)


def _merge_kernel(yr_ref, yd_ref, ga_ref, gb_ref, x_ref, gatea_ref, shf_ref, scf_ref, n2_ref,
                  wr_ref, wd_ref, wo_ref, rwh_ref, rwl_ref, rb_ref,
                  xn_ref, hp_ref, meta_ref, gate_ref, cnt_ref, run_scr, *, tm):
    i = pl.program_id(0)

    @pl.when(i == 0)
    def _():
        run_scr[...] = jnp.zeros(run_scr.shape, F32)

    yr = jnp.dot(yr_ref[...], wr_ref[...], preferred_element_type=F32)
    yd = jnp.dot(yd_ref[...], wd_ref[...], preferred_element_type=F32)
    m = (jax.nn.sigmoid(ga_ref[...].astype(F32)) * yr + jax.nn.sigmoid(gb_ref[...].astype(F32)) * yd)
    z = jnp.dot(m.astype(BF16), wo_ref[...], preferred_element_type=F32)
    xn = x_ref[...] + gatea_ref[0] * z
    xn_ref[...] = xn

    ms = jnp.mean(xn * xn, axis=-1, keepdims=True)
    h2 = xn * lax.rsqrt(ms + EPS) * n2_ref[...]
    h2 = h2 * (1.0 + scf_ref[0]) + shf_ref[0]
    h_hi = h2.astype(BF16)
    bits = pltpu.bitcast(h_hi.astype(F32), U32)
    half = D_MODEL // 2
    hp_ref[...] = (bits[:, :half] >> 16) | (bits[:, half:] & jnp.uint32(0xFFFF0000))

    h_lo = (h2 - h_hi.astype(F32)).astype(BF16)
    logits = (jnp.dot(h_hi, rwh_ref[...], preferred_element_type=F32)
              + jnp.dot(h_lo, rwh_ref[...], preferred_element_type=F32)
              + jnp.dot(h_hi, rwl_ref[...], preferred_element_type=F32) + rb_ref[...])
    lane = lax.broadcasted_iota(I32, (tm, LANES), 1)
    lanef = lane.astype(F32)
    work = logits
    vals, idxs = [], []
    for _ in range(TOP_K):
        mk = jnp.max(work, axis=1, keepdims=True)
        ik = jnp.min(jnp.where(work == mk, lanef, float(LANES)), axis=1, keepdims=True)
        vals.append(mk)
        idxs.append(ik)
        work = jnp.where(lanef == ik, -jnp.inf, work)
    ex = [jnp.exp(v - vals[0]) for v in vals]
    den = ex[0] + ex[1] + ex[2] + ex[3]

    onehot = jnp.zeros((tm, LANES), F32)
    for ik in idxs:
        onehot = onehot + jnp.where(lanef == ik, 1.0, 0.0)
    ri = lax.broadcasted_iota(I32, (tm, tm), 0)
    ci = lax.broadcasted_iota(I32, (tm, tm), 1)
    tri = jnp.where(ri > ci, 1.0, 0.0).astype(BF16)
    base = run_scr[0:1, :] + jnp.dot(tri, onehot.astype(BF16), preferred_element_type=F32)
    run_scr[...] = run_scr[...] + jnp.sum(onehot, axis=0, keepdims=True)

    meta = jnp.zeros((tm, LANES), F32)
    gate_out = jnp.zeros((tm, LANES), F32)
    for k in range(TOP_K):
        rk = jnp.sum(jnp.where(lanef == idxs[k], base, 0.0), axis=1, keepdims=True)
        meta = jnp.where(lane == k, idxs[k], jnp.where(lane == TOP_K + k, rk, meta))
        gate_out = jnp.where(lane == k, ex[k] / den, gate_out)
    meta_ref[...] = meta.T[0:2 * TOP_K, :]
    gate_ref[...] = gate_out
    cnt_ref[...] = run_scr[...]


def _merge(y_ret, y_diff, p_lat, x2, g_a, sh_f, sc_f, norm2_g, w_r, w_d, w_o, rw_hi, rw_lo, rb, seq):
    n = x2.shape[0]
    tm = min(512, seq)
    tpb = seq // tm
    kern = functools.partial(_merge_kernel, tm=tm)
    mod_spec = pl.BlockSpec((1, 1, D_MODEL), lambda i: (i // tpb, 0, 0))
    const = lambda shape: pl.BlockSpec(shape, lambda i: (0,) * len(shape))
    tok = lambda w: pl.BlockSpec((tm, w), lambda i: (i, 0))
    return pl.pallas_call(
        kern,
        out_shape=(jax.ShapeDtypeStruct((n, D_MODEL), F32),
                   jax.ShapeDtypeStruct((n, D_MODEL // 2), U32),
                   jax.ShapeDtypeStruct((2 * TOP_K, n), F32),
                   jax.ShapeDtypeStruct((n, LANES), F32),
                   jax.ShapeDtypeStruct((8, LANES), F32)),
        grid=(n // tm,),
        in_specs=[tok(RET_HEADS * RET_DV), tok(D_MODEL),
                  pl.BlockSpec((tm, COL_TILE), lambda i: (i, 9)),
                  pl.BlockSpec((tm, COL_TILE), lambda i: (i, 10)),
                  tok(D_MODEL), mod_spec, mod_spec, mod_spec, const((1, D_MODEL)),
                  const((RET_HEADS * RET_DV, D_MODEL)), const((D_MODEL, D_MODEL)), const((D_MODEL, D_MODEL)),
                  const((D_MODEL, LANES)), const((D_MODEL, LANES)), const((1, LANES))],
        out_specs=(tok(D_MODEL), tok(D_MODEL // 2), pl.BlockSpec((2 * TOP_K, tm), lambda i: (0, i)),
                   tok(LANES), const((8, LANES))),
        scratch_shapes=[pltpu.VMEM((8, LANES), F32)],
        compiler_params=_params(("arbitrary",), 56),
        name="merge",
    )(y_ret, y_diff, p_lat, p_lat, x2, g_a, sh_f, sc_f, norm2_g, w_r, w_d, w_o, rw_hi, rw_lo, rb)


def _dispatch_kernel(dest_ref, hp_ref, xb_in_ref, xb_ref, sem, *, tm, n_tok):
    del xb_in_ref
    i = pl.program_id(0)

    def body(g, carry):
        r0 = pl.multiple_of(g * SUBLANES, SUBLANES)
        for j in range(SUBLANES):
            for k in range(TOP_K):
                d = dest_ref[k * n_tok + i * tm + r0 + j]
                pltpu.make_async_copy(hp_ref.at[pl.ds(r0 + j, 1), :], xb_ref.at[pl.ds(d, 1), :], sem).start()
        return carry

    lax.fori_loop(0, tm // SUBLANES, body, 0)
    for _ in range(TOP_K):
        pltpu.make_async_copy(hp_ref, xb_ref.at[pl.ds(0, tm), :], sem).wait()


def _dispatch(dest, hp, xb_zero):
    n = hp.shape[0]
    tm = min(512, n)
    kern = functools.partial(_dispatch_kernel, tm=tm, n_tok=n)
    return pl.pallas_call(
        kern,
        out_shape=jax.ShapeDtypeStruct(xb_zero.shape, U32),
        grid_spec=pltpu.PrefetchScalarGridSpec(
            num_scalar_prefetch=1,
            grid=(n // tm,),
            in_specs=[pl.BlockSpec((tm, D_MODEL // 2), lambda i, d: (i, 0)),
                      pl.BlockSpec(memory_space=pl.ANY)],
            out_specs=pl.BlockSpec(memory_space=pl.ANY),
            scratch_shapes=[pltpu.SemaphoreType.DMA(())]),
        input_output_aliases={2: 0},
        compiler_params=_params(("arbitrary",), 32),
        name="dispatch",
    )(dest, hp, xb_zero)


def _expert_kernel(be_ref, nb_ref, nxt_ref, xb_ref, w1_hbm, b1_ref, w2_hbm, b2_ref, y_ref,
                   w1f_scr, w2f_scr, w1b_scr, w2b_scr, x_scr, grp_scr, sems):
    i = pl.program_id(0)
    e = be_ref[i]
    prev = be_ref[jnp.maximum(i - 1, 0)]
    active = i < nb_ref[0]

    def weight_copies(expert, slot):
        return (pltpu.make_async_copy(w1_hbm.at[expert], w1f_scr.at[slot], sems.at[0, slot]),
                pltpu.make_async_copy(w2_hbm.at[expert], w2f_scr.at[slot], sems.at[1, slot]))

    @pl.when(i == 0)
    def _():
        grp_scr[0] = 0
        for cp in weight_copies(e, 0):
            cp.start()

    @pl.when(active & (i > 0) & (e != prev))
    def _():
        grp_scr[0] = grp_scr[0] + 1

    @pl.when(active & ((i == 0) | (e != prev)))
    def _():
        slot = grp_scr[0] % 2
        for cp in weight_copies(e, slot):
            cp.wait()
        w1b_scr[...] = w1f_scr[slot].astype(BF16)
        w2b_scr[...] = w2f_scr[slot].astype(BF16)
        nxt = nxt_ref[e]

        @pl.when(nxt >= 0)
        def _():
            for cp in weight_copies(nxt, 1 - slot):
                cp.start()

    @pl.when(active)
    def _():
        xu = xb_ref[...]
        half = D_MODEL // 2
        x_scr[:, 0:half] = pltpu.bitcast(xu << 16, F32).astype(BF16)
        x_scr[:, half:] = pltpu.bitcast(xu & jnp.uint32(0xFFFF0000), F32).astype(BF16)
        hh = jnp.dot(x_scr[...], w1b_scr[...], preferred_element_type=F32) + b1_ref[0]
        glu = jnp.minimum(hh[:, :D_FF], SWIGLU_LIMIT)
        lin = jnp.clip(hh[:, D_FF:], -SWIGLU_LIMIT, SWIGLU_LIMIT)
        act = glu * jax.nn.sigmoid(SWIGLU_ALPHA * glu) * (lin + 1.0)
        y_ref[...] = jnp.dot(act.astype(BF16), w2b_scr[...], preferred_element_type=F32) + b2_ref[0]

    @pl.when(i >= nb_ref[0])
    def _():
        y_ref[...] = jnp.zeros(y_ref.shape, F32)


def _experts(block_e, n_used, next_e, xb, w1, b1, w2, b2):
    rows = xb.shape[0]
    n_blocks = rows // MOE_BLK
    return pl.pallas_call(
        _expert_kernel,
        out_shape=jax.ShapeDtypeStruct((rows, D_MODEL), F32),
        grid_spec=pltpu.PrefetchScalarGridSpec(
            num_scalar_prefetch=3,
            grid=(n_blocks,),
            in_specs=[pl.BlockSpec((MOE_BLK, D_MODEL // 2), lambda i, be, nb, nx: (i, 0)),
                      pl.BlockSpec(memory_space=pl.ANY),
                      pl.BlockSpec((1, 1, 2 * D_FF), lambda i, be, nb, nx: (be[i], 0, 0)),
                      pl.BlockSpec(memory_space=pl.ANY),
                      pl.BlockSpec((1, 1, D_MODEL), lambda i, be, nb, nx: (be[i], 0, 0))],
            out_specs=pl.BlockSpec((MOE_BLK, D_MODEL), lambda i, be, nb, nx: (i, 0)),
            scratch_shapes=[pltpu.VMEM((2, D_MODEL, 2 * D_FF), F32), pltpu.VMEM((2, D_FF, D_MODEL), F32),
                            pltpu.VMEM((D_MODEL, 2 * D_FF), BF16), pltpu.VMEM((D_FF, D_MODEL), BF16),
                            pltpu.VMEM((MOE_BLK, D_MODEL), BF16),
                            pltpu.SMEM((1,), I32), pltpu.SemaphoreType.DMA((2, 2))]),
        compiler_params=_params(("arbitrary",), 56),
        name="expert",
    )(block_e, n_used, next_e, xb, w1, b1.reshape(N_EXPERTS, 1, 2 * D_FF), w2, b2.reshape(N_EXPERTS, 1, D_MODEL))


def _combine_kernel(dest_ref, yb_ref, gate_ref, xn_ref, gf_ref, o_ref, buf, sem, *, tm, n_tok):
    i = pl.program_id(0)

    def body(g, carry):
        r0 = pl.multiple_of(g * SUBLANES, SUBLANES)
        for j in range(SUBLANES):
            for k in range(TOP_K):
                d = dest_ref[k * n_tok + i * tm + r0 + j]
                pltpu.make_async_copy(yb_ref.at[pl.ds(d, 1), :], buf.at[k, pl.ds(r0 + j, 1), :], sem).start()
        return carry

    lax.fori_loop(0, tm // SUBLANES, body, 0)
    for k in range(TOP_K):
        pltpu.make_async_copy(yb_ref.at[pl.ds(0, tm), :], buf.at[k], sem).wait()
    g = gate_ref[...]
    y = g[:, 0:1] * buf[0]
    for k in range(1, TOP_K):
        y = y + g[:, k:k + 1] * buf[k]
    o_ref[...] = xn_ref[...] + gf_ref[0] * y


def _combine(dest, yb, gate, xn, g_f, seq):
    n = xn.shape[0]
    tm = min(256, seq)
    tpb = seq // tm
    kern = functools.partial(_combine_kernel, tm=tm, n_tok=n)
    return pl.pallas_call(
        kern,
        out_shape=jax.ShapeDtypeStruct((n, D_MODEL), F32),
        grid_spec=pltpu.PrefetchScalarGridSpec(
            num_scalar_prefetch=1,
            grid=(n // tm,),
            in_specs=[pl.BlockSpec(memory_space=pl.ANY),
                      pl.BlockSpec((tm, LANES), lambda i, d: (i, 0)),
                      pl.BlockSpec((tm, D_MODEL), lambda i, d: (i, 0)),
                      pl.BlockSpec((1, 1, D_MODEL), lambda i, d: (i // tpb, 0, 0))],
            out_specs=pl.BlockSpec((tm, D_MODEL), lambda i, d: (i, 0)),
            scratch_shapes=[pltpu.VMEM((TOP_K, tm, D_MODEL), F32), pltpu.SemaphoreType.DMA(())]),
        compiler_params=_params(("arbitrary",), 32),
        name="combine",
    )(dest, yb, gate, xn, g_f)


def _layer(x, ctx, c, c_ctx, norm1_g, norm2_g, w_mod, b_mod, w_in, ret_decay_logit, ret_norm_g,
           diff_q_norm_g, diff_k_norm_g, diff_lambda, diff_norm_g, w_br_ret, w_br_diff, w_out,
           router_w, router_b, exp_w1, exp_b1, exp_w2, exp_b2):
    batch, seq, d = x.shape
    ctx_len = ctx.shape[1]
    assert d == D_MODEL and seq % GRID_W == 0 and batch + 1 <= 8
    n_tok = batch * seq

    cc = jnp.zeros((8, D_MODEL), F32).at[:batch].set(c).at[batch].set(c_ctx)
    mod = _mod(cc, w_mod, b_mod)
    sh_a, sc_a, g_a, sh_f, sc_f, g_f = [mod[:batch, i * D_MODEL:(i + 1) * D_MODEL].reshape(batch, 1, D_MODEL)
                                         for i in range(6)]
    csh_a = mod[batch:batch + 1, 0:D_MODEL].reshape(1, 1, D_MODEL)
    csc_a = mod[batch:batch + 1, D_MODEL:2 * D_MODEL].reshape(1, 1, D_MODEL)

    w_in_bf = w_in.astype(BF16)
    g1 = norm1_g.reshape(1, D_MODEL)
    tile = lambda g: jnp.tile(g.astype(F32), 2)
    qkg = jnp.zeros((8, LANES), F32).at[0].set(tile(diff_q_norm_g) * (DIFF_DH ** -0.5 * LOG2E)).at[1].set(tile(diff_k_norm_g))
    lane = jnp.arange(2 * LANES)
    same_group = lane[:, None] // DIFF_DH == lane[None, :] // DIFF_DH
    gmean = jnp.where(same_group, 1.0 / DIFF_DH, 0.0).astype(BF16)
    gmat = same_group[:LANES, :LANES].astype(BF16)
    x2 = x.reshape(n_tok, D_MODEL)
    tm = min(1024, seq)
    p_lat = _inproj(x2, g1, sh_a, sc_a, w_in_bf, qkg, gmean, _rope_tables(seq), tm, seq // tm)
    p_ctx = _inproj(ctx.reshape(batch * ctx_len, D_MODEL), g1, csh_a, csc_a, w_in_bf, qkg, gmean,
                    _identity_tables(ctx_len), ctx_len, 1)

    lg = jax.nn.log_sigmoid(ret_decay_logit.astype(F32))
    y_ret = _retention(lg, p_lat, p_ctx, ret_norm_g, batch, seq, ctx_len)

    lp = diff_lambda.astype(F32)
    lam = (jnp.exp(jnp.sum(lp[0] * lp[1])) - jnp.exp(jnp.sum(lp[2] * lp[3])) + LAMBDA_INIT).reshape(1)
    y_diff = _diff_attention(lam, p_lat, p_ctx, diff_norm_g, gmat, batch, seq, ctx_len)

    rw = jnp.zeros((D_MODEL, LANES), F32).at[:, :N_EXPERTS].set(router_w)
    rw_hi = rw.astype(BF16)
    rw_lo = (rw - rw_hi.astype(F32)).astype(BF16)
    rb = jnp.full((1, LANES), NEG_BIG, F32).at[0, :N_EXPERTS].set(router_b)
    xn, hp, meta, gate4, cnt = _merge(
        y_ret, y_diff, p_lat, x2, g_a, sh_f, sc_f, norm2_g.reshape(1, D_MODEL),
        w_br_ret.astype(BF16), w_br_diff.astype(BF16), w_out.astype(BF16), rw_hi, rw_lo, rb, seq)

    counts = cnt[0, :N_EXPERTS].astype(I32)
    padded = (counts + MOE_BLK - 1) // MOE_BLK * MOE_BLK
    pad_end = jnp.cumsum(padded)
    pad_start = pad_end - padded
    n_pairs = n_tok * TOP_K
    n_blocks = n_pairs // MOE_BLK + N_EXPERTS
    meta_i = meta.astype(I32)
    is_e = meta_i[None, :TOP_K] == jnp.arange(N_EXPERTS, dtype=I32)[:, None, None]
    dest = (jnp.sum(jnp.where(is_e, pad_start[:, None, None], 0), axis=0) + meta_i[TOP_K:]).reshape(n_pairs)
    block_start = jnp.arange(n_blocks, dtype=I32) * MOE_BLK
    block_e = jnp.minimum(jnp.sum((pad_end[None, :] <= block_start[:, None]).astype(I32), axis=1), N_EXPERTS - 1)
    n_used = (pad_end[-1] // MOE_BLK).reshape(1).astype(I32)

    xb = _dispatch(dest, hp, jnp.zeros((n_blocks * MOE_BLK, D_MODEL // 2), U32))
    e_ids = jnp.arange(N_EXPERTS, dtype=I32)
    later = jnp.where((counts[None, :] > 0) & (e_ids[None, :] > e_ids[:, None]), e_ids[None, :], N_EXPERTS)
    next_e = jnp.min(later, axis=1)
    next_e = jnp.where(next_e == N_EXPERTS, -1, next_e).astype(I32)
    yb = _experts(block_e, n_used, next_e, xb, exp_w1, exp_b1, exp_w2, exp_b2)
    out = _combine(dest, yb, gate4, xn, g_f, seq)
    return out.reshape(batch, seq, D_MODEL)


def kernel(x, c, ctx, c_ctx, norm1_g, norm2_g, w_mod, b_mod, w_in, ret_decay_logit, ret_norm_g, diff_q_norm_g, diff_k_norm_g, diff_lambda, diff_norm_g, w_br_ret, w_br_diff, w_out, router_w, router_b, exp_w1, exp_b1, exp_w2, exp_b2):
    assert norm1_g.shape[0] == 1, "single-layer block"
    return _layer(x, ctx, c, c_ctx, norm1_g[0], norm2_g[0], w_mod[0], b_mod[0], w_in[0], ret_decay_logit[0],
                  ret_norm_g[0], diff_q_norm_g[0], diff_k_norm_g[0], diff_lambda[0], diff_norm_g[0],
                  w_br_ret[0], w_br_diff[0], w_out[0], router_w[0], router_b[0],
                  exp_w1[0], exp_b1[0], exp_w2[0], exp_b2[0])
```

```python
import functools
import math

import numpy as np

import jax
import jax.numpy as jnp
from jax import lax
from jax.experimental import pallas as pl
from jax.experimental.pallas import tpu as pltpu

F32 = jnp.float32
BF16 = jnp.bfloat16
U32 = jnp.uint32
I32 = jnp.int32

D_MODEL = 1024
GRID_W = 64
RET_HEADS = 4
RET_DK = 256
RET_DV = 512
DIFF_DH = 64
DIFF_HEADS = 8
DIFF_DV = 128
N_EXPERTS = 32
TOP_K = 4
D_FF = 1024
SWIGLU_LIMIT = 7.0
SWIGLU_ALPHA = 1.702
ROPE_BASE = 10000.0
EPS = 1e-6
LAMBDA_INIT = 0.8 - 0.6 * math.exp(-0.3 * 0)

IN_COLS = 11264
COL_TILE = 1024
LANES = 128
SUBLANES = 8
MOE_BLK = 256
NEG_BIG = -1e30
LOG2E = 1.4426950408889634
SHIFT_SLACK = 1.0 + 2.0 ** -6
MAX_SAFE_SHIFT = 60.0
RET_SUB = 8
KV_UNROLL = 8
HIGHEST = lax.Precision.HIGHEST
MIB = 1024 * 1024


def _params(sem, vmem_mib):
    return pltpu.CompilerParams(dimension_semantics=sem, vmem_limit_bytes=vmem_mib * MIB)


def _mod_kernel(c_ref, w_ref, b_ref, o_ref):
    c = c_ref[...]
    s = c * jax.nn.sigmoid(c)
    o_ref[...] = jnp.dot(s, w_ref[...], preferred_element_type=F32, precision=HIGHEST) + b_ref[...]


def _mod(cc, w_mod, b_mod):
    n = w_mod.shape[1]
    tn = 1024
    return pl.pallas_call(
        _mod_kernel,
        out_shape=jax.ShapeDtypeStruct((8, n), F32),
        grid=(n // tn,),
        in_specs=[pl.BlockSpec((8, D_MODEL), lambda j: (0, 0)),
                  pl.BlockSpec((D_MODEL, tn), lambda j: (0, j)),
                  pl.BlockSpec((1, tn), lambda j: (0, j))],
        out_specs=pl.BlockSpec((8, tn), lambda j: (0, j)),
        compiler_params=_params(("arbitrary",), 32),
        name="mod",
    )(cc, w_mod, b_mod.reshape(1, n))


def _inproj_kernel(x_ref, g_ref, sh_ref, sc_ref, w_ref, qkg_ref, gmat_ref,
                   cr_ref, sr_ref, cc_ref, sc2_ref, cd_ref, sa_ref, sb_ref,
                   o_ref, h_scr, acc_scr):
    j = pl.program_id(1)

    @pl.when(j == 0)
    def _():
        xf = x_ref[...]
        ms = jnp.mean(xf * xf, axis=-1, keepdims=True)
        y = xf * lax.rsqrt(ms + EPS) * g_ref[...]
        h_scr[...] = (y * (1.0 + sc_ref[0]) + sh_ref[0]).astype(BF16)

    is_ret = j <= 1
    is_dqk = (j == 6) | (j == 7)

    def project():
        return jnp.dot(h_scr[...], w_ref[...], preferred_element_type=F32)

    @pl.when(is_ret)
    def _():
        acc_scr[...] = project()
        scale = jnp.where(j == 0, RET_DK ** -0.5, 1.0).astype(F32)
        for b in range(COL_TILE // LANES):
            xb = acc_scr[:, b * LANES:(b + 1) * LANES]
            cos = cr_ref[...] if b % 2 == 0 else cc_ref[...]
            sin = sr_ref[...] if b % 2 == 0 else sc2_ref[...]
            o = (xb * cos + pltpu.roll(xb, 64, 1) * sin) * scale
            o_ref[:, b * LANES:(b + 1) * LANES] = o.astype(BF16)

    @pl.when(is_dqk)
    def _():
        acc_scr[...] = project()
        g = jnp.where(j == 6, qkg_ref[0:1, :], qkg_ref[1:2, :])
        for b2 in range(COL_TILE // (2 * LANES)):
            x2 = acc_scr[:, b2 * 2 * LANES:(b2 + 1) * 2 * LANES]
            ms2 = jnp.dot((x2 * x2).astype(BF16), gmat_ref[...], preferred_element_type=F32)
            for half in range(2):
                b = 2 * b2 + half
                xb = x2[:, half * LANES:(half + 1) * LANES]
                yn = xb * lax.rsqrt(ms2[:, half * LANES:(half + 1) * LANES] + EPS) * g
                o = yn * cd_ref[...] + pltpu.roll(yn, 16, 1) * sa_ref[...] + pltpu.roll(yn, 112, 1) * sb_ref[...]
                o_ref[:, b * LANES:(b + 1) * LANES] = o.astype(BF16)

    @pl.when(jnp.logical_not(is_ret | is_dqk))
    def _():
        o_ref[...] = project().astype(BF16)


def _inproj(x2, g1, sh, sc, w_bf, qkg, gmat, tables, tm, tiles_per_batch):
    n = x2.shape[0]
    nb = sh.shape[0]
    tab_spec = pl.BlockSpec((tm, LANES), lambda i, j: (i % tiles_per_batch, 0))
    mod_spec = pl.BlockSpec((1, 1, D_MODEL), lambda i, j: (jnp.minimum(i // tiles_per_batch, nb - 1), 0, 0))
    return pl.pallas_call(
        _inproj_kernel,
        out_shape=jax.ShapeDtypeStruct((n, IN_COLS), BF16),
        grid=(n // tm, IN_COLS // COL_TILE),
        in_specs=[pl.BlockSpec((tm, D_MODEL), lambda i, j: (i, 0)),
                  pl.BlockSpec((1, D_MODEL), lambda i, j: (0, 0)),
                  mod_spec, mod_spec,
                  pl.BlockSpec((D_MODEL, COL_TILE), lambda i, j: (0, j)),
                  pl.BlockSpec((8, LANES), lambda i, j: (0, 0)),
                  pl.BlockSpec((2 * LANES, 2 * LANES), lambda i, j: (0, 0))] + [tab_spec] * 7,
        out_specs=pl.BlockSpec((tm, COL_TILE), lambda i, j: (i, j)),
        scratch_shapes=[pltpu.VMEM((tm, D_MODEL), BF16), pltpu.VMEM((tm, COL_TILE), F32)],
        compiler_params=_params(("arbitrary", "arbitrary"), 48),
        name="inproj",
    )(x2, g1, sh, sc, w_bf, qkg, gmat, *tables)


def _rope_tables(seq):
    n_rows = seq // GRID_W
    f32 = np.float32

    def angles(pos, half):
        inv = f32(ROPE_BASE) ** (-np.arange(half, dtype=f32) / f32(half))
        return (pos.astype(f32)[:, None] * inv[None, :]).astype(np.float64)

    ar, ac = angles(np.arange(n_rows), 64), angles(np.arange(GRID_W), 64)
    br, bc = angles(np.arange(n_rows), 16), angles(np.arange(GRID_W), 16)
    zr, zc = np.zeros_like(br), np.zeros_like(bc)
    cat = lambda parts, reps=1: np.tile(np.concatenate(parts, axis=1), (1, reps)).astype(f32)
    by_row = lambda t: jnp.repeat(jnp.asarray(t), GRID_W, axis=0)
    by_col = lambda t: jnp.tile(jnp.asarray(t), (n_rows, 1))
    cr = by_row(cat([np.cos(ar), np.cos(ar)]))
    sr = by_row(cat([-np.sin(ar), np.sin(ar)]))
    cc = by_col(cat([np.cos(ac), np.cos(ac)]))
    sc = by_col(cat([-np.sin(ac), np.sin(ac)]))
    cd = by_row(cat([np.cos(br), np.cos(br), zr, zr], 2)) + by_col(cat([zc, zc, np.cos(bc), np.cos(bc)], 2))
    sa = by_row(cat([zr, np.sin(br), zr, zr], 2)) + by_col(cat([zc, zc, zc, np.sin(bc)], 2))
    sb = by_row(cat([-np.sin(br), zr, zr, zr], 2)) + by_col(cat([zc, zc, -np.sin(bc), zc], 2))
    return [cr, sr, cc, sc, cd, sa, sb]


def _identity_tables(seq):
    one = jnp.ones((seq, LANES), F32)
    zero = jnp.zeros((seq, LANES), F32)
    return [one, zero, one, zero, one, zero, zero]


def _tn_dot(a, b):
    return lax.dot_general(a, b, (((0,), (0,)), ((), ())), preferred_element_type=F32)


def _nt_dot(a, b):
    return lax.dot_general(a, b, (((1,), (1,)), ((), ())), preferred_element_type=F32)


def _ret_kernel(lg_ref, q_ref, k_ref, v_ref, g_ref, ck_ref, cv_ref, gn_ref, o_ref,
                sf_scr, sb_scr, ob_scr, *, chunk, sub, n_blocks, ctx_len):
    h = pl.program_id(1)
    p = pl.program_id(2)
    c = pl.program_id(3)
    lgf = lg_ref[0, h]
    lgb = lg_ref[1, h]

    def col_iota(n):
        return lax.broadcasted_iota(I32, (n, 1), 0).astype(F32)

    def vexp(s):
        return jnp.exp(jnp.zeros((1, 1), F32) + s)

    @pl.when((p == 0) & (c == 0))
    def _():
        jc = col_iota(ctx_len)
        kc = ck_ref[...].astype(F32)
        vc = cv_ref[...]
        sf_scr[...] = _tn_dot((kc * jnp.exp(lgf * (ctx_len - 1.0 - jc))).astype(BF16), vc)
        sb_scr[...] = _tn_dot((kc * jnp.exp(lgb * jc)).astype(BF16), vc)

    ic = col_iota(chunk)

    @pl.when(p == 0)
    def _():
        q_decay = jnp.exp(lgb * (chunk - ic))
        k_decay = jnp.exp(lgb * ic)
        s_decay = vexp(lgb * chunk)
        for j in reversed(range(sub)):
            loc = pl.ds(j * chunk, chunk)
            glob = pl.ds(pl.multiple_of(((n_blocks - 1 - c) * sub + j) * chunk, chunk), chunk)
            qb = (q_ref[loc, :].astype(F32) * q_decay).astype(BF16)
            ob_scr[glob, :] = jnp.dot(qb, sb_scr[...].astype(BF16), preferred_element_type=F32)
            kb = (k_ref[loc, :].astype(F32) * k_decay).astype(BF16)
            sb_scr[...] = s_decay * sb_scr[...] + _tn_dot(kb, v_ref[loc, :])

    @pl.when(p == 1)
    def _():
        ri = lax.broadcasted_iota(I32, (chunk, chunk), 0)
        ci = lax.broadcasted_iota(I32, (chunk, chunk), 1)
        d = (ri - ci).astype(F32)
        mask = jnp.where(d > 0, jnp.exp(lgf * jnp.maximum(d, 0.0)),
                         jnp.where(d < 0, jnp.exp(lgb * jnp.maximum(-d, 0.0)), 2.0))
        q_decay = jnp.exp(lgf * (ic + 1.0))
        k_decay = jnp.exp(lgf * (chunk - 1.0 - ic))
        s_decay = vexp(lgf * chunk)
        for j in range(sub):
            loc = pl.ds(j * chunk, chunk)
            glob = pl.ds(pl.multiple_of((c * sub + j) * chunk, chunk), chunk)
            a = (_nt_dot(q_ref[loc, :], k_ref[loc, :]) * mask).astype(BF16)
            qf = (q_ref[loc, :].astype(F32) * q_decay).astype(BF16)
            o = (jnp.dot(a, v_ref[loc, :], preferred_element_type=F32)
                 + jnp.dot(qf, sf_scr[...].astype(BF16), preferred_element_type=F32)
                 + ob_scr[glob, :])
            kf = (k_ref[loc, :].astype(F32) * k_decay).astype(BF16)
            sf_scr[...] = s_decay * sf_scr[...] + _tn_dot(kf, v_ref[loc, :])
            ms = jnp.mean(o * o, axis=-1, keepdims=True)
            y = o * lax.rsqrt(ms + EPS) * gn_ref[0]
            gt = g_ref[loc, :].astype(F32)
            o_ref[loc, :] = (y * (gt * jax.nn.sigmoid(gt))).astype(BF16)


def _retention(lg, p_lat, p_ctx, ret_norm_g, batch, seq, ctx_len):
    chunk = min(256, seq)
    sub = math.gcd(seq // chunk, RET_SUB)
    blk = chunk * sub
    nb = seq // blk
    kern = functools.partial(_ret_kernel, chunk=chunk, sub=sub, n_blocks=nb, ctx_len=ctx_len)

    def rows(b, p, c):
        return b * nb + jnp.where(p == 0, nb - 1 - c, c)

    def rows_fwd(b, p, c):
        return b * nb + jnp.where(p == 0, 0, c)

    return pl.pallas_call(
        kern,
        out_shape=jax.ShapeDtypeStruct((batch * seq, RET_HEADS * RET_DV), BF16),
        grid=(batch, RET_HEADS, 2, nb),
        in_specs=[pl.BlockSpec(memory_space=pltpu.SMEM),
                  pl.BlockSpec((blk, RET_DK), lambda b, h, p, c: (rows(b, p, c), h)),
                  pl.BlockSpec((blk, RET_DK), lambda b, h, p, c: (rows(b, p, c), 4 + h)),
                  pl.BlockSpec((blk, RET_DV), lambda b, h, p, c: (rows(b, p, c), 4 + h)),
                  pl.BlockSpec((blk, RET_DV), lambda b, h, p, c: (rows_fwd(b, p, c), 8 + h)),
                  pl.BlockSpec((ctx_len, RET_DK), lambda b, h, p, c: (b, 4 + h)),
                  pl.BlockSpec((ctx_len, RET_DV), lambda b, h, p, c: (b, 4 + h)),
                  pl.BlockSpec((1, 1, RET_DV), lambda b, h, p, c: (h, 0, 0))],
        out_specs=pl.BlockSpec((blk, RET_DV), lambda b, h, p, c: (rows_fwd(b, p, c), h)),
        scratch_shapes=[pltpu.VMEM((RET_DK, RET_DV), F32), pltpu.VMEM((RET_DK, RET_DV), F32),
                        pltpu.VMEM((seq, RET_DV), F32)],
        compiler_params=_params(("arbitrary",) * 4, 48),
        name="ret",
    )(lg, p_lat, p_lat, p_lat, p_lat, p_ctx, p_ctx, ret_norm_g.reshape(RET_HEADS, 1, RET_DV))


def _dattn_kernel(lam_ref, q_ref, ck_ref, cv_ref, k_ref, v_ref, gn_ref, gmat_ref, o_ref,
                  qq_scr, kmax_scr, mp_scr, kp_scr, vt_scr, pt_scr, *, tq, tk, n_kv, ta, unroll, ctx_len):
    rows = 2 * tq
    n_all = kp_scr.shape[0]
    qi = pl.program_id(2)

    def sq_norms(t):
        tf = t.astype(F32)
        return jnp.dot((tf * tf).astype(BF16), gmat_ref[...], preferred_element_type=F32)

    def lane_const(n, hot):
        return jnp.where(lax.broadcasted_iota(I32, (n, LANES), 1) < hot, 1.0, 0.0).astype(BF16)

    @pl.when(qi == 0)
    def _():
        kmax_scr[...] = jnp.max(sq_norms(ck_ref[...]), axis=0, keepdims=True)
        kp_scr[:, LANES:2 * LANES] = lane_const(n_all, 3)
        kp_scr[0:ctx_len, 0:LANES] = ck_ref[...]
        vt_scr[:, 0:ctx_len] = cv_ref[...].astype(F32).T.astype(BF16)

        def body(c, carry):
            start = pl.multiple_of(c * tk, tk)
            k = k_ref[pl.ds(start, tk), :]
            kmax_scr[...] = jnp.maximum(kmax_scr[...], jnp.max(sq_norms(k), axis=0, keepdims=True))
            kp_scr[pl.ds(pl.multiple_of(ctx_len + start, LANES), tk), 0:LANES] = k
            vt_scr[:, pl.ds(pl.multiple_of(ctx_len + start, LANES), tk)] = (
                v_ref[pl.ds(start, tk), :].astype(F32).T.astype(BF16))
            return carry

        lax.fori_loop(0, n_kv, body, 0)

    qt = q_ref[...].astype(F32).T
    row = lax.broadcasted_iota(I32, (LANES, tq), 0)
    q1t = jnp.where(row < DIFF_DH, qt, 0.0)
    q2t = jnp.where(row >= DIFF_DH, qt, 0.0)
    qq_scr[0:LANES, 0:tq] = q1t.astype(BF16)
    qq_scr[0:LANES, tq:rows] = q2t.astype(BF16)

    def set_shift(shift):
        neg = -shift
        hi = neg.astype(BF16).astype(F32)
        mid = (neg - hi).astype(BF16).astype(F32)
        lo = neg - hi - mid
        row_r = lax.broadcasted_iota(I32, (LANES, rows), 0)
        pieces = jnp.where(row_r == 0, hi, jnp.where(row_r == 1, mid, jnp.where(row_r == 2, lo, 0.0)))
        qq_scr[LANES:2 * LANES, :] = pieces.astype(BF16)

    kmax = kmax_scr[...]
    b1 = jnp.sqrt(jnp.sum(q1t * q1t, axis=0, keepdims=True) * kmax[:, 0:1]) * SHIFT_SLACK
    b2 = jnp.sqrt(jnp.sum(q2t * q2t, axis=0, keepdims=True) * kmax[:, DIFF_DH:DIFF_DH + 1]) * SHIFT_SLACK
    bound = jnp.concatenate([b1, b2], axis=1)
    set_shift(bound)

    def key_tile(t):
        return pl.ds(pl.multiple_of(t * ta, ta), ta)

    @pl.when(jnp.max(bound) > MAX_SAFE_SHIFT)
    def _():
        mp_scr[...] = jnp.full(mp_scr.shape, NEG_BIG, F32)

        def body(t, carry):
            st = jnp.dot(kp_scr[key_tile(t), 0:LANES], qq_scr[0:LANES, :], preferred_element_type=F32)
            mp_scr[...] = jnp.maximum(mp_scr[...], jnp.max(st, axis=0, keepdims=True))
            return carry

        lax.fori_loop(0, n_all // ta, body, 0)
        set_shift(mp_scr[0:1, :])

    mp_scr[...] = jnp.zeros(mp_scr.shape, F32)

    def body_a(t, carry):
        for u in range(unroll):
            r = key_tile(t * unroll + u)
            p = jnp.exp2(jnp.dot(kp_scr[r, :], qq_scr[...], preferred_element_type=F32))
            pt_scr[r, :] = p.astype(BF16)
            mp_scr[...] += jnp.sum(p.reshape(ta // SUBLANES, SUBLANES, rows), axis=0)
        return carry

    lax.fori_loop(0, n_all // (ta * unroll), body_a, 0)

    acc = jnp.dot(vt_scr[...], pt_scr[...], preferred_element_type=F32)
    ot = acc / jnp.sum(mp_scr[...], axis=0, keepdims=True)
    d = (ot[:, 0:tq] - lam_ref[0] * ot[:, tq:rows]).T
    ms = jnp.mean(d * d, axis=-1, keepdims=True)
    y = d * lax.rsqrt(ms + EPS) * gn_ref[...] * (1.0 - LAMBDA_INIT)
    o_ref[...] = y.astype(BF16)


def _diff_attention(lam, p_lat, p_ctx, diff_norm_g, gmat, batch, seq, ctx_len):
    tq = min(512, seq)
    tk = min(512, seq)
    nq, nk = seq // tq, seq // tk
    n_all = ctx_len + seq
    ta = next(t for t in (528, 512, 384, 320, 256, 128) if n_all % t == 0)
    unroll = math.gcd(n_all // ta, KV_UNROLL)
    assert ctx_len % LANES == 0
    kern = functools.partial(_dattn_kernel, tq=tq, tk=tk, n_kv=nk, ta=ta, unroll=unroll, ctx_len=ctx_len)
    return pl.pallas_call(
        kern,
        out_shape=jax.ShapeDtypeStruct((batch * seq, DIFF_HEADS * DIFF_DV), BF16),
        grid=(batch, DIFF_HEADS, nq),
        in_specs=[pl.BlockSpec(memory_space=pltpu.SMEM),
                  pl.BlockSpec((tq, LANES), lambda b, h, qi: (b * nq + qi, 48 + h)),
                  pl.BlockSpec((ctx_len, LANES), lambda b, h, qi: (b, 56 + h)),
                  pl.BlockSpec((ctx_len, LANES), lambda b, h, qi: (b, 64 + h)),
                  pl.BlockSpec((seq, LANES), lambda b, h, qi: (b, 56 + h)),
                  pl.BlockSpec((seq, LANES), lambda b, h, qi: (b, 64 + h)),
                  pl.BlockSpec((1, LANES), lambda b, h, qi: (0, 0)),
                  pl.BlockSpec((LANES, LANES), lambda b, h, qi: (0, 0))],
        out_specs=pl.BlockSpec((tq, LANES), lambda b, h, qi: (b * nq + qi, h)),
        scratch_shapes=[pltpu.VMEM((2 * LANES, 2 * tq), BF16), pltpu.VMEM((1, LANES), F32),
                        pltpu.VMEM((SUBLANES, 2 * tq), F32), pltpu.VMEM((n_all, 2 * LANES), BF16),
                        pltpu.VMEM((DIFF_DV, n_all), BF16), pltpu.VMEM((n_all, 2 * tq), BF16)],
        compiler_params=_params(("arbitrary",) * 3, 48),
        name="dattn",
    )(lam, p_lat, p_ctx, p_ctx, p_lat, p_lat, diff_norm_g.reshape(1, DIFF_DV), gmat)


def _merge_kernel(yr_ref, yd_ref, ga_ref, gb_ref, x_ref, gatea_ref, shf_ref, scf_ref, n2_ref,
                  wr_ref, wd_ref, wo_ref, rwh_ref, rwl_ref, rb_ref,
                  xn_ref, hp_ref, meta_ref, gate_ref, cnt_ref, run_scr, *, tm):
    i = pl.program_id(0)

    @pl.when(i == 0)
    def _():
        run_scr[...] = jnp.zeros(run_scr.shape, F32)

    yr = jnp.dot(yr_ref[...], wr_ref[...], preferred_element_type=F32)
    yd = jnp.dot(yd_ref[...], wd_ref[...], preferred_element_type=F32)
    m = (jax.nn.sigmoid(ga_ref[...].astype(F32)) * yr + jax.nn.sigmoid(gb_ref[...].astype(F32)) * yd)
    z = jnp.dot(m.astype(BF16), wo_ref[...], preferred_element_type=F32)
    xn = x_ref[...] + gatea_ref[0] * z
    xn_ref[...] = xn

    ms = jnp.mean(xn * xn, axis=-1, keepdims=True)
    h2 = xn * lax.rsqrt(ms + EPS) * n2_ref[...]
    h2 = h2 * (1.0 + scf_ref[0]) + shf_ref[0]
    h_hi = h2.astype(BF16)
    bits = pltpu.bitcast(h_hi.astype(F32), U32)
    half = D_MODEL // 2
    hp_ref[...] = (bits[:, :half] >> 16) | (bits[:, half:] & jnp.uint32(0xFFFF0000))

    h_lo = (h2 - h_hi.astype(F32)).astype(BF16)
    logits = (jnp.dot(h_hi, rwh_ref[...], preferred_element_type=F32)
              + jnp.dot(h_lo, rwh_ref[...], preferred_element_type=F32)
              + jnp.dot(h_hi, rwl_ref[...], preferred_element_type=F32) + rb_ref[...])
    lane = lax.broadcasted_iota(I32, (tm, LANES), 1)
    lanef = lane.astype(F32)
    work = logits
    vals, idxs = [], []
    for _ in range(TOP_K):
        mk = jnp.max(work, axis=1, keepdims=True)
        ik = jnp.min(jnp.where(work == mk, lanef, float(LANES)), axis=1, keepdims=True)
        vals.append(mk)
        idxs.append(ik)
        work = jnp.where(lanef == ik, -jnp.inf, work)
    ex = [jnp.exp(v - vals[0]) for v in vals]
    den = ex[0] + ex[1] + ex[2] + ex[3]

    onehot = jnp.zeros((tm, LANES), F32)
    for ik in idxs:
        onehot = onehot + jnp.where(lanef == ik, 1.0, 0.0)
    ri = lax.broadcasted_iota(I32, (tm, tm), 0)
    ci = lax.broadcasted_iota(I32, (tm, tm), 1)
    tri = jnp.where(ri > ci, 1.0, 0.0).astype(BF16)
    base = run_scr[0:1, :] + jnp.dot(tri, onehot.astype(BF16), preferred_element_type=F32)
    run_scr[...] = run_scr[...] + jnp.sum(onehot, axis=0, keepdims=True)

    meta = jnp.zeros((tm, LANES), F32)
    gate_out = jnp.zeros((tm, LANES), F32)
    for k in range(TOP_K):
        rk = jnp.sum(jnp.where(lanef == idxs[k], base, 0.0), axis=1, keepdims=True)
        meta = jnp.where(lane == k, idxs[k], jnp.where(lane == TOP_K + k, rk, meta))
        gate_out = jnp.where(lane == k, ex[k] / den, gate_out)
    meta_ref[...] = meta.T[0:2 * TOP_K, :]
    gate_ref[...] = gate_out
    cnt_ref[...] = run_scr[...]


def _merge(y_ret, y_diff, p_lat, x2, g_a, sh_f, sc_f, norm2_g, w_r, w_d, w_o, rw_hi, rw_lo, rb, seq):
    n = x2.shape[0]
    tm = min(512, seq)
    tpb = seq // tm
    kern = functools.partial(_merge_kernel, tm=tm)
    mod_spec = pl.BlockSpec((1, 1, D_MODEL), lambda i: (i // tpb, 0, 0))
    const = lambda shape: pl.BlockSpec(shape, lambda i: (0,) * len(shape))
    tok = lambda w: pl.BlockSpec((tm, w), lambda i: (i, 0))
    return pl.pallas_call(
        kern,
        out_shape=(jax.ShapeDtypeStruct((n, D_MODEL), F32),
                   jax.ShapeDtypeStruct((n, D_MODEL // 2), U32),
                   jax.ShapeDtypeStruct((2 * TOP_K, n), F32),
                   jax.ShapeDtypeStruct((n, LANES), F32),
                   jax.ShapeDtypeStruct((8, LANES), F32)),
        grid=(n // tm,),
        in_specs=[tok(RET_HEADS * RET_DV), tok(D_MODEL),
                  pl.BlockSpec((tm, COL_TILE), lambda i: (i, 9)),
                  pl.BlockSpec((tm, COL_TILE), lambda i: (i, 10)),
                  tok(D_MODEL), mod_spec, mod_spec, mod_spec, const((1, D_MODEL)),
                  const((RET_HEADS * RET_DV, D_MODEL)), const((D_MODEL, D_MODEL)), const((D_MODEL, D_MODEL)),
                  const((D_MODEL, LANES)), const((D_MODEL, LANES)), const((1, LANES))],
        out_specs=(tok(D_MODEL), tok(D_MODEL // 2), pl.BlockSpec((2 * TOP_K, tm), lambda i: (0, i)),
                   tok(LANES), const((8, LANES))),
        scratch_shapes=[pltpu.VMEM((8, LANES), F32)],
        compiler_params=_params(("arbitrary",), 56),
        name="merge",
    )(y_ret, y_diff, p_lat, p_lat, x2, g_a, sh_f, sc_f, norm2_g, w_r, w_d, w_o, rw_hi, rw_lo, rb)


def _dispatch_kernel(dest_ref, fill_ref, nb_ref, hp_ref, xb_ref, zero_scr, sem, zsem, *, tm, n_tok, n_blocks):
    i = pl.program_id(0)

    @pl.when(i == 0)
    def _():
        zero_scr[...] = jnp.zeros(zero_scr.shape, U32)
        fill = zero_scr.shape[0]
        for e in range(N_EXPERTS):
            start = pl.multiple_of(fill_ref[e], SUBLANES)
            pltpu.make_async_copy(zero_scr, xb_ref.at[pl.ds(start, fill), :], zsem).start()
        for e in range(N_EXPERTS):
            pltpu.make_async_copy(zero_scr, xb_ref.at[pl.ds(0, fill), :], zsem).wait()
        blk = zero_scr.at[pl.ds(0, MOE_BLK), :]

        def start_blk(b, carry):
            pltpu.make_async_copy(blk, xb_ref.at[pl.ds(pl.multiple_of(b * MOE_BLK, MOE_BLK), MOE_BLK), :], zsem).start()
            return carry

        def wait_blk(b, carry):
            pltpu.make_async_copy(blk, xb_ref.at[pl.ds(0, MOE_BLK), :], zsem).wait()
            return carry

        lax.fori_loop(nb_ref[0], n_blocks, start_blk, 0)
        lax.fori_loop(nb_ref[0], n_blocks, wait_blk, 0)

    def body(g, carry):
        r0 = pl.multiple_of(g * SUBLANES, SUBLANES)
        for j in range(SUBLANES):
            for k in range(TOP_K):
                d = dest_ref[k * n_tok + i * tm + r0 + j]
                pltpu.make_async_copy(hp_ref.at[pl.ds(r0 + j, 1), :], xb_ref.at[pl.ds(d, 1), :], sem).start()
        return carry

    lax.fori_loop(0, tm // SUBLANES, body, 0)
    for _ in range(TOP_K):
        pltpu.make_async_copy(hp_ref, xb_ref.at[pl.ds(0, tm), :], sem).wait()


def _dispatch(dest, fill_start, n_used, hp, n_rows):
    n = hp.shape[0]
    tm = min(512, n)
    kern = functools.partial(_dispatch_kernel, tm=tm, n_tok=n, n_blocks=n_rows // MOE_BLK)
    return pl.pallas_call(
        kern,
        out_shape=jax.ShapeDtypeStruct((n_rows, D_MODEL // 2), U32),
        grid_spec=pltpu.PrefetchScalarGridSpec(
            num_scalar_prefetch=3,
            grid=(n // tm,),
            in_specs=[pl.BlockSpec((tm, D_MODEL // 2), lambda i, d, f, nb: (i, 0))],
            out_specs=pl.BlockSpec(memory_space=pl.ANY),
            scratch_shapes=[pltpu.VMEM((MOE_BLK + SUBLANES, D_MODEL // 2), U32), pltpu.SemaphoreType.DMA(()),
                            pltpu.SemaphoreType.DMA(())]),
        compiler_params=_params(("arbitrary",), 32),
        name="dispatch",
    )(dest, fill_start, n_used, hp)


def _expert_kernel(be_ref, nb_ref, nxt_ref, xb_ref, w1_hbm, b1_ref, w2_hbm, b2_ref, y_ref,
                   w1f_scr, w2f_scr, w1b_scr, w2b_scr, x_scr, grp_scr, sems):
    i = pl.program_id(0)
    e = be_ref[i]
    prev = be_ref[jnp.maximum(i - 1, 0)]
    active = i < nb_ref[0]

    def weight_copies(expert, slot):
        return (pltpu.make_async_copy(w1_hbm.at[expert], w1f_scr.at[slot], sems.at[0, slot]),
                pltpu.make_async_copy(w2_hbm.at[expert], w2f_scr.at[slot], sems.at[1, slot]))

    @pl.when(i == 0)
    def _():
        grp_scr[0] = 0
        for cp in weight_copies(e, 0):
            cp.start()

    @pl.when(active & (i > 0) & (e != prev))
    def _():
        grp_scr[0] = grp_scr[0] + 1

    @pl.when(active & ((i == 0) | (e != prev)))
    def _():
        slot = grp_scr[0] % 2
        for cp in weight_copies(e, slot):
            cp.wait()
        w1b_scr[...] = w1f_scr[slot].astype(BF16)
        w2b_scr[...] = w2f_scr[slot].astype(BF16)
        nxt = nxt_ref[e]

        @pl.when(nxt >= 0)
        def _():
            for cp in weight_copies(nxt, 1 - slot):
                cp.start()

    @pl.when(active)
    def _():
        xu = xb_ref[...]
        half = D_MODEL // 2
        x_scr[:, 0:half] = pltpu.bitcast(xu << 16, F32).astype(BF16)
        x_scr[:, half:] = pltpu.bitcast(xu & jnp.uint32(0xFFFF0000), F32).astype(BF16)
        hh = jnp.dot(x_scr[...], w1b_scr[...], preferred_element_type=F32) + b1_ref[0]
        glu = jnp.minimum(hh[:, :D_FF], SWIGLU_LIMIT)
        lin = jnp.clip(hh[:, D_FF:], -SWIGLU_LIMIT, SWIGLU_LIMIT)
        act = glu * jax.nn.sigmoid(SWIGLU_ALPHA * glu) * (lin + 1.0)
        y_ref[...] = jnp.dot(act.astype(BF16), w2b_scr[...], preferred_element_type=F32) + b2_ref[0]

    @pl.when(i >= nb_ref[0])
    def _():
        y_ref[...] = jnp.zeros(y_ref.shape, F32)


def _experts(block_e, n_used, next_e, xb, w1, b1, w2, b2):
    n_blocks = block_e.shape[0]
    rows = n_blocks * MOE_BLK
    return pl.pallas_call(
        _expert_kernel,
        out_shape=jax.ShapeDtypeStruct((rows, D_MODEL), F32),
        grid_spec=pltpu.PrefetchScalarGridSpec(
            num_scalar_prefetch=3,
            grid=(n_blocks,),
            in_specs=[pl.BlockSpec((MOE_BLK, D_MODEL // 2), lambda i, be, nb, nx: (jnp.minimum(i, nb[0] - 1), 0)),
                      pl.BlockSpec(memory_space=pl.ANY),
                      pl.BlockSpec((1, 1, 2 * D_FF), lambda i, be, nb, nx: (be[i], 0, 0)),
                      pl.BlockSpec(memory_space=pl.ANY),
                      pl.BlockSpec((1, 1, D_MODEL), lambda i, be, nb, nx: (be[i], 0, 0))],
            out_specs=pl.BlockSpec((MOE_BLK, D_MODEL), lambda i, be, nb, nx: (i, 0)),
            scratch_shapes=[pltpu.VMEM((2, D_MODEL, 2 * D_FF), F32), pltpu.VMEM((2, D_FF, D_MODEL), F32),
                            pltpu.VMEM((D_MODEL, 2 * D_FF), BF16), pltpu.VMEM((D_FF, D_MODEL), BF16),
                            pltpu.VMEM((MOE_BLK, D_MODEL), BF16),
                            pltpu.SMEM((1,), I32), pltpu.SemaphoreType.DMA((2, 2))]),
        compiler_params=_params(("arbitrary",), 56),
        name="expert",
    )(block_e, n_used, next_e, xb, w1, b1.reshape(N_EXPERTS, 1, 2 * D_FF), w2, b2.reshape(N_EXPERTS, 1, D_MODEL))


def _combine_kernel(dest_ref, yb_ref, gate_ref, xn_ref, gf_ref, o_ref, buf, sem, *, tm, n_tok):
    i = pl.program_id(0)

    def body(g, carry):
        r0 = pl.multiple_of(g * SUBLANES, SUBLANES)
        for j in range(SUBLANES):
            for k in range(TOP_K):
                d = dest_ref[k * n_tok + i * tm + r0 + j]
                pltpu.make_async_copy(yb_ref.at[pl.ds(d, 1), :], buf.at[k, pl.ds(r0 + j, 1), :], sem).start()
        return carry

    lax.fori_loop(0, tm // SUBLANES, body, 0)
    for k in range(TOP_K):
        pltpu.make_async_copy(yb_ref.at[pl.ds(0, tm), :], buf.at[k], sem).wait()
    g = gate_ref[...]
    y = g[:, 0:1] * buf[0]
    for k in range(1, TOP_K):
        y = y + g[:, k:k + 1] * buf[k]
    o_ref[...] = xn_ref[...] + gf_ref[0] * y


def _combine(dest, yb, gate, xn, g_f, seq):
    n = xn.shape[0]
    tm = min(256, seq)
    tpb = seq // tm
    kern = functools.partial(_combine_kernel, tm=tm, n_tok=n)
    return pl.pallas_call(
        kern,
        out_shape=jax.ShapeDtypeStruct((n, D_MODEL), F32),
        grid_spec=pltpu.PrefetchScalarGridSpec(
            num_scalar_prefetch=1,
            grid=(n // tm,),
            in_specs=[pl.BlockSpec(memory_space=pl.ANY),
                      pl.BlockSpec((tm, LANES), lambda i, d: (i, 0)),
                      pl.BlockSpec((tm, D_MODEL), lambda i, d: (i, 0)),
                      pl.BlockSpec((1, 1, D_MODEL), lambda i, d: (i // tpb, 0, 0))],
            out_specs=pl.BlockSpec((tm, D_MODEL), lambda i, d: (i, 0)),
            scratch_shapes=[pltpu.VMEM((TOP_K, tm, D_MODEL), F32), pltpu.SemaphoreType.DMA(())]),
        compiler_params=_params(("arbitrary",), 32),
        name="combine",
    )(dest, yb, gate, xn, g_f)


def _layer(x, ctx, c, c_ctx, norm1_g, norm2_g, w_mod, b_mod, w_in, ret_decay_logit, ret_norm_g,
           diff_q_norm_g, diff_k_norm_g, diff_lambda, diff_norm_g, w_br_ret, w_br_diff, w_out,
           router_w, router_b, exp_w1, exp_b1, exp_w2, exp_b2):
    batch, seq, d = x.shape
    ctx_len = ctx.shape[1]
    assert d == D_MODEL and seq % GRID_W == 0 and batch + 1 <= 8
    n_tok = batch * seq

    cc = jnp.zeros((8, D_MODEL), F32).at[:batch].set(c).at[batch].set(c_ctx)
    mod = _mod(cc, w_mod, b_mod)
    sh_a, sc_a, g_a, sh_f, sc_f, g_f = [mod[:batch, i * D_MODEL:(i + 1) * D_MODEL].reshape(batch, 1, D_MODEL)
                                         for i in range(6)]
    csh_a = mod[batch:batch + 1, 0:D_MODEL].reshape(1, 1, D_MODEL)
    csc_a = mod[batch:batch + 1, D_MODEL:2 * D_MODEL].reshape(1, 1, D_MODEL)

    w_in_bf = w_in.astype(BF16)
    g1 = norm1_g.reshape(1, D_MODEL)
    tile = lambda g: jnp.tile(g.astype(F32), 2)
    qkg = jnp.zeros((8, LANES), F32).at[0].set(tile(diff_q_norm_g) * (DIFF_DH ** -0.5 * LOG2E)).at[1].set(tile(diff_k_norm_g))
    lane = jnp.arange(2 * LANES)
    same_group = lane[:, None] // DIFF_DH == lane[None, :] // DIFF_DH
    gmean = jnp.where(same_group, 1.0 / DIFF_DH, 0.0).astype(BF16)
    gmat = same_group[:LANES, :LANES].astype(BF16)
    x2 = x.reshape(n_tok, D_MODEL)
    tm = min(1024, seq)
    p_lat = _inproj(x2, g1, sh_a, sc_a, w_in_bf, qkg, gmean, _rope_tables(seq), tm, seq // tm)
    p_ctx = _inproj(ctx.reshape(batch * ctx_len, D_MODEL), g1, csh_a, csc_a, w_in_bf, qkg, gmean,
                    _identity_tables(ctx_len), ctx_len, 1)

    lg = jax.nn.log_sigmoid(ret_decay_logit.astype(F32))
    y_ret = _retention(lg, p_lat, p_ctx, ret_norm_g, batch, seq, ctx_len)

    lp = diff_lambda.astype(F32)
    lam = (jnp.exp(jnp.sum(lp[0] * lp[1])) - jnp.exp(jnp.sum(lp[2] * lp[3])) + LAMBDA_INIT).reshape(1)
    y_diff = _diff_attention(lam, p_lat, p_ctx, diff_norm_g, gmat, batch, seq, ctx_len)

    rw = jnp.zeros((D_MODEL, LANES), F32).at[:, :N_EXPERTS].set(router_w)
    rw_hi = rw.astype(BF16)
    rw_lo = (rw - rw_hi.astype(F32)).astype(BF16)
    rb = jnp.full((1, LANES), NEG_BIG, F32).at[0, :N_EXPERTS].set(router_b)
    xn, hp, meta, gate4, cnt = _merge(
        y_ret, y_diff, p_lat, x2, g_a, sh_f, sc_f, norm2_g.reshape(1, D_MODEL),
        w_br_ret.astype(BF16), w_br_diff.astype(BF16), w_out.astype(BF16), rw_hi, rw_lo, rb, seq)

    counts = cnt[0, :N_EXPERTS].astype(I32)
    padded = (counts + MOE_BLK - 1) // MOE_BLK * MOE_BLK
    pad_end = jnp.cumsum(padded)
    pad_start = pad_end - padded
    n_pairs = n_tok * TOP_K
    n_blocks = n_pairs // MOE_BLK + N_EXPERTS
    meta_i = meta.astype(I32)
    is_e = meta_i[None, :TOP_K] == jnp.arange(N_EXPERTS, dtype=I32)[:, None, None]
    dest = (jnp.sum(jnp.where(is_e, pad_start[:, None, None], 0), axis=0) + meta_i[TOP_K:]).reshape(n_pairs)
    block_start = jnp.arange(n_blocks, dtype=I32) * MOE_BLK
    block_e = jnp.minimum(jnp.sum((pad_end[None, :] <= block_start[:, None]).astype(I32), axis=1), N_EXPERTS - 1)
    n_used = (pad_end[-1] // MOE_BLK).reshape(1).astype(I32)

    fill_start = ((pad_start + counts) // SUBLANES * SUBLANES).astype(I32)
    xb = _dispatch(dest, fill_start, n_used, hp, (n_blocks + 1) * MOE_BLK)
    e_ids = jnp.arange(N_EXPERTS, dtype=I32)
    later = jnp.where((counts[None, :] > 0) & (e_ids[None, :] > e_ids[:, None]), e_ids[None, :], N_EXPERTS)
    next_e = jnp.min(later, axis=1)
    next_e = jnp.where(next_e == N_EXPERTS, -1, next_e).astype(I32)
    yb = _experts(block_e, n_used, next_e, xb, exp_w1, exp_b1, exp_w2, exp_b2)
    out = _combine(dest, yb, gate4, xn, g_f, seq)
    return out.reshape(batch, seq, D_MODEL)


def kernel(x, c, ctx, c_ctx, norm1_g, norm2_g, w_mod, b_mod, w_in, ret_decay_logit, ret_norm_g, diff_q_norm_g, diff_k_norm_g, diff_lambda, diff_norm_g, w_br_ret, w_br_diff, w_out, router_w, router_b, exp_w1, exp_b1, exp_w2, exp_b2):
    assert norm1_g.shape[0] == 1, "single-layer block"
    return _layer(x, ctx, c, c_ctx, norm1_g[0], norm2_g[0], w_mod[0], b_mod[0], w_in[0], ret_decay_logit[0],
                  ret_norm_g[0], diff_q_norm_g[0], diff_k_norm_g[0], diff_lambda[0], diff_norm_g[0],
                  w_br_ret[0], w_br_diff[0], w_out[0], router_w[0], router_b[0],
                  exp_w1[0], exp_b1[0], exp_w2[0], exp_b2[0])
```

```python
import functools
import math

import numpy as np

import jax
import jax.numpy as jnp
from jax import lax
from jax.experimental import pallas as pl
from jax.experimental.pallas import tpu as pltpu

F32 = jnp.float32
BF16 = jnp.bfloat16
U32 = jnp.uint32
I32 = jnp.int32

D_MODEL = 1024
GRID_W = 64
RET_HEADS = 4
RET_DK = 256
RET_DV = 512
DIFF_DH = 64
DIFF_HEADS = 8
DIFF_DV = 128
N_EXPERTS = 32
TOP_K = 4
D_FF = 1024
SWIGLU_LIMIT = 7.0
SWIGLU_ALPHA = 1.702
ROPE_BASE = 10000.0
EPS = 1e-6
LAMBDA_INIT = 0.8 - 0.6 * math.exp(-0.3 * 0)

IN_COLS = 11264
COL_TILE = 1024
LANES = 128
SUBLANES = 8
MOE_BLK = 256
NEG_BIG = -1e30
LOG2E = 1.4426950408889634
SHIFT_SLACK = 1.0 + 2.0 ** -6
MAX_SAFE_SHIFT = 60.0
RET_SUB = 8
KV_UNROLL = 8
HIGHEST = lax.Precision.HIGHEST
MIB = 1024 * 1024


def _params(sem, vmem_mib):
    return pltpu.CompilerParams(dimension_semantics=sem, vmem_limit_bytes=vmem_mib * MIB)


def _mod_kernel(c_ref, w_ref, b_ref, o_ref):
    c = c_ref[...]
    s = c * jax.nn.sigmoid(c)
    o_ref[...] = jnp.dot(s, w_ref[...], preferred_element_type=F32, precision=HIGHEST) + b_ref[...]


def _mod(cc, w_mod, b_mod):
    n = w_mod.shape[1]
    tn = 1024
    return pl.pallas_call(
        _mod_kernel,
        out_shape=jax.ShapeDtypeStruct((8, n), F32),
        grid=(n // tn,),
        in_specs=[pl.BlockSpec((8, D_MODEL), lambda j: (0, 0)),
                  pl.BlockSpec((D_MODEL, tn), lambda j: (0, j)),
                  pl.BlockSpec((1, tn), lambda j: (0, j))],
        out_specs=pl.BlockSpec((8, tn), lambda j: (0, j)),
        compiler_params=_params(("arbitrary",), 32),
        name="mod",
    )(cc, w_mod, b_mod.reshape(1, n))


def _inproj_kernel(x_ref, g_ref, sh_ref, sc_ref, w_ref, qkg_ref, gmat_ref,
                   cr_ref, sr_ref, cc_ref, sc2_ref, cd_ref, sa_ref, sb_ref,
                   o_ref, h_scr, acc_scr):
    j = pl.program_id(1)

    @pl.when(j == 0)
    def _():
        xf = x_ref[...]
        ms = jnp.mean(xf * xf, axis=-1, keepdims=True)
        y = xf * lax.rsqrt(ms + EPS) * g_ref[...]
        h_scr[...] = (y * (1.0 + sc_ref[0]) + sh_ref[0]).astype(BF16)

    is_ret = j <= 1
    is_dqk = (j == 6) | (j == 7)

    def project():
        return jnp.dot(h_scr[...], w_ref[...], preferred_element_type=F32)

    @pl.when(is_ret)
    def _():
        acc_scr[...] = project()
        scale = jnp.where(j == 0, RET_DK ** -0.5, 1.0).astype(F32)
        for b in range(COL_TILE // LANES):
            xb = acc_scr[:, b * LANES:(b + 1) * LANES]
            cos = cr_ref[...] if b % 2 == 0 else cc_ref[...]
            sin = sr_ref[...] if b % 2 == 0 else sc2_ref[...]
            o = (xb * cos + pltpu.roll(xb, 64, 1) * sin) * scale
            o_ref[:, b * LANES:(b + 1) * LANES] = o.astype(BF16)

    @pl.when(is_dqk)
    def _():
        acc_scr[...] = project()
        g = jnp.where(j == 6, qkg_ref[0:1, :], qkg_ref[1:2, :])
        for b2 in range(COL_TILE // (2 * LANES)):
            x2 = acc_scr[:, b2 * 2 * LANES:(b2 + 1) * 2 * LANES]
            ms2 = jnp.dot((x2 * x2).astype(BF16), gmat_ref[...], preferred_element_type=F32)
            for half in range(2):
                b = 2 * b2 + half
                xb = x2[:, half * LANES:(half + 1) * LANES]
                yn = xb * lax.rsqrt(ms2[:, half * LANES:(half + 1) * LANES] + EPS) * g
                o = yn * cd_ref[...] + pltpu.roll(yn, 16, 1) * sa_ref[...] + pltpu.roll(yn, 112, 1) * sb_ref[...]
                o_ref[:, b * LANES:(b + 1) * LANES] = o.astype(BF16)

    @pl.when(jnp.logical_not(is_ret | is_dqk))
    def _():
        o_ref[...] = project().astype(BF16)


def _inproj(x2, g1, sh, sc, w_bf, qkg, gmat, tables, tm, tiles_per_batch):
    n = x2.shape[0]
    nb = sh.shape[0]
    tab_spec = pl.BlockSpec((tm, LANES), lambda i, j: (i % tiles_per_batch, 0))
    mod_spec = pl.BlockSpec((1, 1, D_MODEL), lambda i, j: (jnp.minimum(i // tiles_per_batch, nb - 1), 0, 0))
    return pl.pallas_call(
        _inproj_kernel,
        out_shape=jax.ShapeDtypeStruct((n, IN_COLS), BF16),
        grid=(n // tm, IN_COLS // COL_TILE),
        in_specs=[pl.BlockSpec((tm, D_MODEL), lambda i, j: (i, 0)),
                  pl.BlockSpec((1, D_MODEL), lambda i, j: (0, 0)),
                  mod_spec, mod_spec,
                  pl.BlockSpec((D_MODEL, COL_TILE), lambda i, j: (0, j)),
                  pl.BlockSpec((8, LANES), lambda i, j: (0, 0)),
                  pl.BlockSpec((2 * LANES, 2 * LANES), lambda i, j: (0, 0))] + [tab_spec] * 7,
        out_specs=pl.BlockSpec((tm, COL_TILE), lambda i, j: (i, j)),
        scratch_shapes=[pltpu.VMEM((tm, D_MODEL), BF16), pltpu.VMEM((tm, COL_TILE), F32)],
        compiler_params=_params(("arbitrary", "arbitrary"), 48),
        name="inproj",
    )(x2, g1, sh, sc, w_bf, qkg, gmat, *tables)


def _rope_tables(seq):
    n_rows = seq // GRID_W
    f32 = np.float32

    def angles(pos, half):
        inv = f32(ROPE_BASE) ** (-np.arange(half, dtype=f32) / f32(half))
        return (pos.astype(f32)[:, None] * inv[None, :]).astype(np.float64)

    ar, ac = angles(np.arange(n_rows), 64), angles(np.arange(GRID_W), 64)
    br, bc = angles(np.arange(n_rows), 16), angles(np.arange(GRID_W), 16)
    zr, zc = np.zeros_like(br), np.zeros_like(bc)
    cat = lambda parts, reps=1: np.tile(np.concatenate(parts, axis=1), (1, reps)).astype(f32)
    by_row = lambda t: jnp.repeat(jnp.asarray(t), GRID_W, axis=0)
    by_col = lambda t: jnp.tile(jnp.asarray(t), (n_rows, 1))
    cr = by_row(cat([np.cos(ar), np.cos(ar)]))
    sr = by_row(cat([-np.sin(ar), np.sin(ar)]))
    cc = by_col(cat([np.cos(ac), np.cos(ac)]))
    sc = by_col(cat([-np.sin(ac), np.sin(ac)]))
    cd = by_row(cat([np.cos(br), np.cos(br), zr, zr], 2)) + by_col(cat([zc, zc, np.cos(bc), np.cos(bc)], 2))
    sa = by_row(cat([zr, np.sin(br), zr, zr], 2)) + by_col(cat([zc, zc, zc, np.sin(bc)], 2))
    sb = by_row(cat([-np.sin(br), zr, zr, zr], 2)) + by_col(cat([zc, zc, -np.sin(bc), zc], 2))
    return [cr, sr, cc, sc, cd, sa, sb]


def _identity_tables(seq):
    one = jnp.ones((seq, LANES), F32)
    zero = jnp.zeros((seq, LANES), F32)
    return [one, zero, one, zero, one, zero, zero]


def _tn_dot(a, b):
    return lax.dot_general(a, b, (((0,), (0,)), ((), ())), preferred_element_type=F32)


def _nt_dot(a, b):
    return lax.dot_general(a, b, (((1,), (1,)), ((), ())), preferred_element_type=F32)


def _ret_kernel(lg_ref, q_ref, k_ref, v_ref, g_ref, ck_ref, cv_ref, gn_ref, o_ref,
                sf_scr, sb_scr, ob_scr, *, chunk, sub, n_blocks, ctx_len):
    h = pl.program_id(1)
    p = pl.program_id(2)
    c = pl.program_id(3)
    lgf = lg_ref[0, h]
    lgb = lg_ref[1, h]

    def col_iota(n):
        return lax.broadcasted_iota(I32, (n, 1), 0).astype(F32)

    def vexp(s):
        return jnp.exp(jnp.zeros((1, 1), F32) + s)

    @pl.when((p == 0) & (c == 0))
    def _():
        jc = col_iota(ctx_len)
        kc = ck_ref[...].astype(F32)
        vc = cv_ref[...]
        sf_scr[...] = _tn_dot((kc * jnp.exp(lgf * (ctx_len - 1.0 - jc))).astype(BF16), vc)
        sb_scr[...] = _tn_dot((kc * jnp.exp(lgb * jc)).astype(BF16), vc)

    ic = col_iota(chunk)

    def full_bf16(col):
        return jnp.broadcast_to(col, (chunk, RET_DK)).astype(BF16)

    @pl.when(p == 0)
    def _():
        q_decay = full_bf16(jnp.exp(lgb * (chunk - ic)))
        k_decay = full_bf16(jnp.exp(lgb * ic))
        s_decay = vexp(lgb * chunk)
        for j in reversed(range(sub)):
            loc = pl.ds(j * chunk, chunk)
            glob = pl.ds(pl.multiple_of(((n_blocks - 1 - c) * sub + j) * chunk, chunk), chunk)
            qb = q_ref[loc, :] * q_decay
            ob_scr[glob, :] = jnp.dot(qb, sb_scr[...].astype(BF16), preferred_element_type=F32)
            kb = k_ref[loc, :] * k_decay
            sb_scr[...] = s_decay * sb_scr[...] + _tn_dot(kb, v_ref[loc, :])

    @pl.when(p == 1)
    def _():
        ri = lax.broadcasted_iota(I32, (chunk, chunk), 0)
        ci = lax.broadcasted_iota(I32, (chunk, chunk), 1)
        d = (ri - ci).astype(F32)
        mask = jnp.where(d > 0, jnp.exp(lgf * jnp.maximum(d, 0.0)),
                         jnp.where(d < 0, jnp.exp(lgb * jnp.maximum(-d, 0.0)), 2.0))
        q_decay = full_bf16(jnp.exp(lgf * (ic + 1.0)))
        k_decay = full_bf16(jnp.exp(lgf * (chunk - 1.0 - ic)))
        s_decay = vexp(lgf * chunk)
        for j in range(sub):
            loc = pl.ds(j * chunk, chunk)
            glob = pl.ds(pl.multiple_of((c * sub + j) * chunk, chunk), chunk)
            a = (_nt_dot(q_ref[loc, :], k_ref[loc, :]) * mask).astype(BF16)
            qf = q_ref[loc, :] * q_decay
            o = (jnp.dot(a, v_ref[loc, :], preferred_element_type=F32)
                 + jnp.dot(qf, sf_scr[...].astype(BF16), preferred_element_type=F32)
                 + ob_scr[glob, :])
            kf = k_ref[loc, :] * k_decay
            sf_scr[...] = s_decay * sf_scr[...] + _tn_dot(kf, v_ref[loc, :])
            ms = jnp.mean(o * o, axis=-1, keepdims=True)
            y = o * lax.rsqrt(ms + EPS) * gn_ref[0]
            gt = g_ref[loc, :]
            o_ref[loc, :] = y.astype(BF16) * (gt * jax.nn.sigmoid(gt))


def _retention(lg, p_lat, p_ctx, ret_norm_g, batch, seq, ctx_len):
    chunk = min(256, seq)
    sub = math.gcd(seq // chunk, RET_SUB)
    blk = chunk * sub
    nb = seq // blk
    kern = functools.partial(_ret_kernel, chunk=chunk, sub=sub, n_blocks=nb, ctx_len=ctx_len)

    def rows(b, p, c):
        return b * nb + jnp.where(p == 0, nb - 1 - c, c)

    def rows_fwd(b, p, c):
        return b * nb + jnp.where(p == 0, 0, c)

    return pl.pallas_call(
        kern,
        out_shape=jax.ShapeDtypeStruct((batch * seq, RET_HEADS * RET_DV), BF16),
        grid=(batch, RET_HEADS, 2, nb),
        in_specs=[pl.BlockSpec(memory_space=pltpu.SMEM),
                  pl.BlockSpec((blk, RET_DK), lambda b, h, p, c: (rows(b, p, c), h)),
                  pl.BlockSpec((blk, RET_DK), lambda b, h, p, c: (rows(b, p, c), 4 + h)),
                  pl.BlockSpec((blk, RET_DV), lambda b, h, p, c: (rows(b, p, c), 4 + h)),
                  pl.BlockSpec((blk, RET_DV), lambda b, h, p, c: (rows_fwd(b, p, c), 8 + h)),
                  pl.BlockSpec((ctx_len, RET_DK), lambda b, h, p, c: (b, 4 + h)),
                  pl.BlockSpec((ctx_len, RET_DV), lambda b, h, p, c: (b, 4 + h)),
                  pl.BlockSpec((1, 1, RET_DV), lambda b, h, p, c: (h, 0, 0))],
        out_specs=pl.BlockSpec((blk, RET_DV), lambda b, h, p, c: (rows_fwd(b, p, c), h)),
        scratch_shapes=[pltpu.VMEM((RET_DK, RET_DV), F32), pltpu.VMEM((RET_DK, RET_DV), F32),
                        pltpu.VMEM((seq, RET_DV), F32)],
        compiler_params=_params(("arbitrary",) * 4, 48),
        name="ret",
    )(lg, p_lat, p_lat, p_lat, p_lat, p_ctx, p_ctx, ret_norm_g.reshape(RET_HEADS, 1, RET_DV))


def _dattn_kernel(lam_ref, q_ref, ck_ref, cv_ref, k_ref, v_ref, gn_ref, gmat_ref, o_ref,
                  qq_scr, kmax_scr, mp_scr, kp_scr, vt_scr, pt_scr, *, tq, tk, n_kv, ta, unroll, ctx_len):
    rows = 2 * tq
    n_all = kp_scr.shape[0]
    qi = pl.program_id(2)

    def sq_norms(t):
        tf = t.astype(F32)
        return jnp.dot((tf * tf).astype(BF16), gmat_ref[...], preferred_element_type=F32)

    def lane_const(n, hot):
        return jnp.where(lax.broadcasted_iota(I32, (n, LANES), 1) < hot, 1.0, 0.0).astype(BF16)

    @pl.when(qi == 0)
    def _():
        kmax_scr[...] = jnp.max(sq_norms(ck_ref[...]), axis=0, keepdims=True)
        kp_scr[:, LANES:2 * LANES] = lane_const(n_all, 3)
        kp_scr[0:ctx_len, 0:LANES] = ck_ref[...]
        vt_scr[:, 0:ctx_len] = cv_ref[...].astype(F32).T.astype(BF16)

        def body(c, carry):
            start = pl.multiple_of(c * tk, tk)
            k = k_ref[pl.ds(start, tk), :]
            kmax_scr[...] = jnp.maximum(kmax_scr[...], jnp.max(sq_norms(k), axis=0, keepdims=True))
            kp_scr[pl.ds(pl.multiple_of(ctx_len + start, LANES), tk), 0:LANES] = k
            vt_scr[:, pl.ds(pl.multiple_of(ctx_len + start, LANES), tk)] = (
                v_ref[pl.ds(start, tk), :].astype(F32).T.astype(BF16))
            return carry

        lax.fori_loop(0, n_kv, body, 0)

    qt = q_ref[...].astype(F32).T
    row = lax.broadcasted_iota(I32, (LANES, tq), 0)
    q1t = jnp.where(row < DIFF_DH, qt, 0.0)
    q2t = jnp.where(row >= DIFF_DH, qt, 0.0)
    qq_scr[0:LANES, 0:tq] = q1t.astype(BF16)
    qq_scr[0:LANES, tq:rows] = q2t.astype(BF16)

    def set_shift(shift):
        neg = -shift
        hi = neg.astype(BF16).astype(F32)
        mid = (neg - hi).astype(BF16).astype(F32)
        lo = neg - hi - mid
        row_r = lax.broadcasted_iota(I32, (LANES, rows), 0)
        pieces = jnp.where(row_r == 0, hi, jnp.where(row_r == 1, mid, jnp.where(row_r == 2, lo, 0.0)))
        qq_scr[LANES:2 * LANES, :] = pieces.astype(BF16)

    kmax = kmax_scr[...]
    b1 = jnp.sqrt(jnp.sum(q1t * q1t, axis=0, keepdims=True) * kmax[:, 0:1]) * SHIFT_SLACK
    b2 = jnp.sqrt(jnp.sum(q2t * q2t, axis=0, keepdims=True) * kmax[:, DIFF_DH:DIFF_DH + 1]) * SHIFT_SLACK
    bound = jnp.concatenate([b1, b2], axis=1)
    set_shift(bound)

    def key_tile(t):
        return pl.ds(pl.multiple_of(t * ta, ta), ta)

    @pl.when(jnp.max(bound) > MAX_SAFE_SHIFT)
    def _():
        mp_scr[...] = jnp.full(mp_scr.shape, NEG_BIG, F32)

        def body(t, carry):
            st = jnp.dot(kp_scr[key_tile(t), 0:LANES], qq_scr[0:LANES, :], preferred_element_type=F32)
            mp_scr[...] = jnp.maximum(mp_scr[...], jnp.max(st, axis=0, keepdims=True))
            return carry

        lax.fori_loop(0, n_all // ta, body, 0)
        set_shift(mp_scr[0:1, :])

    mp_scr[...] = jnp.zeros(mp_scr.shape, F32)

    def body_a(t, carry):
        for u in range(unroll):
            r = key_tile(t * unroll + u)
            p = jnp.exp2(jnp.dot(kp_scr[r, :], qq_scr[...], preferred_element_type=F32))
            pt_scr[r, :] = p.astype(BF16)
            mp_scr[...] += jnp.sum(p.reshape(ta // SUBLANES, SUBLANES, rows), axis=0)
        return carry

    lax.fori_loop(0, n_all // (ta * unroll), body_a, 0)

    acc = jnp.dot(vt_scr[...], pt_scr[...], preferred_element_type=F32)
    ot = acc / jnp.sum(mp_scr[...], axis=0, keepdims=True)
    d = (ot[:, 0:tq] - lam_ref[0] * ot[:, tq:rows]).T
    ms = jnp.mean(d * d, axis=-1, keepdims=True)
    y = d * lax.rsqrt(ms + EPS) * gn_ref[...] * (1.0 - LAMBDA_INIT)
    o_ref[...] = y.astype(BF16)


def _diff_attention(lam, p_lat, p_ctx, diff_norm_g, gmat, batch, seq, ctx_len):
    tq = min(512, seq)
    tk = min(512, seq)
    nq, nk = seq // tq, seq // tk
    n_all = ctx_len + seq
    ta = next(t for t in (528, 512, 384, 320, 256, 128) if n_all % t == 0)
    unroll = math.gcd(n_all // ta, KV_UNROLL)
    assert ctx_len % LANES == 0
    kern = functools.partial(_dattn_kernel, tq=tq, tk=tk, n_kv=nk, ta=ta, unroll=unroll, ctx_len=ctx_len)
    return pl.pallas_call(
        kern,
        out_shape=jax.ShapeDtypeStruct((batch * seq, DIFF_HEADS * DIFF_DV), BF16),
        grid=(batch, DIFF_HEADS, nq),
        in_specs=[pl.BlockSpec(memory_space=pltpu.SMEM),
                  pl.BlockSpec((tq, LANES), lambda b, h, qi: (b * nq + qi, 48 + h)),
                  pl.BlockSpec((ctx_len, LANES), lambda b, h, qi: (b, 56 + h)),
                  pl.BlockSpec((ctx_len, LANES), lambda b, h, qi: (b, 64 + h)),
                  pl.BlockSpec((seq, LANES), lambda b, h, qi: (b, 56 + h)),
                  pl.BlockSpec((seq, LANES), lambda b, h, qi: (b, 64 + h)),
                  pl.BlockSpec((1, LANES), lambda b, h, qi: (0, 0)),
                  pl.BlockSpec((LANES, LANES), lambda b, h, qi: (0, 0))],
        out_specs=pl.BlockSpec((tq, LANES), lambda b, h, qi: (b * nq + qi, h)),
        scratch_shapes=[pltpu.VMEM((2 * LANES, 2 * tq), BF16), pltpu.VMEM((1, LANES), F32),
                        pltpu.VMEM((SUBLANES, 2 * tq), F32), pltpu.VMEM((n_all, 2 * LANES), BF16),
                        pltpu.VMEM((DIFF_DV, n_all), BF16), pltpu.VMEM((n_all, 2 * tq), BF16)],
        compiler_params=_params(("arbitrary",) * 3, 48),
        name="dattn",
    )(lam, p_lat, p_ctx, p_ctx, p_lat, p_lat, diff_norm_g.reshape(1, DIFF_DV), gmat)


def _merge_kernel(yr_ref, yd_ref, ga_ref, gb_ref, x_ref, gatea_ref, shf_ref, scf_ref, n2_ref,
                  wr_ref, wd_ref, wo_ref, rwh_ref, rwl_ref, rb_ref,
                  xn_ref, hp_ref, meta_ref, gate_ref, cnt_ref, run_scr, *, tm):
    i = pl.program_id(0)

    @pl.when(i == 0)
    def _():
        run_scr[...] = jnp.zeros(run_scr.shape, F32)

    yr = jnp.dot(yr_ref[...], wr_ref[...], preferred_element_type=F32)
    yd = jnp.dot(yd_ref[...], wd_ref[...], preferred_element_type=F32)
    m = (jax.nn.sigmoid(ga_ref[...].astype(F32)) * yr + jax.nn.sigmoid(gb_ref[...].astype(F32)) * yd)
    z = jnp.dot(m.astype(BF16), wo_ref[...], preferred_element_type=F32)
    xn = x_ref[...] + gatea_ref[0] * z
    xn_ref[...] = xn

    ms = jnp.mean(xn * xn, axis=-1, keepdims=True)
    h2 = xn * lax.rsqrt(ms + EPS) * n2_ref[...]
    h2 = h2 * (1.0 + scf_ref[0]) + shf_ref[0]
    h_hi = h2.astype(BF16)
    bits = pltpu.bitcast(h_hi.astype(F32), U32)
    half = D_MODEL // 2
    hp_ref[...] = (bits[:, :half] >> 16) | (bits[:, half:] & jnp.uint32(0xFFFF0000))

    h_lo = (h2 - h_hi.astype(F32)).astype(BF16)
    logits = (jnp.dot(h_hi, rwh_ref[...], preferred_element_type=F32)
              + jnp.dot(h_lo, rwh_ref[...], preferred_element_type=F32)
              + jnp.dot(h_hi, rwl_ref[...], preferred_element_type=F32) + rb_ref[...])
    lane = lax.broadcasted_iota(I32, (tm, LANES), 1)
    lanef = lane.astype(F32)
    work = logits
    vals, idxs = [], []
    for _ in range(TOP_K):
        mk = jnp.max(work, axis=1, keepdims=True)
        ik = jnp.min(jnp.where(work == mk, lanef, float(LANES)), axis=1, keepdims=True)
        vals.append(mk)
        idxs.append(ik)
        work = jnp.where(lanef == ik, -jnp.inf, work)
    ex = [jnp.exp(v - vals[0]) for v in vals]
    den = ex[0] + ex[1] + ex[2] + ex[3]

    onehot = jnp.zeros((tm, LANES), F32)
    for ik in idxs:
        onehot = onehot + jnp.where(lanef == ik, 1.0, 0.0)
    ri = lax.broadcasted_iota(I32, (tm, tm), 0)
    ci = lax.broadcasted_iota(I32, (tm, tm), 1)
    tri = jnp.where(ri > ci, 1.0, 0.0).astype(BF16)
    base = run_scr[0:1, :] + jnp.dot(tri, onehot.astype(BF16), preferred_element_type=F32)
    run_scr[...] = run_scr[...] + jnp.sum(onehot, axis=0, keepdims=True)

    meta = jnp.zeros((tm, LANES), F32)
    gate_out = jnp.zeros((tm, LANES), F32)
    for k in range(TOP_K):
        rk = jnp.sum(jnp.where(lanef == idxs[k], base, 0.0), axis=1, keepdims=True)
        meta = jnp.where(lane == k, idxs[k], jnp.where(lane == TOP_K + k, rk, meta))
        gate_out = jnp.where(lane == k, ex[k] / den, gate_out)
    meta_ref[...] = meta.T[0:2 * TOP_K, :]
    gate_ref[...] = gate_out
    cnt_ref[...] = run_scr[...]


def _merge(y_ret, y_diff, p_lat, x2, g_a, sh_f, sc_f, norm2_g, w_r, w_d, w_o, rw_hi, rw_lo, rb, seq):
    n = x2.shape[0]
    tm = min(512, seq)
    tpb = seq // tm
    kern = functools.partial(_merge_kernel, tm=tm)
    mod_spec = pl.BlockSpec((1, 1, D_MODEL), lambda i: (i // tpb, 0, 0))
    const = lambda shape: pl.BlockSpec(shape, lambda i: (0,) * len(shape))
    tok = lambda w: pl.BlockSpec((tm, w), lambda i: (i, 0))
    return pl.pallas_call(
        kern,
        out_shape=(jax.ShapeDtypeStruct((n, D_MODEL), F32),
                   jax.ShapeDtypeStruct((n, D_MODEL // 2), U32),
                   jax.ShapeDtypeStruct((2 * TOP_K, n), F32),
                   jax.ShapeDtypeStruct((n, LANES), F32),
                   jax.ShapeDtypeStruct((8, LANES), F32)),
        grid=(n // tm,),
        in_specs=[tok(RET_HEADS * RET_DV), tok(D_MODEL),
                  pl.BlockSpec((tm, COL_TILE), lambda i: (i, 9)),
                  pl.BlockSpec((tm, COL_TILE), lambda i: (i, 10)),
                  tok(D_MODEL), mod_spec, mod_spec, mod_spec, const((1, D_MODEL)),
                  const((RET_HEADS * RET_DV, D_MODEL)), const((D_MODEL, D_MODEL)), const((D_MODEL, D_MODEL)),
                  const((D_MODEL, LANES)), const((D_MODEL, LANES)), const((1, LANES))],
        out_specs=(tok(D_MODEL), tok(D_MODEL // 2), pl.BlockSpec((2 * TOP_K, tm), lambda i: (0, i)),
                   tok(LANES), const((8, LANES))),
        scratch_shapes=[pltpu.VMEM((8, LANES), F32)],
        compiler_params=_params(("arbitrary",), 56),
        name="merge",
    )(y_ret, y_diff, p_lat, p_lat, x2, g_a, sh_f, sc_f, norm2_g, w_r, w_d, w_o, rw_hi, rw_lo, rb)


def _dispatch_kernel(dest_ref, fill_ref, nb_ref, hp_ref, xb_ref, zero_scr, sem, zsem, *, tm, n_tok, n_blocks):
    i = pl.program_id(0)

    @pl.when(i == 0)
    def _():
        zero_scr[...] = jnp.zeros(zero_scr.shape, U32)
        fill = zero_scr.shape[0]
        for e in range(N_EXPERTS):
            start = pl.multiple_of(fill_ref[e], SUBLANES)
            pltpu.make_async_copy(zero_scr, xb_ref.at[pl.ds(start, fill), :], zsem).start()
        for e in range(N_EXPERTS):
            pltpu.make_async_copy(zero_scr, xb_ref.at[pl.ds(0, fill), :], zsem).wait()
        blk = zero_scr.at[pl.ds(0, MOE_BLK), :]

        def start_blk(b, carry):
            pltpu.make_async_copy(blk, xb_ref.at[pl.ds(pl.multiple_of(b * MOE_BLK, MOE_BLK), MOE_BLK), :], zsem).start()
            return carry

        def wait_blk(b, carry):
            pltpu.make_async_copy(blk, xb_ref.at[pl.ds(0, MOE_BLK), :], zsem).wait()
            return carry

        lax.fori_loop(nb_ref[0], n_blocks, start_blk, 0)
        lax.fori_loop(nb_ref[0], n_blocks, wait_blk, 0)

    def body(g, carry):
        r0 = pl.multiple_of(g * SUBLANES, SUBLANES)
        for j in range(SUBLANES):
            for k in range(TOP_K):
                d = dest_ref[k * n_tok + i * tm + r0 + j]
                pltpu.make_async_copy(hp_ref.at[pl.ds(r0 + j, 1), :], xb_ref.at[pl.ds(d, 1), :], sem).start()
        return carry

    lax.fori_loop(0, tm // SUBLANES, body, 0)
    for _ in range(TOP_K):
        pltpu.make_async_copy(hp_ref, xb_ref.at[pl.ds(0, tm), :], sem).wait()


def _dispatch(dest, fill_start, n_used, hp, n_rows):
    n = hp.shape[0]
    tm = min(512, n)
    kern = functools.partial(_dispatch_kernel, tm=tm, n_tok=n, n_blocks=n_rows // MOE_BLK)
    return pl.pallas_call(
        kern,
        out_shape=jax.ShapeDtypeStruct((n_rows, D_MODEL // 2), U32),
        grid_spec=pltpu.PrefetchScalarGridSpec(
            num_scalar_prefetch=3,
            grid=(n // tm,),
            in_specs=[pl.BlockSpec((tm, D_MODEL // 2), lambda i, d, f, nb: (i, 0))],
            out_specs=pl.BlockSpec(memory_space=pl.ANY),
            scratch_shapes=[pltpu.VMEM((MOE_BLK + SUBLANES, D_MODEL // 2), U32), pltpu.SemaphoreType.DMA(()),
                            pltpu.SemaphoreType.DMA(())]),
        compiler_params=_params(("arbitrary",), 32),
        name="dispatch",
    )(dest, fill_start, n_used, hp)


def _expert_kernel(be_ref, nb_ref, nxt_ref, xb_ref, w1_hbm, b1_ref, w2_hbm, b2_ref, y_ref,
                   w1f_scr, w2f_scr, w1b_scr, w2b_scr, x_scr, grp_scr, sems):
    i = pl.program_id(0)
    e = be_ref[i]
    prev = be_ref[jnp.maximum(i - 1, 0)]
    active = i < nb_ref[0]

    def weight_copies(expert, slot):
        return (pltpu.make_async_copy(w1_hbm.at[expert], w1f_scr.at[slot], sems.at[0, slot]),
                pltpu.make_async_copy(w2_hbm.at[expert], w2f_scr.at[slot], sems.at[1, slot]))

    @pl.when(i == 0)
    def _():
        grp_scr[0] = 0
        for cp in weight_copies(e, 0):
            cp.start()

    @pl.when(active & (i > 0) & (e != prev))
    def _():
        grp_scr[0] = grp_scr[0] + 1

    @pl.when(active & ((i == 0) | (e != prev)))
    def _():
        slot = grp_scr[0] % 2
        for cp in weight_copies(e, slot):
            cp.wait()
        w1b_scr[...] = w1f_scr[slot].astype(BF16)
        w2b_scr[...] = w2f_scr[slot].astype(BF16)
        nxt = nxt_ref[e]

        @pl.when(nxt >= 0)
        def _():
            for cp in weight_copies(nxt, 1 - slot):
                cp.start()

    @pl.when(active)
    def _():
        xu = xb_ref[...]
        half = D_MODEL // 2
        x_scr[:, 0:half] = pltpu.bitcast(xu << 16, F32).astype(BF16)
        x_scr[:, half:] = pltpu.bitcast(xu & jnp.uint32(0xFFFF0000), F32).astype(BF16)
        hh = jnp.dot(x_scr[...], w1b_scr[...], preferred_element_type=F32) + b1_ref[0]
        glu = jnp.minimum(hh[:, :D_FF], SWIGLU_LIMIT)
        lin = jnp.clip(hh[:, D_FF:], -SWIGLU_LIMIT, SWIGLU_LIMIT)
        act = glu * jax.nn.sigmoid(SWIGLU_ALPHA * glu) * (lin + 1.0)
        y_ref[...] = jnp.dot(act.astype(BF16), w2b_scr[...], preferred_element_type=F32) + b2_ref[0]

    @pl.when(i >= nb_ref[0])
    def _():
        y_ref[...] = jnp.zeros(y_ref.shape, F32)


def _experts(block_e, n_used, next_e, xb, w1, b1, w2, b2):
    n_blocks = block_e.shape[0]
    rows = n_blocks * MOE_BLK
    return pl.pallas_call(
        _expert_kernel,
        out_shape=jax.ShapeDtypeStruct((rows, D_MODEL), F32),
        grid_spec=pltpu.PrefetchScalarGridSpec(
            num_scalar_prefetch=3,
            grid=(n_blocks,),
            in_specs=[pl.BlockSpec((MOE_BLK, D_MODEL // 2), lambda i, be, nb, nx: (jnp.minimum(i, nb[0] - 1), 0)),
                      pl.BlockSpec(memory_space=pl.ANY),
                      pl.BlockSpec((1, 1, 2 * D_FF), lambda i, be, nb, nx: (be[i], 0, 0)),
                      pl.BlockSpec(memory_space=pl.ANY),
                      pl.BlockSpec((1, 1, D_MODEL), lambda i, be, nb, nx: (be[i], 0, 0))],
            out_specs=pl.BlockSpec((MOE_BLK, D_MODEL), lambda i, be, nb, nx: (i, 0)),
            scratch_shapes=[pltpu.VMEM((2, D_MODEL, 2 * D_FF), F32), pltpu.VMEM((2, D_FF, D_MODEL), F32),
                            pltpu.VMEM((D_MODEL, 2 * D_FF), BF16), pltpu.VMEM((D_FF, D_MODEL), BF16),
                            pltpu.VMEM((MOE_BLK, D_MODEL), BF16),
                            pltpu.SMEM((1,), I32), pltpu.SemaphoreType.DMA((2, 2))]),
        compiler_params=_params(("arbitrary",), 56),
        name="expert",
    )(block_e, n_used, next_e, xb, w1, b1.reshape(N_EXPERTS, 1, 2 * D_FF), w2, b2.reshape(N_EXPERTS, 1, D_MODEL))


def _combine_kernel(dest_ref, yb_ref, gate_ref, xn_ref, gf_ref, o_ref, buf, sem, *, tm, n_tok):
    i = pl.program_id(0)

    def body(g, carry):
        r0 = pl.multiple_of(g * SUBLANES, SUBLANES)
        for j in range(SUBLANES):
            for k in range(TOP_K):
                d = dest_ref[k * n_tok + i * tm + r0 + j]
                pltpu.make_async_copy(yb_ref.at[pl.ds(d, 1), :], buf.at[k, pl.ds(r0 + j, 1), :], sem).start()
        return carry

    lax.fori_loop(0, tm // SUBLANES, body, 0)
    for k in range(TOP_K):
        pltpu.make_async_copy(yb_ref.at[pl.ds(0, tm), :], buf.at[k], sem).wait()
    g = gate_ref[...]
    y = g[:, 0:1] * buf[0]
    for k in range(1, TOP_K):
        y = y + g[:, k:k + 1] * buf[k]
    o_ref[...] = xn_ref[...] + gf_ref[0] * y


def _combine(dest, yb, gate, xn, g_f, seq):
    n = xn.shape[0]
    tm = min(256, seq)
    tpb = seq // tm
    kern = functools.partial(_combine_kernel, tm=tm, n_tok=n)
    return pl.pallas_call(
        kern,
        out_shape=jax.ShapeDtypeStruct((n, D_MODEL), F32),
        grid_spec=pltpu.PrefetchScalarGridSpec(
            num_scalar_prefetch=1,
            grid=(n // tm,),
            in_specs=[pl.BlockSpec(memory_space=pl.ANY),
                      pl.BlockSpec((tm, LANES), lambda i, d: (i, 0)),
                      pl.BlockSpec((tm, D_MODEL), lambda i, d: (i, 0)),
                      pl.BlockSpec((1, 1, D_MODEL), lambda i, d: (i // tpb, 0, 0))],
            out_specs=pl.BlockSpec((tm, D_MODEL), lambda i, d: (i, 0)),
            scratch_shapes=[pltpu.VMEM((TOP_K, tm, D_MODEL), F32), pltpu.SemaphoreType.DMA(())]),
        compiler_params=_params(("arbitrary",), 32),
        name="combine",
    )(dest, yb, gate, xn, g_f)


def _layer(x, ctx, c, c_ctx, norm1_g, norm2_g, w_mod, b_mod, w_in, ret_decay_logit, ret_norm_g,
           diff_q_norm_g, diff_k_norm_g, diff_lambda, diff_norm_g, w_br_ret, w_br_diff, w_out,
           router_w, router_b, exp_w1, exp_b1, exp_w2, exp_b2):
    batch, seq, d = x.shape
    ctx_len = ctx.shape[1]
    assert d == D_MODEL and seq % GRID_W == 0 and batch + 1 <= 8
    n_tok = batch * seq

    cc = jnp.zeros((8, D_MODEL), F32).at[:batch].set(c).at[batch].set(c_ctx)
    mod = _mod(cc, w_mod, b_mod)
    sh_a, sc_a, g_a, sh_f, sc_f, g_f = [mod[:batch, i * D_MODEL:(i + 1) * D_MODEL].reshape(batch, 1, D_MODEL)
                                         for i in range(6)]
    csh_a = mod[batch:batch + 1, 0:D_MODEL].reshape(1, 1, D_MODEL)
    csc_a = mod[batch:batch + 1, D_MODEL:2 * D_MODEL].reshape(1, 1, D_MODEL)

    w_in_bf = w_in.astype(BF16)
    g1 = norm1_g.reshape(1, D_MODEL)
    tile = lambda g: jnp.tile(g.astype(F32), 2)
    qkg = jnp.zeros((8, LANES), F32).at[0].set(tile(diff_q_norm_g) * (DIFF_DH ** -0.5 * LOG2E)).at[1].set(tile(diff_k_norm_g))
    lane = jnp.arange(2 * LANES)
    same_group = lane[:, None] // DIFF_DH == lane[None, :] // DIFF_DH
    gmean = jnp.where(same_group, 1.0 / DIFF_DH, 0.0).astype(BF16)
    gmat = same_group[:LANES, :LANES].astype(BF16)
    x2 = x.reshape(n_tok, D_MODEL)
    tm = min(1024, seq)
    p_lat = _inproj(x2, g1, sh_a, sc_a, w_in_bf, qkg, gmean, _rope_tables(seq), tm, seq // tm)
    p_ctx = _inproj(ctx.reshape(batch * ctx_len, D_MODEL), g1, csh_a, csc_a, w_in_bf, qkg, gmean,
                    _identity_tables(ctx_len), ctx_len, 1)

    lg = jax.nn.log_sigmoid(ret_decay_logit.astype(F32))
    y_ret = _retention(lg, p_lat, p_ctx, ret_norm_g, batch, seq, ctx_len)

    lp = diff_lambda.astype(F32)
    lam = (jnp.exp(jnp.sum(lp[0] * lp[1])) - jnp.exp(jnp.sum(lp[2] * lp[3])) + LAMBDA_INIT).reshape(1)
    y_diff = _diff_attention(lam, p_lat, p_ctx, diff_norm_g, gmat, batch, seq, ctx_len)

    rw = jnp.zeros((D_MODEL, LANES), F32).at[:, :N_EXPERTS].set(router_w)
    rw_hi = rw.astype(BF16)
    rw_lo = (rw - rw_hi.astype(F32)).astype(BF16)
    rb = jnp.full((1, LANES), NEG_BIG, F32).at[0, :N_EXPERTS].set(router_b)
    xn, hp, meta, gate4, cnt = _merge(
        y_ret, y_diff, p_lat, x2, g_a, sh_f, sc_f, norm2_g.reshape(1, D_MODEL),
        w_br_ret.astype(BF16), w_br_diff.astype(BF16), w_out.astype(BF16), rw_hi, rw_lo, rb, seq)

    counts = cnt[0, :N_EXPERTS].astype(I32)
    padded = (counts + MOE_BLK - 1) // MOE_BLK * MOE_BLK
    pad_end = jnp.cumsum(padded)
    pad_start = pad_end - padded
    n_pairs = n_tok * TOP_K
    n_blocks = n_pairs // MOE_BLK + N_EXPERTS
    meta_i = meta.astype(I32)
    is_e = meta_i[None, :TOP_K] == jnp.arange(N_EXPERTS, dtype=I32)[:, None, None]
    dest = (jnp.sum(jnp.where(is_e, pad_start[:, None, None], 0), axis=0) + meta_i[TOP_K:]).reshape(n_pairs)
    block_start = jnp.arange(n_blocks, dtype=I32) * MOE_BLK
    block_e = jnp.minimum(jnp.sum((pad_end[None, :] <= block_start[:, None]).astype(I32), axis=1), N_EXPERTS - 1)
    n_used = (pad_end[-1] // MOE_BLK).reshape(1).astype(I32)

    fill_start = ((pad_start + counts) // SUBLANES * SUBLANES).astype(I32)
    xb = _dispatch(dest, fill_start, n_used, hp, (n_blocks + 1) * MOE_BLK)
    e_ids = jnp.arange(N_EXPERTS, dtype=I32)
    later = jnp.where((counts[None, :] > 0) & (e_ids[None, :] > e_ids[:, None]), e_ids[None, :], N_EXPERTS)
    next_e = jnp.min(later, axis=1)
    next_e = jnp.where(next_e == N_EXPERTS, -1, next_e).astype(I32)
    yb = _experts(block_e, n_used, next_e, xb, exp_w1, exp_b1, exp_w2, exp_b2)
    out = _combine(dest, yb, gate4, xn, g_f, seq)
    return out.reshape(batch, seq, D_MODEL)


def kernel(x, c, ctx, c_ctx, norm1_g, norm2_g, w_mod, b_mod, w_in, ret_decay_logit, ret_norm_g, diff_q_norm_g, diff_k_norm_g, diff_lambda, diff_norm_g, w_br_ret, w_br_diff, w_out, router_w, router_b, exp_w1, exp_b1, exp_w2, exp_b2):
    assert norm1_g.shape[0] == 1, "single-layer block"
    return _layer(x, ctx, c, c_ctx, norm1_g[0], norm2_g[0], w_mod[0], b_mod[0], w_in[0], ret_decay_logit[0],
                  ret_norm_g[0], diff_q_norm_g[0], diff_k_norm_g[0], diff_lambda[0], diff_norm_g[0],
                  w_br_ret[0], w_br_diff[0], w_out[0], router_w[0], router_b[0],
                  exp_w1[0], exp_b1[0], exp_w2[0], exp_b2[0])
```

```python
import functools
import math

import numpy as np

import jax
import jax.numpy as jnp
from jax import lax
from jax.experimental import pallas as pl
from jax.experimental.pallas import tpu as pltpu

F32 = jnp.float32
BF16 = jnp.bfloat16
U32 = jnp.uint32
I32 = jnp.int32

D_MODEL = 1024
GRID_W = 64
RET_HEADS = 4
RET_DK = 256
RET_DV = 512
DIFF_DH = 64
DIFF_HEADS = 8
DIFF_DV = 128
N_EXPERTS = 32
TOP_K = 4
D_FF = 1024
SWIGLU_LIMIT = 7.0
SWIGLU_ALPHA = 1.702
ROPE_BASE = 10000.0
EPS = 1e-6
LAMBDA_INIT = 0.8 - 0.6 * math.exp(-0.3 * 0)

IN_COLS = 11264
COL_TILE = 1024
LANES = 128
SUBLANES = 8
MOE_BLK = 256
NEG_BIG = -1e30
LOG2E = 1.4426950408889634
SHIFT_SLACK = 1.0 + 2.0 ** -6
MAX_SAFE_SHIFT = 60.0
RET_SUB = 8
KV_UNROLL = 4
HIGHEST = lax.Precision.HIGHEST
MIB = 1024 * 1024


def _params(sem, vmem_mib):
    return pltpu.CompilerParams(dimension_semantics=sem, vmem_limit_bytes=vmem_mib * MIB)


def _mod_kernel(c_ref, w_ref, b_ref, o_ref):
    c = c_ref[...]
    s = c * jax.nn.sigmoid(c)
    o_ref[...] = jnp.dot(s, w_ref[...], preferred_element_type=F32, precision=HIGHEST) + b_ref[...]


def _mod(cc, w_mod, b_mod):
    n = w_mod.shape[1]
    tn = 1024
    return pl.pallas_call(
        _mod_kernel,
        out_shape=jax.ShapeDtypeStruct((8, n), F32),
        grid=(n // tn,),
        in_specs=[pl.BlockSpec((8, D_MODEL), lambda j: (0, 0)),
                  pl.BlockSpec((D_MODEL, tn), lambda j: (0, j)),
                  pl.BlockSpec((1, tn), lambda j: (0, j))],
        out_specs=pl.BlockSpec((8, tn), lambda j: (0, j)),
        compiler_params=_params(("arbitrary",), 32),
        name="mod",
    )(cc, w_mod, b_mod.reshape(1, n))


def _inproj_kernel(x_ref, g_ref, sh_ref, sc_ref, w_ref, qkg_ref, gmat_ref,
                   cr_ref, sr_ref, cc_ref, sc2_ref, cd_ref, sa_ref, sb_ref,
                   o_ref, h_scr, acc_scr):
    j = pl.program_id(1)

    @pl.when(j == 0)
    def _():
        xf = x_ref[...]
        ms = jnp.mean(xf * xf, axis=-1, keepdims=True)
        y = xf * lax.rsqrt(ms + EPS) * g_ref[...]
        h_scr[...] = (y * (1.0 + sc_ref[0]) + sh_ref[0]).astype(BF16)

    is_ret = j <= 1
    is_dqk = (j == 6) | (j == 7)

    def project():
        return jnp.dot(h_scr[...], w_ref[...], preferred_element_type=F32)

    @pl.when(is_ret)
    def _():
        acc_scr[...] = project()
        scale = jnp.where(j == 0, RET_DK ** -0.5, 1.0).astype(F32)
        for b in range(COL_TILE // LANES):
            xb = acc_scr[:, b * LANES:(b + 1) * LANES]
            cos = cr_ref[...] if b % 2 == 0 else cc_ref[...]
            sin = sr_ref[...] if b % 2 == 0 else sc2_ref[...]
            o = (xb * cos + pltpu.roll(xb, 64, 1) * sin) * scale
            o_ref[:, b * LANES:(b + 1) * LANES] = o.astype(BF16)

    @pl.when(is_dqk)
    def _():
        acc_scr[...] = project()
        g = jnp.where(j == 6, qkg_ref[0:1, :], qkg_ref[1:2, :])
        for b2 in range(COL_TILE // (2 * LANES)):
            x2 = acc_scr[:, b2 * 2 * LANES:(b2 + 1) * 2 * LANES]
            ms2 = jnp.dot((x2 * x2).astype(BF16), gmat_ref[...], preferred_element_type=F32)
            for half in range(2):
                b = 2 * b2 + half
                xb = x2[:, half * LANES:(half + 1) * LANES]
                yn = xb * lax.rsqrt(ms2[:, half * LANES:(half + 1) * LANES] + EPS) * g
                o = yn * cd_ref[...] + pltpu.roll(yn, 16, 1) * sa_ref[...] + pltpu.roll(yn, 112, 1) * sb_ref[...]
                o_ref[:, b * LANES:(b + 1) * LANES] = o.astype(BF16)

    @pl.when(jnp.logical_not(is_ret | is_dqk))
    def _():
        o_ref[...] = project().astype(BF16)


def _inproj(x2, g1, sh, sc, w_bf, qkg, gmat, tables, tm, tiles_per_batch):
    n = x2.shape[0]
    nb = sh.shape[0]
    tab_spec = pl.BlockSpec((tm, LANES), lambda i, j: (i % tiles_per_batch, 0))
    mod_spec = pl.BlockSpec((1, 1, D_MODEL), lambda i, j: (jnp.minimum(i // tiles_per_batch, nb - 1), 0, 0))
    return pl.pallas_call(
        _inproj_kernel,
        out_shape=jax.ShapeDtypeStruct((n, IN_COLS), BF16),
        grid=(n // tm, IN_COLS // COL_TILE),
        in_specs=[pl.BlockSpec((tm, D_MODEL), lambda i, j: (i, 0)),
                  pl.BlockSpec((1, D_MODEL), lambda i, j: (0, 0)),
                  mod_spec, mod_spec,
                  pl.BlockSpec((D_MODEL, COL_TILE), lambda i, j: (0, j)),
                  pl.BlockSpec((8, LANES), lambda i, j: (0, 0)),
                  pl.BlockSpec((2 * LANES, 2 * LANES), lambda i, j: (0, 0))] + [tab_spec] * 7,
        out_specs=pl.BlockSpec((tm, COL_TILE), lambda i, j: (i, j)),
        scratch_shapes=[pltpu.VMEM((tm, D_MODEL), BF16), pltpu.VMEM((tm, COL_TILE), F32)],
        compiler_params=_params(("arbitrary", "arbitrary"), 48),
        name="inproj",
    )(x2, g1, sh, sc, w_bf, qkg, gmat, *tables)


def _rope_tables(seq):
    n_rows = seq // GRID_W
    f32 = np.float32

    def angles(pos, half):
        inv = f32(ROPE_BASE) ** (-np.arange(half, dtype=f32) / f32(half))
        return (pos.astype(f32)[:, None] * inv[None, :]).astype(np.float64)

    ar, ac = angles(np.arange(n_rows), 64), angles(np.arange(GRID_W), 64)
    br, bc = angles(np.arange(n_rows), 16), angles(np.arange(GRID_W), 16)
    zr, zc = np.zeros_like(br), np.zeros_like(bc)
    cat = lambda parts, reps=1: np.tile(np.concatenate(parts, axis=1), (1, reps)).astype(f32)
    by_row = lambda t: jnp.repeat(jnp.asarray(t), GRID_W, axis=0)
    by_col = lambda t: jnp.tile(jnp.asarray(t), (n_rows, 1))
    cr = by_row(cat([np.cos(ar), np.cos(ar)]))
    sr = by_row(cat([-np.sin(ar), np.sin(ar)]))
    cc = by_col(cat([np.cos(ac), np.cos(ac)]))
    sc = by_col(cat([-np.sin(ac), np.sin(ac)]))
    cd = by_row(cat([np.cos(br), np.cos(br), zr, zr], 2)) + by_col(cat([zc, zc, np.cos(bc), np.cos(bc)], 2))
    sa = by_row(cat([zr, np.sin(br), zr, zr], 2)) + by_col(cat([zc, zc, zc, np.sin(bc)], 2))
    sb = by_row(cat([-np.sin(br), zr, zr, zr], 2)) + by_col(cat([zc, zc, -np.sin(bc), zc], 2))
    return [cr, sr, cc, sc, cd, sa, sb]


def _identity_tables(seq):
    one = jnp.ones((seq, LANES), F32)
    zero = jnp.zeros((seq, LANES), F32)
    return [one, zero, one, zero, one, zero, zero]


def _tn_dot(a, b):
    return lax.dot_general(a, b, (((0,), (0,)), ((), ())), preferred_element_type=F32)


def _nt_dot(a, b):
    return lax.dot_general(a, b, (((1,), (1,)), ((), ())), preferred_element_type=F32)


def _ret_kernel(lg_ref, q_ref, k_ref, v_ref, g_ref, ck_ref, cv_ref, gn_ref, o_ref,
                sf_scr, sb_scr, ob_scr, *, chunk, sub, n_blocks, ctx_len):
    h = pl.program_id(1)
    p = pl.program_id(2)
    c = pl.program_id(3)
    lgf = lg_ref[0, h]
    lgb = lg_ref[1, h]

    def col_iota(n):
        return lax.broadcasted_iota(I32, (n, 1), 0).astype(F32)

    def vexp(s):
        return jnp.exp(jnp.zeros((1, 1), F32) + s)

    @pl.when((p == 0) & (c == 0))
    def _():
        jc = col_iota(ctx_len)
        kc = ck_ref[...].astype(F32)
        vc = cv_ref[...]
        sf_scr[...] = _tn_dot((kc * jnp.exp(lgf * (ctx_len - 1.0 - jc))).astype(BF16), vc)
        sb_scr[...] = _tn_dot((kc * jnp.exp(lgb * jc)).astype(BF16), vc)

    ic = col_iota(chunk)

    def full_bf16(col):
        return jnp.broadcast_to(col, (chunk, RET_DK)).astype(BF16)

    @pl.when(p == 0)
    def _():
        q_decay = full_bf16(jnp.exp(lgb * (chunk - ic)))
        k_decay = full_bf16(jnp.exp(lgb * ic))
        s_decay = vexp(lgb * chunk)
        for j in reversed(range(sub)):
            loc = pl.ds(j * chunk, chunk)
            glob = pl.ds(pl.multiple_of(((n_blocks - 1 - c) * sub + j) * chunk, chunk), chunk)
            qb = q_ref[loc, :] * q_decay
            ob_scr[glob, :] = jnp.dot(qb, sb_scr[...].astype(BF16), preferred_element_type=F32)
            kb = k_ref[loc, :] * k_decay
            sb_scr[...] = s_decay * sb_scr[...] + _tn_dot(kb, v_ref[loc, :])

    @pl.when(p == 1)
    def _():
        ri = lax.broadcasted_iota(I32, (chunk, chunk), 0)
        ci = lax.broadcasted_iota(I32, (chunk, chunk), 1)
        d = (ri - ci).astype(F32)
        mask = jnp.where(d > 0, jnp.exp(lgf * jnp.maximum(d, 0.0)),
                         jnp.where(d < 0, jnp.exp(lgb * jnp.maximum(-d, 0.0)), 2.0))
        q_decay = full_bf16(jnp.exp(lgf * (ic + 1.0)))
        k_decay = full_bf16(jnp.exp(lgf * (chunk - 1.0 - ic)))
        s_decay = vexp(lgf * chunk)
        for j in range(sub):
            loc = pl.ds(j * chunk, chunk)
            glob = pl.ds(pl.multiple_of((c * sub + j) * chunk, chunk), chunk)
            a = (_nt_dot(q_ref[loc, :], k_ref[loc, :]) * mask).astype(BF16)
            qf = q_ref[loc, :] * q_decay
            o = (jnp.dot(a, v_ref[loc, :], preferred_element_type=F32)
                 + jnp.dot(qf, sf_scr[...].astype(BF16), preferred_element_type=F32)
                 + ob_scr[glob, :])
            kf = k_ref[loc, :] * k_decay
            sf_scr[...] = s_decay * sf_scr[...] + _tn_dot(kf, v_ref[loc, :])
            ms = jnp.mean(o * o, axis=-1, keepdims=True)
            y = o * lax.rsqrt(ms + EPS) * gn_ref[0]
            gt = g_ref[loc, :]
            o_ref[loc, :] = y.astype(BF16) * (gt * jax.nn.sigmoid(gt))


def _retention(lg, p_lat, p_ctx, ret_norm_g, batch, seq, ctx_len):
    chunk = min(256, seq)
    sub = math.gcd(seq // chunk, RET_SUB)
    blk = chunk * sub
    nb = seq // blk
    kern = functools.partial(_ret_kernel, chunk=chunk, sub=sub, n_blocks=nb, ctx_len=ctx_len)

    def rows(b, p, c):
        return b * nb + jnp.where(p == 0, nb - 1 - c, c)

    def rows_fwd(b, p, c):
        return b * nb + jnp.where(p == 0, 0, c)

    return pl.pallas_call(
        kern,
        out_shape=jax.ShapeDtypeStruct((batch * seq, RET_HEADS * RET_DV), BF16),
        grid=(batch, RET_HEADS, 2, nb),
        in_specs=[pl.BlockSpec(memory_space=pltpu.SMEM),
                  pl.BlockSpec((blk, RET_DK), lambda b, h, p, c: (rows(b, p, c), h)),
                  pl.BlockSpec((blk, RET_DK), lambda b, h, p, c: (rows(b, p, c), 4 + h)),
                  pl.BlockSpec((blk, RET_DV), lambda b, h, p, c: (rows(b, p, c), 4 + h)),
                  pl.BlockSpec((blk, RET_DV), lambda b, h, p, c: (rows_fwd(b, p, c), 8 + h)),
                  pl.BlockSpec((ctx_len, RET_DK), lambda b, h, p, c: (b, 4 + h)),
                  pl.BlockSpec((ctx_len, RET_DV), lambda b, h, p, c: (b, 4 + h)),
                  pl.BlockSpec((1, 1, RET_DV), lambda b, h, p, c: (h, 0, 0))],
        out_specs=pl.BlockSpec((blk, RET_DV), lambda b, h, p, c: (rows_fwd(b, p, c), h)),
        scratch_shapes=[pltpu.VMEM((RET_DK, RET_DV), F32), pltpu.VMEM((RET_DK, RET_DV), F32),
                        pltpu.VMEM((seq, RET_DV), F32)],
        compiler_params=_params(("arbitrary",) * 4, 48),
        name="ret",
    )(lg, p_lat, p_lat, p_lat, p_lat, p_ctx, p_ctx, ret_norm_g.reshape(RET_HEADS, 1, RET_DV))


def _dattn_kernel(lam_ref, q_ref, ck_ref, cv_ref, k_ref, v_ref, gn_ref, gmat_ref, o_ref,
                  qq_scr, kmax_scr, mp_scr, kp_scr, vt_scr, pt_scr, *, tq, tk, n_kv, ta, unroll, ctx_len):
    rows = 2 * tq
    n_all = kp_scr.shape[0]
    qi = pl.program_id(2)

    def sq_norms(t):
        tf = t.astype(F32)
        return jnp.dot((tf * tf).astype(BF16), gmat_ref[...], preferred_element_type=F32)

    def lane_const(n, hot):
        return jnp.where(lax.broadcasted_iota(I32, (n, LANES), 1) < hot, 1.0, 0.0).astype(BF16)

    @pl.when(qi == 0)
    def _():
        kmax_scr[...] = jnp.max(sq_norms(ck_ref[...]), axis=0, keepdims=True)
        kp_scr[:, LANES:2 * LANES] = lane_const(n_all, 3)
        kp_scr[0:ctx_len, 0:LANES] = ck_ref[...]
        vt_scr[:, 0:ctx_len] = cv_ref[...].astype(F32).T.astype(BF16)

        def body(c, carry):
            start = pl.multiple_of(c * tk, tk)
            k = k_ref[pl.ds(start, tk), :]
            kmax_scr[...] = jnp.maximum(kmax_scr[...], jnp.max(sq_norms(k), axis=0, keepdims=True))
            kp_scr[pl.ds(pl.multiple_of(ctx_len + start, LANES), tk), 0:LANES] = k
            vt_scr[:, pl.ds(pl.multiple_of(ctx_len + start, LANES), tk)] = (
                v_ref[pl.ds(start, tk), :].astype(F32).T.astype(BF16))
            return carry

        lax.fori_loop(0, n_kv, body, 0)

    qt = q_ref[...].astype(F32).T
    row = lax.broadcasted_iota(I32, (LANES, tq), 0)
    q1t = jnp.where(row < DIFF_DH, qt, 0.0)
    q2t = jnp.where(row >= DIFF_DH, qt, 0.0)
    qq_scr[0:LANES, 0:tq] = q1t.astype(BF16)
    qq_scr[0:LANES, tq:rows] = q2t.astype(BF16)

    def set_shift(shift):
        neg = -shift
        hi = neg.astype(BF16).astype(F32)
        mid = (neg - hi).astype(BF16).astype(F32)
        lo = neg - hi - mid
        row_r = lax.broadcasted_iota(I32, (LANES, rows), 0)
        pieces = jnp.where(row_r == 0, hi, jnp.where(row_r == 1, mid, jnp.where(row_r == 2, lo, 0.0)))
        qq_scr[LANES:2 * LANES, :] = pieces.astype(BF16)

    kmax = kmax_scr[...]
    b1 = jnp.sqrt(jnp.sum(q1t * q1t, axis=0, keepdims=True) * kmax[:, 0:1]) * SHIFT_SLACK
    b2 = jnp.sqrt(jnp.sum(q2t * q2t, axis=0, keepdims=True) * kmax[:, DIFF_DH:DIFF_DH + 1]) * SHIFT_SLACK
    bound = jnp.concatenate([b1, b2], axis=1)
    set_shift(bound)

    def key_tile(t):
        return pl.ds(pl.multiple_of(t * ta, ta), ta)

    @pl.when(jnp.max(bound) > MAX_SAFE_SHIFT)
    def _():
        mp_scr[...] = jnp.full(mp_scr.shape, NEG_BIG, F32)

        def body(t, carry):
            st = jnp.dot(kp_scr[key_tile(t), 0:LANES], qq_scr[0:LANES, :], preferred_element_type=F32)
            mp_scr[...] = jnp.maximum(mp_scr[...], jnp.max(st, axis=0, keepdims=True))
            return carry

        lax.fori_loop(0, n_all // ta, body, 0)
        set_shift(mp_scr[0:1, :])

    mp_scr[...] = jnp.zeros(mp_scr.shape, F32)

    def body_a(t, carry):
        for u in range(unroll):
            r = key_tile(t * unroll + u)
            p = jnp.exp2(jnp.dot(kp_scr[r, :], qq_scr[...], preferred_element_type=F32))
            pt_scr[r, :] = p.astype(BF16)
            mp_scr[...] += jnp.sum(p.reshape(ta // SUBLANES, SUBLANES, rows), axis=0)
        return carry

    lax.fori_loop(0, n_all // (ta * unroll), body_a, 0)

    acc = jnp.dot(vt_scr[...], pt_scr[...], preferred_element_type=F32)
    ot = acc / jnp.sum(mp_scr[...], axis=0, keepdims=True)
    d = (ot[:, 0:tq] - lam_ref[0] * ot[:, tq:rows]).T
    ms = jnp.mean(d * d, axis=-1, keepdims=True)
    y = d * lax.rsqrt(ms + EPS) * gn_ref[...] * (1.0 - LAMBDA_INIT)
    o_ref[...] = y.astype(BF16)


def _diff_attention(lam, p_lat, p_ctx, diff_norm_g, gmat, batch, seq, ctx_len):
    tq = min(1024, seq)
    tk = min(512, seq)
    nq, nk = seq // tq, seq // tk
    n_all = ctx_len + seq
    ta = next(t for t in (528, 512, 384, 320, 256, 128) if n_all % t == 0)
    unroll = math.gcd(n_all // ta, KV_UNROLL)
    assert ctx_len % LANES == 0
    kern = functools.partial(_dattn_kernel, tq=tq, tk=tk, n_kv=nk, ta=ta, unroll=unroll, ctx_len=ctx_len)
    return pl.pallas_call(
        kern,
        out_shape=jax.ShapeDtypeStruct((batch * seq, DIFF_HEADS * DIFF_DV), BF16),
        grid=(batch, DIFF_HEADS, nq),
        in_specs=[pl.BlockSpec(memory_space=pltpu.SMEM),
                  pl.BlockSpec((tq, LANES), lambda b, h, qi: (b * nq + qi, 48 + h)),
                  pl.BlockSpec((ctx_len, LANES), lambda b, h, qi: (b, 56 + h)),
                  pl.BlockSpec((ctx_len, LANES), lambda b, h, qi: (b, 64 + h)),
                  pl.BlockSpec((seq, LANES), lambda b, h, qi: (b, 56 + h)),
                  pl.BlockSpec((seq, LANES), lambda b, h, qi: (b, 64 + h)),
                  pl.BlockSpec((1, LANES), lambda b, h, qi: (0, 0)),
                  pl.BlockSpec((LANES, LANES), lambda b, h, qi: (0, 0))],
        out_specs=pl.BlockSpec((tq, LANES), lambda b, h, qi: (b * nq + qi, h)),
        scratch_shapes=[pltpu.VMEM((2 * LANES, 2 * tq), BF16), pltpu.VMEM((1, LANES), F32),
                        pltpu.VMEM((SUBLANES, 2 * tq), F32), pltpu.VMEM((n_all, 2 * LANES), BF16),
                        pltpu.VMEM((DIFF_DV, n_all), BF16), pltpu.VMEM((n_all, 2 * tq), BF16)],
        compiler_params=_params(("arbitrary",) * 3, 58),
        name="dattn",
    )(lam, p_lat, p_ctx, p_ctx, p_lat, p_lat, diff_norm_g.reshape(1, DIFF_DV), gmat)


def _merge_kernel(yr_ref, yd_ref, ga_ref, gb_ref, x_ref, gatea_ref, shf_ref, scf_ref, n2_ref,
                  wr_ref, wd_ref, wo_ref, rwh_ref, rwl_ref, rb_ref,
                  xn_ref, hp_ref, meta_ref, gate_ref, cnt_ref, run_scr, *, tm):
    i = pl.program_id(0)

    @pl.when(i == 0)
    def _():
        run_scr[...] = jnp.zeros(run_scr.shape, F32)

    yr = jnp.dot(yr_ref[...], wr_ref[...], preferred_element_type=F32)
    yd = jnp.dot(yd_ref[...], wd_ref[...], preferred_element_type=F32)
    m = (jax.nn.sigmoid(ga_ref[...].astype(F32)) * yr + jax.nn.sigmoid(gb_ref[...].astype(F32)) * yd)
    z = jnp.dot(m.astype(BF16), wo_ref[...], preferred_element_type=F32)
    xn = x_ref[...] + gatea_ref[0] * z
    xn_ref[...] = xn

    ms = jnp.mean(xn * xn, axis=-1, keepdims=True)
    h2 = xn * lax.rsqrt(ms + EPS) * n2_ref[...]
    h2 = h2 * (1.0 + scf_ref[0]) + shf_ref[0]
    h_hi = h2.astype(BF16)
    bits = pltpu.bitcast(h_hi.astype(F32), U32)
    half = D_MODEL // 2
    hp_ref[...] = (bits[:, :half] >> 16) | (bits[:, half:] & jnp.uint32(0xFFFF0000))

    h_lo = (h2 - h_hi.astype(F32)).astype(BF16)
    logits = (jnp.dot(h_hi, rwh_ref[...], preferred_element_type=F32)
              + jnp.dot(h_lo, rwh_ref[...], preferred_element_type=F32)
              + jnp.dot(h_hi, rwl_ref[...], preferred_element_type=F32) + rb_ref[...])
    lane = lax.broadcasted_iota(I32, (tm, LANES), 1)
    lanef = lane.astype(F32)
    work = logits
    vals, idxs = [], []
    for _ in range(TOP_K):
        mk = jnp.max(work, axis=1, keepdims=True)
        ik = jnp.min(jnp.where(work == mk, lanef, float(LANES)), axis=1, keepdims=True)
        vals.append(mk)
        idxs.append(ik)
        work = jnp.where(lanef == ik, -jnp.inf, work)
    ex = [jnp.exp(v - vals[0]) for v in vals]
    den = ex[0] + ex[1] + ex[2] + ex[3]

    onehot = jnp.zeros((tm, LANES), F32)
    for ik in idxs:
        onehot = onehot + jnp.where(lanef == ik, 1.0, 0.0)
    ri = lax.broadcasted_iota(I32, (tm, tm), 0)
    ci = lax.broadcasted_iota(I32, (tm, tm), 1)
    tri = jnp.where(ri > ci, 1.0, 0.0).astype(BF16)
    base = run_scr[0:1, :] + jnp.dot(tri, onehot.astype(BF16), preferred_element_type=F32)
    run_scr[...] = run_scr[...] + jnp.sum(onehot, axis=0, keepdims=True)

    meta = jnp.zeros((tm, LANES), F32)
    gate_out = jnp.zeros((tm, LANES), F32)
    for k in range(TOP_K):
        rk = jnp.sum(jnp.where(lanef == idxs[k], base, 0.0), axis=1, keepdims=True)
        meta = jnp.where(lane == k, idxs[k], jnp.where(lane == TOP_K + k, rk, meta))
        gate_out = jnp.where(lane == k, ex[k] / den, gate_out)
    meta_ref[...] = meta.T[0:2 * TOP_K, :]
    gate_ref[...] = gate_out
    cnt_ref[...] = run_scr[...]


def _merge(y_ret, y_diff, p_lat, x2, g_a, sh_f, sc_f, norm2_g, w_r, w_d, w_o, rw_hi, rw_lo, rb, seq):
    n = x2.shape[0]
    tm = min(512, seq)
    tpb = seq // tm
    kern = functools.partial(_merge_kernel, tm=tm)
    mod_spec = pl.BlockSpec((1, 1, D_MODEL), lambda i: (i // tpb, 0, 0))
    const = lambda shape: pl.BlockSpec(shape, lambda i: (0,) * len(shape))
    tok = lambda w: pl.BlockSpec((tm, w), lambda i: (i, 0))
    return pl.pallas_call(
        kern,
        out_shape=(jax.ShapeDtypeStruct((n, D_MODEL), F32),
                   jax.ShapeDtypeStruct((n, D_MODEL // 2), U32),
                   jax.ShapeDtypeStruct((2 * TOP_K, n), F32),
                   jax.ShapeDtypeStruct((n, LANES), F32),
                   jax.ShapeDtypeStruct((8, LANES), F32)),
        grid=(n // tm,),
        in_specs=[tok(RET_HEADS * RET_DV), tok(D_MODEL),
                  pl.BlockSpec((tm, COL_TILE), lambda i: (i, 9)),
                  pl.BlockSpec((tm, COL_TILE), lambda i: (i, 10)),
                  tok(D_MODEL), mod_spec, mod_spec, mod_spec, const((1, D_MODEL)),
                  const((RET_HEADS * RET_DV, D_MODEL)), const((D_MODEL, D_MODEL)), const((D_MODEL, D_MODEL)),
                  const((D_MODEL, LANES)), const((D_MODEL, LANES)), const((1, LANES))],
        out_specs=(tok(D_MODEL), tok(D_MODEL // 2), pl.BlockSpec((2 * TOP_K, tm), lambda i: (0, i)),
                   tok(LANES), const((8, LANES))),
        scratch_shapes=[pltpu.VMEM((8, LANES), F32)],
        compiler_params=_params(("arbitrary",), 56),
        name="merge",
    )(y_ret, y_diff, p_lat, p_lat, x2, g_a, sh_f, sc_f, norm2_g, w_r, w_d, w_o, rw_hi, rw_lo, rb)


def _dispatch_kernel(dest_ref, fill_ref, nb_ref, hp_ref, xb_ref, zero_scr, sem, zsem, *, tm, n_tok, n_blocks):
    i = pl.program_id(0)

    @pl.when(i == 0)
    def _():
        zero_scr[...] = jnp.zeros(zero_scr.shape, U32)
        fill = zero_scr.shape[0]
        for e in range(N_EXPERTS):
            start = pl.multiple_of(fill_ref[e], SUBLANES)
            pltpu.make_async_copy(zero_scr, xb_ref.at[pl.ds(start, fill), :], zsem).start()
        for e in range(N_EXPERTS):
            pltpu.make_async_copy(zero_scr, xb_ref.at[pl.ds(0, fill), :], zsem).wait()
        blk = zero_scr.at[pl.ds(0, MOE_BLK), :]

        def start_blk(b, carry):
            pltpu.make_async_copy(blk, xb_ref.at[pl.ds(pl.multiple_of(b * MOE_BLK, MOE_BLK), MOE_BLK), :], zsem).start()
            return carry

        def wait_blk(b, carry):
            pltpu.make_async_copy(blk, xb_ref.at[pl.ds(0, MOE_BLK), :], zsem).wait()
            return carry

        lax.fori_loop(nb_ref[0], n_blocks, start_blk, 0)
        lax.fori_loop(nb_ref[0], n_blocks, wait_blk, 0)

    def body(g, carry):
        r0 = pl.multiple_of(g * SUBLANES, SUBLANES)
        for j in range(SUBLANES):
            for k in range(TOP_K):
                d = dest_ref[k * n_tok + i * tm + r0 + j]
                pltpu.make_async_copy(hp_ref.at[pl.ds(r0 + j, 1), :], xb_ref.at[pl.ds(d, 1), :], sem).start()
        return carry

    lax.fori_loop(0, tm // SUBLANES, body, 0)
    for _ in range(TOP_K):
        pltpu.make_async_copy(hp_ref, xb_ref.at[pl.ds(0, tm), :], sem).wait()


def _dispatch(dest, fill_start, n_used, hp, n_rows):
    n = hp.shape[0]
    tm = min(512, n)
    kern = functools.partial(_dispatch_kernel, tm=tm, n_tok=n, n_blocks=n_rows // MOE_BLK)
    return pl.pallas_call(
        kern,
        out_shape=jax.ShapeDtypeStruct((n_rows, D_MODEL // 2), U32),
        grid_spec=pltpu.PrefetchScalarGridSpec(
            num_scalar_prefetch=3,
            grid=(n // tm,),
            in_specs=[pl.BlockSpec((tm, D_MODEL // 2), lambda i, d, f, nb: (i, 0))],
            out_specs=pl.BlockSpec(memory_space=pl.ANY),
            scratch_shapes=[pltpu.VMEM((MOE_BLK + SUBLANES, D_MODEL // 2), U32), pltpu.SemaphoreType.DMA(()),
                            pltpu.SemaphoreType.DMA(())]),
        compiler_params=_params(("arbitrary",), 32),
        name="dispatch",
    )(dest, fill_start, n_used, hp)


def _expert_kernel(be_ref, nb_ref, nxt_ref, xb_ref, w1_hbm, b1_ref, w2_hbm, b2_ref, y_ref,
                   w1f_scr, w2f_scr, w1b_scr, w2b_scr, x_scr, grp_scr, sems):
    i = pl.program_id(0)
    e = be_ref[i]
    prev = be_ref[jnp.maximum(i - 1, 0)]
    active = i < nb_ref[0]

    def weight_copies(expert, slot):
        return (pltpu.make_async_copy(w1_hbm.at[expert], w1f_scr.at[slot], sems.at[0, slot]),
                pltpu.make_async_copy(w2_hbm.at[expert], w2f_scr.at[slot], sems.at[1, slot]))

    @pl.when(i == 0)
    def _():
        grp_scr[0] = 0
        for cp in weight_copies(e, 0):
            cp.start()

    @pl.when(active & (i > 0) & (e != prev))
    def _():
        grp_scr[0] = grp_scr[0] + 1

    @pl.when(active & ((i == 0) | (e != prev)))
    def _():
        slot = grp_scr[0] % 2
        for cp in weight_copies(e, slot):
            cp.wait()
        w1b_scr[...] = w1f_scr[slot].astype(BF16)
        w2b_scr[...] = w2f_scr[slot].astype(BF16)
        nxt = nxt_ref[e]

        @pl.when(nxt >= 0)
        def _():
            for cp in weight_copies(nxt, 1 - slot):
                cp.start()

    @pl.when(active)
    def _():
        xu = xb_ref[...]
        half = D_MODEL // 2
        x_scr[:, 0:half] = pltpu.bitcast(xu << 16, F32).astype(BF16)
        x_scr[:, half:] = pltpu.bitcast(xu & jnp.uint32(0xFFFF0000), F32).astype(BF16)
        hh = jnp.dot(x_scr[...], w1b_scr[...], preferred_element_type=F32) + b1_ref[0]
        glu = jnp.minimum(hh[:, :D_FF], SWIGLU_LIMIT)
        lin = jnp.clip(hh[:, D_FF:], -SWIGLU_LIMIT, SWIGLU_LIMIT)
        act = glu * jax.nn.sigmoid(SWIGLU_ALPHA * glu) * (lin + 1.0)
        y_ref[...] = jnp.dot(act.astype(BF16), w2b_scr[...], preferred_element_type=F32) + b2_ref[0]

    @pl.when(i >= nb_ref[0])
    def _():
        y_ref[...] = jnp.zeros(y_ref.shape, F32)


def _experts(block_e, n_used, next_e, xb, w1, b1, w2, b2):
    n_blocks = block_e.shape[0]
    rows = n_blocks * MOE_BLK
    return pl.pallas_call(
        _expert_kernel,
        out_shape=jax.ShapeDtypeStruct((rows, D_MODEL), F32),
        grid_spec=pltpu.PrefetchScalarGridSpec(
            num_scalar_prefetch=3,
            grid=(n_blocks,),
            in_specs=[pl.BlockSpec((MOE_BLK, D_MODEL // 2), lambda i, be, nb, nx: (jnp.minimum(i, nb[0] - 1), 0)),
                      pl.BlockSpec(memory_space=pl.ANY),
                      pl.BlockSpec((1, 1, 2 * D_FF), lambda i, be, nb, nx: (be[i], 0, 0)),
                      pl.BlockSpec(memory_space=pl.ANY),
                      pl.BlockSpec((1, 1, D_MODEL), lambda i, be, nb, nx: (be[i], 0, 0))],
            out_specs=pl.BlockSpec((MOE_BLK, D_MODEL), lambda i, be, nb, nx: (i, 0)),
            scratch_shapes=[pltpu.VMEM((2, D_MODEL, 2 * D_FF), F32), pltpu.VMEM((2, D_FF, D_MODEL), F32),
                            pltpu.VMEM((D_MODEL, 2 * D_FF), BF16), pltpu.VMEM((D_FF, D_MODEL), BF16),
                            pltpu.VMEM((MOE_BLK, D_MODEL), BF16),
                            pltpu.SMEM((1,), I32), pltpu.SemaphoreType.DMA((2, 2))]),
        compiler_params=_params(("arbitrary",), 56),
        name="expert",
    )(block_e, n_used, next_e, xb, w1, b1.reshape(N_EXPERTS, 1, 2 * D_FF), w2, b2.reshape(N_EXPERTS, 1, D_MODEL))


def _combine_kernel(dest_ref, yb_ref, gate_ref, xn_ref, gf_ref, o_ref, buf, sem, *, tm, n_tok):
    i = pl.program_id(0)

    def body(g, carry):
        r0 = pl.multiple_of(g * SUBLANES, SUBLANES)
        for j in range(SUBLANES):
            for k in range(TOP_K):
                d = dest_ref[k * n_tok + i * tm + r0 + j]
                pltpu.make_async_copy(yb_ref.at[pl.ds(d, 1), :], buf.at[k, pl.ds(r0 + j, 1), :], sem).start()
        return carry

    lax.fori_loop(0, tm // SUBLANES, body, 0)
    for k in range(TOP_K):
        pltpu.make_async_copy(yb_ref.at[pl.ds(0, tm), :], buf.at[k], sem).wait()
    g = gate_ref[...]
    y = g[:, 0:1] * buf[0]
    for k in range(1, TOP_K):
        y = y + g[:, k:k + 1] * buf[k]
    o_ref[...] = xn_ref[...] + gf_ref[0] * y


def _combine(dest, yb, gate, xn, g_f, seq):
    n = xn.shape[0]
    tm = min(256, seq)
    tpb = seq // tm
    kern = functools.partial(_combine_kernel, tm=tm, n_tok=n)
    return pl.pallas_call(
        kern,
        out_shape=jax.ShapeDtypeStruct((n, D_MODEL), F32),
        grid_spec=pltpu.PrefetchScalarGridSpec(
            num_scalar_prefetch=1,
            grid=(n // tm,),
            in_specs=[pl.BlockSpec(memory_space=pl.ANY),
                      pl.BlockSpec((tm, LANES), lambda i, d: (i, 0)),
                      pl.BlockSpec((tm, D_MODEL), lambda i, d: (i, 0)),
                      pl.BlockSpec((1, 1, D_MODEL), lambda i, d: (i // tpb, 0, 0))],
            out_specs=pl.BlockSpec((tm, D_MODEL), lambda i, d: (i, 0)),
            scratch_shapes=[pltpu.VMEM((TOP_K, tm, D_MODEL), F32), pltpu.SemaphoreType.DMA(())]),
        compiler_params=_params(("arbitrary",), 32),
        name="combine",
    )(dest, yb, gate, xn, g_f)


def _layer(x, ctx, c, c_ctx, norm1_g, norm2_g, w_mod, b_mod, w_in, ret_decay_logit, ret_norm_g,
           diff_q_norm_g, diff_k_norm_g, diff_lambda, diff_norm_g, w_br_ret, w_br_diff, w_out,
           router_w, router_b, exp_w1, exp_b1, exp_w2, exp_b2):
    batch, seq, d = x.shape
    ctx_len = ctx.shape[1]
    assert d == D_MODEL and seq % GRID_W == 0 and batch + 1 <= 8
    n_tok = batch * seq

    cc = jnp.zeros((8, D_MODEL), F32).at[:batch].set(c).at[batch].set(c_ctx)
    mod = _mod(cc, w_mod, b_mod)
    sh_a, sc_a, g_a, sh_f, sc_f, g_f = [mod[:batch, i * D_MODEL:(i + 1) * D_MODEL].reshape(batch, 1, D_MODEL)
                                         for i in range(6)]
    csh_a = mod[batch:batch + 1, 0:D_MODEL].reshape(1, 1, D_MODEL)
    csc_a = mod[batch:batch + 1, D_MODEL:2 * D_MODEL].reshape(1, 1, D_MODEL)

    w_in_bf = w_in.astype(BF16)
    g1 = norm1_g.reshape(1, D_MODEL)
    tile = lambda g: jnp.tile(g.astype(F32), 2)
    qkg = jnp.zeros((8, LANES), F32).at[0].set(tile(diff_q_norm_g) * (DIFF_DH ** -0.5 * LOG2E)).at[1].set(tile(diff_k_norm_g))
    lane = jnp.arange(2 * LANES)
    same_group = lane[:, None] // DIFF_DH == lane[None, :] // DIFF_DH
    gmean = jnp.where(same_group, 1.0 / DIFF_DH, 0.0).astype(BF16)
    gmat = same_group[:LANES, :LANES].astype(BF16)
    x2 = x.reshape(n_tok, D_MODEL)
    tm = min(1024, seq)
    p_lat = _inproj(x2, g1, sh_a, sc_a, w_in_bf, qkg, gmean, _rope_tables(seq), tm, seq // tm)
    p_ctx = _inproj(ctx.reshape(batch * ctx_len, D_MODEL), g1, csh_a, csc_a, w_in_bf, qkg, gmean,
                    _identity_tables(ctx_len), ctx_len, 1)

    lg = jax.nn.log_sigmoid(ret_decay_logit.astype(F32))
    y_ret = _retention(lg, p_lat, p_ctx, ret_norm_g, batch, seq, ctx_len)

    lp = diff_lambda.astype(F32)
    lam = (jnp.exp(jnp.sum(lp[0] * lp[1])) - jnp.exp(jnp.sum(lp[2] * lp[3])) + LAMBDA_INIT).reshape(1)
    y_diff = _diff_attention(lam, p_lat, p_ctx, diff_norm_g, gmat, batch, seq, ctx_len)

    rw = jnp.zeros((D_MODEL, LANES), F32).at[:, :N_EXPERTS].set(router_w)
    rw_hi = rw.astype(BF16)
    rw_lo = (rw - rw_hi.astype(F32)).astype(BF16)
    rb = jnp.full((1, LANES), NEG_BIG, F32).at[0, :N_EXPERTS].set(router_b)
    xn, hp, meta, gate4, cnt = _merge(
        y_ret, y_diff, p_lat, x2, g_a, sh_f, sc_f, norm2_g.reshape(1, D_MODEL),
        w_br_ret.astype(BF16), w_br_diff.astype(BF16), w_out.astype(BF16), rw_hi, rw_lo, rb, seq)

    counts = cnt[0, :N_EXPERTS].astype(I32)
    padded = (counts + MOE_BLK - 1) // MOE_BLK * MOE_BLK
    pad_end = jnp.cumsum(padded)
    pad_start = pad_end - padded
    n_pairs = n_tok * TOP_K
    n_blocks = n_pairs // MOE_BLK + N_EXPERTS
    meta_i = meta.astype(I32)
    is_e = meta_i[None, :TOP_K] == jnp.arange(N_EXPERTS, dtype=I32)[:, None, None]
    dest = (jnp.sum(jnp.where(is_e, pad_start[:, None, None], 0), axis=0) + meta_i[TOP_K:]).reshape(n_pairs)
    block_start = jnp.arange(n_blocks, dtype=I32) * MOE_BLK
    block_e = jnp.minimum(jnp.sum((pad_end[None, :] <= block_start[:, None]).astype(I32), axis=1), N_EXPERTS - 1)
    n_used = (pad_end[-1] // MOE_BLK).reshape(1).astype(I32)

    fill_start = ((pad_start + counts) // SUBLANES * SUBLANES).astype(I32)
    xb = _dispatch(dest, fill_start, n_used, hp, (n_blocks + 1) * MOE_BLK)
    e_ids = jnp.arange(N_EXPERTS, dtype=I32)
    later = jnp.where((counts[None, :] > 0) & (e_ids[None, :] > e_ids[:, None]), e_ids[None, :], N_EXPERTS)
    next_e = jnp.min(later, axis=1)
    next_e = jnp.where(next_e == N_EXPERTS, -1, next_e).astype(I32)
    yb = _experts(block_e, n_used, next_e, xb, exp_w1, exp_b1, exp_w2, exp_b2)
    out = _combine(dest, yb, gate4, xn, g_f, seq)
    return out.reshape(batch, seq, D_MODEL)


def kernel(x, c, ctx, c_ctx, norm1_g, norm2_g, w_mod, b_mod, w_in, ret_decay_logit, ret_norm_g, diff_q_norm_g, diff_k_norm_g, diff_lambda, diff_norm_g, w_br_ret, w_br_diff, w_out, router_w, router_b, exp_w1, exp_b1, exp_w2, exp_b2):
    assert norm1_g.shape[0] == 1, "single-layer block"
    return _layer(x, ctx, c, c_ctx, norm1_g[0], norm2_g[0], w_mod[0], b_mod[0], w_in[0], ret_decay_logit[0],
                  ret_norm_g[0], diff_q_norm_g[0], diff_k_norm_g[0], diff_lambda[0], diff_norm_g[0],
                  w_br_ret[0], w_br_diff[0], w_out[0], router_w[0], router_b[0],
                  exp_w1[0], exp_b1[0], exp_w2[0], exp_b2[0])
```

```python
import functools
import math

import numpy as np

import jax
import jax.numpy as jnp
from jax import lax
from jax.experimental import pallas as pl
from jax.experimental.pallas import tpu as pltpu

F32 = jnp.float32
BF16 = jnp.bfloat16
U32 = jnp.uint32
I32 = jnp.int32

D_MODEL = 1024
GRID_W = 64
RET_HEADS = 4
RET_DK = 256
RET_DV = 512
DIFF_DH = 64
DIFF_HEADS = 8
DIFF_DV = 128
N_EXPERTS = 32
TOP_K = 4
D_FF = 1024
SWIGLU_LIMIT = 7.0
SWIGLU_ALPHA = 1.702
ROPE_BASE = 10000.0
EPS = 1e-6
LAMBDA_INIT = 0.8 - 0.6 * math.exp(-0.3 * 0)

IN_COLS = 11264
COL_TILE = 1024
LANES = 128
SUBLANES = 8
MOE_BLK = 256
NEG_BIG = -1e30
LOG2E = 1.4426950408889634
SHIFT_SLACK = 1.0 + 2.0 ** -6
MAX_SAFE_SHIFT = 60.0
RET_SUB = 8
KV_UNROLL = 4
HIGHEST = lax.Precision.HIGHEST
MIB = 1024 * 1024


def _params(sem, vmem_mib):
    return pltpu.CompilerParams(dimension_semantics=sem, vmem_limit_bytes=vmem_mib * MIB)


def _mod_kernel(c_ref, w_ref, b_ref, o_ref):
    c = c_ref[...]
    s = c * jax.nn.sigmoid(c)
    o_ref[...] = jnp.dot(s, w_ref[...], preferred_element_type=F32, precision=HIGHEST) + b_ref[...]


def _mod(cc, w_mod, b_mod):
    n = w_mod.shape[1]
    tn = 1024
    return pl.pallas_call(
        _mod_kernel,
        out_shape=jax.ShapeDtypeStruct((8, n), F32),
        grid=(n // tn,),
        in_specs=[pl.BlockSpec((8, D_MODEL), lambda j: (0, 0)),
                  pl.BlockSpec((D_MODEL, tn), lambda j: (0, j)),
                  pl.BlockSpec((1, tn), lambda j: (0, j))],
        out_specs=pl.BlockSpec((8, tn), lambda j: (0, j)),
        compiler_params=_params(("arbitrary",), 32),
        name="mod",
    )(cc, w_mod, b_mod.reshape(1, n))


def _inproj_kernel(x_ref, g_ref, sh_ref, sc_ref, w_ref, qkg_ref, gmat_ref,
                   cr_ref, sr_ref, cc_ref, sc2_ref, cd_ref, sa_ref, sb_ref,
                   o_ref, h_scr, acc_scr):
    j = pl.program_id(1)

    @pl.when(j == 0)
    def _():
        xf = x_ref[...]
        ms = jnp.mean(xf * xf, axis=-1, keepdims=True)
        y = xf * lax.rsqrt(ms + EPS) * g_ref[...]
        h_scr[...] = (y * (1.0 + sc_ref[0]) + sh_ref[0]).astype(BF16)

    is_ret = j <= 1
    is_dqk = (j == 6) | (j == 7)

    def project():
        return jnp.dot(h_scr[...], w_ref[...], preferred_element_type=F32)

    @pl.when(is_ret)
    def _():
        acc_scr[...] = project()
        scale = jnp.where(j == 0, RET_DK ** -0.5, 1.0).astype(F32)
        for b in range(COL_TILE // LANES):
            xb = acc_scr[:, b * LANES:(b + 1) * LANES]
            cos = cr_ref[...] if b % 2 == 0 else cc_ref[...]
            sin = sr_ref[...] if b % 2 == 0 else sc2_ref[...]
            o = (xb * cos + pltpu.roll(xb, 64, 1) * sin) * scale
            o_ref[:, b * LANES:(b + 1) * LANES] = o.astype(BF16)

    @pl.when(is_dqk)
    def _():
        acc_scr[...] = project()
        g = jnp.where(j == 6, qkg_ref[0:1, :], qkg_ref[1:2, :])
        for b2 in range(COL_TILE // (2 * LANES)):
            x2 = acc_scr[:, b2 * 2 * LANES:(b2 + 1) * 2 * LANES]
            ms2 = jnp.dot((x2 * x2).astype(BF16), gmat_ref[...], preferred_element_type=F32)
            for half in range(2):
                b = 2 * b2 + half
                xb = x2[:, half * LANES:(half + 1) * LANES]
                yn = xb * lax.rsqrt(ms2[:, half * LANES:(half + 1) * LANES] + EPS) * g
                o = yn * cd_ref[...] + pltpu.roll(yn, 16, 1) * sa_ref[...] + pltpu.roll(yn, 112, 1) * sb_ref[...]
                o_ref[:, b * LANES:(b + 1) * LANES] = o.astype(BF16)

    @pl.when(jnp.logical_not(is_ret | is_dqk))
    def _():
        o_ref[...] = project().astype(BF16)


def _inproj(x2, g1, sh, sc, w_bf, qkg, gmat, tables, tm, tiles_per_batch):
    n = x2.shape[0]
    nb = sh.shape[0]
    tab_spec = pl.BlockSpec((tm, LANES), lambda i, j: (i % tiles_per_batch, 0))
    mod_spec = pl.BlockSpec((1, 1, D_MODEL), lambda i, j: (jnp.minimum(i // tiles_per_batch, nb - 1), 0, 0))
    return pl.pallas_call(
        _inproj_kernel,
        out_shape=jax.ShapeDtypeStruct((n, IN_COLS), BF16),
        grid=(n // tm, IN_COLS // COL_TILE),
        in_specs=[pl.BlockSpec((tm, D_MODEL), lambda i, j: (i, 0)),
                  pl.BlockSpec((1, D_MODEL), lambda i, j: (0, 0)),
                  mod_spec, mod_spec,
                  pl.BlockSpec((D_MODEL, COL_TILE), lambda i, j: (0, j)),
                  pl.BlockSpec((8, LANES), lambda i, j: (0, 0)),
                  pl.BlockSpec((2 * LANES, 2 * LANES), lambda i, j: (0, 0))] + [tab_spec] * 7,
        out_specs=pl.BlockSpec((tm, COL_TILE), lambda i, j: (i, j)),
        scratch_shapes=[pltpu.VMEM((tm, D_MODEL), BF16), pltpu.VMEM((tm, COL_TILE), F32)],
        compiler_params=_params(("arbitrary", "arbitrary"), 48),
        name="inproj",
    )(x2, g1, sh, sc, w_bf, qkg, gmat, *tables)


def _rope_tables(seq):
    n_rows = seq // GRID_W
    f32 = np.float32

    def angles(pos, half):
        inv = f32(ROPE_BASE) ** (-np.arange(half, dtype=f32) / f32(half))
        return (pos.astype(f32)[:, None] * inv[None, :]).astype(np.float64)

    ar, ac = angles(np.arange(n_rows), 64), angles(np.arange(GRID_W), 64)
    br, bc = angles(np.arange(n_rows), 16), angles(np.arange(GRID_W), 16)
    zr, zc = np.zeros_like(br), np.zeros_like(bc)
    cat = lambda parts, reps=1: np.tile(np.concatenate(parts, axis=1), (1, reps)).astype(f32)
    by_row = lambda t: jnp.repeat(jnp.asarray(t), GRID_W, axis=0)
    by_col = lambda t: jnp.tile(jnp.asarray(t), (n_rows, 1))
    cr = by_row(cat([np.cos(ar), np.cos(ar)]))
    sr = by_row(cat([-np.sin(ar), np.sin(ar)]))
    cc = by_col(cat([np.cos(ac), np.cos(ac)]))
    sc = by_col(cat([-np.sin(ac), np.sin(ac)]))
    cd = by_row(cat([np.cos(br), np.cos(br), zr, zr], 2)) + by_col(cat([zc, zc, np.cos(bc), np.cos(bc)], 2))
    sa = by_row(cat([zr, np.sin(br), zr, zr], 2)) + by_col(cat([zc, zc, zc, np.sin(bc)], 2))
    sb = by_row(cat([-np.sin(br), zr, zr, zr], 2)) + by_col(cat([zc, zc, -np.sin(bc), zc], 2))
    return [cr, sr, cc, sc, cd, sa, sb]


def _identity_tables(seq):
    one = jnp.ones((seq, LANES), F32)
    zero = jnp.zeros((seq, LANES), F32)
    return [one, zero, one, zero, one, zero, zero]


def _tn_dot(a, b):
    return lax.dot_general(a, b, (((0,), (0,)), ((), ())), preferred_element_type=F32)


def _nt_dot(a, b):
    return lax.dot_general(a, b, (((1,), (1,)), ((), ())), preferred_element_type=F32)


def _ret_kernel(lg_ref, q_ref, k_ref, v_ref, g_ref, ck_ref, cv_ref, gn_ref, o_ref,
                sf_scr, sb_scr, ob_scr, *, chunk, sub, n_blocks, ctx_len):
    h = pl.program_id(1)
    p = pl.program_id(2)
    c = pl.program_id(3)
    lgf = lg_ref[0, h]
    lgb = lg_ref[1, h]

    def col_iota(n):
        return lax.broadcasted_iota(I32, (n, 1), 0).astype(F32)

    def vexp(s):
        return jnp.exp(jnp.zeros((1, 1), F32) + s)

    @pl.when((p == 0) & (c == 0))
    def _():
        jc = col_iota(ctx_len)
        kc = ck_ref[...].astype(F32)
        vc = cv_ref[...]
        sf_scr[...] = _tn_dot((kc * jnp.exp(lgf * (ctx_len - 1.0 - jc))).astype(BF16), vc)
        sb_scr[...] = _tn_dot((kc * jnp.exp(lgb * jc)).astype(BF16), vc)

    ic = col_iota(chunk)

    def full_bf16(col):
        return jnp.broadcast_to(col, (chunk, RET_DK)).astype(BF16)

    @pl.when(p == 0)
    def _():
        q_decay = full_bf16(jnp.exp(lgb * (chunk - ic)))
        k_decay = full_bf16(jnp.exp(lgb * ic))
        s_decay = vexp(lgb * chunk)
        for j in reversed(range(sub)):
            loc = pl.ds(j * chunk, chunk)
            glob = pl.ds(pl.multiple_of(((n_blocks - 1 - c) * sub + j) * chunk, chunk), chunk)
            qb = q_ref[loc, :] * q_decay
            ob_scr[glob, :] = jnp.dot(qb, sb_scr[...].astype(BF16), preferred_element_type=F32)
            kb = k_ref[loc, :] * k_decay
            sb_scr[...] = s_decay * sb_scr[...] + _tn_dot(kb, v_ref[loc, :])

    @pl.when(p == 1)
    def _():
        ri = lax.broadcasted_iota(I32, (chunk, chunk), 0)
        ci = lax.broadcasted_iota(I32, (chunk, chunk), 1)
        d = (ri - ci).astype(F32)
        mask = jnp.where(d > 0, jnp.exp(lgf * jnp.maximum(d, 0.0)),
                         jnp.where(d < 0, jnp.exp(lgb * jnp.maximum(-d, 0.0)), 2.0))
        q_decay = full_bf16(jnp.exp(lgf * (ic + 1.0)))
        k_decay = full_bf16(jnp.exp(lgf * (chunk - 1.0 - ic)))
        s_decay = vexp(lgf * chunk)
        for j in range(sub):
            loc = pl.ds(j * chunk, chunk)
            glob = pl.ds(pl.multiple_of((c * sub + j) * chunk, chunk), chunk)
            a = (_nt_dot(q_ref[loc, :], k_ref[loc, :]) * mask).astype(BF16)
            qf = q_ref[loc, :] * q_decay
            o = (jnp.dot(a, v_ref[loc, :], preferred_element_type=F32)
                 + jnp.dot(qf, sf_scr[...].astype(BF16), preferred_element_type=F32)
                 + ob_scr[glob, :])
            kf = k_ref[loc, :] * k_decay
            sf_scr[...] = s_decay * sf_scr[...] + _tn_dot(kf, v_ref[loc, :])
            ms = jnp.mean(o * o, axis=-1, keepdims=True)
            y = o * lax.rsqrt(ms + EPS) * gn_ref[0]
            gt = g_ref[loc, :]
            o_ref[loc, :] = y.astype(BF16) * (gt * jax.nn.sigmoid(gt))


def _retention(lg, p_lat, p_ctx, ret_norm_g, batch, seq, ctx_len):
    chunk = min(256, seq)
    sub = math.gcd(seq // chunk, RET_SUB)
    blk = chunk * sub
    nb = seq // blk
    kern = functools.partial(_ret_kernel, chunk=chunk, sub=sub, n_blocks=nb, ctx_len=ctx_len)

    def rows(b, p, c):
        return b * nb + jnp.where(p == 0, nb - 1 - c, c)

    def rows_fwd(b, p, c):
        return b * nb + jnp.where(p == 0, 0, c)

    return pl.pallas_call(
        kern,
        out_shape=jax.ShapeDtypeStruct((batch * seq, RET_HEADS * RET_DV), BF16),
        grid=(batch, RET_HEADS, 2, nb),
        in_specs=[pl.BlockSpec(memory_space=pltpu.SMEM),
                  pl.BlockSpec((blk, RET_DK), lambda b, h, p, c: (rows(b, p, c), h)),
                  pl.BlockSpec((blk, RET_DK), lambda b, h, p, c: (rows(b, p, c), 4 + h)),
                  pl.BlockSpec((blk, RET_DV), lambda b, h, p, c: (rows(b, p, c), 4 + h)),
                  pl.BlockSpec((blk, RET_DV), lambda b, h, p, c: (rows_fwd(b, p, c), 8 + h)),
                  pl.BlockSpec((ctx_len, RET_DK), lambda b, h, p, c: (b, 4 + h)),
                  pl.BlockSpec((ctx_len, RET_DV), lambda b, h, p, c: (b, 4 + h)),
                  pl.BlockSpec((1, 1, RET_DV), lambda b, h, p, c: (h, 0, 0))],
        out_specs=pl.BlockSpec((blk, RET_DV), lambda b, h, p, c: (rows_fwd(b, p, c), h)),
        scratch_shapes=[pltpu.VMEM((RET_DK, RET_DV), F32), pltpu.VMEM((RET_DK, RET_DV), F32),
                        pltpu.VMEM((seq, RET_DV), F32)],
        compiler_params=_params(("arbitrary",) * 4, 48),
        name="ret",
    )(lg, p_lat, p_lat, p_lat, p_lat, p_ctx, p_ctx, ret_norm_g.reshape(RET_HEADS, 1, RET_DV))


def _dattn_kernel(lam_ref, q_ref, ck_ref, cv_ref, k_ref, v_ref, gn_ref, gmat_ref, o_ref,
                  qq_scr, kmax_scr, mp_scr, kp_scr, vt_scr, pt_scr, *, tq, tk, n_kv, ta, unroll, ctx_len):
    rows = 2 * tq
    n_all = kp_scr.shape[0]
    qi = pl.program_id(2)

    def sq_norms(t):
        tf = t.astype(F32)
        return jnp.dot((tf * tf).astype(BF16), gmat_ref[...], preferred_element_type=F32)

    def lane_const(n, hot):
        return jnp.where(lax.broadcasted_iota(I32, (n, LANES), 1) < hot, 1.0, 0.0).astype(BF16)

    @pl.when(qi == 0)
    def _():
        kmax_scr[...] = jnp.max(sq_norms(ck_ref[...]), axis=0, keepdims=True)
        kp_scr[:, LANES:2 * LANES] = lane_const(n_all, 3)
        kp_scr[0:ctx_len, 0:LANES] = ck_ref[...]
        vt_scr[:, 0:ctx_len] = cv_ref[...].astype(F32).T.astype(BF16)

        def body(c, carry):
            start = pl.multiple_of(c * tk, tk)
            k = k_ref[pl.ds(start, tk), :]
            kmax_scr[...] = jnp.maximum(kmax_scr[...], jnp.max(sq_norms(k), axis=0, keepdims=True))
            kp_scr[pl.ds(pl.multiple_of(ctx_len + start, LANES), tk), 0:LANES] = k
            vt_scr[:, pl.ds(pl.multiple_of(ctx_len + start, LANES), tk)] = (
                v_ref[pl.ds(start, tk), :].astype(F32).T.astype(BF16))
            return carry

        lax.fori_loop(0, n_kv, body, 0)

    qt = q_ref[...].astype(F32).T
    row = lax.broadcasted_iota(I32, (LANES, tq), 0)
    q1t = jnp.where(row < DIFF_DH, qt, 0.0)
    q2t = jnp.where(row >= DIFF_DH, qt, 0.0)
    qq_scr[0:LANES, 0:tq] = q1t.astype(BF16)
    qq_scr[0:LANES, tq:rows] = q2t.astype(BF16)

    def set_shift(shift):
        neg = -shift
        hi = neg.astype(BF16).astype(F32)
        mid = (neg - hi).astype(BF16).astype(F32)
        lo = neg - hi - mid
        row_r = lax.broadcasted_iota(I32, (LANES, rows), 0)
        pieces = jnp.where(row_r == 0, hi, jnp.where(row_r == 1, mid, jnp.where(row_r == 2, lo, 0.0)))
        qq_scr[LANES:2 * LANES, :] = pieces.astype(BF16)

    kmax = kmax_scr[...]
    b1 = jnp.sqrt(jnp.sum(q1t * q1t, axis=0, keepdims=True) * kmax[:, 0:1]) * SHIFT_SLACK
    b2 = jnp.sqrt(jnp.sum(q2t * q2t, axis=0, keepdims=True) * kmax[:, DIFF_DH:DIFF_DH + 1]) * SHIFT_SLACK
    bound = jnp.concatenate([b1, b2], axis=1)
    set_shift(bound)

    def key_tile(t):
        return pl.ds(pl.multiple_of(t * ta, ta), ta)

    @pl.when(jnp.max(bound) > MAX_SAFE_SHIFT)
    def _():
        mp_scr[...] = jnp.full(mp_scr.shape, NEG_BIG, F32)

        def body(t, carry):
            st = jnp.dot(kp_scr[key_tile(t), 0:LANES], qq_scr[0:LANES, :], preferred_element_type=F32)
            mp_scr[...] = jnp.maximum(mp_scr[...], jnp.max(st, axis=0, keepdims=True))
            return carry

        lax.fori_loop(0, n_all // ta, body, 0)
        set_shift(mp_scr[0:1, :])

    mp_scr[...] = jnp.zeros(mp_scr.shape, F32)

    def body_a(t, carry):
        for u in range(unroll):
            r = key_tile(t * unroll + u)
            p = jnp.exp2(jnp.dot(kp_scr[r, :], qq_scr[...], preferred_element_type=F32))
            pt_scr[r, :] = p.astype(BF16)
            mp_scr[...] += jnp.sum(p.reshape(ta // SUBLANES, SUBLANES, rows), axis=0)
        return carry

    lax.fori_loop(0, n_all // (ta * unroll), body_a, 0)

    acc = jnp.dot(vt_scr[...], pt_scr[...], preferred_element_type=F32)
    ot = acc / jnp.sum(mp_scr[...], axis=0, keepdims=True)
    d = (ot[:, 0:tq] - lam_ref[0] * ot[:, tq:rows]).T
    ms = jnp.mean(d * d, axis=-1, keepdims=True)
    y = d * lax.rsqrt(ms + EPS) * gn_ref[...] * (1.0 - LAMBDA_INIT)
    o_ref[...] = y.astype(BF16)


def _diff_attention(lam, p_lat, p_ctx, diff_norm_g, gmat, batch, seq, ctx_len):
    tq = min(1024, seq)
    tk = min(512, seq)
    nq, nk = seq // tq, seq // tk
    n_all = ctx_len + seq
    ta = next(t for t in (528, 512, 384, 320, 256, 128) if n_all % t == 0)
    unroll = math.gcd(n_all // ta, KV_UNROLL)
    assert ctx_len % LANES == 0
    kern = functools.partial(_dattn_kernel, tq=tq, tk=tk, n_kv=nk, ta=ta, unroll=unroll, ctx_len=ctx_len)
    return pl.pallas_call(
        kern,
        out_shape=jax.ShapeDtypeStruct((batch * seq, DIFF_HEADS * DIFF_DV), BF16),
        grid=(batch, DIFF_HEADS, nq),
        in_specs=[pl.BlockSpec(memory_space=pltpu.SMEM),
                  pl.BlockSpec((tq, LANES), lambda b, h, qi: (b * nq + qi, 48 + h)),
                  pl.BlockSpec((ctx_len, LANES), lambda b, h, qi: (b, 56 + h)),
                  pl.BlockSpec((ctx_len, LANES), lambda b, h, qi: (b, 64 + h)),
                  pl.BlockSpec((seq, LANES), lambda b, h, qi: (b, 56 + h)),
                  pl.BlockSpec((seq, LANES), lambda b, h, qi: (b, 64 + h)),
                  pl.BlockSpec((1, LANES), lambda b, h, qi: (0, 0)),
                  pl.BlockSpec((LANES, LANES), lambda b, h, qi: (0, 0))],
        out_specs=pl.BlockSpec((tq, LANES), lambda b, h, qi: (b * nq + qi, h)),
        scratch_shapes=[pltpu.VMEM((2 * LANES, 2 * tq), BF16), pltpu.VMEM((1, LANES), F32),
                        pltpu.VMEM((SUBLANES, 2 * tq), F32), pltpu.VMEM((n_all, 2 * LANES), BF16),
                        pltpu.VMEM((DIFF_DV, n_all), BF16), pltpu.VMEM((n_all, 2 * tq), BF16)],
        compiler_params=_params(("arbitrary",) * 3, 58),
        name="dattn",
    )(lam, p_lat, p_ctx, p_ctx, p_lat, p_lat, diff_norm_g.reshape(1, DIFF_DV), gmat)


def _merge_kernel(yr_ref, yd_ref, ga_ref, gb_ref, x_ref, gatea_ref, shf_ref, scf_ref, n2_ref,
                  wr_ref, wd_ref, wo_ref, rwh_ref, rwl_ref, rb_ref,
                  xn_ref, hp_ref, meta_ref, gate_ref, cnt_ref, run_scr, *, tm):
    i = pl.program_id(0)

    @pl.when(i == 0)
    def _():
        run_scr[...] = jnp.zeros(run_scr.shape, F32)

    yr = jnp.dot(yr_ref[...], wr_ref[...], preferred_element_type=F32)
    yd = jnp.dot(yd_ref[...], wd_ref[...], preferred_element_type=F32)
    m = (jax.nn.sigmoid(ga_ref[...].astype(F32)) * yr + jax.nn.sigmoid(gb_ref[...].astype(F32)) * yd)
    z = jnp.dot(m.astype(BF16), wo_ref[...], preferred_element_type=F32)
    xn = x_ref[...] + gatea_ref[0] * z
    xn_ref[...] = xn

    ms = jnp.mean(xn * xn, axis=-1, keepdims=True)
    h2 = xn * lax.rsqrt(ms + EPS) * n2_ref[...]
    h2 = h2 * (1.0 + scf_ref[0]) + shf_ref[0]
    h_hi = h2.astype(BF16)
    bits = pltpu.bitcast(h_hi.astype(F32), U32)
    half = D_MODEL // 2
    hp_ref[...] = (bits[:, :half] >> 16) | (bits[:, half:] & jnp.uint32(0xFFFF0000))

    h_lo = (h2 - h_hi.astype(F32)).astype(BF16)
    logits = (jnp.dot(h_hi, rwh_ref[...], preferred_element_type=F32)
              + jnp.dot(h_lo, rwh_ref[...], preferred_element_type=F32)
              + jnp.dot(h_hi, rwl_ref[...], preferred_element_type=F32) + rb_ref[...])
    lane = lax.broadcasted_iota(I32, (tm, LANES), 1)
    lanef = lane.astype(F32)
    work = logits
    vals, idxs = [], []
    for _ in range(TOP_K):
        mk = jnp.max(work, axis=1, keepdims=True)
        ik = jnp.min(jnp.where(work == mk, lanef, float(LANES)), axis=1, keepdims=True)
        vals.append(mk)
        idxs.append(ik)
        work = jnp.where(lanef == ik, -jnp.inf, work)
    ex = [jnp.exp(v - vals[0]) for v in vals]
    den = ex[0] + ex[1] + ex[2] + ex[3]

    onehot = jnp.zeros((tm, LANES), F32)
    for ik in idxs:
        onehot = onehot + jnp.where(lanef == ik, 1.0, 0.0)
    ri = lax.broadcasted_iota(I32, (tm, tm), 0)
    ci = lax.broadcasted_iota(I32, (tm, tm), 1)
    tri = jnp.where(ri > ci, 1.0, 0.0).astype(BF16)
    base = run_scr[0:1, :] + jnp.dot(tri, onehot.astype(BF16), preferred_element_type=F32)
    run_scr[...] = run_scr[...] + jnp.sum(onehot, axis=0, keepdims=True)

    meta = jnp.zeros((tm, LANES), F32)
    gate_out = jnp.zeros((tm, LANES), F32)
    for k in range(TOP_K):
        rk = jnp.sum(jnp.where(lanef == idxs[k], base, 0.0), axis=1, keepdims=True)
        meta = jnp.where(lane == k, idxs[k], jnp.where(lane == TOP_K + k, rk, meta))
        gate_out = jnp.where(lane == k, ex[k] / den, gate_out)
    meta_ref[...] = meta.T[0:2 * TOP_K, :]
    gate_ref[...] = gate_out
    cnt_ref[...] = run_scr[...]


def _merge(y_ret, y_diff, p_lat, x2, g_a, sh_f, sc_f, norm2_g, w_r, w_d, w_o, rw_hi, rw_lo, rb, seq):
    n = x2.shape[0]
    tm = min(512, seq)
    tpb = seq // tm
    kern = functools.partial(_merge_kernel, tm=tm)
    mod_spec = pl.BlockSpec((1, 1, D_MODEL), lambda i: (i // tpb, 0, 0))
    const = lambda shape: pl.BlockSpec(shape, lambda i: (0,) * len(shape))
    tok = lambda w: pl.BlockSpec((tm, w), lambda i: (i, 0))
    return pl.pallas_call(
        kern,
        out_shape=(jax.ShapeDtypeStruct((n, D_MODEL), F32),
                   jax.ShapeDtypeStruct((n, D_MODEL // 2), U32),
                   jax.ShapeDtypeStruct((2 * TOP_K, n), F32),
                   jax.ShapeDtypeStruct((n, LANES), F32),
                   jax.ShapeDtypeStruct((8, LANES), F32)),
        grid=(n // tm,),
        in_specs=[tok(RET_HEADS * RET_DV), tok(D_MODEL),
                  pl.BlockSpec((tm, COL_TILE), lambda i: (i, 9)),
                  pl.BlockSpec((tm, COL_TILE), lambda i: (i, 10)),
                  tok(D_MODEL), mod_spec, mod_spec, mod_spec, const((1, D_MODEL)),
                  const((RET_HEADS * RET_DV, D_MODEL)), const((D_MODEL, D_MODEL)), const((D_MODEL, D_MODEL)),
                  const((D_MODEL, LANES)), const((D_MODEL, LANES)), const((1, LANES))],
        out_specs=(tok(D_MODEL), tok(D_MODEL // 2), pl.BlockSpec((2 * TOP_K, tm), lambda i: (0, i)),
                   tok(LANES), const((8, LANES))),
        scratch_shapes=[pltpu.VMEM((8, LANES), F32)],
        compiler_params=_params(("arbitrary",), 56),
        name="merge",
    )(y_ret, y_diff, p_lat, p_lat, x2, g_a, sh_f, sc_f, norm2_g, w_r, w_d, w_o, rw_hi, rw_lo, rb)


def _dispatch_kernel(dest_ref, fill_ref, nb_ref, hp_ref, xb_ref, zero_scr, sem, zsem, *, tm, n_tok, n_blocks):
    i = pl.program_id(0)

    @pl.when(i == 0)
    def _():
        zero_scr[...] = jnp.zeros(zero_scr.shape, U32)
        fill = zero_scr.shape[0]
        for e in range(N_EXPERTS):
            start = pl.multiple_of(fill_ref[e], SUBLANES)
            pltpu.make_async_copy(zero_scr, xb_ref.at[pl.ds(start, fill), :], zsem).start()
        for e in range(N_EXPERTS):
            pltpu.make_async_copy(zero_scr, xb_ref.at[pl.ds(0, fill), :], zsem).wait()
        blk = zero_scr.at[pl.ds(0, MOE_BLK), :]

        def start_blk(b, carry):
            pltpu.make_async_copy(blk, xb_ref.at[pl.ds(pl.multiple_of(b * MOE_BLK, MOE_BLK), MOE_BLK), :], zsem).start()
            return carry

        def wait_blk(b, carry):
            pltpu.make_async_copy(blk, xb_ref.at[pl.ds(0, MOE_BLK), :], zsem).wait()
            return carry

        lax.fori_loop(nb_ref[0], n_blocks, start_blk, 0)
        lax.fori_loop(nb_ref[0], n_blocks, wait_blk, 0)

    def body(g, carry):
        r0 = pl.multiple_of(g * SUBLANES, SUBLANES)
        for j in range(SUBLANES):
            for k in range(TOP_K):
                d = dest_ref[k * n_tok + i * tm + r0 + j]
                pltpu.make_async_copy(hp_ref.at[pl.ds(r0 + j, 1), :], xb_ref.at[pl.ds(d, 1), :], sem).start()
        return carry

    lax.fori_loop(0, tm // SUBLANES, body, 0)
    for _ in range(TOP_K):
        pltpu.make_async_copy(hp_ref, xb_ref.at[pl.ds(0, tm), :], sem).wait()


def _dispatch(dest, fill_start, n_used, hp, n_rows):
    n = hp.shape[0]
    tm = min(512, n)
    kern = functools.partial(_dispatch_kernel, tm=tm, n_tok=n, n_blocks=n_rows // MOE_BLK)
    return pl.pallas_call(
        kern,
        out_shape=jax.ShapeDtypeStruct((n_rows, D_MODEL // 2), U32),
        grid_spec=pltpu.PrefetchScalarGridSpec(
            num_scalar_prefetch=3,
            grid=(n // tm,),
            in_specs=[pl.BlockSpec((tm, D_MODEL // 2), lambda i, d, f, nb: (i, 0))],
            out_specs=pl.BlockSpec(memory_space=pl.ANY),
            scratch_shapes=[pltpu.VMEM((MOE_BLK + SUBLANES, D_MODEL // 2), U32), pltpu.SemaphoreType.DMA(()),
                            pltpu.SemaphoreType.DMA(())]),
        compiler_params=_params(("arbitrary",), 32),
        name="dispatch",
    )(dest, fill_start, n_used, hp)


def _expert_kernel(be_ref, nb_ref, nxt_ref, xb_ref, w1_hbm, b1_ref, w2_hbm, b2_ref, y_ref,
                   w1f_scr, w2f_scr, w1b_scr, w2b_scr, x_scr, grp_scr, sems):
    i = pl.program_id(0)
    e = be_ref[i]
    prev = be_ref[jnp.maximum(i - 1, 0)]
    active = i < nb_ref[0]

    def weight_copies(expert, slot):
        return (pltpu.make_async_copy(w1_hbm.at[expert], w1f_scr.at[slot], sems.at[0, slot]),
                pltpu.make_async_copy(w2_hbm.at[expert], w2f_scr.at[slot], sems.at[1, slot]))

    @pl.when(i == 0)
    def _():
        grp_scr[0] = 0
        for cp in weight_copies(e, 0):
            cp.start()

    @pl.when(active & (i > 0) & (e != prev))
    def _():
        grp_scr[0] = grp_scr[0] + 1

    @pl.when(active & ((i == 0) | (e != prev)))
    def _():
        slot = grp_scr[0] % 2
        for cp in weight_copies(e, slot):
            cp.wait()
        w1b_scr[...] = w1f_scr[slot].astype(BF16)
        w2b_scr[...] = w2f_scr[slot].astype(BF16)
        nxt = nxt_ref[e]

        @pl.when(nxt >= 0)
        def _():
            for cp in weight_copies(nxt, 1 - slot):
                cp.start()

    @pl.when(active)
    def _():
        xu = xb_ref[...]
        half = D_MODEL // 2
        x_scr[:, 0:half] = pltpu.bitcast(xu << 16, F32).astype(BF16)
        x_scr[:, half:] = pltpu.bitcast(xu & jnp.uint32(0xFFFF0000), F32).astype(BF16)
        hh = jnp.dot(x_scr[...], w1b_scr[...], preferred_element_type=F32) + b1_ref[0]
        glu = jnp.minimum(hh[:, :D_FF], SWIGLU_LIMIT)
        lin = jnp.clip(hh[:, D_FF:], -SWIGLU_LIMIT, SWIGLU_LIMIT)
        act = glu * jax.nn.sigmoid(SWIGLU_ALPHA * glu) * (lin + 1.0)
        y_ref[...] = jnp.dot(act.astype(BF16), w2b_scr[...], preferred_element_type=F32) + b2_ref[0]

    @pl.when(i >= nb_ref[0])
    def _():
        y_ref[...] = jnp.zeros(y_ref.shape, F32)


def _experts(block_e, n_used, next_e, xb, w1, b1, w2, b2):
    n_blocks = block_e.shape[0]
    rows = n_blocks * MOE_BLK
    return pl.pallas_call(
        _expert_kernel,
        out_shape=jax.ShapeDtypeStruct((rows, D_MODEL), F32),
        grid_spec=pltpu.PrefetchScalarGridSpec(
            num_scalar_prefetch=3,
            grid=(n_blocks,),
            in_specs=[pl.BlockSpec((MOE_BLK, D_MODEL // 2), lambda i, be, nb, nx: (jnp.minimum(i, nb[0] - 1), 0)),
                      pl.BlockSpec(memory_space=pl.ANY),
                      pl.BlockSpec((1, 1, 2 * D_FF), lambda i, be, nb, nx: (be[i], 0, 0)),
                      pl.BlockSpec(memory_space=pl.ANY),
                      pl.BlockSpec((1, 1, D_MODEL), lambda i, be, nb, nx: (be[i], 0, 0))],
            out_specs=pl.BlockSpec((MOE_BLK, D_MODEL), lambda i, be, nb, nx: (i, 0)),
            scratch_shapes=[pltpu.VMEM((2, D_MODEL, 2 * D_FF), F32), pltpu.VMEM((2, D_FF, D_MODEL), F32),
                            pltpu.VMEM((D_MODEL, 2 * D_FF), BF16), pltpu.VMEM((D_FF, D_MODEL), BF16),
                            pltpu.VMEM((MOE_BLK, D_MODEL), BF16),
                            pltpu.SMEM((1,), I32), pltpu.SemaphoreType.DMA((2, 2))]),
        compiler_params=_params(("arbitrary",), 56),
        name="expert",
    )(block_e, n_used, next_e, xb, w1, b1.reshape(N_EXPERTS, 1, 2 * D_FF), w2, b2.reshape(N_EXPERTS, 1, D_MODEL))


def _combine_kernel(dest_ref, yb_ref, gate_ref, xn_ref, gf_ref, o_ref, buf, sem, *, tm, n_tok):
    i = pl.program_id(0)

    def body(g, carry):
        r0 = pl.multiple_of(g * SUBLANES, SUBLANES)
        for j in range(SUBLANES):
            for k in range(TOP_K):
                d = dest_ref[k * n_tok + i * tm + r0 + j]
                pltpu.make_async_copy(yb_ref.at[pl.ds(d, 1), :], buf.at[k, pl.ds(r0 + j, 1), :], sem).start()
        return carry

    lax.fori_loop(0, tm // SUBLANES, body, 0)
    for k in range(TOP_K):
        pltpu.make_async_copy(yb_ref.at[pl.ds(0, tm), :], buf.at[k], sem).wait()
    g = gate_ref[...]
    y = g[:, 0:1] * buf[0]
    for k in range(1, TOP_K):
        y = y + g[:, k:k + 1] * buf[k]
    o_ref[...] = xn_ref[...] + gf_ref[0] * y


def _combine(dest, yb, gate, xn, g_f, seq):
    n = xn.shape[0]
    tm = min(512, seq)
    tpb = seq // tm
    kern = functools.partial(_combine_kernel, tm=tm, n_tok=n)
    return pl.pallas_call(
        kern,
        out_shape=jax.ShapeDtypeStruct((n, D_MODEL), F32),
        grid_spec=pltpu.PrefetchScalarGridSpec(
            num_scalar_prefetch=1,
            grid=(n // tm,),
            in_specs=[pl.BlockSpec(memory_space=pl.ANY),
                      pl.BlockSpec((tm, LANES), lambda i, d: (i, 0)),
                      pl.BlockSpec((tm, D_MODEL), lambda i, d: (i, 0)),
                      pl.BlockSpec((1, 1, D_MODEL), lambda i, d: (i // tpb, 0, 0))],
            out_specs=pl.BlockSpec((tm, D_MODEL), lambda i, d: (i, 0)),
            scratch_shapes=[pltpu.VMEM((TOP_K, tm, D_MODEL), F32), pltpu.SemaphoreType.DMA(())]),
        compiler_params=_params(("arbitrary",), 32),
        name="combine",
    )(dest, yb, gate, xn, g_f)


def _layer(x, ctx, c, c_ctx, norm1_g, norm2_g, w_mod, b_mod, w_in, ret_decay_logit, ret_norm_g,
           diff_q_norm_g, diff_k_norm_g, diff_lambda, diff_norm_g, w_br_ret, w_br_diff, w_out,
           router_w, router_b, exp_w1, exp_b1, exp_w2, exp_b2):
    batch, seq, d = x.shape
    ctx_len = ctx.shape[1]
    assert d == D_MODEL and seq % GRID_W == 0 and batch + 1 <= 8
    n_tok = batch * seq

    cc = jnp.zeros((8, D_MODEL), F32).at[:batch].set(c).at[batch].set(c_ctx)
    mod = _mod(cc, w_mod, b_mod)
    sh_a, sc_a, g_a, sh_f, sc_f, g_f = [mod[:batch, i * D_MODEL:(i + 1) * D_MODEL].reshape(batch, 1, D_MODEL)
                                         for i in range(6)]
    csh_a = mod[batch:batch + 1, 0:D_MODEL].reshape(1, 1, D_MODEL)
    csc_a = mod[batch:batch + 1, D_MODEL:2 * D_MODEL].reshape(1, 1, D_MODEL)

    w_in_bf = w_in.astype(BF16)
    g1 = norm1_g.reshape(1, D_MODEL)
    tile = lambda g: jnp.tile(g.astype(F32), 2)
    qkg = jnp.zeros((8, LANES), F32).at[0].set(tile(diff_q_norm_g) * (DIFF_DH ** -0.5 * LOG2E)).at[1].set(tile(diff_k_norm_g))
    lane = jnp.arange(2 * LANES)
    same_group = lane[:, None] // DIFF_DH == lane[None, :] // DIFF_DH
    gmean = jnp.where(same_group, 1.0 / DIFF_DH, 0.0).astype(BF16)
    gmat = same_group[:LANES, :LANES].astype(BF16)
    x2 = x.reshape(n_tok, D_MODEL)
    tm = min(1024, seq)
    p_lat = _inproj(x2, g1, sh_a, sc_a, w_in_bf, qkg, gmean, _rope_tables(seq), tm, seq // tm)
    p_ctx = _inproj(ctx.reshape(batch * ctx_len, D_MODEL), g1, csh_a, csc_a, w_in_bf, qkg, gmean,
                    _identity_tables(ctx_len), ctx_len, 1)

    lg = jax.nn.log_sigmoid(ret_decay_logit.astype(F32))
    y_ret = _retention(lg, p_lat, p_ctx, ret_norm_g, batch, seq, ctx_len)

    lp = diff_lambda.astype(F32)
    lam = (jnp.exp(jnp.sum(lp[0] * lp[1])) - jnp.exp(jnp.sum(lp[2] * lp[3])) + LAMBDA_INIT).reshape(1)
    y_diff = _diff_attention(lam, p_lat, p_ctx, diff_norm_g, gmat, batch, seq, ctx_len)

    rw = jnp.zeros((D_MODEL, LANES), F32).at[:, :N_EXPERTS].set(router_w)
    rw_hi = rw.astype(BF16)
    rw_lo = (rw - rw_hi.astype(F32)).astype(BF16)
    rb = jnp.full((1, LANES), NEG_BIG, F32).at[0, :N_EXPERTS].set(router_b)
    xn, hp, meta, gate4, cnt = _merge(
        y_ret, y_diff, p_lat, x2, g_a, sh_f, sc_f, norm2_g.reshape(1, D_MODEL),
        w_br_ret.astype(BF16), w_br_diff.astype(BF16), w_out.astype(BF16), rw_hi, rw_lo, rb, seq)

    counts = cnt[0, :N_EXPERTS].astype(I32)
    padded = (counts + MOE_BLK - 1) // MOE_BLK * MOE_BLK
    pad_end = jnp.cumsum(padded)
    pad_start = pad_end - padded
    n_pairs = n_tok * TOP_K
    n_blocks = n_pairs // MOE_BLK + N_EXPERTS
    meta_i = meta.astype(I32)
    is_e = meta_i[None, :TOP_K] == jnp.arange(N_EXPERTS, dtype=I32)[:, None, None]
    dest = (jnp.sum(jnp.where(is_e, pad_start[:, None, None], 0), axis=0) + meta_i[TOP_K:]).reshape(n_pairs)
    block_start = jnp.arange(n_blocks, dtype=I32) * MOE_BLK
    block_e = jnp.minimum(jnp.sum((pad_end[None, :] <= block_start[:, None]).astype(I32), axis=1), N_EXPERTS - 1)
    n_used = (pad_end[-1] // MOE_BLK).reshape(1).astype(I32)

    fill_start = ((pad_start + counts) // SUBLANES * SUBLANES).astype(I32)
    xb = _dispatch(dest, fill_start, n_used, hp, (n_blocks + 1) * MOE_BLK)
    e_ids = jnp.arange(N_EXPERTS, dtype=I32)
    later = jnp.where((counts[None, :] > 0) & (e_ids[None, :] > e_ids[:, None]), e_ids[None, :], N_EXPERTS)
    next_e = jnp.min(later, axis=1)
    next_e = jnp.where(next_e == N_EXPERTS, -1, next_e).astype(I32)
    yb = _experts(block_e, n_used, next_e, xb, exp_w1, exp_b1, exp_w2, exp_b2)
    out = _combine(dest, yb, gate4, xn, g_f, seq)
    return out.reshape(batch, seq, D_MODEL)


def kernel(x, c, ctx, c_ctx, norm1_g, norm2_g, w_mod, b_mod, w_in, ret_decay_logit, ret_norm_g, diff_q_norm_g, diff_k_norm_g, diff_lambda, diff_norm_g, w_br_ret, w_br_diff, w_out, router_w, router_b, exp_w1, exp_b1, exp_w2, exp_b2):
    assert norm1_g.shape[0] == 1, "single-layer block"
    return _layer(x, ctx, c, c_ctx, norm1_g[0], norm2_g[0], w_mod[0], b_mod[0], w_in[0], ret_decay_logit[0],
                  ret_norm_g[0], diff_q_norm_g[0], diff_k_norm_g[0], diff_lambda[0], diff_norm_g[0],
                  w_br_ret[0], w_br_diff[0], w_out[0], router_w[0], router_b[0],
                  exp_w1[0], exp_b1[0], exp_w2[0], exp_b2[0])
```

```python
import functools
import math

import numpy as np

import jax
import jax.numpy as jnp
from jax import lax
from jax.experimental import pallas as pl
from jax.experimental.pallas import tpu as pltpu

F32 = jnp.float32
BF16 = jnp.bfloat16
U32 = jnp.uint32
I32 = jnp.int32

D_MODEL = 1024
GRID_W = 64
RET_HEADS = 4
RET_DK = 256
RET_DV = 512
DIFF_DH = 64
DIFF_HEADS = 8
DIFF_DV = 128
N_EXPERTS = 32
TOP_K = 4
D_FF = 1024
SWIGLU_LIMIT = 7.0
SWIGLU_ALPHA = 1.702
ROPE_BASE = 10000.0
EPS = 1e-6
LAMBDA_INIT = 0.8 - 0.6 * math.exp(-0.3 * 0)

IN_COLS = 11264
COL_TILE = 1024
LANES = 128
SUBLANES = 8
MOE_BLK = 256
NEG_BIG = -1e30
LOG2E = 1.4426950408889634
SHIFT_SLACK = 1.0 + 2.0 ** -6
MAX_SAFE_SHIFT = 60.0
RET_SUB = 8
KV_UNROLL = 4
HIGHEST = lax.Precision.HIGHEST
MIB = 1024 * 1024


def _params(sem, vmem_mib):
    return pltpu.CompilerParams(dimension_semantics=sem, vmem_limit_bytes=vmem_mib * MIB)


def _mod_kernel(c_ref, w_ref, b_ref, o_ref):
    c = c_ref[...]
    s = c * jax.nn.sigmoid(c)
    o_ref[...] = jnp.dot(s, w_ref[...], preferred_element_type=F32, precision=HIGHEST) + b_ref[...]


def _mod(cc, w_mod, b_mod):
    n = w_mod.shape[1]
    tn = 1024
    return pl.pallas_call(
        _mod_kernel,
        out_shape=jax.ShapeDtypeStruct((8, n), F32),
        grid=(n // tn,),
        in_specs=[pl.BlockSpec((8, D_MODEL), lambda j: (0, 0)),
                  pl.BlockSpec((D_MODEL, tn), lambda j: (0, j)),
                  pl.BlockSpec((1, tn), lambda j: (0, j))],
        out_specs=pl.BlockSpec((8, tn), lambda j: (0, j)),
        compiler_params=_params(("arbitrary",), 32),
        name="mod",
    )(cc, w_mod, b_mod.reshape(1, n))


def _inproj_kernel(x_ref, g_ref, sh_ref, sc_ref, w_ref, qkg_ref, gmat_ref,
                   cr_ref, sr_ref, cc_ref, sc2_ref, cd_ref, sa_ref, sb_ref,
                   o_ref, h_scr, acc_scr):
    j = pl.program_id(1)

    @pl.when(j == 0)
    def _():
        xf = x_ref[...]
        ms = jnp.mean(xf * xf, axis=-1, keepdims=True)
        y = xf * lax.rsqrt(ms + EPS) * g_ref[...]
        h_scr[...] = (y * (1.0 + sc_ref[0]) + sh_ref[0]).astype(BF16)

    is_ret = j <= 1
    is_dqk = (j == 6) | (j == 7)

    def project():
        return jnp.dot(h_scr[...], w_ref[...], preferred_element_type=F32)

    @pl.when(is_ret)
    def _():
        acc_scr[...] = project()
        scale = jnp.where(j == 0, RET_DK ** -0.5, 1.0).astype(F32)
        for b in range(COL_TILE // LANES):
            xb = acc_scr[:, b * LANES:(b + 1) * LANES]
            cos = cr_ref[...] if b % 2 == 0 else cc_ref[...]
            sin = sr_ref[...] if b % 2 == 0 else sc2_ref[...]
            o = (xb * cos + pltpu.roll(xb, 64, 1) * sin) * scale
            o_ref[:, b * LANES:(b + 1) * LANES] = o.astype(BF16)

    @pl.when(is_dqk)
    def _():
        acc_scr[...] = project()
        g = jnp.where(j == 6, qkg_ref[0:1, :], qkg_ref[1:2, :])
        for b2 in range(COL_TILE // (2 * LANES)):
            x2 = acc_scr[:, b2 * 2 * LANES:(b2 + 1) * 2 * LANES]
            ms2 = jnp.dot((x2 * x2).astype(BF16), gmat_ref[...], preferred_element_type=F32)
            for half in range(2):
                b = 2 * b2 + half
                xb = x2[:, half * LANES:(half + 1) * LANES]
                yn = xb * lax.rsqrt(ms2[:, half * LANES:(half + 1) * LANES] + EPS) * g
                o = yn * cd_ref[...] + pltpu.roll(yn, 16, 1) * sa_ref[...] + pltpu.roll(yn, 112, 1) * sb_ref[...]
                o_ref[:, b * LANES:(b + 1) * LANES] = o.astype(BF16)

    @pl.when(jnp.logical_not(is_ret | is_dqk))
    def _():
        o_ref[...] = project().astype(BF16)


def _inproj(x2, g1, sh, sc, w_bf, qkg, gmat, tables, tm, tiles_per_batch):
    n = x2.shape[0]
    nb = sh.shape[0]
    tab_spec = pl.BlockSpec((tm, LANES), lambda i, j: (i % tiles_per_batch, 0))
    mod_spec = pl.BlockSpec((1, 1, D_MODEL), lambda i, j: (jnp.minimum(i // tiles_per_batch, nb - 1), 0, 0))
    return pl.pallas_call(
        _inproj_kernel,
        out_shape=jax.ShapeDtypeStruct((n, IN_COLS), BF16),
        grid=(n // tm, IN_COLS // COL_TILE),
        in_specs=[pl.BlockSpec((tm, D_MODEL), lambda i, j: (i, 0)),
                  pl.BlockSpec((1, D_MODEL), lambda i, j: (0, 0)),
                  mod_spec, mod_spec,
                  pl.BlockSpec((D_MODEL, COL_TILE), lambda i, j: (0, j)),
                  pl.BlockSpec((8, LANES), lambda i, j: (0, 0)),
                  pl.BlockSpec((2 * LANES, 2 * LANES), lambda i, j: (0, 0))] + [tab_spec] * 7,
        out_specs=pl.BlockSpec((tm, COL_TILE), lambda i, j: (i, j)),
        scratch_shapes=[pltpu.VMEM((tm, D_MODEL), BF16), pltpu.VMEM((tm, COL_TILE), F32)],
        compiler_params=_params(("arbitrary", "arbitrary"), 48),
        name="inproj",
    )(x2, g1, sh, sc, w_bf, qkg, gmat, *tables)


def _rope_tables(seq):
    n_rows = seq // GRID_W
    f32 = np.float32

    def angles(pos, half):
        inv = f32(ROPE_BASE) ** (-np.arange(half, dtype=f32) / f32(half))
        return (pos.astype(f32)[:, None] * inv[None, :]).astype(np.float64)

    ar, ac = angles(np.arange(n_rows), 64), angles(np.arange(GRID_W), 64)
    br, bc = angles(np.arange(n_rows), 16), angles(np.arange(GRID_W), 16)
    zr, zc = np.zeros_like(br), np.zeros_like(bc)
    cat = lambda parts, reps=1: np.tile(np.concatenate(parts, axis=1), (1, reps)).astype(f32)
    by_row = lambda t: jnp.repeat(jnp.asarray(t), GRID_W, axis=0)
    by_col = lambda t: jnp.tile(jnp.asarray(t), (n_rows, 1))
    cr = by_row(cat([np.cos(ar), np.cos(ar)]))
    sr = by_row(cat([-np.sin(ar), np.sin(ar)]))
    cc = by_col(cat([np.cos(ac), np.cos(ac)]))
    sc = by_col(cat([-np.sin(ac), np.sin(ac)]))
    cd = by_row(cat([np.cos(br), np.cos(br), zr, zr], 2)) + by_col(cat([zc, zc, np.cos(bc), np.cos(bc)], 2))
    sa = by_row(cat([zr, np.sin(br), zr, zr], 2)) + by_col(cat([zc, zc, zc, np.sin(bc)], 2))
    sb = by_row(cat([-np.sin(br), zr, zr, zr], 2)) + by_col(cat([zc, zc, -np.sin(bc), zc], 2))
    return [cr, sr, cc, sc, cd, sa, sb]


def _identity_tables(seq):
    one = jnp.ones((seq, LANES), F32)
    zero = jnp.zeros((seq, LANES), F32)
    return [one, zero, one, zero, one, zero, zero]


def _tn_dot(a, b):
    return lax.dot_general(a, b, (((0,), (0,)), ((), ())), preferred_element_type=F32)


def _nt_dot(a, b):
    return lax.dot_general(a, b, (((1,), (1,)), ((), ())), preferred_element_type=F32)


def _ret_kernel(lg_ref, q_ref, k_ref, v_ref, g_ref, ck_ref, cv_ref, gn_ref, o_ref,
                sf_scr, sb_scr, ob_scr, *, chunk, sub, n_blocks, ctx_len):
    h = pl.program_id(1)
    p = pl.program_id(2)
    c = pl.program_id(3)
    lgf = lg_ref[0, h]
    lgb = lg_ref[1, h]

    def col_iota(n):
        return lax.broadcasted_iota(I32, (n, 1), 0).astype(F32)

    def vexp(s):
        return jnp.exp(jnp.zeros((1, 1), F32) + s)

    @pl.when((p == 0) & (c == 0))
    def _():
        jc = col_iota(ctx_len)
        kc = ck_ref[...].astype(F32)
        vc = cv_ref[...]
        sf_scr[...] = _tn_dot((kc * jnp.exp(lgf * (ctx_len - 1.0 - jc))).astype(BF16), vc)
        sb_scr[...] = _tn_dot((kc * jnp.exp(lgb * jc)).astype(BF16), vc)

    ic = col_iota(chunk)

    def full_bf16(col):
        return jnp.broadcast_to(col, (chunk, RET_DK)).astype(BF16)

    @pl.when(p == 0)
    def _():
        q_decay = full_bf16(jnp.exp(lgb * (chunk - ic)))
        k_decay = full_bf16(jnp.exp(lgb * ic))
        s_decay = vexp(lgb * chunk)
        for j in reversed(range(sub)):
            loc = pl.ds(j * chunk, chunk)
            glob = pl.ds(pl.multiple_of(((n_blocks - 1 - c) * sub + j) * chunk, chunk), chunk)
            qb = q_ref[loc, :] * q_decay
            ob_scr[glob, :] = jnp.dot(qb, sb_scr[...].astype(BF16), preferred_element_type=F32)
            kb = k_ref[loc, :] * k_decay
            sb_scr[...] = s_decay * sb_scr[...] + _tn_dot(kb, v_ref[loc, :])

    @pl.when(p == 1)
    def _():
        ri = lax.broadcasted_iota(I32, (chunk, chunk), 0)
        ci = lax.broadcasted_iota(I32, (chunk, chunk), 1)
        d = (ri - ci).astype(F32)
        mask = jnp.where(d > 0, jnp.exp(lgf * jnp.maximum(d, 0.0)),
                         jnp.where(d < 0, jnp.exp(lgb * jnp.maximum(-d, 0.0)), 2.0))
        q_decay = full_bf16(jnp.exp(lgf * (ic + 1.0)))
        k_decay = full_bf16(jnp.exp(lgf * (chunk - 1.0 - ic)))
        s_decay = vexp(lgf * chunk)
        for j in range(sub):
            loc = pl.ds(j * chunk, chunk)
            glob = pl.ds(pl.multiple_of((c * sub + j) * chunk, chunk), chunk)
            a = (_nt_dot(q_ref[loc, :], k_ref[loc, :]) * mask).astype(BF16)
            qf = q_ref[loc, :] * q_decay
            o = (jnp.dot(a, v_ref[loc, :], preferred_element_type=F32)
                 + jnp.dot(qf, sf_scr[...].astype(BF16), preferred_element_type=F32)
                 + ob_scr[glob, :])
            kf = k_ref[loc, :] * k_decay
            sf_scr[...] = s_decay * sf_scr[...] + _tn_dot(kf, v_ref[loc, :])
            ms = jnp.mean(o * o, axis=-1, keepdims=True)
            y = o * lax.rsqrt(ms + EPS) * gn_ref[0]
            gt = g_ref[loc, :]
            o_ref[loc, :] = y.astype(BF16) * (gt * jax.nn.sigmoid(gt))


def _retention(lg, p_lat, p_ctx, ret_norm_g, batch, seq, ctx_len):
    chunk = min(256, seq)
    sub = math.gcd(seq // chunk, RET_SUB)
    blk = chunk * sub
    nb = seq // blk
    kern = functools.partial(_ret_kernel, chunk=chunk, sub=sub, n_blocks=nb, ctx_len=ctx_len)

    def rows(b, p, c):
        return b * nb + jnp.where(p == 0, nb - 1 - c, c)

    def rows_fwd(b, p, c):
        return b * nb + jnp.where(p == 0, 0, c)

    return pl.pallas_call(
        kern,
        out_shape=jax.ShapeDtypeStruct((batch * seq, RET_HEADS * RET_DV), BF16),
        grid=(batch, RET_HEADS, 2, nb),
        in_specs=[pl.BlockSpec(memory_space=pltpu.SMEM),
                  pl.BlockSpec((blk, RET_DK), lambda b, h, p, c: (rows(b, p, c), h)),
                  pl.BlockSpec((blk, RET_DK), lambda b, h, p, c: (rows(b, p, c), 4 + h)),
                  pl.BlockSpec((blk, RET_DV), lambda b, h, p, c: (rows(b, p, c), 4 + h)),
                  pl.BlockSpec((blk, RET_DV), lambda b, h, p, c: (rows_fwd(b, p, c), 8 + h)),
                  pl.BlockSpec((ctx_len, RET_DK), lambda b, h, p, c: (b, 4 + h)),
                  pl.BlockSpec((ctx_len, RET_DV), lambda b, h, p, c: (b, 4 + h)),
                  pl.BlockSpec((1, 1, RET_DV), lambda b, h, p, c: (h, 0, 0))],
        out_specs=pl.BlockSpec((blk, RET_DV), lambda b, h, p, c: (rows_fwd(b, p, c), h)),
        scratch_shapes=[pltpu.VMEM((RET_DK, RET_DV), F32), pltpu.VMEM((RET_DK, RET_DV), F32),
                        pltpu.VMEM((seq, RET_DV), F32)],
        compiler_params=_params(("arbitrary",) * 4, 48),
        name="ret",
    )(lg, p_lat, p_lat, p_lat, p_lat, p_ctx, p_ctx, ret_norm_g.reshape(RET_HEADS, 1, RET_DV))


def _dattn_kernel(lam_ref, q_ref, ck_ref, cv_ref, k_ref, v_ref, gn_ref, gmat_ref, o_ref,
                  qq_scr, kmax_scr, mp_scr, kp_scr, vt_scr, pt_scr, *, tq, tk, n_kv, ta, unroll, ctx_len):
    rows = 2 * tq
    n_all = kp_scr.shape[0]
    qi = pl.program_id(2)

    def sq_norms(t):
        tf = t.astype(F32)
        return jnp.dot((tf * tf).astype(BF16), gmat_ref[...], preferred_element_type=F32)

    def lane_const(n, hot):
        return jnp.where(lax.broadcasted_iota(I32, (n, LANES), 1) < hot, 1.0, 0.0).astype(BF16)

    @pl.when(qi == 0)
    def _():
        kmax_scr[...] = jnp.max(sq_norms(ck_ref[...]), axis=0, keepdims=True)
        kp_scr[:, LANES:2 * LANES] = lane_const(n_all, 3)
        kp_scr[0:ctx_len, 0:LANES] = ck_ref[...]
        vt_scr[:, 0:ctx_len] = cv_ref[...].astype(F32).T.astype(BF16)

        def body(c, carry):
            start = pl.multiple_of(c * tk, tk)
            k = k_ref[pl.ds(start, tk), :]
            kmax_scr[...] = jnp.maximum(kmax_scr[...], jnp.max(sq_norms(k), axis=0, keepdims=True))
            kp_scr[pl.ds(pl.multiple_of(ctx_len + start, LANES), tk), 0:LANES] = k
            vt_scr[:, pl.ds(pl.multiple_of(ctx_len + start, LANES), tk)] = (
                v_ref[pl.ds(start, tk), :].astype(F32).T.astype(BF16))
            return carry

        lax.fori_loop(0, n_kv, body, 0)

    qt = q_ref[...].astype(F32).T
    row = lax.broadcasted_iota(I32, (LANES, tq), 0)
    q1t = jnp.where(row < DIFF_DH, qt, 0.0)
    q2t = jnp.where(row >= DIFF_DH, qt, 0.0)
    qq_scr[0:LANES, 0:tq] = q1t.astype(BF16)
    qq_scr[0:LANES, tq:rows] = q2t.astype(BF16)

    def set_shift(shift):
        neg = -shift
        hi = neg.astype(BF16).astype(F32)
        mid = (neg - hi).astype(BF16).astype(F32)
        lo = neg - hi - mid
        row_r = lax.broadcasted_iota(I32, (LANES, rows), 0)
        pieces = jnp.where(row_r == 0, hi, jnp.where(row_r == 1, mid, jnp.where(row_r == 2, lo, 0.0)))
        qq_scr[LANES:2 * LANES, :] = pieces.astype(BF16)

    kmax = kmax_scr[...]
    b1 = jnp.sqrt(jnp.sum(q1t * q1t, axis=0, keepdims=True) * kmax[:, 0:1]) * SHIFT_SLACK
    b2 = jnp.sqrt(jnp.sum(q2t * q2t, axis=0, keepdims=True) * kmax[:, DIFF_DH:DIFF_DH + 1]) * SHIFT_SLACK
    bound = jnp.concatenate([b1, b2], axis=1)
    set_shift(bound)

    def key_tile(t):
        return pl.ds(pl.multiple_of(t * ta, ta), ta)

    @pl.when(jnp.max(bound) > MAX_SAFE_SHIFT)
    def _():
        mp_scr[...] = jnp.full(mp_scr.shape, NEG_BIG, F32)

        def body(t, carry):
            st = jnp.dot(kp_scr[key_tile(t), 0:LANES], qq_scr[0:LANES, :], preferred_element_type=F32)
            mp_scr[...] = jnp.maximum(mp_scr[...], jnp.max(st, axis=0, keepdims=True))
            return carry

        lax.fori_loop(0, n_all // ta, body, 0)
        set_shift(mp_scr[0:1, :])

    mp_scr[...] = jnp.zeros(mp_scr.shape, F32)

    def body_a(t, carry):
        for u in range(unroll):
            r = key_tile(t * unroll + u)
            p = jnp.exp2(jnp.dot(kp_scr[r, :], qq_scr[...], preferred_element_type=F32))
            pt_scr[r, :] = p.astype(BF16)
            mp_scr[...] += jnp.sum(p.reshape(ta // SUBLANES, SUBLANES, rows), axis=0)
        return carry

    lax.fori_loop(0, n_all // (ta * unroll), body_a, 0)

    acc = jnp.dot(vt_scr[...], pt_scr[...], preferred_element_type=F32)
    ot = acc / jnp.sum(mp_scr[...], axis=0, keepdims=True)
    d = (ot[:, 0:tq] - lam_ref[0] * ot[:, tq:rows]).T
    ms = jnp.mean(d * d, axis=-1, keepdims=True)
    y = d * lax.rsqrt(ms + EPS) * gn_ref[...] * (1.0 - LAMBDA_INIT)
    o_ref[...] = y.astype(BF16)


def _diff_attention(lam, p_lat, p_ctx, diff_norm_g, gmat, batch, seq, ctx_len):
    tq = min(1024, seq)
    tk = min(512, seq)
    nq, nk = seq // tq, seq // tk
    n_all = ctx_len + seq
    ta = next(t for t in (528, 512, 384, 320, 256, 128) if n_all % t == 0)
    unroll = math.gcd(n_all // ta, KV_UNROLL)
    assert ctx_len % LANES == 0
    kern = functools.partial(_dattn_kernel, tq=tq, tk=tk, n_kv=nk, ta=ta, unroll=unroll, ctx_len=ctx_len)
    return pl.pallas_call(
        kern,
        out_shape=jax.ShapeDtypeStruct((batch * seq, DIFF_HEADS * DIFF_DV), BF16),
        grid=(batch, DIFF_HEADS, nq),
        in_specs=[pl.BlockSpec(memory_space=pltpu.SMEM),
                  pl.BlockSpec((tq, LANES), lambda b, h, qi: (b * nq + qi, 48 + h)),
                  pl.BlockSpec((ctx_len, LANES), lambda b, h, qi: (b, 56 + h)),
                  pl.BlockSpec((ctx_len, LANES), lambda b, h, qi: (b, 64 + h)),
                  pl.BlockSpec((seq, LANES), lambda b, h, qi: (b, 56 + h)),
                  pl.BlockSpec((seq, LANES), lambda b, h, qi: (b, 64 + h)),
                  pl.BlockSpec((1, LANES), lambda b, h, qi: (0, 0)),
                  pl.BlockSpec((LANES, LANES), lambda b, h, qi: (0, 0))],
        out_specs=pl.BlockSpec((tq, LANES), lambda b, h, qi: (b * nq + qi, h)),
        scratch_shapes=[pltpu.VMEM((2 * LANES, 2 * tq), BF16), pltpu.VMEM((1, LANES), F32),
                        pltpu.VMEM((SUBLANES, 2 * tq), F32), pltpu.VMEM((n_all, 2 * LANES), BF16),
                        pltpu.VMEM((DIFF_DV, n_all), BF16), pltpu.VMEM((n_all, 2 * tq), BF16)],
        compiler_params=_params(("arbitrary",) * 3, 58),
        name="dattn",
    )(lam, p_lat, p_ctx, p_ctx, p_lat, p_lat, diff_norm_g.reshape(1, DIFF_DV), gmat)


def _merge_kernel(yr_ref, yd_ref, ga_ref, gb_ref, x_ref, gatea_ref, shf_ref, scf_ref, n2_ref,
                  wr_ref, wd_ref, wo_ref, rwh_ref, rwl_ref, rb_ref,
                  xn_ref, hp_ref, meta_ref, gate_ref, cnt_ref, run_scr, *, tm):
    i = pl.program_id(0)

    @pl.when(i == 0)
    def _():
        run_scr[...] = jnp.zeros(run_scr.shape, F32)

    yr = jnp.dot(yr_ref[...], wr_ref[...], preferred_element_type=F32)
    yd = jnp.dot(yd_ref[...], wd_ref[...], preferred_element_type=F32)
    m = (jax.nn.sigmoid(ga_ref[...].astype(F32)) * yr + jax.nn.sigmoid(gb_ref[...].astype(F32)) * yd)
    z = jnp.dot(m.astype(BF16), wo_ref[...], preferred_element_type=F32)
    xn = x_ref[...] + gatea_ref[0] * z
    xn_ref[...] = xn

    ms = jnp.mean(xn * xn, axis=-1, keepdims=True)
    h2 = xn * lax.rsqrt(ms + EPS) * n2_ref[...]
    h2 = h2 * (1.0 + scf_ref[0]) + shf_ref[0]
    h_hi = h2.astype(BF16)
    bits = pltpu.bitcast(h_hi.astype(F32), U32)
    half = D_MODEL // 2
    hp_ref[...] = (bits[:, :half] >> 16) | (bits[:, half:] & jnp.uint32(0xFFFF0000))

    h_lo = (h2 - h_hi.astype(F32)).astype(BF16)
    logits = (jnp.dot(h_hi, rwh_ref[...], preferred_element_type=F32)
              + jnp.dot(h_lo, rwh_ref[...], preferred_element_type=F32)
              + jnp.dot(h_hi, rwl_ref[...], preferred_element_type=F32) + rb_ref[...])
    lane = lax.broadcasted_iota(I32, (tm, LANES), 1)
    lanef = lane.astype(F32)
    work = logits
    vals, idxs = [], []
    for _ in range(TOP_K):
        mk = jnp.max(work, axis=1, keepdims=True)
        ik = jnp.min(jnp.where(work == mk, lanef, float(LANES)), axis=1, keepdims=True)
        vals.append(mk)
        idxs.append(ik)
        work = jnp.where(lanef == ik, -jnp.inf, work)
    ex = [jnp.exp(v - vals[0]) for v in vals]
    den = ex[0] + ex[1] + ex[2] + ex[3]

    onehot = jnp.zeros((tm, LANES), F32)
    for ik in idxs:
        onehot = onehot + jnp.where(lanef == ik, 1.0, 0.0)
    ri = lax.broadcasted_iota(I32, (tm, tm), 0)
    ci = lax.broadcasted_iota(I32, (tm, tm), 1)
    tri = jnp.where(ri > ci, 1.0, 0.0).astype(BF16)
    base = run_scr[0:1, :] + jnp.dot(tri, onehot.astype(BF16), preferred_element_type=F32)
    run_scr[...] = run_scr[...] + jnp.sum(onehot, axis=0, keepdims=True)

    meta = jnp.zeros((tm, LANES), F32)
    gate_out = jnp.zeros((tm, LANES), F32)
    for k in range(TOP_K):
        rk = jnp.sum(jnp.where(lanef == idxs[k], base, 0.0), axis=1, keepdims=True)
        meta = jnp.where(lane == k, idxs[k], jnp.where(lane == TOP_K + k, rk, meta))
        gate_out = jnp.where(lane == k, ex[k] / den, gate_out)
    meta_ref[...] = meta.T[0:2 * TOP_K, :]
    gate_ref[...] = gate_out
    cnt_ref[...] = run_scr[...]


def _merge(y_ret, y_diff, p_lat, x2, g_a, sh_f, sc_f, norm2_g, w_r, w_d, w_o, rw_hi, rw_lo, rb, seq):
    n = x2.shape[0]
    tm = min(512, seq)
    tpb = seq // tm
    kern = functools.partial(_merge_kernel, tm=tm)
    mod_spec = pl.BlockSpec((1, 1, D_MODEL), lambda i: (i // tpb, 0, 0))
    const = lambda shape: pl.BlockSpec(shape, lambda i: (0,) * len(shape))
    tok = lambda w: pl.BlockSpec((tm, w), lambda i: (i, 0))
    return pl.pallas_call(
        kern,
        out_shape=(jax.ShapeDtypeStruct((n, D_MODEL), F32),
                   jax.ShapeDtypeStruct((n, D_MODEL // 2), U32),
                   jax.ShapeDtypeStruct((2 * TOP_K, n), F32),
                   jax.ShapeDtypeStruct((n, LANES), F32),
                   jax.ShapeDtypeStruct((8, LANES), F32)),
        grid=(n // tm,),
        in_specs=[tok(RET_HEADS * RET_DV), tok(D_MODEL),
                  pl.BlockSpec((tm, COL_TILE), lambda i: (i, 9)),
                  pl.BlockSpec((tm, COL_TILE), lambda i: (i, 10)),
                  tok(D_MODEL), mod_spec, mod_spec, mod_spec, const((1, D_MODEL)),
                  const((RET_HEADS * RET_DV, D_MODEL)), const((D_MODEL, D_MODEL)), const((D_MODEL, D_MODEL)),
                  const((D_MODEL, LANES)), const((D_MODEL, LANES)), const((1, LANES))],
        out_specs=(tok(D_MODEL), tok(D_MODEL // 2), pl.BlockSpec((2 * TOP_K, tm), lambda i: (0, i)),
                   tok(LANES), const((8, LANES))),
        scratch_shapes=[pltpu.VMEM((8, LANES), F32)],
        compiler_params=_params(("arbitrary",), 56),
        name="merge",
    )(y_ret, y_diff, p_lat, p_lat, x2, g_a, sh_f, sc_f, norm2_g, w_r, w_d, w_o, rw_hi, rw_lo, rb)


def _dispatch_kernel(dest_ref, fill_ref, nb_ref, hp_ref, xb_ref, zero_scr, sem, zsem, *, tm, n_tok, n_blocks):
    i = pl.program_id(0)

    @pl.when(i == 0)
    def _():
        zero_scr[...] = jnp.zeros(zero_scr.shape, U32)
        fill = zero_scr.shape[0]
        for e in range(N_EXPERTS):
            start = pl.multiple_of(fill_ref[e], SUBLANES)
            pltpu.make_async_copy(zero_scr, xb_ref.at[pl.ds(start, fill), :], zsem).start()
        for e in range(N_EXPERTS):
            pltpu.make_async_copy(zero_scr, xb_ref.at[pl.ds(0, fill), :], zsem).wait()
        blk = zero_scr.at[pl.ds(0, MOE_BLK), :]

        def start_blk(b, carry):
            pltpu.make_async_copy(blk, xb_ref.at[pl.ds(pl.multiple_of(b * MOE_BLK, MOE_BLK), MOE_BLK), :], zsem).start()
            return carry

        def wait_blk(b, carry):
            pltpu.make_async_copy(blk, xb_ref.at[pl.ds(0, MOE_BLK), :], zsem).wait()
            return carry

        lax.fori_loop(nb_ref[0], n_blocks, start_blk, 0)
        lax.fori_loop(nb_ref[0], n_blocks, wait_blk, 0)

    def body(g, carry):
        r0 = pl.multiple_of(g * SUBLANES, SUBLANES)
        for j in range(SUBLANES):
            for k in range(TOP_K):
                d = dest_ref[k * n_tok + i * tm + r0 + j]
                pltpu.make_async_copy(hp_ref.at[pl.ds(r0 + j, 1), :], xb_ref.at[pl.ds(d, 1), :], sem).start()
        return carry

    lax.fori_loop(0, tm // SUBLANES, body, 0)
    for _ in range(TOP_K):
        pltpu.make_async_copy(hp_ref, xb_ref.at[pl.ds(0, tm), :], sem).wait()


def _dispatch(dest, fill_start, n_used, hp, n_rows):
    n = hp.shape[0]
    tm = min(512, n)
    kern = functools.partial(_dispatch_kernel, tm=tm, n_tok=n, n_blocks=n_rows // MOE_BLK)
    return pl.pallas_call(
        kern,
        out_shape=jax.ShapeDtypeStruct((n_rows, D_MODEL // 2), U32),
        grid_spec=pltpu.PrefetchScalarGridSpec(
            num_scalar_prefetch=3,
            grid=(n // tm,),
            in_specs=[pl.BlockSpec((tm, D_MODEL // 2), lambda i, d, f, nb: (i, 0))],
            out_specs=pl.BlockSpec(memory_space=pl.ANY),
            scratch_shapes=[pltpu.VMEM((MOE_BLK + SUBLANES, D_MODEL // 2), U32), pltpu.SemaphoreType.DMA(()),
                            pltpu.SemaphoreType.DMA(())]),
        compiler_params=_params(("arbitrary",), 32),
        name="dispatch",
    )(dest, fill_start, n_used, hp)


def _expert_kernel(be_ref, nb_ref, nxt_ref, xb_ref, w1_hbm, b1_ref, w2_hbm, b2_ref, y_ref,
                   w1f_scr, w2f_scr, w1b_scr, w2b_scr, x_scr, grp_scr, sems):
    i = pl.program_id(0)
    e = be_ref[i]
    prev = be_ref[jnp.maximum(i - 1, 0)]
    active = i < nb_ref[0]

    def weight_copies(expert, slot):
        return (pltpu.make_async_copy(w1_hbm.at[expert], w1f_scr.at[slot], sems.at[0, slot]),
                pltpu.make_async_copy(w2_hbm.at[expert], w2f_scr.at[slot], sems.at[1, slot]))

    @pl.when(i == 0)
    def _():
        grp_scr[0] = 0
        for cp in weight_copies(e, 0):
            cp.start()

    @pl.when(active & (i > 0) & (e != prev))
    def _():
        grp_scr[0] = grp_scr[0] + 1

    @pl.when(active & ((i == 0) | (e != prev)))
    def _():
        slot = grp_scr[0] % 2
        for cp in weight_copies(e, slot):
            cp.wait()
        w1b_scr[...] = w1f_scr[slot].astype(BF16)
        w2b_scr[...] = w2f_scr[slot].astype(BF16)
        nxt = nxt_ref[e]

        @pl.when(nxt >= 0)
        def _():
            for cp in weight_copies(nxt, 1 - slot):
                cp.start()

    @pl.when(active)
    def _():
        xu = xb_ref[...]
        half = D_MODEL // 2
        x_scr[:, 0:half] = pltpu.bitcast(xu << 16, F32).astype(BF16)
        x_scr[:, half:] = pltpu.bitcast(xu & jnp.uint32(0xFFFF0000), F32).astype(BF16)
        hh = jnp.dot(x_scr[...], w1b_scr[...], preferred_element_type=F32) + b1_ref[0]
        glu = jnp.minimum(hh[:, :D_FF], SWIGLU_LIMIT)
        lin = jnp.clip(hh[:, D_FF:], -SWIGLU_LIMIT, SWIGLU_LIMIT)
        act = glu * jax.nn.sigmoid(SWIGLU_ALPHA * glu) * (lin + 1.0)
        y_ref[...] = jnp.dot(act.astype(BF16), w2b_scr[...], preferred_element_type=F32) + b2_ref[0]

    @pl.when(i >= nb_ref[0])
    def _():
        y_ref[...] = jnp.zeros(y_ref.shape, F32)


def _experts(block_e, n_used, next_e, xb, w1, b1, w2, b2):
    n_blocks = block_e.shape[0]
    rows = n_blocks * MOE_BLK
    return pl.pallas_call(
        _expert_kernel,
        out_shape=jax.ShapeDtypeStruct((rows, D_MODEL), F32),
        grid_spec=pltpu.PrefetchScalarGridSpec(
            num_scalar_prefetch=3,
            grid=(n_blocks,),
            in_specs=[pl.BlockSpec((MOE_BLK, D_MODEL // 2), lambda i, be, nb, nx: (jnp.minimum(i, nb[0] - 1), 0)),
                      pl.BlockSpec(memory_space=pl.ANY),
                      pl.BlockSpec((1, 1, 2 * D_FF), lambda i, be, nb, nx: (be[i], 0, 0)),
                      pl.BlockSpec(memory_space=pl.ANY),
                      pl.BlockSpec((1, 1, D_MODEL), lambda i, be, nb, nx: (be[i], 0, 0))],
            out_specs=pl.BlockSpec((MOE_BLK, D_MODEL), lambda i, be, nb, nx: (i, 0)),
            scratch_shapes=[pltpu.VMEM((2, D_MODEL, 2 * D_FF), F32), pltpu.VMEM((2, D_FF, D_MODEL), F32),
                            pltpu.VMEM((D_MODEL, 2 * D_FF), BF16), pltpu.VMEM((D_FF, D_MODEL), BF16),
                            pltpu.VMEM((MOE_BLK, D_MODEL), BF16),
                            pltpu.SMEM((1,), I32), pltpu.SemaphoreType.DMA((2, 2))]),
        compiler_params=_params(("arbitrary",), 56),
        name="expert",
    )(block_e, n_used, next_e, xb, w1, b1.reshape(N_EXPERTS, 1, 2 * D_FF), w2, b2.reshape(N_EXPERTS, 1, D_MODEL))


def _combine_kernel(dest_ref, yb_ref, gate_ref, xn_ref, gf_ref, o_ref, buf, sems, *, tm, n_tok, n_steps):
    i = pl.program_id(0)

    def issue(tile, slot):
        def body(g, carry):
            r0 = pl.multiple_of(g * SUBLANES, SUBLANES)
            for j in range(SUBLANES):
                for k in range(TOP_K):
                    d = dest_ref[k * n_tok + tile * tm + r0 + j]
                    pltpu.make_async_copy(yb_ref.at[pl.ds(d, 1), :], buf.at[slot, k, pl.ds(r0 + j, 1), :],
                                          sems.at[slot]).start()
            return carry

        lax.fori_loop(0, tm // SUBLANES, body, 0)

    def finish(slot):
        for k in range(TOP_K):
            pltpu.make_async_copy(yb_ref.at[pl.ds(0, tm), :], buf.at[slot, k], sems.at[slot]).wait()
        g = gate_ref[...]
        y = g[:, 0:1] * buf[slot, 0]
        for k in range(1, TOP_K):
            y = y + g[:, k:k + 1] * buf[slot, k]
        o_ref[...] = xn_ref[...] + gf_ref[0] * y

    @pl.when(i == 0)
    def _():
        issue(0, 0)

    for slot in range(2):
        @pl.when(lax.rem(i, 2) == slot)
        def _():
            @pl.when(i + 1 < n_steps)
            def _():
                issue(i + 1, 1 - slot)

            finish(slot)


def _combine(dest, yb, gate, xn, g_f, seq):
    n = xn.shape[0]
    tm = min(512, seq)
    tpb = seq // tm
    kern = functools.partial(_combine_kernel, tm=tm, n_tok=n, n_steps=n // tm)
    return pl.pallas_call(
        kern,
        out_shape=jax.ShapeDtypeStruct((n, D_MODEL), F32),
        grid_spec=pltpu.PrefetchScalarGridSpec(
            num_scalar_prefetch=1,
            grid=(n // tm,),
            in_specs=[pl.BlockSpec(memory_space=pl.ANY),
                      pl.BlockSpec((tm, LANES), lambda i, d: (i, 0)),
                      pl.BlockSpec((tm, D_MODEL), lambda i, d: (i, 0)),
                      pl.BlockSpec((1, 1, D_MODEL), lambda i, d: (i // tpb, 0, 0))],
            out_specs=pl.BlockSpec((tm, D_MODEL), lambda i, d: (i, 0)),
            scratch_shapes=[pltpu.VMEM((2, TOP_K, tm, D_MODEL), F32), pltpu.SemaphoreType.DMA((2,))]),
        compiler_params=_params(("arbitrary",), 40),
        name="combine",
    )(dest, yb, gate, xn, g_f)


def _layer(x, ctx, c, c_ctx, norm1_g, norm2_g, w_mod, b_mod, w_in, ret_decay_logit, ret_norm_g,
           diff_q_norm_g, diff_k_norm_g, diff_lambda, diff_norm_g, w_br_ret, w_br_diff, w_out,
           router_w, router_b, exp_w1, exp_b1, exp_w2, exp_b2):
    batch, seq, d = x.shape
    ctx_len = ctx.shape[1]
    assert d == D_MODEL and seq % GRID_W == 0 and batch + 1 <= 8
    n_tok = batch * seq

    cc = jnp.zeros((8, D_MODEL), F32).at[:batch].set(c).at[batch].set(c_ctx)
    mod = _mod(cc, w_mod, b_mod)
    sh_a, sc_a, g_a, sh_f, sc_f, g_f = [mod[:batch, i * D_MODEL:(i + 1) * D_MODEL].reshape(batch, 1, D_MODEL)
                                         for i in range(6)]
    csh_a = mod[batch:batch + 1, 0:D_MODEL].reshape(1, 1, D_MODEL)
    csc_a = mod[batch:batch + 1, D_MODEL:2 * D_MODEL].reshape(1, 1, D_MODEL)

    w_in_bf = w_in.astype(BF16)
    g1 = norm1_g.reshape(1, D_MODEL)
    tile = lambda g: jnp.tile(g.astype(F32), 2)
    qkg = jnp.zeros((8, LANES), F32).at[0].set(tile(diff_q_norm_g) * (DIFF_DH ** -0.5 * LOG2E)).at[1].set(tile(diff_k_norm_g))
    lane = jnp.arange(2 * LANES)
    same_group = lane[:, None] // DIFF_DH == lane[None, :] // DIFF_DH
    gmean = jnp.where(same_group, 1.0 / DIFF_DH, 0.0).astype(BF16)
    gmat = same_group[:LANES, :LANES].astype(BF16)
    x2 = x.reshape(n_tok, D_MODEL)
    tm = min(1024, seq)
    p_lat = _inproj(x2, g1, sh_a, sc_a, w_in_bf, qkg, gmean, _rope_tables(seq), tm, seq // tm)
    p_ctx = _inproj(ctx.reshape(batch * ctx_len, D_MODEL), g1, csh_a, csc_a, w_in_bf, qkg, gmean,
                    _identity_tables(ctx_len), ctx_len, 1)

    lg = jax.nn.log_sigmoid(ret_decay_logit.astype(F32))
    y_ret = _retention(lg, p_lat, p_ctx, ret_norm_g, batch, seq, ctx_len)

    lp = diff_lambda.astype(F32)
    lam = (jnp.exp(jnp.sum(lp[0] * lp[1])) - jnp.exp(jnp.sum(lp[2] * lp[3])) + LAMBDA_INIT).reshape(1)
    y_diff = _diff_attention(lam, p_lat, p_ctx, diff_norm_g, gmat, batch, seq, ctx_len)

    rw = jnp.zeros((D_MODEL, LANES), F32).at[:, :N_EXPERTS].set(router_w)
    rw_hi = rw.astype(BF16)
    rw_lo = (rw - rw_hi.astype(F32)).astype(BF16)
    rb = jnp.full((1, LANES), NEG_BIG, F32).at[0, :N_EXPERTS].set(router_b)
    xn, hp, meta, gate4, cnt = _merge(
        y_ret, y_diff, p_lat, x2, g_a, sh_f, sc_f, norm2_g.reshape(1, D_MODEL),
        w_br_ret.astype(BF16), w_br_diff.astype(BF16), w_out.astype(BF16), rw_hi, rw_lo, rb, seq)

    counts = cnt[0, :N_EXPERTS].astype(I32)
    padded = (counts + MOE_BLK - 1) // MOE_BLK * MOE_BLK
    pad_end = jnp.cumsum(padded)
    pad_start = pad_end - padded
    n_pairs = n_tok * TOP_K
    n_blocks = n_pairs // MOE_BLK + N_EXPERTS
    meta_i = meta.astype(I32)
    is_e = meta_i[None, :TOP_K] == jnp.arange(N_EXPERTS, dtype=I32)[:, None, None]
    dest = (jnp.sum(jnp.where(is_e, pad_start[:, None, None], 0), axis=0) + meta_i[TOP_K:]).reshape(n_pairs)
    block_start = jnp.arange(n_blocks, dtype=I32) * MOE_BLK
    block_e = jnp.minimum(jnp.sum((pad_end[None, :] <= block_start[:, None]).astype(I32), axis=1), N_EXPERTS - 1)
    n_used = (pad_end[-1] // MOE_BLK).reshape(1).astype(I32)

    fill_start = ((pad_start + counts) // SUBLANES * SUBLANES).astype(I32)
    xb = _dispatch(dest, fill_start, n_used, hp, (n_blocks + 1) * MOE_BLK)
    e_ids = jnp.arange(N_EXPERTS, dtype=I32)
    later = jnp.where((counts[None, :] > 0) & (e_ids[None, :] > e_ids[:, None]), e_ids[None, :], N_EXPERTS)
    next_e = jnp.min(later, axis=1)
    next_e = jnp.where(next_e == N_EXPERTS, -1, next_e).astype(I32)
    yb = _experts(block_e, n_used, next_e, xb, exp_w1, exp_b1, exp_w2, exp_b2)
    out = _combine(dest, yb, gate4, xn, g_f, seq)
    return out.reshape(batch, seq, D_MODEL)


def kernel(x, c, ctx, c_ctx, norm1_g, norm2_g, w_mod, b_mod, w_in, ret_decay_logit, ret_norm_g, diff_q_norm_g, diff_k_norm_g, diff_lambda, diff_norm_g, w_br_ret, w_br_diff, w_out, router_w, router_b, exp_w1, exp_b1, exp_w2, exp_b2):
    assert norm1_g.shape[0] == 1, "single-layer block"
    return _layer(x, ctx, c, c_ctx, norm1_g[0], norm2_g[0], w_mod[0], b_mod[0], w_in[0], ret_decay_logit[0],
                  ret_norm_g[0], diff_q_norm_g[0], diff_k_norm_g[0], diff_lambda[0], diff_norm_g[0],
                  w_br_ret[0], w_br_diff[0], w_out[0], router_w[0], router_b[0],
                  exp_w1[0], exp_b1[0], exp_w2[0], exp_b2[0])
```

```python
import functools
import math

import numpy as np

import jax
import jax.numpy as jnp
from jax import lax
from jax.experimental import pallas as pl
from jax.experimental.pallas import tpu as pltpu

F32 = jnp.float32
BF16 = jnp.bfloat16
U32 = jnp.uint32
I32 = jnp.int32

D_MODEL = 1024
GRID_W = 64
RET_HEADS = 4
RET_DK = 256
RET_DV = 512
DIFF_DH = 64
DIFF_HEADS = 8
DIFF_DV = 128
N_EXPERTS = 32
TOP_K = 4
D_FF = 1024
SWIGLU_LIMIT = 7.0
SWIGLU_ALPHA = 1.702
ROPE_BASE = 10000.0
EPS = 1e-6
LAMBDA_INIT = 0.8 - 0.6 * math.exp(-0.3 * 0)

IN_COLS = 11264
COL_TILE = 1024
LANES = 128
SUBLANES = 8
MOE_BLK = 256
NEG_BIG = -1e30
LOG2E = 1.4426950408889634
SHIFT_SLACK = 1.0 + 2.0 ** -6
MAX_SAFE_SHIFT = 60.0
RET_SUB = 8
KV_UNROLL = 4
HIGHEST = lax.Precision.HIGHEST
MIB = 1024 * 1024


def _params(sem, vmem_mib):
    return pltpu.CompilerParams(dimension_semantics=sem, vmem_limit_bytes=vmem_mib * MIB)


def _mod_kernel(c_ref, w_ref, b_ref, o_ref):
    c = c_ref[...]
    s = c * jax.nn.sigmoid(c)
    o_ref[...] = jnp.dot(s, w_ref[...], preferred_element_type=F32, precision=HIGHEST) + b_ref[...]


def _mod(cc, w_mod, b_mod):
    n = w_mod.shape[1]
    tn = 1024
    return pl.pallas_call(
        _mod_kernel,
        out_shape=jax.ShapeDtypeStruct((8, n), F32),
        grid=(n // tn,),
        in_specs=[pl.BlockSpec((8, D_MODEL), lambda j: (0, 0)),
                  pl.BlockSpec((D_MODEL, tn), lambda j: (0, j)),
                  pl.BlockSpec((1, tn), lambda j: (0, j))],
        out_specs=pl.BlockSpec((8, tn), lambda j: (0, j)),
        compiler_params=_params(("arbitrary",), 32),
        name="mod",
    )(cc, w_mod, b_mod.reshape(1, n))


def _inproj_kernel(x_ref, g_ref, sh_ref, sc_ref, w_ref, qkg_ref, gmat_ref,
                   cr_ref, sr_ref, cc_ref, sc2_ref, cd_ref, sa_ref, sb_ref,
                   o_ref, h_scr, acc_scr):
    j = pl.program_id(1)

    @pl.when(j == 0)
    def _():
        xf = x_ref[...]
        ms = jnp.mean(xf * xf, axis=-1, keepdims=True)
        y = xf * lax.rsqrt(ms + EPS) * g_ref[...]
        h_scr[...] = (y * (1.0 + sc_ref[0]) + sh_ref[0]).astype(BF16)

    is_ret = j <= 1
    is_dqk = (j == 6) | (j == 7)

    def project():
        return jnp.dot(h_scr[...], w_ref[...], preferred_element_type=F32)

    @pl.when(is_ret)
    def _():
        acc_scr[...] = project()
        scale = jnp.where(j == 0, RET_DK ** -0.5, 1.0).astype(F32)
        for b in range(COL_TILE // LANES):
            xb = acc_scr[:, b * LANES:(b + 1) * LANES]
            cos = cr_ref[...] if b % 2 == 0 else cc_ref[...]
            sin = sr_ref[...] if b % 2 == 0 else sc2_ref[...]
            o = (xb * cos + pltpu.roll(xb, 64, 1) * sin) * scale
            o_ref[:, b * LANES:(b + 1) * LANES] = o.astype(BF16)

    @pl.when(is_dqk)
    def _():
        acc_scr[...] = project()
        g = jnp.where(j == 6, qkg_ref[0:1, :], qkg_ref[1:2, :])
        for b2 in range(COL_TILE // (2 * LANES)):
            x2 = acc_scr[:, b2 * 2 * LANES:(b2 + 1) * 2 * LANES]
            ms2 = jnp.dot((x2 * x2).astype(BF16), gmat_ref[...], preferred_element_type=F32)
            for half in range(2):
                b = 2 * b2 + half
                xb = x2[:, half * LANES:(half + 1) * LANES]
                yn = xb * lax.rsqrt(ms2[:, half * LANES:(half + 1) * LANES] + EPS) * g
                o = yn * cd_ref[...] + pltpu.roll(yn, 16, 1) * sa_ref[...] + pltpu.roll(yn, 112, 1) * sb_ref[...]
                o_ref[:, b * LANES:(b + 1) * LANES] = o.astype(BF16)

    @pl.when(jnp.logical_not(is_ret | is_dqk))
    def _():
        o_ref[...] = project().astype(BF16)


def _inproj(x2, g1, sh, sc, w_bf, qkg, gmat, tables, tm, tiles_per_batch):
    n = x2.shape[0]
    nb = sh.shape[0]
    tab_spec = pl.BlockSpec((tm, LANES), lambda i, j: (i % tiles_per_batch, 0))
    mod_spec = pl.BlockSpec((1, 1, D_MODEL), lambda i, j: (jnp.minimum(i // tiles_per_batch, nb - 1), 0, 0))
    return pl.pallas_call(
        _inproj_kernel,
        out_shape=jax.ShapeDtypeStruct((n, IN_COLS), BF16),
        grid=(n // tm, IN_COLS // COL_TILE),
        in_specs=[pl.BlockSpec((tm, D_MODEL), lambda i, j: (i, 0)),
                  pl.BlockSpec((1, D_MODEL), lambda i, j: (0, 0)),
                  mod_spec, mod_spec,
                  pl.BlockSpec((D_MODEL, COL_TILE), lambda i, j: (0, j)),
                  pl.BlockSpec((8, LANES), lambda i, j: (0, 0)),
                  pl.BlockSpec((2 * LANES, 2 * LANES), lambda i, j: (0, 0))] + [tab_spec] * 7,
        out_specs=pl.BlockSpec((tm, COL_TILE), lambda i, j: (i, j)),
        scratch_shapes=[pltpu.VMEM((tm, D_MODEL), BF16), pltpu.VMEM((tm, COL_TILE), F32)],
        compiler_params=_params(("arbitrary", "arbitrary"), 48),
        name="inproj",
    )(x2, g1, sh, sc, w_bf, qkg, gmat, *tables)


def _rope_tables(seq):
    n_rows = seq // GRID_W
    f32 = np.float32

    def angles(pos, half):
        inv = f32(ROPE_BASE) ** (-np.arange(half, dtype=f32) / f32(half))
        return (pos.astype(f32)[:, None] * inv[None, :]).astype(np.float64)

    ar, ac = angles(np.arange(n_rows), 64), angles(np.arange(GRID_W), 64)
    br, bc = angles(np.arange(n_rows), 16), angles(np.arange(GRID_W), 16)
    zr, zc = np.zeros_like(br), np.zeros_like(bc)
    cat = lambda parts, reps=1: np.tile(np.concatenate(parts, axis=1), (1, reps)).astype(f32)
    by_row = lambda t: jnp.repeat(jnp.asarray(t), GRID_W, axis=0)
    by_col = lambda t: jnp.tile(jnp.asarray(t), (n_rows, 1))
    cr = by_row(cat([np.cos(ar), np.cos(ar)]))
    sr = by_row(cat([-np.sin(ar), np.sin(ar)]))
    cc = by_col(cat([np.cos(ac), np.cos(ac)]))
    sc = by_col(cat([-np.sin(ac), np.sin(ac)]))
    cd = by_row(cat([np.cos(br), np.cos(br), zr, zr], 2)) + by_col(cat([zc, zc, np.cos(bc), np.cos(bc)], 2))
    sa = by_row(cat([zr, np.sin(br), zr, zr], 2)) + by_col(cat([zc, zc, zc, np.sin(bc)], 2))
    sb = by_row(cat([-np.sin(br), zr, zr, zr], 2)) + by_col(cat([zc, zc, -np.sin(bc), zc], 2))
    return [cr, sr, cc, sc, cd, sa, sb]


def _identity_tables(seq):
    one = jnp.ones((seq, LANES), F32)
    zero = jnp.zeros((seq, LANES), F32)
    return [one, zero, one, zero, one, zero, zero]


def _tn_dot(a, b):
    return lax.dot_general(a, b, (((0,), (0,)), ((), ())), preferred_element_type=F32)


def _nt_dot(a, b):
    return lax.dot_general(a, b, (((1,), (1,)), ((), ())), preferred_element_type=F32)


def _ret_kernel(lg_ref, q_ref, k_ref, v_ref, g_ref, ck_ref, cv_ref, gn_ref, o_ref,
                sf_scr, sb_scr, ob_scr, *, chunk, sub, n_blocks, ctx_len):
    h = pl.program_id(1)
    p = pl.program_id(2)
    c = pl.program_id(3)
    lgf = lg_ref[0, h]
    lgb = lg_ref[1, h]

    def col_iota(n):
        return lax.broadcasted_iota(I32, (n, 1), 0).astype(F32)

    def vexp(s):
        return jnp.exp(jnp.zeros((1, 1), F32) + s)

    @pl.when((p == 0) & (c == 0))
    def _():
        jc = col_iota(ctx_len)
        kc = ck_ref[...].astype(F32)
        vc = cv_ref[...]
        sf_scr[...] = _tn_dot((kc * jnp.exp(lgf * (ctx_len - 1.0 - jc))).astype(BF16), vc)
        sb_scr[...] = _tn_dot((kc * jnp.exp(lgb * jc)).astype(BF16), vc)

    ic = col_iota(chunk)

    def full_bf16(col):
        return jnp.broadcast_to(col, (chunk, RET_DK)).astype(BF16)

    @pl.when(p == 0)
    def _():
        q_decay = full_bf16(jnp.exp(lgb * (chunk - ic)))
        k_decay = full_bf16(jnp.exp(lgb * ic))
        s_decay = vexp(lgb * chunk)
        for j in reversed(range(sub)):
            loc = pl.ds(j * chunk, chunk)
            glob = pl.ds(pl.multiple_of(((n_blocks - 1 - c) * sub + j) * chunk, chunk), chunk)
            qb = q_ref[loc, :] * q_decay
            ob_scr[glob, :] = jnp.dot(qb, sb_scr[...].astype(BF16), preferred_element_type=F32)
            kb = k_ref[loc, :] * k_decay
            sb_scr[...] = s_decay * sb_scr[...] + _tn_dot(kb, v_ref[loc, :])

    @pl.when(p == 1)
    def _():
        ri = lax.broadcasted_iota(I32, (chunk, chunk), 0)
        ci = lax.broadcasted_iota(I32, (chunk, chunk), 1)
        d = (ri - ci).astype(F32)
        mask = jnp.where(d > 0, jnp.exp(lgf * jnp.maximum(d, 0.0)),
                         jnp.where(d < 0, jnp.exp(lgb * jnp.maximum(-d, 0.0)), 2.0))
        q_decay = full_bf16(jnp.exp(lgf * (ic + 1.0)))
        k_decay = full_bf16(jnp.exp(lgf * (chunk - 1.0 - ic)))
        s_decay = vexp(lgf * chunk)
        for j in range(sub):
            loc = pl.ds(j * chunk, chunk)
            glob = pl.ds(pl.multiple_of((c * sub + j) * chunk, chunk), chunk)
            a = (_nt_dot(q_ref[loc, :], k_ref[loc, :]) * mask).astype(BF16)
            qf = q_ref[loc, :] * q_decay
            o = (jnp.dot(a, v_ref[loc, :], preferred_element_type=F32)
                 + jnp.dot(qf, sf_scr[...].astype(BF16), preferred_element_type=F32)
                 + ob_scr[glob, :])
            kf = k_ref[loc, :] * k_decay
            sf_scr[...] = s_decay * sf_scr[...] + _tn_dot(kf, v_ref[loc, :])
            ms = jnp.mean(o * o, axis=-1, keepdims=True)
            y = o * lax.rsqrt(ms + EPS) * gn_ref[0]
            gt = g_ref[loc, :]
            o_ref[loc, :] = y.astype(BF16) * (gt * jax.nn.sigmoid(gt))


def _retention(lg, p_lat, p_ctx, ret_norm_g, batch, seq, ctx_len):
    chunk = min(256, seq)
    sub = math.gcd(seq // chunk, RET_SUB)
    blk = chunk * sub
    nb = seq // blk
    kern = functools.partial(_ret_kernel, chunk=chunk, sub=sub, n_blocks=nb, ctx_len=ctx_len)

    def rows(b, p, c):
        return b * nb + jnp.where(p == 0, nb - 1 - c, c)

    def rows_fwd(b, p, c):
        return b * nb + jnp.where(p == 0, 0, c)

    return pl.pallas_call(
        kern,
        out_shape=jax.ShapeDtypeStruct((batch * seq, RET_HEADS * RET_DV), BF16),
        grid=(batch, RET_HEADS, 2, nb),
        in_specs=[pl.BlockSpec(memory_space=pltpu.SMEM),
                  pl.BlockSpec((blk, RET_DK), lambda b, h, p, c: (rows(b, p, c), h)),
                  pl.BlockSpec((blk, RET_DK), lambda b, h, p, c: (rows(b, p, c), 4 + h)),
                  pl.BlockSpec((blk, RET_DV), lambda b, h, p, c: (rows(b, p, c), 4 + h)),
                  pl.BlockSpec((blk, RET_DV), lambda b, h, p, c: (rows_fwd(b, p, c), 8 + h)),
                  pl.BlockSpec((ctx_len, RET_DK), lambda b, h, p, c: (b, 4 + h)),
                  pl.BlockSpec((ctx_len, RET_DV), lambda b, h, p, c: (b, 4 + h)),
                  pl.BlockSpec((1, 1, RET_DV), lambda b, h, p, c: (h, 0, 0))],
        out_specs=pl.BlockSpec((blk, RET_DV), lambda b, h, p, c: (rows_fwd(b, p, c), h)),
        scratch_shapes=[pltpu.VMEM((RET_DK, RET_DV), F32), pltpu.VMEM((RET_DK, RET_DV), F32),
                        pltpu.VMEM((seq, RET_DV), F32)],
        compiler_params=_params(("arbitrary",) * 4, 48),
        name="ret",
    )(lg, p_lat, p_lat, p_lat, p_lat, p_ctx, p_ctx, ret_norm_g.reshape(RET_HEADS, 1, RET_DV))


def _dattn_kernel(lam_ref, q_ref, ck_ref, cv_ref, k_ref, v_ref, gn_ref, gmat_ref, o_ref,
                  qq_scr, kmax_scr, mp_scr, kp_scr, vt_scr, pt_scr, *, tq, tk, n_kv, ta, unroll, ctx_len):
    rows = 2 * tq
    n_all = kp_scr.shape[0]
    qi = pl.program_id(2)

    def sq_norms(t):
        tf = t.astype(F32)
        return jnp.dot((tf * tf).astype(BF16), gmat_ref[...], preferred_element_type=F32)

    def lane_const(n, hot):
        return jnp.where(lax.broadcasted_iota(I32, (n, LANES), 1) < hot, 1.0, 0.0).astype(BF16)

    @pl.when(qi == 0)
    def _():
        kmax_scr[...] = jnp.max(sq_norms(ck_ref[...]), axis=0, keepdims=True)
        kp_scr[:, LANES:2 * LANES] = lane_const(n_all, 3)
        kp_scr[0:ctx_len, 0:LANES] = ck_ref[...]
        vt_scr[:, 0:ctx_len] = cv_ref[...].astype(F32).T.astype(BF16)

        def body(c, carry):
            start = pl.multiple_of(c * tk, tk)
            k = k_ref[pl.ds(start, tk), :]
            kmax_scr[...] = jnp.maximum(kmax_scr[...], jnp.max(sq_norms(k), axis=0, keepdims=True))
            kp_scr[pl.ds(pl.multiple_of(ctx_len + start, LANES), tk), 0:LANES] = k
            vt_scr[:, pl.ds(pl.multiple_of(ctx_len + start, LANES), tk)] = (
                v_ref[pl.ds(start, tk), :].astype(F32).T.astype(BF16))
            return carry

        lax.fori_loop(0, n_kv, body, 0)

    qt = q_ref[...].astype(F32).T
    row = lax.broadcasted_iota(I32, (LANES, tq), 0)
    q1t = jnp.where(row < DIFF_DH, qt, 0.0)
    q2t = jnp.where(row >= DIFF_DH, qt, 0.0)
    qq_scr[0:LANES, 0:tq] = q1t.astype(BF16)
    qq_scr[0:LANES, tq:rows] = q2t.astype(BF16)

    def set_shift(shift):
        neg = -shift
        hi = neg.astype(BF16).astype(F32)
        mid = (neg - hi).astype(BF16).astype(F32)
        lo = neg - hi - mid
        row_r = lax.broadcasted_iota(I32, (LANES, rows), 0)
        pieces = jnp.where(row_r == 0, hi, jnp.where(row_r == 1, mid, jnp.where(row_r == 2, lo, 0.0)))
        qq_scr[LANES:2 * LANES, :] = pieces.astype(BF16)

    kmax = kmax_scr[...]
    b1 = jnp.sqrt(jnp.sum(q1t * q1t, axis=0, keepdims=True) * kmax[:, 0:1]) * SHIFT_SLACK
    b2 = jnp.sqrt(jnp.sum(q2t * q2t, axis=0, keepdims=True) * kmax[:, DIFF_DH:DIFF_DH + 1]) * SHIFT_SLACK
    bound = jnp.concatenate([b1, b2], axis=1)
    set_shift(bound)

    def key_tile(t):
        return pl.ds(pl.multiple_of(t * ta, ta), ta)

    @pl.when(jnp.max(bound) > MAX_SAFE_SHIFT)
    def _():
        mp_scr[...] = jnp.full(mp_scr.shape, NEG_BIG, F32)

        def body(t, carry):
            st = jnp.dot(kp_scr[key_tile(t), 0:LANES], qq_scr[0:LANES, :], preferred_element_type=F32)
            mp_scr[...] = jnp.maximum(mp_scr[...], jnp.max(st, axis=0, keepdims=True))
            return carry

        lax.fori_loop(0, n_all // ta, body, 0)
        set_shift(mp_scr[0:1, :])

    mp_scr[...] = jnp.zeros(mp_scr.shape, F32)

    def body_a(t, carry):
        for u in range(unroll):
            r = key_tile(t * unroll + u)
            p = jnp.exp2(jnp.dot(kp_scr[r, :], qq_scr[...], preferred_element_type=F32))
            pt_scr[r, :] = p.astype(BF16)
            mp_scr[...] += jnp.sum(p.reshape(ta // SUBLANES, SUBLANES, rows), axis=0)
        return carry

    lax.fori_loop(0, n_all // (ta * unroll), body_a, 0)

    acc = jnp.dot(vt_scr[...], pt_scr[...], preferred_element_type=F32)
    ot = acc / jnp.sum(mp_scr[...], axis=0, keepdims=True)
    d = (ot[:, 0:tq] - lam_ref[0] * ot[:, tq:rows]).T
    ms = jnp.mean(d * d, axis=-1, keepdims=True)
    y = d * lax.rsqrt(ms + EPS) * gn_ref[...] * (1.0 - LAMBDA_INIT)
    o_ref[...] = y.astype(BF16)


def _diff_attention(lam, p_lat, p_ctx, diff_norm_g, gmat, batch, seq, ctx_len):
    tq = min(1024, seq)
    tk = min(512, seq)
    nq, nk = seq // tq, seq // tk
    n_all = ctx_len + seq
    ta = next(t for t in (528, 512, 384, 320, 256, 128) if n_all % t == 0)
    unroll = math.gcd(n_all // ta, KV_UNROLL)
    assert ctx_len % LANES == 0
    kern = functools.partial(_dattn_kernel, tq=tq, tk=tk, n_kv=nk, ta=ta, unroll=unroll, ctx_len=ctx_len)
    return pl.pallas_call(
        kern,
        out_shape=jax.ShapeDtypeStruct((batch * seq, DIFF_HEADS * DIFF_DV), BF16),
        grid=(batch, DIFF_HEADS, nq),
        in_specs=[pl.BlockSpec(memory_space=pltpu.SMEM),
                  pl.BlockSpec((tq, LANES), lambda b, h, qi: (b * nq + qi, 48 + h)),
                  pl.BlockSpec((ctx_len, LANES), lambda b, h, qi: (b, 56 + h)),
                  pl.BlockSpec((ctx_len, LANES), lambda b, h, qi: (b, 64 + h)),
                  pl.BlockSpec((seq, LANES), lambda b, h, qi: (b, 56 + h)),
                  pl.BlockSpec((seq, LANES), lambda b, h, qi: (b, 64 + h)),
                  pl.BlockSpec((1, LANES), lambda b, h, qi: (0, 0)),
                  pl.BlockSpec((LANES, LANES), lambda b, h, qi: (0, 0))],
        out_specs=pl.BlockSpec((tq, LANES), lambda b, h, qi: (b * nq + qi, h)),
        scratch_shapes=[pltpu.VMEM((2 * LANES, 2 * tq), BF16), pltpu.VMEM((1, LANES), F32),
                        pltpu.VMEM((SUBLANES, 2 * tq), F32), pltpu.VMEM((n_all, 2 * LANES), BF16),
                        pltpu.VMEM((DIFF_DV, n_all), BF16), pltpu.VMEM((n_all, 2 * tq), BF16)],
        compiler_params=_params(("arbitrary",) * 3, 58),
        name="dattn",
    )(lam, p_lat, p_ctx, p_ctx, p_lat, p_lat, diff_norm_g.reshape(1, DIFF_DV), gmat)


def _merge_kernel(yr_ref, yd_ref, ga_ref, gb_ref, x_ref, gatea_ref, shf_ref, scf_ref, n2_ref,
                  wr_ref, wd_ref, wo_ref, rwh_ref, rwl_ref, rb_ref,
                  xn_ref, hp_ref, meta_ref, gate_ref, cnt_ref, run_scr, *, tm):
    i = pl.program_id(0)

    @pl.when(i == 0)
    def _():
        run_scr[...] = jnp.zeros(run_scr.shape, F32)

    yr = jnp.dot(yr_ref[...], wr_ref[...], preferred_element_type=F32)
    yd = jnp.dot(yd_ref[...], wd_ref[...], preferred_element_type=F32)
    m = (jax.nn.sigmoid(ga_ref[...].astype(F32)) * yr + jax.nn.sigmoid(gb_ref[...].astype(F32)) * yd)
    z = jnp.dot(m.astype(BF16), wo_ref[...], preferred_element_type=F32)
    xn = x_ref[...] + gatea_ref[0] * z
    xn_ref[...] = xn

    ms = jnp.mean(xn * xn, axis=-1, keepdims=True)
    h2 = xn * lax.rsqrt(ms + EPS) * n2_ref[...]
    h2 = h2 * (1.0 + scf_ref[0]) + shf_ref[0]
    h_hi = h2.astype(BF16)
    bits = pltpu.bitcast(h_hi.astype(F32), U32)
    half = D_MODEL // 2
    hp_ref[...] = (bits[:, :half] >> 16) | (bits[:, half:] & jnp.uint32(0xFFFF0000))

    h_lo = (h2 - h_hi.astype(F32)).astype(BF16)
    logits = (jnp.dot(h_hi, rwh_ref[...], preferred_element_type=F32)
              + jnp.dot(h_lo, rwh_ref[...], preferred_element_type=F32)
              + jnp.dot(h_hi, rwl_ref[...], preferred_element_type=F32) + rb_ref[...])
    lane = lax.broadcasted_iota(I32, (tm, LANES), 1)
    lanef = lane.astype(F32)
    work = logits
    vals, idxs = [], []
    for _ in range(TOP_K):
        mk = jnp.max(work, axis=1, keepdims=True)
        ik = jnp.min(jnp.where(work == mk, lanef, float(LANES)), axis=1, keepdims=True)
        vals.append(mk)
        idxs.append(ik)
        work = jnp.where(lanef == ik, -jnp.inf, work)
    ex = [jnp.exp(v - vals[0]) for v in vals]
    den = ex[0] + ex[1] + ex[2] + ex[3]

    onehot = jnp.zeros((tm, LANES), F32)
    for ik in idxs:
        onehot = onehot + jnp.where(lanef == ik, 1.0, 0.0)
    ri = lax.broadcasted_iota(I32, (tm, tm), 0)
    ci = lax.broadcasted_iota(I32, (tm, tm), 1)
    tri = jnp.where(ri > ci, 1.0, 0.0).astype(BF16)
    base = run_scr[0:1, :] + jnp.dot(tri, onehot.astype(BF16), preferred_element_type=F32)
    run_scr[...] = run_scr[...] + jnp.sum(onehot, axis=0, keepdims=True)

    meta = jnp.zeros((tm, LANES), F32)
    gate_out = jnp.zeros((tm, LANES), F32)
    for k in range(TOP_K):
        rk = jnp.sum(jnp.where(lanef == idxs[k], base, 0.0), axis=1, keepdims=True)
        meta = jnp.where(lane == k, idxs[k], jnp.where(lane == TOP_K + k, rk, meta))
        gate_out = jnp.where(lane == k, ex[k] / den, gate_out)
    meta_ref[...] = meta.T[0:2 * TOP_K, :]
    gate_ref[...] = gate_out
    cnt_ref[...] = run_scr[...]


def _merge(y_ret, y_diff, p_lat, x2, g_a, sh_f, sc_f, norm2_g, w_r, w_d, w_o, rw_hi, rw_lo, rb, seq):
    n = x2.shape[0]
    tm = min(512, seq)
    tpb = seq // tm
    kern = functools.partial(_merge_kernel, tm=tm)
    mod_spec = pl.BlockSpec((1, 1, D_MODEL), lambda i: (i // tpb, 0, 0))
    const = lambda shape: pl.BlockSpec(shape, lambda i: (0,) * len(shape))
    tok = lambda w: pl.BlockSpec((tm, w), lambda i: (i, 0))
    return pl.pallas_call(
        kern,
        out_shape=(jax.ShapeDtypeStruct((n, D_MODEL), F32),
                   jax.ShapeDtypeStruct((n, D_MODEL // 2), U32),
                   jax.ShapeDtypeStruct((2 * TOP_K, n), F32),
                   jax.ShapeDtypeStruct((n, LANES), F32),
                   jax.ShapeDtypeStruct((8, LANES), F32)),
        grid=(n // tm,),
        in_specs=[tok(RET_HEADS * RET_DV), tok(D_MODEL),
                  pl.BlockSpec((tm, COL_TILE), lambda i: (i, 9)),
                  pl.BlockSpec((tm, COL_TILE), lambda i: (i, 10)),
                  tok(D_MODEL), mod_spec, mod_spec, mod_spec, const((1, D_MODEL)),
                  const((RET_HEADS * RET_DV, D_MODEL)), const((D_MODEL, D_MODEL)), const((D_MODEL, D_MODEL)),
                  const((D_MODEL, LANES)), const((D_MODEL, LANES)), const((1, LANES))],
        out_specs=(tok(D_MODEL), tok(D_MODEL // 2), pl.BlockSpec((2 * TOP_K, tm), lambda i: (0, i)),
                   tok(LANES), const((8, LANES))),
        scratch_shapes=[pltpu.VMEM((8, LANES), F32)],
        compiler_params=_params(("arbitrary",), 56),
        name="merge",
    )(y_ret, y_diff, p_lat, p_lat, x2, g_a, sh_f, sc_f, norm2_g, w_r, w_d, w_o, rw_hi, rw_lo, rb)


def _dispatch_kernel(dest_ref, fill_ref, nb_ref, hp_ref, xb_ref, zero_scr, stage, sems, zsem, *,
                     tm, n_tok, n_blocks, n_steps):
    i = pl.program_id(0)

    @pl.when(i == 0)
    def _():
        zero_scr[...] = jnp.zeros(zero_scr.shape, U32)
        fill = zero_scr.shape[0]
        for e in range(N_EXPERTS):
            start = pl.multiple_of(fill_ref[e], SUBLANES)
            pltpu.make_async_copy(zero_scr, xb_ref.at[pl.ds(start, fill), :], zsem).start()
        for e in range(N_EXPERTS):
            pltpu.make_async_copy(zero_scr, xb_ref.at[pl.ds(0, fill), :], zsem).wait()
        blk = zero_scr.at[pl.ds(0, MOE_BLK), :]

        def start_blk(b, carry):
            pltpu.make_async_copy(blk, xb_ref.at[pl.ds(pl.multiple_of(b * MOE_BLK, MOE_BLK), MOE_BLK), :], zsem).start()
            return carry

        def wait_blk(b, carry):
            pltpu.make_async_copy(blk, xb_ref.at[pl.ds(0, MOE_BLK), :], zsem).wait()
            return carry

        lax.fori_loop(nb_ref[0], n_blocks, start_blk, 0)
        lax.fori_loop(nb_ref[0], n_blocks, wait_blk, 0)

    def drain(slot):
        for _ in range(TOP_K):
            pltpu.make_async_copy(stage.at[slot], xb_ref.at[pl.ds(0, tm), :], sems.at[slot]).wait()

    for slot in range(2):
        @pl.when(lax.rem(i, 2) == slot)
        def _():
            stage[slot] = hp_ref[...]

            def body(g, carry):
                r0 = pl.multiple_of(g * SUBLANES, SUBLANES)
                for j in range(SUBLANES):
                    for k in range(TOP_K):
                        d = dest_ref[k * n_tok + i * tm + r0 + j]
                        pltpu.make_async_copy(stage.at[slot, pl.ds(r0 + j, 1), :], xb_ref.at[pl.ds(d, 1), :],
                                              sems.at[slot]).start()
                return carry

            lax.fori_loop(0, tm // SUBLANES, body, 0)

            @pl.when(i > 0)
            def _():
                drain(1 - slot)

            @pl.when(i == n_steps - 1)
            def _():
                drain(slot)


def _dispatch(dest, fill_start, n_used, hp, n_rows):
    n = hp.shape[0]
    tm = min(512, n)
    kern = functools.partial(_dispatch_kernel, tm=tm, n_tok=n, n_blocks=n_rows // MOE_BLK, n_steps=n // tm)
    return pl.pallas_call(
        kern,
        out_shape=jax.ShapeDtypeStruct((n_rows, D_MODEL // 2), U32),
        grid_spec=pltpu.PrefetchScalarGridSpec(
            num_scalar_prefetch=3,
            grid=(n // tm,),
            in_specs=[pl.BlockSpec((tm, D_MODEL // 2), lambda i, d, f, nb: (i, 0))],
            out_specs=pl.BlockSpec(memory_space=pl.ANY),
            scratch_shapes=[pltpu.VMEM((MOE_BLK + SUBLANES, D_MODEL // 2), U32),
                            pltpu.VMEM((2, tm, D_MODEL // 2), U32),
                            pltpu.SemaphoreType.DMA((2,)), pltpu.SemaphoreType.DMA(())]),
        compiler_params=_params(("arbitrary",), 32),
        name="dispatch",
    )(dest, fill_start, n_used, hp)


def _expert_kernel(be_ref, nb_ref, nxt_ref, xb_ref, w1_hbm, b1_ref, w2_hbm, b2_ref, y_ref,
                   w1f_scr, w2f_scr, w1b_scr, w2b_scr, x_scr, grp_scr, sems):
    i = pl.program_id(0)
    e = be_ref[i]
    prev = be_ref[jnp.maximum(i - 1, 0)]
    active = i < nb_ref[0]

    def weight_copies(expert, slot):
        return (pltpu.make_async_copy(w1_hbm.at[expert], w1f_scr.at[slot], sems.at[0, slot]),
                pltpu.make_async_copy(w2_hbm.at[expert], w2f_scr.at[slot], sems.at[1, slot]))

    @pl.when(i == 0)
    def _():
        grp_scr[0] = 0
        for cp in weight_copies(e, 0):
            cp.start()

    @pl.when(active & (i > 0) & (e != prev))
    def _():
        grp_scr[0] = grp_scr[0] + 1

    @pl.when(active & ((i == 0) | (e != prev)))
    def _():
        slot = grp_scr[0] % 2
        for cp in weight_copies(e, slot):
            cp.wait()
        w1b_scr[...] = w1f_scr[slot].astype(BF16)
        w2b_scr[...] = w2f_scr[slot].astype(BF16)
        nxt = nxt_ref[e]

        @pl.when(nxt >= 0)
        def _():
            for cp in weight_copies(nxt, 1 - slot):
                cp.start()

    @pl.when(active)
    def _():
        xu = xb_ref[...]
        half = D_MODEL // 2
        x_scr[:, 0:half] = pltpu.bitcast(xu << 16, F32).astype(BF16)
        x_scr[:, half:] = pltpu.bitcast(xu & jnp.uint32(0xFFFF0000), F32).astype(BF16)
        hh = jnp.dot(x_scr[...], w1b_scr[...], preferred_element_type=F32) + b1_ref[0]
        glu = jnp.minimum(hh[:, :D_FF], SWIGLU_LIMIT)
        lin = jnp.clip(hh[:, D_FF:], -SWIGLU_LIMIT, SWIGLU_LIMIT)
        act = glu * jax.nn.sigmoid(SWIGLU_ALPHA * glu) * (lin + 1.0)
        y_ref[...] = jnp.dot(act.astype(BF16), w2b_scr[...], preferred_element_type=F32) + b2_ref[0]

    @pl.when(i >= nb_ref[0])
    def _():
        y_ref[...] = jnp.zeros(y_ref.shape, F32)


def _experts(block_e, n_used, next_e, xb, w1, b1, w2, b2):
    n_blocks = block_e.shape[0]
    rows = n_blocks * MOE_BLK
    return pl.pallas_call(
        _expert_kernel,
        out_shape=jax.ShapeDtypeStruct((rows, D_MODEL), F32),
        grid_spec=pltpu.PrefetchScalarGridSpec(
            num_scalar_prefetch=3,
            grid=(n_blocks,),
            in_specs=[pl.BlockSpec((MOE_BLK, D_MODEL // 2), lambda i, be, nb, nx: (jnp.minimum(i, nb[0] - 1), 0)),
                      pl.BlockSpec(memory_space=pl.ANY),
                      pl.BlockSpec((1, 1, 2 * D_FF), lambda i, be, nb, nx: (be[i], 0, 0)),
                      pl.BlockSpec(memory_space=pl.ANY),
                      pl.BlockSpec((1, 1, D_MODEL), lambda i, be, nb, nx: (be[i], 0, 0))],
            out_specs=pl.BlockSpec((MOE_BLK, D_MODEL), lambda i, be, nb, nx: (i, 0)),
            scratch_shapes=[pltpu.VMEM((2, D_MODEL, 2 * D_FF), F32), pltpu.VMEM((2, D_FF, D_MODEL), F32),
                            pltpu.VMEM((D_MODEL, 2 * D_FF), BF16), pltpu.VMEM((D_FF, D_MODEL), BF16),
                            pltpu.VMEM((MOE_BLK, D_MODEL), BF16),
                            pltpu.SMEM((1,), I32), pltpu.SemaphoreType.DMA((2, 2))]),
        compiler_params=_params(("arbitrary",), 56),
        name="expert",
    )(block_e, n_used, next_e, xb, w1, b1.reshape(N_EXPERTS, 1, 2 * D_FF), w2, b2.reshape(N_EXPERTS, 1, D_MODEL))


def _combine_kernel(dest_ref, yb_ref, gate_ref, xn_ref, gf_ref, o_ref, buf, sems, *, tm, n_tok, n_steps):
    i = pl.program_id(0)

    def issue(tile, slot):
        def body(g, carry):
            r0 = pl.multiple_of(g * SUBLANES, SUBLANES)
            for j in range(SUBLANES):
                for k in range(TOP_K):
                    d = dest_ref[k * n_tok + tile * tm + r0 + j]
                    pltpu.make_async_copy(yb_ref.at[pl.ds(d, 1), :], buf.at[slot, k, pl.ds(r0 + j, 1), :],
                                          sems.at[slot]).start()
            return carry

        lax.fori_loop(0, tm // SUBLANES, body, 0)

    def finish(slot):
        for k in range(TOP_K):
            pltpu.make_async_copy(yb_ref.at[pl.ds(0, tm), :], buf.at[slot, k], sems.at[slot]).wait()
        g = gate_ref[...]
        y = g[:, 0:1] * buf[slot, 0]
        for k in range(1, TOP_K):
            y = y + g[:, k:k + 1] * buf[slot, k]
        o_ref[...] = xn_ref[...] + gf_ref[0] * y

    @pl.when(i == 0)
    def _():
        issue(0, 0)

    for slot in range(2):
        @pl.when(lax.rem(i, 2) == slot)
        def _():
            @pl.when(i + 1 < n_steps)
            def _():
                issue(i + 1, 1 - slot)

            finish(slot)


def _combine(dest, yb, gate, xn, g_f, seq):
    n = xn.shape[0]
    tm = min(512, seq)
    tpb = seq // tm
    kern = functools.partial(_combine_kernel, tm=tm, n_tok=n, n_steps=n // tm)
    return pl.pallas_call(
        kern,
        out_shape=jax.ShapeDtypeStruct((n, D_MODEL), F32),
        grid_spec=pltpu.PrefetchScalarGridSpec(
            num_scalar_prefetch=1,
            grid=(n // tm,),
            in_specs=[pl.BlockSpec(memory_space=pl.ANY),
                      pl.BlockSpec((tm, LANES), lambda i, d: (i, 0)),
                      pl.BlockSpec((tm, D_MODEL), lambda i, d: (i, 0)),
                      pl.BlockSpec((1, 1, D_MODEL), lambda i, d: (i // tpb, 0, 0))],
            out_specs=pl.BlockSpec((tm, D_MODEL), lambda i, d: (i, 0)),
            scratch_shapes=[pltpu.VMEM((2, TOP_K, tm, D_MODEL), F32), pltpu.SemaphoreType.DMA((2,))]),
        compiler_params=_params(("arbitrary",), 40),
        name="combine",
    )(dest, yb, gate, xn, g_f)


def _layer(x, ctx, c, c_ctx, norm1_g, norm2_g, w_mod, b_mod, w_in, ret_decay_logit, ret_norm_g,
           diff_q_norm_g, diff_k_norm_g, diff_lambda, diff_norm_g, w_br_ret, w_br_diff, w_out,
           router_w, router_b, exp_w1, exp_b1, exp_w2, exp_b2):
    batch, seq, d = x.shape
    ctx_len = ctx.shape[1]
    assert d == D_MODEL and seq % GRID_W == 0 and batch + 1 <= 8
    n_tok = batch * seq

    cc = jnp.zeros((8, D_MODEL), F32).at[:batch].set(c).at[batch].set(c_ctx)
    mod = _mod(cc, w_mod, b_mod)
    sh_a, sc_a, g_a, sh_f, sc_f, g_f = [mod[:batch, i * D_MODEL:(i + 1) * D_MODEL].reshape(batch, 1, D_MODEL)
                                         for i in range(6)]
    csh_a = mod[batch:batch + 1, 0:D_MODEL].reshape(1, 1, D_MODEL)
    csc_a = mod[batch:batch + 1, D_MODEL:2 * D_MODEL].reshape(1, 1, D_MODEL)

    w_in_bf = w_in.astype(BF16)
    g1 = norm1_g.reshape(1, D_MODEL)
    tile = lambda g: jnp.tile(g.astype(F32), 2)
    qkg = jnp.zeros((8, LANES), F32).at[0].set(tile(diff_q_norm_g) * (DIFF_DH ** -0.5 * LOG2E)).at[1].set(tile(diff_k_norm_g))
    lane = jnp.arange(2 * LANES)
    same_group = lane[:, None] // DIFF_DH == lane[None, :] // DIFF_DH
    gmean = jnp.where(same_group, 1.0 / DIFF_DH, 0.0).astype(BF16)
    gmat = same_group[:LANES, :LANES].astype(BF16)
    x2 = x.reshape(n_tok, D_MODEL)
    tm = min(1024, seq)
    p_lat = _inproj(x2, g1, sh_a, sc_a, w_in_bf, qkg, gmean, _rope_tables(seq), tm, seq // tm)
    p_ctx = _inproj(ctx.reshape(batch * ctx_len, D_MODEL), g1, csh_a, csc_a, w_in_bf, qkg, gmean,
                    _identity_tables(ctx_len), ctx_len, 1)

    lg = jax.nn.log_sigmoid(ret_decay_logit.astype(F32))
    y_ret = _retention(lg, p_lat, p_ctx, ret_norm_g, batch, seq, ctx_len)

    lp = diff_lambda.astype(F32)
    lam = (jnp.exp(jnp.sum(lp[0] * lp[1])) - jnp.exp(jnp.sum(lp[2] * lp[3])) + LAMBDA_INIT).reshape(1)
    y_diff = _diff_attention(lam, p_lat, p_ctx, diff_norm_g, gmat, batch, seq, ctx_len)

    rw = jnp.zeros((D_MODEL, LANES), F32).at[:, :N_EXPERTS].set(router_w)
    rw_hi = rw.astype(BF16)
    rw_lo = (rw - rw_hi.astype(F32)).astype(BF16)
    rb = jnp.full((1, LANES), NEG_BIG, F32).at[0, :N_EXPERTS].set(router_b)
    xn, hp, meta, gate4, cnt = _merge(
        y_ret, y_diff, p_lat, x2, g_a, sh_f, sc_f, norm2_g.reshape(1, D_MODEL),
        w_br_ret.astype(BF16), w_br_diff.astype(BF16), w_out.astype(BF16), rw_hi, rw_lo, rb, seq)

    counts = cnt[0, :N_EXPERTS].astype(I32)
    padded = (counts + MOE_BLK - 1) // MOE_BLK * MOE_BLK
    pad_end = jnp.cumsum(padded)
    pad_start = pad_end - padded
    n_pairs = n_tok * TOP_K
    n_blocks = n_pairs // MOE_BLK + N_EXPERTS
    meta_i = meta.astype(I32)
    is_e = meta_i[None, :TOP_K] == jnp.arange(N_EXPERTS, dtype=I32)[:, None, None]
    dest = (jnp.sum(jnp.where(is_e, pad_start[:, None, None], 0), axis=0) + meta_i[TOP_K:]).reshape(n_pairs)
    block_start = jnp.arange(n_blocks, dtype=I32) * MOE_BLK
    block_e = jnp.minimum(jnp.sum((pad_end[None, :] <= block_start[:, None]).astype(I32), axis=1), N_EXPERTS - 1)
    n_used = (pad_end[-1] // MOE_BLK).reshape(1).astype(I32)

    fill_start = ((pad_start + counts) // SUBLANES * SUBLANES).astype(I32)
    xb = _dispatch(dest, fill_start, n_used, hp, (n_blocks + 1) * MOE_BLK)
    e_ids = jnp.arange(N_EXPERTS, dtype=I32)
    later = jnp.where((counts[None, :] > 0) & (e_ids[None, :] > e_ids[:, None]), e_ids[None, :], N_EXPERTS)
    next_e = jnp.min(later, axis=1)
    next_e = jnp.where(next_e == N_EXPERTS, -1, next_e).astype(I32)
    yb = _experts(block_e, n_used, next_e, xb, exp_w1, exp_b1, exp_w2, exp_b2)
    out = _combine(dest, yb, gate4, xn, g_f, seq)
    return out.reshape(batch, seq, D_MODEL)


def kernel(x, c, ctx, c_ctx, norm1_g, norm2_g, w_mod, b_mod, w_in, ret_decay_logit, ret_norm_g, diff_q_norm_g, diff_k_norm_g, diff_lambda, diff_norm_g, w_br_ret, w_br_diff, w_out, router_w, router_b, exp_w1, exp_b1, exp_w2, exp_b2):
    assert norm1_g.shape[0] == 1, "single-layer block"
    return _layer(x, ctx, c, c_ctx, norm1_g[0], norm2_g[0], w_mod[0], b_mod[0], w_in[0], ret_decay_logit[0],
                  ret_norm_g[0], diff_q_norm_g[0], diff_k_norm_g[0], diff_lambda[0], diff_norm_g[0],
                  w_br_ret[0], w_br_diff[0], w_out[0], router_w[0], router_b[0],
                  exp_w1[0], exp_b1[0], exp_w2[0], exp_b2[0])
```

```python
import functools
import math

import numpy as np

import jax
import jax.numpy as jnp
from jax import lax
from jax.experimental import pallas as pl
from jax.experimental.pallas import tpu as pltpu

F32 = jnp.float32
BF16 = jnp.bfloat16
U32 = jnp.uint32
I32 = jnp.int32

D_MODEL = 1024
GRID_W = 64
RET_HEADS = 4
RET_DK = 256
RET_DV = 512
DIFF_DH = 64
DIFF_HEADS = 8
DIFF_DV = 128
N_EXPERTS = 32
TOP_K = 4
D_FF = 1024
SWIGLU_LIMIT = 7.0
SWIGLU_ALPHA = 1.702
ROPE_BASE = 10000.0
EPS = 1e-6
LAMBDA_INIT = 0.8 - 0.6 * math.exp(-0.3 * 0)

IN_COLS = 11264
COL_TILE = 1024
LANES = 128
SUBLANES = 8
MOE_BLK = 256
NEG_BIG = -1e30
LOG2E = 1.4426950408889634
SHIFT_SLACK = 1.0 + 2.0 ** -6
MAX_SAFE_SHIFT = 60.0
RET_SUB = 8
KV_UNROLL = 4
HIGHEST = lax.Precision.HIGHEST
MIB = 1024 * 1024


def _params(sem, vmem_mib):
    return pltpu.CompilerParams(dimension_semantics=sem, vmem_limit_bytes=vmem_mib * MIB)


def _mod_kernel(c_ref, w_ref, b_ref, o_ref):
    c = c_ref[...]
    s = c * jax.nn.sigmoid(c)
    o_ref[...] = jnp.dot(s, w_ref[...], preferred_element_type=F32, precision=HIGHEST) + b_ref[...]


def _mod(cc, w_mod, b_mod):
    n = w_mod.shape[1]
    tn = 1024
    return pl.pallas_call(
        _mod_kernel,
        out_shape=jax.ShapeDtypeStruct((8, n), F32),
        grid=(n // tn,),
        in_specs=[pl.BlockSpec((8, D_MODEL), lambda j: (0, 0)),
                  pl.BlockSpec((D_MODEL, tn), lambda j: (0, j)),
                  pl.BlockSpec((1, tn), lambda j: (0, j))],
        out_specs=pl.BlockSpec((8, tn), lambda j: (0, j)),
        compiler_params=_params(("arbitrary",), 32),
        name="mod",
    )(cc, w_mod, b_mod.reshape(1, n))


def _inproj_kernel(x_ref, g_ref, sh_ref, sc_ref, w_ref, qkg_ref, gmat_ref,
                   cr_ref, sr_ref, cc_ref, sc2_ref, cd_ref, sa_ref, sb_ref,
                   o_ref, h_scr, acc_scr):
    j = pl.program_id(1)

    @pl.when(j == 0)
    def _():
        xf = x_ref[...]
        ms = jnp.mean(xf * xf, axis=-1, keepdims=True)
        y = xf * lax.rsqrt(ms + EPS) * g_ref[...]
        h_scr[...] = (y * (1.0 + sc_ref[0]) + sh_ref[0]).astype(BF16)

    is_ret = j <= 1
    is_dqk = (j == 6) | (j == 7)

    def project():
        return jnp.dot(h_scr[...], w_ref[...], preferred_element_type=F32)

    @pl.when(is_ret)
    def _():
        acc_scr[...] = project()
        scale = jnp.where(j == 0, RET_DK ** -0.5, 1.0).astype(F32)
        for b in range(COL_TILE // LANES):
            xb = acc_scr[:, b * LANES:(b + 1) * LANES]
            cos = cr_ref[...] if b % 2 == 0 else cc_ref[...]
            sin = sr_ref[...] if b % 2 == 0 else sc2_ref[...]
            o = (xb * cos + pltpu.roll(xb, 64, 1) * sin) * scale
            o_ref[:, b * LANES:(b + 1) * LANES] = o.astype(BF16)

    @pl.when(is_dqk)
    def _():
        acc_scr[...] = project()
        g = jnp.where(j == 6, qkg_ref[0:1, :], qkg_ref[1:2, :])
        for b2 in range(COL_TILE // (2 * LANES)):
            x2 = acc_scr[:, b2 * 2 * LANES:(b2 + 1) * 2 * LANES]
            ms2 = jnp.dot((x2 * x2).astype(BF16), gmat_ref[...], preferred_element_type=F32)
            for half in range(2):
                b = 2 * b2 + half
                xb = x2[:, half * LANES:(half + 1) * LANES]
                yn = xb * lax.rsqrt(ms2[:, half * LANES:(half + 1) * LANES] + EPS) * g
                o = yn * cd_ref[...] + pltpu.roll(yn, 16, 1) * sa_ref[...] + pltpu.roll(yn, 112, 1) * sb_ref[...]
                o_ref[:, b * LANES:(b + 1) * LANES] = o.astype(BF16)

    @pl.when(jnp.logical_not(is_ret | is_dqk))
    def _():
        o_ref[...] = project().astype(BF16)


def _inproj(x2, g1, sh, sc, w_bf, qkg, gmat, tables, tm, tiles_per_batch):
    n = x2.shape[0]
    nb = sh.shape[0]
    tab_spec = pl.BlockSpec((tm, LANES), lambda i, j: (i % tiles_per_batch, 0))
    mod_spec = pl.BlockSpec((1, 1, D_MODEL), lambda i, j: (jnp.minimum(i // tiles_per_batch, nb - 1), 0, 0))
    return pl.pallas_call(
        _inproj_kernel,
        out_shape=jax.ShapeDtypeStruct((n, IN_COLS), BF16),
        grid=(n // tm, IN_COLS // COL_TILE),
        in_specs=[pl.BlockSpec((tm, D_MODEL), lambda i, j: (i, 0)),
                  pl.BlockSpec((1, D_MODEL), lambda i, j: (0, 0)),
                  mod_spec, mod_spec,
                  pl.BlockSpec((D_MODEL, COL_TILE), lambda i, j: (0, j)),
                  pl.BlockSpec((8, LANES), lambda i, j: (0, 0)),
                  pl.BlockSpec((2 * LANES, 2 * LANES), lambda i, j: (0, 0))] + [tab_spec] * 7,
        out_specs=pl.BlockSpec((tm, COL_TILE), lambda i, j: (i, j)),
        scratch_shapes=[pltpu.VMEM((tm, D_MODEL), BF16), pltpu.VMEM((tm, COL_TILE), F32)],
        compiler_params=_params(("arbitrary", "arbitrary"), 48),
        name="inproj",
    )(x2, g1, sh, sc, w_bf, qkg, gmat, *tables)


def _rope_tables(seq):
    n_rows = seq // GRID_W
    f32 = np.float32

    def angles(pos, half):
        inv = f32(ROPE_BASE) ** (-np.arange(half, dtype=f32) / f32(half))
        return (pos.astype(f32)[:, None] * inv[None, :]).astype(np.float64)

    ar, ac = angles(np.arange(n_rows), 64), angles(np.arange(GRID_W), 64)
    br, bc = angles(np.arange(n_rows), 16), angles(np.arange(GRID_W), 16)
    zr, zc = np.zeros_like(br), np.zeros_like(bc)
    cat = lambda parts, reps=1: np.tile(np.concatenate(parts, axis=1), (1, reps)).astype(f32)
    by_row = lambda t: jnp.repeat(jnp.asarray(t), GRID_W, axis=0)
    by_col = lambda t: jnp.tile(jnp.asarray(t), (n_rows, 1))
    cr = by_row(cat([np.cos(ar), np.cos(ar)]))
    sr = by_row(cat([-np.sin(ar), np.sin(ar)]))
    cc = by_col(cat([np.cos(ac), np.cos(ac)]))
    sc = by_col(cat([-np.sin(ac), np.sin(ac)]))
    cd = by_row(cat([np.cos(br), np.cos(br), zr, zr], 2)) + by_col(cat([zc, zc, np.cos(bc), np.cos(bc)], 2))
    sa = by_row(cat([zr, np.sin(br), zr, zr], 2)) + by_col(cat([zc, zc, zc, np.sin(bc)], 2))
    sb = by_row(cat([-np.sin(br), zr, zr, zr], 2)) + by_col(cat([zc, zc, -np.sin(bc), zc], 2))
    return [cr, sr, cc, sc, cd, sa, sb]


def _identity_tables(seq):
    one = jnp.ones((seq, LANES), F32)
    zero = jnp.zeros((seq, LANES), F32)
    return [one, zero, one, zero, one, zero, zero]


def _tn_dot(a, b):
    return lax.dot_general(a, b, (((0,), (0,)), ((), ())), preferred_element_type=F32)


def _nt_dot(a, b):
    return lax.dot_general(a, b, (((1,), (1,)), ((), ())), preferred_element_type=F32)


def _ret_kernel(lg_ref, q_ref, k_ref, v_ref, g_ref, ck_ref, cv_ref, gn_ref, o_ref,
                sf_scr, sb_scr, ob_scr, *, chunk, sub, n_blocks, ctx_len):
    h = pl.program_id(1)
    p = pl.program_id(2)
    c = pl.program_id(3)
    lgf = lg_ref[0, h]
    lgb = lg_ref[1, h]

    def col_iota(n):
        return lax.broadcasted_iota(I32, (n, 1), 0).astype(F32)

    def vexp(s):
        return jnp.exp(jnp.zeros((1, 1), F32) + s)

    @pl.when((p == 0) & (c == 0))
    def _():
        jc = col_iota(ctx_len)
        kc = ck_ref[...].astype(F32)
        vc = cv_ref[...]
        sf_scr[...] = _tn_dot((kc * jnp.exp(lgf * (ctx_len - 1.0 - jc))).astype(BF16), vc)
        sb_scr[...] = _tn_dot((kc * jnp.exp(lgb * jc)).astype(BF16), vc)

    ic = col_iota(chunk)

    def full_bf16(col):
        return jnp.broadcast_to(col, (chunk, RET_DK)).astype(BF16)

    @pl.when(p == 0)
    def _():
        q_decay = full_bf16(jnp.exp(lgb * (chunk - ic)))
        k_decay = full_bf16(jnp.exp(lgb * ic))
        s_decay = vexp(lgb * chunk)
        for j in reversed(range(sub)):
            loc = pl.ds(j * chunk, chunk)
            glob = pl.ds(pl.multiple_of(((n_blocks - 1 - c) * sub + j) * chunk, chunk), chunk)
            qb = q_ref[loc, :] * q_decay
            ob_scr[glob, :] = jnp.dot(qb, sb_scr[...].astype(BF16), preferred_element_type=F32)
            kb = k_ref[loc, :] * k_decay
            sb_scr[...] = s_decay * sb_scr[...] + _tn_dot(kb, v_ref[loc, :])

    @pl.when(p == 1)
    def _():
        ri = lax.broadcasted_iota(I32, (chunk, chunk), 0)
        ci = lax.broadcasted_iota(I32, (chunk, chunk), 1)
        d = (ri - ci).astype(F32)
        mask = jnp.where(d > 0, jnp.exp(lgf * jnp.maximum(d, 0.0)),
                         jnp.where(d < 0, jnp.exp(lgb * jnp.maximum(-d, 0.0)), 2.0))
        q_decay = full_bf16(jnp.exp(lgf * (ic + 1.0)))
        k_decay = full_bf16(jnp.exp(lgf * (chunk - 1.0 - ic)))
        s_decay = vexp(lgf * chunk)
        for j in range(sub):
            loc = pl.ds(j * chunk, chunk)
            glob = pl.ds(pl.multiple_of((c * sub + j) * chunk, chunk), chunk)
            a = (_nt_dot(q_ref[loc, :], k_ref[loc, :]) * mask).astype(BF16)
            qf = q_ref[loc, :] * q_decay
            o = (jnp.dot(a, v_ref[loc, :], preferred_element_type=F32)
                 + jnp.dot(qf, sf_scr[...].astype(BF16), preferred_element_type=F32)
                 + ob_scr[glob, :])
            kf = k_ref[loc, :] * k_decay
            sf_scr[...] = s_decay * sf_scr[...] + _tn_dot(kf, v_ref[loc, :])
            ms = jnp.mean(o * o, axis=-1, keepdims=True)
            y = o * lax.rsqrt(ms + EPS) * gn_ref[0]
            gt = g_ref[loc, :]
            o_ref[loc, :] = y.astype(BF16) * (gt * jax.nn.sigmoid(gt))


def _retention(lg, p_lat, p_ctx, ret_norm_g, batch, seq, ctx_len):
    chunk = min(256, seq)
    sub = math.gcd(seq // chunk, RET_SUB)
    blk = chunk * sub
    nb = seq // blk
    kern = functools.partial(_ret_kernel, chunk=chunk, sub=sub, n_blocks=nb, ctx_len=ctx_len)

    def rows(b, p, c):
        return b * nb + jnp.where(p == 0, nb - 1 - c, c)

    def rows_fwd(b, p, c):
        return b * nb + jnp.where(p == 0, 0, c)

    return pl.pallas_call(
        kern,
        out_shape=jax.ShapeDtypeStruct((batch * seq, RET_HEADS * RET_DV), BF16),
        grid=(batch, RET_HEADS, 2, nb),
        in_specs=[pl.BlockSpec(memory_space=pltpu.SMEM),
                  pl.BlockSpec((blk, RET_DK), lambda b, h, p, c: (rows(b, p, c), h)),
                  pl.BlockSpec((blk, RET_DK), lambda b, h, p, c: (rows(b, p, c), 4 + h)),
                  pl.BlockSpec((blk, RET_DV), lambda b, h, p, c: (rows(b, p, c), 4 + h)),
                  pl.BlockSpec((blk, RET_DV), lambda b, h, p, c: (rows_fwd(b, p, c), 8 + h)),
                  pl.BlockSpec((ctx_len, RET_DK), lambda b, h, p, c: (b, 4 + h)),
                  pl.BlockSpec((ctx_len, RET_DV), lambda b, h, p, c: (b, 4 + h)),
                  pl.BlockSpec((1, 1, RET_DV), lambda b, h, p, c: (h, 0, 0))],
        out_specs=pl.BlockSpec((blk, RET_DV), lambda b, h, p, c: (rows_fwd(b, p, c), h)),
        scratch_shapes=[pltpu.VMEM((RET_DK, RET_DV), F32), pltpu.VMEM((RET_DK, RET_DV), F32),
                        pltpu.VMEM((seq, RET_DV), F32)],
        compiler_params=_params(("arbitrary",) * 4, 48),
        name="ret",
    )(lg, p_lat, p_lat, p_lat, p_lat, p_ctx, p_ctx, ret_norm_g.reshape(RET_HEADS, 1, RET_DV))


def _dattn_kernel(lam_ref, q_ref, ck_ref, cv_ref, k_ref, v_ref, gn_ref, gmat_ref, o_ref,
                  qq_scr, kmax_scr, mp_scr, kp_scr, vt_scr, pt_scr, *, tq, tk, n_kv, ta, unroll, ctx_len):
    rows = 2 * tq
    n_all = kp_scr.shape[0]
    qi = pl.program_id(2)

    def sq_norms(t):
        tf = t.astype(F32)
        return jnp.dot((tf * tf).astype(BF16), gmat_ref[...], preferred_element_type=F32)

    def lane_const(n, hot):
        return jnp.where(lax.broadcasted_iota(I32, (n, LANES), 1) < hot, 1.0, 0.0).astype(BF16)

    @pl.when(qi == 0)
    def _():
        kmax_scr[...] = jnp.max(sq_norms(ck_ref[...]), axis=0, keepdims=True)
        kp_scr[:, LANES:2 * LANES] = lane_const(n_all, 3)
        kp_scr[0:ctx_len, 0:LANES] = ck_ref[...]
        vt_scr[:, 0:ctx_len] = cv_ref[...].astype(F32).T.astype(BF16)

        prep = math.gcd(n_kv, 4)

        def body(c, carry):
            norms = []
            for u in range(prep):
                start = pl.multiple_of((c * prep + u) * tk, tk)
                k = k_ref[pl.ds(start, tk), :]
                norms.append(jnp.max(sq_norms(k), axis=0, keepdims=True))
                kp_scr[pl.ds(pl.multiple_of(ctx_len + start, LANES), tk), 0:LANES] = k
                vt_scr[:, pl.ds(pl.multiple_of(ctx_len + start, LANES), tk)] = (
                    v_ref[pl.ds(start, tk), :].astype(F32).T.astype(BF16))
            kmax_scr[...] = jnp.maximum(kmax_scr[...], functools.reduce(jnp.maximum, norms))
            return carry

        lax.fori_loop(0, n_kv // prep, body, 0)

    qt = q_ref[...].astype(F32).T
    row = lax.broadcasted_iota(I32, (LANES, tq), 0)
    q1t = jnp.where(row < DIFF_DH, qt, 0.0)
    q2t = jnp.where(row >= DIFF_DH, qt, 0.0)
    qq_scr[0:LANES, 0:tq] = q1t.astype(BF16)
    qq_scr[0:LANES, tq:rows] = q2t.astype(BF16)

    def set_shift(shift):
        neg = -shift
        hi = neg.astype(BF16).astype(F32)
        mid = (neg - hi).astype(BF16).astype(F32)
        lo = neg - hi - mid
        row_r = lax.broadcasted_iota(I32, (LANES, rows), 0)
        pieces = jnp.where(row_r == 0, hi, jnp.where(row_r == 1, mid, jnp.where(row_r == 2, lo, 0.0)))
        qq_scr[LANES:2 * LANES, :] = pieces.astype(BF16)

    kmax = kmax_scr[...]
    b1 = jnp.sqrt(jnp.sum(q1t * q1t, axis=0, keepdims=True) * kmax[:, 0:1]) * SHIFT_SLACK
    b2 = jnp.sqrt(jnp.sum(q2t * q2t, axis=0, keepdims=True) * kmax[:, DIFF_DH:DIFF_DH + 1]) * SHIFT_SLACK
    bound = jnp.concatenate([b1, b2], axis=1)
    set_shift(bound)

    def key_tile(t):
        return pl.ds(pl.multiple_of(t * ta, ta), ta)

    @pl.when(jnp.max(bound) > MAX_SAFE_SHIFT)
    def _():
        mp_scr[...] = jnp.full(mp_scr.shape, NEG_BIG, F32)

        def body(t, carry):
            st = jnp.dot(kp_scr[key_tile(t), 0:LANES], qq_scr[0:LANES, :], preferred_element_type=F32)
            mp_scr[...] = jnp.maximum(mp_scr[...], jnp.max(st, axis=0, keepdims=True))
            return carry

        lax.fori_loop(0, n_all // ta, body, 0)
        set_shift(mp_scr[0:1, :])

    mp_scr[...] = jnp.zeros(mp_scr.shape, F32)

    def body_a(t, carry):
        for u in range(unroll):
            r = key_tile(t * unroll + u)
            p = jnp.exp2(jnp.dot(kp_scr[r, :], qq_scr[...], preferred_element_type=F32))
            pt_scr[r, :] = p.astype(BF16)
            mp_scr[...] += jnp.sum(p.reshape(ta // SUBLANES, SUBLANES, rows), axis=0)
        return carry

    lax.fori_loop(0, n_all // (ta * unroll), body_a, 0)

    acc = jnp.dot(vt_scr[...], pt_scr[...], preferred_element_type=F32)
    ot = acc / jnp.sum(mp_scr[...], axis=0, keepdims=True)
    d = (ot[:, 0:tq] - lam_ref[0] * ot[:, tq:rows]).T
    ms = jnp.mean(d * d, axis=-1, keepdims=True)
    y = d * lax.rsqrt(ms + EPS) * gn_ref[...] * (1.0 - LAMBDA_INIT)
    o_ref[...] = y.astype(BF16)


def _diff_attention(lam, p_lat, p_ctx, diff_norm_g, gmat, batch, seq, ctx_len):
    tq = min(1024, seq)
    tk = min(512, seq)
    nq, nk = seq // tq, seq // tk
    n_all = ctx_len + seq
    ta = next(t for t in (528, 512, 384, 320, 256, 128) if n_all % t == 0)
    unroll = math.gcd(n_all // ta, KV_UNROLL)
    assert ctx_len % LANES == 0
    kern = functools.partial(_dattn_kernel, tq=tq, tk=tk, n_kv=nk, ta=ta, unroll=unroll, ctx_len=ctx_len)
    return pl.pallas_call(
        kern,
        out_shape=jax.ShapeDtypeStruct((batch * seq, DIFF_HEADS * DIFF_DV), BF16),
        grid=(batch, DIFF_HEADS, nq),
        in_specs=[pl.BlockSpec(memory_space=pltpu.SMEM),
                  pl.BlockSpec((tq, LANES), lambda b, h, qi: (b * nq + qi, 48 + h)),
                  pl.BlockSpec((ctx_len, LANES), lambda b, h, qi: (b, 56 + h)),
                  pl.BlockSpec((ctx_len, LANES), lambda b, h, qi: (b, 64 + h)),
                  pl.BlockSpec((seq, LANES), lambda b, h, qi: (b, 56 + h)),
                  pl.BlockSpec((seq, LANES), lambda b, h, qi: (b, 64 + h)),
                  pl.BlockSpec((1, LANES), lambda b, h, qi: (0, 0)),
                  pl.BlockSpec((LANES, LANES), lambda b, h, qi: (0, 0))],
        out_specs=pl.BlockSpec((tq, LANES), lambda b, h, qi: (b * nq + qi, h)),
        scratch_shapes=[pltpu.VMEM((2 * LANES, 2 * tq), BF16), pltpu.VMEM((1, LANES), F32),
                        pltpu.VMEM((SUBLANES, 2 * tq), F32), pltpu.VMEM((n_all, 2 * LANES), BF16),
                        pltpu.VMEM((DIFF_DV, n_all), BF16), pltpu.VMEM((n_all, 2 * tq), BF16)],
        compiler_params=_params(("arbitrary",) * 3, 58),
        name="dattn",
    )(lam, p_lat, p_ctx, p_ctx, p_lat, p_lat, diff_norm_g.reshape(1, DIFF_DV), gmat)


def _merge_kernel(yr_ref, yd_ref, ga_ref, gb_ref, x_ref, gatea_ref, shf_ref, scf_ref, n2_ref,
                  wr_ref, wd_ref, wo_ref, rwh_ref, rwl_ref, rb_ref,
                  xn_ref, hp_ref, meta_ref, gate_ref, cnt_ref, run_scr, *, tm):
    i = pl.program_id(0)

    @pl.when(i == 0)
    def _():
        run_scr[...] = jnp.zeros(run_scr.shape, F32)

    yr = jnp.dot(yr_ref[...], wr_ref[...], preferred_element_type=F32)
    yd = jnp.dot(yd_ref[...], wd_ref[...], preferred_element_type=F32)
    m = (jax.nn.sigmoid(ga_ref[...].astype(F32)) * yr + jax.nn.sigmoid(gb_ref[...].astype(F32)) * yd)
    z = jnp.dot(m.astype(BF16), wo_ref[...], preferred_element_type=F32)
    xn = x_ref[...] + gatea_ref[0] * z
    xn_ref[...] = xn

    ms = jnp.mean(xn * xn, axis=-1, keepdims=True)
    h2 = xn * lax.rsqrt(ms + EPS) * n2_ref[...]
    h2 = h2 * (1.0 + scf_ref[0]) + shf_ref[0]
    h_hi = h2.astype(BF16)
    bits = pltpu.bitcast(h_hi.astype(F32), U32)
    half = D_MODEL // 2
    hp_ref[...] = (bits[:, :half] >> 16) | (bits[:, half:] & jnp.uint32(0xFFFF0000))

    h_lo = (h2 - h_hi.astype(F32)).astype(BF16)
    logits = (jnp.dot(h_hi, rwh_ref[...], preferred_element_type=F32)
              + jnp.dot(h_lo, rwh_ref[...], preferred_element_type=F32)
              + jnp.dot(h_hi, rwl_ref[...], preferred_element_type=F32) + rb_ref[...])
    lane = lax.broadcasted_iota(I32, (tm, LANES), 1)
    lanef = lane.astype(F32)
    work = logits
    vals, idxs = [], []
    for _ in range(TOP_K):
        mk = jnp.max(work, axis=1, keepdims=True)
        ik = jnp.min(jnp.where(work == mk, lanef, float(LANES)), axis=1, keepdims=True)
        vals.append(mk)
        idxs.append(ik)
        work = jnp.where(lanef == ik, -jnp.inf, work)
    ex = [jnp.exp(v - vals[0]) for v in vals]
    den = ex[0] + ex[1] + ex[2] + ex[3]

    onehot = jnp.zeros((tm, LANES), F32)
    for ik in idxs:
        onehot = onehot + jnp.where(lanef == ik, 1.0, 0.0)
    ri = lax.broadcasted_iota(I32, (tm, tm), 0)
    ci = lax.broadcasted_iota(I32, (tm, tm), 1)
    tri = jnp.where(ri > ci, 1.0, 0.0).astype(BF16)
    base = run_scr[0:1, :] + jnp.dot(tri, onehot.astype(BF16), preferred_element_type=F32)
    run_scr[...] = run_scr[...] + jnp.sum(onehot, axis=0, keepdims=True)

    meta = jnp.zeros((tm, LANES), F32)
    gate_out = jnp.zeros((tm, LANES), F32)
    for k in range(TOP_K):
        rk = jnp.sum(jnp.where(lanef == idxs[k], base, 0.0), axis=1, keepdims=True)
        meta = jnp.where(lane == k, idxs[k], jnp.where(lane == TOP_K + k, rk, meta))
        gate_out = jnp.where(lane == k, ex[k] / den, gate_out)
    meta_ref[...] = meta.T[0:2 * TOP_K, :]
    gate_ref[...] = gate_out
    cnt_ref[...] = run_scr[...]


def _merge(y_ret, y_diff, p_lat, x2, g_a, sh_f, sc_f, norm2_g, w_r, w_d, w_o, rw_hi, rw_lo, rb, seq):
    n = x2.shape[0]
    tm = min(512, seq)
    tpb = seq // tm
    kern = functools.partial(_merge_kernel, tm=tm)
    mod_spec = pl.BlockSpec((1, 1, D_MODEL), lambda i: (i // tpb, 0, 0))
    const = lambda shape: pl.BlockSpec(shape, lambda i: (0,) * len(shape))
    tok = lambda w: pl.BlockSpec((tm, w), lambda i: (i, 0))
    return pl.pallas_call(
        kern,
        out_shape=(jax.ShapeDtypeStruct((n, D_MODEL), F32),
                   jax.ShapeDtypeStruct((n, D_MODEL // 2), U32),
                   jax.ShapeDtypeStruct((2 * TOP_K, n), F32),
                   jax.ShapeDtypeStruct((n, LANES), F32),
                   jax.ShapeDtypeStruct((8, LANES), F32)),
        grid=(n // tm,),
        in_specs=[tok(RET_HEADS * RET_DV), tok(D_MODEL),
                  pl.BlockSpec((tm, COL_TILE), lambda i: (i, 9)),
                  pl.BlockSpec((tm, COL_TILE), lambda i: (i, 10)),
                  tok(D_MODEL), mod_spec, mod_spec, mod_spec, const((1, D_MODEL)),
                  const((RET_HEADS * RET_DV, D_MODEL)), const((D_MODEL, D_MODEL)), const((D_MODEL, D_MODEL)),
                  const((D_MODEL, LANES)), const((D_MODEL, LANES)), const((1, LANES))],
        out_specs=(tok(D_MODEL), tok(D_MODEL // 2), pl.BlockSpec((2 * TOP_K, tm), lambda i: (0, i)),
                   tok(LANES), const((8, LANES))),
        scratch_shapes=[pltpu.VMEM((8, LANES), F32)],
        compiler_params=_params(("arbitrary",), 56),
        name="merge",
    )(y_ret, y_diff, p_lat, p_lat, x2, g_a, sh_f, sc_f, norm2_g, w_r, w_d, w_o, rw_hi, rw_lo, rb)


def _dispatch_kernel(dest_ref, fill_ref, nb_ref, hp_ref, xb_ref, zero_scr, stage, sems, zsem, *,
                     tm, n_tok, n_blocks, n_steps):
    i = pl.program_id(0)

    @pl.when(i == 0)
    def _():
        zero_scr[...] = jnp.zeros(zero_scr.shape, U32)
        fill = zero_scr.shape[0]
        for e in range(N_EXPERTS):
            start = pl.multiple_of(fill_ref[e], SUBLANES)
            pltpu.make_async_copy(zero_scr, xb_ref.at[pl.ds(start, fill), :], zsem).start()
        for e in range(N_EXPERTS):
            pltpu.make_async_copy(zero_scr, xb_ref.at[pl.ds(0, fill), :], zsem).wait()
        blk = zero_scr.at[pl.ds(0, MOE_BLK), :]

        def start_blk(b, carry):
            pltpu.make_async_copy(blk, xb_ref.at[pl.ds(pl.multiple_of(b * MOE_BLK, MOE_BLK), MOE_BLK), :], zsem).start()
            return carry

        def wait_blk(b, carry):
            pltpu.make_async_copy(blk, xb_ref.at[pl.ds(0, MOE_BLK), :], zsem).wait()
            return carry

        lax.fori_loop(nb_ref[0], n_blocks, start_blk, 0)
        lax.fori_loop(nb_ref[0], n_blocks, wait_blk, 0)

    def drain(slot):
        for _ in range(TOP_K):
            pltpu.make_async_copy(stage.at[slot], xb_ref.at[pl.ds(0, tm), :], sems.at[slot]).wait()

    for slot in range(2):
        @pl.when(lax.rem(i, 2) == slot)
        def _():
            stage[slot] = hp_ref[...]

            def body(g, carry):
                r0 = pl.multiple_of(g * SUBLANES, SUBLANES)
                for j in range(SUBLANES):
                    for k in range(TOP_K):
                        d = dest_ref[k * n_tok + i * tm + r0 + j]
                        pltpu.make_async_copy(stage.at[slot, pl.ds(r0 + j, 1), :], xb_ref.at[pl.ds(d, 1), :],
                                              sems.at[slot]).start()
                return carry

            lax.fori_loop(0, tm // SUBLANES, body, 0)

            @pl.when(i > 0)
            def _():
                drain(1 - slot)

            @pl.when(i == n_steps - 1)
            def _():
                drain(slot)


def _dispatch(dest, fill_start, n_used, hp, n_rows):
    n = hp.shape[0]
    tm = min(512, n)
    kern = functools.partial(_dispatch_kernel, tm=tm, n_tok=n, n_blocks=n_rows // MOE_BLK, n_steps=n // tm)
    return pl.pallas_call(
        kern,
        out_shape=jax.ShapeDtypeStruct((n_rows, D_MODEL // 2), U32),
        grid_spec=pltpu.PrefetchScalarGridSpec(
            num_scalar_prefetch=3,
            grid=(n // tm,),
            in_specs=[pl.BlockSpec((tm, D_MODEL // 2), lambda i, d, f, nb: (i, 0))],
            out_specs=pl.BlockSpec(memory_space=pl.ANY),
            scratch_shapes=[pltpu.VMEM((MOE_BLK + SUBLANES, D_MODEL // 2), U32),
                            pltpu.VMEM((2, tm, D_MODEL // 2), U32),
                            pltpu.SemaphoreType.DMA((2,)), pltpu.SemaphoreType.DMA(())]),
        compiler_params=_params(("arbitrary",), 32),
        name="dispatch",
    )(dest, fill_start, n_used, hp)


def _expert_kernel(be_ref, nb_ref, nxt_ref, xb_ref, w1_hbm, b1_ref, w2_hbm, b2_ref, y_ref,
                   w1f_scr, w2f_scr, w1b_scr, w2b_scr, x_scr, grp_scr, sems):
    i = pl.program_id(0)
    e = be_ref[i]
    prev = be_ref[jnp.maximum(i - 1, 0)]
    active = i < nb_ref[0]

    def weight_copies(expert, slot):
        return (pltpu.make_async_copy(w1_hbm.at[expert], w1f_scr.at[slot], sems.at[0, slot]),
                pltpu.make_async_copy(w2_hbm.at[expert], w2f_scr.at[slot], sems.at[1, slot]))

    @pl.when(i == 0)
    def _():
        grp_scr[0] = 0
        for cp in weight_copies(e, 0):
            cp.start()

    @pl.when(active & (i > 0) & (e != prev))
    def _():
        grp_scr[0] = grp_scr[0] + 1

    @pl.when(active & ((i == 0) | (e != prev)))
    def _():
        slot = grp_scr[0] % 2
        for cp in weight_copies(e, slot):
            cp.wait()
        w1b_scr[...] = w1f_scr[slot].astype(BF16)
        w2b_scr[...] = w2f_scr[slot].astype(BF16)
        nxt = nxt_ref[e]

        @pl.when(nxt >= 0)
        def _():
            for cp in weight_copies(nxt, 1 - slot):
                cp.start()

    @pl.when(active)
    def _():
        xu = xb_ref[...]
        half = D_MODEL // 2
        x_scr[:, 0:half] = pltpu.bitcast(xu << 16, F32).astype(BF16)
        x_scr[:, half:] = pltpu.bitcast(xu & jnp.uint32(0xFFFF0000), F32).astype(BF16)
        hh = jnp.dot(x_scr[...], w1b_scr[...], preferred_element_type=F32) + b1_ref[0]
        glu = jnp.minimum(hh[:, :D_FF], SWIGLU_LIMIT)
        lin = jnp.clip(hh[:, D_FF:], -SWIGLU_LIMIT, SWIGLU_LIMIT)
        act = glu * jax.nn.sigmoid(SWIGLU_ALPHA * glu) * (lin + 1.0)
        y_ref[...] = jnp.dot(act.astype(BF16), w2b_scr[...], preferred_element_type=F32) + b2_ref[0]

    @pl.when(i >= nb_ref[0])
    def _():
        y_ref[...] = jnp.zeros(y_ref.shape, F32)


def _experts(block_e, n_used, next_e, xb, w1, b1, w2, b2):
    n_blocks = block_e.shape[0]
    rows = n_blocks * MOE_BLK
    return pl.pallas_call(
        _expert_kernel,
        out_shape=jax.ShapeDtypeStruct((rows, D_MODEL), F32),
        grid_spec=pltpu.PrefetchScalarGridSpec(
            num_scalar_prefetch=3,
            grid=(n_blocks,),
            in_specs=[pl.BlockSpec((MOE_BLK, D_MODEL // 2), lambda i, be, nb, nx: (jnp.minimum(i, nb[0] - 1), 0)),
                      pl.BlockSpec(memory_space=pl.ANY),
                      pl.BlockSpec((1, 1, 2 * D_FF), lambda i, be, nb, nx: (be[i], 0, 0)),
                      pl.BlockSpec(memory_space=pl.ANY),
                      pl.BlockSpec((1, 1, D_MODEL), lambda i, be, nb, nx: (be[i], 0, 0))],
            out_specs=pl.BlockSpec((MOE_BLK, D_MODEL), lambda i, be, nb, nx: (i, 0)),
            scratch_shapes=[pltpu.VMEM((2, D_MODEL, 2 * D_FF), F32), pltpu.VMEM((2, D_FF, D_MODEL), F32),
                            pltpu.VMEM((D_MODEL, 2 * D_FF), BF16), pltpu.VMEM((D_FF, D_MODEL), BF16),
                            pltpu.VMEM((MOE_BLK, D_MODEL), BF16),
                            pltpu.SMEM((1,), I32), pltpu.SemaphoreType.DMA((2, 2))]),
        compiler_params=_params(("arbitrary",), 56),
        name="expert",
    )(block_e, n_used, next_e, xb, w1, b1.reshape(N_EXPERTS, 1, 2 * D_FF), w2, b2.reshape(N_EXPERTS, 1, D_MODEL))


def _combine_kernel(dest_ref, yb_ref, gate_ref, xn_ref, gf_ref, o_ref, buf, sems, *, tm, n_tok, n_steps):
    i = pl.program_id(0)

    def issue(tile, slot):
        def body(g, carry):
            r0 = pl.multiple_of(g * SUBLANES, SUBLANES)
            for j in range(SUBLANES):
                for k in range(TOP_K):
                    d = dest_ref[k * n_tok + tile * tm + r0 + j]
                    pltpu.make_async_copy(yb_ref.at[pl.ds(d, 1), :], buf.at[slot, k, pl.ds(r0 + j, 1), :],
                                          sems.at[slot]).start()
            return carry

        lax.fori_loop(0, tm // SUBLANES, body, 0)

    def finish(slot):
        for k in range(TOP_K):
            pltpu.make_async_copy(yb_ref.at[pl.ds(0, tm), :], buf.at[slot, k], sems.at[slot]).wait()
        g = gate_ref[...]
        y = g[:, 0:1] * buf[slot, 0]
        for k in range(1, TOP_K):
            y = y + g[:, k:k + 1] * buf[slot, k]
        o_ref[...] = xn_ref[...] + gf_ref[0] * y

    @pl.when(i == 0)
    def _():
        issue(0, 0)

    for slot in range(2):
        @pl.when(lax.rem(i, 2) == slot)
        def _():
            @pl.when(i + 1 < n_steps)
            def _():
                issue(i + 1, 1 - slot)

            finish(slot)


def _combine(dest, yb, gate, xn, g_f, seq):
    n = xn.shape[0]
    tm = min(512, seq)
    tpb = seq // tm
    kern = functools.partial(_combine_kernel, tm=tm, n_tok=n, n_steps=n // tm)
    return pl.pallas_call(
        kern,
        out_shape=jax.ShapeDtypeStruct((n, D_MODEL), F32),
        grid_spec=pltpu.PrefetchScalarGridSpec(
            num_scalar_prefetch=1,
            grid=(n // tm,),
            in_specs=[pl.BlockSpec(memory_space=pl.ANY),
                      pl.BlockSpec((tm, LANES), lambda i, d: (i, 0)),
                      pl.BlockSpec((tm, D_MODEL), lambda i, d: (i, 0)),
                      pl.BlockSpec((1, 1, D_MODEL), lambda i, d: (i // tpb, 0, 0))],
            out_specs=pl.BlockSpec((tm, D_MODEL), lambda i, d: (i, 0)),
            scratch_shapes=[pltpu.VMEM((2, TOP_K, tm, D_MODEL), F32), pltpu.SemaphoreType.DMA((2,))]),
        compiler_params=_params(("arbitrary",), 40),
        name="combine",
    )(dest, yb, gate, xn, g_f)


def _layer(x, ctx, c, c_ctx, norm1_g, norm2_g, w_mod, b_mod, w_in, ret_decay_logit, ret_norm_g,
           diff_q_norm_g, diff_k_norm_g, diff_lambda, diff_norm_g, w_br_ret, w_br_diff, w_out,
           router_w, router_b, exp_w1, exp_b1, exp_w2, exp_b2):
    batch, seq, d = x.shape
    ctx_len = ctx.shape[1]
    assert d == D_MODEL and seq % GRID_W == 0 and batch + 1 <= 8
    n_tok = batch * seq

    cc = jnp.zeros((8, D_MODEL), F32).at[:batch].set(c).at[batch].set(c_ctx)
    mod = _mod(cc, w_mod, b_mod)
    sh_a, sc_a, g_a, sh_f, sc_f, g_f = [mod[:batch, i * D_MODEL:(i + 1) * D_MODEL].reshape(batch, 1, D_MODEL)
                                         for i in range(6)]
    csh_a = mod[batch:batch + 1, 0:D_MODEL].reshape(1, 1, D_MODEL)
    csc_a = mod[batch:batch + 1, D_MODEL:2 * D_MODEL].reshape(1, 1, D_MODEL)

    w_in_bf = w_in.astype(BF16)
    g1 = norm1_g.reshape(1, D_MODEL)
    tile = lambda g: jnp.tile(g.astype(F32), 2)
    qkg = jnp.zeros((8, LANES), F32).at[0].set(tile(diff_q_norm_g) * (DIFF_DH ** -0.5 * LOG2E)).at[1].set(tile(diff_k_norm_g))
    lane = jnp.arange(2 * LANES)
    same_group = lane[:, None] // DIFF_DH == lane[None, :] // DIFF_DH
    gmean = jnp.where(same_group, 1.0 / DIFF_DH, 0.0).astype(BF16)
    gmat = same_group[:LANES, :LANES].astype(BF16)
    x2 = x.reshape(n_tok, D_MODEL)
    tm = min(1024, seq)
    p_lat = _inproj(x2, g1, sh_a, sc_a, w_in_bf, qkg, gmean, _rope_tables(seq), tm, seq // tm)
    p_ctx = _inproj(ctx.reshape(batch * ctx_len, D_MODEL), g1, csh_a, csc_a, w_in_bf, qkg, gmean,
                    _identity_tables(ctx_len), ctx_len, 1)

    lg = jax.nn.log_sigmoid(ret_decay_logit.astype(F32))
    y_ret = _retention(lg, p_lat, p_ctx, ret_norm_g, batch, seq, ctx_len)

    lp = diff_lambda.astype(F32)
    lam = (jnp.exp(jnp.sum(lp[0] * lp[1])) - jnp.exp(jnp.sum(lp[2] * lp[3])) + LAMBDA_INIT).reshape(1)
    y_diff = _diff_attention(lam, p_lat, p_ctx, diff_norm_g, gmat, batch, seq, ctx_len)

    rw = jnp.zeros((D_MODEL, LANES), F32).at[:, :N_EXPERTS].set(router_w)
    rw_hi = rw.astype(BF16)
    rw_lo = (rw - rw_hi.astype(F32)).astype(BF16)
    rb = jnp.full((1, LANES), NEG_BIG, F32).at[0, :N_EXPERTS].set(router_b)
    xn, hp, meta, gate4, cnt = _merge(
        y_ret, y_diff, p_lat, x2, g_a, sh_f, sc_f, norm2_g.reshape(1, D_MODEL),
        w_br_ret.astype(BF16), w_br_diff.astype(BF16), w_out.astype(BF16), rw_hi, rw_lo, rb, seq)

    counts = cnt[0, :N_EXPERTS].astype(I32)
    padded = (counts + MOE_BLK - 1) // MOE_BLK * MOE_BLK
    pad_end = jnp.cumsum(padded)
    pad_start = pad_end - padded
    n_pairs = n_tok * TOP_K
    n_blocks = n_pairs // MOE_BLK + N_EXPERTS
    meta_i = meta.astype(I32)
    is_e = meta_i[None, :TOP_K] == jnp.arange(N_EXPERTS, dtype=I32)[:, None, None]
    dest = (jnp.sum(jnp.where(is_e, pad_start[:, None, None], 0), axis=0) + meta_i[TOP_K:]).reshape(n_pairs)
    block_start = jnp.arange(n_blocks, dtype=I32) * MOE_BLK
    block_e = jnp.minimum(jnp.sum((pad_end[None, :] <= block_start[:, None]).astype(I32), axis=1), N_EXPERTS - 1)
    n_used = (pad_end[-1] // MOE_BLK).reshape(1).astype(I32)

    fill_start = ((pad_start + counts) // SUBLANES * SUBLANES).astype(I32)
    xb = _dispatch(dest, fill_start, n_used, hp, (n_blocks + 1) * MOE_BLK)
    e_ids = jnp.arange(N_EXPERTS, dtype=I32)
    later = jnp.where((counts[None, :] > 0) & (e_ids[None, :] > e_ids[:, None]), e_ids[None, :], N_EXPERTS)
    next_e = jnp.min(later, axis=1)
    next_e = jnp.where(next_e == N_EXPERTS, -1, next_e).astype(I32)
    yb = _experts(block_e, n_used, next_e, xb, exp_w1, exp_b1, exp_w2, exp_b2)
    out = _combine(dest, yb, gate4, xn, g_f, seq)
    return out.reshape(batch, seq, D_MODEL)


def kernel(x, c, ctx, c_ctx, norm1_g, norm2_g, w_mod, b_mod, w_in, ret_decay_logit, ret_norm_g, diff_q_norm_g, diff_k_norm_g, diff_lambda, diff_norm_g, w_br_ret, w_br_diff, w_out, router_w, router_b, exp_w1, exp_b1, exp_w2, exp_b2):
    assert norm1_g.shape[0] == 1, "single-layer block"
    return _layer(x, ctx, c, c_ctx, norm1_g[0], norm2_g[0], w_mod[0], b_mod[0], w_in[0], ret_decay_logit[0],
                  ret_norm_g[0], diff_q_norm_g[0], diff_k_norm_g[0], diff_lambda[0], diff_norm_g[0],
                  w_br_ret[0], w_br_diff[0], w_out[0], router_w[0], router_b[0],
                  exp_w1[0], exp_b1[0], exp_w2[0], exp_b2[0])
```

```python
import functools
import math

import numpy as np

import jax
import jax.numpy as jnp
from jax import lax
from jax.experimental import pallas as pl
from jax.experimental.pallas import tpu as pltpu

F32 = jnp.float32
BF16 = jnp.bfloat16
U32 = jnp.uint32
I32 = jnp.int32

D_MODEL = 1024
GRID_W = 64
RET_HEADS = 4
RET_DK = 256
RET_DV = 512
DIFF_DH = 64
DIFF_HEADS = 8
DIFF_DV = 128
N_EXPERTS = 32
TOP_K = 4
D_FF = 1024
SWIGLU_LIMIT = 7.0
SWIGLU_ALPHA = 1.702
ROPE_BASE = 10000.0
EPS = 1e-6
LAMBDA_INIT = 0.8 - 0.6 * math.exp(-0.3 * 0)

IN_COLS = 11264
COL_TILE = 1024
CTX_COLS = (1, 2, 3, 7, 8)
LANES = 128
SUBLANES = 8
MOE_BLK = 256
NEG_BIG = -1e30
LOG2E = 1.4426950408889634
SHIFT_SLACK = 1.0 + 2.0 ** -6
MAX_SAFE_SHIFT = 60.0
RET_SUB = 8
KV_UNROLL = 4
HIGHEST = lax.Precision.HIGHEST
MIB = 1024 * 1024


def _params(sem, vmem_mib):
    return pltpu.CompilerParams(dimension_semantics=sem, vmem_limit_bytes=vmem_mib * MIB)


def _mod_kernel(c_ref, w_ref, b_ref, o_ref):
    c = c_ref[...]
    s = c * jax.nn.sigmoid(c)
    o_ref[...] = jnp.dot(s, w_ref[...], preferred_element_type=F32, precision=HIGHEST) + b_ref[...]


def _mod(cc, w_mod, b_mod):
    n = w_mod.shape[1]
    tn = 1024
    return pl.pallas_call(
        _mod_kernel,
        out_shape=jax.ShapeDtypeStruct((8, n), F32),
        grid=(n // tn,),
        in_specs=[pl.BlockSpec((8, D_MODEL), lambda j: (0, 0)),
                  pl.BlockSpec((D_MODEL, tn), lambda j: (0, j)),
                  pl.BlockSpec((1, tn), lambda j: (0, j))],
        out_specs=pl.BlockSpec((8, tn), lambda j: (0, j)),
        compiler_params=_params(("arbitrary",), 32),
        name="mod",
    )(cc, w_mod, b_mod.reshape(1, n))


def _inproj_kernel(cols_ref, x_ref, g_ref, sh_ref, sc_ref, w_ref, qkg_ref, gmat_ref,
                   cr_ref, sr_ref, cc_ref, sc2_ref, cd_ref, sa_ref, sb_ref,
                   o_ref, h_scr, acc_scr):
    j = pl.program_id(1)
    jc = cols_ref[j]

    @pl.when(j == 0)
    def _():
        xf = x_ref[...]
        ms = jnp.mean(xf * xf, axis=-1, keepdims=True)
        y = xf * lax.rsqrt(ms + EPS) * g_ref[...]
        h_scr[...] = (y * (1.0 + sc_ref[0]) + sh_ref[0]).astype(BF16)

    is_ret = jc <= 1
    is_dqk = (jc == 6) | (jc == 7)

    def project():
        return jnp.dot(h_scr[...], w_ref[...], preferred_element_type=F32)

    @pl.when(is_ret)
    def _():
        acc_scr[...] = project()
        scale = jnp.where(jc == 0, RET_DK ** -0.5, 1.0).astype(F32)
        for b in range(COL_TILE // LANES):
            xb = acc_scr[:, b * LANES:(b + 1) * LANES]
            cos = cr_ref[...] if b % 2 == 0 else cc_ref[...]
            sin = sr_ref[...] if b % 2 == 0 else sc2_ref[...]
            o = (xb * cos + pltpu.roll(xb, 64, 1) * sin) * scale
            o_ref[:, b * LANES:(b + 1) * LANES] = o.astype(BF16)

    @pl.when(is_dqk)
    def _():
        acc_scr[...] = project()
        g = jnp.where(jc == 6, qkg_ref[0:1, :], qkg_ref[1:2, :])
        for b2 in range(COL_TILE // (2 * LANES)):
            x2 = acc_scr[:, b2 * 2 * LANES:(b2 + 1) * 2 * LANES]
            ms2 = jnp.dot((x2 * x2).astype(BF16), gmat_ref[...], preferred_element_type=F32)
            for half in range(2):
                b = 2 * b2 + half
                xb = x2[:, half * LANES:(half + 1) * LANES]
                yn = xb * lax.rsqrt(ms2[:, half * LANES:(half + 1) * LANES] + EPS) * g
                o = yn * cd_ref[...] + pltpu.roll(yn, 16, 1) * sa_ref[...] + pltpu.roll(yn, 112, 1) * sb_ref[...]
                o_ref[:, b * LANES:(b + 1) * LANES] = o.astype(BF16)

    @pl.when(jnp.logical_not(is_ret | is_dqk))
    def _():
        o_ref[...] = project().astype(BF16)


def _inproj(x2, g1, sh, sc, w_bf, qkg, gmat, tables, tm, tiles_per_batch, cols):
    n = x2.shape[0]
    nb = sh.shape[0]
    tab_spec = pl.BlockSpec((tm, LANES), lambda i, j, c: (i % tiles_per_batch, 0))
    mod_spec = pl.BlockSpec((1, 1, D_MODEL), lambda i, j, c: (jnp.minimum(i // tiles_per_batch, nb - 1), 0, 0))
    return pl.pallas_call(
        _inproj_kernel,
        out_shape=jax.ShapeDtypeStruct((n, len(cols) * COL_TILE), BF16),
        grid_spec=pltpu.PrefetchScalarGridSpec(
            num_scalar_prefetch=1,
            grid=(n // tm, len(cols)),
            in_specs=[pl.BlockSpec((tm, D_MODEL), lambda i, j, c: (i, 0)),
                      pl.BlockSpec((1, D_MODEL), lambda i, j, c: (0, 0)),
                      mod_spec, mod_spec,
                      pl.BlockSpec((D_MODEL, COL_TILE), lambda i, j, c: (0, c[j])),
                      pl.BlockSpec((8, LANES), lambda i, j, c: (0, 0)),
                      pl.BlockSpec((2 * LANES, 2 * LANES), lambda i, j, c: (0, 0))] + [tab_spec] * 7,
            out_specs=pl.BlockSpec((tm, COL_TILE), lambda i, j, c: (i, j)),
            scratch_shapes=[pltpu.VMEM((tm, D_MODEL), BF16), pltpu.VMEM((tm, COL_TILE), F32)]),
        compiler_params=_params(("arbitrary", "arbitrary"), 48),
        name="inproj",
    )(jnp.asarray(cols, I32), x2, g1, sh, sc, w_bf, qkg, gmat, *tables)


def _rope_tables(seq):
    n_rows = seq // GRID_W
    f32 = np.float32

    def angles(pos, half):
        inv = f32(ROPE_BASE) ** (-np.arange(half, dtype=f32) / f32(half))
        return (pos.astype(f32)[:, None] * inv[None, :]).astype(np.float64)

    ar, ac = angles(np.arange(n_rows), 64), angles(np.arange(GRID_W), 64)
    br, bc = angles(np.arange(n_rows), 16), angles(np.arange(GRID_W), 16)
    zr, zc = np.zeros_like(br), np.zeros_like(bc)
    cat = lambda parts, reps=1: np.tile(np.concatenate(parts, axis=1), (1, reps)).astype(f32)
    by_row = lambda t: jnp.repeat(jnp.asarray(t), GRID_W, axis=0)
    by_col = lambda t: jnp.tile(jnp.asarray(t), (n_rows, 1))
    cr = by_row(cat([np.cos(ar), np.cos(ar)]))
    sr = by_row(cat([-np.sin(ar), np.sin(ar)]))
    cc = by_col(cat([np.cos(ac), np.cos(ac)]))
    sc = by_col(cat([-np.sin(ac), np.sin(ac)]))
    cd = by_row(cat([np.cos(br), np.cos(br), zr, zr], 2)) + by_col(cat([zc, zc, np.cos(bc), np.cos(bc)], 2))
    sa = by_row(cat([zr, np.sin(br), zr, zr], 2)) + by_col(cat([zc, zc, zc, np.sin(bc)], 2))
    sb = by_row(cat([-np.sin(br), zr, zr, zr], 2)) + by_col(cat([zc, zc, -np.sin(bc), zc], 2))
    return [cr, sr, cc, sc, cd, sa, sb]


def _identity_tables(seq):
    one = jnp.ones((seq, LANES), F32)
    zero = jnp.zeros((seq, LANES), F32)
    return [one, zero, one, zero, one, zero, zero]


def _tn_dot(a, b):
    return lax.dot_general(a, b, (((0,), (0,)), ((), ())), preferred_element_type=F32)


def _nt_dot(a, b):
    return lax.dot_general(a, b, (((1,), (1,)), ((), ())), preferred_element_type=F32)


def _ret_kernel(lg_ref, q_ref, k_ref, v_ref, g_ref, ck_ref, cv_ref, gn_ref, o_ref,
                sf_scr, sb_scr, ob_scr, *, chunk, sub, n_blocks, ctx_len):
    h = pl.program_id(1)
    p = pl.program_id(2)
    c = pl.program_id(3)
    lgf = lg_ref[0, h]
    lgb = lg_ref[1, h]

    def col_iota(n):
        return lax.broadcasted_iota(I32, (n, 1), 0).astype(F32)

    def vexp(s):
        return jnp.exp(jnp.zeros((1, 1), F32) + s)

    @pl.when((p == 0) & (c == 0))
    def _():
        jc = col_iota(ctx_len)
        kc = ck_ref[...].astype(F32)
        vc = cv_ref[...]
        sf_scr[...] = _tn_dot((kc * jnp.exp(lgf * (ctx_len - 1.0 - jc))).astype(BF16), vc)
        sb_scr[...] = _tn_dot((kc * jnp.exp(lgb * jc)).astype(BF16), vc)

    ic = col_iota(chunk)

    def full_bf16(col):
        return jnp.broadcast_to(col, (chunk, RET_DK)).astype(BF16)

    @pl.when(p == 0)
    def _():
        q_decay = full_bf16(jnp.exp(lgb * (chunk - ic)))
        k_decay = full_bf16(jnp.exp(lgb * ic))
        s_decay = vexp(lgb * chunk)
        for j in reversed(range(sub)):
            loc = pl.ds(j * chunk, chunk)
            glob = pl.ds(pl.multiple_of(((n_blocks - 1 - c) * sub + j) * chunk, chunk), chunk)
            qb = q_ref[loc, :] * q_decay
            ob_scr[glob, :] = jnp.dot(qb, sb_scr[...].astype(BF16), preferred_element_type=F32)
            kb = k_ref[loc, :] * k_decay
            sb_scr[...] = s_decay * sb_scr[...] + _tn_dot(kb, v_ref[loc, :])

    @pl.when(p == 1)
    def _():
        ri = lax.broadcasted_iota(I32, (chunk, chunk), 0)
        ci = lax.broadcasted_iota(I32, (chunk, chunk), 1)
        d = (ri - ci).astype(F32)
        mask = jnp.where(d > 0, jnp.exp(lgf * jnp.maximum(d, 0.0)),
                         jnp.where(d < 0, jnp.exp(lgb * jnp.maximum(-d, 0.0)), 2.0))
        q_decay = full_bf16(jnp.exp(lgf * (ic + 1.0)))
        k_decay = full_bf16(jnp.exp(lgf * (chunk - 1.0 - ic)))
        s_decay = vexp(lgf * chunk)
        for j in range(sub):
            loc = pl.ds(j * chunk, chunk)
            glob = pl.ds(pl.multiple_of((c * sub + j) * chunk, chunk), chunk)
            a = (_nt_dot(q_ref[loc, :], k_ref[loc, :]) * mask).astype(BF16)
            qf = q_ref[loc, :] * q_decay
            o = (jnp.dot(a, v_ref[loc, :], preferred_element_type=F32)
                 + jnp.dot(qf, sf_scr[...].astype(BF16), preferred_element_type=F32)
                 + ob_scr[glob, :])
            kf = k_ref[loc, :] * k_decay
            sf_scr[...] = s_decay * sf_scr[...] + _tn_dot(kf, v_ref[loc, :])
            ms = jnp.mean(o * o, axis=-1, keepdims=True)
            y = o * lax.rsqrt(ms + EPS) * gn_ref[0]
            gt = g_ref[loc, :]
            o_ref[loc, :] = y.astype(BF16) * (gt * jax.nn.sigmoid(gt))


def _retention(lg, p_lat, p_ctx, ret_norm_g, batch, seq, ctx_len, ctx_k_blk, ctx_v_blk):
    chunk = min(256, seq)
    sub = math.gcd(seq // chunk, RET_SUB)
    blk = chunk * sub
    nb = seq // blk
    kern = functools.partial(_ret_kernel, chunk=chunk, sub=sub, n_blocks=nb, ctx_len=ctx_len)

    def rows(b, p, c):
        return b * nb + jnp.where(p == 0, nb - 1 - c, c)

    def rows_fwd(b, p, c):
        return b * nb + jnp.where(p == 0, 0, c)

    return pl.pallas_call(
        kern,
        out_shape=jax.ShapeDtypeStruct((batch * seq, RET_HEADS * RET_DV), BF16),
        grid=(batch, RET_HEADS, 2, nb),
        in_specs=[pl.BlockSpec(memory_space=pltpu.SMEM),
                  pl.BlockSpec((blk, RET_DK), lambda b, h, p, c: (rows(b, p, c), h)),
                  pl.BlockSpec((blk, RET_DK), lambda b, h, p, c: (rows(b, p, c), 4 + h)),
                  pl.BlockSpec((blk, RET_DV), lambda b, h, p, c: (rows(b, p, c), 4 + h)),
                  pl.BlockSpec((blk, RET_DV), lambda b, h, p, c: (rows_fwd(b, p, c), 8 + h)),
                  pl.BlockSpec((ctx_len, RET_DK), lambda b, h, p, c: (b, ctx_k_blk + h)),
                  pl.BlockSpec((ctx_len, RET_DV), lambda b, h, p, c: (b, ctx_v_blk + h)),
                  pl.BlockSpec((1, 1, RET_DV), lambda b, h, p, c: (h, 0, 0))],
        out_specs=pl.BlockSpec((blk, RET_DV), lambda b, h, p, c: (rows_fwd(b, p, c), h)),
        scratch_shapes=[pltpu.VMEM((RET_DK, RET_DV), F32), pltpu.VMEM((RET_DK, RET_DV), F32),
                        pltpu.VMEM((seq, RET_DV), F32)],
        compiler_params=_params(("arbitrary",) * 4, 48),
        name="ret",
    )(lg, p_lat, p_lat, p_lat, p_lat, p_ctx, p_ctx, ret_norm_g.reshape(RET_HEADS, 1, RET_DV))


def _dattn_kernel(lam_ref, q_ref, ck_ref, cv_ref, k_ref, v_ref, gn_ref, gmat_ref, o_ref,
                  qq_scr, kmax_scr, mp_scr, kp_scr, vt_scr, pt_scr, *, tq, tk, n_kv, ta, unroll, ctx_len):
    rows = 2 * tq
    n_all = kp_scr.shape[0]
    qi = pl.program_id(2)

    def sq_norms(t):
        tf = t.astype(F32)
        return jnp.dot((tf * tf).astype(BF16), gmat_ref[...], preferred_element_type=F32)

    def lane_const(n, hot):
        return jnp.where(lax.broadcasted_iota(I32, (n, LANES), 1) < hot, 1.0, 0.0).astype(BF16)

    @pl.when(qi == 0)
    def _():
        kmax_scr[...] = jnp.max(sq_norms(ck_ref[...]), axis=0, keepdims=True)
        kp_scr[:, LANES:2 * LANES] = lane_const(n_all, 3)
        kp_scr[0:ctx_len, 0:LANES] = ck_ref[...]
        vt_scr[:, 0:ctx_len] = cv_ref[...].astype(F32).T.astype(BF16)

        prep = math.gcd(n_kv, 4)

        def body(c, carry):
            norms = []
            for u in range(prep):
                start = pl.multiple_of((c * prep + u) * tk, tk)
                k = k_ref[pl.ds(start, tk), :]
                norms.append(jnp.max(sq_norms(k), axis=0, keepdims=True))
                kp_scr[pl.ds(pl.multiple_of(ctx_len + start, LANES), tk), 0:LANES] = k
                vt_scr[:, pl.ds(pl.multiple_of(ctx_len + start, LANES), tk)] = (
                    v_ref[pl.ds(start, tk), :].astype(F32).T.astype(BF16))
            kmax_scr[...] = jnp.maximum(kmax_scr[...], functools.reduce(jnp.maximum, norms))
            return carry

        lax.fori_loop(0, n_kv // prep, body, 0)

    qt = q_ref[...].astype(F32).T
    row = lax.broadcasted_iota(I32, (LANES, tq), 0)
    q1t = jnp.where(row < DIFF_DH, qt, 0.0)
    q2t = jnp.where(row >= DIFF_DH, qt, 0.0)
    qq_scr[0:LANES, 0:tq] = q1t.astype(BF16)
    qq_scr[0:LANES, tq:rows] = q2t.astype(BF16)

    def set_shift(shift):
        neg = -shift
        hi = neg.astype(BF16).astype(F32)
        mid = (neg - hi).astype(BF16).astype(F32)
        lo = neg - hi - mid
        row_r = lax.broadcasted_iota(I32, (LANES, rows), 0)
        pieces = jnp.where(row_r == 0, hi, jnp.where(row_r == 1, mid, jnp.where(row_r == 2, lo, 0.0)))
        qq_scr[LANES:2 * LANES, :] = pieces.astype(BF16)

    kmax = kmax_scr[...]
    b1 = jnp.sqrt(jnp.sum(q1t * q1t, axis=0, keepdims=True) * kmax[:, 0:1]) * SHIFT_SLACK
    b2 = jnp.sqrt(jnp.sum(q2t * q2t, axis=0, keepdims=True) * kmax[:, DIFF_DH:DIFF_DH + 1]) * SHIFT_SLACK
    bound = jnp.concatenate([b1, b2], axis=1)
    set_shift(bound)

    def key_tile(t):
        return pl.ds(pl.multiple_of(t * ta, ta), ta)

    @pl.when(jnp.max(bound) > MAX_SAFE_SHIFT)
    def _():
        mp_scr[...] = jnp.full(mp_scr.shape, NEG_BIG, F32)

        def body(t, carry):
            st = jnp.dot(kp_scr[key_tile(t), 0:LANES], qq_scr[0:LANES, :], preferred_element_type=F32)
            mp_scr[...] = jnp.maximum(mp_scr[...], jnp.max(st, axis=0, keepdims=True))
            return carry

        lax.fori_loop(0, n_all // ta, body, 0)
        set_shift(mp_scr[0:1, :])

    mp_scr[...] = jnp.zeros(mp_scr.shape, F32)

    def body_a(t, carry):
        for u in range(unroll):
            r = key_tile(t * unroll + u)
            p = jnp.exp2(jnp.dot(kp_scr[r, :], qq_scr[...], preferred_element_type=F32))
            pt_scr[r, :] = p.astype(BF16)
            mp_scr[...] += jnp.sum(p.reshape(ta // SUBLANES, SUBLANES, rows), axis=0)
        return carry

    lax.fori_loop(0, n_all // (ta * unroll), body_a, 0)

    acc = jnp.dot(vt_scr[...], pt_scr[...], preferred_element_type=F32)
    ot = acc / jnp.sum(mp_scr[...], axis=0, keepdims=True)
    d = (ot[:, 0:tq] - lam_ref[0] * ot[:, tq:rows]).T
    ms = jnp.mean(d * d, axis=-1, keepdims=True)
    y = d * lax.rsqrt(ms + EPS) * gn_ref[...] * (1.0 - LAMBDA_INIT)
    o_ref[...] = y.astype(BF16)


def _diff_attention(lam, p_lat, p_ctx, diff_norm_g, gmat, batch, seq, ctx_len, ctx_k_blk, ctx_v_blk):
    tq = min(1024, seq)
    tk = min(512, seq)
    nq, nk = seq // tq, seq // tk
    n_all = ctx_len + seq
    ta = next(t for t in (528, 512, 384, 320, 256, 128) if n_all % t == 0)
    unroll = math.gcd(n_all // ta, KV_UNROLL)
    assert ctx_len % LANES == 0
    kern = functools.partial(_dattn_kernel, tq=tq, tk=tk, n_kv=nk, ta=ta, unroll=unroll, ctx_len=ctx_len)
    return pl.pallas_call(
        kern,
        out_shape=jax.ShapeDtypeStruct((batch * seq, DIFF_HEADS * DIFF_DV), BF16),
        grid=(batch, DIFF_HEADS, nq),
        in_specs=[pl.BlockSpec(memory_space=pltpu.SMEM),
                  pl.BlockSpec((tq, LANES), lambda b, h, qi: (b * nq + qi, 48 + h)),
                  pl.BlockSpec((ctx_len, LANES), lambda b, h, qi: (b, ctx_k_blk + h)),
                  pl.BlockSpec((ctx_len, LANES), lambda b, h, qi: (b, ctx_v_blk + h)),
                  pl.BlockSpec((seq, LANES), lambda b, h, qi: (b, 56 + h)),
                  pl.BlockSpec((seq, LANES), lambda b, h, qi: (b, 64 + h)),
                  pl.BlockSpec((1, LANES), lambda b, h, qi: (0, 0)),
                  pl.BlockSpec((LANES, LANES), lambda b, h, qi: (0, 0))],
        out_specs=pl.BlockSpec((tq, LANES), lambda b, h, qi: (b * nq + qi, h)),
        scratch_shapes=[pltpu.VMEM((2 * LANES, 2 * tq), BF16), pltpu.VMEM((1, LANES), F32),
                        pltpu.VMEM((SUBLANES, 2 * tq), F32), pltpu.VMEM((n_all, 2 * LANES), BF16),
                        pltpu.VMEM((DIFF_DV, n_all), BF16), pltpu.VMEM((n_all, 2 * tq), BF16)],
        compiler_params=_params(("arbitrary",) * 3, 58),
        name="dattn",
    )(lam, p_lat, p_ctx, p_ctx, p_lat, p_lat, diff_norm_g.reshape(1, DIFF_DV), gmat)


def _merge_kernel(yr_ref, yd_ref, ga_ref, gb_ref, x_ref, gatea_ref, shf_ref, scf_ref, n2_ref,
                  wr_ref, wd_ref, wo_ref, rwh_ref, rwl_ref, rb_ref,
                  xn_ref, hp_ref, meta_ref, gate_ref, cnt_ref, run_scr, *, tm):
    i = pl.program_id(0)

    @pl.when(i == 0)
    def _():
        run_scr[...] = jnp.zeros(run_scr.shape, F32)

    yr = jnp.dot(yr_ref[...], wr_ref[...], preferred_element_type=F32)
    yd = jnp.dot(yd_ref[...], wd_ref[...], preferred_element_type=F32)
    m = (jax.nn.sigmoid(ga_ref[...].astype(F32)) * yr + jax.nn.sigmoid(gb_ref[...].astype(F32)) * yd)
    z = jnp.dot(m.astype(BF16), wo_ref[...], preferred_element_type=F32)
    xn = x_ref[...] + gatea_ref[0] * z
    xn_ref[...] = xn

    ms = jnp.mean(xn * xn, axis=-1, keepdims=True)
    h2 = xn * lax.rsqrt(ms + EPS) * n2_ref[...]
    h2 = h2 * (1.0 + scf_ref[0]) + shf_ref[0]
    h_hi = h2.astype(BF16)
    bits = pltpu.bitcast(h_hi.astype(F32), U32)
    half = D_MODEL // 2
    hp_ref[...] = (bits[:, :half] >> 16) | (bits[:, half:] & jnp.uint32(0xFFFF0000))

    h_lo = (h2 - h_hi.astype(F32)).astype(BF16)
    logits = (jnp.dot(h_hi, rwh_ref[...], preferred_element_type=F32)
              + jnp.dot(h_lo, rwh_ref[...], preferred_element_type=F32)
              + jnp.dot(h_hi, rwl_ref[...], preferred_element_type=F32) + rb_ref[...])
    lane = lax.broadcasted_iota(I32, (tm, LANES), 1)
    lanef = lane.astype(F32)
    work = logits
    vals, idxs = [], []
    for _ in range(TOP_K):
        mk = jnp.max(work, axis=1, keepdims=True)
        ik = jnp.min(jnp.where(work == mk, lanef, float(LANES)), axis=1, keepdims=True)
        vals.append(mk)
        idxs.append(ik)
        work = jnp.where(lanef == ik, -jnp.inf, work)
    ex = [jnp.exp(v - vals[0]) for v in vals]
    den = ex[0] + ex[1] + ex[2] + ex[3]

    onehot = jnp.zeros((tm, LANES), F32)
    for ik in idxs:
        onehot = onehot + jnp.where(lanef == ik, 1.0, 0.0)
    ri = lax.broadcasted_iota(I32, (tm, tm), 0)
    ci = lax.broadcasted_iota(I32, (tm, tm), 1)
    tri = jnp.where(ri > ci, 1.0, 0.0).astype(BF16)
    base = run_scr[0:1, :] + jnp.dot(tri, onehot.astype(BF16), preferred_element_type=F32)
    run_scr[...] = run_scr[...] + jnp.sum(onehot, axis=0, keepdims=True)

    meta = jnp.zeros((tm, LANES), F32)
    gate_out = jnp.zeros((tm, LANES), F32)
    for k in range(TOP_K):
        rk = jnp.sum(jnp.where(lanef == idxs[k], base, 0.0), axis=1, keepdims=True)
        meta = jnp.where(lane == k, idxs[k], jnp.where(lane == TOP_K + k, rk, meta))
        gate_out = jnp.where(lane == k, ex[k] / den, gate_out)
    meta_ref[...] = meta.T[0:2 * TOP_K, :]
    gate_ref[...] = gate_out
    cnt_ref[...] = run_scr[...]


def _merge(y_ret, y_diff, p_lat, x2, g_a, sh_f, sc_f, norm2_g, w_r, w_d, w_o, rw_hi, rw_lo, rb, seq):
    n = x2.shape[0]
    tm = min(512, seq)
    tpb = seq // tm
    kern = functools.partial(_merge_kernel, tm=tm)
    mod_spec = pl.BlockSpec((1, 1, D_MODEL), lambda i: (i // tpb, 0, 0))
    const = lambda shape: pl.BlockSpec(shape, lambda i: (0,) * len(shape))
    tok = lambda w: pl.BlockSpec((tm, w), lambda i: (i, 0))
    return pl.pallas_call(
        kern,
        out_shape=(jax.ShapeDtypeStruct((n, D_MODEL), F32),
                   jax.ShapeDtypeStruct((n, D_MODEL // 2), U32),
                   jax.ShapeDtypeStruct((2 * TOP_K, n), F32),
                   jax.ShapeDtypeStruct((n, LANES), F32),
                   jax.ShapeDtypeStruct((8, LANES), F32)),
        grid=(n // tm,),
        in_specs=[tok(RET_HEADS * RET_DV), tok(D_MODEL),
                  pl.BlockSpec((tm, COL_TILE), lambda i: (i, 9)),
                  pl.BlockSpec((tm, COL_TILE), lambda i: (i, 10)),
                  tok(D_MODEL), mod_spec, mod_spec, mod_spec, const((1, D_MODEL)),
                  const((RET_HEADS * RET_DV, D_MODEL)), const((D_MODEL, D_MODEL)), const((D_MODEL, D_MODEL)),
                  const((D_MODEL, LANES)), const((D_MODEL, LANES)), const((1, LANES))],
        out_specs=(tok(D_MODEL), tok(D_MODEL // 2), pl.BlockSpec((2 * TOP_K, tm), lambda i: (0, i)),
                   tok(LANES), const((8, LANES))),
        scratch_shapes=[pltpu.VMEM((8, LANES), F32)],
        compiler_params=_params(("arbitrary",), 56),
        name="merge",
    )(y_ret, y_diff, p_lat, p_lat, x2, g_a, sh_f, sc_f, norm2_g, w_r, w_d, w_o, rw_hi, rw_lo, rb)


def _dispatch_kernel(dest_ref, fill_ref, nb_ref, hp_ref, xb_ref, zero_scr, stage, sems, zsem, *,
                     tm, n_tok, n_blocks, n_steps):
    i = pl.program_id(0)

    @pl.when(i == 0)
    def _():
        zero_scr[...] = jnp.zeros(zero_scr.shape, U32)
        fill = zero_scr.shape[0]
        for e in range(N_EXPERTS):
            start = pl.multiple_of(fill_ref[e], SUBLANES)
            pltpu.make_async_copy(zero_scr, xb_ref.at[pl.ds(start, fill), :], zsem).start()
        for e in range(N_EXPERTS):
            pltpu.make_async_copy(zero_scr, xb_ref.at[pl.ds(0, fill), :], zsem).wait()
        blk = zero_scr.at[pl.ds(0, MOE_BLK), :]

        def start_blk(b, carry):
            pltpu.make_async_copy(blk, xb_ref.at[pl.ds(pl.multiple_of(b * MOE_BLK, MOE_BLK), MOE_BLK), :], zsem).start()
            return carry

        def wait_blk(b, carry):
            pltpu.make_async_copy(blk, xb_ref.at[pl.ds(0, MOE_BLK), :], zsem).wait()
            return carry

        lax.fori_loop(nb_ref[0], n_blocks, start_blk, 0)
        lax.fori_loop(nb_ref[0], n_blocks, wait_blk, 0)

    def drain(slot):
        for _ in range(TOP_K):
            pltpu.make_async_copy(stage.at[slot], xb_ref.at[pl.ds(0, tm), :], sems.at[slot]).wait()

    for slot in range(2):
        @pl.when(lax.rem(i, 2) == slot)
        def _():
            stage[slot] = hp_ref[...]

            def body(g, carry):
                r0 = pl.multiple_of(g * SUBLANES, SUBLANES)
                for j in range(SUBLANES):
                    for k in range(TOP_K):
                        d = dest_ref[k * n_tok + i * tm + r0 + j]
                        pltpu.make_async_copy(stage.at[slot, pl.ds(r0 + j, 1), :], xb_ref.at[pl.ds(d, 1), :],
                                              sems.at[slot]).start()
                return carry

            lax.fori_loop(0, tm // SUBLANES, body, 0)

            @pl.when(i > 0)
            def _():
                drain(1 - slot)

            @pl.when(i == n_steps - 1)
            def _():
                drain(slot)


def _dispatch(dest, fill_start, n_used, hp, n_rows):
    n = hp.shape[0]
    tm = min(512, n)
    kern = functools.partial(_dispatch_kernel, tm=tm, n_tok=n, n_blocks=n_rows // MOE_BLK, n_steps=n // tm)
    return pl.pallas_call(
        kern,
        out_shape=jax.ShapeDtypeStruct((n_rows, D_MODEL // 2), U32),
        grid_spec=pltpu.PrefetchScalarGridSpec(
            num_scalar_prefetch=3,
            grid=(n // tm,),
            in_specs=[pl.BlockSpec((tm, D_MODEL // 2), lambda i, d, f, nb: (i, 0))],
            out_specs=pl.BlockSpec(memory_space=pl.ANY),
            scratch_shapes=[pltpu.VMEM((MOE_BLK + SUBLANES, D_MODEL // 2), U32),
                            pltpu.VMEM((2, tm, D_MODEL // 2), U32),
                            pltpu.SemaphoreType.DMA((2,)), pltpu.SemaphoreType.DMA(())]),
        compiler_params=_params(("arbitrary",), 32),
        name="dispatch",
    )(dest, fill_start, n_used, hp)


def _expert_kernel(be_ref, nb_ref, nxt_ref, xb_ref, w1_hbm, b1_ref, w2_hbm, b2_ref, y_ref,
                   w1f_scr, w2f_scr, w1b_scr, w2b_scr, x_scr, grp_scr, sems):
    i = pl.program_id(0)
    e = be_ref[i]
    prev = be_ref[jnp.maximum(i - 1, 0)]
    active = i < nb_ref[0]

    def weight_copies(expert, slot):
        return (pltpu.make_async_copy(w1_hbm.at[expert], w1f_scr.at[slot], sems.at[0, slot]),
                pltpu.make_async_copy(w2_hbm.at[expert], w2f_scr.at[slot], sems.at[1, slot]))

    @pl.when(i == 0)
    def _():
        grp_scr[0] = 0
        for cp in weight_copies(e, 0):
            cp.start()

    @pl.when(active & (i > 0) & (e != prev))
    def _():
        grp_scr[0] = grp_scr[0] + 1

    @pl.when(active & ((i == 0) | (e != prev)))
    def _():
        slot = grp_scr[0] % 2
        for cp in weight_copies(e, slot):
            cp.wait()
        w1b_scr[...] = w1f_scr[slot].astype(BF16)
        w2b_scr[...] = w2f_scr[slot].astype(BF16)
        nxt = nxt_ref[e]

        @pl.when(nxt >= 0)
        def _():
            for cp in weight_copies(nxt, 1 - slot):
                cp.start()

    @pl.when(active)
    def _():
        xu = xb_ref[...]
        half = D_MODEL // 2
        x_scr[:, 0:half] = pltpu.bitcast(xu << 16, F32).astype(BF16)
        x_scr[:, half:] = pltpu.bitcast(xu & jnp.uint32(0xFFFF0000), F32).astype(BF16)
        hh = jnp.dot(x_scr[...], w1b_scr[...], preferred_element_type=F32) + b1_ref[0]
        glu = jnp.minimum(hh[:, :D_FF], SWIGLU_LIMIT)
        lin = jnp.clip(hh[:, D_FF:], -SWIGLU_LIMIT, SWIGLU_LIMIT)
        act = glu * jax.nn.sigmoid(SWIGLU_ALPHA * glu) * (lin + 1.0)
        y_ref[...] = jnp.dot(act.astype(BF16), w2b_scr[...], preferred_element_type=F32) + b2_ref[0]

    @pl.when(i >= nb_ref[0])
    def _():
        y_ref[...] = jnp.zeros(y_ref.shape, F32)


def _experts(block_e, n_used, next_e, xb, w1, b1, w2, b2):
    n_blocks = block_e.shape[0]
    rows = n_blocks * MOE_BLK
    return pl.pallas_call(
        _expert_kernel,
        out_shape=jax.ShapeDtypeStruct((rows, D_MODEL), F32),
        grid_spec=pltpu.PrefetchScalarGridSpec(
            num_scalar_prefetch=3,
            grid=(n_blocks,),
            in_specs=[pl.BlockSpec((MOE_BLK, D_MODEL // 2), lambda i, be, nb, nx: (jnp.minimum(i, nb[0] - 1), 0)),
                      pl.BlockSpec(memory_space=pl.ANY),
                      pl.BlockSpec((1, 1, 2 * D_FF), lambda i, be, nb, nx: (be[i], 0, 0)),
                      pl.BlockSpec(memory_space=pl.ANY),
                      pl.BlockSpec((1, 1, D_MODEL), lambda i, be, nb, nx: (be[i], 0, 0))],
            out_specs=pl.BlockSpec((MOE_BLK, D_MODEL), lambda i, be, nb, nx: (i, 0)),
            scratch_shapes=[pltpu.VMEM((2, D_MODEL, 2 * D_FF), F32), pltpu.VMEM((2, D_FF, D_MODEL), F32),
                            pltpu.VMEM((D_MODEL, 2 * D_FF), BF16), pltpu.VMEM((D_FF, D_MODEL), BF16),
                            pltpu.VMEM((MOE_BLK, D_MODEL), BF16),
                            pltpu.SMEM((1,), I32), pltpu.SemaphoreType.DMA((2, 2))]),
        compiler_params=_params(("arbitrary",), 56),
        name="expert",
    )(block_e, n_used, next_e, xb, w1, b1.reshape(N_EXPERTS, 1, 2 * D_FF), w2, b2.reshape(N_EXPERTS, 1, D_MODEL))


def _combine_kernel(dest_ref, yb_ref, gate_ref, xn_ref, gf_ref, o_ref, buf, sems, *, tm, n_tok, n_steps):
    i = pl.program_id(0)

    def issue(tile, slot):
        def body(g, carry):
            r0 = pl.multiple_of(g * SUBLANES, SUBLANES)
            for j in range(SUBLANES):
                for k in range(TOP_K):
                    d = dest_ref[k * n_tok + tile * tm + r0 + j]
                    pltpu.make_async_copy(yb_ref.at[pl.ds(d, 1), :], buf.at[slot, k, pl.ds(r0 + j, 1), :],
                                          sems.at[slot]).start()
            return carry

        lax.fori_loop(0, tm // SUBLANES, body, 0)

    def finish(slot):
        for k in range(TOP_K):
            pltpu.make_async_copy(yb_ref.at[pl.ds(0, tm), :], buf.at[slot, k], sems.at[slot]).wait()
        g = gate_ref[...]
        y = g[:, 0:1] * buf[slot, 0]
        for k in range(1, TOP_K):
            y = y + g[:, k:k + 1] * buf[slot, k]
        o_ref[...] = xn_ref[...] + gf_ref[0] * y

    @pl.when(i == 0)
    def _():
        issue(0, 0)

    for slot in range(2):
        @pl.when(lax.rem(i, 2) == slot)
        def _():
            @pl.when(i + 1 < n_steps)
            def _():
                issue(i + 1, 1 - slot)

            finish(slot)


def _combine(dest, yb, gate, xn, g_f, seq):
    n = xn.shape[0]
    tm = min(512, seq)
    tpb = seq // tm
    kern = functools.partial(_combine_kernel, tm=tm, n_tok=n, n_steps=n // tm)
    return pl.pallas_call(
        kern,
        out_shape=jax.ShapeDtypeStruct((n, D_MODEL), F32),
        grid_spec=pltpu.PrefetchScalarGridSpec(
            num_scalar_prefetch=1,
            grid=(n // tm,),
            in_specs=[pl.BlockSpec(memory_space=pl.ANY),
                      pl.BlockSpec((tm, LANES), lambda i, d: (i, 0)),
                      pl.BlockSpec((tm, D_MODEL), lambda i, d: (i, 0)),
                      pl.BlockSpec((1, 1, D_MODEL), lambda i, d: (i // tpb, 0, 0))],
            out_specs=pl.BlockSpec((tm, D_MODEL), lambda i, d: (i, 0)),
            scratch_shapes=[pltpu.VMEM((2, TOP_K, tm, D_MODEL), F32), pltpu.SemaphoreType.DMA((2,))]),
        compiler_params=_params(("arbitrary",), 40),
        name="combine",
    )(dest, yb, gate, xn, g_f)


def _layer(x, ctx, c, c_ctx, norm1_g, norm2_g, w_mod, b_mod, w_in, ret_decay_logit, ret_norm_g,
           diff_q_norm_g, diff_k_norm_g, diff_lambda, diff_norm_g, w_br_ret, w_br_diff, w_out,
           router_w, router_b, exp_w1, exp_b1, exp_w2, exp_b2):
    batch, seq, d = x.shape
    ctx_len = ctx.shape[1]
    assert d == D_MODEL and seq % GRID_W == 0 and batch + 1 <= 8
    n_tok = batch * seq

    cc = jnp.zeros((8, D_MODEL), F32).at[:batch].set(c).at[batch].set(c_ctx)
    mod = _mod(cc, w_mod, b_mod)
    sh_a, sc_a, g_a, sh_f, sc_f, g_f = [mod[:batch, i * D_MODEL:(i + 1) * D_MODEL].reshape(batch, 1, D_MODEL)
                                         for i in range(6)]
    csh_a = mod[batch:batch + 1, 0:D_MODEL].reshape(1, 1, D_MODEL)
    csc_a = mod[batch:batch + 1, D_MODEL:2 * D_MODEL].reshape(1, 1, D_MODEL)

    w_in_bf = w_in.astype(BF16)
    g1 = norm1_g.reshape(1, D_MODEL)
    tile = lambda g: jnp.tile(g.astype(F32), 2)
    qkg = jnp.zeros((8, LANES), F32).at[0].set(tile(diff_q_norm_g) * (DIFF_DH ** -0.5 * LOG2E)).at[1].set(tile(diff_k_norm_g))
    lane = jnp.arange(2 * LANES)
    same_group = lane[:, None] // DIFF_DH == lane[None, :] // DIFF_DH
    gmean = jnp.where(same_group, 1.0 / DIFF_DH, 0.0).astype(BF16)
    gmat = same_group[:LANES, :LANES].astype(BF16)
    x2 = x.reshape(n_tok, D_MODEL)
    tm = min(1024, seq)
    p_lat = _inproj(x2, g1, sh_a, sc_a, w_in_bf, qkg, gmean, _rope_tables(seq), tm, seq // tm,
                    tuple(range(IN_COLS // COL_TILE)))
    p_ctx = _inproj(ctx.reshape(batch * ctx_len, D_MODEL), g1, csh_a, csc_a, w_in_bf, qkg, gmean,
                    _identity_tables(ctx_len), ctx_len, 1, CTX_COLS)

    lg = jax.nn.log_sigmoid(ret_decay_logit.astype(F32))
    y_ret = _retention(lg, p_lat, p_ctx, ret_norm_g, batch, seq, ctx_len,
                       CTX_COLS.index(1) * COL_TILE // RET_DK, CTX_COLS.index(2) * COL_TILE // RET_DV)

    lp = diff_lambda.astype(F32)
    lam = (jnp.exp(jnp.sum(lp[0] * lp[1])) - jnp.exp(jnp.sum(lp[2] * lp[3])) + LAMBDA_INIT).reshape(1)
    y_diff = _diff_attention(lam, p_lat, p_ctx, diff_norm_g, gmat, batch, seq, ctx_len,
                             CTX_COLS.index(7) * COL_TILE // LANES, CTX_COLS.index(8) * COL_TILE // LANES)

    rw = jnp.zeros((D_MODEL, LANES), F32).at[:, :N_EXPERTS].set(router_w)
    rw_hi = rw.astype(BF16)
    rw_lo = (rw - rw_hi.astype(F32)).astype(BF16)
    rb = jnp.full((1, LANES), NEG_BIG, F32).at[0, :N_EXPERTS].set(router_b)
    xn, hp, meta, gate4, cnt = _merge(
        y_ret, y_diff, p_lat, x2, g_a, sh_f, sc_f, norm2_g.reshape(1, D_MODEL),
        w_br_ret.astype(BF16), w_br_diff.astype(BF16), w_out.astype(BF16), rw_hi, rw_lo, rb, seq)

    counts = cnt[0, :N_EXPERTS].astype(I32)
    padded = (counts + MOE_BLK - 1) // MOE_BLK * MOE_BLK
    pad_end = jnp.cumsum(padded)
    pad_start = pad_end - padded
    n_pairs = n_tok * TOP_K
    n_blocks = n_pairs // MOE_BLK + N_EXPERTS
    meta_i = meta.astype(I32)
    is_e = meta_i[None, :TOP_K] == jnp.arange(N_EXPERTS, dtype=I32)[:, None, None]
    dest = (jnp.sum(jnp.where(is_e, pad_start[:, None, None], 0), axis=0) + meta_i[TOP_K:]).reshape(n_pairs)
    block_start = jnp.arange(n_blocks, dtype=I32) * MOE_BLK
    block_e = jnp.minimum(jnp.sum((pad_end[None, :] <= block_start[:, None]).astype(I32), axis=1), N_EXPERTS - 1)
    n_used = (pad_end[-1] // MOE_BLK).reshape(1).astype(I32)

    fill_start = ((pad_start + counts) // SUBLANES * SUBLANES).astype(I32)
    xb = _dispatch(dest, fill_start, n_used, hp, (n_blocks + 1) * MOE_BLK)
    e_ids = jnp.arange(N_EXPERTS, dtype=I32)
    later = jnp.where((counts[None, :] > 0) & (e_ids[None, :] > e_ids[:, None]), e_ids[None, :], N_EXPERTS)
    next_e = jnp.min(later, axis=1)
    next_e = jnp.where(next_e == N_EXPERTS, -1, next_e).astype(I32)
    yb = _experts(block_e, n_used, next_e, xb, exp_w1, exp_b1, exp_w2, exp_b2)
    out = _combine(dest, yb, gate4, xn, g_f, seq)
    return out.reshape(batch, seq, D_MODEL)


def kernel(x, c, ctx, c_ctx, norm1_g, norm2_g, w_mod, b_mod, w_in, ret_decay_logit, ret_norm_g, diff_q_norm_g, diff_k_norm_g, diff_lambda, diff_norm_g, w_br_ret, w_br_diff, w_out, router_w, router_b, exp_w1, exp_b1, exp_w2, exp_b2):
    assert norm1_g.shape[0] == 1, "single-layer block"
    return _layer(x, ctx, c, c_ctx, norm1_g[0], norm2_g[0], w_mod[0], b_mod[0], w_in[0], ret_decay_logit[0],
                  ret_norm_g[0], diff_q_norm_g[0], diff_k_norm_g[0], diff_lambda[0], diff_norm_g[0],
                  w_br_ret[0], w_br_diff[0], w_out[0], router_w[0], router_b[0],
                  exp_w1[0], exp_b1[0], exp_w2[0], exp_b2[0])
```

```python
import functools
import math

import numpy as np

import jax
import jax.numpy as jnp
from jax import lax
from jax.experimental import pallas as pl
from jax.experimental.pallas import tpu as pltpu

F32 = jnp.float32
BF16 = jnp.bfloat16
U32 = jnp.uint32
I32 = jnp.int32

D_MODEL = 1024
GRID_W = 64
RET_HEADS = 4
RET_DK = 256
RET_DV = 512
DIFF_DH = 64
DIFF_HEADS = 8
DIFF_DV = 128
N_EXPERTS = 32
TOP_K = 4
D_FF = 1024
SWIGLU_LIMIT = 7.0
SWIGLU_ALPHA = 1.702
ROPE_BASE = 10000.0
EPS = 1e-6
LAMBDA_INIT = 0.8 - 0.6 * math.exp(-0.3 * 0)

IN_COLS = 11264
COL_TILE = 1024
CTX_COLS = (1, 2, 3, 7, 8)
LANES = 128
SUBLANES = 8
MOE_BLK = 256
NEG_BIG = -1e30
LOG2E = 1.4426950408889634
SHIFT_SLACK = 1.0 + 2.0 ** -6
MAX_SAFE_SHIFT = 60.0
RET_SUB = 8
KV_UNROLL = 4
HIGHEST = lax.Precision.HIGHEST
MIB = 1024 * 1024


def _params(sem, vmem_mib):
    return pltpu.CompilerParams(dimension_semantics=sem, vmem_limit_bytes=vmem_mib * MIB)


def _mod_kernel(c_ref, w_ref, b_ref, o_ref):
    c = c_ref[...]
    s = c * jax.nn.sigmoid(c)
    o_ref[...] = jnp.dot(s, w_ref[...], preferred_element_type=F32, precision=HIGHEST) + b_ref[...]


def _mod(cc, w_mod, b_mod):
    n = w_mod.shape[1]
    tn = 1024
    return pl.pallas_call(
        _mod_kernel,
        out_shape=jax.ShapeDtypeStruct((8, n), F32),
        grid=(n // tn,),
        in_specs=[pl.BlockSpec((8, D_MODEL), lambda j: (0, 0)),
                  pl.BlockSpec((D_MODEL, tn), lambda j: (0, j)),
                  pl.BlockSpec((1, tn), lambda j: (0, j))],
        out_specs=pl.BlockSpec((8, tn), lambda j: (0, j)),
        compiler_params=_params(("arbitrary",), 32),
        name="mod",
    )(cc, w_mod, b_mod.reshape(1, n))


def _inproj_kernel(cols_ref, x_ref, g_ref, sh_ref, sc_ref, w_ref, qkg_ref, gmat_ref,
                   cr_ref, sr_ref, cc_ref, sc2_ref, cd_ref, sa_ref, sb_ref,
                   o_ref, h_scr, acc_scr):
    j = pl.program_id(1)
    jc = cols_ref[j]

    @pl.when(j == 0)
    def _():
        xf = x_ref[...]
        ms = jnp.mean(xf * xf, axis=-1, keepdims=True)
        y = xf * lax.rsqrt(ms + EPS) * g_ref[...]
        h_scr[...] = (y * (1.0 + sc_ref[0]) + sh_ref[0]).astype(BF16)

    is_ret = jc <= 1
    is_dqk = (jc == 6) | (jc == 7)

    def project():
        return jnp.dot(h_scr[...], w_ref[...], preferred_element_type=F32)

    @pl.when(is_ret)
    def _():
        acc_scr[...] = project()
        scale = jnp.where(jc == 0, RET_DK ** -0.5, 1.0).astype(F32)
        for b in range(COL_TILE // LANES):
            xb = acc_scr[:, b * LANES:(b + 1) * LANES]
            cos = cr_ref[...] if b % 2 == 0 else cc_ref[...]
            sin = sr_ref[...] if b % 2 == 0 else sc2_ref[...]
            o = (xb * cos + pltpu.roll(xb, 64, 1) * sin) * scale
            o_ref[:, b * LANES:(b + 1) * LANES] = o.astype(BF16)

    @pl.when(is_dqk)
    def _():
        acc_scr[...] = project()
        g = jnp.where(jc == 6, qkg_ref[0:1, :], qkg_ref[1:2, :])
        for b2 in range(COL_TILE // (2 * LANES)):
            x2 = acc_scr[:, b2 * 2 * LANES:(b2 + 1) * 2 * LANES]
            ms2 = jnp.dot((x2 * x2).astype(BF16), gmat_ref[...], preferred_element_type=F32)
            for half in range(2):
                b = 2 * b2 + half
                xb = x2[:, half * LANES:(half + 1) * LANES]
                yn = xb * lax.rsqrt(ms2[:, half * LANES:(half + 1) * LANES] + EPS) * g
                o = yn * cd_ref[...] + pltpu.roll(yn, 16, 1) * sa_ref[...] + pltpu.roll(yn, 112, 1) * sb_ref[...]
                o_ref[:, b * LANES:(b + 1) * LANES] = o.astype(BF16)

    @pl.when(jnp.logical_not(is_ret | is_dqk))
    def _():
        o_ref[...] = project().astype(BF16)


def _inproj(x2, g1, sh, sc, w_bf, qkg, gmat, tables, tm, tiles_per_batch, cols):
    n = x2.shape[0]
    nb = sh.shape[0]
    tab_spec = pl.BlockSpec((tm, LANES), lambda i, j, c: (i % tiles_per_batch, 0))
    mod_spec = pl.BlockSpec((1, 1, D_MODEL), lambda i, j, c: (jnp.minimum(i // tiles_per_batch, nb - 1), 0, 0))
    return pl.pallas_call(
        _inproj_kernel,
        out_shape=jax.ShapeDtypeStruct((n, len(cols) * COL_TILE), BF16),
        grid_spec=pltpu.PrefetchScalarGridSpec(
            num_scalar_prefetch=1,
            grid=(n // tm, len(cols)),
            in_specs=[pl.BlockSpec((tm, D_MODEL), lambda i, j, c: (i, 0)),
                      pl.BlockSpec((1, D_MODEL), lambda i, j, c: (0, 0)),
                      mod_spec, mod_spec,
                      pl.BlockSpec((D_MODEL, COL_TILE), lambda i, j, c: (0, c[j])),
                      pl.BlockSpec((8, LANES), lambda i, j, c: (0, 0)),
                      pl.BlockSpec((2 * LANES, 2 * LANES), lambda i, j, c: (0, 0))] + [tab_spec] * 7,
            out_specs=pl.BlockSpec((tm, COL_TILE), lambda i, j, c: (i, j)),
            scratch_shapes=[pltpu.VMEM((tm, D_MODEL), BF16), pltpu.VMEM((tm, COL_TILE), F32)]),
        compiler_params=_params(("arbitrary", "arbitrary"), 48),
        name="inproj",
    )(jnp.asarray(cols, I32), x2, g1, sh, sc, w_bf, qkg, gmat, *tables)


def _rope_tables(seq):
    n_rows = seq // GRID_W
    f32 = np.float32

    def angles(pos, half):
        inv = f32(ROPE_BASE) ** (-np.arange(half, dtype=f32) / f32(half))
        return (pos.astype(f32)[:, None] * inv[None, :]).astype(np.float64)

    ar, ac = angles(np.arange(n_rows), 64), angles(np.arange(GRID_W), 64)
    br, bc = angles(np.arange(n_rows), 16), angles(np.arange(GRID_W), 16)
    zr, zc = np.zeros_like(br), np.zeros_like(bc)
    cat = lambda parts, reps=1: np.tile(np.concatenate(parts, axis=1), (1, reps)).astype(f32)
    by_row = lambda t: jnp.repeat(jnp.asarray(t), GRID_W, axis=0)
    by_col = lambda t: jnp.tile(jnp.asarray(t), (n_rows, 1))
    cr = by_row(cat([np.cos(ar), np.cos(ar)]))
    sr = by_row(cat([-np.sin(ar), np.sin(ar)]))
    cc = by_col(cat([np.cos(ac), np.cos(ac)]))
    sc = by_col(cat([-np.sin(ac), np.sin(ac)]))
    cd = by_row(cat([np.cos(br), np.cos(br), zr, zr], 2)) + by_col(cat([zc, zc, np.cos(bc), np.cos(bc)], 2))
    sa = by_row(cat([zr, np.sin(br), zr, zr], 2)) + by_col(cat([zc, zc, zc, np.sin(bc)], 2))
    sb = by_row(cat([-np.sin(br), zr, zr, zr], 2)) + by_col(cat([zc, zc, -np.sin(bc), zc], 2))
    return [cr, sr, cc, sc, cd, sa, sb]


def _identity_tables(seq):
    one = jnp.ones((seq, LANES), F32)
    zero = jnp.zeros((seq, LANES), F32)
    return [one, zero, one, zero, one, zero, zero]


def _tn_dot(a, b):
    return lax.dot_general(a, b, (((0,), (0,)), ((), ())), preferred_element_type=F32)


def _nt_dot(a, b):
    return lax.dot_general(a, b, (((1,), (1,)), ((), ())), preferred_element_type=F32)


def _ret_kernel(lg_ref, q_ref, k_ref, v_ref, g_ref, ck_ref, cv_ref, gn_ref, o_ref,
                sf_scr, sb_scr, ob_scr, *, chunk, sub, n_blocks, ctx_len):
    h = pl.program_id(1)
    p = pl.program_id(2)
    c = pl.program_id(3)
    lgf = lg_ref[0, h]
    lgb = lg_ref[1, h]

    def col_iota(n):
        return lax.broadcasted_iota(I32, (n, 1), 0).astype(F32)

    def vexp(s):
        return jnp.exp(jnp.zeros((1, 1), F32) + s)

    @pl.when((p == 0) & (c == 0))
    def _():
        jc = col_iota(ctx_len)
        kc = ck_ref[...].astype(F32)
        vc = cv_ref[...]
        sf_scr[...] = _tn_dot((kc * jnp.exp(lgf * (ctx_len - 1.0 - jc))).astype(BF16), vc)
        sb_scr[...] = _tn_dot((kc * jnp.exp(lgb * jc)).astype(BF16), vc)

    ic = col_iota(chunk)

    def full_bf16(col):
        return jnp.broadcast_to(col, (chunk, RET_DK)).astype(BF16)

    @pl.when(p == 0)
    def _():
        q_decay = full_bf16(jnp.exp(lgb * (chunk - ic)))
        k_decay = full_bf16(jnp.exp(lgb * ic))
        s_decay = vexp(lgb * chunk)
        for j in reversed(range(sub)):
            loc = pl.ds(j * chunk, chunk)
            glob = pl.ds(pl.multiple_of(((n_blocks - 1 - c) * sub + j) * chunk, chunk), chunk)
            qb = q_ref[loc, :] * q_decay
            ob_scr[glob, :] = jnp.dot(qb, sb_scr[...].astype(BF16), preferred_element_type=F32)
            kb = k_ref[loc, :] * k_decay
            sb_scr[...] = s_decay * sb_scr[...] + _tn_dot(kb, v_ref[loc, :])

    @pl.when(p == 1)
    def _():
        ri = lax.broadcasted_iota(I32, (chunk, chunk), 0)
        ci = lax.broadcasted_iota(I32, (chunk, chunk), 1)
        d = (ri - ci).astype(F32)
        mask = jnp.where(d > 0, jnp.exp(lgf * jnp.maximum(d, 0.0)),
                         jnp.where(d < 0, jnp.exp(lgb * jnp.maximum(-d, 0.0)), 2.0))
        q_decay = full_bf16(jnp.exp(lgf * (ic + 1.0)))
        k_decay = full_bf16(jnp.exp(lgf * (chunk - 1.0 - ic)))
        s_decay = vexp(lgf * chunk)
        for j in range(sub):
            loc = pl.ds(j * chunk, chunk)
            glob = pl.ds(pl.multiple_of((c * sub + j) * chunk, chunk), chunk)
            a = (_nt_dot(q_ref[loc, :], k_ref[loc, :]) * mask).astype(BF16)
            qf = q_ref[loc, :] * q_decay
            o = (jnp.dot(a, v_ref[loc, :], preferred_element_type=F32)
                 + jnp.dot(qf, sf_scr[...].astype(BF16), preferred_element_type=F32)
                 + ob_scr[glob, :])
            kf = k_ref[loc, :] * k_decay
            sf_scr[...] = s_decay * sf_scr[...] + _tn_dot(kf, v_ref[loc, :])
            ms = jnp.mean(o * o, axis=-1, keepdims=True)
            y = o * lax.rsqrt(ms + EPS) * gn_ref[0]
            gt = g_ref[loc, :]
            o_ref[loc, :] = y.astype(BF16) * (gt * jax.nn.sigmoid(gt))


def _retention(lg, p_lat, p_ctx, ret_norm_g, batch, seq, ctx_len, ctx_k_blk, ctx_v_blk):
    chunk = min(256, seq)
    sub = math.gcd(seq // chunk, RET_SUB)
    blk = chunk * sub
    nb = seq // blk
    kern = functools.partial(_ret_kernel, chunk=chunk, sub=sub, n_blocks=nb, ctx_len=ctx_len)

    def rows(b, p, c):
        return b * nb + jnp.where(p == 0, nb - 1 - c, c)

    def rows_fwd(b, p, c):
        return b * nb + jnp.where(p == 0, 0, c)

    return pl.pallas_call(
        kern,
        out_shape=jax.ShapeDtypeStruct((batch * seq, RET_HEADS * RET_DV), BF16),
        grid=(batch, RET_HEADS, 2, nb),
        in_specs=[pl.BlockSpec(memory_space=pltpu.SMEM),
                  pl.BlockSpec((blk, RET_DK), lambda b, h, p, c: (rows(b, p, c), h)),
                  pl.BlockSpec((blk, RET_DK), lambda b, h, p, c: (rows(b, p, c), 4 + h)),
                  pl.BlockSpec((blk, RET_DV), lambda b, h, p, c: (rows(b, p, c), 4 + h)),
                  pl.BlockSpec((blk, RET_DV), lambda b, h, p, c: (rows_fwd(b, p, c), 8 + h)),
                  pl.BlockSpec((ctx_len, RET_DK), lambda b, h, p, c: (b, ctx_k_blk + h)),
                  pl.BlockSpec((ctx_len, RET_DV), lambda b, h, p, c: (b, ctx_v_blk + h)),
                  pl.BlockSpec((1, 1, RET_DV), lambda b, h, p, c: (h, 0, 0))],
        out_specs=pl.BlockSpec((blk, RET_DV), lambda b, h, p, c: (rows_fwd(b, p, c), h)),
        scratch_shapes=[pltpu.VMEM((RET_DK, RET_DV), F32), pltpu.VMEM((RET_DK, RET_DV), F32),
                        pltpu.VMEM((seq, RET_DV), F32)],
        compiler_params=_params(("arbitrary",) * 4, 48),
        name="ret",
    )(lg, p_lat, p_lat, p_lat, p_lat, p_ctx, p_ctx, ret_norm_g.reshape(RET_HEADS, 1, RET_DV))


def _dattn_kernel(lam_ref, q_ref, ck_ref, cv_ref, k_ref, v_ref, gn_ref, gmat_ref, o_ref,
                  qq_scr, kmax_scr, mp_scr, kp_scr, vt_scr, pt_scr, *, tq, tk, n_kv, ta, unroll, ctx_len):
    rows = 2 * tq
    n_all = kp_scr.shape[0]
    qi = pl.program_id(2)

    def sq_norms(t):
        tf = t.astype(F32)
        return jnp.dot((tf * tf).astype(BF16), gmat_ref[...], preferred_element_type=F32)

    def lane_const(n, hot):
        return jnp.where(lax.broadcasted_iota(I32, (n, LANES), 1) < hot, 1.0, 0.0).astype(BF16)

    @pl.when(qi == 0)
    def _():
        kmax_scr[...] = jnp.max(sq_norms(ck_ref[...]), axis=0, keepdims=True)
        kp_scr[:, LANES:2 * LANES] = lane_const(n_all, 3)
        kp_scr[0:ctx_len, 0:LANES] = ck_ref[...]
        vt_scr[:, 0:ctx_len] = cv_ref[...].astype(F32).T.astype(BF16)

        prep = math.gcd(n_kv, 4)

        def body(c, carry):
            norms = []
            for u in range(prep):
                start = pl.multiple_of((c * prep + u) * tk, tk)
                k = k_ref[pl.ds(start, tk), :]
                norms.append(jnp.max(sq_norms(k), axis=0, keepdims=True))
                kp_scr[pl.ds(pl.multiple_of(ctx_len + start, LANES), tk), 0:LANES] = k
                vt_scr[:, pl.ds(pl.multiple_of(ctx_len + start, LANES), tk)] = (
                    v_ref[pl.ds(start, tk), :].astype(F32).T.astype(BF16))
            kmax_scr[...] = jnp.maximum(kmax_scr[...], functools.reduce(jnp.maximum, norms))
            return carry

        lax.fori_loop(0, n_kv // prep, body, 0)

    qt = q_ref[...].astype(F32).T
    row = lax.broadcasted_iota(I32, (LANES, tq), 0)
    q1t = jnp.where(row < DIFF_DH, qt, 0.0)
    q2t = jnp.where(row >= DIFF_DH, qt, 0.0)
    qq_scr[0:LANES, 0:tq] = q1t.astype(BF16)
    qq_scr[0:LANES, tq:rows] = q2t.astype(BF16)

    def set_shift(shift):
        neg = -shift
        hi = neg.astype(BF16).astype(F32)
        mid = (neg - hi).astype(BF16).astype(F32)
        lo = neg - hi - mid
        row_r = lax.broadcasted_iota(I32, (LANES, rows), 0)
        pieces = jnp.where(row_r == 0, hi, jnp.where(row_r == 1, mid, jnp.where(row_r == 2, lo, 0.0)))
        qq_scr[LANES:2 * LANES, :] = pieces.astype(BF16)

    kmax = kmax_scr[...]
    b1 = jnp.sqrt(jnp.sum(q1t * q1t, axis=0, keepdims=True) * kmax[:, 0:1]) * SHIFT_SLACK
    b2 = jnp.sqrt(jnp.sum(q2t * q2t, axis=0, keepdims=True) * kmax[:, DIFF_DH:DIFF_DH + 1]) * SHIFT_SLACK
    bound = jnp.concatenate([b1, b2], axis=1)
    set_shift(bound)

    def key_tile(t):
        return pl.ds(pl.multiple_of(t * ta, ta), ta)

    @pl.when(jnp.max(bound) > MAX_SAFE_SHIFT)
    def _():
        mp_scr[...] = jnp.full(mp_scr.shape, NEG_BIG, F32)

        def body(t, carry):
            st = jnp.dot(kp_scr[key_tile(t), 0:LANES], qq_scr[0:LANES, :], preferred_element_type=F32)
            mp_scr[...] = jnp.maximum(mp_scr[...], jnp.max(st, axis=0, keepdims=True))
            return carry

        lax.fori_loop(0, n_all // ta, body, 0)
        set_shift(mp_scr[0:1, :])

    mp_scr[...] = jnp.zeros(mp_scr.shape, F32)

    def body_a(t, carry):
        for u in range(unroll):
            r = key_tile(t * unroll + u)
            p = jnp.exp2(jnp.dot(kp_scr[r, :], qq_scr[...], preferred_element_type=F32))
            pt_scr[r, :] = p.astype(BF16)
            mp_scr[...] += jnp.sum(p.reshape(ta // SUBLANES, SUBLANES, rows), axis=0)
        return carry

    lax.fori_loop(0, n_all // (ta * unroll), body_a, 0)

    acc = jnp.dot(vt_scr[...], pt_scr[...], preferred_element_type=F32)
    ot = acc / jnp.sum(mp_scr[...], axis=0, keepdims=True)
    d = (ot[:, 0:tq] - lam_ref[0] * ot[:, tq:rows]).T
    ms = jnp.mean(d * d, axis=-1, keepdims=True)
    y = d * lax.rsqrt(ms + EPS) * gn_ref[...] * (1.0 - LAMBDA_INIT)
    o_ref[...] = y.astype(BF16)


def _diff_attention(lam, p_lat, p_ctx, diff_norm_g, gmat, batch, seq, ctx_len, ctx_k_blk, ctx_v_blk):
    tq = min(1024, seq)
    tk = min(512, seq)
    nq, nk = seq // tq, seq // tk
    n_all = ctx_len + seq
    ta = next(t for t in (528, 512, 384, 320, 256, 128) if n_all % t == 0)
    unroll = math.gcd(n_all // ta, KV_UNROLL)
    assert ctx_len % LANES == 0
    kern = functools.partial(_dattn_kernel, tq=tq, tk=tk, n_kv=nk, ta=ta, unroll=unroll, ctx_len=ctx_len)
    return pl.pallas_call(
        kern,
        out_shape=jax.ShapeDtypeStruct((batch * seq, DIFF_HEADS * DIFF_DV), BF16),
        grid=(batch, DIFF_HEADS, nq),
        in_specs=[pl.BlockSpec(memory_space=pltpu.SMEM),
                  pl.BlockSpec((tq, LANES), lambda b, h, qi: (b * nq + qi, 48 + h)),
                  pl.BlockSpec((ctx_len, LANES), lambda b, h, qi: (b, ctx_k_blk + h)),
                  pl.BlockSpec((ctx_len, LANES), lambda b, h, qi: (b, ctx_v_blk + h)),
                  pl.BlockSpec((seq, LANES), lambda b, h, qi: (b, 56 + h)),
                  pl.BlockSpec((seq, LANES), lambda b, h, qi: (b, 64 + h)),
                  pl.BlockSpec((1, LANES), lambda b, h, qi: (0, 0)),
                  pl.BlockSpec((LANES, LANES), lambda b, h, qi: (0, 0))],
        out_specs=pl.BlockSpec((tq, LANES), lambda b, h, qi: (b * nq + qi, h)),
        scratch_shapes=[pltpu.VMEM((2 * LANES, 2 * tq), BF16), pltpu.VMEM((1, LANES), F32),
                        pltpu.VMEM((SUBLANES, 2 * tq), F32), pltpu.VMEM((n_all, 2 * LANES), BF16),
                        pltpu.VMEM((DIFF_DV, n_all), BF16), pltpu.VMEM((n_all, 2 * tq), BF16)],
        compiler_params=_params(("arbitrary",) * 3, 58),
        name="dattn",
    )(lam, p_lat, p_ctx, p_ctx, p_lat, p_lat, diff_norm_g.reshape(1, DIFF_DV), gmat)


def _merge_kernel(yr_ref, yd_ref, ga_ref, gb_ref, x_ref, gatea_ref, shf_ref, scf_ref, n2_ref,
                  wr_ref, wd_ref, wo_ref, rwh_ref, rwl_ref, rb_ref,
                  xn_ref, hp_ref, meta_ref, gate_ref, cnt_ref, run_scr, *, tm):
    i = pl.program_id(0)

    @pl.when(i == 0)
    def _():
        run_scr[...] = jnp.zeros(run_scr.shape, F32)

    yr = jnp.dot(yr_ref[...], wr_ref[...], preferred_element_type=F32)
    yd = jnp.dot(yd_ref[...], wd_ref[...], preferred_element_type=F32)
    m = (jax.nn.sigmoid(ga_ref[...].astype(F32)) * yr + jax.nn.sigmoid(gb_ref[...].astype(F32)) * yd)
    z = jnp.dot(m.astype(BF16), wo_ref[...], preferred_element_type=F32)
    xn = x_ref[...] + gatea_ref[0] * z
    xn_ref[...] = xn

    ms = jnp.mean(xn * xn, axis=-1, keepdims=True)
    h2 = xn * lax.rsqrt(ms + EPS) * n2_ref[...]
    h2 = h2 * (1.0 + scf_ref[0]) + shf_ref[0]
    h_hi = h2.astype(BF16)
    bits = pltpu.bitcast(h_hi.astype(F32), U32)
    half = D_MODEL // 2
    hp_ref[...] = (bits[:, :half] >> 16) | (bits[:, half:] & jnp.uint32(0xFFFF0000))

    h_lo = (h2 - h_hi.astype(F32)).astype(BF16)
    logits = (jnp.dot(h_hi, rwh_ref[...], preferred_element_type=F32)
              + jnp.dot(h_lo, rwh_ref[...], preferred_element_type=F32)
              + jnp.dot(h_hi, rwl_ref[...], preferred_element_type=F32) + rb_ref[...])
    lane = lax.broadcasted_iota(I32, (tm, LANES), 1)
    lanef = lane.astype(F32)
    work = logits
    vals, idxs = [], []
    for _ in range(TOP_K):
        mk = jnp.max(work, axis=1, keepdims=True)
        ik = jnp.min(jnp.where(work == mk, lanef, float(LANES)), axis=1, keepdims=True)
        vals.append(mk)
        idxs.append(ik)
        work = jnp.where(lanef == ik, -jnp.inf, work)
    ex = [jnp.exp(v - vals[0]) for v in vals]
    den = ex[0] + ex[1] + ex[2] + ex[3]

    onehot = jnp.zeros((tm, LANES), F32)
    for ik in idxs:
        onehot = onehot + jnp.where(lanef == ik, 1.0, 0.0)
    ri = lax.broadcasted_iota(I32, (tm, tm), 0)
    ci = lax.broadcasted_iota(I32, (tm, tm), 1)
    tri = jnp.where(ri > ci, 1.0, 0.0).astype(BF16)
    base = run_scr[0:1, :] + jnp.dot(tri, onehot.astype(BF16), preferred_element_type=F32)
    run_scr[...] = run_scr[...] + jnp.sum(onehot, axis=0, keepdims=True)

    meta = jnp.zeros((tm, LANES), F32)
    gate_out = jnp.zeros((tm, LANES), F32)
    for k in range(TOP_K):
        rk = jnp.sum(jnp.where(lanef == idxs[k], base, 0.0), axis=1, keepdims=True)
        meta = jnp.where(lane == k, idxs[k], jnp.where(lane == TOP_K + k, rk, meta))
        gate_out = jnp.where(lane == k, ex[k] / den, gate_out)
    meta_ref[...] = meta.T[0:2 * TOP_K, :]
    gate_ref[...] = gate_out
    cnt_ref[...] = run_scr[...]


def _merge(y_ret, y_diff, p_lat, x2, g_a, sh_f, sc_f, norm2_g, w_r, w_d, w_o, rw_hi, rw_lo, rb, seq):
    n = x2.shape[0]
    tm = min(512, seq)
    tpb = seq // tm
    kern = functools.partial(_merge_kernel, tm=tm)
    mod_spec = pl.BlockSpec((1, 1, D_MODEL), lambda i: (i // tpb, 0, 0))
    const = lambda shape: pl.BlockSpec(shape, lambda i: (0,) * len(shape))
    tok = lambda w: pl.BlockSpec((tm, w), lambda i: (i, 0))
    return pl.pallas_call(
        kern,
        out_shape=(jax.ShapeDtypeStruct((n, D_MODEL), F32),
                   jax.ShapeDtypeStruct((n, D_MODEL // 2), U32),
                   jax.ShapeDtypeStruct((2 * TOP_K, n), F32),
                   jax.ShapeDtypeStruct((n, LANES), F32),
                   jax.ShapeDtypeStruct((8, LANES), F32)),
        grid=(n // tm,),
        in_specs=[tok(RET_HEADS * RET_DV), tok(D_MODEL),
                  pl.BlockSpec((tm, COL_TILE), lambda i: (i, 9)),
                  pl.BlockSpec((tm, COL_TILE), lambda i: (i, 10)),
                  tok(D_MODEL), mod_spec, mod_spec, mod_spec, const((1, D_MODEL)),
                  const((RET_HEADS * RET_DV, D_MODEL)), const((D_MODEL, D_MODEL)), const((D_MODEL, D_MODEL)),
                  const((D_MODEL, LANES)), const((D_MODEL, LANES)), const((1, LANES))],
        out_specs=(tok(D_MODEL), tok(D_MODEL // 2), pl.BlockSpec((2 * TOP_K, tm), lambda i: (0, i)),
                   tok(LANES), const((8, LANES))),
        scratch_shapes=[pltpu.VMEM((8, LANES), F32)],
        compiler_params=_params(("arbitrary",), 56),
        name="merge",
    )(y_ret, y_diff, p_lat, p_lat, x2, g_a, sh_f, sc_f, norm2_g, w_r, w_d, w_o, rw_hi, rw_lo, rb)


def _dispatch_kernel(dest_ref, fill_ref, nb_ref, hp_ref, xb_ref, zero_scr, stage, sems, zsem, *,
                     tm, n_tok, n_blocks, n_steps):
    i = pl.program_id(0)

    @pl.when(i == 0)
    def _():
        zero_scr[...] = jnp.zeros(zero_scr.shape, U32)
        fill = zero_scr.shape[0]
        for e in range(N_EXPERTS):
            start = pl.multiple_of(fill_ref[e], SUBLANES)
            pltpu.make_async_copy(zero_scr, xb_ref.at[pl.ds(start, fill), :], zsem).start()
        for e in range(N_EXPERTS):
            pltpu.make_async_copy(zero_scr, xb_ref.at[pl.ds(0, fill), :], zsem).wait()
        blk = zero_scr.at[pl.ds(0, MOE_BLK), :]

        def start_blk(b, carry):
            pltpu.make_async_copy(blk, xb_ref.at[pl.ds(pl.multiple_of(b * MOE_BLK, MOE_BLK), MOE_BLK), :], zsem).start()
            return carry

        def wait_blk(b, carry):
            pltpu.make_async_copy(blk, xb_ref.at[pl.ds(0, MOE_BLK), :], zsem).wait()
            return carry

        lax.fori_loop(nb_ref[0], n_blocks, start_blk, 0)
        lax.fori_loop(nb_ref[0], n_blocks, wait_blk, 0)

    def drain(slot):
        for _ in range(TOP_K):
            pltpu.make_async_copy(stage.at[slot], xb_ref.at[pl.ds(0, tm), :], sems.at[slot]).wait()

    for slot in range(2):
        @pl.when(lax.rem(i, 2) == slot)
        def _():
            stage[slot] = hp_ref[...]

            def body(g, carry):
                r0 = pl.multiple_of(g * SUBLANES, SUBLANES)
                for j in range(SUBLANES):
                    for k in range(TOP_K):
                        d = dest_ref[k * n_tok + i * tm + r0 + j]
                        pltpu.make_async_copy(stage.at[slot, pl.ds(r0 + j, 1), :], xb_ref.at[pl.ds(d, 1), :],
                                              sems.at[slot]).start(priority=k % 2)
                return carry

            lax.fori_loop(0, tm // SUBLANES, body, 0)

            @pl.when(i > 0)
            def _():
                drain(1 - slot)

            @pl.when(i == n_steps - 1)
            def _():
                drain(slot)


def _dispatch(dest, fill_start, n_used, hp, n_rows):
    n = hp.shape[0]
    tm = min(512, n)
    kern = functools.partial(_dispatch_kernel, tm=tm, n_tok=n, n_blocks=n_rows // MOE_BLK, n_steps=n // tm)
    return pl.pallas_call(
        kern,
        out_shape=jax.ShapeDtypeStruct((n_rows, D_MODEL // 2), U32),
        grid_spec=pltpu.PrefetchScalarGridSpec(
            num_scalar_prefetch=3,
            grid=(n // tm,),
            in_specs=[pl.BlockSpec((tm, D_MODEL // 2), lambda i, d, f, nb: (i, 0))],
            out_specs=pl.BlockSpec(memory_space=pl.ANY),
            scratch_shapes=[pltpu.VMEM((MOE_BLK + SUBLANES, D_MODEL // 2), U32),
                            pltpu.VMEM((2, tm, D_MODEL // 2), U32),
                            pltpu.SemaphoreType.DMA((2,)), pltpu.SemaphoreType.DMA(())]),
        compiler_params=_params(("arbitrary",), 32),
        name="dispatch",
    )(dest, fill_start, n_used, hp)


def _expert_kernel(be_ref, nb_ref, nxt_ref, xb_ref, w1_hbm, b1_ref, w2_hbm, b2_ref, y_ref,
                   w1f_scr, w2f_scr, w1b_scr, w2b_scr, x_scr, grp_scr, sems):
    i = pl.program_id(0)
    e = be_ref[i]
    prev = be_ref[jnp.maximum(i - 1, 0)]
    active = i < nb_ref[0]

    def weight_copies(expert, slot):
        return (pltpu.make_async_copy(w1_hbm.at[expert], w1f_scr.at[slot], sems.at[0, slot]),
                pltpu.make_async_copy(w2_hbm.at[expert], w2f_scr.at[slot], sems.at[1, slot]))

    @pl.when(i == 0)
    def _():
        grp_scr[0] = 0
        for cp in weight_copies(e, 0):
            cp.start()

    @pl.when(active & (i > 0) & (e != prev))
    def _():
        grp_scr[0] = grp_scr[0] + 1

    @pl.when(active & ((i == 0) | (e != prev)))
    def _():
        slot = grp_scr[0] % 2
        for cp in weight_copies(e, slot):
            cp.wait()
        w1b_scr[...] = w1f_scr[slot].astype(BF16)
        w2b_scr[...] = w2f_scr[slot].astype(BF16)
        nxt = nxt_ref[e]

        @pl.when(nxt >= 0)
        def _():
            for cp in weight_copies(nxt, 1 - slot):
                cp.start()

    @pl.when(active)
    def _():
        xu = xb_ref[...]
        half = D_MODEL // 2
        x_scr[:, 0:half] = pltpu.bitcast(xu << 16, F32).astype(BF16)
        x_scr[:, half:] = pltpu.bitcast(xu & jnp.uint32(0xFFFF0000), F32).astype(BF16)
        hh = jnp.dot(x_scr[...], w1b_scr[...], preferred_element_type=F32) + b1_ref[0]
        glu = jnp.minimum(hh[:, :D_FF], SWIGLU_LIMIT)
        lin = jnp.clip(hh[:, D_FF:], -SWIGLU_LIMIT, SWIGLU_LIMIT)
        act = glu * jax.nn.sigmoid(SWIGLU_ALPHA * glu) * (lin + 1.0)
        y_ref[...] = jnp.dot(act.astype(BF16), w2b_scr[...], preferred_element_type=F32) + b2_ref[0]

    @pl.when(i >= nb_ref[0])
    def _():
        y_ref[...] = jnp.zeros(y_ref.shape, F32)


def _experts(block_e, n_used, next_e, xb, w1, b1, w2, b2):
    n_blocks = block_e.shape[0]
    rows = n_blocks * MOE_BLK
    return pl.pallas_call(
        _expert_kernel,
        out_shape=jax.ShapeDtypeStruct((rows, D_MODEL), F32),
        grid_spec=pltpu.PrefetchScalarGridSpec(
            num_scalar_prefetch=3,
            grid=(n_blocks,),
            in_specs=[pl.BlockSpec((MOE_BLK, D_MODEL // 2), lambda i, be, nb, nx: (jnp.minimum(i, nb[0] - 1), 0)),
                      pl.BlockSpec(memory_space=pl.ANY),
                      pl.BlockSpec((1, 1, 2 * D_FF), lambda i, be, nb, nx: (be[i], 0, 0)),
                      pl.BlockSpec(memory_space=pl.ANY),
                      pl.BlockSpec((1, 1, D_MODEL), lambda i, be, nb, nx: (be[i], 0, 0))],
            out_specs=pl.BlockSpec((MOE_BLK, D_MODEL), lambda i, be, nb, nx: (i, 0)),
            scratch_shapes=[pltpu.VMEM((2, D_MODEL, 2 * D_FF), F32), pltpu.VMEM((2, D_FF, D_MODEL), F32),
                            pltpu.VMEM((D_MODEL, 2 * D_FF), BF16), pltpu.VMEM((D_FF, D_MODEL), BF16),
                            pltpu.VMEM((MOE_BLK, D_MODEL), BF16),
                            pltpu.SMEM((1,), I32), pltpu.SemaphoreType.DMA((2, 2))]),
        compiler_params=_params(("arbitrary",), 56),
        name="expert",
    )(block_e, n_used, next_e, xb, w1, b1.reshape(N_EXPERTS, 1, 2 * D_FF), w2, b2.reshape(N_EXPERTS, 1, D_MODEL))


def _combine_kernel(dest_ref, yb_ref, gate_ref, xn_ref, gf_ref, o_ref, buf, sems, *, tm, n_tok, n_steps):
    i = pl.program_id(0)

    def issue(tile, slot):
        def body(g, carry):
            r0 = pl.multiple_of(g * SUBLANES, SUBLANES)
            for j in range(SUBLANES):
                for k in range(TOP_K):
                    d = dest_ref[k * n_tok + tile * tm + r0 + j]
                    pltpu.make_async_copy(yb_ref.at[pl.ds(d, 1), :], buf.at[slot, k, pl.ds(r0 + j, 1), :],
                                          sems.at[slot]).start(priority=k % 2)
            return carry

        lax.fori_loop(0, tm // SUBLANES, body, 0)

    def finish(slot):
        for k in range(TOP_K):
            pltpu.make_async_copy(yb_ref.at[pl.ds(0, tm), :], buf.at[slot, k], sems.at[slot]).wait()
        g = gate_ref[...]
        y = g[:, 0:1] * buf[slot, 0]
        for k in range(1, TOP_K):
            y = y + g[:, k:k + 1] * buf[slot, k]
        o_ref[...] = xn_ref[...] + gf_ref[0] * y

    @pl.when(i == 0)
    def _():
        issue(0, 0)

    for slot in range(2):
        @pl.when(lax.rem(i, 2) == slot)
        def _():
            @pl.when(i + 1 < n_steps)
            def _():
                issue(i + 1, 1 - slot)

            finish(slot)


def _combine(dest, yb, gate, xn, g_f, seq):
    n = xn.shape[0]
    tm = min(512, seq)
    tpb = seq // tm
    kern = functools.partial(_combine_kernel, tm=tm, n_tok=n, n_steps=n // tm)
    return pl.pallas_call(
        kern,
        out_shape=jax.ShapeDtypeStruct((n, D_MODEL), F32),
        grid_spec=pltpu.PrefetchScalarGridSpec(
            num_scalar_prefetch=1,
            grid=(n // tm,),
            in_specs=[pl.BlockSpec(memory_space=pl.ANY),
                      pl.BlockSpec((tm, LANES), lambda i, d: (i, 0)),
                      pl.BlockSpec((tm, D_MODEL), lambda i, d: (i, 0)),
                      pl.BlockSpec((1, 1, D_MODEL), lambda i, d: (i // tpb, 0, 0))],
            out_specs=pl.BlockSpec((tm, D_MODEL), lambda i, d: (i, 0)),
            scratch_shapes=[pltpu.VMEM((2, TOP_K, tm, D_MODEL), F32), pltpu.SemaphoreType.DMA((2,))]),
        compiler_params=_params(("arbitrary",), 40),
        name="combine",
    )(dest, yb, gate, xn, g_f)


def _layer(x, ctx, c, c_ctx, norm1_g, norm2_g, w_mod, b_mod, w_in, ret_decay_logit, ret_norm_g,
           diff_q_norm_g, diff_k_norm_g, diff_lambda, diff_norm_g, w_br_ret, w_br_diff, w_out,
           router_w, router_b, exp_w1, exp_b1, exp_w2, exp_b2):
    batch, seq, d = x.shape
    ctx_len = ctx.shape[1]
    assert d == D_MODEL and seq % GRID_W == 0 and batch + 1 <= 8
    n_tok = batch * seq

    cc = jnp.zeros((8, D_MODEL), F32).at[:batch].set(c).at[batch].set(c_ctx)
    mod = _mod(cc, w_mod, b_mod)
    sh_a, sc_a, g_a, sh_f, sc_f, g_f = [mod[:batch, i * D_MODEL:(i + 1) * D_MODEL].reshape(batch, 1, D_MODEL)
                                         for i in range(6)]
    csh_a = mod[batch:batch + 1, 0:D_MODEL].reshape(1, 1, D_MODEL)
    csc_a = mod[batch:batch + 1, D_MODEL:2 * D_MODEL].reshape(1, 1, D_MODEL)

    w_in_bf = w_in.astype(BF16)
    g1 = norm1_g.reshape(1, D_MODEL)
    tile = lambda g: jnp.tile(g.astype(F32), 2)
    qkg = jnp.zeros((8, LANES), F32).at[0].set(tile(diff_q_norm_g) * (DIFF_DH ** -0.5 * LOG2E)).at[1].set(tile(diff_k_norm_g))
    lane = jnp.arange(2 * LANES)
    same_group = lane[:, None] // DIFF_DH == lane[None, :] // DIFF_DH
    gmean = jnp.where(same_group, 1.0 / DIFF_DH, 0.0).astype(BF16)
    gmat = same_group[:LANES, :LANES].astype(BF16)
    x2 = x.reshape(n_tok, D_MODEL)
    tm = min(1024, seq)
    p_lat = _inproj(x2, g1, sh_a, sc_a, w_in_bf, qkg, gmean, _rope_tables(seq), tm, seq // tm,
                    tuple(range(IN_COLS // COL_TILE)))
    p_ctx = _inproj(ctx.reshape(batch * ctx_len, D_MODEL), g1, csh_a, csc_a, w_in_bf, qkg, gmean,
                    _identity_tables(ctx_len), ctx_len, 1, CTX_COLS)

    lg = jax.nn.log_sigmoid(ret_decay_logit.astype(F32))
    y_ret = _retention(lg, p_lat, p_ctx, ret_norm_g, batch, seq, ctx_len,
                       CTX_COLS.index(1) * COL_TILE // RET_DK, CTX_COLS.index(2) * COL_TILE // RET_DV)

    lp = diff_lambda.astype(F32)
    lam = (jnp.exp(jnp.sum(lp[0] * lp[1])) - jnp.exp(jnp.sum(lp[2] * lp[3])) + LAMBDA_INIT).reshape(1)
    y_diff = _diff_attention(lam, p_lat, p_ctx, diff_norm_g, gmat, batch, seq, ctx_len,
                             CTX_COLS.index(7) * COL_TILE // LANES, CTX_COLS.index(8) * COL_TILE // LANES)

    rw = jnp.zeros((D_MODEL, LANES), F32).at[:, :N_EXPERTS].set(router_w)
    rw_hi = rw.astype(BF16)
    rw_lo = (rw - rw_hi.astype(F32)).astype(BF16)
    rb = jnp.full((1, LANES), NEG_BIG, F32).at[0, :N_EXPERTS].set(router_b)
    xn, hp, meta, gate4, cnt = _merge(
        y_ret, y_diff, p_lat, x2, g_a, sh_f, sc_f, norm2_g.reshape(1, D_MODEL),
        w_br_ret.astype(BF16), w_br_diff.astype(BF16), w_out.astype(BF16), rw_hi, rw_lo, rb, seq)

    counts = cnt[0, :N_EXPERTS].astype(I32)
    padded = (counts + MOE_BLK - 1) // MOE_BLK * MOE_BLK
    pad_end = jnp.cumsum(padded)
    pad_start = pad_end - padded
    n_pairs = n_tok * TOP_K
    n_blocks = n_pairs // MOE_BLK + N_EXPERTS
    meta_i = meta.astype(I32)
    is_e = meta_i[None, :TOP_K] == jnp.arange(N_EXPERTS, dtype=I32)[:, None, None]
    dest = (jnp.sum(jnp.where(is_e, pad_start[:, None, None], 0), axis=0) + meta_i[TOP_K:]).reshape(n_pairs)
    block_start = jnp.arange(n_blocks, dtype=I32) * MOE_BLK
    block_e = jnp.minimum(jnp.sum((pad_end[None, :] <= block_start[:, None]).astype(I32), axis=1), N_EXPERTS - 1)
    n_used = (pad_end[-1] // MOE_BLK).reshape(1).astype(I32)

    fill_start = ((pad_start + counts) // SUBLANES * SUBLANES).astype(I32)
    xb = _dispatch(dest, fill_start, n_used, hp, (n_blocks + 1) * MOE_BLK)
    e_ids = jnp.arange(N_EXPERTS, dtype=I32)
    later = jnp.where((counts[None, :] > 0) & (e_ids[None, :] > e_ids[:, None]), e_ids[None, :], N_EXPERTS)
    next_e = jnp.min(later, axis=1)
    next_e = jnp.where(next_e == N_EXPERTS, -1, next_e).astype(I32)
    yb = _experts(block_e, n_used, next_e, xb, exp_w1, exp_b1, exp_w2, exp_b2)
    out = _combine(dest, yb, gate4, xn, g_f, seq)
    return out.reshape(batch, seq, D_MODEL)


def kernel(x, c, ctx, c_ctx, norm1_g, norm2_g, w_mod, b_mod, w_in, ret_decay_logit, ret_norm_g, diff_q_norm_g, diff_k_norm_g, diff_lambda, diff_norm_g, w_br_ret, w_br_diff, w_out, router_w, router_b, exp_w1, exp_b1, exp_w2, exp_b2):
    assert norm1_g.shape[0] == 1, "single-layer block"
    return _layer(x, ctx, c, c_ctx, norm1_g[0], norm2_g[0], w_mod[0], b_mod[0], w_in[0], ret_decay_logit[0],
                  ret_norm_g[0], diff_q_norm_g[0], diff_k_norm_g[0], diff_lambda[0], diff_norm_g[0],
                  w_br_ret[0], w_br_diff[0], w_out[0], router_w[0], router_b[0],
                  exp_w1[0], exp_b1[0], exp_w2[0], exp_b2[0])
```

```python
import functools
import math

import numpy as np

import jax
import jax.numpy as jnp
from jax import lax
from jax.experimental import pallas as pl
from jax.experimental.pallas import tpu as pltpu

F32 = jnp.float32
BF16 = jnp.bfloat16
U32 = jnp.uint32
I32 = jnp.int32

D_MODEL = 1024
GRID_W = 64
RET_HEADS = 4
RET_DK = 256
RET_DV = 512
DIFF_DH = 64
DIFF_HEADS = 8
DIFF_DV = 128
N_EXPERTS = 32
TOP_K = 4
D_FF = 1024
SWIGLU_LIMIT = 7.0
SWIGLU_ALPHA = 1.702
ROPE_BASE = 10000.0
EPS = 1e-6
LAMBDA_INIT = 0.8 - 0.6 * math.exp(-0.3 * 0)

IN_COLS = 11264
COL_TILE = 1024
CTX_COLS = (1, 2, 3, 7, 8)
LANES = 128
SUBLANES = 8
MOE_BLK = 256
NEG_BIG = -1e30
LOG2E = 1.4426950408889634
SHIFT_SLACK = 1.0 + 2.0 ** -6
MAX_SAFE_SHIFT = 60.0
RET_SUB = 8
KV_UNROLL = 8
HIGHEST = lax.Precision.HIGHEST
MIB = 1024 * 1024


def _params(sem, vmem_mib):
    return pltpu.CompilerParams(dimension_semantics=sem, vmem_limit_bytes=vmem_mib * MIB)


def _mod_kernel(c_ref, w_ref, b_ref, o_ref):
    c = c_ref[...]
    s = c * jax.nn.sigmoid(c)
    o_ref[...] = jnp.dot(s, w_ref[...], preferred_element_type=F32, precision=HIGHEST) + b_ref[...]


def _mod(cc, w_mod, b_mod):
    n = w_mod.shape[1]
    tn = 1024
    return pl.pallas_call(
        _mod_kernel,
        out_shape=jax.ShapeDtypeStruct((8, n), F32),
        grid=(n // tn,),
        in_specs=[pl.BlockSpec((8, D_MODEL), lambda j: (0, 0)),
                  pl.BlockSpec((D_MODEL, tn), lambda j: (0, j)),
                  pl.BlockSpec((1, tn), lambda j: (0, j))],
        out_specs=pl.BlockSpec((8, tn), lambda j: (0, j)),
        compiler_params=_params(("arbitrary",), 32),
        name="mod",
    )(cc, w_mod, b_mod.reshape(1, n))


def _inproj_kernel(cols_ref, x_ref, g_ref, sh_ref, sc_ref, w_ref, qkg_ref, gmat_ref,
                   cr_ref, sr_ref, cc_ref, sc2_ref, cd_ref, sa_ref, sb_ref,
                   o_ref, h_scr, acc_scr):
    j = pl.program_id(1)
    jc = cols_ref[j]

    @pl.when(j == 0)
    def _():
        xf = x_ref[...]
        ms = jnp.mean(xf * xf, axis=-1, keepdims=True)
        y = xf * lax.rsqrt(ms + EPS) * g_ref[...]
        h_scr[...] = (y * (1.0 + sc_ref[0]) + sh_ref[0]).astype(BF16)

    is_ret = jc <= 1
    is_dqk = (jc == 6) | (jc == 7)

    def project():
        return jnp.dot(h_scr[...], w_ref[...], preferred_element_type=F32)

    @pl.when(is_ret)
    def _():
        acc_scr[...] = project()
        scale = jnp.where(jc == 0, RET_DK ** -0.5, 1.0).astype(F32)
        for b in range(COL_TILE // LANES):
            xb = acc_scr[:, b * LANES:(b + 1) * LANES]
            cos = cr_ref[...] if b % 2 == 0 else cc_ref[...]
            sin = sr_ref[...] if b % 2 == 0 else sc2_ref[...]
            o = (xb * cos + pltpu.roll(xb, 64, 1) * sin) * scale
            o_ref[:, b * LANES:(b + 1) * LANES] = o.astype(BF16)

    @pl.when(is_dqk)
    def _():
        acc_scr[...] = project()
        g = jnp.where(jc == 6, qkg_ref[0:1, :], qkg_ref[1:2, :])
        for b2 in range(COL_TILE // (2 * LANES)):
            x2 = acc_scr[:, b2 * 2 * LANES:(b2 + 1) * 2 * LANES]
            ms2 = jnp.dot((x2 * x2).astype(BF16), gmat_ref[...], preferred_element_type=F32)
            for half in range(2):
                b = 2 * b2 + half
                xb = x2[:, half * LANES:(half + 1) * LANES]
                yn = xb * lax.rsqrt(ms2[:, half * LANES:(half + 1) * LANES] + EPS) * g
                o = yn * cd_ref[...] + pltpu.roll(yn, 16, 1) * sa_ref[...] + pltpu.roll(yn, 112, 1) * sb_ref[...]
                o_ref[:, b * LANES:(b + 1) * LANES] = o.astype(BF16)

    @pl.when(jnp.logical_not(is_ret | is_dqk))
    def _():
        o_ref[...] = project().astype(BF16)


def _inproj(x2, g1, sh, sc, w_bf, qkg, gmat, tables, tm, tiles_per_batch, cols):
    n = x2.shape[0]
    nb = sh.shape[0]
    tab_spec = pl.BlockSpec((tm, LANES), lambda i, j, c: (i % tiles_per_batch, 0))
    mod_spec = pl.BlockSpec((1, 1, D_MODEL), lambda i, j, c: (jnp.minimum(i // tiles_per_batch, nb - 1), 0, 0))
    return pl.pallas_call(
        _inproj_kernel,
        out_shape=jax.ShapeDtypeStruct((n, len(cols) * COL_TILE), BF16),
        grid_spec=pltpu.PrefetchScalarGridSpec(
            num_scalar_prefetch=1,
            grid=(n // tm, len(cols)),
            in_specs=[pl.BlockSpec((tm, D_MODEL), lambda i, j, c: (i, 0)),
                      pl.BlockSpec((1, D_MODEL), lambda i, j, c: (0, 0)),
                      mod_spec, mod_spec,
                      pl.BlockSpec((D_MODEL, COL_TILE), lambda i, j, c: (0, c[j])),
                      pl.BlockSpec((8, LANES), lambda i, j, c: (0, 0)),
                      pl.BlockSpec((2 * LANES, 2 * LANES), lambda i, j, c: (0, 0))] + [tab_spec] * 7,
            out_specs=pl.BlockSpec((tm, COL_TILE), lambda i, j, c: (i, j)),
            scratch_shapes=[pltpu.VMEM((tm, D_MODEL), BF16), pltpu.VMEM((tm, COL_TILE), F32)]),
        compiler_params=_params(("arbitrary", "arbitrary"), 48),
        name="inproj",
    )(jnp.asarray(cols, I32), x2, g1, sh, sc, w_bf, qkg, gmat, *tables)


def _rope_tables(seq):
    n_rows = seq // GRID_W
    f32 = np.float32

    def angles(pos, half):
        inv = f32(ROPE_BASE) ** (-np.arange(half, dtype=f32) / f32(half))
        return (pos.astype(f32)[:, None] * inv[None, :]).astype(np.float64)

    ar, ac = angles(np.arange(n_rows), 64), angles(np.arange(GRID_W), 64)
    br, bc = angles(np.arange(n_rows), 16), angles(np.arange(GRID_W), 16)
    zr, zc = np.zeros_like(br), np.zeros_like(bc)
    cat = lambda parts, reps=1: np.tile(np.concatenate(parts, axis=1), (1, reps)).astype(f32)
    by_row = lambda t: jnp.repeat(jnp.asarray(t), GRID_W, axis=0)
    by_col = lambda t: jnp.tile(jnp.asarray(t), (n_rows, 1))
    cr = by_row(cat([np.cos(ar), np.cos(ar)]))
    sr = by_row(cat([-np.sin(ar), np.sin(ar)]))
    cc = by_col(cat([np.cos(ac), np.cos(ac)]))
    sc = by_col(cat([-np.sin(ac), np.sin(ac)]))
    cd = by_row(cat([np.cos(br), np.cos(br), zr, zr], 2)) + by_col(cat([zc, zc, np.cos(bc), np.cos(bc)], 2))
    sa = by_row(cat([zr, np.sin(br), zr, zr], 2)) + by_col(cat([zc, zc, zc, np.sin(bc)], 2))
    sb = by_row(cat([-np.sin(br), zr, zr, zr], 2)) + by_col(cat([zc, zc, -np.sin(bc), zc], 2))
    return [cr, sr, cc, sc, cd, sa, sb]


def _identity_tables(seq):
    one = jnp.ones((seq, LANES), F32)
    zero = jnp.zeros((seq, LANES), F32)
    return [one, zero, one, zero, one, zero, zero]


def _tn_dot(a, b):
    return lax.dot_general(a, b, (((0,), (0,)), ((), ())), preferred_element_type=F32)


def _nt_dot(a, b):
    return lax.dot_general(a, b, (((1,), (1,)), ((), ())), preferred_element_type=F32)


def _ret_kernel(lg_ref, q_ref, k_ref, v_ref, g_ref, ck_ref, cv_ref, gn_ref, o_ref,
                sf_scr, sb_scr, ob_scr, *, chunk, sub, n_blocks, ctx_len):
    h = pl.program_id(1)
    p = pl.program_id(2)
    c = pl.program_id(3)
    lgf = lg_ref[0, h]
    lgb = lg_ref[1, h]

    def col_iota(n):
        return lax.broadcasted_iota(I32, (n, 1), 0).astype(F32)

    def vexp(s):
        return jnp.exp(jnp.zeros((1, 1), F32) + s)

    @pl.when((p == 0) & (c == 0))
    def _():
        jc = col_iota(ctx_len)
        kc = ck_ref[...].astype(F32)
        vc = cv_ref[...]
        sf_scr[...] = _tn_dot((kc * jnp.exp(lgf * (ctx_len - 1.0 - jc))).astype(BF16), vc)
        sb_scr[...] = _tn_dot((kc * jnp.exp(lgb * jc)).astype(BF16), vc)

    ic = col_iota(chunk)

    def full_bf16(col):
        return jnp.broadcast_to(col, (chunk, RET_DK)).astype(BF16)

    @pl.when(p == 0)
    def _():
        q_decay = full_bf16(jnp.exp(lgb * (chunk - ic)))
        k_decay = full_bf16(jnp.exp(lgb * ic))
        s_decay = vexp(lgb * chunk)
        for j in reversed(range(sub)):
            loc = pl.ds(j * chunk, chunk)
            glob = pl.ds(pl.multiple_of(((n_blocks - 1 - c) * sub + j) * chunk, chunk), chunk)
            qb = q_ref[loc, :] * q_decay
            ob_scr[glob, :] = jnp.dot(qb, sb_scr[...].astype(BF16), preferred_element_type=F32)
            kb = k_ref[loc, :] * k_decay
            sb_scr[...] = s_decay * sb_scr[...] + _tn_dot(kb, v_ref[loc, :])

    @pl.when(p == 1)
    def _():
        ri = lax.broadcasted_iota(I32, (chunk, chunk), 0)
        ci = lax.broadcasted_iota(I32, (chunk, chunk), 1)
        d = (ri - ci).astype(F32)
        mask = jnp.where(d > 0, jnp.exp(lgf * jnp.maximum(d, 0.0)),
                         jnp.where(d < 0, jnp.exp(lgb * jnp.maximum(-d, 0.0)), 2.0))
        q_decay = full_bf16(jnp.exp(lgf * (ic + 1.0)))
        k_decay = full_bf16(jnp.exp(lgf * (chunk - 1.0 - ic)))
        s_decay = vexp(lgf * chunk)
        for j in range(sub):
            loc = pl.ds(j * chunk, chunk)
            glob = pl.ds(pl.multiple_of((c * sub + j) * chunk, chunk), chunk)
            a = (_nt_dot(q_ref[loc, :], k_ref[loc, :]) * mask).astype(BF16)
            qf = q_ref[loc, :] * q_decay
            o = (jnp.dot(a, v_ref[loc, :], preferred_element_type=F32)
                 + jnp.dot(qf, sf_scr[...].astype(BF16), preferred_element_type=F32)
                 + ob_scr[glob, :])
            kf = k_ref[loc, :] * k_decay
            sf_scr[...] = s_decay * sf_scr[...] + _tn_dot(kf, v_ref[loc, :])
            ms = jnp.mean(o * o, axis=-1, keepdims=True)
            y = o * lax.rsqrt(ms + EPS) * gn_ref[0]
            gt = g_ref[loc, :]
            o_ref[loc, :] = y.astype(BF16) * (gt * jax.nn.sigmoid(gt))


def _retention(lg, p_lat, p_ctx, ret_norm_g, batch, seq, ctx_len, ctx_k_blk, ctx_v_blk):
    chunk = min(256, seq)
    sub = math.gcd(seq // chunk, RET_SUB)
    blk = chunk * sub
    nb = seq // blk
    kern = functools.partial(_ret_kernel, chunk=chunk, sub=sub, n_blocks=nb, ctx_len=ctx_len)

    def rows(b, p, c):
        return b * nb + jnp.where(p == 0, nb - 1 - c, c)

    def rows_fwd(b, p, c):
        return b * nb + jnp.where(p == 0, 0, c)

    return pl.pallas_call(
        kern,
        out_shape=jax.ShapeDtypeStruct((batch * seq, RET_HEADS * RET_DV), BF16),
        grid=(batch, RET_HEADS, 2, nb),
        in_specs=[pl.BlockSpec(memory_space=pltpu.SMEM),
                  pl.BlockSpec((blk, RET_DK), lambda b, h, p, c: (rows(b, p, c), h)),
                  pl.BlockSpec((blk, RET_DK), lambda b, h, p, c: (rows(b, p, c), 4 + h)),
                  pl.BlockSpec((blk, RET_DV), lambda b, h, p, c: (rows(b, p, c), 4 + h)),
                  pl.BlockSpec((blk, RET_DV), lambda b, h, p, c: (rows_fwd(b, p, c), 8 + h)),
                  pl.BlockSpec((ctx_len, RET_DK), lambda b, h, p, c: (b, ctx_k_blk + h)),
                  pl.BlockSpec((ctx_len, RET_DV), lambda b, h, p, c: (b, ctx_v_blk + h)),
                  pl.BlockSpec((1, 1, RET_DV), lambda b, h, p, c: (h, 0, 0))],
        out_specs=pl.BlockSpec((blk, RET_DV), lambda b, h, p, c: (rows_fwd(b, p, c), h)),
        scratch_shapes=[pltpu.VMEM((RET_DK, RET_DV), F32), pltpu.VMEM((RET_DK, RET_DV), F32),
                        pltpu.VMEM((seq, RET_DV), F32)],
        compiler_params=_params(("arbitrary",) * 4, 48),
        name="ret",
    )(lg, p_lat, p_lat, p_lat, p_lat, p_ctx, p_ctx, ret_norm_g.reshape(RET_HEADS, 1, RET_DV))


def _dattn_kernel(lam_ref, q_ref, ck_ref, cv_ref, k_ref, v_ref, gn_ref, gmat_ref, o_ref,
                  qq_scr, kmax_scr, mp_scr, kp_scr, vt_scr, pt_scr, *, tq, tk, n_kv, ta, unroll, ctx_len):
    rows = 2 * tq
    n_all = kp_scr.shape[0]
    qi = pl.program_id(2)

    def sq_norms(t):
        tf = t.astype(F32)
        return jnp.dot((tf * tf).astype(BF16), gmat_ref[...], preferred_element_type=F32)

    def lane_const(n, hot):
        return jnp.where(lax.broadcasted_iota(I32, (n, LANES), 1) < hot, 1.0, 0.0).astype(BF16)

    @pl.when(qi == 0)
    def _():
        kmax_scr[...] = jnp.max(sq_norms(ck_ref[...]), axis=0, keepdims=True)
        kp_scr[:, LANES:2 * LANES] = lane_const(n_all, 3)
        kp_scr[0:ctx_len, 0:LANES] = ck_ref[...]
        vt_scr[:, 0:ctx_len] = cv_ref[...].astype(F32).T.astype(BF16)

        prep = math.gcd(n_kv, 4)

        def body(c, carry):
            norms = []
            for u in range(prep):
                start = pl.multiple_of((c * prep + u) * tk, tk)
                k = k_ref[pl.ds(start, tk), :]
                norms.append(jnp.max(sq_norms(k), axis=0, keepdims=True))
                kp_scr[pl.ds(pl.multiple_of(ctx_len + start, LANES), tk), 0:LANES] = k
                vt_scr[:, pl.ds(pl.multiple_of(ctx_len + start, LANES), tk)] = (
                    v_ref[pl.ds(start, tk), :].astype(F32).T.astype(BF16))
            kmax_scr[...] = jnp.maximum(kmax_scr[...], functools.reduce(jnp.maximum, norms))
            return carry

        lax.fori_loop(0, n_kv // prep, body, 0)

    qt = q_ref[...].astype(F32).T
    row = lax.broadcasted_iota(I32, (LANES, tq), 0)
    q1t = jnp.where(row < DIFF_DH, qt, 0.0)
    q2t = jnp.where(row >= DIFF_DH, qt, 0.0)
    qq_scr[0:LANES, 0:tq] = q1t.astype(BF16)
    qq_scr[0:LANES, tq:rows] = q2t.astype(BF16)

    def set_shift(shift):
        neg = -shift
        hi = neg.astype(BF16).astype(F32)
        mid = (neg - hi).astype(BF16).astype(F32)
        lo = neg - hi - mid
        row_r = lax.broadcasted_iota(I32, (LANES, rows), 0)
        pieces = jnp.where(row_r == 0, hi, jnp.where(row_r == 1, mid, jnp.where(row_r == 2, lo, 0.0)))
        qq_scr[LANES:2 * LANES, :] = pieces.astype(BF16)

    kmax = kmax_scr[...]
    b1 = jnp.sqrt(jnp.sum(q1t * q1t, axis=0, keepdims=True) * kmax[:, 0:1]) * SHIFT_SLACK
    b2 = jnp.sqrt(jnp.sum(q2t * q2t, axis=0, keepdims=True) * kmax[:, DIFF_DH:DIFF_DH + 1]) * SHIFT_SLACK
    bound = jnp.concatenate([b1, b2], axis=1)
    set_shift(bound)

    def key_tile(t):
        return pl.ds(pl.multiple_of(t * ta, ta), ta)

    @pl.when(jnp.max(bound) > MAX_SAFE_SHIFT)
    def _():
        mp_scr[...] = jnp.full(mp_scr.shape, NEG_BIG, F32)

        def body(t, carry):
            st = jnp.dot(kp_scr[key_tile(t), 0:LANES], qq_scr[0:LANES, :], preferred_element_type=F32)
            mp_scr[...] = jnp.maximum(mp_scr[...], jnp.max(st, axis=0, keepdims=True))
            return carry

        lax.fori_loop(0, n_all // ta, body, 0)
        set_shift(mp_scr[0:1, :])

    mp_scr[...] = jnp.zeros(mp_scr.shape, F32)

    def body_a(t, carry):
        for u in range(unroll):
            r = key_tile(t * unroll + u)
            p = jnp.exp2(jnp.dot(kp_scr[r, :], qq_scr[...], preferred_element_type=F32))
            pt_scr[r, :] = p.astype(BF16)
            mp_scr[...] += jnp.sum(p.reshape(ta // SUBLANES, SUBLANES, rows), axis=0)
        return carry

    lax.fori_loop(0, n_all // (ta * unroll), body_a, 0)

    acc = jnp.dot(vt_scr[...], pt_scr[...], preferred_element_type=F32)
    ot = acc / jnp.sum(mp_scr[...], axis=0, keepdims=True)
    d = (ot[:, 0:tq] - lam_ref[0] * ot[:, tq:rows]).T
    ms = jnp.mean(d * d, axis=-1, keepdims=True)
    y = d * lax.rsqrt(ms + EPS) * gn_ref[...] * (1.0 - LAMBDA_INIT)
    o_ref[...] = y.astype(BF16)


def _diff_attention(lam, p_lat, p_ctx, diff_norm_g, gmat, batch, seq, ctx_len, ctx_k_blk, ctx_v_blk):
    tq = min(1024, seq)
    tk = min(512, seq)
    nq, nk = seq // tq, seq // tk
    n_all = ctx_len + seq
    ta = next(t for t in (528, 512, 384, 320, 256, 128) if n_all % t == 0)
    unroll = math.gcd(n_all // ta, KV_UNROLL)
    assert ctx_len % LANES == 0
    kern = functools.partial(_dattn_kernel, tq=tq, tk=tk, n_kv=nk, ta=ta, unroll=unroll, ctx_len=ctx_len)
    return pl.pallas_call(
        kern,
        out_shape=jax.ShapeDtypeStruct((batch * seq, DIFF_HEADS * DIFF_DV), BF16),
        grid=(batch, DIFF_HEADS, nq),
        in_specs=[pl.BlockSpec(memory_space=pltpu.SMEM),
                  pl.BlockSpec((tq, LANES), lambda b, h, qi: (b * nq + qi, 48 + h)),
                  pl.BlockSpec((ctx_len, LANES), lambda b, h, qi: (b, ctx_k_blk + h)),
                  pl.BlockSpec((ctx_len, LANES), lambda b, h, qi: (b, ctx_v_blk + h)),
                  pl.BlockSpec((seq, LANES), lambda b, h, qi: (b, 56 + h)),
                  pl.BlockSpec((seq, LANES), lambda b, h, qi: (b, 64 + h)),
                  pl.BlockSpec((1, LANES), lambda b, h, qi: (0, 0)),
                  pl.BlockSpec((LANES, LANES), lambda b, h, qi: (0, 0))],
        out_specs=pl.BlockSpec((tq, LANES), lambda b, h, qi: (b * nq + qi, h)),
        scratch_shapes=[pltpu.VMEM((2 * LANES, 2 * tq), BF16), pltpu.VMEM((1, LANES), F32),
                        pltpu.VMEM((SUBLANES, 2 * tq), F32), pltpu.VMEM((n_all, 2 * LANES), BF16),
                        pltpu.VMEM((DIFF_DV, n_all), BF16), pltpu.VMEM((n_all, 2 * tq), BF16)],
        compiler_params=_params(("arbitrary",) * 3, 58),
        name="dattn",
    )(lam, p_lat, p_ctx, p_ctx, p_lat, p_lat, diff_norm_g.reshape(1, DIFF_DV), gmat)


def _merge_kernel(yr_ref, yd_ref, ga_ref, gb_ref, x_ref, gatea_ref, shf_ref, scf_ref, n2_ref,
                  wr_ref, wd_ref, wo_ref, rwh_ref, rwl_ref, rb_ref,
                  xn_ref, hp_ref, meta_ref, gate_ref, cnt_ref, run_scr, *, tm):
    i = pl.program_id(0)

    @pl.when(i == 0)
    def _():
        run_scr[...] = jnp.zeros(run_scr.shape, F32)

    yr = jnp.dot(yr_ref[...], wr_ref[...], preferred_element_type=F32)
    yd = jnp.dot(yd_ref[...], wd_ref[...], preferred_element_type=F32)
    m = (jax.nn.sigmoid(ga_ref[...].astype(F32)) * yr + jax.nn.sigmoid(gb_ref[...].astype(F32)) * yd)
    z = jnp.dot(m.astype(BF16), wo_ref[...], preferred_element_type=F32)
    xn = x_ref[...] + gatea_ref[0] * z
    xn_ref[...] = xn

    ms = jnp.mean(xn * xn, axis=-1, keepdims=True)
    h2 = xn * lax.rsqrt(ms + EPS) * n2_ref[...]
    h2 = h2 * (1.0 + scf_ref[0]) + shf_ref[0]
    h_hi = h2.astype(BF16)
    bits = pltpu.bitcast(h_hi.astype(F32), U32)
    half = D_MODEL // 2
    hp_ref[...] = (bits[:, :half] >> 16) | (bits[:, half:] & jnp.uint32(0xFFFF0000))

    h_lo = (h2 - h_hi.astype(F32)).astype(BF16)
    logits = (jnp.dot(h_hi, rwh_ref[...], preferred_element_type=F32)
              + jnp.dot(h_lo, rwh_ref[...], preferred_element_type=F32)
              + jnp.dot(h_hi, rwl_ref[...], preferred_element_type=F32) + rb_ref[...])
    lane = lax.broadcasted_iota(I32, (tm, LANES), 1)
    lanef = lane.astype(F32)
    work = logits
    vals, idxs = [], []
    for _ in range(TOP_K):
        mk = jnp.max(work, axis=1, keepdims=True)
        ik = jnp.min(jnp.where(work == mk, lanef, float(LANES)), axis=1, keepdims=True)
        vals.append(mk)
        idxs.append(ik)
        work = jnp.where(lanef == ik, -jnp.inf, work)
    ex = [jnp.exp(v - vals[0]) for v in vals]
    den = ex[0] + ex[1] + ex[2] + ex[3]

    onehot = jnp.zeros((tm, LANES), F32)
    for ik in idxs:
        onehot = onehot + jnp.where(lanef == ik, 1.0, 0.0)
    ri = lax.broadcasted_iota(I32, (tm, tm), 0)
    ci = lax.broadcasted_iota(I32, (tm, tm), 1)
    tri = jnp.where(ri > ci, 1.0, 0.0).astype(BF16)
    base = run_scr[0:1, :] + jnp.dot(tri, onehot.astype(BF16), preferred_element_type=F32)
    run_scr[...] = run_scr[...] + jnp.sum(onehot, axis=0, keepdims=True)

    meta = jnp.zeros((tm, LANES), F32)
    gate_out = jnp.zeros((tm, LANES), F32)
    for k in range(TOP_K):
        rk = jnp.sum(jnp.where(lanef == idxs[k], base, 0.0), axis=1, keepdims=True)
        meta = jnp.where(lane == k, idxs[k], jnp.where(lane == TOP_K + k, rk, meta))
        gate_out = jnp.where(lane == k, ex[k] / den, gate_out)
    meta_ref[...] = meta.T[0:2 * TOP_K, :]
    gate_ref[...] = gate_out
    cnt_ref[...] = run_scr[...]


def _merge(y_ret, y_diff, p_lat, x2, g_a, sh_f, sc_f, norm2_g, w_r, w_d, w_o, rw_hi, rw_lo, rb, seq):
    n = x2.shape[0]
    tm = min(512, seq)
    tpb = seq // tm
    kern = functools.partial(_merge_kernel, tm=tm)
    mod_spec = pl.BlockSpec((1, 1, D_MODEL), lambda i: (i // tpb, 0, 0))
    const = lambda shape: pl.BlockSpec(shape, lambda i: (0,) * len(shape))
    tok = lambda w: pl.BlockSpec((tm, w), lambda i: (i, 0))
    return pl.pallas_call(
        kern,
        out_shape=(jax.ShapeDtypeStruct((n, D_MODEL), F32),
                   jax.ShapeDtypeStruct((n, D_MODEL // 2), U32),
                   jax.ShapeDtypeStruct((2 * TOP_K, n), F32),
                   jax.ShapeDtypeStruct((n, LANES), F32),
                   jax.ShapeDtypeStruct((8, LANES), F32)),
        grid=(n // tm,),
        in_specs=[tok(RET_HEADS * RET_DV), tok(D_MODEL),
                  pl.BlockSpec((tm, COL_TILE), lambda i: (i, 9)),
                  pl.BlockSpec((tm, COL_TILE), lambda i: (i, 10)),
                  tok(D_MODEL), mod_spec, mod_spec, mod_spec, const((1, D_MODEL)),
                  const((RET_HEADS * RET_DV, D_MODEL)), const((D_MODEL, D_MODEL)), const((D_MODEL, D_MODEL)),
                  const((D_MODEL, LANES)), const((D_MODEL, LANES)), const((1, LANES))],
        out_specs=(tok(D_MODEL), tok(D_MODEL // 2), pl.BlockSpec((2 * TOP_K, tm), lambda i: (0, i)),
                   tok(LANES), const((8, LANES))),
        scratch_shapes=[pltpu.VMEM((8, LANES), F32)],
        compiler_params=_params(("arbitrary",), 56),
        name="merge",
    )(y_ret, y_diff, p_lat, p_lat, x2, g_a, sh_f, sc_f, norm2_g, w_r, w_d, w_o, rw_hi, rw_lo, rb)


def _dispatch_kernel(dest_ref, fill_ref, nb_ref, hp_ref, xb_ref, zero_scr, stage, sems, zsem, *,
                     tm, n_tok, n_blocks, n_steps):
    i = pl.program_id(0)

    @pl.when(i == 0)
    def _():
        zero_scr[...] = jnp.zeros(zero_scr.shape, U32)
        fill = zero_scr.shape[0]
        for e in range(N_EXPERTS):
            start = pl.multiple_of(fill_ref[e], SUBLANES)
            pltpu.make_async_copy(zero_scr, xb_ref.at[pl.ds(start, fill), :], zsem).start()
        for e in range(N_EXPERTS):
            pltpu.make_async_copy(zero_scr, xb_ref.at[pl.ds(0, fill), :], zsem).wait()
        blk = zero_scr.at[pl.ds(0, MOE_BLK), :]

        def start_blk(b, carry):
            pltpu.make_async_copy(blk, xb_ref.at[pl.ds(pl.multiple_of(b * MOE_BLK, MOE_BLK), MOE_BLK), :], zsem).start()
            return carry

        def wait_blk(b, carry):
            pltpu.make_async_copy(blk, xb_ref.at[pl.ds(0, MOE_BLK), :], zsem).wait()
            return carry

        lax.fori_loop(nb_ref[0], n_blocks, start_blk, 0)
        lax.fori_loop(nb_ref[0], n_blocks, wait_blk, 0)

    def drain(slot):
        for _ in range(TOP_K):
            pltpu.make_async_copy(stage.at[slot], xb_ref.at[pl.ds(0, tm), :], sems.at[slot]).wait()

    for slot in range(2):
        @pl.when(lax.rem(i, 2) == slot)
        def _():
            stage[slot] = hp_ref[...]

            def body(g, carry):
                r0 = pl.multiple_of(g * SUBLANES, SUBLANES)
                for j in range(SUBLANES):
                    for k in range(TOP_K):
                        d = dest_ref[k * n_tok + i * tm + r0 + j]
                        pltpu.make_async_copy(stage.at[slot, pl.ds(r0 + j, 1), :], xb_ref.at[pl.ds(d, 1), :],
                                              sems.at[slot]).start(priority=k % 2)
                return carry

            lax.fori_loop(0, tm // SUBLANES, body, 0)

            @pl.when(i > 0)
            def _():
                drain(1 - slot)

            @pl.when(i == n_steps - 1)
            def _():
                drain(slot)


def _dispatch(dest, fill_start, n_used, hp, n_rows):
    n = hp.shape[0]
    tm = min(512, n)
    kern = functools.partial(_dispatch_kernel, tm=tm, n_tok=n, n_blocks=n_rows // MOE_BLK, n_steps=n // tm)
    return pl.pallas_call(
        kern,
        out_shape=jax.ShapeDtypeStruct((n_rows, D_MODEL // 2), U32),
        grid_spec=pltpu.PrefetchScalarGridSpec(
            num_scalar_prefetch=3,
            grid=(n // tm,),
            in_specs=[pl.BlockSpec((tm, D_MODEL // 2), lambda i, d, f, nb: (i, 0))],
            out_specs=pl.BlockSpec(memory_space=pl.ANY),
            scratch_shapes=[pltpu.VMEM((MOE_BLK + SUBLANES, D_MODEL // 2), U32),
                            pltpu.VMEM((2, tm, D_MODEL // 2), U32),
                            pltpu.SemaphoreType.DMA((2,)), pltpu.SemaphoreType.DMA(())]),
        compiler_params=_params(("arbitrary",), 32),
        name="dispatch",
    )(dest, fill_start, n_used, hp)


def _expert_kernel(be_ref, nb_ref, nxt_ref, xb_ref, w1_hbm, b1_ref, w2_hbm, b2_ref, y_ref,
                   w1f_scr, w2f_scr, w1b_scr, w2b_scr, x_scr, grp_scr, sems):
    i = pl.program_id(0)
    e = be_ref[i]
    prev = be_ref[jnp.maximum(i - 1, 0)]
    active = i < nb_ref[0]

    def weight_copies(expert, slot):
        return (pltpu.make_async_copy(w1_hbm.at[expert], w1f_scr.at[slot], sems.at[0, slot]),
                pltpu.make_async_copy(w2_hbm.at[expert], w2f_scr.at[slot], sems.at[1, slot]))

    @pl.when(i == 0)
    def _():
        grp_scr[0] = 0
        for cp in weight_copies(e, 0):
            cp.start()

    @pl.when(active & (i > 0) & (e != prev))
    def _():
        grp_scr[0] = grp_scr[0] + 1

    @pl.when(active & ((i == 0) | (e != prev)))
    def _():
        slot = grp_scr[0] % 2
        for cp in weight_copies(e, slot):
            cp.wait()
        w1b_scr[...] = w1f_scr[slot].astype(BF16)
        w2b_scr[...] = w2f_scr[slot].astype(BF16)
        nxt = nxt_ref[e]

        @pl.when(nxt >= 0)
        def _():
            for cp in weight_copies(nxt, 1 - slot):
                cp.start()

    @pl.when(active)
    def _():
        xu = xb_ref[...]
        half = D_MODEL // 2
        x_scr[:, 0:half] = pltpu.bitcast(xu << 16, F32).astype(BF16)
        x_scr[:, half:] = pltpu.bitcast(xu & jnp.uint32(0xFFFF0000), F32).astype(BF16)
        hh = jnp.dot(x_scr[...], w1b_scr[...], preferred_element_type=F32) + b1_ref[0]
        glu = jnp.minimum(hh[:, :D_FF], SWIGLU_LIMIT)
        lin = jnp.clip(hh[:, D_FF:], -SWIGLU_LIMIT, SWIGLU_LIMIT)
        act = glu * jax.nn.sigmoid(SWIGLU_ALPHA * glu) * (lin + 1.0)
        y_ref[...] = jnp.dot(act.astype(BF16), w2b_scr[...], preferred_element_type=F32) + b2_ref[0]

    @pl.when(i >= nb_ref[0])
    def _():
        y_ref[...] = jnp.zeros(y_ref.shape, F32)


def _experts(block_e, n_used, next_e, xb, w1, b1, w2, b2):
    n_blocks = block_e.shape[0]
    rows = n_blocks * MOE_BLK
    return pl.pallas_call(
        _expert_kernel,
        out_shape=jax.ShapeDtypeStruct((rows, D_MODEL), F32),
        grid_spec=pltpu.PrefetchScalarGridSpec(
            num_scalar_prefetch=3,
            grid=(n_blocks,),
            in_specs=[pl.BlockSpec((MOE_BLK, D_MODEL // 2), lambda i, be, nb, nx: (jnp.minimum(i, nb[0] - 1), 0)),
                      pl.BlockSpec(memory_space=pl.ANY),
                      pl.BlockSpec((1, 1, 2 * D_FF), lambda i, be, nb, nx: (be[i], 0, 0)),
                      pl.BlockSpec(memory_space=pl.ANY),
                      pl.BlockSpec((1, 1, D_MODEL), lambda i, be, nb, nx: (be[i], 0, 0))],
            out_specs=pl.BlockSpec((MOE_BLK, D_MODEL), lambda i, be, nb, nx: (i, 0)),
            scratch_shapes=[pltpu.VMEM((2, D_MODEL, 2 * D_FF), F32), pltpu.VMEM((2, D_FF, D_MODEL), F32),
                            pltpu.VMEM((D_MODEL, 2 * D_FF), BF16), pltpu.VMEM((D_FF, D_MODEL), BF16),
                            pltpu.VMEM((MOE_BLK, D_MODEL), BF16),
                            pltpu.SMEM((1,), I32), pltpu.SemaphoreType.DMA((2, 2))]),
        compiler_params=_params(("arbitrary",), 56),
        name="expert",
    )(block_e, n_used, next_e, xb, w1, b1.reshape(N_EXPERTS, 1, 2 * D_FF), w2, b2.reshape(N_EXPERTS, 1, D_MODEL))


def _combine_kernel(dest_ref, yb_ref, gate_ref, xn_ref, gf_ref, o_ref, buf, sems, *, tm, n_tok, n_steps):
    i = pl.program_id(0)

    def issue(tile, slot):
        def body(g, carry):
            r0 = pl.multiple_of(g * SUBLANES, SUBLANES)
            for j in range(SUBLANES):
                for k in range(TOP_K):
                    d = dest_ref[k * n_tok + tile * tm + r0 + j]
                    pltpu.make_async_copy(yb_ref.at[pl.ds(d, 1), :], buf.at[slot, k, pl.ds(r0 + j, 1), :],
                                          sems.at[slot]).start(priority=k % 2)
            return carry

        lax.fori_loop(0, tm // SUBLANES, body, 0)

    def finish(slot):
        for k in range(TOP_K):
            pltpu.make_async_copy(yb_ref.at[pl.ds(0, tm), :], buf.at[slot, k], sems.at[slot]).wait()
        g = gate_ref[...]
        y = g[:, 0:1] * buf[slot, 0]
        for k in range(1, TOP_K):
            y = y + g[:, k:k + 1] * buf[slot, k]
        o_ref[...] = xn_ref[...] + gf_ref[0] * y

    @pl.when(i == 0)
    def _():
        issue(0, 0)

    for slot in range(2):
        @pl.when(lax.rem(i, 2) == slot)
        def _():
            @pl.when(i + 1 < n_steps)
            def _():
                issue(i + 1, 1 - slot)

            finish(slot)


def _combine(dest, yb, gate, xn, g_f, seq):
    n = xn.shape[0]
    tm = min(512, seq)
    tpb = seq // tm
    kern = functools.partial(_combine_kernel, tm=tm, n_tok=n, n_steps=n // tm)
    return pl.pallas_call(
        kern,
        out_shape=jax.ShapeDtypeStruct((n, D_MODEL), F32),
        grid_spec=pltpu.PrefetchScalarGridSpec(
            num_scalar_prefetch=1,
            grid=(n // tm,),
            in_specs=[pl.BlockSpec(memory_space=pl.ANY),
                      pl.BlockSpec((tm, LANES), lambda i, d: (i, 0)),
                      pl.BlockSpec((tm, D_MODEL), lambda i, d: (i, 0)),
                      pl.BlockSpec((1, 1, D_MODEL), lambda i, d: (i // tpb, 0, 0))],
            out_specs=pl.BlockSpec((tm, D_MODEL), lambda i, d: (i, 0)),
            scratch_shapes=[pltpu.VMEM((2, TOP_K, tm, D_MODEL), F32), pltpu.SemaphoreType.DMA((2,))]),
        compiler_params=_params(("arbitrary",), 40),
        name="combine",
    )(dest, yb, gate, xn, g_f)


def _layer(x, ctx, c, c_ctx, norm1_g, norm2_g, w_mod, b_mod, w_in, ret_decay_logit, ret_norm_g,
           diff_q_norm_g, diff_k_norm_g, diff_lambda, diff_norm_g, w_br_ret, w_br_diff, w_out,
           router_w, router_b, exp_w1, exp_b1, exp_w2, exp_b2):
    batch, seq, d = x.shape
    ctx_len = ctx.shape[1]
    assert d == D_MODEL and seq % GRID_W == 0 and batch + 1 <= 8
    n_tok = batch * seq

    cc = jnp.zeros((8, D_MODEL), F32).at[:batch].set(c).at[batch].set(c_ctx)
    mod = _mod(cc, w_mod, b_mod)
    sh_a, sc_a, g_a, sh_f, sc_f, g_f = [mod[:batch, i * D_MODEL:(i + 1) * D_MODEL].reshape(batch, 1, D_MODEL)
                                         for i in range(6)]
    csh_a = mod[batch:batch + 1, 0:D_MODEL].reshape(1, 1, D_MODEL)
    csc_a = mod[batch:batch + 1, D_MODEL:2 * D_MODEL].reshape(1, 1, D_MODEL)

    w_in_bf = w_in.astype(BF16)
    g1 = norm1_g.reshape(1, D_MODEL)
    tile = lambda g: jnp.tile(g.astype(F32), 2)
    qkg = jnp.zeros((8, LANES), F32).at[0].set(tile(diff_q_norm_g) * (DIFF_DH ** -0.5 * LOG2E)).at[1].set(tile(diff_k_norm_g))
    lane = jnp.arange(2 * LANES)
    same_group = lane[:, None] // DIFF_DH == lane[None, :] // DIFF_DH
    gmean = jnp.where(same_group, 1.0 / DIFF_DH, 0.0).astype(BF16)
    gmat = same_group[:LANES, :LANES].astype(BF16)
    x2 = x.reshape(n_tok, D_MODEL)
    tm = min(1024, seq)
    p_lat = _inproj(x2, g1, sh_a, sc_a, w_in_bf, qkg, gmean, _rope_tables(seq), tm, seq // tm,
                    tuple(range(IN_COLS // COL_TILE)))
    p_ctx = _inproj(ctx.reshape(batch * ctx_len, D_MODEL), g1, csh_a, csc_a, w_in_bf, qkg, gmean,
                    _identity_tables(ctx_len), ctx_len, 1, CTX_COLS)

    lg = jax.nn.log_sigmoid(ret_decay_logit.astype(F32))
    y_ret = _retention(lg, p_lat, p_ctx, ret_norm_g, batch, seq, ctx_len,
                       CTX_COLS.index(1) * COL_TILE // RET_DK, CTX_COLS.index(2) * COL_TILE // RET_DV)

    lp = diff_lambda.astype(F32)
    lam = (jnp.exp(jnp.sum(lp[0] * lp[1])) - jnp.exp(jnp.sum(lp[2] * lp[3])) + LAMBDA_INIT).reshape(1)
    y_diff = _diff_attention(lam, p_lat, p_ctx, diff_norm_g, gmat, batch, seq, ctx_len,
                             CTX_COLS.index(7) * COL_TILE // LANES, CTX_COLS.index(8) * COL_TILE // LANES)

    rw = jnp.zeros((D_MODEL, LANES), F32).at[:, :N_EXPERTS].set(router_w)
    rw_hi = rw.astype(BF16)
    rw_lo = (rw - rw_hi.astype(F32)).astype(BF16)
    rb = jnp.full((1, LANES), NEG_BIG, F32).at[0, :N_EXPERTS].set(router_b)
    xn, hp, meta, gate4, cnt = _merge(
        y_ret, y_diff, p_lat, x2, g_a, sh_f, sc_f, norm2_g.reshape(1, D_MODEL),
        w_br_ret.astype(BF16), w_br_diff.astype(BF16), w_out.astype(BF16), rw_hi, rw_lo, rb, seq)

    counts = cnt[0, :N_EXPERTS].astype(I32)
    padded = (counts + MOE_BLK - 1) // MOE_BLK * MOE_BLK
    pad_end = jnp.cumsum(padded)
    pad_start = pad_end - padded
    n_pairs = n_tok * TOP_K
    n_blocks = n_pairs // MOE_BLK + N_EXPERTS
    meta_i = meta.astype(I32)
    is_e = meta_i[None, :TOP_K] == jnp.arange(N_EXPERTS, dtype=I32)[:, None, None]
    dest = (jnp.sum(jnp.where(is_e, pad_start[:, None, None], 0), axis=0) + meta_i[TOP_K:]).reshape(n_pairs)
    block_start = jnp.arange(n_blocks, dtype=I32) * MOE_BLK
    block_e = jnp.minimum(jnp.sum((pad_end[None, :] <= block_start[:, None]).astype(I32), axis=1), N_EXPERTS - 1)
    n_used = (pad_end[-1] // MOE_BLK).reshape(1).astype(I32)

    fill_start = ((pad_start + counts) // SUBLANES * SUBLANES).astype(I32)
    xb = _dispatch(dest, fill_start, n_used, hp, (n_blocks + 1) * MOE_BLK)
    e_ids = jnp.arange(N_EXPERTS, dtype=I32)
    later = jnp.where((counts[None, :] > 0) & (e_ids[None, :] > e_ids[:, None]), e_ids[None, :], N_EXPERTS)
    next_e = jnp.min(later, axis=1)
    next_e = jnp.where(next_e == N_EXPERTS, -1, next_e).astype(I32)
    yb = _experts(block_e, n_used, next_e, xb, exp_w1, exp_b1, exp_w2, exp_b2)
    out = _combine(dest, yb, gate4, xn, g_f, seq)
    return out.reshape(batch, seq, D_MODEL)


def kernel(x, c, ctx, c_ctx, norm1_g, norm2_g, w_mod, b_mod, w_in, ret_decay_logit, ret_norm_g, diff_q_norm_g, diff_k_norm_g, diff_lambda, diff_norm_g, w_br_ret, w_br_diff, w_out, router_w, router_b, exp_w1, exp_b1, exp_w2, exp_b2):
    assert norm1_g.shape[0] == 1, "single-layer block"
    return _layer(x, ctx, c, c_ctx, norm1_g[0], norm2_g[0], w_mod[0], b_mod[0], w_in[0], ret_decay_logit[0],
                  ret_norm_g[0], diff_q_norm_g[0], diff_k_norm_g[0], diff_lambda[0], diff_norm_g[0],
                  w_br_ret[0], w_br_diff[0], w_out[0], router_w[0], router_b[0],
                  exp_w1[0], exp_b1[0], exp_w2[0], exp_b2[0])
```

```python
import functools
import math

import numpy as np

import jax
import jax.numpy as jnp
from jax import lax
from jax.experimental import pallas as pl
from jax.experimental.pallas import tpu as pltpu

F32 = jnp.float32
BF16 = jnp.bfloat16
U32 = jnp.uint32
I32 = jnp.int32

D_MODEL = 1024
GRID_W = 64
RET_HEADS = 4
RET_DK = 256
RET_DV = 512
DIFF_DH = 64
DIFF_HEADS = 8
DIFF_DV = 128
N_EXPERTS = 32
TOP_K = 4
D_FF = 1024
SWIGLU_LIMIT = 7.0
SWIGLU_ALPHA = 1.702
ROPE_BASE = 10000.0
EPS = 1e-6
LAMBDA_INIT = 0.8 - 0.6 * math.exp(-0.3 * 0)

IN_COLS = 11264
COL_TILE = 1024
CTX_COLS = (1, 2, 3, 7, 8)
LANES = 128
SUBLANES = 8
MOE_BLK = 256
NEG_BIG = -1e30
LOG2E = 1.4426950408889634
SHIFT_SLACK = 1.0 + 2.0 ** -6
MAX_SAFE_SHIFT = 60.0
RET_SUB = 8
KV_UNROLL = 8
HIGHEST = lax.Precision.HIGHEST
MIB = 1024 * 1024


def _params(sem, vmem_mib):
    return pltpu.CompilerParams(dimension_semantics=sem, vmem_limit_bytes=vmem_mib * MIB)


def _mod_kernel(c_ref, w_ref, b_ref, o_ref):
    c = c_ref[...]
    s = c * jax.nn.sigmoid(c)
    o_ref[...] = jnp.dot(s, w_ref[...], preferred_element_type=F32, precision=HIGHEST) + b_ref[...]


def _mod(cc, w_mod, b_mod):
    n = w_mod.shape[1]
    tn = 1024
    return pl.pallas_call(
        _mod_kernel,
        out_shape=jax.ShapeDtypeStruct((8, n), F32),
        grid=(n // tn,),
        in_specs=[pl.BlockSpec((8, D_MODEL), lambda j: (0, 0)),
                  pl.BlockSpec((D_MODEL, tn), lambda j: (0, j)),
                  pl.BlockSpec((1, tn), lambda j: (0, j))],
        out_specs=pl.BlockSpec((8, tn), lambda j: (0, j)),
        compiler_params=_params(("arbitrary",), 32),
        name="mod",
    )(cc, w_mod, b_mod.reshape(1, n))


def _inproj_kernel(cols_ref, x_ref, g_ref, sh_ref, sc_ref, w_ref, qkg_ref, gmat_ref,
                   cr_ref, sr_ref, cc_ref, sc2_ref, cd_ref, sa_ref, sb_ref,
                   o_ref, h_scr, acc_scr):
    j = pl.program_id(1)
    jc = cols_ref[j]

    @pl.when(j == 0)
    def _():
        xf = x_ref[...]
        ms = jnp.mean(xf * xf, axis=-1, keepdims=True)
        y = xf * lax.rsqrt(ms + EPS) * g_ref[...]
        h_scr[...] = (y * (1.0 + sc_ref[0]) + sh_ref[0]).astype(BF16)

    is_ret = jc <= 1
    is_dqk = (jc == 6) | (jc == 7)

    def project():
        return jnp.dot(h_scr[...], w_ref[...], preferred_element_type=F32)

    @pl.when(is_ret)
    def _():
        acc_scr[...] = project()
        scale = jnp.where(jc == 0, RET_DK ** -0.5, 1.0).astype(F32)
        for b in range(COL_TILE // LANES):
            xb = acc_scr[:, b * LANES:(b + 1) * LANES]
            cos = cr_ref[...] if b % 2 == 0 else cc_ref[...]
            sin = sr_ref[...] if b % 2 == 0 else sc2_ref[...]
            o = (xb * cos + pltpu.roll(xb, 64, 1) * sin) * scale
            o_ref[:, b * LANES:(b + 1) * LANES] = o.astype(BF16)

    @pl.when(is_dqk)
    def _():
        acc_scr[...] = project()
        g = jnp.where(jc == 6, qkg_ref[0:1, :], qkg_ref[1:2, :])
        for b2 in range(COL_TILE // (2 * LANES)):
            x2 = acc_scr[:, b2 * 2 * LANES:(b2 + 1) * 2 * LANES]
            ms2 = jnp.dot((x2 * x2).astype(BF16), gmat_ref[...], preferred_element_type=F32)
            for half in range(2):
                b = 2 * b2 + half
                xb = x2[:, half * LANES:(half + 1) * LANES]
                yn = xb * lax.rsqrt(ms2[:, half * LANES:(half + 1) * LANES] + EPS) * g
                o = yn * cd_ref[...] + pltpu.roll(yn, 16, 1) * sa_ref[...] + pltpu.roll(yn, 112, 1) * sb_ref[...]
                o_ref[:, b * LANES:(b + 1) * LANES] = o.astype(BF16)

    @pl.when(jnp.logical_not(is_ret | is_dqk))
    def _():
        o_ref[...] = project().astype(BF16)


def _inproj(x2, g1, sh, sc, w_bf, qkg, gmat, tables, tm, tiles_per_batch, cols):
    n = x2.shape[0]
    nb = sh.shape[0]
    tab_spec = pl.BlockSpec((tm, LANES), lambda i, j, c: (i % tiles_per_batch, 0))
    mod_spec = pl.BlockSpec((1, 1, D_MODEL), lambda i, j, c: (jnp.minimum(i // tiles_per_batch, nb - 1), 0, 0))
    return pl.pallas_call(
        _inproj_kernel,
        out_shape=jax.ShapeDtypeStruct((n, len(cols) * COL_TILE), BF16),
        grid_spec=pltpu.PrefetchScalarGridSpec(
            num_scalar_prefetch=1,
            grid=(n // tm, len(cols)),
            in_specs=[pl.BlockSpec((tm, D_MODEL), lambda i, j, c: (i, 0)),
                      pl.BlockSpec((1, D_MODEL), lambda i, j, c: (0, 0)),
                      mod_spec, mod_spec,
                      pl.BlockSpec((D_MODEL, COL_TILE), lambda i, j, c: (0, c[j])),
                      pl.BlockSpec((8, LANES), lambda i, j, c: (0, 0)),
                      pl.BlockSpec((2 * LANES, 2 * LANES), lambda i, j, c: (0, 0))] + [tab_spec] * 7,
            out_specs=pl.BlockSpec((tm, COL_TILE), lambda i, j, c: (i, j)),
            scratch_shapes=[pltpu.VMEM((tm, D_MODEL), BF16), pltpu.VMEM((tm, COL_TILE), F32)]),
        compiler_params=_params(("arbitrary", "arbitrary"), 48),
        name="inproj",
    )(jnp.asarray(cols, I32), x2, g1, sh, sc, w_bf, qkg, gmat, *tables)


def _rope_tables(seq):
    n_rows = seq // GRID_W
    f32 = np.float32

    def angles(pos, half):
        inv = f32(ROPE_BASE) ** (-np.arange(half, dtype=f32) / f32(half))
        return (pos.astype(f32)[:, None] * inv[None, :]).astype(np.float64)

    ar, ac = angles(np.arange(n_rows), 64), angles(np.arange(GRID_W), 64)
    br, bc = angles(np.arange(n_rows), 16), angles(np.arange(GRID_W), 16)
    zr, zc = np.zeros_like(br), np.zeros_like(bc)
    cat = lambda parts, reps=1: np.tile(np.concatenate(parts, axis=1), (1, reps)).astype(f32)
    by_row = lambda t: jnp.repeat(jnp.asarray(t), GRID_W, axis=0)
    by_col = lambda t: jnp.tile(jnp.asarray(t), (n_rows, 1))
    cr = by_row(cat([np.cos(ar), np.cos(ar)]))
    sr = by_row(cat([-np.sin(ar), np.sin(ar)]))
    cc = by_col(cat([np.cos(ac), np.cos(ac)]))
    sc = by_col(cat([-np.sin(ac), np.sin(ac)]))
    cd = by_row(cat([np.cos(br), np.cos(br), zr, zr], 2)) + by_col(cat([zc, zc, np.cos(bc), np.cos(bc)], 2))
    sa = by_row(cat([zr, np.sin(br), zr, zr], 2)) + by_col(cat([zc, zc, zc, np.sin(bc)], 2))
    sb = by_row(cat([-np.sin(br), zr, zr, zr], 2)) + by_col(cat([zc, zc, -np.sin(bc), zc], 2))
    return [cr, sr, cc, sc, cd, sa, sb]


def _identity_tables(seq):
    one = jnp.ones((seq, LANES), F32)
    zero = jnp.zeros((seq, LANES), F32)
    return [one, zero, one, zero, one, zero, zero]


def _tn_dot(a, b):
    return lax.dot_general(a, b, (((0,), (0,)), ((), ())), preferred_element_type=F32)


def _nt_dot(a, b):
    return lax.dot_general(a, b, (((1,), (1,)), ((), ())), preferred_element_type=F32)


def _ret_kernel(lg_ref, q_ref, k_ref, v_ref, g_ref, ck_ref, cv_ref, gn_ref, o_ref,
                sf_scr, sb_scr, ob_scr, *, chunk, sub, n_blocks, ctx_len):
    h = pl.program_id(1)
    p = pl.program_id(2)
    c = pl.program_id(3)
    lgf = lg_ref[0, h]
    lgb = lg_ref[1, h]

    def col_iota(n):
        return lax.broadcasted_iota(I32, (n, 1), 0).astype(F32)

    def vexp(s):
        return jnp.exp(jnp.zeros((1, 1), F32) + s)

    @pl.when((p == 0) & (c == 0))
    def _():
        jc = col_iota(ctx_len)
        kc = ck_ref[...].astype(F32)
        vc = cv_ref[...]
        sf_scr[...] = _tn_dot((kc * jnp.exp(lgf * (ctx_len - 1.0 - jc))).astype(BF16), vc)
        sb_scr[...] = _tn_dot((kc * jnp.exp(lgb * jc)).astype(BF16), vc)

    ic = col_iota(chunk)

    def full_bf16(col):
        return jnp.broadcast_to(col, (chunk, RET_DK)).astype(BF16)

    @pl.when(p == 0)
    def _():
        q_decay = full_bf16(jnp.exp(lgb * (chunk - ic)))
        k_decay = full_bf16(jnp.exp(lgb * ic))
        s_decay = vexp(lgb * chunk)
        for j in reversed(range(sub)):
            loc = pl.ds(j * chunk, chunk)
            glob = pl.ds(pl.multiple_of(((n_blocks - 1 - c) * sub + j) * chunk, chunk), chunk)
            qb = q_ref[loc, :] * q_decay
            ob_scr[glob, :] = jnp.dot(qb, sb_scr[...].astype(BF16), preferred_element_type=F32)
            kb = k_ref[loc, :] * k_decay
            sb_scr[...] = s_decay * sb_scr[...] + _tn_dot(kb, v_ref[loc, :])

    @pl.when(p == 1)
    def _():
        ri = lax.broadcasted_iota(I32, (chunk, chunk), 0)
        ci = lax.broadcasted_iota(I32, (chunk, chunk), 1)
        d = (ri - ci).astype(F32)
        mask = jnp.where(d > 0, jnp.exp(lgf * jnp.maximum(d, 0.0)),
                         jnp.where(d < 0, jnp.exp(lgb * jnp.maximum(-d, 0.0)), 2.0))
        q_decay = full_bf16(jnp.exp(lgf * (ic + 1.0)))
        k_decay = full_bf16(jnp.exp(lgf * (chunk - 1.0 - ic)))
        s_decay = vexp(lgf * chunk)
        for j in range(sub):
            loc = pl.ds(j * chunk, chunk)
            glob = pl.ds(pl.multiple_of((c * sub + j) * chunk, chunk), chunk)
            a = (_nt_dot(q_ref[loc, :], k_ref[loc, :]) * mask).astype(BF16)
            qf = q_ref[loc, :] * q_decay
            o = (jnp.dot(a, v_ref[loc, :], preferred_element_type=F32)
                 + jnp.dot(qf, sf_scr[...].astype(BF16), preferred_element_type=F32)
                 + ob_scr[glob, :])
            kf = k_ref[loc, :] * k_decay
            sf_scr[...] = s_decay * sf_scr[...] + _tn_dot(kf, v_ref[loc, :])
            ms = jnp.mean(o * o, axis=-1, keepdims=True)
            y = o * lax.rsqrt(ms + EPS) * gn_ref[0]
            gt = g_ref[loc, :]
            o_ref[loc, :] = y.astype(BF16) * (gt * jax.nn.sigmoid(gt))


def _retention(lg, p_lat, p_ctx, ret_norm_g, batch, seq, ctx_len, ctx_k_blk, ctx_v_blk):
    chunk = min(256, seq)
    sub = math.gcd(seq // chunk, RET_SUB)
    blk = chunk * sub
    nb = seq // blk
    kern = functools.partial(_ret_kernel, chunk=chunk, sub=sub, n_blocks=nb, ctx_len=ctx_len)

    def rows(b, p, c):
        return b * nb + jnp.where(p == 0, nb - 1 - c, c)

    def rows_fwd(b, p, c):
        return b * nb + jnp.where(p == 0, 0, c)

    return pl.pallas_call(
        kern,
        out_shape=jax.ShapeDtypeStruct((batch * seq, RET_HEADS * RET_DV), BF16),
        grid=(batch, RET_HEADS, 2, nb),
        in_specs=[pl.BlockSpec(memory_space=pltpu.SMEM),
                  pl.BlockSpec((blk, RET_DK), lambda b, h, p, c: (rows(b, p, c), h)),
                  pl.BlockSpec((blk, RET_DK), lambda b, h, p, c: (rows(b, p, c), 4 + h)),
                  pl.BlockSpec((blk, RET_DV), lambda b, h, p, c: (rows(b, p, c), 4 + h)),
                  pl.BlockSpec((blk, RET_DV), lambda b, h, p, c: (rows_fwd(b, p, c), 8 + h)),
                  pl.BlockSpec((ctx_len, RET_DK), lambda b, h, p, c: (b, ctx_k_blk + h)),
                  pl.BlockSpec((ctx_len, RET_DV), lambda b, h, p, c: (b, ctx_v_blk + h)),
                  pl.BlockSpec((1, 1, RET_DV), lambda b, h, p, c: (h, 0, 0))],
        out_specs=pl.BlockSpec((blk, RET_DV), lambda b, h, p, c: (rows_fwd(b, p, c), h)),
        scratch_shapes=[pltpu.VMEM((RET_DK, RET_DV), F32), pltpu.VMEM((RET_DK, RET_DV), F32),
                        pltpu.VMEM((seq, RET_DV), F32)],
        compiler_params=_params(("arbitrary",) * 4, 48),
        name="ret",
    )(lg, p_lat, p_lat, p_lat, p_lat, p_ctx, p_ctx, ret_norm_g.reshape(RET_HEADS, 1, RET_DV))


def _dattn_kernel(lam_ref, q_ref, ck_ref, cv_ref, k_ref, v_ref, gn_ref, gmat_ref, o_ref,
                  qq_scr, kmax_scr, mp_scr, kp_scr, vt_scr, pt_scr, *, tq, tk, n_kv, ta, unroll, ctx_len):
    rows = 2 * tq
    n_all = kp_scr.shape[0]
    qi = pl.program_id(2)

    def sq_norms(t):
        tf = t.astype(F32)
        return jnp.dot((tf * tf).astype(BF16), gmat_ref[...], preferred_element_type=F32)

    def lane_const(n, hot):
        return jnp.where(lax.broadcasted_iota(I32, (n, LANES), 1) < hot, 1.0, 0.0).astype(BF16)

    @pl.when(qi == 0)
    def _():
        kmax_scr[...] = jnp.max(sq_norms(ck_ref[...]), axis=0, keepdims=True)
        kp_scr[:, LANES:2 * LANES] = lane_const(n_all, 3)
        kp_scr[0:ctx_len, 0:LANES] = ck_ref[...]
        vt_scr[:, 0:ctx_len] = cv_ref[...].astype(F32).T.astype(BF16)

        prep = math.gcd(n_kv, 4)

        def body(c, carry):
            norms = []
            for u in range(prep):
                start = pl.multiple_of((c * prep + u) * tk, tk)
                k = k_ref[pl.ds(start, tk), :]
                norms.append(jnp.max(sq_norms(k), axis=0, keepdims=True))
                kp_scr[pl.ds(pl.multiple_of(ctx_len + start, LANES), tk), 0:LANES] = k
                vt_scr[:, pl.ds(pl.multiple_of(ctx_len + start, LANES), tk)] = (
                    v_ref[pl.ds(start, tk), :].astype(F32).T.astype(BF16))
            kmax_scr[...] = jnp.maximum(kmax_scr[...], functools.reduce(jnp.maximum, norms))
            return carry

        lax.fori_loop(0, n_kv // prep, body, 0)

    qt = q_ref[...].astype(F32).T
    row = lax.broadcasted_iota(I32, (LANES, tq), 0)
    q1t = jnp.where(row < DIFF_DH, qt, 0.0)
    q2t = jnp.where(row >= DIFF_DH, qt, 0.0)
    qq_scr[0:LANES, 0:tq] = q1t.astype(BF16)
    qq_scr[0:LANES, tq:rows] = q2t.astype(BF16)

    def set_shift(shift):
        neg = -shift
        hi = neg.astype(BF16).astype(F32)
        mid = (neg - hi).astype(BF16).astype(F32)
        lo = neg - hi - mid
        row_r = lax.broadcasted_iota(I32, (LANES, rows), 0)
        pieces = jnp.where(row_r == 0, hi, jnp.where(row_r == 1, mid, jnp.where(row_r == 2, lo, 0.0)))
        qq_scr[LANES:2 * LANES, :] = pieces.astype(BF16)

    kmax = kmax_scr[...]
    b1 = jnp.sqrt(jnp.sum(q1t * q1t, axis=0, keepdims=True) * kmax[:, 0:1]) * SHIFT_SLACK
    b2 = jnp.sqrt(jnp.sum(q2t * q2t, axis=0, keepdims=True) * kmax[:, DIFF_DH:DIFF_DH + 1]) * SHIFT_SLACK
    bound = jnp.concatenate([b1, b2], axis=1)
    set_shift(bound)

    def key_tile(t):
        return pl.ds(pl.multiple_of(t * ta, ta), ta)

    @pl.when(jnp.max(bound) > MAX_SAFE_SHIFT)
    def _():
        mp_scr[...] = jnp.full(mp_scr.shape, NEG_BIG, F32)

        def body(t, carry):
            st = jnp.dot(kp_scr[key_tile(t), 0:LANES], qq_scr[0:LANES, :], preferred_element_type=F32)
            mp_scr[...] = jnp.maximum(mp_scr[...], jnp.max(st, axis=0, keepdims=True))
            return carry

        lax.fori_loop(0, n_all // ta, body, 0)
        set_shift(mp_scr[0:1, :])

    mp_scr[...] = jnp.zeros(mp_scr.shape, F32)

    def body_a(t, carry):
        for u in range(unroll):
            r = key_tile(t * unroll + u)
            p = jnp.exp2(jnp.dot(kp_scr[r, :], qq_scr[...], preferred_element_type=F32))
            pt_scr[r, :] = p.astype(BF16)
            mp_scr[...] += jnp.sum(p.reshape(ta // SUBLANES, SUBLANES, rows), axis=0)
        return carry

    lax.fori_loop(0, n_all // (ta * unroll), body_a, 0)

    acc = jnp.dot(vt_scr[...], pt_scr[...], preferred_element_type=F32)
    ot = acc / jnp.sum(mp_scr[...], axis=0, keepdims=True)
    d = (ot[:, 0:tq] - lam_ref[0] * ot[:, tq:rows]).T
    ms = jnp.mean(d * d, axis=-1, keepdims=True)
    y = d * lax.rsqrt(ms + EPS) * gn_ref[...] * (1.0 - LAMBDA_INIT)
    o_ref[...] = y.astype(BF16)


def _diff_attention(lam, p_lat, p_ctx, diff_norm_g, gmat, batch, seq, ctx_len, ctx_k_blk, ctx_v_blk):
    tq = min(1024, seq)
    tk = min(512, seq)
    nq, nk = seq // tq, seq // tk
    n_all = ctx_len + seq
    ta = next(t for t in (528, 512, 384, 320, 256, 128) if n_all % t == 0)
    unroll = math.gcd(n_all // ta, KV_UNROLL)
    assert ctx_len % LANES == 0
    kern = functools.partial(_dattn_kernel, tq=tq, tk=tk, n_kv=nk, ta=ta, unroll=unroll, ctx_len=ctx_len)
    return pl.pallas_call(
        kern,
        out_shape=jax.ShapeDtypeStruct((batch * seq, DIFF_HEADS * DIFF_DV), BF16),
        grid=(batch, DIFF_HEADS, nq),
        in_specs=[pl.BlockSpec(memory_space=pltpu.SMEM),
                  pl.BlockSpec((tq, LANES), lambda b, h, qi: (b * nq + qi, 48 + h)),
                  pl.BlockSpec((ctx_len, LANES), lambda b, h, qi: (b, ctx_k_blk + h)),
                  pl.BlockSpec((ctx_len, LANES), lambda b, h, qi: (b, ctx_v_blk + h)),
                  pl.BlockSpec((seq, LANES), lambda b, h, qi: (b, 56 + h)),
                  pl.BlockSpec((seq, LANES), lambda b, h, qi: (b, 64 + h)),
                  pl.BlockSpec((1, LANES), lambda b, h, qi: (0, 0)),
                  pl.BlockSpec((LANES, LANES), lambda b, h, qi: (0, 0))],
        out_specs=pl.BlockSpec((tq, LANES), lambda b, h, qi: (b * nq + qi, h)),
        scratch_shapes=[pltpu.VMEM((2 * LANES, 2 * tq), BF16), pltpu.VMEM((1, LANES), F32),
                        pltpu.VMEM((SUBLANES, 2 * tq), F32), pltpu.VMEM((n_all, 2 * LANES), BF16),
                        pltpu.VMEM((DIFF_DV, n_all), BF16), pltpu.VMEM((n_all, 2 * tq), BF16)],
        compiler_params=_params(("arbitrary",) * 3, 58),
        name="dattn",
    )(lam, p_lat, p_ctx, p_ctx, p_lat, p_lat, diff_norm_g.reshape(1, DIFF_DV), gmat)


def _merge_kernel(yr_ref, yd_ref, ga_ref, gb_ref, x_ref, gatea_ref, shf_ref, scf_ref, n2_ref,
                  wr_ref, wd_ref, wo_ref, rwh_ref, rwl_ref, rb_ref,
                  xn_ref, hp_ref, meta_ref, gate_ref, cnt_ref, run_scr, *, tm):
    i = pl.program_id(0)

    @pl.when(i == 0)
    def _():
        run_scr[...] = jnp.zeros(run_scr.shape, F32)

    yr = jnp.dot(yr_ref[...], wr_ref[...], preferred_element_type=F32)
    yd = jnp.dot(yd_ref[...], wd_ref[...], preferred_element_type=F32)
    m = (jax.nn.sigmoid(ga_ref[...].astype(F32)) * yr + jax.nn.sigmoid(gb_ref[...].astype(F32)) * yd)
    z = jnp.dot(m.astype(BF16), wo_ref[...], preferred_element_type=F32)
    xn = x_ref[...] + gatea_ref[0] * z
    xn_ref[...] = xn

    ms = jnp.mean(xn * xn, axis=-1, keepdims=True)
    h2 = xn * lax.rsqrt(ms + EPS) * n2_ref[...]
    h2 = h2 * (1.0 + scf_ref[0]) + shf_ref[0]
    h_hi = h2.astype(BF16)
    bits = pltpu.bitcast(h_hi.astype(F32), U32)
    half = D_MODEL // 2
    hp_ref[...] = (bits[:, :half] >> 16) | (bits[:, half:] & jnp.uint32(0xFFFF0000))

    h_lo = (h2 - h_hi.astype(F32)).astype(BF16)
    logits = (jnp.dot(h_hi, rwh_ref[...], preferred_element_type=F32)
              + jnp.dot(h_lo, rwh_ref[...], preferred_element_type=F32)
              + jnp.dot(h_hi, rwl_ref[...], preferred_element_type=F32) + rb_ref[...])
    lane = lax.broadcasted_iota(I32, (tm, LANES), 1)
    lanef = lane.astype(F32)
    work = logits
    vals, idxs = [], []
    for _ in range(TOP_K):
        mk = jnp.max(work, axis=1, keepdims=True)
        ik = jnp.min(jnp.where(work == mk, lanef, float(LANES)), axis=1, keepdims=True)
        vals.append(mk)
        idxs.append(ik)
        work = jnp.where(lanef == ik, -jnp.inf, work)
    ex = [jnp.exp(v - vals[0]) for v in vals]
    den = ex[0] + ex[1] + ex[2] + ex[3]

    onehot = jnp.zeros((tm, LANES), F32)
    for ik in idxs:
        onehot = onehot + jnp.where(lanef == ik, 1.0, 0.0)
    ri = lax.broadcasted_iota(I32, (tm, tm), 0)
    ci = lax.broadcasted_iota(I32, (tm, tm), 1)
    tri = jnp.where(ri > ci, 1.0, 0.0).astype(BF16)
    base = run_scr[0:1, :] + jnp.dot(tri, onehot.astype(BF16), preferred_element_type=F32)
    run_scr[...] = run_scr[...] + jnp.sum(onehot, axis=0, keepdims=True)

    meta = jnp.zeros((tm, LANES), F32)
    gate_out = jnp.zeros((tm, LANES), F32)
    for k in range(TOP_K):
        rk = jnp.sum(jnp.where(lanef == idxs[k], base, 0.0), axis=1, keepdims=True)
        meta = jnp.where(lane == k, idxs[k], jnp.where(lane == TOP_K + k, rk, meta))
        gate_out = jnp.where(lane == k, ex[k] / den, gate_out)
    meta_ref[...] = meta.T[0:2 * TOP_K, :]
    gate_ref[...] = gate_out
    cnt_ref[...] = run_scr[...]


def _merge(y_ret, y_diff, p_lat, x2, g_a, sh_f, sc_f, norm2_g, w_r, w_d, w_o, rw_hi, rw_lo, rb, seq):
    n = x2.shape[0]
    tm = min(1024, seq)
    tpb = seq // tm
    kern = functools.partial(_merge_kernel, tm=tm)
    mod_spec = pl.BlockSpec((1, 1, D_MODEL), lambda i: (i // tpb, 0, 0))
    const = lambda shape: pl.BlockSpec(shape, lambda i: (0,) * len(shape))
    tok = lambda w: pl.BlockSpec((tm, w), lambda i: (i, 0))
    return pl.pallas_call(
        kern,
        out_shape=(jax.ShapeDtypeStruct((n, D_MODEL), F32),
                   jax.ShapeDtypeStruct((n, D_MODEL // 2), U32),
                   jax.ShapeDtypeStruct((2 * TOP_K, n), F32),
                   jax.ShapeDtypeStruct((n, LANES), F32),
                   jax.ShapeDtypeStruct((8, LANES), F32)),
        grid=(n // tm,),
        in_specs=[tok(RET_HEADS * RET_DV), tok(D_MODEL),
                  pl.BlockSpec((tm, COL_TILE), lambda i: (i, 9)),
                  pl.BlockSpec((tm, COL_TILE), lambda i: (i, 10)),
                  tok(D_MODEL), mod_spec, mod_spec, mod_spec, const((1, D_MODEL)),
                  const((RET_HEADS * RET_DV, D_MODEL)), const((D_MODEL, D_MODEL)), const((D_MODEL, D_MODEL)),
                  const((D_MODEL, LANES)), const((D_MODEL, LANES)), const((1, LANES))],
        out_specs=(tok(D_MODEL), tok(D_MODEL // 2), pl.BlockSpec((2 * TOP_K, tm), lambda i: (0, i)),
                   tok(LANES), const((8, LANES))),
        scratch_shapes=[pltpu.VMEM((8, LANES), F32)],
        compiler_params=_params(("arbitrary",), 60),
        name="merge",
    )(y_ret, y_diff, p_lat, p_lat, x2, g_a, sh_f, sc_f, norm2_g, w_r, w_d, w_o, rw_hi, rw_lo, rb)


def _dispatch_kernel(dest_ref, fill_ref, nb_ref, hp_ref, xb_ref, zero_scr, stage, sems, zsem, *,
                     tm, n_tok, n_blocks, n_steps):
    i = pl.program_id(0)

    @pl.when(i == 0)
    def _():
        zero_scr[...] = jnp.zeros(zero_scr.shape, U32)
        fill = zero_scr.shape[0]
        for e in range(N_EXPERTS):
            start = pl.multiple_of(fill_ref[e], SUBLANES)
            pltpu.make_async_copy(zero_scr, xb_ref.at[pl.ds(start, fill), :], zsem).start()
        for e in range(N_EXPERTS):
            pltpu.make_async_copy(zero_scr, xb_ref.at[pl.ds(0, fill), :], zsem).wait()
        blk = zero_scr.at[pl.ds(0, MOE_BLK), :]

        def start_blk(b, carry):
            pltpu.make_async_copy(blk, xb_ref.at[pl.ds(pl.multiple_of(b * MOE_BLK, MOE_BLK), MOE_BLK), :], zsem).start()
            return carry

        def wait_blk(b, carry):
            pltpu.make_async_copy(blk, xb_ref.at[pl.ds(0, MOE_BLK), :], zsem).wait()
            return carry

        lax.fori_loop(nb_ref[0], n_blocks, start_blk, 0)
        lax.fori_loop(nb_ref[0], n_blocks, wait_blk, 0)

    def drain(slot):
        for _ in range(TOP_K):
            pltpu.make_async_copy(stage.at[slot], xb_ref.at[pl.ds(0, tm), :], sems.at[slot]).wait()

    for slot in range(2):
        @pl.when(lax.rem(i, 2) == slot)
        def _():
            stage[slot] = hp_ref[...]

            def body(g, carry):
                r0 = pl.multiple_of(g * SUBLANES, SUBLANES)
                for j in range(SUBLANES):
                    for k in range(TOP_K):
                        d = dest_ref[k * n_tok + i * tm + r0 + j]
                        pltpu.make_async_copy(stage.at[slot, pl.ds(r0 + j, 1), :], xb_ref.at[pl.ds(d, 1), :],
                                              sems.at[slot]).start(priority=k % 2)
                return carry

            lax.fori_loop(0, tm // SUBLANES, body, 0)

            @pl.when(i > 0)
            def _():
                drain(1 - slot)

            @pl.when(i == n_steps - 1)
            def _():
                drain(slot)


def _dispatch(dest, fill_start, n_used, hp, n_rows):
    n = hp.shape[0]
    tm = min(512, n)
    kern = functools.partial(_dispatch_kernel, tm=tm, n_tok=n, n_blocks=n_rows // MOE_BLK, n_steps=n // tm)
    return pl.pallas_call(
        kern,
        out_shape=jax.ShapeDtypeStruct((n_rows, D_MODEL // 2), U32),
        grid_spec=pltpu.PrefetchScalarGridSpec(
            num_scalar_prefetch=3,
            grid=(n // tm,),
            in_specs=[pl.BlockSpec((tm, D_MODEL // 2), lambda i, d, f, nb: (i, 0))],
            out_specs=pl.BlockSpec(memory_space=pl.ANY),
            scratch_shapes=[pltpu.VMEM((MOE_BLK + SUBLANES, D_MODEL // 2), U32),
                            pltpu.VMEM((2, tm, D_MODEL // 2), U32),
                            pltpu.SemaphoreType.DMA((2,)), pltpu.SemaphoreType.DMA(())]),
        compiler_params=_params(("arbitrary",), 32),
        name="dispatch",
    )(dest, fill_start, n_used, hp)


def _expert_kernel(be_ref, nb_ref, nxt_ref, xb_ref, w1_hbm, b1_ref, w2_hbm, b2_ref, y_ref,
                   w1f_scr, w2f_scr, w1b_scr, w2b_scr, x_scr, grp_scr, sems):
    i = pl.program_id(0)
    e = be_ref[i]
    prev = be_ref[jnp.maximum(i - 1, 0)]
    active = i < nb_ref[0]

    def weight_copies(expert, slot):
        return (pltpu.make_async_copy(w1_hbm.at[expert], w1f_scr.at[slot], sems.at[0, slot]),
                pltpu.make_async_copy(w2_hbm.at[expert], w2f_scr.at[slot], sems.at[1, slot]))

    @pl.when(i == 0)
    def _():
        grp_scr[0] = 0
        for cp in weight_copies(e, 0):
            cp.start()

    @pl.when(active & (i > 0) & (e != prev))
    def _():
        grp_scr[0] = grp_scr[0] + 1

    @pl.when(active & ((i == 0) | (e != prev)))
    def _():
        slot = grp_scr[0] % 2
        for cp in weight_copies(e, slot):
            cp.wait()
        w1b_scr[...] = w1f_scr[slot].astype(BF16)
        w2b_scr[...] = w2f_scr[slot].astype(BF16)
        nxt = nxt_ref[e]

        @pl.when(nxt >= 0)
        def _():
            for cp in weight_copies(nxt, 1 - slot):
                cp.start()

    @pl.when(active)
    def _():
        xu = xb_ref[...]
        half = D_MODEL // 2
        x_scr[:, 0:half] = pltpu.bitcast(xu << 16, F32).astype(BF16)
        x_scr[:, half:] = pltpu.bitcast(xu & jnp.uint32(0xFFFF0000), F32).astype(BF16)
        hh = jnp.dot(x_scr[...], w1b_scr[...], preferred_element_type=F32) + b1_ref[0]
        glu = jnp.minimum(hh[:, :D_FF], SWIGLU_LIMIT)
        lin = jnp.clip(hh[:, D_FF:], -SWIGLU_LIMIT, SWIGLU_LIMIT)
        act = glu * jax.nn.sigmoid(SWIGLU_ALPHA * glu) * (lin + 1.0)
        y_ref[...] = jnp.dot(act.astype(BF16), w2b_scr[...], preferred_element_type=F32) + b2_ref[0]

    @pl.when(i >= nb_ref[0])
    def _():
        y_ref[...] = jnp.zeros(y_ref.shape, F32)


def _experts(block_e, n_used, next_e, xb, w1, b1, w2, b2):
    n_blocks = block_e.shape[0]
    rows = n_blocks * MOE_BLK
    return pl.pallas_call(
        _expert_kernel,
        out_shape=jax.ShapeDtypeStruct((rows, D_MODEL), F32),
        grid_spec=pltpu.PrefetchScalarGridSpec(
            num_scalar_prefetch=3,
            grid=(n_blocks,),
            in_specs=[pl.BlockSpec((MOE_BLK, D_MODEL // 2), lambda i, be, nb, nx: (jnp.minimum(i, nb[0] - 1), 0)),
                      pl.BlockSpec(memory_space=pl.ANY),
                      pl.BlockSpec((1, 1, 2 * D_FF), lambda i, be, nb, nx: (be[i], 0, 0)),
                      pl.BlockSpec(memory_space=pl.ANY),
                      pl.BlockSpec((1, 1, D_MODEL), lambda i, be, nb, nx: (be[i], 0, 0))],
            out_specs=pl.BlockSpec((MOE_BLK, D_MODEL), lambda i, be, nb, nx: (i, 0)),
            scratch_shapes=[pltpu.VMEM((2, D_MODEL, 2 * D_FF), F32), pltpu.VMEM((2, D_FF, D_MODEL), F32),
                            pltpu.VMEM((D_MODEL, 2 * D_FF), BF16), pltpu.VMEM((D_FF, D_MODEL), BF16),
                            pltpu.VMEM((MOE_BLK, D_MODEL), BF16),
                            pltpu.SMEM((1,), I32), pltpu.SemaphoreType.DMA((2, 2))]),
        compiler_params=_params(("arbitrary",), 56),
        name="expert",
    )(block_e, n_used, next_e, xb, w1, b1.reshape(N_EXPERTS, 1, 2 * D_FF), w2, b2.reshape(N_EXPERTS, 1, D_MODEL))


def _combine_kernel(dest_ref, yb_ref, gate_ref, xn_ref, gf_ref, o_ref, buf, sems, *, tm, n_tok, n_steps):
    i = pl.program_id(0)

    def issue(tile, slot):
        def body(g, carry):
            r0 = pl.multiple_of(g * SUBLANES, SUBLANES)
            for j in range(SUBLANES):
                for k in range(TOP_K):
                    d = dest_ref[k * n_tok + tile * tm + r0 + j]
                    pltpu.make_async_copy(yb_ref.at[pl.ds(d, 1), :], buf.at[slot, k, pl.ds(r0 + j, 1), :],
                                          sems.at[slot]).start(priority=k % 2)
            return carry

        lax.fori_loop(0, tm // SUBLANES, body, 0)

    def finish(slot):
        for k in range(TOP_K):
            pltpu.make_async_copy(yb_ref.at[pl.ds(0, tm), :], buf.at[slot, k], sems.at[slot]).wait()
        g = gate_ref[...]
        y = g[:, 0:1] * buf[slot, 0]
        for k in range(1, TOP_K):
            y = y + g[:, k:k + 1] * buf[slot, k]
        o_ref[...] = xn_ref[...] + gf_ref[0] * y

    @pl.when(i == 0)
    def _():
        issue(0, 0)

    for slot in range(2):
        @pl.when(lax.rem(i, 2) == slot)
        def _():
            @pl.when(i + 1 < n_steps)
            def _():
                issue(i + 1, 1 - slot)

            finish(slot)


def _combine(dest, yb, gate, xn, g_f, seq):
    n = xn.shape[0]
    tm = min(512, seq)
    tpb = seq // tm
    kern = functools.partial(_combine_kernel, tm=tm, n_tok=n, n_steps=n // tm)
    return pl.pallas_call(
        kern,
        out_shape=jax.ShapeDtypeStruct((n, D_MODEL), F32),
        grid_spec=pltpu.PrefetchScalarGridSpec(
            num_scalar_prefetch=1,
            grid=(n // tm,),
            in_specs=[pl.BlockSpec(memory_space=pl.ANY),
                      pl.BlockSpec((tm, LANES), lambda i, d: (i, 0)),
                      pl.BlockSpec((tm, D_MODEL), lambda i, d: (i, 0)),
                      pl.BlockSpec((1, 1, D_MODEL), lambda i, d: (i // tpb, 0, 0))],
            out_specs=pl.BlockSpec((tm, D_MODEL), lambda i, d: (i, 0)),
            scratch_shapes=[pltpu.VMEM((2, TOP_K, tm, D_MODEL), F32), pltpu.SemaphoreType.DMA((2,))]),
        compiler_params=_params(("arbitrary",), 40),
        name="combine",
    )(dest, yb, gate, xn, g_f)


def _layer(x, ctx, c, c_ctx, norm1_g, norm2_g, w_mod, b_mod, w_in, ret_decay_logit, ret_norm_g,
           diff_q_norm_g, diff_k_norm_g, diff_lambda, diff_norm_g, w_br_ret, w_br_diff, w_out,
           router_w, router_b, exp_w1, exp_b1, exp_w2, exp_b2):
    batch, seq, d = x.shape
    ctx_len = ctx.shape[1]
    assert d == D_MODEL and seq % GRID_W == 0 and batch + 1 <= 8
    n_tok = batch * seq

    cc = jnp.zeros((8, D_MODEL), F32).at[:batch].set(c).at[batch].set(c_ctx)
    mod = _mod(cc, w_mod, b_mod)
    sh_a, sc_a, g_a, sh_f, sc_f, g_f = [mod[:batch, i * D_MODEL:(i + 1) * D_MODEL].reshape(batch, 1, D_MODEL)
                                         for i in range(6)]
    csh_a = mod[batch:batch + 1, 0:D_MODEL].reshape(1, 1, D_MODEL)
    csc_a = mod[batch:batch + 1, D_MODEL:2 * D_MODEL].reshape(1, 1, D_MODEL)

    w_in_bf = w_in.astype(BF16)
    g1 = norm1_g.reshape(1, D_MODEL)
    tile = lambda g: jnp.tile(g.astype(F32), 2)
    qkg = jnp.zeros((8, LANES), F32).at[0].set(tile(diff_q_norm_g) * (DIFF_DH ** -0.5 * LOG2E)).at[1].set(tile(diff_k_norm_g))
    lane = jnp.arange(2 * LANES)
    same_group = lane[:, None] // DIFF_DH == lane[None, :] // DIFF_DH
    gmean = jnp.where(same_group, 1.0 / DIFF_DH, 0.0).astype(BF16)
    gmat = same_group[:LANES, :LANES].astype(BF16)
    x2 = x.reshape(n_tok, D_MODEL)
    tm = min(1024, seq)
    p_lat = _inproj(x2, g1, sh_a, sc_a, w_in_bf, qkg, gmean, _rope_tables(seq), tm, seq // tm,
                    tuple(range(IN_COLS // COL_TILE)))
    p_ctx = _inproj(ctx.reshape(batch * ctx_len, D_MODEL), g1, csh_a, csc_a, w_in_bf, qkg, gmean,
                    _identity_tables(ctx_len), ctx_len, 1, CTX_COLS)

    lg = jax.nn.log_sigmoid(ret_decay_logit.astype(F32))
    y_ret = _retention(lg, p_lat, p_ctx, ret_norm_g, batch, seq, ctx_len,
                       CTX_COLS.index(1) * COL_TILE // RET_DK, CTX_COLS.index(2) * COL_TILE // RET_DV)

    lp = diff_lambda.astype(F32)
    lam = (jnp.exp(jnp.sum(lp[0] * lp[1])) - jnp.exp(jnp.sum(lp[2] * lp[3])) + LAMBDA_INIT).reshape(1)
    y_diff = _diff_attention(lam, p_lat, p_ctx, diff_norm_g, gmat, batch, seq, ctx_len,
                             CTX_COLS.index(7) * COL_TILE // LANES, CTX_COLS.index(8) * COL_TILE // LANES)

    rw = jnp.zeros((D_MODEL, LANES), F32).at[:, :N_EXPERTS].set(router_w)
    rw_hi = rw.astype(BF16)
    rw_lo = (rw - rw_hi.astype(F32)).astype(BF16)
    rb = jnp.full((1, LANES), NEG_BIG, F32).at[0, :N_EXPERTS].set(router_b)
    xn, hp, meta, gate4, cnt = _merge(
        y_ret, y_diff, p_lat, x2, g_a, sh_f, sc_f, norm2_g.reshape(1, D_MODEL),
        w_br_ret.astype(BF16), w_br_diff.astype(BF16), w_out.astype(BF16), rw_hi, rw_lo, rb, seq)

    counts = cnt[0, :N_EXPERTS].astype(I32)
    padded = (counts + MOE_BLK - 1) // MOE_BLK * MOE_BLK
    pad_end = jnp.cumsum(padded)
    pad_start = pad_end - padded
    n_pairs = n_tok * TOP_K
    n_blocks = n_pairs // MOE_BLK + N_EXPERTS
    meta_i = meta.astype(I32)
    is_e = meta_i[None, :TOP_K] == jnp.arange(N_EXPERTS, dtype=I32)[:, None, None]
    dest = (jnp.sum(jnp.where(is_e, pad_start[:, None, None], 0), axis=0) + meta_i[TOP_K:]).reshape(n_pairs)
    block_start = jnp.arange(n_blocks, dtype=I32) * MOE_BLK
    block_e = jnp.minimum(jnp.sum((pad_end[None, :] <= block_start[:, None]).astype(I32), axis=1), N_EXPERTS - 1)
    n_used = (pad_end[-1] // MOE_BLK).reshape(1).astype(I32)

    fill_start = ((pad_start + counts) // SUBLANES * SUBLANES).astype(I32)
    xb = _dispatch(dest, fill_start, n_used, hp, (n_blocks + 1) * MOE_BLK)
    e_ids = jnp.arange(N_EXPERTS, dtype=I32)
    later = jnp.where((counts[None, :] > 0) & (e_ids[None, :] > e_ids[:, None]), e_ids[None, :], N_EXPERTS)
    next_e = jnp.min(later, axis=1)
    next_e = jnp.where(next_e == N_EXPERTS, -1, next_e).astype(I32)
    yb = _experts(block_e, n_used, next_e, xb, exp_w1, exp_b1, exp_w2, exp_b2)
    out = _combine(dest, yb, gate4, xn, g_f, seq)
    return out.reshape(batch, seq, D_MODEL)


def kernel(x, c, ctx, c_ctx, norm1_g, norm2_g, w_mod, b_mod, w_in, ret_decay_logit, ret_norm_g, diff_q_norm_g, diff_k_norm_g, diff_lambda, diff_norm_g, w_br_ret, w_br_diff, w_out, router_w, router_b, exp_w1, exp_b1, exp_w2, exp_b2):
    assert norm1_g.shape[0] == 1, "single-layer block"
    return _layer(x, ctx, c, c_ctx, norm1_g[0], norm2_g[0], w_mod[0], b_mod[0], w_in[0], ret_decay_logit[0],
                  ret_norm_g[0], diff_q_norm_g[0], diff_k_norm_g[0], diff_lambda[0], diff_norm_g[0],
                  w_br_ret[0], w_br_diff[0], w_out[0], router_w[0], router_b[0],
                  exp_w1[0], exp_b1[0], exp_w2[0], exp_b2[0])
```

```python
import functools
import math

import numpy as np

import jax
import jax.numpy as jnp
from jax import lax
from jax.experimental import pallas as pl
from jax.experimental.pallas import tpu as pltpu

F32 = jnp.float32
BF16 = jnp.bfloat16
U32 = jnp.uint32
I32 = jnp.int32

D_MODEL = 1024
GRID_W = 64
RET_HEADS = 4
RET_DK = 256
RET_DV = 512
DIFF_DH = 64
DIFF_HEADS = 8
DIFF_DV = 128
N_EXPERTS = 32
TOP_K = 4
D_FF = 1024
SWIGLU_LIMIT = 7.0
SWIGLU_ALPHA = 1.702
ROPE_BASE = 10000.0
EPS = 1e-6
LAMBDA_INIT = 0.8 - 0.6 * math.exp(-0.3 * 0)

IN_COLS = 11264
COL_TILE = 1024
CTX_COLS = (1, 2, 3, 7, 8)
LANES = 128
SUBLANES = 8
MOE_BLK = 256
EXPERT_PAIR = 2
NEG_BIG = -1e30
LOG2E = 1.4426950408889634
SHIFT_SLACK = 1.0 + 2.0 ** -6
MAX_SAFE_SHIFT = 60.0
RET_SUB = 8
KV_UNROLL = 8
HIGHEST = lax.Precision.HIGHEST
MIB = 1024 * 1024


def _params(sem, vmem_mib):
    return pltpu.CompilerParams(dimension_semantics=sem, vmem_limit_bytes=vmem_mib * MIB)


def _mod_kernel(c_ref, w_ref, b_ref, o_ref):
    c = c_ref[...]
    s = c * jax.nn.sigmoid(c)
    o_ref[...] = jnp.dot(s, w_ref[...], preferred_element_type=F32, precision=HIGHEST) + b_ref[...]


def _mod(cc, w_mod, b_mod):
    n = w_mod.shape[1]
    tn = 1024
    return pl.pallas_call(
        _mod_kernel,
        out_shape=jax.ShapeDtypeStruct((8, n), F32),
        grid=(n // tn,),
        in_specs=[pl.BlockSpec((8, D_MODEL), lambda j: (0, 0)),
                  pl.BlockSpec((D_MODEL, tn), lambda j: (0, j)),
                  pl.BlockSpec((1, tn), lambda j: (0, j))],
        out_specs=pl.BlockSpec((8, tn), lambda j: (0, j)),
        compiler_params=_params(("arbitrary",), 32),
        name="mod",
    )(cc, w_mod, b_mod.reshape(1, n))


def _inproj_kernel(cols_ref, x_ref, g_ref, sh_ref, sc_ref, w_ref, qkg_ref, gmat_ref,
                   cr_ref, sr_ref, cc_ref, sc2_ref, cd_ref, sa_ref, sb_ref,
                   o_ref, h_scr, acc_scr):
    j = pl.program_id(1)
    jc = cols_ref[j]

    @pl.when(j == 0)
    def _():
        xf = x_ref[...]
        ms = jnp.mean(xf * xf, axis=-1, keepdims=True)
        y = xf * lax.rsqrt(ms + EPS) * g_ref[...]
        h_scr[...] = (y * (1.0 + sc_ref[0]) + sh_ref[0]).astype(BF16)

    is_ret = jc <= 1
    is_dqk = (jc == 6) | (jc == 7)

    def project():
        return jnp.dot(h_scr[...], w_ref[...], preferred_element_type=F32)

    @pl.when(is_ret)
    def _():
        acc_scr[...] = project()
        scale = jnp.where(jc == 0, RET_DK ** -0.5, 1.0).astype(F32)
        for b in range(COL_TILE // LANES):
            xb = acc_scr[:, b * LANES:(b + 1) * LANES]
            cos = cr_ref[...] if b % 2 == 0 else cc_ref[...]
            sin = sr_ref[...] if b % 2 == 0 else sc2_ref[...]
            o = (xb * cos + pltpu.roll(xb, 64, 1) * sin) * scale
            o_ref[:, b * LANES:(b + 1) * LANES] = o.astype(BF16)

    @pl.when(is_dqk)
    def _():
        acc_scr[...] = project()
        g = jnp.where(jc == 6, qkg_ref[0:1, :], qkg_ref[1:2, :])
        for b2 in range(COL_TILE // (2 * LANES)):
            x2 = acc_scr[:, b2 * 2 * LANES:(b2 + 1) * 2 * LANES]
            ms2 = jnp.dot((x2 * x2).astype(BF16), gmat_ref[...], preferred_element_type=F32)
            for half in range(2):
                b = 2 * b2 + half
                xb = x2[:, half * LANES:(half + 1) * LANES]
                yn = xb * lax.rsqrt(ms2[:, half * LANES:(half + 1) * LANES] + EPS) * g
                o = yn * cd_ref[...] + pltpu.roll(yn, 16, 1) * sa_ref[...] + pltpu.roll(yn, 112, 1) * sb_ref[...]
                o_ref[:, b * LANES:(b + 1) * LANES] = o.astype(BF16)

    @pl.when(jnp.logical_not(is_ret | is_dqk))
    def _():
        o_ref[...] = project().astype(BF16)


def _inproj(x2, g1, sh, sc, w_bf, qkg, gmat, tables, tm, tiles_per_batch, cols):
    n = x2.shape[0]
    nb = sh.shape[0]
    tab_spec = pl.BlockSpec((tm, LANES), lambda i, j, c: (i % tiles_per_batch, 0))
    mod_spec = pl.BlockSpec((1, 1, D_MODEL), lambda i, j, c: (jnp.minimum(i // tiles_per_batch, nb - 1), 0, 0))
    return pl.pallas_call(
        _inproj_kernel,
        out_shape=jax.ShapeDtypeStruct((n, len(cols) * COL_TILE), BF16),
        grid_spec=pltpu.PrefetchScalarGridSpec(
            num_scalar_prefetch=1,
            grid=(n // tm, len(cols)),
            in_specs=[pl.BlockSpec((tm, D_MODEL), lambda i, j, c: (i, 0)),
                      pl.BlockSpec((1, D_MODEL), lambda i, j, c: (0, 0)),
                      mod_spec, mod_spec,
                      pl.BlockSpec((D_MODEL, COL_TILE), lambda i, j, c: (0, c[j])),
                      pl.BlockSpec((8, LANES), lambda i, j, c: (0, 0)),
                      pl.BlockSpec((2 * LANES, 2 * LANES), lambda i, j, c: (0, 0))] + [tab_spec] * 7,
            out_specs=pl.BlockSpec((tm, COL_TILE), lambda i, j, c: (i, j)),
            scratch_shapes=[pltpu.VMEM((tm, D_MODEL), BF16), pltpu.VMEM((tm, COL_TILE), F32)]),
        compiler_params=_params(("arbitrary", "arbitrary"), 48),
        name="inproj",
    )(jnp.asarray(cols, I32), x2, g1, sh, sc, w_bf, qkg, gmat, *tables)


def _rope_tables(seq):
    n_rows = seq // GRID_W
    f32 = np.float32

    def angles(pos, half):
        inv = f32(ROPE_BASE) ** (-np.arange(half, dtype=f32) / f32(half))
        return (pos.astype(f32)[:, None] * inv[None, :]).astype(np.float64)

    ar, ac = angles(np.arange(n_rows), 64), angles(np.arange(GRID_W), 64)
    br, bc = angles(np.arange(n_rows), 16), angles(np.arange(GRID_W), 16)
    zr, zc = np.zeros_like(br), np.zeros_like(bc)
    cat = lambda parts, reps=1: np.tile(np.concatenate(parts, axis=1), (1, reps)).astype(f32)
    by_row = lambda t: jnp.repeat(jnp.asarray(t), GRID_W, axis=0)
    by_col = lambda t: jnp.tile(jnp.asarray(t), (n_rows, 1))
    cr = by_row(cat([np.cos(ar), np.cos(ar)]))
    sr = by_row(cat([-np.sin(ar), np.sin(ar)]))
    cc = by_col(cat([np.cos(ac), np.cos(ac)]))
    sc = by_col(cat([-np.sin(ac), np.sin(ac)]))
    cd = by_row(cat([np.cos(br), np.cos(br), zr, zr], 2)) + by_col(cat([zc, zc, np.cos(bc), np.cos(bc)], 2))
    sa = by_row(cat([zr, np.sin(br), zr, zr], 2)) + by_col(cat([zc, zc, zc, np.sin(bc)], 2))
    sb = by_row(cat([-np.sin(br), zr, zr, zr], 2)) + by_col(cat([zc, zc, -np.sin(bc), zc], 2))
    return [cr, sr, cc, sc, cd, sa, sb]


def _identity_tables(seq):
    one = jnp.ones((seq, LANES), F32)
    zero = jnp.zeros((seq, LANES), F32)
    return [one, zero, one, zero, one, zero, zero]


def _tn_dot(a, b):
    return lax.dot_general(a, b, (((0,), (0,)), ((), ())), preferred_element_type=F32)


def _nt_dot(a, b):
    return lax.dot_general(a, b, (((1,), (1,)), ((), ())), preferred_element_type=F32)


def _ret_kernel(lg_ref, q_ref, k_ref, v_ref, g_ref, ck_ref, cv_ref, gn_ref, o_ref,
                sf_scr, sb_scr, ob_scr, *, chunk, sub, n_blocks, ctx_len):
    h = pl.program_id(1)
    p = pl.program_id(2)
    c = pl.program_id(3)
    lgf = lg_ref[0, h]
    lgb = lg_ref[1, h]

    def col_iota(n):
        return lax.broadcasted_iota(I32, (n, 1), 0).astype(F32)

    def vexp(s):
        return jnp.exp(jnp.zeros((1, 1), F32) + s)

    @pl.when((p == 0) & (c == 0))
    def _():
        jc = col_iota(ctx_len)
        kc = ck_ref[...].astype(F32)
        vc = cv_ref[...]
        sf_scr[...] = _tn_dot((kc * jnp.exp(lgf * (ctx_len - 1.0 - jc))).astype(BF16), vc)
        sb_scr[...] = _tn_dot((kc * jnp.exp(lgb * jc)).astype(BF16), vc)

    ic = col_iota(chunk)

    def full_bf16(col):
        return jnp.broadcast_to(col, (chunk, RET_DK)).astype(BF16)

    @pl.when(p == 0)
    def _():
        q_decay = full_bf16(jnp.exp(lgb * (chunk - ic)))
        k_decay = full_bf16(jnp.exp(lgb * ic))
        s_decay = vexp(lgb * chunk)
        for j in reversed(range(sub)):
            loc = pl.ds(j * chunk, chunk)
            glob = pl.ds(pl.multiple_of(((n_blocks - 1 - c) * sub + j) * chunk, chunk), chunk)
            qb = q_ref[loc, :] * q_decay
            ob_scr[glob, :] = jnp.dot(qb, sb_scr[...].astype(BF16), preferred_element_type=F32)
            kb = k_ref[loc, :] * k_decay
            sb_scr[...] = s_decay * sb_scr[...] + _tn_dot(kb, v_ref[loc, :])

    @pl.when(p == 1)
    def _():
        ri = lax.broadcasted_iota(I32, (chunk, chunk), 0)
        ci = lax.broadcasted_iota(I32, (chunk, chunk), 1)
        d = (ri - ci).astype(F32)
        mask = jnp.where(d > 0, jnp.exp(lgf * jnp.maximum(d, 0.0)),
                         jnp.where(d < 0, jnp.exp(lgb * jnp.maximum(-d, 0.0)), 2.0))
        q_decay = full_bf16(jnp.exp(lgf * (ic + 1.0)))
        k_decay = full_bf16(jnp.exp(lgf * (chunk - 1.0 - ic)))
        s_decay = vexp(lgf * chunk)
        for j in range(sub):
            loc = pl.ds(j * chunk, chunk)
            glob = pl.ds(pl.multiple_of((c * sub + j) * chunk, chunk), chunk)
            a = (_nt_dot(q_ref[loc, :], k_ref[loc, :]) * mask).astype(BF16)
            qf = q_ref[loc, :] * q_decay
            o = (jnp.dot(a, v_ref[loc, :], preferred_element_type=F32)
                 + jnp.dot(qf, sf_scr[...].astype(BF16), preferred_element_type=F32)
                 + ob_scr[glob, :])
            kf = k_ref[loc, :] * k_decay
            sf_scr[...] = s_decay * sf_scr[...] + _tn_dot(kf, v_ref[loc, :])
            ms = jnp.mean(o * o, axis=-1, keepdims=True)
            y = o * lax.rsqrt(ms + EPS) * gn_ref[0]
            gt = g_ref[loc, :]
            o_ref[loc, :] = y.astype(BF16) * (gt * jax.nn.sigmoid(gt))


def _retention(lg, p_lat, p_ctx, ret_norm_g, batch, seq, ctx_len, ctx_k_blk, ctx_v_blk):
    chunk = min(256, seq)
    sub = math.gcd(seq // chunk, RET_SUB)
    blk = chunk * sub
    nb = seq // blk
    kern = functools.partial(_ret_kernel, chunk=chunk, sub=sub, n_blocks=nb, ctx_len=ctx_len)

    def rows(b, p, c):
        return b * nb + jnp.where(p == 0, nb - 1 - c, c)

    def rows_fwd(b, p, c):
        return b * nb + jnp.where(p == 0, 0, c)

    return pl.pallas_call(
        kern,
        out_shape=jax.ShapeDtypeStruct((batch * seq, RET_HEADS * RET_DV), BF16),
        grid=(batch, RET_HEADS, 2, nb),
        in_specs=[pl.BlockSpec(memory_space=pltpu.SMEM),
                  pl.BlockSpec((blk, RET_DK), lambda b, h, p, c: (rows(b, p, c), h)),
                  pl.BlockSpec((blk, RET_DK), lambda b, h, p, c: (rows(b, p, c), 4 + h)),
                  pl.BlockSpec((blk, RET_DV), lambda b, h, p, c: (rows(b, p, c), 4 + h)),
                  pl.BlockSpec((blk, RET_DV), lambda b, h, p, c: (rows_fwd(b, p, c), 8 + h)),
                  pl.BlockSpec((ctx_len, RET_DK), lambda b, h, p, c: (b, ctx_k_blk + h)),
                  pl.BlockSpec((ctx_len, RET_DV), lambda b, h, p, c: (b, ctx_v_blk + h)),
                  pl.BlockSpec((1, 1, RET_DV), lambda b, h, p, c: (h, 0, 0))],
        out_specs=pl.BlockSpec((blk, RET_DV), lambda b, h, p, c: (rows_fwd(b, p, c), h)),
        scratch_shapes=[pltpu.VMEM((RET_DK, RET_DV), F32), pltpu.VMEM((RET_DK, RET_DV), F32),
                        pltpu.VMEM((seq, RET_DV), F32)],
        compiler_params=_params(("arbitrary",) * 4, 48),
        name="ret",
    )(lg, p_lat, p_lat, p_lat, p_lat, p_ctx, p_ctx, ret_norm_g.reshape(RET_HEADS, 1, RET_DV))


def _dattn_kernel(lam_ref, q_ref, ck_ref, cv_ref, k_ref, v_ref, gn_ref, gmat_ref, o_ref,
                  qq_scr, kmax_scr, mp_scr, kp_scr, vt_scr, pt_scr, *, tq, tk, n_kv, ta, unroll, ctx_len):
    rows = 2 * tq
    n_all = kp_scr.shape[0]
    qi = pl.program_id(2)

    def sq_norms(t):
        tf = t.astype(F32)
        return jnp.dot((tf * tf).astype(BF16), gmat_ref[...], preferred_element_type=F32)

    def lane_const(n, hot):
        return jnp.where(lax.broadcasted_iota(I32, (n, LANES), 1) < hot, 1.0, 0.0).astype(BF16)

    @pl.when(qi == 0)
    def _():
        kmax_scr[...] = jnp.max(sq_norms(ck_ref[...]), axis=0, keepdims=True)
        kp_scr[:, LANES:2 * LANES] = lane_const(n_all, 3)
        kp_scr[0:ctx_len, 0:LANES] = ck_ref[...]
        vt_scr[:, 0:ctx_len] = cv_ref[...].astype(F32).T.astype(BF16)

        prep = math.gcd(n_kv, 4)

        def body(c, carry):
            norms = []
            for u in range(prep):
                start = pl.multiple_of((c * prep + u) * tk, tk)
                k = k_ref[pl.ds(start, tk), :]
                norms.append(jnp.max(sq_norms(k), axis=0, keepdims=True))
                kp_scr[pl.ds(pl.multiple_of(ctx_len + start, LANES), tk), 0:LANES] = k
                vt_scr[:, pl.ds(pl.multiple_of(ctx_len + start, LANES), tk)] = (
                    v_ref[pl.ds(start, tk), :].astype(F32).T.astype(BF16))
            kmax_scr[...] = jnp.maximum(kmax_scr[...], functools.reduce(jnp.maximum, norms))
            return carry

        lax.fori_loop(0, n_kv // prep, body, 0)

    qt = q_ref[...].astype(F32).T
    row = lax.broadcasted_iota(I32, (LANES, tq), 0)
    q1t = jnp.where(row < DIFF_DH, qt, 0.0)
    q2t = jnp.where(row >= DIFF_DH, qt, 0.0)
    qq_scr[0:LANES, 0:tq] = q1t.astype(BF16)
    qq_scr[0:LANES, tq:rows] = q2t.astype(BF16)

    def set_shift(shift):
        neg = -shift
        hi = neg.astype(BF16).astype(F32)
        mid = (neg - hi).astype(BF16).astype(F32)
        lo = neg - hi - mid
        row_r = lax.broadcasted_iota(I32, (LANES, rows), 0)
        pieces = jnp.where(row_r == 0, hi, jnp.where(row_r == 1, mid, jnp.where(row_r == 2, lo, 0.0)))
        qq_scr[LANES:2 * LANES, :] = pieces.astype(BF16)

    kmax = kmax_scr[...]
    b1 = jnp.sqrt(jnp.sum(q1t * q1t, axis=0, keepdims=True) * kmax[:, 0:1]) * SHIFT_SLACK
    b2 = jnp.sqrt(jnp.sum(q2t * q2t, axis=0, keepdims=True) * kmax[:, DIFF_DH:DIFF_DH + 1]) * SHIFT_SLACK
    bound = jnp.concatenate([b1, b2], axis=1)
    set_shift(bound)

    def key_tile(t):
        return pl.ds(pl.multiple_of(t * ta, ta), ta)

    @pl.when(jnp.max(bound) > MAX_SAFE_SHIFT)
    def _():
        mp_scr[...] = jnp.full(mp_scr.shape, NEG_BIG, F32)

        def body(t, carry):
            st = jnp.dot(kp_scr[key_tile(t), 0:LANES], qq_scr[0:LANES, :], preferred_element_type=F32)
            mp_scr[...] = jnp.maximum(mp_scr[...], jnp.max(st, axis=0, keepdims=True))
            return carry

        lax.fori_loop(0, n_all // ta, body, 0)
        set_shift(mp_scr[0:1, :])

    mp_scr[...] = jnp.zeros(mp_scr.shape, F32)

    def body_a(t, carry):
        for u in range(unroll):
            r = key_tile(t * unroll + u)
            p = jnp.exp2(jnp.dot(kp_scr[r, :], qq_scr[...], preferred_element_type=F32))
            pt_scr[r, :] = p.astype(BF16)
            mp_scr[...] += jnp.sum(p.reshape(ta // SUBLANES, SUBLANES, rows), axis=0)
        return carry

    lax.fori_loop(0, n_all // (ta * unroll), body_a, 0)

    acc = jnp.dot(vt_scr[...], pt_scr[...], preferred_element_type=F32)
    ot = acc / jnp.sum(mp_scr[...], axis=0, keepdims=True)
    d = (ot[:, 0:tq] - lam_ref[0] * ot[:, tq:rows]).T
    ms = jnp.mean(d * d, axis=-1, keepdims=True)
    y = d * lax.rsqrt(ms + EPS) * gn_ref[...] * (1.0 - LAMBDA_INIT)
    o_ref[...] = y.astype(BF16)


def _diff_attention(lam, p_lat, p_ctx, diff_norm_g, gmat, batch, seq, ctx_len, ctx_k_blk, ctx_v_blk):
    tq = min(1024, seq)
    tk = min(512, seq)
    nq, nk = seq // tq, seq // tk
    n_all = ctx_len + seq
    ta = next(t for t in (528, 512, 384, 320, 256, 128) if n_all % t == 0)
    unroll = math.gcd(n_all // ta, KV_UNROLL)
    assert ctx_len % LANES == 0
    kern = functools.partial(_dattn_kernel, tq=tq, tk=tk, n_kv=nk, ta=ta, unroll=unroll, ctx_len=ctx_len)
    return pl.pallas_call(
        kern,
        out_shape=jax.ShapeDtypeStruct((batch * seq, DIFF_HEADS * DIFF_DV), BF16),
        grid=(batch, DIFF_HEADS, nq),
        in_specs=[pl.BlockSpec(memory_space=pltpu.SMEM),
                  pl.BlockSpec((tq, LANES), lambda b, h, qi: (b * nq + qi, 48 + h)),
                  pl.BlockSpec((ctx_len, LANES), lambda b, h, qi: (b, ctx_k_blk + h)),
                  pl.BlockSpec((ctx_len, LANES), lambda b, h, qi: (b, ctx_v_blk + h)),
                  pl.BlockSpec((seq, LANES), lambda b, h, qi: (b, 56 + h)),
                  pl.BlockSpec((seq, LANES), lambda b, h, qi: (b, 64 + h)),
                  pl.BlockSpec((1, LANES), lambda b, h, qi: (0, 0)),
                  pl.BlockSpec((LANES, LANES), lambda b, h, qi: (0, 0))],
        out_specs=pl.BlockSpec((tq, LANES), lambda b, h, qi: (b * nq + qi, h)),
        scratch_shapes=[pltpu.VMEM((2 * LANES, 2 * tq), BF16), pltpu.VMEM((1, LANES), F32),
                        pltpu.VMEM((SUBLANES, 2 * tq), F32), pltpu.VMEM((n_all, 2 * LANES), BF16),
                        pltpu.VMEM((DIFF_DV, n_all), BF16), pltpu.VMEM((n_all, 2 * tq), BF16)],
        compiler_params=_params(("arbitrary",) * 3, 58),
        name="dattn",
    )(lam, p_lat, p_ctx, p_ctx, p_lat, p_lat, diff_norm_g.reshape(1, DIFF_DV), gmat)


def _merge_kernel(yr_ref, yd_ref, ga_ref, gb_ref, x_ref, gatea_ref, shf_ref, scf_ref, n2_ref,
                  wr_ref, wd_ref, wo_ref, rwh_ref, rwl_ref, rb_ref,
                  xn_ref, hp_ref, meta_ref, gate_ref, cnt_ref, run_scr, *, tm):
    i = pl.program_id(0)

    @pl.when(i == 0)
    def _():
        run_scr[...] = jnp.zeros(run_scr.shape, F32)

    yr = jnp.dot(yr_ref[...], wr_ref[...], preferred_element_type=F32)
    yd = jnp.dot(yd_ref[...], wd_ref[...], preferred_element_type=F32)
    m = (jax.nn.sigmoid(ga_ref[...].astype(F32)) * yr + jax.nn.sigmoid(gb_ref[...].astype(F32)) * yd)
    z = jnp.dot(m.astype(BF16), wo_ref[...], preferred_element_type=F32)
    xn = x_ref[...] + gatea_ref[0] * z
    xn_ref[...] = xn

    ms = jnp.mean(xn * xn, axis=-1, keepdims=True)
    h2 = xn * lax.rsqrt(ms + EPS) * n2_ref[...]
    h2 = h2 * (1.0 + scf_ref[0]) + shf_ref[0]
    h_hi = h2.astype(BF16)
    bits = pltpu.bitcast(h_hi.astype(F32), U32)
    half = D_MODEL // 2
    hp_ref[...] = (bits[:, :half] >> 16) | (bits[:, half:] & jnp.uint32(0xFFFF0000))

    h_lo = (h2 - h_hi.astype(F32)).astype(BF16)
    logits = (jnp.dot(h_hi, rwh_ref[...], preferred_element_type=F32)
              + jnp.dot(h_lo, rwh_ref[...], preferred_element_type=F32)
              + jnp.dot(h_hi, rwl_ref[...], preferred_element_type=F32) + rb_ref[...])
    lane = lax.broadcasted_iota(I32, (tm, LANES), 1)
    lanef = lane.astype(F32)
    work = logits
    vals, idxs = [], []
    for _ in range(TOP_K):
        mk = jnp.max(work, axis=1, keepdims=True)
        ik = jnp.min(jnp.where(work == mk, lanef, float(LANES)), axis=1, keepdims=True)
        vals.append(mk)
        idxs.append(ik)
        work = jnp.where(lanef == ik, -jnp.inf, work)
    ex = [jnp.exp(v - vals[0]) for v in vals]
    den = ex[0] + ex[1] + ex[2] + ex[3]

    onehot = jnp.zeros((tm, LANES), F32)
    for ik in idxs:
        onehot = onehot + jnp.where(lanef == ik, 1.0, 0.0)
    ri = lax.broadcasted_iota(I32, (tm, tm), 0)
    ci = lax.broadcasted_iota(I32, (tm, tm), 1)
    tri = jnp.where(ri > ci, 1.0, 0.0).astype(BF16)
    base = run_scr[0:1, :] + jnp.dot(tri, onehot.astype(BF16), preferred_element_type=F32)
    run_scr[...] = run_scr[...] + jnp.sum(onehot, axis=0, keepdims=True)

    meta = jnp.zeros((tm, LANES), F32)
    gate_out = jnp.zeros((tm, LANES), F32)
    for k in range(TOP_K):
        rk = jnp.sum(jnp.where(lanef == idxs[k], base, 0.0), axis=1, keepdims=True)
        meta = jnp.where(lane == k, idxs[k], jnp.where(lane == TOP_K + k, rk, meta))
        gate_out = jnp.where(lane == k, ex[k] / den, gate_out)
    meta_ref[...] = meta.T[0:2 * TOP_K, :]
    gate_ref[...] = gate_out
    cnt_ref[...] = run_scr[...]


def _merge(y_ret, y_diff, p_lat, x2, g_a, sh_f, sc_f, norm2_g, w_r, w_d, w_o, rw_hi, rw_lo, rb, seq):
    n = x2.shape[0]
    tm = min(1024, seq)
    tpb = seq // tm
    kern = functools.partial(_merge_kernel, tm=tm)
    mod_spec = pl.BlockSpec((1, 1, D_MODEL), lambda i: (i // tpb, 0, 0))
    const = lambda shape: pl.BlockSpec(shape, lambda i: (0,) * len(shape))
    tok = lambda w: pl.BlockSpec((tm, w), lambda i: (i, 0))
    return pl.pallas_call(
        kern,
        out_shape=(jax.ShapeDtypeStruct((n, D_MODEL), F32),
                   jax.ShapeDtypeStruct((n, D_MODEL // 2), U32),
                   jax.ShapeDtypeStruct((2 * TOP_K, n), F32),
                   jax.ShapeDtypeStruct((n, LANES), F32),
                   jax.ShapeDtypeStruct((8, LANES), F32)),
        grid=(n // tm,),
        in_specs=[tok(RET_HEADS * RET_DV), tok(D_MODEL),
                  pl.BlockSpec((tm, COL_TILE), lambda i: (i, 9)),
                  pl.BlockSpec((tm, COL_TILE), lambda i: (i, 10)),
                  tok(D_MODEL), mod_spec, mod_spec, mod_spec, const((1, D_MODEL)),
                  const((RET_HEADS * RET_DV, D_MODEL)), const((D_MODEL, D_MODEL)), const((D_MODEL, D_MODEL)),
                  const((D_MODEL, LANES)), const((D_MODEL, LANES)), const((1, LANES))],
        out_specs=(tok(D_MODEL), tok(D_MODEL // 2), pl.BlockSpec((2 * TOP_K, tm), lambda i: (0, i)),
                   tok(LANES), const((8, LANES))),
        scratch_shapes=[pltpu.VMEM((8, LANES), F32)],
        compiler_params=_params(("arbitrary",), 60),
        name="merge",
    )(y_ret, y_diff, p_lat, p_lat, x2, g_a, sh_f, sc_f, norm2_g, w_r, w_d, w_o, rw_hi, rw_lo, rb)


def _dispatch_kernel(dest_ref, fill_ref, nb_ref, hp_ref, xb_ref, zero_scr, stage, sems, zsem, *,
                     tm, n_tok, n_blocks, n_steps):
    i = pl.program_id(0)

    @pl.when(i == 0)
    def _():
        zero_scr[...] = jnp.zeros(zero_scr.shape, U32)
        fill = zero_scr.shape[0]
        for e in range(N_EXPERTS):
            start = pl.multiple_of(fill_ref[e], SUBLANES)
            pltpu.make_async_copy(zero_scr, xb_ref.at[pl.ds(start, fill), :], zsem).start()
        for e in range(N_EXPERTS):
            pltpu.make_async_copy(zero_scr, xb_ref.at[pl.ds(0, fill), :], zsem).wait()
        blk = zero_scr.at[pl.ds(0, MOE_BLK), :]

        def start_blk(b, carry):
            pltpu.make_async_copy(blk, xb_ref.at[pl.ds(pl.multiple_of(b * MOE_BLK, MOE_BLK), MOE_BLK), :], zsem).start()
            return carry

        def wait_blk(b, carry):
            pltpu.make_async_copy(blk, xb_ref.at[pl.ds(0, MOE_BLK), :], zsem).wait()
            return carry

        lax.fori_loop(nb_ref[0], n_blocks, start_blk, 0)
        lax.fori_loop(nb_ref[0], n_blocks, wait_blk, 0)

    def drain(slot):
        for _ in range(TOP_K):
            pltpu.make_async_copy(stage.at[slot], xb_ref.at[pl.ds(0, tm), :], sems.at[slot]).wait()

    for slot in range(2):
        @pl.when(lax.rem(i, 2) == slot)
        def _():
            stage[slot] = hp_ref[...]

            def body(g, carry):
                r0 = pl.multiple_of(g * SUBLANES, SUBLANES)
                for j in range(SUBLANES):
                    for k in range(TOP_K):
                        d = dest_ref[k * n_tok + i * tm + r0 + j]
                        pltpu.make_async_copy(stage.at[slot, pl.ds(r0 + j, 1), :], xb_ref.at[pl.ds(d, 1), :],
                                              sems.at[slot]).start(priority=k % 2)
                return carry

            lax.fori_loop(0, tm // SUBLANES, body, 0)

            @pl.when(i > 0)
            def _():
                drain(1 - slot)

            @pl.when(i == n_steps - 1)
            def _():
                drain(slot)


def _dispatch(dest, fill_start, n_used, hp, n_rows):
    n = hp.shape[0]
    tm = min(512, n)
    kern = functools.partial(_dispatch_kernel, tm=tm, n_tok=n, n_blocks=n_rows // MOE_BLK, n_steps=n // tm)
    return pl.pallas_call(
        kern,
        out_shape=jax.ShapeDtypeStruct((n_rows, D_MODEL // 2), U32),
        grid_spec=pltpu.PrefetchScalarGridSpec(
            num_scalar_prefetch=3,
            grid=(n // tm,),
            in_specs=[pl.BlockSpec((tm, D_MODEL // 2), lambda i, d, f, nb: (i, 0))],
            out_specs=pl.BlockSpec(memory_space=pl.ANY),
            scratch_shapes=[pltpu.VMEM((MOE_BLK + SUBLANES, D_MODEL // 2), U32),
                            pltpu.VMEM((2, tm, D_MODEL // 2), U32),
                            pltpu.SemaphoreType.DMA((2,)), pltpu.SemaphoreType.DMA(())]),
        compiler_params=_params(("arbitrary",), 32),
        name="dispatch",
    )(dest, fill_start, n_used, hp)


def _expert_kernel(be_ref, nb_ref, nxt_ref, xb_ref, w1_hbm, b1_ref, w2_hbm, b2_ref, y_ref,
                   w1f_scr, w2f_scr, w1b_scr, w2b_scr, x_scr, grp_scr, sems):
    def weight_copies(expert, slot):
        return (pltpu.make_async_copy(w1_hbm.at[expert], w1f_scr.at[slot], sems.at[0, slot]),
                pltpu.make_async_copy(w2_hbm.at[expert], w2f_scr.at[slot], sems.at[1, slot]))

    def one_block(i, rows):
        e = be_ref[i]
        prev = be_ref[jnp.maximum(i - 1, 0)]
        active = i < nb_ref[0]

        @pl.when(i == 0)
        def _():
            grp_scr[0] = 0
            for cp in weight_copies(e, 0):
                cp.start()

        @pl.when(active & (i > 0) & (e != prev))
        def _():
            grp_scr[0] = grp_scr[0] + 1

        @pl.when(active & ((i == 0) | (e != prev)))
        def _():
            slot = grp_scr[0] % 2
            for cp in weight_copies(e, slot):
                cp.wait()
            w1b_scr[...] = w1f_scr[slot].astype(BF16)
            w2b_scr[...] = w2f_scr[slot].astype(BF16)
            nxt = nxt_ref[e]

            @pl.when(nxt >= 0)
            def _():
                for cp in weight_copies(nxt, 1 - slot):
                    cp.start()

        @pl.when(active)
        def _():
            xu = xb_ref[rows, :]
            half = D_MODEL // 2
            x_scr[:, 0:half] = pltpu.bitcast(xu << 16, F32).astype(BF16)
            x_scr[:, half:] = pltpu.bitcast(xu & jnp.uint32(0xFFFF0000), F32).astype(BF16)
            hh = jnp.dot(x_scr[...], w1b_scr[...], preferred_element_type=F32) + b1_ref[e]
            glu = jnp.minimum(hh[:, :D_FF], SWIGLU_LIMIT)
            lin = jnp.clip(hh[:, D_FF:], -SWIGLU_LIMIT, SWIGLU_LIMIT)
            act = glu * jax.nn.sigmoid(SWIGLU_ALPHA * glu) * (lin + 1.0)
            y_ref[rows, :] = jnp.dot(act.astype(BF16), w2b_scr[...], preferred_element_type=F32) + b2_ref[e]

        @pl.when(i >= nb_ref[0])
        def _():
            y_ref[rows, :] = jnp.zeros((MOE_BLK, D_MODEL), F32)

    for part in range(EXPERT_PAIR):
        one_block(pl.program_id(0) * EXPERT_PAIR + part, pl.ds(part * MOE_BLK, MOE_BLK))


def _experts(block_e, n_used, next_e, xb, w1, b1, w2, b2):
    n_blocks = block_e.shape[0]
    assert n_blocks % EXPERT_PAIR == 0
    rows = n_blocks * MOE_BLK
    step_rows = EXPERT_PAIR * MOE_BLK
    return pl.pallas_call(
        _expert_kernel,
        out_shape=jax.ShapeDtypeStruct((rows, D_MODEL), F32),
        grid_spec=pltpu.PrefetchScalarGridSpec(
            num_scalar_prefetch=3,
            grid=(n_blocks // EXPERT_PAIR,),
            in_specs=[pl.BlockSpec((step_rows, D_MODEL // 2),
                                   lambda i, be, nb, nx: (jnp.minimum(i, (nb[0] - 1) // EXPERT_PAIR), 0)),
                      pl.BlockSpec(memory_space=pl.ANY),
                      pl.BlockSpec((N_EXPERTS, 1, 2 * D_FF), lambda i, be, nb, nx: (0, 0, 0)),
                      pl.BlockSpec(memory_space=pl.ANY),
                      pl.BlockSpec((N_EXPERTS, 1, D_MODEL), lambda i, be, nb, nx: (0, 0, 0))],
            out_specs=pl.BlockSpec((step_rows, D_MODEL), lambda i, be, nb, nx: (i, 0)),
            scratch_shapes=[pltpu.VMEM((2, D_MODEL, 2 * D_FF), F32), pltpu.VMEM((2, D_FF, D_MODEL), F32),
                            pltpu.VMEM((D_MODEL, 2 * D_FF), BF16), pltpu.VMEM((D_FF, D_MODEL), BF16),
                            pltpu.VMEM((MOE_BLK, D_MODEL), BF16),
                            pltpu.SMEM((1,), I32), pltpu.SemaphoreType.DMA((2, 2))]),
        compiler_params=_params(("arbitrary",), 56),
        name="expert",
    )(block_e, n_used, next_e, xb, w1, b1.reshape(N_EXPERTS, 1, 2 * D_FF), w2, b2.reshape(N_EXPERTS, 1, D_MODEL))


def _combine_kernel(dest_ref, yb_ref, gate_ref, xn_ref, gf_ref, o_ref, buf, sems, *, tm, n_tok, n_steps):
    i = pl.program_id(0)

    def issue(tile, slot):
        def body(g, carry):
            r0 = pl.multiple_of(g * SUBLANES, SUBLANES)
            for j in range(SUBLANES):
                for k in range(TOP_K):
                    d = dest_ref[k * n_tok + tile * tm + r0 + j]
                    pltpu.make_async_copy(yb_ref.at[pl.ds(d, 1), :], buf.at[slot, k, pl.ds(r0 + j, 1), :],
                                          sems.at[slot]).start(priority=k % 2)
            return carry

        lax.fori_loop(0, tm // SUBLANES, body, 0)

    def finish(slot):
        for k in range(TOP_K):
            pltpu.make_async_copy(yb_ref.at[pl.ds(0, tm), :], buf.at[slot, k], sems.at[slot]).wait()
        g = gate_ref[...]
        y = g[:, 0:1] * buf[slot, 0]
        for k in range(1, TOP_K):
            y = y + g[:, k:k + 1] * buf[slot, k]
        o_ref[...] = xn_ref[...] + gf_ref[0] * y

    @pl.when(i == 0)
    def _():
        issue(0, 0)

    for slot in range(2):
        @pl.when(lax.rem(i, 2) == slot)
        def _():
            @pl.when(i + 1 < n_steps)
            def _():
                issue(i + 1, 1 - slot)

            finish(slot)


def _combine(dest, yb, gate, xn, g_f, seq):
    n = xn.shape[0]
    tm = min(512, seq)
    tpb = seq // tm
    kern = functools.partial(_combine_kernel, tm=tm, n_tok=n, n_steps=n // tm)
    return pl.pallas_call(
        kern,
        out_shape=jax.ShapeDtypeStruct((n, D_MODEL), F32),
        grid_spec=pltpu.PrefetchScalarGridSpec(
            num_scalar_prefetch=1,
            grid=(n // tm,),
            in_specs=[pl.BlockSpec(memory_space=pl.ANY),
                      pl.BlockSpec((tm, LANES), lambda i, d: (i, 0)),
                      pl.BlockSpec((tm, D_MODEL), lambda i, d: (i, 0)),
                      pl.BlockSpec((1, 1, D_MODEL), lambda i, d: (i // tpb, 0, 0))],
            out_specs=pl.BlockSpec((tm, D_MODEL), lambda i, d: (i, 0)),
            scratch_shapes=[pltpu.VMEM((2, TOP_K, tm, D_MODEL), F32), pltpu.SemaphoreType.DMA((2,))]),
        compiler_params=_params(("arbitrary",), 40),
        name="combine",
    )(dest, yb, gate, xn, g_f)


def _layer(x, ctx, c, c_ctx, norm1_g, norm2_g, w_mod, b_mod, w_in, ret_decay_logit, ret_norm_g,
           diff_q_norm_g, diff_k_norm_g, diff_lambda, diff_norm_g, w_br_ret, w_br_diff, w_out,
           router_w, router_b, exp_w1, exp_b1, exp_w2, exp_b2):
    batch, seq, d = x.shape
    ctx_len = ctx.shape[1]
    assert d == D_MODEL and seq % GRID_W == 0 and batch + 1 <= 8
    n_tok = batch * seq

    cc = jnp.zeros((8, D_MODEL), F32).at[:batch].set(c).at[batch].set(c_ctx)
    mod = _mod(cc, w_mod, b_mod)
    sh_a, sc_a, g_a, sh_f, sc_f, g_f = [mod[:batch, i * D_MODEL:(i + 1) * D_MODEL].reshape(batch, 1, D_MODEL)
                                         for i in range(6)]
    csh_a = mod[batch:batch + 1, 0:D_MODEL].reshape(1, 1, D_MODEL)
    csc_a = mod[batch:batch + 1, D_MODEL:2 * D_MODEL].reshape(1, 1, D_MODEL)

    w_in_bf = w_in.astype(BF16)
    g1 = norm1_g.reshape(1, D_MODEL)
    tile = lambda g: jnp.tile(g.astype(F32), 2)
    qkg = jnp.zeros((8, LANES), F32).at[0].set(tile(diff_q_norm_g) * (DIFF_DH ** -0.5 * LOG2E)).at[1].set(tile(diff_k_norm_g))
    lane = jnp.arange(2 * LANES)
    same_group = lane[:, None] // DIFF_DH == lane[None, :] // DIFF_DH
    gmean = jnp.where(same_group, 1.0 / DIFF_DH, 0.0).astype(BF16)
    gmat = same_group[:LANES, :LANES].astype(BF16)
    x2 = x.reshape(n_tok, D_MODEL)
    tm = min(1024, seq)
    p_lat = _inproj(x2, g1, sh_a, sc_a, w_in_bf, qkg, gmean, _rope_tables(seq), tm, seq // tm,
                    tuple(range(IN_COLS // COL_TILE)))
    p_ctx = _inproj(ctx.reshape(batch * ctx_len, D_MODEL), g1, csh_a, csc_a, w_in_bf, qkg, gmean,
                    _identity_tables(ctx_len), ctx_len, 1, CTX_COLS)

    lg = jax.nn.log_sigmoid(ret_decay_logit.astype(F32))
    y_ret = _retention(lg, p_lat, p_ctx, ret_norm_g, batch, seq, ctx_len,
                       CTX_COLS.index(1) * COL_TILE // RET_DK, CTX_COLS.index(2) * COL_TILE // RET_DV)

    lp = diff_lambda.astype(F32)
    lam = (jnp.exp(jnp.sum(lp[0] * lp[1])) - jnp.exp(jnp.sum(lp[2] * lp[3])) + LAMBDA_INIT).reshape(1)
    y_diff = _diff_attention(lam, p_lat, p_ctx, diff_norm_g, gmat, batch, seq, ctx_len,
                             CTX_COLS.index(7) * COL_TILE // LANES, CTX_COLS.index(8) * COL_TILE // LANES)

    rw = jnp.zeros((D_MODEL, LANES), F32).at[:, :N_EXPERTS].set(router_w)
    rw_hi = rw.astype(BF16)
    rw_lo = (rw - rw_hi.astype(F32)).astype(BF16)
    rb = jnp.full((1, LANES), NEG_BIG, F32).at[0, :N_EXPERTS].set(router_b)
    xn, hp, meta, gate4, cnt = _merge(
        y_ret, y_diff, p_lat, x2, g_a, sh_f, sc_f, norm2_g.reshape(1, D_MODEL),
        w_br_ret.astype(BF16), w_br_diff.astype(BF16), w_out.astype(BF16), rw_hi, rw_lo, rb, seq)

    counts = cnt[0, :N_EXPERTS].astype(I32)
    padded = (counts + MOE_BLK - 1) // MOE_BLK * MOE_BLK
    pad_end = jnp.cumsum(padded)
    pad_start = pad_end - padded
    n_pairs = n_tok * TOP_K
    n_blocks = n_pairs // MOE_BLK + N_EXPERTS
    meta_i = meta.astype(I32)
    is_e = meta_i[None, :TOP_K] == jnp.arange(N_EXPERTS, dtype=I32)[:, None, None]
    dest = (jnp.sum(jnp.where(is_e, pad_start[:, None, None], 0), axis=0) + meta_i[TOP_K:]).reshape(n_pairs)
    block_start = jnp.arange(n_blocks, dtype=I32) * MOE_BLK
    block_e = jnp.minimum(jnp.sum((pad_end[None, :] <= block_start[:, None]).astype(I32), axis=1), N_EXPERTS - 1)
    n_used = (pad_end[-1] // MOE_BLK).reshape(1).astype(I32)

    fill_start = ((pad_start + counts) // SUBLANES * SUBLANES).astype(I32)
    xb = _dispatch(dest, fill_start, n_used, hp, (n_blocks + 1) * MOE_BLK)
    e_ids = jnp.arange(N_EXPERTS, dtype=I32)
    later = jnp.where((counts[None, :] > 0) & (e_ids[None, :] > e_ids[:, None]), e_ids[None, :], N_EXPERTS)
    next_e = jnp.min(later, axis=1)
    next_e = jnp.where(next_e == N_EXPERTS, -1, next_e).astype(I32)
    yb = _experts(block_e, n_used, next_e, xb, exp_w1, exp_b1, exp_w2, exp_b2)
    out = _combine(dest, yb, gate4, xn, g_f, seq)
    return out.reshape(batch, seq, D_MODEL)


def kernel(x, c, ctx, c_ctx, norm1_g, norm2_g, w_mod, b_mod, w_in, ret_decay_logit, ret_norm_g, diff_q_norm_g, diff_k_norm_g, diff_lambda, diff_norm_g, w_br_ret, w_br_diff, w_out, router_w, router_b, exp_w1, exp_b1, exp_w2, exp_b2):
    assert norm1_g.shape[0] == 1, "single-layer block"
    return _layer(x, ctx, c, c_ctx, norm1_g[0], norm2_g[0], w_mod[0], b_mod[0], w_in[0], ret_decay_logit[0],
                  ret_norm_g[0], diff_q_norm_g[0], diff_k_norm_g[0], diff_lambda[0], diff_norm_g[0],
                  w_br_ret[0], w_br_diff[0], w_out[0], router_w[0], router_b[0],
                  exp_w1[0], exp_b1[0], exp_w2[0], exp_b2[0])
```

```python
import functools
import math

import numpy as np

import jax
import jax.numpy as jnp
from jax import lax
from jax.experimental import pallas as pl
from jax.experimental.pallas import tpu as pltpu

F32 = jnp.float32
BF16 = jnp.bfloat16
U32 = jnp.uint32
I32 = jnp.int32

D_MODEL = 1024
GRID_W = 64
RET_HEADS = 4
RET_DK = 256
RET_DV = 512
DIFF_DH = 64
DIFF_HEADS = 8
DIFF_DV = 128
N_EXPERTS = 32
TOP_K = 4
D_FF = 1024
SWIGLU_LIMIT = 7.0
SWIGLU_ALPHA = 1.702
ROPE_BASE = 10000.0
EPS = 1e-6
LAMBDA_INIT = 0.8 - 0.6 * math.exp(-0.3 * 0)

IN_COLS = 11264
COL_TILE = 1024
CTX_COLS = (1, 2, 3, 7, 8)
LANES = 128
SUBLANES = 8
MOE_BLK = 256
EXPERT_PAIR = 4
NEG_BIG = -1e30
LOG2E = 1.4426950408889634
SHIFT_SLACK = 1.0 + 2.0 ** -6
MAX_SAFE_SHIFT = 60.0
RET_SUB = 8
KV_UNROLL = 8
HIGHEST = lax.Precision.HIGHEST
MIB = 1024 * 1024


def _params(sem, vmem_mib):
    return pltpu.CompilerParams(dimension_semantics=sem, vmem_limit_bytes=vmem_mib * MIB)


def _mod_kernel(c_ref, w_ref, b_ref, o_ref):
    c = c_ref[...]
    s = c * jax.nn.sigmoid(c)
    o_ref[...] = jnp.dot(s, w_ref[...], preferred_element_type=F32, precision=HIGHEST) + b_ref[...]


def _mod(cc, w_mod, b_mod):
    n = w_mod.shape[1]
    tn = 1024
    return pl.pallas_call(
        _mod_kernel,
        out_shape=jax.ShapeDtypeStruct((8, n), F32),
        grid=(n // tn,),
        in_specs=[pl.BlockSpec((8, D_MODEL), lambda j: (0, 0)),
                  pl.BlockSpec((D_MODEL, tn), lambda j: (0, j)),
                  pl.BlockSpec((1, tn), lambda j: (0, j))],
        out_specs=pl.BlockSpec((8, tn), lambda j: (0, j)),
        compiler_params=_params(("arbitrary",), 32),
        name="mod",
    )(cc, w_mod, b_mod.reshape(1, n))


def _inproj_kernel(cols_ref, x_ref, g_ref, sh_ref, sc_ref, w_ref, qkg_ref, gmat_ref,
                   cr_ref, sr_ref, cc_ref, sc2_ref, cd_ref, sa_ref, sb_ref,
                   o_ref, h_scr, acc_scr):
    j = pl.program_id(1)
    jc = cols_ref[j]

    @pl.when(j == 0)
    def _():
        xf = x_ref[...]
        ms = jnp.mean(xf * xf, axis=-1, keepdims=True)
        y = xf * lax.rsqrt(ms + EPS) * g_ref[...]
        h_scr[...] = (y * (1.0 + sc_ref[0]) + sh_ref[0]).astype(BF16)

    is_ret = jc <= 1
    is_dqk = (jc == 6) | (jc == 7)

    def project():
        return jnp.dot(h_scr[...], w_ref[...], preferred_element_type=F32)

    @pl.when(is_ret)
    def _():
        acc_scr[...] = project()
        scale = jnp.where(jc == 0, RET_DK ** -0.5, 1.0).astype(F32)
        for b in range(COL_TILE // LANES):
            xb = acc_scr[:, b * LANES:(b + 1) * LANES]
            cos = cr_ref[...] if b % 2 == 0 else cc_ref[...]
            sin = sr_ref[...] if b % 2 == 0 else sc2_ref[...]
            o = (xb * cos + pltpu.roll(xb, 64, 1) * sin) * scale
            o_ref[:, b * LANES:(b + 1) * LANES] = o.astype(BF16)

    @pl.when(is_dqk)
    def _():
        acc_scr[...] = project()
        g = jnp.where(jc == 6, qkg_ref[0:1, :], qkg_ref[1:2, :])
        for b2 in range(COL_TILE // (2 * LANES)):
            x2 = acc_scr[:, b2 * 2 * LANES:(b2 + 1) * 2 * LANES]
            ms2 = jnp.dot((x2 * x2).astype(BF16), gmat_ref[...], preferred_element_type=F32)
            for half in range(2):
                b = 2 * b2 + half
                xb = x2[:, half * LANES:(half + 1) * LANES]
                yn = xb * lax.rsqrt(ms2[:, half * LANES:(half + 1) * LANES] + EPS) * g
                o = yn * cd_ref[...] + pltpu.roll(yn, 16, 1) * sa_ref[...] + pltpu.roll(yn, 112, 1) * sb_ref[...]
                o_ref[:, b * LANES:(b + 1) * LANES] = o.astype(BF16)

    @pl.when(jnp.logical_not(is_ret | is_dqk))
    def _():
        o_ref[...] = project().astype(BF16)


def _inproj(x2, g1, sh, sc, w_bf, qkg, gmat, tables, tm, tiles_per_batch, cols):
    n = x2.shape[0]
    nb = sh.shape[0]
    tab_spec = pl.BlockSpec((tm, LANES), lambda i, j, c: (i % tiles_per_batch, 0))
    mod_spec = pl.BlockSpec((1, 1, D_MODEL), lambda i, j, c: (jnp.minimum(i // tiles_per_batch, nb - 1), 0, 0))
    return pl.pallas_call(
        _inproj_kernel,
        out_shape=jax.ShapeDtypeStruct((n, len(cols) * COL_TILE), BF16),
        grid_spec=pltpu.PrefetchScalarGridSpec(
            num_scalar_prefetch=1,
            grid=(n // tm, len(cols)),
            in_specs=[pl.BlockSpec((tm, D_MODEL), lambda i, j, c: (i, 0)),
                      pl.BlockSpec((1, D_MODEL), lambda i, j, c: (0, 0)),
                      mod_spec, mod_spec,
                      pl.BlockSpec((D_MODEL, COL_TILE), lambda i, j, c: (0, c[j])),
                      pl.BlockSpec((8, LANES), lambda i, j, c: (0, 0)),
                      pl.BlockSpec((2 * LANES, 2 * LANES), lambda i, j, c: (0, 0))] + [tab_spec] * 7,
            out_specs=pl.BlockSpec((tm, COL_TILE), lambda i, j, c: (i, j)),
            scratch_shapes=[pltpu.VMEM((tm, D_MODEL), BF16), pltpu.VMEM((tm, COL_TILE), F32)]),
        compiler_params=_params(("arbitrary", "arbitrary"), 48),
        name="inproj",
    )(jnp.asarray(cols, I32), x2, g1, sh, sc, w_bf, qkg, gmat, *tables)


def _rope_tables(seq):
    n_rows = seq // GRID_W
    f32 = np.float32

    def angles(pos, half):
        inv = f32(ROPE_BASE) ** (-np.arange(half, dtype=f32) / f32(half))
        return (pos.astype(f32)[:, None] * inv[None, :]).astype(np.float64)

    ar, ac = angles(np.arange(n_rows), 64), angles(np.arange(GRID_W), 64)
    br, bc = angles(np.arange(n_rows), 16), angles(np.arange(GRID_W), 16)
    zr, zc = np.zeros_like(br), np.zeros_like(bc)
    cat = lambda parts, reps=1: np.tile(np.concatenate(parts, axis=1), (1, reps)).astype(f32)
    by_row = lambda t: jnp.repeat(jnp.asarray(t), GRID_W, axis=0)
    by_col = lambda t: jnp.tile(jnp.asarray(t), (n_rows, 1))
    cr = by_row(cat([np.cos(ar), np.cos(ar)]))
    sr = by_row(cat([-np.sin(ar), np.sin(ar)]))
    cc = by_col(cat([np.cos(ac), np.cos(ac)]))
    sc = by_col(cat([-np.sin(ac), np.sin(ac)]))
    cd = by_row(cat([np.cos(br), np.cos(br), zr, zr], 2)) + by_col(cat([zc, zc, np.cos(bc), np.cos(bc)], 2))
    sa = by_row(cat([zr, np.sin(br), zr, zr], 2)) + by_col(cat([zc, zc, zc, np.sin(bc)], 2))
    sb = by_row(cat([-np.sin(br), zr, zr, zr], 2)) + by_col(cat([zc, zc, -np.sin(bc), zc], 2))
    return [cr, sr, cc, sc, cd, sa, sb]


def _identity_tables(seq):
    one = jnp.ones((seq, LANES), F32)
    zero = jnp.zeros((seq, LANES), F32)
    return [one, zero, one, zero, one, zero, zero]


def _tn_dot(a, b):
    return lax.dot_general(a, b, (((0,), (0,)), ((), ())), preferred_element_type=F32)


def _nt_dot(a, b):
    return lax.dot_general(a, b, (((1,), (1,)), ((), ())), preferred_element_type=F32)


def _ret_kernel(lg_ref, q_ref, k_ref, v_ref, g_ref, ck_ref, cv_ref, gn_ref, o_ref,
                sf_scr, sb_scr, ob_scr, *, chunk, sub, n_blocks, ctx_len):
    h = pl.program_id(1)
    p = pl.program_id(2)
    c = pl.program_id(3)
    lgf = lg_ref[0, h]
    lgb = lg_ref[1, h]

    def col_iota(n):
        return lax.broadcasted_iota(I32, (n, 1), 0).astype(F32)

    def vexp(s):
        return jnp.exp(jnp.zeros((1, 1), F32) + s)

    @pl.when((p == 0) & (c == 0))
    def _():
        jc = col_iota(ctx_len)
        kc = ck_ref[...].astype(F32)
        vc = cv_ref[...]
        sf_scr[...] = _tn_dot((kc * jnp.exp(lgf * (ctx_len - 1.0 - jc))).astype(BF16), vc)
        sb_scr[...] = _tn_dot((kc * jnp.exp(lgb * jc)).astype(BF16), vc)

    ic = col_iota(chunk)

    def full_bf16(col):
        return jnp.broadcast_to(col, (chunk, RET_DK)).astype(BF16)

    @pl.when(p == 0)
    def _():
        q_decay = full_bf16(jnp.exp(lgb * (chunk - ic)))
        k_decay = full_bf16(jnp.exp(lgb * ic))
        s_decay = vexp(lgb * chunk)
        for j in reversed(range(sub)):
            loc = pl.ds(j * chunk, chunk)
            glob = pl.ds(pl.multiple_of(((n_blocks - 1 - c) * sub + j) * chunk, chunk), chunk)
            qb = q_ref[loc, :] * q_decay
            ob_scr[glob, :] = jnp.dot(qb, sb_scr[...].astype(BF16), preferred_element_type=F32)
            kb = k_ref[loc, :] * k_decay
            sb_scr[...] = s_decay * sb_scr[...] + _tn_dot(kb, v_ref[loc, :])

    @pl.when(p == 1)
    def _():
        ri = lax.broadcasted_iota(I32, (chunk, chunk), 0)
        ci = lax.broadcasted_iota(I32, (chunk, chunk), 1)
        d = (ri - ci).astype(F32)
        mask = jnp.where(d > 0, jnp.exp(lgf * jnp.maximum(d, 0.0)),
                         jnp.where(d < 0, jnp.exp(lgb * jnp.maximum(-d, 0.0)), 2.0))
        q_decay = full_bf16(jnp.exp(lgf * (ic + 1.0)))
        k_decay = full_bf16(jnp.exp(lgf * (chunk - 1.0 - ic)))
        s_decay = vexp(lgf * chunk)
        for j in range(sub):
            loc = pl.ds(j * chunk, chunk)
            glob = pl.ds(pl.multiple_of((c * sub + j) * chunk, chunk), chunk)
            a = (_nt_dot(q_ref[loc, :], k_ref[loc, :]) * mask).astype(BF16)
            qf = q_ref[loc, :] * q_decay
            o = (jnp.dot(a, v_ref[loc, :], preferred_element_type=F32)
                 + jnp.dot(qf, sf_scr[...].astype(BF16), preferred_element_type=F32)
                 + ob_scr[glob, :])
            kf = k_ref[loc, :] * k_decay
            sf_scr[...] = s_decay * sf_scr[...] + _tn_dot(kf, v_ref[loc, :])
            ms = jnp.mean(o * o, axis=-1, keepdims=True)
            y = o * lax.rsqrt(ms + EPS) * gn_ref[0]
            gt = g_ref[loc, :]
            o_ref[loc, :] = y.astype(BF16) * (gt * jax.nn.sigmoid(gt))


def _retention(lg, p_lat, p_ctx, ret_norm_g, batch, seq, ctx_len, ctx_k_blk, ctx_v_blk):
    chunk = min(256, seq)
    sub = math.gcd(seq // chunk, RET_SUB)
    blk = chunk * sub
    nb = seq // blk
    kern = functools.partial(_ret_kernel, chunk=chunk, sub=sub, n_blocks=nb, ctx_len=ctx_len)

    def rows(b, p, c):
        return b * nb + jnp.where(p == 0, nb - 1 - c, c)

    def rows_fwd(b, p, c):
        return b * nb + jnp.where(p == 0, 0, c)

    return pl.pallas_call(
        kern,
        out_shape=jax.ShapeDtypeStruct((batch * seq, RET_HEADS * RET_DV), BF16),
        grid=(batch, RET_HEADS, 2, nb),
        in_specs=[pl.BlockSpec(memory_space=pltpu.SMEM),
                  pl.BlockSpec((blk, RET_DK), lambda b, h, p, c: (rows(b, p, c), h)),
                  pl.BlockSpec((blk, RET_DK), lambda b, h, p, c: (rows(b, p, c), 4 + h)),
                  pl.BlockSpec((blk, RET_DV), lambda b, h, p, c: (rows(b, p, c), 4 + h)),
                  pl.BlockSpec((blk, RET_DV), lambda b, h, p, c: (rows_fwd(b, p, c), 8 + h)),
                  pl.BlockSpec((ctx_len, RET_DK), lambda b, h, p, c: (b, ctx_k_blk + h)),
                  pl.BlockSpec((ctx_len, RET_DV), lambda b, h, p, c: (b, ctx_v_blk + h)),
                  pl.BlockSpec((1, 1, RET_DV), lambda b, h, p, c: (h, 0, 0))],
        out_specs=pl.BlockSpec((blk, RET_DV), lambda b, h, p, c: (rows_fwd(b, p, c), h)),
        scratch_shapes=[pltpu.VMEM((RET_DK, RET_DV), F32), pltpu.VMEM((RET_DK, RET_DV), F32),
                        pltpu.VMEM((seq, RET_DV), F32)],
        compiler_params=_params(("arbitrary",) * 4, 48),
        name="ret",
    )(lg, p_lat, p_lat, p_lat, p_lat, p_ctx, p_ctx, ret_norm_g.reshape(RET_HEADS, 1, RET_DV))


def _dattn_kernel(lam_ref, q_ref, ck_ref, cv_ref, k_ref, v_ref, gn_ref, gmat_ref, o_ref,
                  qq_scr, kmax_scr, mp_scr, kp_scr, vt_scr, pt_scr, *, tq, tk, n_kv, ta, unroll, ctx_len):
    rows = 2 * tq
    n_all = kp_scr.shape[0]
    qi = pl.program_id(2)

    def sq_norms(t):
        tf = t.astype(F32)
        return jnp.dot((tf * tf).astype(BF16), gmat_ref[...], preferred_element_type=F32)

    def lane_const(n, hot):
        return jnp.where(lax.broadcasted_iota(I32, (n, LANES), 1) < hot, 1.0, 0.0).astype(BF16)

    @pl.when(qi == 0)
    def _():
        kmax_scr[...] = jnp.max(sq_norms(ck_ref[...]), axis=0, keepdims=True)
        kp_scr[:, LANES:2 * LANES] = lane_const(n_all, 3)
        kp_scr[0:ctx_len, 0:LANES] = ck_ref[...]
        vt_scr[:, 0:ctx_len] = cv_ref[...].astype(F32).T.astype(BF16)

        prep = math.gcd(n_kv, 4)

        def body(c, carry):
            norms = []
            for u in range(prep):
                start = pl.multiple_of((c * prep + u) * tk, tk)
                k = k_ref[pl.ds(start, tk), :]
                norms.append(jnp.max(sq_norms(k), axis=0, keepdims=True))
                kp_scr[pl.ds(pl.multiple_of(ctx_len + start, LANES), tk), 0:LANES] = k
                vt_scr[:, pl.ds(pl.multiple_of(ctx_len + start, LANES), tk)] = (
                    v_ref[pl.ds(start, tk), :].astype(F32).T.astype(BF16))
            kmax_scr[...] = jnp.maximum(kmax_scr[...], functools.reduce(jnp.maximum, norms))
            return carry

        lax.fori_loop(0, n_kv // prep, body, 0)

    qt = q_ref[...].astype(F32).T
    row = lax.broadcasted_iota(I32, (LANES, tq), 0)
    q1t = jnp.where(row < DIFF_DH, qt, 0.0)
    q2t = jnp.where(row >= DIFF_DH, qt, 0.0)
    qq_scr[0:LANES, 0:tq] = q1t.astype(BF16)
    qq_scr[0:LANES, tq:rows] = q2t.astype(BF16)

    def set_shift(shift):
        neg = -shift
        hi = neg.astype(BF16).astype(F32)
        mid = (neg - hi).astype(BF16).astype(F32)
        lo = neg - hi - mid
        row_r = lax.broadcasted_iota(I32, (LANES, rows), 0)
        pieces = jnp.where(row_r == 0, hi, jnp.where(row_r == 1, mid, jnp.where(row_r == 2, lo, 0.0)))
        qq_scr[LANES:2 * LANES, :] = pieces.astype(BF16)

    kmax = kmax_scr[...]
    b1 = jnp.sqrt(jnp.sum(q1t * q1t, axis=0, keepdims=True) * kmax[:, 0:1]) * SHIFT_SLACK
    b2 = jnp.sqrt(jnp.sum(q2t * q2t, axis=0, keepdims=True) * kmax[:, DIFF_DH:DIFF_DH + 1]) * SHIFT_SLACK
    bound = jnp.concatenate([b1, b2], axis=1)
    set_shift(bound)

    def key_tile(t):
        return pl.ds(pl.multiple_of(t * ta, ta), ta)

    @pl.when(jnp.max(bound) > MAX_SAFE_SHIFT)
    def _():
        mp_scr[...] = jnp.full(mp_scr.shape, NEG_BIG, F32)

        def body(t, carry):
            st = jnp.dot(kp_scr[key_tile(t), 0:LANES], qq_scr[0:LANES, :], preferred_element_type=F32)
            mp_scr[...] = jnp.maximum(mp_scr[...], jnp.max(st, axis=0, keepdims=True))
            return carry

        lax.fori_loop(0, n_all // ta, body, 0)
        set_shift(mp_scr[0:1, :])

    mp_scr[...] = jnp.zeros(mp_scr.shape, F32)

    def body_a(t, carry):
        for u in range(unroll):
            r = key_tile(t * unroll + u)
            p = jnp.exp2(jnp.dot(kp_scr[r, :], qq_scr[...], preferred_element_type=F32))
            pt_scr[r, :] = p.astype(BF16)
            mp_scr[...] += jnp.sum(p.reshape(ta // SUBLANES, SUBLANES, rows), axis=0)
        return carry

    lax.fori_loop(0, n_all // (ta * unroll), body_a, 0)

    acc = jnp.dot(vt_scr[...], pt_scr[...], preferred_element_type=F32)
    ot = acc / jnp.sum(mp_scr[...], axis=0, keepdims=True)
    d = (ot[:, 0:tq] - lam_ref[0] * ot[:, tq:rows]).T
    ms = jnp.mean(d * d, axis=-1, keepdims=True)
    y = d * lax.rsqrt(ms + EPS) * gn_ref[...] * (1.0 - LAMBDA_INIT)
    o_ref[...] = y.astype(BF16)


def _diff_attention(lam, p_lat, p_ctx, diff_norm_g, gmat, batch, seq, ctx_len, ctx_k_blk, ctx_v_blk):
    tq = min(1024, seq)
    tk = min(512, seq)
    nq, nk = seq // tq, seq // tk
    n_all = ctx_len + seq
    ta = next(t for t in (528, 512, 384, 320, 256, 128) if n_all % t == 0)
    unroll = math.gcd(n_all // ta, KV_UNROLL)
    assert ctx_len % LANES == 0
    kern = functools.partial(_dattn_kernel, tq=tq, tk=tk, n_kv=nk, ta=ta, unroll=unroll, ctx_len=ctx_len)
    return pl.pallas_call(
        kern,
        out_shape=jax.ShapeDtypeStruct((batch * seq, DIFF_HEADS * DIFF_DV), BF16),
        grid=(batch, DIFF_HEADS, nq),
        in_specs=[pl.BlockSpec(memory_space=pltpu.SMEM),
                  pl.BlockSpec((tq, LANES), lambda b, h, qi: (b * nq + qi, 48 + h)),
                  pl.BlockSpec((ctx_len, LANES), lambda b, h, qi: (b, ctx_k_blk + h)),
                  pl.BlockSpec((ctx_len, LANES), lambda b, h, qi: (b, ctx_v_blk + h)),
                  pl.BlockSpec((seq, LANES), lambda b, h, qi: (b, 56 + h)),
                  pl.BlockSpec((seq, LANES), lambda b, h, qi: (b, 64 + h)),
                  pl.BlockSpec((1, LANES), lambda b, h, qi: (0, 0)),
                  pl.BlockSpec((LANES, LANES), lambda b, h, qi: (0, 0))],
        out_specs=pl.BlockSpec((tq, LANES), lambda b, h, qi: (b * nq + qi, h)),
        scratch_shapes=[pltpu.VMEM((2 * LANES, 2 * tq), BF16), pltpu.VMEM((1, LANES), F32),
                        pltpu.VMEM((SUBLANES, 2 * tq), F32), pltpu.VMEM((n_all, 2 * LANES), BF16),
                        pltpu.VMEM((DIFF_DV, n_all), BF16), pltpu.VMEM((n_all, 2 * tq), BF16)],
        compiler_params=_params(("arbitrary",) * 3, 58),
        name="dattn",
    )(lam, p_lat, p_ctx, p_ctx, p_lat, p_lat, diff_norm_g.reshape(1, DIFF_DV), gmat)


def _merge_kernel(yr_ref, yd_ref, ga_ref, gb_ref, x_ref, gatea_ref, shf_ref, scf_ref, n2_ref,
                  wr_ref, wd_ref, wo_ref, rwh_ref, rwl_ref, rb_ref,
                  xn_ref, hp_ref, meta_ref, gate_ref, cnt_ref, run_scr, *, tm):
    i = pl.program_id(0)

    @pl.when(i == 0)
    def _():
        run_scr[...] = jnp.zeros(run_scr.shape, F32)

    yr = jnp.dot(yr_ref[...], wr_ref[...], preferred_element_type=F32)
    yd = jnp.dot(yd_ref[...], wd_ref[...], preferred_element_type=F32)
    m = (jax.nn.sigmoid(ga_ref[...].astype(F32)) * yr + jax.nn.sigmoid(gb_ref[...].astype(F32)) * yd)
    z = jnp.dot(m.astype(BF16), wo_ref[...], preferred_element_type=F32)
    xn = x_ref[...] + gatea_ref[0] * z
    xn_ref[...] = xn

    ms = jnp.mean(xn * xn, axis=-1, keepdims=True)
    h2 = xn * lax.rsqrt(ms + EPS) * n2_ref[...]
    h2 = h2 * (1.0 + scf_ref[0]) + shf_ref[0]
    h_hi = h2.astype(BF16)
    bits = pltpu.bitcast(h_hi.astype(F32), U32)
    half = D_MODEL // 2
    hp_ref[...] = (bits[:, :half] >> 16) | (bits[:, half:] & jnp.uint32(0xFFFF0000))

    h_lo = (h2 - h_hi.astype(F32)).astype(BF16)
    logits = (jnp.dot(h_hi, rwh_ref[...], preferred_element_type=F32)
              + jnp.dot(h_lo, rwh_ref[...], preferred_element_type=F32)
              + jnp.dot(h_hi, rwl_ref[...], preferred_element_type=F32) + rb_ref[...])
    lane = lax.broadcasted_iota(I32, (tm, LANES), 1)
    lanef = lane.astype(F32)
    work = logits
    vals, idxs = [], []
    for _ in range(TOP_K):
        mk = jnp.max(work, axis=1, keepdims=True)
        ik = jnp.min(jnp.where(work == mk, lanef, float(LANES)), axis=1, keepdims=True)
        vals.append(mk)
        idxs.append(ik)
        work = jnp.where(lanef == ik, -jnp.inf, work)
    ex = [jnp.exp(v - vals[0]) for v in vals]
    den = ex[0] + ex[1] + ex[2] + ex[3]

    onehot = jnp.zeros((tm, LANES), F32)
    for ik in idxs:
        onehot = onehot + jnp.where(lanef == ik, 1.0, 0.0)
    ri = lax.broadcasted_iota(I32, (tm, tm), 0)
    ci = lax.broadcasted_iota(I32, (tm, tm), 1)
    tri = jnp.where(ri > ci, 1.0, 0.0).astype(BF16)
    base = run_scr[0:1, :] + jnp.dot(tri, onehot.astype(BF16), preferred_element_type=F32)
    run_scr[...] = run_scr[...] + jnp.sum(onehot, axis=0, keepdims=True)

    meta = jnp.zeros((tm, LANES), F32)
    gate_out = jnp.zeros((tm, LANES), F32)
    for k in range(TOP_K):
        rk = jnp.sum(jnp.where(lanef == idxs[k], base, 0.0), axis=1, keepdims=True)
        meta = jnp.where(lane == k, idxs[k], jnp.where(lane == TOP_K + k, rk, meta))
        gate_out = jnp.where(lane == k, ex[k] / den, gate_out)
    meta_ref[...] = meta.T[0:2 * TOP_K, :]
    gate_ref[...] = gate_out
    cnt_ref[...] = run_scr[...]


def _merge(y_ret, y_diff, p_lat, x2, g_a, sh_f, sc_f, norm2_g, w_r, w_d, w_o, rw_hi, rw_lo, rb, seq):
    n = x2.shape[0]
    tm = min(1024, seq)
    tpb = seq // tm
    kern = functools.partial(_merge_kernel, tm=tm)
    mod_spec = pl.BlockSpec((1, 1, D_MODEL), lambda i: (i // tpb, 0, 0))
    const = lambda shape: pl.BlockSpec(shape, lambda i: (0,) * len(shape))
    tok = lambda w: pl.BlockSpec((tm, w), lambda i: (i, 0))
    return pl.pallas_call(
        kern,
        out_shape=(jax.ShapeDtypeStruct((n, D_MODEL), F32),
                   jax.ShapeDtypeStruct((n, D_MODEL // 2), U32),
                   jax.ShapeDtypeStruct((2 * TOP_K, n), F32),
                   jax.ShapeDtypeStruct((n, LANES), F32),
                   jax.ShapeDtypeStruct((8, LANES), F32)),
        grid=(n // tm,),
        in_specs=[tok(RET_HEADS * RET_DV), tok(D_MODEL),
                  pl.BlockSpec((tm, COL_TILE), lambda i: (i, 9)),
                  pl.BlockSpec((tm, COL_TILE), lambda i: (i, 10)),
                  tok(D_MODEL), mod_spec, mod_spec, mod_spec, const((1, D_MODEL)),
                  const((RET_HEADS * RET_DV, D_MODEL)), const((D_MODEL, D_MODEL)), const((D_MODEL, D_MODEL)),
                  const((D_MODEL, LANES)), const((D_MODEL, LANES)), const((1, LANES))],
        out_specs=(tok(D_MODEL), tok(D_MODEL // 2), pl.BlockSpec((2 * TOP_K, tm), lambda i: (0, i)),
                   tok(LANES), const((8, LANES))),
        scratch_shapes=[pltpu.VMEM((8, LANES), F32)],
        compiler_params=_params(("arbitrary",), 60),
        name="merge",
    )(y_ret, y_diff, p_lat, p_lat, x2, g_a, sh_f, sc_f, norm2_g, w_r, w_d, w_o, rw_hi, rw_lo, rb)


def _dispatch_kernel(dest_ref, fill_ref, nb_ref, hp_ref, xb_ref, zero_scr, stage, sems, zsem, *,
                     tm, n_tok, n_blocks, n_steps):
    i = pl.program_id(0)

    @pl.when(i == 0)
    def _():
        zero_scr[...] = jnp.zeros(zero_scr.shape, U32)
        fill = zero_scr.shape[0]
        for e in range(N_EXPERTS):
            start = pl.multiple_of(fill_ref[e], SUBLANES)
            pltpu.make_async_copy(zero_scr, xb_ref.at[pl.ds(start, fill), :], zsem).start()
        for e in range(N_EXPERTS):
            pltpu.make_async_copy(zero_scr, xb_ref.at[pl.ds(0, fill), :], zsem).wait()
        blk = zero_scr.at[pl.ds(0, MOE_BLK), :]

        def start_blk(b, carry):
            pltpu.make_async_copy(blk, xb_ref.at[pl.ds(pl.multiple_of(b * MOE_BLK, MOE_BLK), MOE_BLK), :], zsem).start()
            return carry

        def wait_blk(b, carry):
            pltpu.make_async_copy(blk, xb_ref.at[pl.ds(0, MOE_BLK), :], zsem).wait()
            return carry

        lax.fori_loop(nb_ref[0], n_blocks, start_blk, 0)
        lax.fori_loop(nb_ref[0], n_blocks, wait_blk, 0)

    def drain(slot):
        for _ in range(TOP_K):
            pltpu.make_async_copy(stage.at[slot], xb_ref.at[pl.ds(0, tm), :], sems.at[slot]).wait()

    for slot in range(2):
        @pl.when(lax.rem(i, 2) == slot)
        def _():
            stage[slot] = hp_ref[...]

            def body(g, carry):
                r0 = pl.multiple_of(g * SUBLANES, SUBLANES)
                for j in range(SUBLANES):
                    for k in range(TOP_K):
                        d = dest_ref[k * n_tok + i * tm + r0 + j]
                        pltpu.make_async_copy(stage.at[slot, pl.ds(r0 + j, 1), :], xb_ref.at[pl.ds(d, 1), :],
                                              sems.at[slot]).start(priority=k % 2)
                return carry

            lax.fori_loop(0, tm // SUBLANES, body, 0)

            @pl.when(i > 0)
            def _():
                drain(1 - slot)

            @pl.when(i == n_steps - 1)
            def _():
                drain(slot)


def _dispatch(dest, fill_start, n_used, hp, n_rows):
    n = hp.shape[0]
    tm = min(512, n)
    kern = functools.partial(_dispatch_kernel, tm=tm, n_tok=n, n_blocks=n_rows // MOE_BLK, n_steps=n // tm)
    return pl.pallas_call(
        kern,
        out_shape=jax.ShapeDtypeStruct((n_rows, D_MODEL // 2), U32),
        grid_spec=pltpu.PrefetchScalarGridSpec(
            num_scalar_prefetch=3,
            grid=(n // tm,),
            in_specs=[pl.BlockSpec((tm, D_MODEL // 2), lambda i, d, f, nb: (i, 0))],
            out_specs=pl.BlockSpec(memory_space=pl.ANY),
            scratch_shapes=[pltpu.VMEM((MOE_BLK + SUBLANES, D_MODEL // 2), U32),
                            pltpu.VMEM((2, tm, D_MODEL // 2), U32),
                            pltpu.SemaphoreType.DMA((2,)), pltpu.SemaphoreType.DMA(())]),
        compiler_params=_params(("arbitrary",), 32),
        name="dispatch",
    )(dest, fill_start, n_used, hp)


def _expert_kernel(be_ref, nb_ref, nxt_ref, xb_ref, w1_hbm, b1_ref, w2_hbm, b2_ref, y_ref,
                   w1f_scr, w2f_scr, w1b_scr, w2b_scr, x_scr, grp_scr, sems):
    def weight_copies(expert, slot):
        return (pltpu.make_async_copy(w1_hbm.at[expert], w1f_scr.at[slot], sems.at[0, slot]),
                pltpu.make_async_copy(w2_hbm.at[expert], w2f_scr.at[slot], sems.at[1, slot]))

    def one_block(i, rows):
        e = be_ref[i]
        prev = be_ref[jnp.maximum(i - 1, 0)]
        active = i < nb_ref[0]

        @pl.when(i == 0)
        def _():
            grp_scr[0] = 0
            for cp in weight_copies(e, 0):
                cp.start()

        @pl.when(active & (i > 0) & (e != prev))
        def _():
            grp_scr[0] = grp_scr[0] + 1

        @pl.when(active & ((i == 0) | (e != prev)))
        def _():
            slot = grp_scr[0] % 2
            for cp in weight_copies(e, slot):
                cp.wait()
            w1b_scr[...] = w1f_scr[slot].astype(BF16)
            w2b_scr[...] = w2f_scr[slot].astype(BF16)
            nxt = nxt_ref[e]

            @pl.when(nxt >= 0)
            def _():
                for cp in weight_copies(nxt, 1 - slot):
                    cp.start()

        @pl.when(active)
        def _():
            xu = xb_ref[rows, :]
            half = D_MODEL // 2
            x_scr[:, 0:half] = pltpu.bitcast(xu << 16, F32).astype(BF16)
            x_scr[:, half:] = pltpu.bitcast(xu & jnp.uint32(0xFFFF0000), F32).astype(BF16)
            hh = jnp.dot(x_scr[...], w1b_scr[...], preferred_element_type=F32) + b1_ref[e]
            glu = jnp.minimum(hh[:, :D_FF], SWIGLU_LIMIT)
            lin = jnp.clip(hh[:, D_FF:], -SWIGLU_LIMIT, SWIGLU_LIMIT)
            act = glu * jax.nn.sigmoid(SWIGLU_ALPHA * glu) * (lin + 1.0)
            y_ref[rows, :] = jnp.dot(act.astype(BF16), w2b_scr[...], preferred_element_type=F32) + b2_ref[e]

        @pl.when(i >= nb_ref[0])
        def _():
            y_ref[rows, :] = jnp.zeros((MOE_BLK, D_MODEL), F32)

    for part in range(EXPERT_PAIR):
        one_block(pl.program_id(0) * EXPERT_PAIR + part, pl.ds(part * MOE_BLK, MOE_BLK))


def _experts(block_e, n_used, next_e, xb, w1, b1, w2, b2):
    n_blocks = block_e.shape[0]
    assert n_blocks % EXPERT_PAIR == 0
    rows = n_blocks * MOE_BLK
    step_rows = EXPERT_PAIR * MOE_BLK
    return pl.pallas_call(
        _expert_kernel,
        out_shape=jax.ShapeDtypeStruct((rows, D_MODEL), F32),
        grid_spec=pltpu.PrefetchScalarGridSpec(
            num_scalar_prefetch=3,
            grid=(n_blocks // EXPERT_PAIR,),
            in_specs=[pl.BlockSpec((step_rows, D_MODEL // 2),
                                   lambda i, be, nb, nx: (jnp.minimum(i, (nb[0] - 1) // EXPERT_PAIR), 0)),
                      pl.BlockSpec(memory_space=pl.ANY),
                      pl.BlockSpec((N_EXPERTS, 1, 2 * D_FF), lambda i, be, nb, nx: (0, 0, 0)),
                      pl.BlockSpec(memory_space=pl.ANY),
                      pl.BlockSpec((N_EXPERTS, 1, D_MODEL), lambda i, be, nb, nx: (0, 0, 0))],
            out_specs=pl.BlockSpec((step_rows, D_MODEL), lambda i, be, nb, nx: (i, 0)),
            scratch_shapes=[pltpu.VMEM((2, D_MODEL, 2 * D_FF), F32), pltpu.VMEM((2, D_FF, D_MODEL), F32),
                            pltpu.VMEM((D_MODEL, 2 * D_FF), BF16), pltpu.VMEM((D_FF, D_MODEL), BF16),
                            pltpu.VMEM((MOE_BLK, D_MODEL), BF16),
                            pltpu.SMEM((1,), I32), pltpu.SemaphoreType.DMA((2, 2))]),
        compiler_params=_params(("arbitrary",), 56),
        name="expert",
    )(block_e, n_used, next_e, xb, w1, b1.reshape(N_EXPERTS, 1, 2 * D_FF), w2, b2.reshape(N_EXPERTS, 1, D_MODEL))


def _combine_kernel(dest_ref, yb_ref, gate_ref, xn_ref, gf_ref, o_ref, buf, sems, *, tm, n_tok, n_steps):
    i = pl.program_id(0)

    def issue(tile, slot):
        def body(g, carry):
            r0 = pl.multiple_of(g * SUBLANES, SUBLANES)
            for j in range(SUBLANES):
                for k in range(TOP_K):
                    d = dest_ref[k * n_tok + tile * tm + r0 + j]
                    pltpu.make_async_copy(yb_ref.at[pl.ds(d, 1), :], buf.at[slot, k, pl.ds(r0 + j, 1), :],
                                          sems.at[slot]).start(priority=k % 2)
            return carry

        lax.fori_loop(0, tm // SUBLANES, body, 0)

    def finish(slot):
        for k in range(TOP_K):
            pltpu.make_async_copy(yb_ref.at[pl.ds(0, tm), :], buf.at[slot, k], sems.at[slot]).wait()
        g = gate_ref[...]
        y = g[:, 0:1] * buf[slot, 0]
        for k in range(1, TOP_K):
            y = y + g[:, k:k + 1] * buf[slot, k]
        o_ref[...] = xn_ref[...] + gf_ref[0] * y

    @pl.when(i == 0)
    def _():
        issue(0, 0)

    for slot in range(2):
        @pl.when(lax.rem(i, 2) == slot)
        def _():
            @pl.when(i + 1 < n_steps)
            def _():
                issue(i + 1, 1 - slot)

            finish(slot)


def _combine(dest, yb, gate, xn, g_f, seq):
    n = xn.shape[0]
    tm = min(512, seq)
    tpb = seq // tm
    kern = functools.partial(_combine_kernel, tm=tm, n_tok=n, n_steps=n // tm)
    return pl.pallas_call(
        kern,
        out_shape=jax.ShapeDtypeStruct((n, D_MODEL), F32),
        grid_spec=pltpu.PrefetchScalarGridSpec(
            num_scalar_prefetch=1,
            grid=(n // tm,),
            in_specs=[pl.BlockSpec(memory_space=pl.ANY),
                      pl.BlockSpec((tm, LANES), lambda i, d: (i, 0)),
                      pl.BlockSpec((tm, D_MODEL), lambda i, d: (i, 0)),
                      pl.BlockSpec((1, 1, D_MODEL), lambda i, d: (i // tpb, 0, 0))],
            out_specs=pl.BlockSpec((tm, D_MODEL), lambda i, d: (i, 0)),
            scratch_shapes=[pltpu.VMEM((2, TOP_K, tm, D_MODEL), F32), pltpu.SemaphoreType.DMA((2,))]),
        compiler_params=_params(("arbitrary",), 40),
        name="combine",
    )(dest, yb, gate, xn, g_f)


def _layer(x, ctx, c, c_ctx, norm1_g, norm2_g, w_mod, b_mod, w_in, ret_decay_logit, ret_norm_g,
           diff_q_norm_g, diff_k_norm_g, diff_lambda, diff_norm_g, w_br_ret, w_br_diff, w_out,
           router_w, router_b, exp_w1, exp_b1, exp_w2, exp_b2):
    batch, seq, d = x.shape
    ctx_len = ctx.shape[1]
    assert d == D_MODEL and seq % GRID_W == 0 and batch + 1 <= 8
    n_tok = batch * seq

    cc = jnp.zeros((8, D_MODEL), F32).at[:batch].set(c).at[batch].set(c_ctx)
    mod = _mod(cc, w_mod, b_mod)
    sh_a, sc_a, g_a, sh_f, sc_f, g_f = [mod[:batch, i * D_MODEL:(i + 1) * D_MODEL].reshape(batch, 1, D_MODEL)
                                         for i in range(6)]
    csh_a = mod[batch:batch + 1, 0:D_MODEL].reshape(1, 1, D_MODEL)
    csc_a = mod[batch:batch + 1, D_MODEL:2 * D_MODEL].reshape(1, 1, D_MODEL)

    w_in_bf = w_in.astype(BF16)
    g1 = norm1_g.reshape(1, D_MODEL)
    tile = lambda g: jnp.tile(g.astype(F32), 2)
    qkg = jnp.zeros((8, LANES), F32).at[0].set(tile(diff_q_norm_g) * (DIFF_DH ** -0.5 * LOG2E)).at[1].set(tile(diff_k_norm_g))
    lane = jnp.arange(2 * LANES)
    same_group = lane[:, None] // DIFF_DH == lane[None, :] // DIFF_DH
    gmean = jnp.where(same_group, 1.0 / DIFF_DH, 0.0).astype(BF16)
    gmat = same_group[:LANES, :LANES].astype(BF16)
    x2 = x.reshape(n_tok, D_MODEL)
    tm = min(1024, seq)
    p_lat = _inproj(x2, g1, sh_a, sc_a, w_in_bf, qkg, gmean, _rope_tables(seq), tm, seq // tm,
                    tuple(range(IN_COLS // COL_TILE)))
    p_ctx = _inproj(ctx.reshape(batch * ctx_len, D_MODEL), g1, csh_a, csc_a, w_in_bf, qkg, gmean,
                    _identity_tables(ctx_len), ctx_len, 1, CTX_COLS)

    lg = jax.nn.log_sigmoid(ret_decay_logit.astype(F32))
    y_ret = _retention(lg, p_lat, p_ctx, ret_norm_g, batch, seq, ctx_len,
                       CTX_COLS.index(1) * COL_TILE // RET_DK, CTX_COLS.index(2) * COL_TILE // RET_DV)

    lp = diff_lambda.astype(F32)
    lam = (jnp.exp(jnp.sum(lp[0] * lp[1])) - jnp.exp(jnp.sum(lp[2] * lp[3])) + LAMBDA_INIT).reshape(1)
    y_diff = _diff_attention(lam, p_lat, p_ctx, diff_norm_g, gmat, batch, seq, ctx_len,
                             CTX_COLS.index(7) * COL_TILE // LANES, CTX_COLS.index(8) * COL_TILE // LANES)

    rw = jnp.zeros((D_MODEL, LANES), F32).at[:, :N_EXPERTS].set(router_w)
    rw_hi = rw.astype(BF16)
    rw_lo = (rw - rw_hi.astype(F32)).astype(BF16)
    rb = jnp.full((1, LANES), NEG_BIG, F32).at[0, :N_EXPERTS].set(router_b)
    xn, hp, meta, gate4, cnt = _merge(
        y_ret, y_diff, p_lat, x2, g_a, sh_f, sc_f, norm2_g.reshape(1, D_MODEL),
        w_br_ret.astype(BF16), w_br_diff.astype(BF16), w_out.astype(BF16), rw_hi, rw_lo, rb, seq)

    counts = cnt[0, :N_EXPERTS].astype(I32)
    padded = (counts + MOE_BLK - 1) // MOE_BLK * MOE_BLK
    pad_end = jnp.cumsum(padded)
    pad_start = pad_end - padded
    n_pairs = n_tok * TOP_K
    n_blocks = n_pairs // MOE_BLK + N_EXPERTS
    meta_i = meta.astype(I32)
    is_e = meta_i[None, :TOP_K] == jnp.arange(N_EXPERTS, dtype=I32)[:, None, None]
    dest = (jnp.sum(jnp.where(is_e, pad_start[:, None, None], 0), axis=0) + meta_i[TOP_K:]).reshape(n_pairs)
    block_start = jnp.arange(n_blocks, dtype=I32) * MOE_BLK
    block_e = jnp.minimum(jnp.sum((pad_end[None, :] <= block_start[:, None]).astype(I32), axis=1), N_EXPERTS - 1)
    n_used = (pad_end[-1] // MOE_BLK).reshape(1).astype(I32)

    fill_start = ((pad_start + counts) // SUBLANES * SUBLANES).astype(I32)
    xb = _dispatch(dest, fill_start, n_used, hp, (n_blocks + 1) * MOE_BLK)
    e_ids = jnp.arange(N_EXPERTS, dtype=I32)
    later = jnp.where((counts[None, :] > 0) & (e_ids[None, :] > e_ids[:, None]), e_ids[None, :], N_EXPERTS)
    next_e = jnp.min(later, axis=1)
    next_e = jnp.where(next_e == N_EXPERTS, -1, next_e).astype(I32)
    yb = _experts(block_e, n_used, next_e, xb, exp_w1, exp_b1, exp_w2, exp_b2)
    out = _combine(dest, yb, gate4, xn, g_f, seq)
    return out.reshape(batch, seq, D_MODEL)


def kernel(x, c, ctx, c_ctx, norm1_g, norm2_g, w_mod, b_mod, w_in, ret_decay_logit, ret_norm_g, diff_q_norm_g, diff_k_norm_g, diff_lambda, diff_norm_g, w_br_ret, w_br_diff, w_out, router_w, router_b, exp_w1, exp_b1, exp_w2, exp_b2):
    assert norm1_g.shape[0] == 1, "single-layer block"
    return _layer(x, ctx, c, c_ctx, norm1_g[0], norm2_g[0], w_mod[0], b_mod[0], w_in[0], ret_decay_logit[0],
                  ret_norm_g[0], diff_q_norm_g[0], diff_k_norm_g[0], diff_lambda[0], diff_norm_g[0],
                  w_br_ret[0], w_br_diff[0], w_out[0], router_w[0], router_b[0],
                  exp_w1[0], exp_b1[0], exp_w2[0], exp_b2[0])
```
